```python
import math
import jax
import jax.numpy as jnp
from jax import lax
import numpy as np

D_MODEL = 1024
BATCH = 8
SEQ = 4096
DEPTH = 1

GRID_W = 64
CTX_LEN = 256
MIX_WIDTH = D_MODEL
ATTN_WIDTH = D_MODEL // 2
HEAD_DIM = 64
N_Q_HEADS = ATTN_WIDTH // HEAD_DIM
N_KV_HEADS = 2
GQA_GROUP = N_Q_HEADS // N_KV_HEADS
WINDOW = 128
BLOCK = 128
CONV_WIDTH_CH = MIX_WIDTH - ATTN_WIDTH
CONV_KERNEL = 31
FFN_HIDDEN = 2816
FFN_CONV = 3
ROPE_BASE = 10000.0
ROPE_AXIS_DIM = HEAD_DIM // 2
NORM_EPS = 1e-6
NEG_INF = -1e30

Q_COLS = N_Q_HEADS * HEAD_DIM
KV_COLS = N_KV_HEADS * HEAD_DIM
GLU_COLS = 2 * CONV_WIDTH_CH
IN_COLS = Q_COLS + 2 * KV_COLS + GLU_COLS

kernel_name = "hybrid_window_gqa_conformer_convffn_dit_layer"


def rms_norm(x, w):
    xf = x.astype(jnp.float32)
    y = xf * lax.rsqrt(jnp.mean(xf * xf, axis=-1, keepdims=True) + NORM_EPS)
    return (y * w.astype(jnp.float32)).astype(x.dtype)


def layer_norm(x, w, b):
    xf = x.astype(jnp.float32)
    mu = jnp.mean(xf, axis=-1, keepdims=True)
    var = jnp.mean(jnp.square(xf - mu), axis=-1, keepdims=True)
    y = (xf - mu) * lax.rsqrt(var + NORM_EPS)
    return (y * w.astype(jnp.float32) + b.astype(jnp.float32)).astype(x.dtype)


def modulate(h, shift, scale):
    return h * (1 + scale) + shift


def depthwise_conv(u, w, b):
    width = w.shape[0]
    pad = (width - 1) // 2
    y = lax.conv_general_dilated(
        u, w[:, None, :].astype(u.dtype), window_strides=(1,),
        padding=[(pad, pad)], dimension_numbers=("NWC", "WIO", "NWC"),
        feature_group_count=u.shape[-1])
    return y + b.astype(u.dtype)


def axial_rope_tables(row, col):
    inv = ROPE_BASE ** (-jnp.arange(0, ROPE_AXIS_DIM, 2, dtype=jnp.float32) / ROPE_AXIS_DIM)
    ang_r = row.astype(jnp.float32)[:, None] * inv
    ang_c = col.astype(jnp.float32)[:, None] * inv
    ang = jnp.concatenate([ang_r, ang_r, ang_c, ang_c], axis=-1)
    return jnp.cos(ang), jnp.sin(ang)


def rotate_half(u):
    u1, u2 = jnp.split(u, 2, axis=-1)
    return jnp.concatenate([-u2, u1], axis=-1)


def apply_axial_rope(x, cos, sin):
    s, hd = cos.shape
    shape = (s,) + (1,) * (x.ndim - 3) + (hd,)
    cos = cos.reshape(shape).astype(x.dtype)
    sin = sin.reshape(shape).astype(x.dtype)
    xr, xc = jnp.split(x, 2, axis=-1)
    rotated = jnp.concatenate([rotate_half(xr), rotate_half(xc)], axis=-1)
    return x * cos + rotated * sin


def windowed_gqa_with_context(q, k, v, k_ctx, v_ctx, sink):
    b, s = q.shape[0], q.shape[1]
    nb = s // BLOCK
    n_ctx = k_ctx.shape[1]
    scale = 1.0 / math.sqrt(HEAD_DIM)
    k_pad = jnp.pad(k, ((0, 0), (BLOCK, BLOCK), (0, 0), (0, 0)))
    v_pad = jnp.pad(v, ((0, 0), (BLOCK, BLOCK), (0, 0), (0, 0)))
    q_blocks = jnp.moveaxis(q.reshape(b, nb, BLOCK, N_KV_HEADS, GQA_GROUP, HEAD_DIM), 1, 0)
    offs_q = jnp.arange(BLOCK)
    offs_k = jnp.arange(3 * BLOCK)
    sink_col = jnp.broadcast_to(sink.astype(jnp.float32)[None, :, :, None, None],
                                (b, N_KV_HEADS, GQA_GROUP, BLOCK, 1))

    def one_block(args):
        i, qb = args
        kb = lax.dynamic_slice_in_dim(k_pad, i * BLOCK, 3 * BLOCK, axis=1)
        vb = lax.dynamic_slice_in_dim(v_pad, i * BLOCK, 3 * BLOCK, axis=1)
        qpos = i * BLOCK + offs_q
        kpos = (i - 1) * BLOCK + offs_k
        valid = ((jnp.abs(qpos[:, None] - kpos[None, :]) <= WINDOW)
                 & (kpos[None, :] >= 0) & (kpos[None, :] < s))
        s_loc = jnp.einsum("bqhgd,bkhd->bhgqk", qb, kb).astype(jnp.float32) * scale
        s_loc = jnp.where(valid, s_loc, NEG_INF)
        s_ctx = jnp.einsum("bqhgd,bkhd->bhgqk", qb, k_ctx).astype(jnp.float32) * scale
        p = jax.nn.softmax(jnp.concatenate([sink_col, s_ctx, s_loc], axis=-1), axis=-1)
        p = p.astype(v.dtype)
        o = (jnp.einsum("bhgqk,bkhd->bqhgd", p[..., 1:1 + n_ctx], v_ctx)
             + jnp.einsum("bhgqk,bkhd->bqhgd", p[..., 1 + n_ctx:], vb))
        return o

    out = lax.map(one_block, (jnp.arange(nb), q_blocks))
    return jnp.moveaxis(out, 0, 1).reshape(b, s, N_KV_HEADS * GQA_GROUP * HEAD_DIM)


def _fwd_setup_inputs(seed: int = 0) -> dict:
    key = jax.random.key(seed)
    ks = jax.random.split(key, 24)
    f32 = jnp.float32

    def nrm(k, shape, scale):
        return jax.random.normal(k, shape, f32) * scale

    L = DEPTH
    return {
        "x": nrm(ks[0], (BATCH, SEQ, D_MODEL), 1.0),
        "c": nrm(ks[1], (BATCH, D_MODEL), 1.0),
        "ctx": nrm(ks[2], (BATCH, CTX_LEN, D_MODEL), 1.0),
        "c_ctx": nrm(ks[3], (D_MODEL,), 1.0),
        "w_mod": nrm(ks[4], (L, D_MODEL, 6 * D_MODEL), 0.5 * D_MODEL ** -0.5),
        "b_mod": nrm(ks[5], (L, 6 * D_MODEL), 0.02),
        "norm_mix_w": 1.0 + nrm(ks[6], (L, D_MODEL), 0.02),
        "w_in": nrm(ks[7], (L, D_MODEL, IN_COLS), D_MODEL ** -0.5),
        "q_norm_w": 1.0 + nrm(ks[8], (L, HEAD_DIM), 0.02),
        "k_norm_w": 1.0 + nrm(ks[9], (L, HEAD_DIM), 0.02),
        "sink_logit": nrm(ks[10], (L, N_Q_HEADS), 0.5),
        "conv_w": nrm(ks[11], (L, CONV_KERNEL, CONV_WIDTH_CH), CONV_KERNEL ** -0.5),
        "conv_b": nrm(ks[12], (L, CONV_WIDTH_CH), 0.02),
        "conv_norm_w": 1.0 + nrm(ks[13], (L, CONV_WIDTH_CH), 0.02),
        "conv_norm_b": nrm(ks[14], (L, CONV_WIDTH_CH), 0.02),
        "w_out": nrm(ks[15], (L, MIX_WIDTH, D_MODEL), MIX_WIDTH ** -0.5),
        "norm_ffn_w": 1.0 + nrm(ks[16], (L, D_MODEL), 0.02),
        "w_up": nrm(ks[17], (L, D_MODEL, 2 * FFN_HIDDEN), D_MODEL ** -0.5),
        "ffn_conv_w": nrm(ks[18], (L, FFN_CONV, 2 * FFN_HIDDEN), FFN_CONV ** -0.5),
        "ffn_conv_b": nrm(ks[19], (L, 2 * FFN_HIDDEN), 0.02),
        "w_down": nrm(ks[20], (L, FFN_HIDDEN, D_MODEL), FFN_HIDDEN ** -0.5),
    }


def _fwd_reference(x, c, ctx, c_ctx, w_mod, b_mod, norm_mix_w, w_in, q_norm_w, k_norm_w,
              sink_logit, conv_w, conv_b, conv_norm_w, conv_norm_b, w_out,
              norm_ffn_w, w_up, ffn_conv_w, ffn_conv_b, w_down):
    b, s, d = x.shape
    rows = s // GRID_W
    row = jnp.broadcast_to(jnp.arange(rows)[:, None], (rows, GRID_W)).reshape(s)
    col = jnp.broadcast_to(jnp.arange(GRID_W)[None, :], (rows, GRID_W)).reshape(s)
    cos, sin = axial_rope_tables(row, col)

    for l in range(DEPTH):
        mod = jax.nn.silu(c) @ w_mod[l] + b_mod[l]
        sh1, sc1, g1, sh2, sc2, g2 = [m[:, None, :] for m in jnp.split(mod, 6, axis=-1)]
        mod_ctx = jax.nn.silu(c_ctx) @ w_mod[l] + b_mod[l]
        sh1c, sc1c = mod_ctx[:d], mod_ctx[d:2 * d]

        h = modulate(rms_norm(x, norm_mix_w[l]), sh1, sc1)
        hc = modulate(rms_norm(ctx, norm_mix_w[l]), sh1c, sc1c)
        proj = h @ w_in[l]
        q = proj[..., :Q_COLS].reshape(b, s, N_KV_HEADS, GQA_GROUP, HEAD_DIM)
        k = proj[..., Q_COLS:Q_COLS + KV_COLS].reshape(b, s, N_KV_HEADS, HEAD_DIM)
        v = proj[..., Q_COLS + KV_COLS:Q_COLS + 2 * KV_COLS].reshape(b, s, N_KV_HEADS, HEAD_DIM)
        glu_in = proj[..., Q_COLS + 2 * KV_COLS:]

        kv_ctx = hc @ w_in[l][:, Q_COLS:Q_COLS + 2 * KV_COLS]
        n_ctx = ctx.shape[1]
        k_ctx = rms_norm(kv_ctx[..., :KV_COLS].reshape(b, n_ctx, N_KV_HEADS, HEAD_DIM), k_norm_w[l])
        v_ctx = kv_ctx[..., KV_COLS:].reshape(b, n_ctx, N_KV_HEADS, HEAD_DIM)

        q = apply_axial_rope(rms_norm(q, q_norm_w[l]), cos, sin)
        k = apply_axial_rope(rms_norm(k, k_norm_w[l]), cos, sin)
        sink = sink_logit[l].reshape(N_KV_HEADS, GQA_GROUP)
        attn_out = windowed_gqa_with_context(q, k, v, k_ctx, v_ctx, sink)

        ga, gb = jnp.split(glu_in, 2, axis=-1)
        u = ga * jax.nn.sigmoid(gb)
        u = depthwise_conv(u, conv_w[l], conv_b[l])
        u = jax.nn.silu(layer_norm(u, conv_norm_w[l], conv_norm_b[l]))

        mix = jnp.concatenate([attn_out, u], axis=-1) @ w_out[l]
        x = x + g1 * mix

        h2 = modulate(rms_norm(x, norm_ffn_w[l]), sh2, sc2)
        up = depthwise_conv(h2 @ w_up[l], ffn_conv_w[l], ffn_conv_b[l])
        gate, val = jnp.split(up, 2, axis=-1)
        x = x + g2 * ((jax.nn.silu(gate) * val) @ w_down[l])
    return x


import jax as _jax
import jax.numpy as _jnp

TWIN_FORMAT = 'train_step'
FWD_PARAMS = ['x', 'c', 'ctx', 'c_ctx', 'w_mod', 'b_mod', 'norm_mix_w', 'w_in', 'q_norm_w', 'k_norm_w', 'sink_logit', 'conv_w', 'conv_b', 'conv_norm_w', 'conv_norm_b', 'w_out', 'norm_ffn_w', 'w_up', 'ffn_conv_w', 'ffn_conv_b', 'w_down']
TWIN_WEIGHTS = ['c_ctx', 'w_mod', 'b_mod', 'norm_mix_w', 'w_in', 'q_norm_w', 'k_norm_w', 'sink_logit', 'conv_w', 'conv_b', 'conv_norm_w', 'conv_norm_b', 'w_out', 'norm_ffn_w', 'w_up', 'ffn_conv_w', 'ffn_conv_b', 'w_down']
TWIN_DIFF_INPUT = 'x'
TWIN_INPUTS = ['x', 'c', 'ctx', 'c_ctx', 'w_mod', 'b_mod', 'norm_mix_w', 'w_in', 'q_norm_w', 'k_norm_w', 'sink_logit', 'conv_w', 'conv_b', 'conv_norm_w', 'conv_norm_b', 'w_out', 'norm_ffn_w', 'w_up', 'ffn_conv_w', 'ffn_conv_b', 'w_down', 'loss_target', 'm_c_ctx', 'm_w_mod', 'm_b_mod', 'm_norm_mix_w', 'm_w_in', 'm_q_norm_w', 'm_k_norm_w', 'm_sink_logit', 'm_conv_w', 'm_conv_b', 'm_conv_norm_w', 'm_conv_norm_b', 'm_w_out', 'm_norm_ffn_w', 'm_w_up', 'm_ffn_conv_w', 'm_ffn_conv_b', 'm_w_down', 'v_c_ctx', 'v_w_mod', 'v_b_mod', 'v_norm_mix_w', 'v_w_in', 'v_q_norm_w', 'v_k_norm_w', 'v_sink_logit', 'v_conv_w', 'v_conv_b', 'v_conv_norm_w', 'v_conv_norm_b', 'v_w_out', 'v_norm_ffn_w', 'v_w_up', 'v_ffn_conv_w', 'v_ffn_conv_b', 'v_w_down']
TWIN_OUTPUTS = ['loss', 'grad_x', 'grad_c_ctx', 'grad_w_mod', 'grad_b_mod', 'grad_norm_mix_w', 'grad_w_in', 'grad_q_norm_w', 'grad_k_norm_w', 'grad_sink_logit', 'grad_conv_w', 'grad_conv_b', 'grad_conv_norm_w', 'grad_conv_norm_b', 'grad_w_out', 'grad_norm_ffn_w', 'grad_w_up', 'grad_ffn_conv_w', 'grad_ffn_conv_b', 'grad_w_down', 'delta_c_ctx', 'delta_w_mod', 'delta_b_mod', 'delta_norm_mix_w', 'delta_w_in', 'delta_q_norm_w', 'delta_k_norm_w', 'delta_sink_logit', 'delta_conv_w', 'delta_conv_b', 'delta_conv_norm_w', 'delta_conv_norm_b', 'delta_w_out', 'delta_norm_ffn_w', 'delta_w_up', 'delta_ffn_conv_w', 'delta_ffn_conv_b', 'delta_w_down', 'new_m_c_ctx', 'new_m_w_mod', 'new_m_b_mod', 'new_m_norm_mix_w', 'new_m_w_in', 'new_m_q_norm_w', 'new_m_k_norm_w', 'new_m_sink_logit', 'new_m_conv_w', 'new_m_conv_b', 'new_m_conv_norm_w', 'new_m_conv_norm_b', 'new_m_w_out', 'new_m_norm_ffn_w', 'new_m_w_up', 'new_m_ffn_conv_w', 'new_m_ffn_conv_b', 'new_m_w_down', 'new_v_c_ctx', 'new_v_w_mod', 'new_v_b_mod', 'new_v_norm_mix_w', 'new_v_w_in', 'new_v_q_norm_w', 'new_v_k_norm_w', 'new_v_sink_logit', 'new_v_conv_w', 'new_v_conv_b', 'new_v_conv_norm_w', 'new_v_conv_norm_b', 'new_v_w_out', 'new_v_norm_ffn_w', 'new_v_w_up', 'new_v_ffn_conv_w', 'new_v_ffn_conv_b', 'new_v_w_down']
TWIN_LEAF_KINDS = {'loss': 'loss', 'grad_x': 'grad_x', 'grad_c_ctx': 'grad_w', 'grad_w_mod': 'grad_w', 'grad_b_mod': 'grad_w', 'grad_norm_mix_w': 'grad_w', 'grad_w_in': 'grad_w', 'grad_q_norm_w': 'grad_w', 'grad_k_norm_w': 'grad_w', 'grad_sink_logit': 'grad_w', 'grad_conv_w': 'grad_w', 'grad_conv_b': 'grad_w', 'grad_conv_norm_w': 'grad_w', 'grad_conv_norm_b': 'grad_w', 'grad_w_out': 'grad_w', 'grad_norm_ffn_w': 'grad_w', 'grad_w_up': 'grad_w', 'grad_ffn_conv_w': 'grad_w', 'grad_ffn_conv_b': 'grad_w', 'grad_w_down': 'grad_w', 'delta_c_ctx': 'delta_w', 'delta_w_mod': 'delta_w', 'delta_b_mod': 'delta_w', 'delta_norm_mix_w': 'delta_w', 'delta_w_in': 'delta_w', 'delta_q_norm_w': 'delta_w', 'delta_k_norm_w': 'delta_w', 'delta_sink_logit': 'delta_w', 'delta_conv_w': 'delta_w', 'delta_conv_b': 'delta_w', 'delta_conv_norm_w': 'delta_w', 'delta_conv_norm_b': 'delta_w', 'delta_w_out': 'delta_w', 'delta_norm_ffn_w': 'delta_w', 'delta_w_up': 'delta_w', 'delta_ffn_conv_w': 'delta_w', 'delta_ffn_conv_b': 'delta_w', 'delta_w_down': 'delta_w', 'new_m_c_ctx': 'new_m', 'new_m_w_mod': 'new_m', 'new_m_b_mod': 'new_m', 'new_m_norm_mix_w': 'new_m', 'new_m_w_in': 'new_m', 'new_m_q_norm_w': 'new_m', 'new_m_k_norm_w': 'new_m', 'new_m_sink_logit': 'new_m', 'new_m_conv_w': 'new_m', 'new_m_conv_b': 'new_m', 'new_m_conv_norm_w': 'new_m', 'new_m_conv_norm_b': 'new_m', 'new_m_w_out': 'new_m', 'new_m_norm_ffn_w': 'new_m', 'new_m_w_up': 'new_m', 'new_m_ffn_conv_w': 'new_m', 'new_m_ffn_conv_b': 'new_m', 'new_m_w_down': 'new_m', 'new_v_c_ctx': 'new_v', 'new_v_w_mod': 'new_v', 'new_v_b_mod': 'new_v', 'new_v_norm_mix_w': 'new_v', 'new_v_w_in': 'new_v', 'new_v_q_norm_w': 'new_v', 'new_v_k_norm_w': 'new_v', 'new_v_sink_logit': 'new_v', 'new_v_conv_w': 'new_v', 'new_v_conv_b': 'new_v', 'new_v_conv_norm_w': 'new_v', 'new_v_conv_norm_b': 'new_v', 'new_v_w_out': 'new_v', 'new_v_norm_ffn_w': 'new_v', 'new_v_w_up': 'new_v', 'new_v_ffn_conv_w': 'new_v', 'new_v_ffn_conv_b': 'new_v', 'new_v_w_down': 'new_v'}


def _forward(args):
    return _fwd_reference(*[args[k] for k in FWD_PARAMS])


def _output_shape():
    out = _jax.eval_shape(lambda: _forward(_fwd_setup_inputs(0)))
    return out.shape, out.dtype

N_MICROBATCH = 1
ADAM_LR = 0.001
ADAM_B1 = 0.9
ADAM_B2 = 0.999
ADAM_EPS = 1e-08
ADAM_WD = 0.01
ADAM_STEP = 10
PER_EXAMPLE_BATCH_AXIS = {'x': 0, 'c': 0, 'ctx': 0, 'loss_target': 0}
SHARED_INPUTS = []
_WEIGHT_DTYPES = {'c_ctx': _jnp.float32, 'w_mod': _jnp.float32, 'b_mod': _jnp.float32, 'norm_mix_w': _jnp.float32, 'w_in': _jnp.float32, 'q_norm_w': _jnp.float32, 'k_norm_w': _jnp.float32, 'sink_logit': _jnp.float32, 'conv_w': _jnp.float32, 'conv_b': _jnp.float32, 'conv_norm_w': _jnp.float32, 'conv_norm_b': _jnp.float32, 'w_out': _jnp.float32, 'norm_ffn_w': _jnp.float32, 'w_up': _jnp.float32, 'ffn_conv_w': _jnp.float32, 'ffn_conv_b': _jnp.float32, 'w_down': _jnp.float32}
MOMENT_SCALE = {'c_ctx': 1.440197e-01, 'w_mod': 8.727709e-01, 'b_mod': 1.931587e+00, 'norm_mix_w': 4.555458e-02, 'w_in': 1.237728e-01, 'q_norm_w': 1.162980e-01, 'k_norm_w': 1.144289e-01, 'sink_logit': 1.635729e-02, 'conv_w': 1.032524e-01, 'conv_b': 6.952699e-01, 'conv_norm_w': 1.405866e+00, 'conv_norm_b': 9.369959e-01, 'w_out': 1.520979e-01, 'norm_ffn_w': 3.431085e+00, 'w_up': 9.799252e-02, 'ffn_conv_w': 5.029533e-01, 'ffn_conv_b': 4.111150e-01, 'w_down': 7.811159e-02}


def _to_microbatches(a, axis):
    t = _jnp.moveaxis(a, axis, 0)
    t = t.reshape((N_MICROBATCH, t.shape[0] // N_MICROBATCH) + t.shape[1:])
    return _jnp.moveaxis(t, 1, axis + 1)


def setup_inputs(seed: int = 0) -> dict:
    inp = _fwd_setup_inputs(seed)
    key = _jax.random.fold_in(_jax.random.key(seed), 7919)
    shape, _ = _output_shape()
    out = dict(inp)
    out["loss_target"] = _jax.random.normal(_jax.random.fold_in(key, 0), shape, _jnp.float32)
    for i, name in enumerate(TWIN_WEIGHTS):
        w = inp[name].astype(_jnp.float32)
        if MOMENT_SCALE is None:
            s = _jnp.sqrt(_jnp.mean(_jnp.square(w)) + 1e-30)
        else:
            s = MOMENT_SCALE[name]
        km, kv = _jax.random.split(_jax.random.fold_in(key, i + 1))
        out[name] = w
        out["m_" + name] = s * _jax.random.normal(km, w.shape, _jnp.float32)
        out["v_" + name] = (s * s) * _jax.random.uniform(kv, w.shape, _jnp.float32, 0.5, 1.5)
    if N_MICROBATCH > 1:
        for name, axis in PER_EXAMPLE_BATCH_AXIS.items():
            out[name] = _to_microbatches(out[name], axis)
    return {'x': out['x'], 'c': out['c'], 'ctx': out['ctx'], 'c_ctx': out['c_ctx'], 'w_mod': out['w_mod'], 'b_mod': out['b_mod'], 'norm_mix_w': out['norm_mix_w'], 'w_in': out['w_in'], 'q_norm_w': out['q_norm_w'], 'k_norm_w': out['k_norm_w'], 'sink_logit': out['sink_logit'], 'conv_w': out['conv_w'], 'conv_b': out['conv_b'], 'conv_norm_w': out['conv_norm_w'], 'conv_norm_b': out['conv_norm_b'], 'w_out': out['w_out'], 'norm_ffn_w': out['norm_ffn_w'], 'w_up': out['w_up'], 'ffn_conv_w': out['ffn_conv_w'], 'ffn_conv_b': out['ffn_conv_b'], 'w_down': out['w_down'], 'loss_target': out['loss_target'], 'm_c_ctx': out['m_c_ctx'], 'm_w_mod': out['m_w_mod'], 'm_b_mod': out['m_b_mod'], 'm_norm_mix_w': out['m_norm_mix_w'], 'm_w_in': out['m_w_in'], 'm_q_norm_w': out['m_q_norm_w'], 'm_k_norm_w': out['m_k_norm_w'], 'm_sink_logit': out['m_sink_logit'], 'm_conv_w': out['m_conv_w'], 'm_conv_b': out['m_conv_b'], 'm_conv_norm_w': out['m_conv_norm_w'], 'm_conv_norm_b': out['m_conv_norm_b'], 'm_w_out': out['m_w_out'], 'm_norm_ffn_w': out['m_norm_ffn_w'], 'm_w_up': out['m_w_up'], 'm_ffn_conv_w': out['m_ffn_conv_w'], 'm_ffn_conv_b': out['m_ffn_conv_b'], 'm_w_down': out['m_w_down'], 'v_c_ctx': out['v_c_ctx'], 'v_w_mod': out['v_w_mod'], 'v_b_mod': out['v_b_mod'], 'v_norm_mix_w': out['v_norm_mix_w'], 'v_w_in': out['v_w_in'], 'v_q_norm_w': out['v_q_norm_w'], 'v_k_norm_w': out['v_k_norm_w'], 'v_sink_logit': out['v_sink_logit'], 'v_conv_w': out['v_conv_w'], 'v_conv_b': out['v_conv_b'], 'v_conv_norm_w': out['v_conv_norm_w'], 'v_conv_norm_b': out['v_conv_norm_b'], 'v_w_out': out['v_w_out'], 'v_norm_ffn_w': out['v_norm_ffn_w'], 'v_w_up': out['v_w_up'], 'v_ffn_conv_w': out['v_ffn_conv_w'], 'v_ffn_conv_b': out['v_ffn_conv_b'], 'v_w_down': out['v_w_down']}


def _loss(weights, diff, rest, loss_target):
    with _jax.named_scope("forward"):
        args = {**rest, TWIN_DIFF_INPUT: diff, **{k: w.astype(_WEIGHT_DTYPES[k]) for k, w in weights.items()}}
        y = _forward(args)
    with _jax.named_scope("loss_head"):
        err = _jnp.square(y.astype(_jnp.float32) - loss_target)
        return 0.5 * _jnp.sum(_jnp.mean(err, axis=-1)) if err.ndim else 0.5 * err


def _adamw(w, g, m, v):
    m = ADAM_B1 * m + (1.0 - ADAM_B1) * g
    v = ADAM_B2 * v + (1.0 - ADAM_B2) * _jnp.square(g)
    m_hat = m / (1.0 - ADAM_B1 ** ADAM_STEP)
    v_hat = v / (1.0 - ADAM_B2 ** ADAM_STEP)
    delta = -ADAM_LR * (m_hat / (_jnp.sqrt(v_hat) + ADAM_EPS) + ADAM_WD * w)
    return delta, m, v


def reference(x, c, ctx, c_ctx, w_mod, b_mod, norm_mix_w, w_in, q_norm_w, k_norm_w, sink_logit, conv_w, conv_b, conv_norm_w, conv_norm_b, w_out, norm_ffn_w, w_up, ffn_conv_w, ffn_conv_b, w_down, loss_target, m_c_ctx, m_w_mod, m_b_mod, m_norm_mix_w, m_w_in, m_q_norm_w, m_k_norm_w, m_sink_logit, m_conv_w, m_conv_b, m_conv_norm_w, m_conv_norm_b, m_w_out, m_norm_ffn_w, m_w_up, m_ffn_conv_w, m_ffn_conv_b, m_w_down, v_c_ctx, v_w_mod, v_b_mod, v_norm_mix_w, v_w_in, v_q_norm_w, v_k_norm_w, v_sink_logit, v_conv_w, v_conv_b, v_conv_norm_w, v_conv_norm_b, v_w_out, v_norm_ffn_w, v_w_up, v_ffn_conv_w, v_ffn_conv_b, v_w_down):
    given = dict(x=x, c=c, ctx=ctx, c_ctx=c_ctx, w_mod=w_mod, b_mod=b_mod, norm_mix_w=norm_mix_w, w_in=w_in, q_norm_w=q_norm_w, k_norm_w=k_norm_w, sink_logit=sink_logit, conv_w=conv_w, conv_b=conv_b, conv_norm_w=conv_norm_w, conv_norm_b=conv_norm_b, w_out=w_out, norm_ffn_w=norm_ffn_w, w_up=w_up, ffn_conv_w=ffn_conv_w, ffn_conv_b=ffn_conv_b, w_down=w_down, loss_target=loss_target, m_c_ctx=m_c_ctx, m_w_mod=m_w_mod, m_b_mod=m_b_mod, m_norm_mix_w=m_norm_mix_w, m_w_in=m_w_in, m_q_norm_w=m_q_norm_w, m_k_norm_w=m_k_norm_w, m_sink_logit=m_sink_logit, m_conv_w=m_conv_w, m_conv_b=m_conv_b, m_conv_norm_w=m_conv_norm_w, m_conv_norm_b=m_conv_norm_b, m_w_out=m_w_out, m_norm_ffn_w=m_norm_ffn_w, m_w_up=m_w_up, m_ffn_conv_w=m_ffn_conv_w, m_ffn_conv_b=m_ffn_conv_b, m_w_down=m_w_down, v_c_ctx=v_c_ctx, v_w_mod=v_w_mod, v_b_mod=v_b_mod, v_norm_mix_w=v_norm_mix_w, v_w_in=v_w_in, v_q_norm_w=v_q_norm_w, v_k_norm_w=v_k_norm_w, v_sink_logit=v_sink_logit, v_conv_w=v_conv_w, v_conv_b=v_conv_b, v_conv_norm_w=v_conv_norm_w, v_conv_norm_b=v_conv_norm_b, v_w_out=v_w_out, v_norm_ffn_w=v_norm_ffn_w, v_w_up=v_w_up, v_ffn_conv_w=v_ffn_conv_w, v_ffn_conv_b=v_ffn_conv_b, v_w_down=v_w_down)
    weights = {n: given[n] for n in TWIN_WEIGHTS}
    shared = {n: given[n] for n in SHARED_INPUTS}
    per_example = {n: given[n] for n in ['x', 'c', 'ctx']}
    grad_fn = _jax.value_and_grad(_loss, argnums=(0, 1))

    def one_microbatch(ex, loss_target):
        ex = dict(ex)
        diff = ex.pop(TWIN_DIFF_INPUT)
        return grad_fn(weights, diff, {**shared, **ex}, loss_target)

    if N_MICROBATCH == 1:
        loss, (grad_w, grad_x) = one_microbatch(per_example, given["loss_target"])
    else:
        def body(carry, xs):
            loss_sum, grad_sum = carry
            l_k, (gw_k, gx_k) = one_microbatch(xs[0], xs[1])
            with _jax.named_scope("update"):
                return (loss_sum + l_k, _jax.tree.map(_jnp.add, grad_sum, gw_k)), gx_k

        init = (_jnp.zeros((), _jnp.float32), _jax.tree.map(_jnp.zeros_like, weights))
        (loss, grad_w), grad_x = _jax.lax.scan(body, init, (per_example, given["loss_target"]))
    with _jax.named_scope("update"):
        delta_w, new_m, new_v = {}, {}, {}
        for n in TWIN_WEIGHTS:
            delta_w[n], new_m[n], new_v[n] = _adamw(weights[n], grad_w[n], given["m_" + n], given["v_" + n])
    return (loss, grad_x, *[grad_w[n] for n in TWIN_WEIGHTS], *[delta_w[n] for n in TWIN_WEIGHTS],
            *[new_m[n] for n in TWIN_WEIGHTS], *[new_v[n] for n in TWIN_WEIGHTS])
```

```python
import functools
import math

import jax
import jax.numpy as jnp
from jax import lax
from jax.experimental import pallas as pl
from jax.experimental.pallas import tpu as pltpu

F32 = jnp.float32
MXU_DT = jnp.bfloat16

D_MODEL = 1024
GRID_W = 64
HEAD_DIM = 64
N_Q_HEADS = 8
N_KV_HEADS = 2
GQA_GROUP = 4
WINDOW = 128
BLOCK = 128
Q_COLS = 512
KV_COLS = 128
QK_COLS = Q_COLS + KV_COLS
QKV_COLS = Q_COLS + 2 * KV_COLS
CONV_CH = 512
GLU_COLS = 2 * CONV_CH
IN_COLS = QKV_COLS + GLU_COLS
CONV_K = 31
CONV_PAD = 15
FFN_H = 2816
FFN_K = 3
ROPE_BASE = 10000.0
EPS = 1e-6
NEG_INF = -1e30
N_DEV = 8
HALO = 16
LANES = 128

ADAM_LR = 0.001
ADAM_B1 = 0.9
ADAM_B2 = 0.999
ADAM_EPS = 1e-08
ADAM_WD = 0.01
ADAM_STEP = 10

MESH_AXES = ("x", "y", "c")
MESH = pl.DeviceIdType.MESH
VMEM_LIMIT = 56 << 20


def _cp(*dims):
    return pltpu.CompilerParams(dimension_semantics=dims or None, vmem_limit_bytes=VMEM_LIMIT)


def _pick(n, prefs, also=0):
    for p in prefs:
        if n % p == 0 and also % p == 0:
            return p
    raise ValueError((n, prefs, also))


def _row(ts, w, col=0):
    return pl.BlockSpec((ts, w), lambda i: (i, col))


def _vec(w):
    return pl.BlockSpec((1, w), lambda i: (0, 0))


def _colsum(v):
    return jnp.sum(v, axis=0, keepdims=True)


def _sigmoid(v):
    return 1.0 / (1.0 + jnp.exp(-v))


def _mm(a, b, m, n, k, *, ta=False, tb=False, n0=0, k0=0, add=None, out_dtype=F32, name):
    tm = m if m <= 1024 else _pick(m, (1024, 512, 256, 128))
    tn = _pick(n, (512, 256, 128), n0)
    tk = k if k < 128 else _pick(k, (512, 256, 128), k0)
    nk = k // tk
    a_spec = (pl.BlockSpec((tk, tm), lambda i, j, kk: (kk, i)) if ta
              else pl.BlockSpec((tm, tk), lambda i, j, kk: (i, kk)))
    nb0, kb0 = n0 // tn, k0 // tk
    b_spec = (pl.BlockSpec((tn, tk), lambda i, j, kk: (j + nb0, kk + kb0)) if tb
              else pl.BlockSpec((tk, tn), lambda i, j, kk: (kk + kb0, j + nb0)))
    o_spec = pl.BlockSpec((tm, tn), lambda i, j, kk: (i, j))
    dims = (((0 if ta else 1,), (1 if tb else 0,)), ((), ()))
    has_add = add is not None

    def body(*refs):
        a_ref, b_ref = refs[0], refs[1]
        add_ref = refs[2] if has_add else None
        o_ref, acc_ref = refs[-2], refs[-1]
        kk = pl.program_id(2)
        part = lax.dot_general(a_ref[...].astype(MXU_DT), b_ref[...].astype(MXU_DT), dims,
                               preferred_element_type=F32)

        @pl.when(kk == 0)
        def _():
            acc_ref[...] = part + add_ref[...].astype(F32) if has_add else part

        @pl.when(kk > 0)
        def _():
            acc_ref[...] += part

        @pl.when(kk == nk - 1)
        def _():
            o_ref[...] = acc_ref[...].astype(o_ref.dtype)

    ins = [a, b] + ([add] if has_add else [])
    specs = [a_spec, b_spec] + ([o_spec] if has_add else [])
    return pl.pallas_call(
        body, name=name, grid=(m // tm, n // tn, nk),
        in_specs=specs, out_specs=o_spec,
        out_shape=jax.ShapeDtypeStruct((m, n), out_dtype),
        scratch_shapes=[pltpu.VMEM((tm, tn), F32)],
        compiler_params=_cp("parallel", "parallel", "arbitrary"),
    )(*ins)


def _rms_stats(xv):
    r = lax.rsqrt(jnp.mean(xv * xv, axis=-1, keepdims=True) + EPS)
    return r, xv * r


def _prenorm(x, nw, sc, sh, *, name):
    rows, d = x.shape
    ts = min(rows, 256)

    def body(x_ref, nw_ref, sc_ref, sh_ref, h_ref):
        _, xn = _rms_stats(x_ref[...])
        h_ref[...] = ((xn * nw_ref[...]) * (1.0 + sc_ref[...]) + sh_ref[...]).astype(h_ref.dtype)

    return pl.pallas_call(
        body, name=name, grid=(rows // ts,),
        in_specs=[_row(ts, d), _vec(d), _vec(d), _vec(d)], out_specs=_row(ts, d),
        out_shape=jax.ShapeDtypeStruct((rows, d), MXU_DT), compiler_params=_cp("parallel"),
    )(x, nw, sc, sh)


def _resid_prenorm(x, mix, g1, nw, sc, sh, *, name):
    rows, d = x.shape
    ts = min(rows, 256)

    def body(x_ref, mix_ref, g_ref, nw_ref, sc_ref, sh_ref, x1_ref, h_ref):
        x1 = x_ref[...] + g_ref[...] * mix_ref[...]
        x1_ref[...] = x1
        _, xn = _rms_stats(x1)
        h_ref[...] = ((xn * nw_ref[...]) * (1.0 + sc_ref[...]) + sh_ref[...]).astype(h_ref.dtype)

    return pl.pallas_call(
        body, name=name, grid=(rows // ts,),
        in_specs=[_row(ts, d), _row(ts, d), _vec(d), _vec(d), _vec(d), _vec(d)],
        out_specs=[_row(ts, d), _row(ts, d)],
        out_shape=[jax.ShapeDtypeStruct((rows, d), F32), jax.ShapeDtypeStruct((rows, d), MXU_DT)],
        compiler_params=_cp("parallel"),
    )(x, mix, g1, nw, sc, sh)


def _loss_head(x1, ffn, g2, target, *, name):
    rows, d = x1.shape
    ts = min(rows, 256)

    def body(x1_ref, f_ref, g_ref, t_ref, loss_ref, dy_ref, dffn_ref, dg_ref):
        i = pl.program_id(0)
        f = f_ref[...]
        e = x1_ref[...] + g_ref[...] * f - t_ref[...]
        per_tok = jnp.mean(e * e, axis=-1, keepdims=True)
        part = 0.5 * jnp.sum(per_tok, axis=0, keepdims=True)
        dy = e * (1.0 / d)
        dy_ref[...] = dy
        dffn_ref[...] = (dy * g_ref[...]).astype(dffn_ref.dtype)

        @pl.when(i == 0)
        def _():
            loss_ref[...] = jnp.zeros_like(loss_ref)
            dg_ref[...] = jnp.zeros_like(dg_ref)

        loss_ref[...] += jnp.broadcast_to(part, loss_ref.shape)
        dg_ref[...] += _colsum(dy * f)

    return pl.pallas_call(
        body, name=name, grid=(rows // ts,),
        in_specs=[_row(ts, d), _row(ts, d), _vec(d), _row(ts, d)],
        out_specs=[pl.BlockSpec((8, LANES), lambda i: (0, 0)), _row(ts, d), _row(ts, d), _vec(d)],
        out_shape=[jax.ShapeDtypeStruct((8, LANES), F32), jax.ShapeDtypeStruct((rows, d), F32),
                   jax.ShapeDtypeStruct((rows, d), MXU_DT), jax.ShapeDtypeStruct((1, d), F32)],
        compiler_params=_cp("arbitrary"),
    )(x1, ffn, g2, target)


def _norm_bwd(dh, xin, nw, sc, *, res=None, gate=None, want_dx=True, name):
    rows, d = xin.shape
    ts = min(rows, 256)
    has_res, has_gate = res is not None, gate is not None

    def body(*refs):
        it = iter(refs)
        dh_ref, x_ref, nw_ref, sc_ref = next(it), next(it), next(it), next(it)
        res_ref = next(it) if has_res else None
        gated_ref, g_ref = (next(it), next(it)) if has_gate else (None, None)
        dx_ref = next(it) if want_dx else None
        dgx_ref = next(it) if has_gate else None
        dsh_ref, dsc_ref, dnw_ref = next(it), next(it), next(it)
        dg_ref = next(it) if has_gate else None
        i = pl.program_id(0)
        dhv = dh_ref[...]
        r, xn = _rms_stats(x_ref[...])
        dn = dhv * (1.0 + sc_ref[...])

        @pl.when(i == 0)
        def _():
            dsh_ref[...] = jnp.zeros_like(dsh_ref)
            dsc_ref[...] = jnp.zeros_like(dsc_ref)
            dnw_ref[...] = jnp.zeros_like(dnw_ref)
            if has_gate:
                dg_ref[...] = jnp.zeros_like(dg_ref)

        dsh_ref[...] += _colsum(dhv)
        dsc_ref[...] += _colsum(dhv * (xn * nw_ref[...]))
        dnw_ref[...] += _colsum(dn * xn)
        if want_dx:
            dxn = dn * nw_ref[...]
            dx = r * (dxn - xn * jnp.mean(dxn * xn, axis=-1, keepdims=True))
            if has_res:
                dx = dx + res_ref[...]
            dx_ref[...] = dx
            if has_gate:
                dgx_ref[...] = (dx * g_ref[...]).astype(dgx_ref.dtype)
                dg_ref[...] += _colsum(dx * gated_ref[...])

    ins = [dh, xin, nw, sc] + ([res] if has_res else []) + (list(gate) if has_gate else [])
    in_specs = ([_row(ts, d), _row(ts, d), _vec(d), _vec(d)] + ([_row(ts, d)] if has_res else [])
                + ([_row(ts, d), _vec(d)] if has_gate else []))
    out_specs, out_shape = [], []
    if want_dx:
        out_specs.append(_row(ts, d)); out_shape.append(jax.ShapeDtypeStruct((rows, d), F32))
    if has_gate:
        out_specs.append(_row(ts, d)); out_shape.append(jax.ShapeDtypeStruct((rows, d), MXU_DT))
    for _ in range(3 + int(has_gate)):
        out_specs.append(_vec(d)); out_shape.append(jax.ShapeDtypeStruct((1, d), F32))
    return pl.pallas_call(
        body, name=name, grid=(rows // ts,), in_specs=in_specs, out_specs=out_specs,
        out_shape=out_shape, compiler_params=_cp("arbitrary"),
    )(*ins)


def _group_sum(v, g):
    hi = v.astype(MXU_DT)
    lo = (v - hi.astype(F32)).astype(MXU_DT)
    return (jnp.dot(hi, g, preferred_element_type=F32) + jnp.dot(lo, g, preferred_element_type=F32))


def _rot(v):
    lane = lax.broadcasted_iota(jnp.int32, v.shape, 1)
    first = (lane & 31) < 16
    return jnp.where(first, -pltpu.roll(v, LANES - 16, 1), pltpu.roll(v, 16, 1))


def _head_group_matrix():
    r = jnp.arange(LANES) // HEAD_DIM
    return (r[:, None] == r[None, :]).astype(MXU_DT)


def _qk_prep(xin, width, cos, sin, w, *, name):
    rows = xin.shape[0]
    ts = min(rows, 256)
    nch = width // LANES

    def body(x_ref, cos_ref, sin_ref, w_ref, g_ref, o_ref):
        cs, sn, g = cos_ref[...], sin_ref[...], g_ref[...]
        for ch in range(nch):
            sl = slice(ch * LANES, (ch + 1) * LANES)
            xv = x_ref[:, sl]
            r = lax.rsqrt(_group_sum(xv * xv, g) * (1.0 / HEAD_DIM) + EPS)
            yw = (xv * r) * w_ref[:, sl]
            o_ref[:, sl] = (yw * cs + _rot(yw) * sn).astype(o_ref.dtype)

    return pl.pallas_call(
        body, name=name, grid=(rows // ts,),
        in_specs=[_row(ts, width), _row(ts, LANES), _row(ts, LANES), _vec(width),
                  pl.BlockSpec((LANES, LANES), lambda i: (0, 0))],
        out_specs=_row(ts, width),
        out_shape=jax.ShapeDtypeStruct((rows, width), MXU_DT), compiler_params=_cp("parallel"),
    )(xin, cos, sin, w, _head_group_matrix())


def _qk_prep_bwd(xin, dout, width, cos, sin, w, *, name):
    rows = xin.shape[0]
    ts = min(rows, 256)
    nch = width // LANES

    def body(x_ref, d_ref, cos_ref, sin_ref, w_ref, g_ref, dx_ref, dw_ref):
        i = pl.program_id(0)
        cs, sn, g = cos_ref[...], sin_ref[...], g_ref[...]

        @pl.when(i == 0)
        def _():
            dw_ref[...] = jnp.zeros_like(dw_ref)

        for ch in range(nch):
            sl = slice(ch * LANES, (ch + 1) * LANES)
            xv = x_ref[:, sl]
            dv = d_ref[:, sl].astype(F32)
            r = lax.rsqrt(_group_sum(xv * xv, g) * (1.0 / HEAD_DIM) + EPS)
            n = xv * r
            dyw = dv * cs - _rot(dv * sn)
            dw_ref[:, sl] += _colsum(dyw * n)
            dn = dyw * w_ref[:, sl]
            gm = _group_sum(dn * n, g) * (1.0 / HEAD_DIM)
            dx_ref[:, sl] = (r * (dn - n * gm)).astype(dx_ref.dtype)

    return pl.pallas_call(
        body, name=name, grid=(rows // ts,),
        in_specs=[_row(ts, width), _row(ts, width), _row(ts, LANES), _row(ts, LANES), _vec(width),
                  pl.BlockSpec((LANES, LANES), lambda i: (0, 0))],
        out_specs=[_row(ts, width), _vec(width)],
        out_shape=[jax.ShapeDtypeStruct((rows, width), MXU_DT), jax.ShapeDtypeStruct((1, width), F32)],
        compiler_params=_cp("arbitrary"),
    )(xin, dout, cos, sin, w, _head_group_matrix())


def _attn_mask(i, seq):
    shape = (GQA_GROUP * BLOCK, 3 * BLOCK)
    r = lax.broadcasted_iota(jnp.int32, shape, 0) & (BLOCK - 1)
    col = lax.broadcasted_iota(jnp.int32, shape, 1)
    diff = col - r
    kpos = col + (i - 1) * BLOCK
    return (diff >= 0) & (diff <= 2 * WINDOW) & (kpos >= 0) & (kpos < seq)


def _sink_col(sink_ref, kv):
    return jnp.concatenate(
        [jnp.full((BLOCK, 1), sink_ref[0, kv * GQA_GROUP + g], F32) for g in range(GQA_GROUP)], axis=0)


_NT = (((1,), (1,)), ((), ()))
_TN = (((0,), (0,)), ((), ()))


def _attn_specs(seq, n_ctx):
    qs = pl.BlockSpec((N_Q_HEADS, BLOCK, HEAD_DIM), lambda i: (0, i, 0))
    kvs = pl.BlockSpec((N_KV_HEADS, seq + 2 * BLOCK, HEAD_DIM), lambda i: (0, 0, 0))
    cs = pl.BlockSpec((N_KV_HEADS, n_ctx, HEAD_DIM), lambda i: (0, 0, 0))
    ls = pl.BlockSpec((N_Q_HEADS, BLOCK, 1), lambda i: (0, i, 0))
    return qs, kvs, cs, ls


def _attn_fwd(sink, q, k, v, kc, vc, *, name):
    seq, n_ctx = q.shape[1], kc.shape[1]
    scale = 1.0 / math.sqrt(HEAD_DIM)
    gb = GQA_GROUP * BLOCK

    def body(sink_ref, q_ref, k_ref, v_ref, kc_ref, vc_ref, o_ref, lse_ref):
        i = pl.program_id(0)
        valid = _attn_mask(i, seq)
        start = pl.multiple_of(i * BLOCK, BLOCK)
        for kv in range(N_KV_HEADS):
            hs = slice(kv * GQA_GROUP, (kv + 1) * GQA_GROUP)
            qs = q_ref[hs].reshape(gb, HEAD_DIM)
            kw = k_ref[kv, pl.ds(start, 3 * BLOCK), :]
            vw = v_ref[kv, pl.ds(start, 3 * BLOCK), :]
            s_loc = lax.dot_general(qs, kw, _NT, preferred_element_type=F32) * scale
            s_loc = jnp.where(valid, s_loc, NEG_INF)
            s_ctx = lax.dot_general(qs, kc_ref[kv], _NT, preferred_element_type=F32) * scale
            sk = _sink_col(sink_ref, kv)
            m = jnp.maximum(jnp.maximum(jnp.max(s_loc, axis=-1, keepdims=True),
                                        jnp.max(s_ctx, axis=-1, keepdims=True)), sk)
            p_loc = jnp.exp(s_loc - m)
            p_ctx = jnp.exp(s_ctx - m)
            l = (jnp.sum(p_loc, axis=-1, keepdims=True) + jnp.sum(p_ctx, axis=-1, keepdims=True)
                 + jnp.exp(sk - m))
            o = (jnp.dot(p_loc.astype(MXU_DT), vw, preferred_element_type=F32)
                 + jnp.dot(p_ctx.astype(MXU_DT), vc_ref[kv], preferred_element_type=F32)) / l
            o_ref[hs] = o.reshape(GQA_GROUP, BLOCK, HEAD_DIM).astype(o_ref.dtype)
            lse_ref[hs] = (m + jnp.log(l)).reshape(GQA_GROUP, BLOCK, 1)

    qs, kvs, cs, ls = _attn_specs(seq, n_ctx)
    return pl.pallas_call(
        body, name=name, grid=(seq // BLOCK,),
        in_specs=[pl.BlockSpec(memory_space=pltpu.SMEM), qs, kvs, kvs, cs, cs],
        out_specs=[qs, ls],
        out_shape=[jax.ShapeDtypeStruct(q.shape, MXU_DT), jax.ShapeDtypeStruct((N_Q_HEADS, seq, 1), F32)],
        compiler_params=_cp("parallel"),
    )(sink, q, k, v, kc, vc)


def _attn_bwd(sink, q, k, v, kc, vc, o, lse, do, *, name):
    seq, n_ctx = q.shape[1], kc.shape[1]
    scale = 1.0 / math.sqrt(HEAD_DIM)
    gb = GQA_GROUP * BLOCK

    def body(sink_ref, q_ref, k_ref, v_ref, kc_ref, vc_ref, o_ref, lse_ref, do_ref,
             dq_ref, dk_ref, dv_ref, dkc_ref, dvc_ref, ds_ref):
        i = pl.program_id(0)

        @pl.when(i == 0)
        def _():
            dk_ref[...] = jnp.zeros_like(dk_ref)
            dv_ref[...] = jnp.zeros_like(dv_ref)
            dkc_ref[...] = jnp.zeros_like(dkc_ref)
            dvc_ref[...] = jnp.zeros_like(dvc_ref)
            ds_ref[...] = jnp.zeros_like(ds_ref)

        valid = _attn_mask(i, seq)
        start = pl.multiple_of(i * BLOCK, BLOCK)
        win = pl.ds(start, 3 * BLOCK)
        for kv in range(N_KV_HEADS):
            hs = slice(kv * GQA_GROUP, (kv + 1) * GQA_GROUP)
            qs = q_ref[hs].reshape(gb, HEAD_DIM)
            dos = do_ref[hs].reshape(gb, HEAD_DIM)
            lse_s = lse_ref[hs].reshape(gb, 1)
            delta = jnp.sum(dos.astype(F32) * o_ref[hs].reshape(gb, HEAD_DIM).astype(F32),
                            axis=-1, keepdims=True)
            kw, vw = k_ref[kv, win, :], v_ref[kv, win, :]
            kcv, vcv = kc_ref[kv], vc_ref[kv]
            s_loc = lax.dot_general(qs, kw, _NT, preferred_element_type=F32) * scale
            s_loc = jnp.where(valid, s_loc, NEG_INF)
            s_ctx = lax.dot_general(qs, kcv, _NT, preferred_element_type=F32) * scale
            p_loc = jnp.exp(s_loc - lse_s)
            p_ctx = jnp.exp(s_ctx - lse_s)
            p_sink = jnp.exp(_sink_col(sink_ref, kv) - lse_s)
            dp_loc = lax.dot_general(dos, vw, _NT, preferred_element_type=F32)
            dp_ctx = lax.dot_general(dos, vcv, _NT, preferred_element_type=F32)
            ds_loc = (p_loc * (dp_loc - delta) * scale).astype(MXU_DT)
            ds_ctx = (p_ctx * (dp_ctx - delta) * scale).astype(MXU_DT)
            dq = (jnp.dot(ds_loc, kw, preferred_element_type=F32)
                  + jnp.dot(ds_ctx, kcv, preferred_element_type=F32))
            dq_ref[hs] = dq.reshape(GQA_GROUP, BLOCK, HEAD_DIM).astype(dq_ref.dtype)
            dk_ref[kv, win, :] += lax.dot_general(ds_loc, qs, _TN, preferred_element_type=F32)
            dv_ref[kv, win, :] += lax.dot_general(p_loc.astype(MXU_DT), dos, _TN, preferred_element_type=F32)
            dkc_ref[kv] += lax.dot_general(ds_ctx, qs, _TN, preferred_element_type=F32)
            dvc_ref[kv] += lax.dot_general(p_ctx.astype(MXU_DT), dos, _TN, preferred_element_type=F32)
            ds_ref[hs] += (-(p_sink * delta)).reshape(GQA_GROUP, BLOCK, 1)

    qs, kvs, cs, ls = _attn_specs(seq, n_ctx)
    sk = pl.BlockSpec((N_Q_HEADS, BLOCK, 1), lambda i: (0, 0, 0))
    return pl.pallas_call(
        body, name=name, grid=(seq // BLOCK,),
        in_specs=[pl.BlockSpec(memory_space=pltpu.SMEM), qs, kvs, kvs, cs, cs, qs, ls, qs],
        out_specs=[qs, kvs, kvs, cs, cs, sk],
        out_shape=[jax.ShapeDtypeStruct(q.shape, F32), jax.ShapeDtypeStruct(k.shape, F32),
                   jax.ShapeDtypeStruct(v.shape, F32), jax.ShapeDtypeStruct(kc.shape, F32),
                   jax.ShapeDtypeStruct(vc.shape, F32), jax.ShapeDtypeStruct((N_Q_HEADS, BLOCK, 1), F32)],
        compiler_params=_cp("arbitrary"),
    )(sink, q, k, v, kc, vc, o, lse, do)


def _halo_specs(ts, w, rows, col=0):
    per = ts // HALO
    last = rows // HALO - 1
    return [pl.BlockSpec((HALO, w), lambda i: (jnp.maximum(i * per - 1, 0), col)),
            pl.BlockSpec((ts, w), lambda i: (i, col)),
            pl.BlockSpec((HALO, w), lambda i: (jnp.minimum((i + 1) * per, last), col))]


def _glu(v):
    return v[:, :CONV_CH] * _sigmoid(v[:, CONV_CH:])


def _ln_stats(u):
    mu = jnp.mean(u, axis=-1, keepdims=True)
    xc = u - mu
    rstd = lax.rsqrt(jnp.mean(xc * xc, axis=-1, keepdims=True) + EPS)
    return xc * rstd, rstd


def _conv_fwd(glu, cw, cb, lw, lb, *, name):
    rows = glu.shape[0]
    ts = min(rows, 256)
    nt = rows // ts

    def body(gp_ref, g_ref, gn_ref, cw_ref, cb_ref, lw_ref, lb_ref, u3_ref, u1_ref, ext_ref):
        i = pl.program_id(0)
        ext_ref[0:HALO, :] = jnp.where(i > 0, _glu(gp_ref[...]), 0.0)
        ext_ref[HALO:HALO + ts, :] = _glu(g_ref[...])
        ext_ref[HALO + ts:, :] = jnp.where(i < nt - 1, _glu(gn_ref[...]), 0.0)
        acc = jnp.broadcast_to(cb_ref[...], (ts, CONV_CH))
        for j in range(CONV_K):
            acc = acc + cw_ref[j:j + 1, :] * ext_ref[HALO - CONV_PAD + j:HALO - CONV_PAD + j + ts, :]
        u1_ref[...] = acc
        xh, _ = _ln_stats(acc)
        u2 = xh * lw_ref[...] + lb_ref[...]
        u3_ref[...] = (u2 * _sigmoid(u2)).astype(u3_ref.dtype)

    full = lambda shape: pl.BlockSpec(shape, lambda i: (0,) * len(shape))
    return pl.pallas_call(
        body, name=name, grid=(nt,),
        in_specs=_halo_specs(ts, GLU_COLS, rows) + [full((CONV_K, CONV_CH))] + [_vec(CONV_CH)] * 3,
        out_specs=[_row(ts, CONV_CH), _row(ts, CONV_CH)],
        out_shape=[jax.ShapeDtypeStruct((rows, CONV_CH), MXU_DT), jax.ShapeDtypeStruct((rows, CONV_CH), F32)],
        scratch_shapes=[pltpu.VMEM((ts + 2 * HALO, CONV_CH), F32)],
        compiler_params=_cp("parallel"),
    )(glu, glu, glu, cw, cb, lw, lb)


def _conv_bwd(glu, u1, du3, cw, lw, lb, *, name):
    rows = glu.shape[0]
    ts = min(rows, 256)
    nt = rows // ts
    te = ts + 2 * HALO

    def du1_of(u1v, du3v, lw_v, lb_v):
        xh, rstd = _ln_stats(u1v)
        u2 = xh * lw_v + lb_v
        sg = _sigmoid(u2)
        du2 = du3v * (sg * (1.0 + u2 * (1.0 - sg)))
        dxh = du2 * lw_v
        du1 = rstd * (dxh - jnp.mean(dxh, axis=-1, keepdims=True)
                      - xh * jnp.mean(dxh * xh, axis=-1, keepdims=True))
        return du1, du2, xh

    def body(gp_ref, g_ref, gn_ref, up_ref, u_ref, un_ref, dp_ref, d_ref, dn_ref, cw_ref, lw_ref, lb_ref,
             dglu_ref, dcw_ref, dvec_ref, u0_ref, du1_ref):
        i = pl.program_id(0)
        lw_v, lb_v = lw_ref[...], lb_ref[...]

        @pl.when(i == 0)
        def _():
            dcw_ref[...] = jnp.zeros_like(dcw_ref)
            dvec_ref[...] = jnp.zeros_like(dvec_ref)

        gv = g_ref[...]
        u0_ref[0:HALO, :] = jnp.where(i > 0, _glu(gp_ref[...]), 0.0)
        u0_ref[HALO:HALO + ts, :] = _glu(gv)
        u0_ref[HALO + ts:, :] = jnp.where(i < nt - 1, _glu(gn_ref[...]), 0.0)
        d_prev, _, _ = du1_of(up_ref[...], dp_ref[...], lw_v, lb_v)
        d_main, du2, xh = du1_of(u_ref[...], d_ref[...], lw_v, lb_v)
        d_next, _, _ = du1_of(un_ref[...], dn_ref[...], lw_v, lb_v)
        du1_ref[0:HALO, :] = jnp.where(i > 0, d_prev, 0.0)
        du1_ref[HALO:HALO + ts, :] = d_main
        du1_ref[HALO + ts:, :] = jnp.where(i < nt - 1, d_next, 0.0)

        rid = lax.broadcasted_iota(jnp.int32, (8, CONV_CH), 0)
        dvec_ref[...] += (jnp.where(rid == 0, _colsum(d_main), 0.0)
                          + jnp.where(rid == 1, _colsum(du2 * xh), 0.0)
                          + jnp.where(rid == 2, _colsum(du2), 0.0))
        du0 = jnp.zeros((ts, CONV_CH), F32)
        for j in range(CONV_K):
            lo = HALO + CONV_PAD - j
            du0 = du0 + cw_ref[j:j + 1, :] * du1_ref[lo:lo + ts, :]
            prod = d_main * u0_ref[HALO - CONV_PAD + j:HALO - CONV_PAD + j + ts, :]
            dcw_ref[j] += jnp.sum(prod.reshape(ts // 8, 8, CONV_CH), axis=0)
        ga, sg = gv[:, :CONV_CH], _sigmoid(gv[:, CONV_CH:])
        dglu_ref[:, :CONV_CH] = (du0 * sg).astype(dglu_ref.dtype)
        dglu_ref[:, CONV_CH:] = (du0 * ga * sg * (1.0 - sg)).astype(dglu_ref.dtype)

    full = lambda shape: pl.BlockSpec(shape, lambda i: (0,) * len(shape))
    return pl.pallas_call(
        body, name=name, grid=(nt,),
        in_specs=(_halo_specs(ts, GLU_COLS, rows) + _halo_specs(ts, CONV_CH, rows)
                  + _halo_specs(ts, CONV_CH, rows) + [full((CONV_K, CONV_CH)), _vec(CONV_CH), _vec(CONV_CH)]),
        out_specs=[_row(ts, GLU_COLS), full((CONV_K, 8, CONV_CH)), full((8, CONV_CH))],
        out_shape=[jax.ShapeDtypeStruct((rows, GLU_COLS), MXU_DT),
                   jax.ShapeDtypeStruct((CONV_K, 8, CONV_CH), F32), jax.ShapeDtypeStruct((8, CONV_CH), F32)],
        scratch_shapes=[pltpu.VMEM((te, CONV_CH), F32), pltpu.VMEM((te, CONV_CH), F32)],
        compiler_params=_cp("arbitrary"),
    )(glu, glu, glu, u1, u1, u1, du3, du3, du3, cw, lw, lb)


FFN_CW = 256
FFN_NJ = FFN_H // FFN_CW


def _ffn_halo_specs(ts, rows, col_of, inner_rows):
    per = ts // HALO
    last = rows // HALO - 1
    if inner_rows:
        return [pl.BlockSpec((HALO, FFN_CW), lambda j, i: (jnp.maximum(i * per - 1, 0), col_of(j))),
                pl.BlockSpec((ts, FFN_CW), lambda j, i: (i, col_of(j))),
                pl.BlockSpec((HALO, FFN_CW), lambda j, i: (jnp.minimum((i + 1) * per, last), col_of(j)))]
    return [pl.BlockSpec((HALO, FFN_CW), lambda i, j: (jnp.maximum(i * per - 1, 0), col_of(j))),
            pl.BlockSpec((ts, FFN_CW), lambda i, j: (i, col_of(j))),
            pl.BlockSpec((HALO, FFN_CW), lambda i, j: (jnp.minimum((i + 1) * per, last), col_of(j)))]


def _shift_rows(v, down):
    return pltpu.roll(v, 1 if down else v.shape[0] - 1, 0)


def _conv3(ext, w, b):
    return w[0:1, :] * _shift_rows(ext, True) + w[1:2, :] * ext + w[2:3, :] * _shift_rows(ext, False) + b


def _ffn_ext(p_ref, m_ref, n_ref, i, nt):
    return jnp.concatenate([jnp.where(i > 0, p_ref[...].astype(F32), 0.0), m_ref[...].astype(F32),
                            jnp.where(i < nt - 1, n_ref[...].astype(F32), 0.0)], axis=0)


def _ffn_act(up0, w3, b3, *, name):
    rows = up0.shape[0]
    ts = min(rows, 256)
    nt = rows // ts

    def body(gp, g, gn, vp, v, vn, wg, wv, bg, bv, a_ref):
        i = pl.program_id(0)
        gate = _conv3(_ffn_ext(gp, g, gn, i, nt), wg[...], bg[...])[HALO:HALO + ts]
        val = _conv3(_ffn_ext(vp, v, vn, i, nt), wv[...], bv[...])[HALO:HALO + ts]
        a_ref[...] = (gate * _sigmoid(gate) * val).astype(a_ref.dtype)

    gcol, vcol = (lambda j: j), (lambda j: j + FFN_NJ)
    wspec = lambda col_of: pl.BlockSpec((FFN_K, FFN_CW), lambda i, j: (0, col_of(j)))
    bspec = lambda col_of: pl.BlockSpec((1, FFN_CW), lambda i, j: (0, col_of(j)))
    return pl.pallas_call(
        body, name=name, grid=(nt, FFN_NJ),
        in_specs=(_ffn_halo_specs(ts, rows, gcol, False) + _ffn_halo_specs(ts, rows, vcol, False)
                  + [wspec(gcol), wspec(vcol), bspec(gcol), bspec(vcol)]),
        out_specs=pl.BlockSpec((ts, FFN_CW), lambda i, j: (i, j)),
        out_shape=jax.ShapeDtypeStruct((rows, FFN_H), MXU_DT), compiler_params=_cp("parallel", "parallel"),
    )(up0, up0, up0, up0, up0, up0, w3, w3, b3, b3)


def _ffn_act_bwd(up0, da, w3, b3, *, name):
    rows = up0.shape[0]
    ts = min(rows, 256)
    nt = rows // ts

    def body(gp, g, gn, vp, v, vn, ap, a, an, wg, wv, bg, bv, dg_ref, dv_ref, sg_ref, sv_ref):
        i = pl.program_id(1)

        @pl.when(i == 0)
        def _():
            sg_ref[...] = jnp.zeros_like(sg_ref)
            sv_ref[...] = jnp.zeros_like(sv_ref)

        g_ext, v_ext = _ffn_ext(gp, g, gn, i, nt), _ffn_ext(vp, v, vn, i, nt)
        da_ext = _ffn_ext(ap, a, an, i, nt)
        wgv, wvv = wg[...], wv[...]
        gate = _conv3(g_ext, wgv, bg[...])
        val = _conv3(v_ext, wvv, bv[...])
        sgm = _sigmoid(gate)
        dgate = da_ext * val * (sgm * (1.0 + gate * (1.0 - sgm)))
        dval = da_ext * (gate * sgm)
        main = slice(HALO, HALO + ts)
        rid = lax.broadcasted_iota(jnp.int32, (8, FFN_CW), 0)
        for dup, src, w, d_ref, s_ref in ((dgate, g_ext, wgv, dg_ref, sg_ref), (dval, v_ext, wvv, dv_ref, sv_ref)):
            d0 = (w[0:1, :] * _shift_rows(dup, False) + w[1:2, :] * dup + w[2:3, :] * _shift_rows(dup, True))
            d_ref[...] = d0[main].astype(d_ref.dtype)
            dm = dup[main]
            s_ref[...] += (jnp.where(rid == 0, _colsum(dm * _shift_rows(src, True)[main]), 0.0)
                           + jnp.where(rid == 1, _colsum(dm * src[main]), 0.0)
                           + jnp.where(rid == 2, _colsum(dm * _shift_rows(src, False)[main]), 0.0)
                           + jnp.where(rid == 3, _colsum(dm), 0.0))

    gcol, vcol = (lambda j: j), (lambda j: j + FFN_NJ)
    wspec = lambda col_of: pl.BlockSpec((FFN_K, FFN_CW), lambda j, i: (0, col_of(j)))
    bspec = lambda col_of: pl.BlockSpec((1, FFN_CW), lambda j, i: (0, col_of(j)))
    ospec = pl.BlockSpec((ts, FFN_CW), lambda j, i: (i, j))
    sspec = pl.BlockSpec((8, FFN_CW), lambda j, i: (0, j))
    return pl.pallas_call(
        body, name=name, grid=(FFN_NJ, nt),
        in_specs=(_ffn_halo_specs(ts, rows, gcol, True) + _ffn_halo_specs(ts, rows, vcol, True)
                  + _ffn_halo_specs(ts, rows, gcol, True)
                  + [wspec(gcol), wspec(vcol), bspec(gcol), bspec(vcol)]),
        out_specs=[ospec, ospec, sspec, sspec],
        out_shape=[jax.ShapeDtypeStruct((rows, FFN_H), MXU_DT), jax.ShapeDtypeStruct((rows, FFN_H), MXU_DT),
                   jax.ShapeDtypeStruct((8, FFN_H), F32), jax.ShapeDtypeStruct((8, FFN_H), F32)],
        compiler_params=_cp("parallel", "arbitrary"),
    )(up0, up0, up0, up0, up0, up0, da, da, da, w3, w3, b3, b3)


def _adam_math(w, g, m, v):
    m = ADAM_B1 * m + (1.0 - ADAM_B1) * g
    v = ADAM_B2 * v + (1.0 - ADAM_B2) * (g * g)
    m_hat = m / (1.0 - ADAM_B1 ** ADAM_STEP)
    v_hat = v / (1.0 - ADAM_B2 ** ADAM_STEP)
    delta = -ADAM_LR * (m_hat / (jnp.sqrt(v_hat) + ADAM_EPS) + ADAM_WD * w)
    return delta, m, v


def _adam(w, m, v, parts, *, name):
    rows, cols = w.shape
    nparts = parts.shape[0]
    tr = rows if rows <= 256 else _pick(rows, (256, 128, 64, 32, 16, 8))

    def body(w_ref, m_ref, v_ref, p_ref, g_ref, d_ref, nm_ref, nv_ref):
        g = p_ref[0].astype(F32)
        for p in range(1, nparts):
            g = g + p_ref[p].astype(F32)
        g_ref[...] = g
        d_ref[...], nm_ref[...], nv_ref[...] = _adam_math(w_ref[...], g, m_ref[...], v_ref[...])

    spec = _row(tr, cols)
    return pl.pallas_call(
        body, name=name, grid=(rows // tr,),
        in_specs=[spec, spec, spec, pl.BlockSpec((nparts, tr, cols), lambda i: (0, i, 0))],
        out_specs=[spec] * 4, out_shape=[jax.ShapeDtypeStruct((rows, cols), F32)] * 4,
        compiler_params=_cp("parallel"),
    )(w, m, v, parts)


def _sum_parts(parts, *, name):
    nparts, rows, cols = parts.shape
    tr = rows if rows <= 256 else _pick(rows, (256, 128, 64, 32, 16, 8))

    def body(p_ref, o_ref):
        g = p_ref[0].astype(F32)
        for p in range(1, nparts):
            g = g + p_ref[p].astype(F32)
        o_ref[...] = g

    return pl.pallas_call(
        body, name=name, grid=(rows // tr,),
        in_specs=[pl.BlockSpec((nparts, tr, cols), lambda i: (0, i, 0))], out_specs=_row(tr, cols),
        out_shape=jax.ShapeDtypeStruct((rows, cols), F32), compiler_params=_cp("parallel"),
    )(parts)


def _my_place():
    return lax.axis_index("x"), lax.axis_index("y"), lax.axis_index("c")


def _dev_index(p):
    return 4 * p[0] + 2 * p[1] + p[2]


def _all_gather(xs, *, hbm, name):
    n = len(xs)
    ms = [v.shape[0] for v in xs]

    def body(*refs):
        x_refs, o_refs = refs[:n], refs[n:2 * n]
        send_sems, recv_sems, local_sems = refs[2 * n:]
        x, y, c = _my_place()
        me, sib = (x, y, c), (x, y, 1 - c)
        chips = [(1 - x, y), (x, 1 - y), (1 - x, 1 - y)]

        def rows(a, p):
            return o_refs[a].at[pl.ds(pl.multiple_of(_dev_index(p) * ms[a], 8), ms[a])]

        def copy(a, k, block, to, src=None):
            return pltpu.make_async_remote_copy(
                src_ref=rows(a, block) if src is None else src, dst_ref=rows(a, block),
                send_sem=send_sems.at[a * 7 + k], recv_sem=recv_sems.at[a * 7 + k],
                device_id=to, device_id_type=MESH)

        mine = [pltpu.make_async_copy(x_refs[a], rows(a, me), local_sems.at[a]) for a in range(n)]
        for cp in mine:
            cp.start()
        first = []
        for a in range(n):
            first.append(copy(a, 0, me, sib, src=x_refs[a]))
            first += [copy(a, 1 + j, me, (*chip, c), src=x_refs[a]) for j, chip in enumerate(chips)]
        for cp in first:
            cp.start()
        passed = []
        for j, chip in enumerate(chips):
            for a in range(n):
                copy(a, 1 + j, (*chip, c), me).wait_recv()
                cp = copy(a, 4 + j, (*chip, c), sib)
                cp.start()
                passed.append(cp)
        for a in range(n):
            copy(a, 0, sib, me).wait_recv()
            for j, chip in enumerate(chips):
                copy(a, 4 + j, (*chip, 1 - c), me).wait_recv()
        for cp in first + passed:
            cp.wait_send()
        for cp in mine:
            cp.wait()

    space = pl.ANY if hbm else pltpu.VMEM
    return pl.pallas_call(
        body, name=name,
        out_shape=[jax.ShapeDtypeStruct((N_DEV * v.shape[0], v.shape[1]), v.dtype) for v in xs],
        in_specs=[pl.BlockSpec(memory_space=space)] * n, out_specs=[pl.BlockSpec(memory_space=space)] * n,
        scratch_shapes=[pltpu.SemaphoreType.DMA((7 * n,)), pltpu.SemaphoreType.DMA((7 * n,)),
                        pltpu.SemaphoreType.DMA((n,))],
        compiler_params=pltpu.CompilerParams(vmem_limit_bytes=VMEM_LIMIT),
    )(*xs)


def _exchange_blocks(gs, *, name):
    n = len(gs)
    rs = [v.shape[0] // N_DEV for v in gs]
    flips = [(bx, by, bc) for bx in (0, 1) for by in (0, 1) for bc in (0, 1)][1:]

    def body(*refs):
        g_refs, o_refs = refs[:n], refs[n:2 * n]
        send_sems, recv_sems, local_sems = refs[2 * n:]
        x, y, c = _my_place()
        me = (x, y, c)

        def block(ref, a, p):
            return ref.at[pl.ds(_dev_index(p) * rs[a], rs[a])]

        def peer(f):
            return (1 - x if f[0] else x, 1 - y if f[1] else y, 1 - c if f[2] else c)

        def copy(a, k, to):
            return pltpu.make_async_remote_copy(
                src_ref=block(g_refs[a], a, to), dst_ref=block(o_refs[a], a, me),
                send_sem=send_sems.at[a * 7 + k], recv_sem=recv_sems.at[a * 7 + k],
                device_id=to, device_id_type=MESH)

        def arrival(a, k, frm):
            return pltpu.make_async_remote_copy(
                src_ref=block(g_refs[a], a, frm), dst_ref=block(o_refs[a], a, frm),
                send_sem=send_sems.at[a * 7 + k], recv_sem=recv_sems.at[a * 7 + k],
                device_id=frm, device_id_type=MESH)

        mine = [pltpu.make_async_copy(block(g_refs[a], a, me), block(o_refs[a], a, me), local_sems.at[a])
                for a in range(n)]
        for cp in mine:
            cp.start()
        sends = [copy(a, k, peer(f)) for a in range(n) for k, f in enumerate(flips)]
        for cp in sends:
            cp.start()
        for a in range(n):
            for k, f in enumerate(flips):
                arrival(a, k, peer(f)).wait_recv()
        for cp in sends:
            cp.wait_send()
        for cp in mine:
            cp.wait()

    return pl.pallas_call(
        body, name=name,
        out_shape=[jax.ShapeDtypeStruct(v.shape, v.dtype) for v in gs],
        in_specs=[pl.BlockSpec(memory_space=pl.ANY)] * n, out_specs=[pl.BlockSpec(memory_space=pl.ANY)] * n,
        scratch_shapes=[pltpu.SemaphoreType.DMA((7 * n,)), pltpu.SemaphoreType.DMA((7 * n,)),
                        pltpu.SemaphoreType.DMA((n,))],
        compiler_params=pltpu.CompilerParams(vmem_limit_bytes=VMEM_LIMIT),
    )(*gs)


def _rope_tables(seq):
    t = jnp.arange(seq)
    row, col = t // GRID_W, t % GRID_W
    half = HEAD_DIM // 2
    inv = ROPE_BASE ** (-jnp.arange(0, half, 2, dtype=F32) / half)
    ang_r = row.astype(F32)[:, None] * inv
    ang_c = col.astype(F32)[:, None] * inv
    ang = jnp.concatenate([ang_r, ang_r, ang_c, ang_c], axis=-1)
    return jnp.tile(jnp.cos(ang), (1, 2)), jnp.tile(jnp.sin(ang), (1, 2))


def _to_heads(v, nh):
    return v.reshape(v.shape[0], nh, HEAD_DIM).transpose(1, 0, 2)


def _from_heads(v):
    return v.transpose(1, 0, 2).reshape(v.shape[1], v.shape[0] * HEAD_DIM)


def _pad_rows(v):
    return jnp.pad(v, ((0, 0), (BLOCK, BLOCK), (0, 0)))


def _pack(vs):
    flat = jnp.concatenate([v.reshape(-1).astype(F32) for v in vs])
    total = -(-flat.shape[0] // (8 * LANES)) * (8 * LANES)
    return jnp.pad(flat, (0, total - flat.shape[0])).reshape(-1, LANES)


def _unpack(packed, like):
    flat, out, off = packed.reshape(-1), [], 0
    for v in like:
        size = math.prod(v.shape)
        out.append(flat[off:off + size].reshape(v.shape))
        off += size
    return out


def _silu(v):
    return v * jax.nn.sigmoid(v)


def kernel(x, c, ctx, c_ctx, w_mod, b_mod, norm_mix_w, w_in, q_norm_w, k_norm_w, sink_logit, conv_w, conv_b, conv_norm_w, conv_norm_b, w_out, norm_ffn_w, w_up, ffn_conv_w, ffn_conv_b, w_down, loss_target, m_c_ctx, m_w_mod, m_b_mod, m_norm_mix_w, m_w_in, m_q_norm_w, m_k_norm_w, m_sink_logit, m_conv_w, m_conv_b, m_conv_norm_w, m_conv_norm_b, m_w_out, m_norm_ffn_w, m_w_up, m_ffn_conv_w, m_ffn_conv_b, m_w_down, v_c_ctx, v_w_mod, v_b_mod, v_norm_mix_w, v_w_in, v_q_norm_w, v_k_norm_w, v_sink_logit, v_conv_w, v_conv_b, v_conv_norm_w, v_conv_norm_b, v_w_out, v_norm_ffn_w, v_w_up, v_ffn_conv_w, v_ffn_conv_b, v_w_down):
    d = D_MODEL
    seq, n_ctx = x.shape[1], ctx.shape[1]
    me = _dev_index(_my_place())
    xs, ctxs, tgt = x[0], ctx[0], loss_target[0]

    w_in_t, w_up_t, w_out_f, w_down_f = _all_gather(
        [w_in[0].T.astype(MXU_DT), w_up[0].T.astype(MXU_DT), w_out[0].astype(MXU_DT), w_down[0].astype(MXU_DT)],
        hbm=True, name="gather_weights")
    small = _pack([c[0], conv_w[0], ffn_conv_w[0]])
    small_all = _all_gather([small], hbm=False, name="gather_small")[0].reshape(N_DEV, -1)
    c_all, cw_all, fw_all = [jnp.stack(t) for t in zip(*[
        _unpack(small_all[dv], [c[0], conv_w[0], ffn_conv_w[0]]) for dv in range(N_DEV)])]
    conv_w_f = cw_all.transpose(1, 0, 2).reshape(CONV_K, CONV_CH)
    ffn_w_f = fw_all.transpose(1, 0, 2).reshape(FFN_K, 2 * FFN_H)

    mcols = w_mod.shape[2]
    act = jnp.zeros((16, d), F32).at[:N_DEV].set(_silu(c_all)).at[N_DEV].set(_silu(c_ctx))
    mod_part = _mm(act, w_mod[0], 16, mcols, d, name="mod_fwd")
    mod_all = _all_gather([mod_part], hbm=False, name="gather_mod")[0]
    mod_all = mod_all.reshape(N_DEV, 16, mcols).transpose(1, 0, 2).reshape(16, 6 * d) + b_mod
    mod = lax.dynamic_slice_in_dim(mod_all, me, 1, axis=0)
    sh1, sc1, g1, sh2, sc2, g2 = [mod[:, k * d:(k + 1) * d] for k in range(6)]
    sh1c, sc1c = mod_all[N_DEV:N_DEV + 1, :d], mod_all[N_DEV:N_DEV + 1, d:2 * d]

    cos, sin = _rope_tables(seq)
    ones_c, zeros_c = jnp.ones((n_ctx, LANES), F32), jnp.zeros((n_ctx, LANES), F32)
    qk_w = jnp.concatenate([jnp.tile(q_norm_w, (1, N_Q_HEADS)), jnp.tile(k_norm_w, (1, N_KV_HEADS))], axis=1)
    kc_w = jnp.tile(k_norm_w, (1, N_KV_HEADS))

    h = _prenorm(xs, norm_mix_w, sc1, sh1, name="prenorm_mix")
    hc = _prenorm(ctxs, norm_mix_w, sc1c, sh1c, name="prenorm_ctx")
    qkv = _mm(h, w_in_t, seq, QKV_COLS, d, tb=True, name="proj_qkv")
    glu = _mm(h, w_in_t, seq, GLU_COLS, d, tb=True, n0=QKV_COLS, name="proj_glu")
    kv_ctx = _mm(hc, w_in_t, n_ctx, 2 * KV_COLS, d, tb=True, n0=Q_COLS, name="proj_ctx")
    qk_r = _qk_prep(qkv, QK_COLS, cos, sin, qk_w, name="qk_prep")
    kc_n = _qk_prep(kv_ctx, KV_COLS, ones_c, zeros_c, kc_w, name="k_ctx_prep")
    q_h = _to_heads(qk_r[:, :Q_COLS], N_Q_HEADS)
    k_h = _pad_rows(_to_heads(qk_r[:, Q_COLS:], N_KV_HEADS))
    v_h = _pad_rows(_to_heads(qkv[:, QK_COLS:].astype(MXU_DT), N_KV_HEADS))
    kc_h = _to_heads(kc_n, N_KV_HEADS)
    vc_h = _to_heads(kv_ctx[:, KV_COLS:].astype(MXU_DT), N_KV_HEADS)
    o_h, lse = _attn_fwd(sink_logit, q_h, k_h, v_h, kc_h, vc_h, name="attn_fwd")
    attn_o = _from_heads(o_h)
    u3, u1 = _conv_fwd(glu, conv_w_f, conv_b, conv_norm_w, conv_norm_b, name="conv_fwd")
    mix = _mm(attn_o, w_out_f, seq, d, Q_COLS, name="out_attn")
    mix = _mm(u3, w_out_f, seq, d, CONV_CH, k0=Q_COLS, add=mix, name="out_conv")

    x1, h2 = _resid_prenorm(xs, mix, g1, norm_ffn_w, sc2, sh2, name="prenorm_ffn")
    up0 = _mm(h2, w_up_t, seq, 2 * FFN_H, d, tb=True, out_dtype=MXU_DT, name="ffn_up")
    act_a = _ffn_act(up0, ffn_w_f, ffn_conv_b, name="ffn_act")
    ffn = _mm(act_a, w_down_f, seq, d, FFN_H, name="ffn_down")
    loss_p, dy, dffn, dg2 = _loss_head(x1, ffn, g2, tgt, name="loss_head")
    loss = lax.psum(loss_p[0, 0], MESH_AXES)

    da = _mm(dffn, w_down_f, seq, FFN_H, d, tb=True, out_dtype=MXU_DT, name="ffn_down_dx")
    gw_down = _mm(act_a, dffn, FFN_H, d, seq, ta=True, out_dtype=MXU_DT, name="ffn_down_dw")
    dgate0, dval0, s_gate, s_val = _ffn_act_bwd(up0, da, ffn_w_f, ffn_conv_b, name="ffn_act_bwd")
    dh2 = _mm(dgate0, w_up_t, seq, d, FFN_H, name="ffn_up_dx_gate")
    dh2 = _mm(dval0, w_up_t, seq, d, FFN_H, k0=FFN_H, add=dh2, name="ffn_up_dx_val")
    gw_up_t = jnp.concatenate([
        _mm(dgate0, h2, FFN_H, d, seq, ta=True, out_dtype=MXU_DT, name="ffn_up_dw_gate"),
        _mm(dval0, h2, FFN_H, d, seq, ta=True, out_dtype=MXU_DT, name="ffn_up_dw_val")], axis=0)
    dx1, dmix, dsh2, dsc2, dnw2, dg1 = _norm_bwd(
        dh2, x1, norm_ffn_w, sc2, res=dy, gate=(mix, g1), name="prenorm_ffn_bwd")

    dattn = _mm(dmix, w_out_f, seq, Q_COLS, d, tb=True, out_dtype=MXU_DT, name="out_dx_attn")
    du3 = _mm(dmix, w_out_f, seq, CONV_CH, d, tb=True, n0=Q_COLS, name="out_dx_conv")
    gw_out = jnp.concatenate([
        _mm(attn_o, dmix, Q_COLS, d, seq, ta=True, out_dtype=MXU_DT, name="out_dw_attn"),
        _mm(u3, dmix, CONV_CH, d, seq, ta=True, out_dtype=MXU_DT, name="out_dw_conv")], axis=0)
    dglu, dcw8, dvec = _conv_bwd(glu, u1, du3, conv_w_f, conv_norm_w, conv_norm_b, name="conv_bwd")
    dq_h, dk_h, dv_h, dkc_h, dvc_h, dsink_rows = _attn_bwd(
        sink_logit, q_h, k_h, v_h, kc_h, vc_h, o_h, lse, _to_heads(dattn, N_Q_HEADS), name="attn_bwd")
    dqk_r = jnp.concatenate([_from_heads(dq_h), _from_heads(dk_h[:, BLOCK:BLOCK + seq])], axis=1)
    dqk, dqk_w = _qk_prep_bwd(qkv, dqk_r, QK_COLS, cos, sin, qk_w, name="qk_prep_bwd")
    dkc, dkc_w = _qk_prep_bwd(kv_ctx, _from_heads(dkc_h), KV_COLS, ones_c, zeros_c, kc_w, name="k_ctx_prep_bwd")
    dqkv = jnp.concatenate([dqk, _from_heads(dv_h[:, BLOCK:BLOCK + seq]).astype(MXU_DT)], axis=1)
    dkv_ctx = jnp.concatenate([dkc, _from_heads(dvc_h).astype(MXU_DT)], axis=1)
    dh = _mm(dqkv, w_in_t, seq, d, QKV_COLS, name="proj_dx_qkv")
    dh = _mm(dglu, w_in_t, seq, d, GLU_COLS, k0=QKV_COLS, add=dh, name="proj_dx_glu")
    dhc = _mm(dkv_ctx, w_in_t, n_ctx, d, 2 * KV_COLS, k0=Q_COLS, name="proj_dx_ctx")
    gw_qkv = _mm(dqkv, h, QKV_COLS, d, seq, ta=True, name="proj_dw_qkv")
    gw_ctx = _mm(dkv_ctx, hc, 2 * KV_COLS, d, n_ctx, ta=True, name="proj_dw_ctx")
    gw_glu = _mm(dglu, h, GLU_COLS, d, seq, ta=True, out_dtype=MXU_DT, name="proj_dw_glu")
    gw_in_t = jnp.concatenate([gw_qkv.at[Q_COLS:].add(gw_ctx).astype(MXU_DT), gw_glu], axis=0)
    grad_x, dsh1, dsc1, dnw1 = _norm_bwd(dh, xs, norm_mix_w, sc1, res=dx1, name="prenorm_mix_bwd")
    dsh1c, dsc1c, dnw1c = _norm_bwd(dhc, ctxs, norm_mix_w, sc1c, want_dx=False, name="prenorm_ctx_bwd")

    rx_in, rx_up, rx_out, rx_down = _exchange_blocks([gw_in_t, gw_up_t, gw_out, gw_down], name="exchange_grads")
    dmod = jnp.concatenate([dsh1, dsc1, dg1, dsh2, dsc2, dg2], axis=1)
    dmod_ctx = jnp.concatenate([dsh1c, dsc1c], axis=1)
    d_qn = dqk_w[0, :Q_COLS].reshape(N_Q_HEADS, HEAD_DIM).sum(0)
    d_kn = (dqk_w[0, Q_COLS:].reshape(N_KV_HEADS, HEAD_DIM).sum(0)
            + dkc_w[0].reshape(N_KV_HEADS, HEAD_DIM).sum(0))
    d_ffn_w = jnp.concatenate([s_gate[:FFN_K], s_val[:FFN_K]], axis=1)
    d_ffn_b = jnp.concatenate([s_gate[FFN_K], s_val[FFN_K]])
    summed_like = [dnw1 + dnw1c, d_qn, d_kn, dsink_rows.sum((1, 2)), dvec[0], dvec[1], dvec[2], dnw2,
                   d_ffn_b, dcw8.sum(1), d_ffn_w]
    pack = _pack([dmod, dmod_ctx] + summed_like)
    pack_all = _all_gather([pack], hbm=False, name="gather_small_grads")[0]
    pack_all = pack_all.reshape(N_DEV, pack.shape[0], LANES)
    tot = _sum_parts(pack_all, name="sum_small_grads")
    (dmod_sum, dmc_sum, g_nmix, g_qn, g_kn, g_sink, g_cb, g_lw, g_lb, g_nffn, g_fb, g_cw_f, g_fw_f) = _unpack(
        tot, [dmod, dmod_ctx] + summed_like)
    dmod_all = pack_all.reshape(N_DEV, -1)[:, :6 * d]
    g_b_mod = dmod_sum.at[:, :2 * d].add(dmc_sum)

    lo = me * mcols
    dm_rows = jnp.zeros((16, 6 * d), F32).at[:N_DEV].set(dmod_all).at[N_DEV, :2 * d].set(dmc_sum[0])
    dm_mine = lax.dynamic_slice_in_dim(dm_rows, lo, mcols, axis=1)
    parts_mod = _mm(act, dm_mine, d, mcols, 16, ta=True, name="mod_dw")[None]
    dact_part = _mm(dm_mine[N_DEV:N_DEV + 8], w_mod[0], 8, d, mcols, tb=True, name="mod_dx_ctx")
    dact_all = _all_gather([dact_part], hbm=False, name="gather_c_ctx_grad")[0].reshape(N_DEV, 8, d)
    dact = _sum_parts(dact_all, name="sum_c_ctx_grad")[0]
    sg = jax.nn.sigmoid(c_ctx)
    g_c_ctx = dact * (sg * (1.0 + c_ctx * (1.0 - sg)))

    def stacked(rx):
        return rx.reshape(N_DEV, rx.shape[0] // N_DEV, rx.shape[1])

    g_w_in = _sum_parts(stacked(rx_in), name="sum_w_in").T[None]
    g_w_up = _sum_parts(stacked(rx_up), name="sum_w_up").T[None]
    big = {}
    big["w_in"] = _adam(w_in[0], m_w_in[0], v_w_in[0], g_w_in, name="adam_w_in")
    big["w_up"] = _adam(w_up[0], m_w_up[0], v_w_up[0], g_w_up, name="adam_w_up")
    big["w_out"] = _adam(w_out[0], m_w_out[0], v_w_out[0], stacked(rx_out), name="adam_w_out")
    big["w_down"] = _adam(w_down[0], m_w_down[0], v_w_down[0], stacked(rx_down), name="adam_w_down")
    big["w_mod"] = _adam(w_mod[0], m_w_mod[0], v_w_mod[0], parts_mod, name="adam_w_mod")

    ccols, fcols = conv_w.shape[2], ffn_conv_w.shape[2]
    g_conv_w = lax.dynamic_slice_in_dim(g_cw_f, me * ccols, ccols, axis=1)
    g_ffn_w = lax.dynamic_slice_in_dim(g_fw_f, me * fcols, fcols, axis=1)
    names = ["c_ctx", "b_mod", "norm_mix_w", "q_norm_w", "k_norm_w", "sink_logit", "conv_w", "conv_b",
             "conv_norm_w", "conv_norm_b", "norm_ffn_w", "ffn_conv_w", "ffn_conv_b"]
    ws = [c_ctx, b_mod, norm_mix_w, q_norm_w, k_norm_w, sink_logit, conv_w, conv_b, conv_norm_w, conv_norm_b,
          norm_ffn_w, ffn_conv_w, ffn_conv_b]
    msm = [m_c_ctx, m_b_mod, m_norm_mix_w, m_q_norm_w, m_k_norm_w, m_sink_logit, m_conv_w, m_conv_b,
           m_conv_norm_w, m_conv_norm_b, m_norm_ffn_w, m_ffn_conv_w, m_ffn_conv_b]
    vsm = [v_c_ctx, v_b_mod, v_norm_mix_w, v_q_norm_w, v_k_norm_w, v_sink_logit, v_conv_w, v_conv_b,
           v_conv_norm_w, v_conv_norm_b, v_norm_ffn_w, v_ffn_conv_w, v_ffn_conv_b]
    gsm = [g_c_ctx, g_b_mod, g_nmix, g_qn, g_kn, g_sink, g_conv_w, g_cb, g_lw, g_lb, g_nffn, g_ffn_w, g_fb]
    res = _adam(_pack(ws), _pack(msm), _pack(vsm), _pack(gsm)[None], name="adam_small")
    sm = {nm: vals for nm, vals in zip(names, zip(*[_unpack(r, ws) for r in res]))}

    def out4(nm):
        if nm in sm:
            return sm[nm]
        return tuple(t[None] for t in big[nm])

    order = ["c_ctx", "w_mod", "b_mod", "norm_mix_w", "w_in", "q_norm_w", "k_norm_w", "sink_logit", "conv_w",
             "conv_b", "conv_norm_w", "conv_norm_b", "w_out", "norm_ffn_w", "w_up", "ffn_conv_w", "ffn_conv_b",
             "w_down"]
    quads = [out4(nm) for nm in order]
    return (loss, grad_x[None], *[q[0] for q in quads], *[q[1] for q in quads],
            *[q[2] for q in quads], *[q[3] for q in quads])
```

```python
import functools
import math

import jax
import jax.numpy as jnp
from jax import lax
from jax.experimental import pallas as pl
from jax.experimental.pallas import tpu as pltpu

F32 = jnp.float32
MXU_DT = jnp.bfloat16

D_MODEL = 1024
GRID_W = 64
HEAD_DIM = 64
N_Q_HEADS = 8
N_KV_HEADS = 2
GQA_GROUP = 4
WINDOW = 128
BLOCK = 128
Q_COLS = 512
KV_COLS = 128
QK_COLS = Q_COLS + KV_COLS
QKV_COLS = Q_COLS + 2 * KV_COLS
CONV_CH = 512
GLU_COLS = 2 * CONV_CH
IN_COLS = QKV_COLS + GLU_COLS
CONV_K = 31
CONV_PAD = 15
FFN_H = 2816
FFN_K = 3
ROPE_BASE = 10000.0
EPS = 1e-6
NEG_INF = -1e30
N_DEV = 8
HALO = 16
LANES = 128

ADAM_LR = 0.001
ADAM_B1 = 0.9
ADAM_B2 = 0.999
ADAM_EPS = 1e-08
ADAM_WD = 0.01
ADAM_STEP = 10

MESH_AXES = ("x", "y", "c")
MESH = pl.DeviceIdType.MESH
VMEM_LIMIT = 56 << 20
MM_VMEM_BUDGET = 44 << 20
GLU_OFF = 1024
IN_PAD = GLU_OFF + GLU_COLS


def _cp(*dims):
    return pltpu.CompilerParams(dimension_semantics=dims or None, vmem_limit_bytes=VMEM_LIMIT)


def _pick(n, prefs, also=0):
    for p in prefs:
        if n % p == 0 and also % p == 0:
            return p
    raise ValueError((n, prefs, also))


def _row(ts, w, col=0):
    return pl.BlockSpec((ts, w), lambda i: (i, col))


def _vec(w):
    return pl.BlockSpec((1, w), lambda i: (0, 0))


def _colsum(v):
    return jnp.sum(v, axis=0, keepdims=True)


def _sigmoid(v):
    return 0.5 * jnp.tanh(0.5 * v) + 0.5


def _mm(a, b, m, n, k, *, ta=False, tb=False, n0=0, k0=0, add=None, out_dtype=F32, name):
    has_add = add is not None
    sa, sb, so = a.dtype.itemsize, b.dtype.itemsize, jnp.dtype(out_dtype).itemsize
    sadd = add.dtype.itemsize if has_add else 0

    def fits(tm, tn):
        return 2 * (k * (tm * sa + tn * sb) + tm * tn * (so + sadd)) <= MM_VMEM_BUDGET

    tms = [m] if m <= 1024 else [t for t in (1024, 1408, 768, 512, 256, 128) if m % t == 0]
    tns = [t for t in ((1024, 512, 256, 128) if ta else (512, 1408, 256, 128)) if n % t == 0 and n0 % t == 0]
    tm, tn = next((tm, tn) for tm in tms for tn in tns if fits(tm, tn))
    assert k0 % k == 0 or k0 == 0, (k0, k)
    a_spec = (pl.BlockSpec((k, tm), lambda i, j: (0, i)) if ta else pl.BlockSpec((tm, k), lambda i, j: (i, 0)))
    nb0 = n0 // tn
    if tb:
        assert k0 == 0
        b_spec = pl.BlockSpec((tn, k), lambda i, j: (j + nb0, 0))
    else:
        assert k0 % k == 0, (k0, k)
        kb0 = k0 // k
        b_spec = pl.BlockSpec((k, tn), lambda i, j: (kb0, j + nb0))
    o_spec = pl.BlockSpec((tm, tn), lambda i, j: (i, j))
    dims = (((0 if ta else 1,), (1 if tb else 0,)), ((), ()))

    def body(*refs):
        a_ref, b_ref, o_ref = refs[0], refs[1], refs[-1]
        res = lax.dot_general(a_ref[...].astype(MXU_DT), b_ref[...].astype(MXU_DT), dims,
                              preferred_element_type=F32)
        if has_add:
            res = res + refs[2][...].astype(F32)
        o_ref[...] = res.astype(o_ref.dtype)

    ins = [a, b] + ([add] if has_add else [])
    specs = [a_spec, b_spec] + ([o_spec] if has_add else [])
    return pl.pallas_call(
        body, name=name, grid=(m // tm, n // tn),
        in_specs=specs, out_specs=o_spec,
        out_shape=jax.ShapeDtypeStruct((m, n), out_dtype),
        compiler_params=_cp("parallel", "parallel"),
    )(*ins)


def _rms_stats(xv):
    r = lax.rsqrt(jnp.mean(xv * xv, axis=-1, keepdims=True) + EPS)
    return r, xv * r


def _prenorm(x, nw, sc, sh, *, name):
    rows, d = x.shape
    ts = min(rows, 256)

    def body(x_ref, nw_ref, sc_ref, sh_ref, h_ref):
        _, xn = _rms_stats(x_ref[...])
        h_ref[...] = ((xn * nw_ref[...]) * (1.0 + sc_ref[...]) + sh_ref[...]).astype(h_ref.dtype)

    return pl.pallas_call(
        body, name=name, grid=(rows // ts,),
        in_specs=[_row(ts, d), _vec(d), _vec(d), _vec(d)], out_specs=_row(ts, d),
        out_shape=jax.ShapeDtypeStruct((rows, d), MXU_DT), compiler_params=_cp("parallel"),
    )(x, nw, sc, sh)


def _resid_prenorm(x, mix, g1, nw, sc, sh, *, name):
    rows, d = x.shape
    ts = min(rows, 256)

    def body(x_ref, mix_ref, g_ref, nw_ref, sc_ref, sh_ref, x1_ref, h_ref):
        x1 = x_ref[...] + g_ref[...] * mix_ref[...]
        x1_ref[...] = x1
        _, xn = _rms_stats(x1)
        h_ref[...] = ((xn * nw_ref[...]) * (1.0 + sc_ref[...]) + sh_ref[...]).astype(h_ref.dtype)

    return pl.pallas_call(
        body, name=name, grid=(rows // ts,),
        in_specs=[_row(ts, d), _row(ts, d), _vec(d), _vec(d), _vec(d), _vec(d)],
        out_specs=[_row(ts, d), _row(ts, d)],
        out_shape=[jax.ShapeDtypeStruct((rows, d), F32), jax.ShapeDtypeStruct((rows, d), MXU_DT)],
        compiler_params=_cp("parallel"),
    )(x, mix, g1, nw, sc, sh)


def _loss_head(x1, ffn, g2, target, *, name):
    rows, d = x1.shape
    ts = min(rows, 256)

    def body(x1_ref, f_ref, g_ref, t_ref, loss_ref, dy_ref, dffn_ref, dg_ref):
        i = pl.program_id(0)
        f = f_ref[...]
        e = x1_ref[...] + g_ref[...] * f - t_ref[...]
        per_tok = jnp.mean(e * e, axis=-1, keepdims=True)
        part = 0.5 * jnp.sum(per_tok, axis=0, keepdims=True)
        dy = e * (1.0 / d)
        dy_ref[...] = dy
        dffn_ref[...] = (dy * g_ref[...]).astype(dffn_ref.dtype)

        @pl.when(i == 0)
        def _():
            loss_ref[...] = jnp.zeros_like(loss_ref)
            dg_ref[...] = jnp.zeros_like(dg_ref)

        loss_ref[...] += jnp.broadcast_to(part, loss_ref.shape)
        dg_ref[...] += _colsum(dy * f)

    return pl.pallas_call(
        body, name=name, grid=(rows // ts,),
        in_specs=[_row(ts, d), _row(ts, d), _vec(d), _row(ts, d)],
        out_specs=[pl.BlockSpec((8, LANES), lambda i: (0, 0)), _row(ts, d), _row(ts, d), _vec(d)],
        out_shape=[jax.ShapeDtypeStruct((8, LANES), F32), jax.ShapeDtypeStruct((rows, d), F32),
                   jax.ShapeDtypeStruct((rows, d), MXU_DT), jax.ShapeDtypeStruct((1, d), F32)],
        compiler_params=_cp("arbitrary"),
    )(x1, ffn, g2, target)


def _norm_bwd(dh, xin, nw, sc, *, res=None, gate=None, want_dx=True, name):
    rows, d = xin.shape
    ts = min(rows, 256)
    has_res, has_gate = res is not None, gate is not None

    def body(*refs):
        it = iter(refs)
        dh_ref, x_ref, nw_ref, sc_ref = next(it), next(it), next(it), next(it)
        res_ref = next(it) if has_res else None
        gated_ref, g_ref = (next(it), next(it)) if has_gate else (None, None)
        dx_ref = next(it) if want_dx else None
        dgx_ref = next(it) if has_gate else None
        dsh_ref, dsc_ref, dnw_ref = next(it), next(it), next(it)
        dg_ref = next(it) if has_gate else None
        i = pl.program_id(0)
        dhv = dh_ref[...]
        r, xn = _rms_stats(x_ref[...])
        dn = dhv * (1.0 + sc_ref[...])

        @pl.when(i == 0)
        def _():
            dsh_ref[...] = jnp.zeros_like(dsh_ref)
            dsc_ref[...] = jnp.zeros_like(dsc_ref)
            dnw_ref[...] = jnp.zeros_like(dnw_ref)
            if has_gate:
                dg_ref[...] = jnp.zeros_like(dg_ref)

        dsh_ref[...] += _colsum(dhv)
        dsc_ref[...] += _colsum(dhv * (xn * nw_ref[...]))
        dnw_ref[...] += _colsum(dn * xn)
        if want_dx:
            dxn = dn * nw_ref[...]
            dx = r * (dxn - xn * jnp.mean(dxn * xn, axis=-1, keepdims=True))
            if has_res:
                dx = dx + res_ref[...]
            dx_ref[...] = dx
            if has_gate:
                dgx_ref[...] = (dx * g_ref[...]).astype(dgx_ref.dtype)
                dg_ref[...] += _colsum(dx * gated_ref[...])

    ins = [dh, xin, nw, sc] + ([res] if has_res else []) + (list(gate) if has_gate else [])
    in_specs = ([_row(ts, d), _row(ts, d), _vec(d), _vec(d)] + ([_row(ts, d)] if has_res else [])
                + ([_row(ts, d), _vec(d)] if has_gate else []))
    out_specs, out_shape = [], []
    if want_dx:
        out_specs.append(_row(ts, d)); out_shape.append(jax.ShapeDtypeStruct((rows, d), F32))
    if has_gate:
        out_specs.append(_row(ts, d)); out_shape.append(jax.ShapeDtypeStruct((rows, d), MXU_DT))
    for _ in range(3 + int(has_gate)):
        out_specs.append(_vec(d)); out_shape.append(jax.ShapeDtypeStruct((1, d), F32))
    return pl.pallas_call(
        body, name=name, grid=(rows // ts,), in_specs=in_specs, out_specs=out_specs,
        out_shape=out_shape, compiler_params=_cp("arbitrary"),
    )(*ins)


def _group_sum(v, g):
    hi = v.astype(MXU_DT)
    lo = (v - hi.astype(F32)).astype(MXU_DT)
    return (jnp.dot(hi, g, preferred_element_type=F32) + jnp.dot(lo, g, preferred_element_type=F32))


def _rot(v):
    lane = lax.broadcasted_iota(jnp.int32, v.shape, 1)
    first = (lane & 31) < 16
    return jnp.where(first, -pltpu.roll(v, LANES - 16, 1), pltpu.roll(v, 16, 1))


def _head_group_matrix():
    r = jnp.arange(LANES) // HEAD_DIM
    return (r[:, None] == r[None, :]).astype(MXU_DT)


def _qk_prep(xin, width, cos, sin, w, *, name):
    rows = xin.shape[0]
    ts = min(rows, 256)
    nch = width // LANES

    def body(x_ref, cos_ref, sin_ref, w_ref, g_ref, o_ref):
        cs, sn, g = cos_ref[...], sin_ref[...], g_ref[...]
        for ch in range(nch):
            sl = slice(ch * LANES, (ch + 1) * LANES)
            xv = x_ref[:, sl]
            r = lax.rsqrt(_group_sum(xv * xv, g) * (1.0 / HEAD_DIM) + EPS)
            yw = (xv * r) * w_ref[:, sl]
            o_ref[:, sl] = (yw * cs + _rot(yw) * sn).astype(o_ref.dtype)

    return pl.pallas_call(
        body, name=name, grid=(rows // ts,),
        in_specs=[_row(ts, width), _row(ts, LANES), _row(ts, LANES), _vec(width),
                  pl.BlockSpec((LANES, LANES), lambda i: (0, 0))],
        out_specs=_row(ts, width),
        out_shape=jax.ShapeDtypeStruct((rows, width), MXU_DT), compiler_params=_cp("parallel"),
    )(xin, cos, sin, w, _head_group_matrix())


def _qk_prep_bwd(xin, dout, width, cos, sin, w, *, tail=None, name):
    rows = xin.shape[0]
    ts = min(rows, 256)
    nch = width // LANES
    has_tail = tail is not None

    def body(*refs):
        if has_tail:
            x_ref, d_ref, t_ref, cos_ref, sin_ref, w_ref, g_ref, dx_ref, dw_ref = refs
        else:
            x_ref, d_ref, cos_ref, sin_ref, w_ref, g_ref, dx_ref, dw_ref = refs
        i = pl.program_id(0)
        cs, sn, g = cos_ref[...], sin_ref[...], g_ref[...]

        @pl.when(i == 0)
        def _():
            dw_ref[...] = jnp.zeros_like(dw_ref)

        if has_tail:
            dx_ref[:, width:width + LANES] = t_ref[...].astype(dx_ref.dtype)
            dx_ref[:, width + LANES:] = jnp.zeros((ts, GLU_OFF - width - LANES), dx_ref.dtype)

        for ch in range(nch):
            sl = slice(ch * LANES, (ch + 1) * LANES)
            xv = x_ref[:, sl]
            dv = d_ref[:, sl].astype(F32)
            r = lax.rsqrt(_group_sum(xv * xv, g) * (1.0 / HEAD_DIM) + EPS)
            n = xv * r
            dyw = dv * cs - _rot(dv * sn)
            dw_ref[:, sl] += _colsum(dyw * n)
            dn = dyw * w_ref[:, sl]
            gm = _group_sum(dn * n, g) * (1.0 / HEAD_DIM)
            dx_ref[:, sl] = (r * (dn - n * gm)).astype(dx_ref.dtype)

    ins = [xin, dout] + ([tail] if has_tail else []) + [cos, sin, w, _head_group_matrix()]
    in_specs = ([_row(ts, width), _row(ts, width)] + ([_row(ts, LANES)] if has_tail else [])
                + [_row(ts, LANES), _row(ts, LANES), _vec(width), pl.BlockSpec((LANES, LANES), lambda i: (0, 0))])
    out_w, arr_w = (GLU_OFF, IN_PAD) if has_tail else (width, width)
    return pl.pallas_call(
        body, name=name, grid=(rows // ts,), in_specs=in_specs,
        out_specs=[_row(ts, out_w), _vec(width)],
        out_shape=[jax.ShapeDtypeStruct((rows, arr_w), MXU_DT), jax.ShapeDtypeStruct((1, width), F32)],
        compiler_params=_cp("arbitrary"),
    )(*ins)


def _attn_mask(i, seq):
    shape = (GQA_GROUP * BLOCK, 3 * BLOCK)
    r = lax.broadcasted_iota(jnp.int32, shape, 0) & (BLOCK - 1)
    col = lax.broadcasted_iota(jnp.int32, shape, 1)
    diff = col - r
    kpos = col + (i - 1) * BLOCK
    return (diff >= 0) & (diff <= 2 * WINDOW) & (kpos >= 0) & (kpos < seq)


def _sink_col(sink_ref, kv):
    return jnp.concatenate(
        [jnp.full((BLOCK, 1), sink_ref[0, kv * GQA_GROUP + g], F32) for g in range(GQA_GROUP)], axis=0)


_NT = (((1,), (1,)), ((), ()))
_TN = (((0,), (0,)), ((), ()))


def _attn_specs(seq, n_ctx):
    qs = pl.BlockSpec((N_Q_HEADS, BLOCK, HEAD_DIM), lambda i: (0, i, 0))
    kvs = pl.BlockSpec((N_KV_HEADS, seq + 2 * BLOCK, HEAD_DIM), lambda i: (0, 0, 0))
    cs = pl.BlockSpec((N_KV_HEADS, n_ctx, HEAD_DIM), lambda i: (0, 0, 0))
    ls = pl.BlockSpec((N_Q_HEADS, BLOCK, 1), lambda i: (0, i, 0))
    return qs, kvs, cs, ls


def _attn_fwd(sink, q, k, v, kc, vc, *, name):
    seq, n_ctx = q.shape[1], kc.shape[1]
    scale = 1.0 / math.sqrt(HEAD_DIM)
    gb = GQA_GROUP * BLOCK

    def body(sink_ref, q_ref, k_ref, v_ref, kc_ref, vc_ref, o_ref, lse_ref):
        i = pl.program_id(0)
        valid = _attn_mask(i, seq)
        start = pl.multiple_of(i * BLOCK, BLOCK)
        for kv in range(N_KV_HEADS):
            hs = slice(kv * GQA_GROUP, (kv + 1) * GQA_GROUP)
            qs = q_ref[hs].reshape(gb, HEAD_DIM)
            kw = k_ref[kv, pl.ds(start, 3 * BLOCK), :]
            vw = v_ref[kv, pl.ds(start, 3 * BLOCK), :]
            s_loc = lax.dot_general(qs, kw, _NT, preferred_element_type=F32) * scale
            s_loc = jnp.where(valid, s_loc, NEG_INF)
            s_ctx = lax.dot_general(qs, kc_ref[kv], _NT, preferred_element_type=F32) * scale
            sk = _sink_col(sink_ref, kv)
            m = jnp.maximum(jnp.maximum(jnp.max(s_loc, axis=-1, keepdims=True),
                                        jnp.max(s_ctx, axis=-1, keepdims=True)), sk)
            p_loc = jnp.exp(s_loc - m)
            p_ctx = jnp.exp(s_ctx - m)
            l = (jnp.sum(p_loc, axis=-1, keepdims=True) + jnp.sum(p_ctx, axis=-1, keepdims=True)
                 + jnp.exp(sk - m))
            o = (jnp.dot(p_loc.astype(MXU_DT), vw, preferred_element_type=F32)
                 + jnp.dot(p_ctx.astype(MXU_DT), vc_ref[kv], preferred_element_type=F32)) / l
            o_ref[hs] = o.reshape(GQA_GROUP, BLOCK, HEAD_DIM).astype(o_ref.dtype)
            lse_ref[hs] = (m + jnp.log(l)).reshape(GQA_GROUP, BLOCK, 1)

    qs, kvs, cs, ls = _attn_specs(seq, n_ctx)
    return pl.pallas_call(
        body, name=name, grid=(seq // BLOCK,),
        in_specs=[pl.BlockSpec(memory_space=pltpu.SMEM), qs, kvs, kvs, cs, cs],
        out_specs=[qs, ls],
        out_shape=[jax.ShapeDtypeStruct(q.shape, MXU_DT), jax.ShapeDtypeStruct((N_Q_HEADS, seq, 1), F32)],
        compiler_params=_cp("parallel"),
    )(sink, q, k, v, kc, vc)


def _attn_bwd(sink, q, k, v, kc, vc, o, lse, do, *, name):
    seq, n_ctx = q.shape[1], kc.shape[1]
    scale = 1.0 / math.sqrt(HEAD_DIM)
    gb = GQA_GROUP * BLOCK

    def body(sink_ref, q_ref, k_ref, v_ref, kc_ref, vc_ref, o_ref, lse_ref, do_ref,
             dq_ref, dk_ref, dv_ref, dkc_ref, dvc_ref, ds_ref):
        i = pl.program_id(0)

        @pl.when(i == 0)
        def _():
            dk_ref[...] = jnp.zeros_like(dk_ref)
            dv_ref[...] = jnp.zeros_like(dv_ref)
            dkc_ref[...] = jnp.zeros_like(dkc_ref)
            dvc_ref[...] = jnp.zeros_like(dvc_ref)
            ds_ref[...] = jnp.zeros_like(ds_ref)

        valid = _attn_mask(i, seq)
        start = pl.multiple_of(i * BLOCK, BLOCK)
        win = pl.ds(start, 3 * BLOCK)
        for kv in range(N_KV_HEADS):
            hs = slice(kv * GQA_GROUP, (kv + 1) * GQA_GROUP)
            qs = q_ref[hs].reshape(gb, HEAD_DIM)
            dos = do_ref[hs].reshape(gb, HEAD_DIM)
            lse_s = lse_ref[hs].reshape(gb, 1)
            delta = jnp.sum(dos.astype(F32) * o_ref[hs].reshape(gb, HEAD_DIM).astype(F32),
                            axis=-1, keepdims=True)
            kw, vw = k_ref[kv, win, :], v_ref[kv, win, :]
            kcv, vcv = kc_ref[kv], vc_ref[kv]
            s_loc = lax.dot_general(qs, kw, _NT, preferred_element_type=F32) * scale
            s_loc = jnp.where(valid, s_loc, NEG_INF)
            s_ctx = lax.dot_general(qs, kcv, _NT, preferred_element_type=F32) * scale
            p_loc = jnp.exp(s_loc - lse_s)
            p_ctx = jnp.exp(s_ctx - lse_s)
            p_sink = jnp.exp(_sink_col(sink_ref, kv) - lse_s)
            dp_loc = lax.dot_general(dos, vw, _NT, preferred_element_type=F32)
            dp_ctx = lax.dot_general(dos, vcv, _NT, preferred_element_type=F32)
            ds_loc = (p_loc * (dp_loc - delta) * scale).astype(MXU_DT)
            ds_ctx = (p_ctx * (dp_ctx - delta) * scale).astype(MXU_DT)
            dq = (jnp.dot(ds_loc, kw, preferred_element_type=F32)
                  + jnp.dot(ds_ctx, kcv, preferred_element_type=F32))
            dq_ref[hs] = dq.reshape(GQA_GROUP, BLOCK, HEAD_DIM).astype(dq_ref.dtype)
            dk_ref[kv, win, :] += lax.dot_general(ds_loc, qs, _TN, preferred_element_type=F32)
            dv_ref[kv, win, :] += lax.dot_general(p_loc.astype(MXU_DT), dos, _TN, preferred_element_type=F32)
            dkc_ref[kv] += lax.dot_general(ds_ctx, qs, _TN, preferred_element_type=F32)
            dvc_ref[kv] += lax.dot_general(p_ctx.astype(MXU_DT), dos, _TN, preferred_element_type=F32)
            ds_ref[hs] += (-(p_sink * delta)).reshape(GQA_GROUP, BLOCK, 1)

    qs, kvs, cs, ls = _attn_specs(seq, n_ctx)
    sk = pl.BlockSpec((N_Q_HEADS, BLOCK, 1), lambda i: (0, 0, 0))
    return pl.pallas_call(
        body, name=name, grid=(seq // BLOCK,),
        in_specs=[pl.BlockSpec(memory_space=pltpu.SMEM), qs, kvs, kvs, cs, cs, qs, ls, qs],
        out_specs=[qs, kvs, kvs, cs, cs, sk],
        out_shape=[jax.ShapeDtypeStruct(q.shape, F32), jax.ShapeDtypeStruct(k.shape, F32),
                   jax.ShapeDtypeStruct(v.shape, F32), jax.ShapeDtypeStruct(kc.shape, F32),
                   jax.ShapeDtypeStruct(vc.shape, F32), jax.ShapeDtypeStruct((N_Q_HEADS, BLOCK, 1), F32)],
        compiler_params=_cp("arbitrary"),
    )(sink, q, k, v, kc, vc, o, lse, do)


def _halo_specs(ts, w, rows, col=0):
    per = ts // HALO
    last = rows // HALO - 1
    return [pl.BlockSpec((HALO, w), lambda i: (jnp.maximum(i * per - 1, 0), col)),
            pl.BlockSpec((ts, w), lambda i: (i, col)),
            pl.BlockSpec((HALO, w), lambda i: (jnp.minimum((i + 1) * per, last), col))]


def _glu(v):
    return v[:, :CONV_CH] * _sigmoid(v[:, CONV_CH:])


def _ln_stats(u):
    mu = jnp.mean(u, axis=-1, keepdims=True)
    xc = u - mu
    rstd = lax.rsqrt(jnp.mean(xc * xc, axis=-1, keepdims=True) + EPS)
    return xc * rstd, rstd


def _conv_fwd(glu, col, cw, cb, lw, lb, *, name):
    rows = glu.shape[0]
    ts = min(rows, 256)
    nt = rows // ts

    def body(gp_ref, g_ref, gn_ref, cw_ref, cb_ref, lw_ref, lb_ref, u3_ref, u1_ref, ext_ref):
        i = pl.program_id(0)
        ext_ref[0:HALO, :] = jnp.where(i > 0, _glu(gp_ref[...]), 0.0)
        ext_ref[HALO:HALO + ts, :] = _glu(g_ref[...])
        ext_ref[HALO + ts:, :] = jnp.where(i < nt - 1, _glu(gn_ref[...]), 0.0)
        acc = jnp.broadcast_to(cb_ref[...], (ts, CONV_CH))
        for j in range(CONV_K):
            acc = acc + cw_ref[j:j + 1, :] * ext_ref[HALO - CONV_PAD + j:HALO - CONV_PAD + j + ts, :]
        u1_ref[...] = acc
        xh, _ = _ln_stats(acc)
        u2 = xh * lw_ref[...] + lb_ref[...]
        u3_ref[...] = (u2 * _sigmoid(u2)).astype(u3_ref.dtype)

    full = lambda shape: pl.BlockSpec(shape, lambda i: (0,) * len(shape))
    return pl.pallas_call(
        body, name=name, grid=(nt,),
        in_specs=_halo_specs(ts, GLU_COLS, rows, col) + [full((CONV_K, CONV_CH))] + [_vec(CONV_CH)] * 3,
        out_specs=[_row(ts, CONV_CH), _row(ts, CONV_CH)],
        out_shape=[jax.ShapeDtypeStruct((rows, CONV_CH), MXU_DT), jax.ShapeDtypeStruct((rows, CONV_CH), F32)],
        scratch_shapes=[pltpu.VMEM((ts + 2 * HALO, CONV_CH), F32)],
        compiler_params=_cp("parallel"),
    )(glu, glu, glu, cw, cb, lw, lb)


def _conv_bwd(glu, col, u1, du3, dproj, cw, lw, lb, *, name):
    rows = glu.shape[0]
    ts = min(rows, 256)
    nt = rows // ts
    te = ts + 2 * HALO

    def du1_of(u1v, du3v, lw_v, lb_v):
        xh, rstd = _ln_stats(u1v)
        u2 = xh * lw_v + lb_v
        sg = _sigmoid(u2)
        du2 = du3v * (sg * (1.0 + u2 * (1.0 - sg)))
        dxh = du2 * lw_v
        du1 = rstd * (dxh - jnp.mean(dxh, axis=-1, keepdims=True)
                      - xh * jnp.mean(dxh * xh, axis=-1, keepdims=True))
        return du1, du2, xh

    def body(gp_ref, g_ref, gn_ref, up_ref, u_ref, un_ref, dp_ref, d_ref, dn_ref, cw_ref, lw_ref, lb_ref,
             _, dglu_ref, dcw_ref, dvec_ref, u0_ref, du1_ref):
        i = pl.program_id(0)
        lw_v, lb_v = lw_ref[...], lb_ref[...]

        @pl.when(i == 0)
        def _():
            dcw_ref[...] = jnp.zeros_like(dcw_ref)
            dvec_ref[...] = jnp.zeros_like(dvec_ref)

        gv = g_ref[...]
        u0_ref[0:HALO, :] = jnp.where(i > 0, _glu(gp_ref[...]), 0.0)
        u0_ref[HALO:HALO + ts, :] = _glu(gv)
        u0_ref[HALO + ts:, :] = jnp.where(i < nt - 1, _glu(gn_ref[...]), 0.0)
        d_prev, _, _ = du1_of(up_ref[...], dp_ref[...], lw_v, lb_v)
        d_main, du2, xh = du1_of(u_ref[...], d_ref[...], lw_v, lb_v)
        d_next, _, _ = du1_of(un_ref[...], dn_ref[...], lw_v, lb_v)
        du1_ref[0:HALO, :] = jnp.where(i > 0, d_prev, 0.0)
        du1_ref[HALO:HALO + ts, :] = d_main
        du1_ref[HALO + ts:, :] = jnp.where(i < nt - 1, d_next, 0.0)

        rid = lax.broadcasted_iota(jnp.int32, (8, CONV_CH), 0)
        dvec_ref[...] += (jnp.where(rid == 0, _colsum(d_main), 0.0)
                          + jnp.where(rid == 1, _colsum(du2 * xh), 0.0)
                          + jnp.where(rid == 2, _colsum(du2), 0.0))
        du0 = jnp.zeros((ts, CONV_CH), F32)
        for j in range(CONV_K):
            lo = HALO + CONV_PAD - j
            du0 = du0 + cw_ref[j:j + 1, :] * du1_ref[lo:lo + ts, :]
            prod = d_main * u0_ref[HALO - CONV_PAD + j:HALO - CONV_PAD + j + ts, :]
            dcw_ref[j] += jnp.sum(prod.reshape(ts // 8, 8, CONV_CH), axis=0)
        ga, sg = gv[:, :CONV_CH], _sigmoid(gv[:, CONV_CH:])
        dglu_ref[:, :CONV_CH] = (du0 * sg).astype(dglu_ref.dtype)
        dglu_ref[:, CONV_CH:] = (du0 * ga * sg * (1.0 - sg)).astype(dglu_ref.dtype)

    full = lambda shape: pl.BlockSpec(shape, lambda i: (0,) * len(shape))
    return pl.pallas_call(
        body, name=name, grid=(nt,),
        in_specs=(_halo_specs(ts, GLU_COLS, rows, col) + _halo_specs(ts, CONV_CH, rows)
                  + _halo_specs(ts, CONV_CH, rows) + [full((CONV_K, CONV_CH)), _vec(CONV_CH), _vec(CONV_CH)]
                  + [pl.BlockSpec(memory_space=pl.ANY)]),
        out_specs=[_row(ts, GLU_COLS, col), full((CONV_K, 8, CONV_CH)), full((8, CONV_CH))],
        out_shape=[jax.ShapeDtypeStruct(dproj.shape, dproj.dtype),
                   jax.ShapeDtypeStruct((CONV_K, 8, CONV_CH), F32), jax.ShapeDtypeStruct((8, CONV_CH), F32)],
        scratch_shapes=[pltpu.VMEM((te, CONV_CH), F32), pltpu.VMEM((te, CONV_CH), F32)],
        input_output_aliases={12: 0},
        compiler_params=_cp("arbitrary"),
    )(glu, glu, glu, u1, u1, u1, du3, du3, du3, cw, lw, lb, dproj)


FFN_CW = 1408
FFN_NJ = FFN_H // FFN_CW
FFN_LO = HALO - 8


def _ffn_halo_specs(ts, rows, col_of, inner_rows):
    per = ts // HALO
    last = rows // HALO - 1
    if inner_rows:
        return [pl.BlockSpec((HALO, FFN_CW), lambda j, i: (jnp.maximum(i * per - 1, 0), col_of(j))),
                pl.BlockSpec((ts, FFN_CW), lambda j, i: (i, col_of(j))),
                pl.BlockSpec((HALO, FFN_CW), lambda j, i: (jnp.minimum((i + 1) * per, last), col_of(j)))]
    return [pl.BlockSpec((HALO, FFN_CW), lambda i, j: (jnp.maximum(i * per - 1, 0), col_of(j))),
            pl.BlockSpec((ts, FFN_CW), lambda i, j: (i, col_of(j))),
            pl.BlockSpec((HALO, FFN_CW), lambda i, j: (jnp.minimum((i + 1) * per, last), col_of(j)))]


def _ffn_fill(x_ref, p_ref, m_ref, n_ref, sl, i, nt, ts):
    x_ref[0:HALO, :] = jnp.where(i > 0, p_ref[:, sl].astype(F32), 0.0)
    x_ref[HALO:HALO + ts, :] = m_ref[:, sl].astype(F32)
    x_ref[HALO + ts:, :] = jnp.where(i < nt - 1, n_ref[:, sl].astype(F32), 0.0)


def _conv3_at(x_ref, lo, n, w, b):
    return (w[0:1, :] * x_ref[lo - 1:lo - 1 + n, :] + w[1:2, :] * x_ref[lo:lo + n, :]
            + w[2:3, :] * x_ref[lo + 1:lo + 1 + n, :] + b)


def _ffn_act(up0, w3, b3, *, name):
    rows = up0.shape[0]
    ts = min(rows, 256)
    nt = rows // ts

    def body(gp, g, gn, vp, v, vn, wg, wv, bg, bv, a_ref, xg_ref, xv_ref):
        i = pl.program_id(0)
        for ch in range(FFN_CW // LANES):
            sl = slice(ch * LANES, (ch + 1) * LANES)
            _ffn_fill(xg_ref, gp, g, gn, sl, i, nt, ts)
            _ffn_fill(xv_ref, vp, v, vn, sl, i, nt, ts)
            gate = _conv3_at(xg_ref, HALO, ts, wg[:, sl], bg[:, sl])
            val = _conv3_at(xv_ref, HALO, ts, wv[:, sl], bv[:, sl])
            a_ref[:, sl] = (gate * _sigmoid(gate) * val).astype(a_ref.dtype)

    gcol, vcol = (lambda j: j), (lambda j: j + FFN_NJ)
    wspec = lambda col_of: pl.BlockSpec((FFN_K, FFN_CW), lambda i, j: (0, col_of(j)))
    bspec = lambda col_of: pl.BlockSpec((1, FFN_CW), lambda i, j: (0, col_of(j)))
    return pl.pallas_call(
        body, name=name, grid=(nt, FFN_NJ),
        in_specs=(_ffn_halo_specs(ts, rows, gcol, False) + _ffn_halo_specs(ts, rows, vcol, False)
                  + [wspec(gcol), wspec(vcol), bspec(gcol), bspec(vcol)]),
        out_specs=pl.BlockSpec((ts, FFN_CW), lambda i, j: (i, j)),
        out_shape=jax.ShapeDtypeStruct((rows, FFN_H), MXU_DT),
        scratch_shapes=[pltpu.VMEM((ts + 2 * HALO, LANES), F32)] * 2,
        compiler_params=_cp("parallel", "parallel"),
    )(up0, up0, up0, up0, up0, up0, w3, w3, b3, b3)


def _ffn_act_bwd(up0, da, w3, b3, *, name):
    rows = up0.shape[0]
    ts = min(rows, 256)
    nt = rows // ts

    ne = ts + 16
    off = HALO - FFN_LO

    def body(gp, g, gn, vp, v, vn, ap, a, an, wg, wv, bg, bv, dg_ref, dv_ref, sg_ref, sv_ref,
             xg_ref, xv_ref, xa_ref, eg_ref, ev_ref):
        i = pl.program_id(1)

        @pl.when(i == 0)
        def _():
            sg_ref[...] = jnp.zeros_like(sg_ref)
            sv_ref[...] = jnp.zeros_like(sv_ref)

        rid = lax.broadcasted_iota(jnp.int32, (8, LANES), 0)
        for ch in range(FFN_CW // LANES):
            sl = slice(ch * LANES, (ch + 1) * LANES)
            _ffn_fill(xg_ref, gp, g, gn, sl, i, nt, ts)
            _ffn_fill(xv_ref, vp, v, vn, sl, i, nt, ts)
            _ffn_fill(xa_ref, ap, a, an, sl, i, nt, ts)
            wgv, wvv = wg[:, sl], wv[:, sl]
            gate = _conv3_at(xg_ref, FFN_LO, ne, wgv, bg[:, sl])
            val = _conv3_at(xv_ref, FFN_LO, ne, wvv, bv[:, sl])
            da_e = xa_ref[FFN_LO:FFN_LO + ne, :]
            sgm = _sigmoid(gate)
            eg_ref[...] = da_e * val * (sgm * (1.0 + gate * (1.0 - sgm)))
            ev_ref[...] = da_e * (gate * sgm)
            for e_ref, x_ref, w, d_ref, s_ref in ((eg_ref, xg_ref, wgv, dg_ref, sg_ref),
                                                  (ev_ref, xv_ref, wvv, dv_ref, sv_ref)):
                dm = e_ref[off:off + ts, :]
                d0 = (w[0:1, :] * e_ref[off + 1:off + 1 + ts, :] + w[1:2, :] * dm
                      + w[2:3, :] * e_ref[off - 1:off - 1 + ts, :])
                d_ref[:, sl] = d0.astype(d_ref.dtype)
                s_ref[:, sl] += (jnp.where(rid == 0, _colsum(dm * x_ref[HALO - 1:HALO - 1 + ts, :]), 0.0)
                                 + jnp.where(rid == 1, _colsum(dm * x_ref[HALO:HALO + ts, :]), 0.0)
                                 + jnp.where(rid == 2, _colsum(dm * x_ref[HALO + 1:HALO + 1 + ts, :]), 0.0)
                                 + jnp.where(rid == 3, _colsum(dm), 0.0))

    gcol, vcol = (lambda j: j), (lambda j: j + FFN_NJ)
    wspec = lambda col_of: pl.BlockSpec((FFN_K, FFN_CW), lambda j, i: (0, col_of(j)))
    bspec = lambda col_of: pl.BlockSpec((1, FFN_CW), lambda j, i: (0, col_of(j)))
    ospec = pl.BlockSpec((ts, FFN_CW), lambda j, i: (i, j))
    sspec = pl.BlockSpec((8, FFN_CW), lambda j, i: (0, j))
    return pl.pallas_call(
        body, name=name, grid=(FFN_NJ, nt),
        in_specs=(_ffn_halo_specs(ts, rows, gcol, True) + _ffn_halo_specs(ts, rows, vcol, True)
                  + _ffn_halo_specs(ts, rows, gcol, True)
                  + [wspec(gcol), wspec(vcol), bspec(gcol), bspec(vcol)]),
        out_specs=[ospec, ospec, sspec, sspec],
        out_shape=[jax.ShapeDtypeStruct((rows, FFN_H), MXU_DT), jax.ShapeDtypeStruct((rows, FFN_H), MXU_DT),
                   jax.ShapeDtypeStruct((8, FFN_H), F32), jax.ShapeDtypeStruct((8, FFN_H), F32)],
        scratch_shapes=([pltpu.VMEM((ts + 2 * HALO, LANES), F32)] * 3 + [pltpu.VMEM((ne, LANES), F32)] * 2),
        compiler_params=_cp("parallel", "arbitrary"),
    )(up0, up0, up0, up0, up0, up0, da, da, da, w3, w3, b3, b3)


def _adam_math(w, g, m, v):
    m = ADAM_B1 * m + (1.0 - ADAM_B1) * g
    v = ADAM_B2 * v + (1.0 - ADAM_B2) * (g * g)
    m_hat = m / (1.0 - ADAM_B1 ** ADAM_STEP)
    v_hat = v / (1.0 - ADAM_B2 ** ADAM_STEP)
    delta = -ADAM_LR * (m_hat / (jnp.sqrt(v_hat) + ADAM_EPS) + ADAM_WD * w)
    return delta, m, v


def _adam(w, m, v, parts, *, name):
    rows, cols = w.shape
    nparts = parts.shape[0]
    tr = rows if rows <= 256 else _pick(rows, (256, 128, 64, 32, 16, 8))

    def body(w_ref, m_ref, v_ref, p_ref, g_ref, d_ref, nm_ref, nv_ref):
        g = p_ref[0].astype(F32)
        for p in range(1, nparts):
            g = g + p_ref[p].astype(F32)
        g_ref[...] = g
        d_ref[...], nm_ref[...], nv_ref[...] = _adam_math(w_ref[...], g, m_ref[...], v_ref[...])

    spec = _row(tr, cols)
    return pl.pallas_call(
        body, name=name, grid=(rows // tr,),
        in_specs=[spec, spec, spec, pl.BlockSpec((nparts, tr, cols), lambda i: (0, i, 0))],
        out_specs=[spec] * 4, out_shape=[jax.ShapeDtypeStruct((rows, cols), F32)] * 4,
        compiler_params=_cp("parallel"),
    )(w, m, v, parts)


def _adam_many(ws, ms, vs, gs, *, name):
    n = len(ws)

    def body(*refs):
        ins, outs = refs[:4 * n], refs[4 * n:]
        for k in range(n):
            delta, new_m, new_v = _adam_math(ins[k][...], ins[3 * n + k][...], ins[n + k][...], ins[2 * n + k][...])
            outs[k][...], outs[n + k][...], outs[2 * n + k][...] = delta, new_m, new_v

    vm = pl.BlockSpec(memory_space=pltpu.VMEM)
    res = pl.pallas_call(
        body, name=name, in_specs=[vm] * (4 * n), out_specs=[vm] * (3 * n),
        out_shape=[jax.ShapeDtypeStruct(w.shape, F32) for w in ws] * 3,
        compiler_params=pltpu.CompilerParams(vmem_limit_bytes=VMEM_LIMIT),
    )(*ws, *ms, *vs, *gs)
    return res[:n], res[n:2 * n], res[2 * n:]


def _sum_parts(parts, *, name):
    nparts, rows, cols = parts.shape
    tr = rows if rows <= 256 else _pick(rows, (256, 128, 64, 32, 16, 8))

    def body(p_ref, o_ref):
        g = p_ref[0].astype(F32)
        for p in range(1, nparts):
            g = g + p_ref[p].astype(F32)
        o_ref[...] = g

    return pl.pallas_call(
        body, name=name, grid=(rows // tr,),
        in_specs=[pl.BlockSpec((nparts, tr, cols), lambda i: (0, i, 0))], out_specs=_row(tr, cols),
        out_shape=jax.ShapeDtypeStruct((rows, cols), F32), compiler_params=_cp("parallel"),
    )(parts)


def _my_place():
    return lax.axis_index("x"), lax.axis_index("y"), lax.axis_index("c")


def _dev_index(p):
    return 4 * p[0] + 2 * p[1] + p[2]


def _all_gather(xs, *, hbm, name):
    n = len(xs)
    ms = [v.shape[0] for v in xs]

    def body(*refs):
        x_refs, o_refs = refs[:n], refs[n:2 * n]
        send_sems, recv_sems, local_sems = refs[2 * n:]
        x, y, c = _my_place()
        me, sib = (x, y, c), (x, y, 1 - c)
        chips = [(1 - x, y), (x, 1 - y), (1 - x, 1 - y)]

        def rows(a, p):
            return o_refs[a].at[pl.ds(pl.multiple_of(_dev_index(p) * ms[a], 8), ms[a])]

        def copy(a, k, block, to, src=None):
            return pltpu.make_async_remote_copy(
                src_ref=rows(a, block) if src is None else src, dst_ref=rows(a, block),
                send_sem=send_sems.at[a * 7 + k], recv_sem=recv_sems.at[a * 7 + k],
                device_id=to, device_id_type=MESH)

        mine = [pltpu.make_async_copy(x_refs[a], rows(a, me), local_sems.at[a]) for a in range(n)]
        for cp in mine:
            cp.start()
        first = []
        for a in range(n):
            first.append(copy(a, 0, me, sib, src=x_refs[a]))
            first += [copy(a, 1 + j, me, (*chip, c), src=x_refs[a]) for j, chip in enumerate(chips)]
        for cp in first:
            cp.start()
        passed = []
        for j, chip in enumerate(chips):
            for a in range(n):
                copy(a, 1 + j, (*chip, c), me).wait_recv()
                cp = copy(a, 4 + j, (*chip, c), sib)
                cp.start()
                passed.append(cp)
        for a in range(n):
            copy(a, 0, sib, me).wait_recv()
            for j, chip in enumerate(chips):
                copy(a, 4 + j, (*chip, 1 - c), me).wait_recv()
        for cp in first + passed:
            cp.wait_send()
        for cp in mine:
            cp.wait()

    space = pl.ANY if hbm else pltpu.VMEM
    return pl.pallas_call(
        body, name=name,
        out_shape=[jax.ShapeDtypeStruct((N_DEV * v.shape[0], v.shape[1]), v.dtype) for v in xs],
        in_specs=[pl.BlockSpec(memory_space=space)] * n, out_specs=[pl.BlockSpec(memory_space=space)] * n,
        scratch_shapes=[pltpu.SemaphoreType.DMA((7 * n,)), pltpu.SemaphoreType.DMA((7 * n,)),
                        pltpu.SemaphoreType.DMA((n,))],
        compiler_params=pltpu.CompilerParams(vmem_limit_bytes=VMEM_LIMIT),
    )(*xs)


def _exchange_blocks(gs, *, name):
    n = len(gs)
    rs = [v.shape[0] // N_DEV for v in gs]
    flips = [(bx, by, bc) for bx in (0, 1) for by in (0, 1) for bc in (0, 1)][1:]

    def body(*refs):
        g_refs, o_refs = refs[:n], refs[n:2 * n]
        send_sems, recv_sems, local_sems = refs[2 * n:]
        x, y, c = _my_place()
        me = (x, y, c)

        def block(ref, a, p):
            return ref.at[pl.ds(_dev_index(p) * rs[a], rs[a])]

        def peer(f):
            return (1 - x if f[0] else x, 1 - y if f[1] else y, 1 - c if f[2] else c)

        def copy(a, k, to):
            return pltpu.make_async_remote_copy(
                src_ref=block(g_refs[a], a, to), dst_ref=block(o_refs[a], a, me),
                send_sem=send_sems.at[a * 7 + k], recv_sem=recv_sems.at[a * 7 + k],
                device_id=to, device_id_type=MESH)

        def arrival(a, k, frm):
            return pltpu.make_async_remote_copy(
                src_ref=block(g_refs[a], a, frm), dst_ref=block(o_refs[a], a, frm),
                send_sem=send_sems.at[a * 7 + k], recv_sem=recv_sems.at[a * 7 + k],
                device_id=frm, device_id_type=MESH)

        mine = [pltpu.make_async_copy(block(g_refs[a], a, me), block(o_refs[a], a, me), local_sems.at[a])
                for a in range(n)]
        for cp in mine:
            cp.start()
        sends = [copy(a, k, peer(f)) for a in range(n) for k, f in enumerate(flips)]
        for cp in sends:
            cp.start()
        for a in range(n):
            for k, f in enumerate(flips):
                arrival(a, k, peer(f)).wait_recv()
        for cp in sends:
            cp.wait_send()
        for cp in mine:
            cp.wait()

    return pl.pallas_call(
        body, name=name,
        out_shape=[jax.ShapeDtypeStruct(v.shape, v.dtype) for v in gs],
        in_specs=[pl.BlockSpec(memory_space=pl.ANY)] * n, out_specs=[pl.BlockSpec(memory_space=pl.ANY)] * n,
        scratch_shapes=[pltpu.SemaphoreType.DMA((7 * n,)), pltpu.SemaphoreType.DMA((7 * n,)),
                        pltpu.SemaphoreType.DMA((n,))],
        compiler_params=pltpu.CompilerParams(vmem_limit_bytes=VMEM_LIMIT),
    )(*gs)


def _rope_tables(seq):
    t = jnp.arange(seq)
    row, col = t // GRID_W, t % GRID_W
    half = HEAD_DIM // 2
    inv = ROPE_BASE ** (-jnp.arange(0, half, 2, dtype=F32) / half)
    ang_r = row.astype(F32)[:, None] * inv
    ang_c = col.astype(F32)[:, None] * inv
    ang = jnp.concatenate([ang_r, ang_r, ang_c, ang_c], axis=-1)
    return jnp.tile(jnp.cos(ang), (1, 2)), jnp.tile(jnp.sin(ang), (1, 2))


def _to_heads(v, nh):
    return v.reshape(v.shape[0], nh, HEAD_DIM).transpose(1, 0, 2)


def _from_heads(v):
    return v.transpose(1, 0, 2).reshape(v.shape[1], v.shape[0] * HEAD_DIM)


def _pad_rows(v):
    return jnp.pad(v, ((0, 0), (BLOCK, BLOCK), (0, 0)))


def _pack(vs):
    flat = jnp.concatenate([v.reshape(-1).astype(F32) for v in vs])
    total = -(-flat.shape[0] // (8 * LANES)) * (8 * LANES)
    return jnp.pad(flat, (0, total - flat.shape[0])).reshape(-1, LANES)


def _unpack(packed, like):
    flat, out, off = packed.reshape(-1), [], 0
    for v in like:
        size = math.prod(v.shape)
        out.append(flat[off:off + size].reshape(v.shape))
        off += size
    return out


def _silu(v):
    return v * jax.nn.sigmoid(v)


def kernel(x, c, ctx, c_ctx, w_mod, b_mod, norm_mix_w, w_in, q_norm_w, k_norm_w, sink_logit, conv_w, conv_b, conv_norm_w, conv_norm_b, w_out, norm_ffn_w, w_up, ffn_conv_w, ffn_conv_b, w_down, loss_target, m_c_ctx, m_w_mod, m_b_mod, m_norm_mix_w, m_w_in, m_q_norm_w, m_k_norm_w, m_sink_logit, m_conv_w, m_conv_b, m_conv_norm_w, m_conv_norm_b, m_w_out, m_norm_ffn_w, m_w_up, m_ffn_conv_w, m_ffn_conv_b, m_w_down, v_c_ctx, v_w_mod, v_b_mod, v_norm_mix_w, v_w_in, v_q_norm_w, v_k_norm_w, v_sink_logit, v_conv_w, v_conv_b, v_conv_norm_w, v_conv_norm_b, v_w_out, v_norm_ffn_w, v_w_up, v_ffn_conv_w, v_ffn_conv_b, v_w_down):
    d = D_MODEL
    seq, n_ctx = x.shape[1], ctx.shape[1]
    me = _dev_index(_my_place())
    xs, ctxs, tgt = x[0], ctx[0], loss_target[0]

    w_in_t, w_up_t, w_out_f, w_down_f = _all_gather(
        [w_in[0].T.astype(MXU_DT), w_up[0].T.astype(MXU_DT), w_out[0].astype(MXU_DT), w_down[0].astype(MXU_DT)],
        hbm=True, name="gather_weights")
    w_in_p = jnp.concatenate([w_in_t[:QKV_COLS], jnp.zeros((GLU_OFF - QKV_COLS, d), MXU_DT), w_in_t[QKV_COLS:]])
    small = _pack([c[0], conv_w[0], ffn_conv_w[0]])
    small_all = _all_gather([small], hbm=False, name="gather_small")[0].reshape(N_DEV, -1)
    n_cw, n_fw = conv_w[0].size, ffn_conv_w[0].size
    c_all = small_all[:, :d]
    cw_all = small_all[:, d:d + n_cw].reshape(N_DEV, CONV_K, -1)
    fw_all = small_all[:, d + n_cw:d + n_cw + n_fw].reshape(N_DEV, FFN_K, -1)
    conv_w_f = cw_all.transpose(1, 0, 2).reshape(CONV_K, CONV_CH)
    ffn_w_f = fw_all.transpose(1, 0, 2).reshape(FFN_K, 2 * FFN_H)

    mcols = w_mod.shape[2]
    act = jnp.zeros((16, d), F32).at[:N_DEV].set(_silu(c_all)).at[N_DEV].set(_silu(c_ctx))
    mod_part = _mm(act, w_mod[0], 16, mcols, d, name="mod_fwd")
    mod_all = _all_gather([mod_part], hbm=False, name="gather_mod")[0]
    mod_all = mod_all.reshape(N_DEV, 16, mcols).transpose(1, 0, 2).reshape(16, 6 * d) + b_mod
    mod = lax.dynamic_slice_in_dim(mod_all, me, 1, axis=0)
    sh1, sc1, g1, sh2, sc2, g2 = [mod[:, k * d:(k + 1) * d] for k in range(6)]
    sh1c, sc1c = mod_all[N_DEV:N_DEV + 1, :d], mod_all[N_DEV:N_DEV + 1, d:2 * d]

    cos, sin = _rope_tables(seq)
    ones_c, zeros_c = jnp.ones((n_ctx, LANES), F32), jnp.zeros((n_ctx, LANES), F32)
    qk_w = jnp.concatenate([jnp.tile(q_norm_w, (1, N_Q_HEADS)), jnp.tile(k_norm_w, (1, N_KV_HEADS))], axis=1)
    kc_w = jnp.tile(k_norm_w, (1, N_KV_HEADS))

    h = _prenorm(xs, norm_mix_w, sc1, sh1, name="prenorm_mix")
    hc = _prenorm(ctxs, norm_mix_w, sc1c, sh1c, name="prenorm_ctx")
    proj = _mm(h, w_in_p, seq, IN_PAD, d, tb=True, name="proj_in")
    kv_ctx = _mm(hc, w_in_p, n_ctx, 2 * KV_COLS, d, tb=True, n0=Q_COLS, name="proj_ctx")
    qk_r = _qk_prep(proj, QK_COLS, cos, sin, qk_w, name="qk_prep")
    kc_n = _qk_prep(kv_ctx, KV_COLS, ones_c, zeros_c, kc_w, name="k_ctx_prep")
    q_h = _to_heads(qk_r[:, :Q_COLS], N_Q_HEADS)
    k_h = _pad_rows(_to_heads(qk_r[:, Q_COLS:], N_KV_HEADS))
    v_h = _pad_rows(_to_heads(proj[:, QK_COLS:QKV_COLS].astype(MXU_DT), N_KV_HEADS))
    kc_h = _to_heads(kc_n, N_KV_HEADS)
    vc_h = _to_heads(kv_ctx[:, KV_COLS:].astype(MXU_DT), N_KV_HEADS)
    o_h, lse = _attn_fwd(sink_logit, q_h, k_h, v_h, kc_h, vc_h, name="attn_fwd")
    attn_o = _from_heads(o_h)
    u3, u1 = _conv_fwd(proj, GLU_OFF // GLU_COLS, conv_w_f, conv_b, conv_norm_w, conv_norm_b, name="conv_fwd")
    mix = _mm(attn_o, w_out_f, seq, d, Q_COLS, name="out_attn")
    mix = _mm(u3, w_out_f, seq, d, CONV_CH, k0=Q_COLS, add=mix, name="out_conv")

    x1, h2 = _resid_prenorm(xs, mix, g1, norm_ffn_w, sc2, sh2, name="prenorm_ffn")
    up0 = _mm(h2, w_up_t, seq, 2 * FFN_H, d, tb=True, out_dtype=MXU_DT, name="ffn_up")
    act_a = _ffn_act(up0, ffn_w_f, ffn_conv_b, name="ffn_act")
    ffn = _mm(act_a, w_down_f, seq, d, FFN_H, name="ffn_down")
    loss_p, dy, dffn, dg2 = _loss_head(x1, ffn, g2, tgt, name="loss_head")

    da = _mm(dffn, w_down_f, seq, FFN_H, d, tb=True, out_dtype=MXU_DT, name="ffn_down_dx")
    gw_down = _mm(act_a, dffn, FFN_H, d, seq, ta=True, out_dtype=MXU_DT, name="ffn_down_dw")
    dgate0, dval0, s_gate, s_val = _ffn_act_bwd(up0, da, ffn_w_f, ffn_conv_b, name="ffn_act_bwd")
    dh2 = _mm(dgate0, w_up_t, seq, d, FFN_H, name="ffn_up_dx_gate")
    dh2 = _mm(dval0, w_up_t, seq, d, FFN_H, k0=FFN_H, add=dh2, name="ffn_up_dx_val")
    gw_up_t = jnp.concatenate([
        _mm(dgate0, h2, FFN_H, d, seq, ta=True, out_dtype=MXU_DT, name="ffn_up_dw_gate"),
        _mm(dval0, h2, FFN_H, d, seq, ta=True, out_dtype=MXU_DT, name="ffn_up_dw_val")], axis=0)
    dx1, dmix, dsh2, dsc2, dnw2, dg1 = _norm_bwd(
        dh2, x1, norm_ffn_w, sc2, res=dy, gate=(mix, g1), name="prenorm_ffn_bwd")

    dattn = _mm(dmix, w_out_f, seq, Q_COLS, d, tb=True, out_dtype=MXU_DT, name="out_dx_attn")
    du3 = _mm(dmix, w_out_f, seq, CONV_CH, d, tb=True, n0=Q_COLS, name="out_dx_conv")
    gw_out = jnp.concatenate([
        _mm(attn_o, dmix, Q_COLS, d, seq, ta=True, out_dtype=MXU_DT, name="out_dw_attn"),
        _mm(u3, dmix, CONV_CH, d, seq, ta=True, out_dtype=MXU_DT, name="out_dw_conv")], axis=0)
    dq_h, dk_h, dv_h, dkc_h, dvc_h, dsink_rows = _attn_bwd(
        sink_logit, q_h, k_h, v_h, kc_h, vc_h, o_h, lse, _to_heads(dattn, N_Q_HEADS), name="attn_bwd")
    dqk_r = jnp.concatenate([_from_heads(dq_h), _from_heads(dk_h[:, BLOCK:BLOCK + seq])], axis=1)
    dproj, dqk_w = _qk_prep_bwd(proj, dqk_r, QK_COLS, cos, sin, qk_w,
                                tail=_from_heads(dv_h[:, BLOCK:BLOCK + seq]), name="qk_prep_bwd")
    dproj, dcw8, dvec = _conv_bwd(proj, GLU_OFF // GLU_COLS, u1, du3, dproj, conv_w_f, conv_norm_w, conv_norm_b,
                                  name="conv_bwd")
    dkc, dkc_w = _qk_prep_bwd(kv_ctx, _from_heads(dkc_h), KV_COLS, ones_c, zeros_c, kc_w, name="k_ctx_prep_bwd")
    dkv_ctx = jnp.concatenate([dkc, _from_heads(dvc_h).astype(MXU_DT)], axis=1)
    dh = _mm(dproj, w_in_p, seq, d, IN_PAD, name="proj_dx")
    dhc = _mm(dkv_ctx, w_in_p, n_ctx, d, 2 * KV_COLS, k0=Q_COLS, name="proj_dx_ctx")
    gw_p = _mm(dproj, h, IN_PAD, d, seq, ta=True, name="proj_dw")
    gw_ctx = _mm(dkv_ctx, hc, 2 * KV_COLS, d, n_ctx, ta=True, name="proj_dw_ctx")
    gw_in_t = jnp.concatenate([gw_p[:Q_COLS], gw_p[Q_COLS:QKV_COLS] + gw_ctx, gw_p[GLU_OFF:]], axis=0).astype(MXU_DT)
    grad_x, dsh1, dsc1, dnw1 = _norm_bwd(dh, xs, norm_mix_w, sc1, res=dx1, name="prenorm_mix_bwd")
    dsh1c, dsc1c, dnw1c = _norm_bwd(dhc, ctxs, norm_mix_w, sc1c, want_dx=False, name="prenorm_ctx_bwd")

    rx_in, rx_up, rx_out, rx_down = _exchange_blocks([gw_in_t, gw_up_t, gw_out, gw_down], name="exchange_grads")
    dmod = jnp.concatenate([dsh1, dsc1, dg1, dsh2, dsc2, dg2], axis=1)
    dmod_ctx = jnp.concatenate([dsh1c, dsc1c], axis=1)
    d_qn = dqk_w[0, :Q_COLS].reshape(N_Q_HEADS, HEAD_DIM).sum(0)
    d_kn = (dqk_w[0, Q_COLS:].reshape(N_KV_HEADS, HEAD_DIM).sum(0)
            + dkc_w[0].reshape(N_KV_HEADS, HEAD_DIM).sum(0))
    d_ffn_w = jnp.concatenate([s_gate[:FFN_K], s_val[:FFN_K]], axis=1)
    d_ffn_b = jnp.concatenate([s_gate[FFN_K], s_val[FFN_K]])
    summed_like = [(dnw1 + dnw1c), d_qn[None], d_kn[None], dsink_rows.sum((1, 2))[None], dvec[0:1], dvec[1:2],
                   dvec[2:3], dnw2, d_ffn_b[None], dcw8.sum(1), d_ffn_w, loss_p[0:1, 0:1]]
    pack = _pack([dmod, dmod_ctx] + summed_like)
    pack_all = _all_gather([pack], hbm=False, name="gather_small_grads")[0]
    pack_all = pack_all.reshape(N_DEV, pack.shape[0], LANES)
    tot = _sum_parts(pack_all, name="sum_small_grads")
    (dmod_sum, dmc_sum, g_nmix, g_qn, g_kn, g_sink, g_cb, g_lw, g_lb, g_nffn, g_fb, g_cw_f, g_fw_f,
     loss_sum) = _unpack(tot, [dmod, dmod_ctx] + summed_like)
    loss = loss_sum[0, 0]
    dmod_all = pack_all.reshape(N_DEV, -1)[:, :6 * d]
    g_b_mod = dmod_sum.at[:, :2 * d].add(dmc_sum)

    lo = me * mcols
    dm_rows = jnp.zeros((16, 6 * d), F32).at[:N_DEV].set(dmod_all).at[N_DEV, :2 * d].set(dmc_sum[0])
    dm_mine = lax.dynamic_slice_in_dim(dm_rows, lo, mcols, axis=1)
    parts_mod = _mm(act, dm_mine, d, mcols, 16, ta=True, name="mod_dw")[None]
    dact_part = _mm(dm_mine[N_DEV:N_DEV + 8], w_mod[0], 8, d, mcols, tb=True, name="mod_dx_ctx")
    dact_all = _all_gather([dact_part], hbm=False, name="gather_c_ctx_grad")[0].reshape(N_DEV, 8, d)
    dact = _sum_parts(dact_all, name="sum_c_ctx_grad")[0]
    sg = jax.nn.sigmoid(c_ctx)
    g_c_ctx = dact * (sg * (1.0 + c_ctx * (1.0 - sg)))

    def stacked(rx):
        return rx.reshape(N_DEV, rx.shape[0] // N_DEV, rx.shape[1])

    g_w_in = _sum_parts(stacked(rx_in), name="sum_w_in").T[None]
    g_w_up = _sum_parts(stacked(rx_up), name="sum_w_up").T[None]
    big = {}
    big["w_in"] = _adam(w_in[0], m_w_in[0], v_w_in[0], g_w_in, name="adam_w_in")
    big["w_up"] = _adam(w_up[0], m_w_up[0], v_w_up[0], g_w_up, name="adam_w_up")
    big["w_out"] = _adam(w_out[0], m_w_out[0], v_w_out[0], stacked(rx_out), name="adam_w_out")
    big["w_down"] = _adam(w_down[0], m_w_down[0], v_w_down[0], stacked(rx_down), name="adam_w_down")
    big["w_mod"] = _adam(w_mod[0], m_w_mod[0], v_w_mod[0], parts_mod, name="adam_w_mod")

    ccols, fcols = conv_w.shape[2], ffn_conv_w.shape[2]
    g_conv_w = lax.dynamic_slice_in_dim(g_cw_f, me * ccols, ccols, axis=1)[None]
    g_ffn_w = lax.dynamic_slice_in_dim(g_fw_f, me * fcols, fcols, axis=1)[None]
    names = ["c_ctx", "b_mod", "norm_mix_w", "q_norm_w", "k_norm_w", "sink_logit", "conv_w", "conv_b",
             "conv_norm_w", "conv_norm_b", "norm_ffn_w", "ffn_conv_w", "ffn_conv_b"]
    ws = [c_ctx, b_mod, norm_mix_w, q_norm_w, k_norm_w, sink_logit, conv_w, conv_b, conv_norm_w, conv_norm_b,
          norm_ffn_w, ffn_conv_w, ffn_conv_b]
    msm = [m_c_ctx, m_b_mod, m_norm_mix_w, m_q_norm_w, m_k_norm_w, m_sink_logit, m_conv_w, m_conv_b,
           m_conv_norm_w, m_conv_norm_b, m_norm_ffn_w, m_ffn_conv_w, m_ffn_conv_b]
    vsm = [v_c_ctx, v_b_mod, v_norm_mix_w, v_q_norm_w, v_k_norm_w, v_sink_logit, v_conv_w, v_conv_b,
           v_conv_norm_w, v_conv_norm_b, v_norm_ffn_w, v_ffn_conv_w, v_ffn_conv_b]
    gsm = [g_c_ctx, g_b_mod, g_nmix, g_qn, g_kn, g_sink, g_conv_w, g_cb, g_lw, g_lb, g_nffn, g_ffn_w, g_fb]
    deltas, new_ms, new_vs = _adam_many(ws, msm, vsm, gsm, name="adam_small")
    sm = {nm: vals for nm, vals in zip(names, zip(gsm, deltas, new_ms, new_vs))}

    def out4(nm):
        if nm in sm:
            return sm[nm]
        return tuple(t[None] for t in big[nm])

    order = ["c_ctx", "w_mod", "b_mod", "norm_mix_w", "w_in", "q_norm_w", "k_norm_w", "sink_logit", "conv_w",
             "conv_b", "conv_norm_w", "conv_norm_b", "w_out", "norm_ffn_w", "w_up", "ffn_conv_w", "ffn_conv_b",
             "w_down"]
    quads = [out4(nm) for nm in order]
    return (loss, grad_x[None], *[q[0] for q in quads], *[q[1] for q in quads],
            *[q[2] for q in quads], *[q[3] for q in quads])
```

```python
import math
from typing import Callable, NamedTuple

import jax
import jax.numpy as jnp
from jax import lax
from jax.experimental import pallas as pl
from jax.experimental.pallas import tpu as pltpu

F32 = jnp.float32
MXU_DT = jnp.bfloat16

D_MODEL = 1024
GRID_W = 64
HEAD_DIM = 64
N_Q_HEADS = 8
N_KV_HEADS = 2
GQA_GROUP = 4
WINDOW = 128
BLOCK = 128
Q_COLS = 512
KV_COLS = 128
QK_COLS = Q_COLS + KV_COLS
QKV_COLS = Q_COLS + 2 * KV_COLS
CONV_CH = 512
GLU_COLS = 2 * CONV_CH
IN_COLS = QKV_COLS + GLU_COLS
CONV_K = 31
CONV_PAD = 15
FFN_H = 2816
FFN_K = 3
ROPE_BASE = 10000.0
EPS = 1e-6
NEG_INF = -1e30
N_DEV = 8
HALO = 16
LANES = 128

ADAM_LR = 0.001
ADAM_B1 = 0.9
ADAM_B2 = 0.999
ADAM_EPS = 1e-08
ADAM_WD = 0.01
ADAM_STEP = 10

MESH_AXES = ("x", "y", "c")
MESH = pl.DeviceIdType.MESH
VMEM_LIMIT = 56 << 20
MM_VMEM_BUDGET = 44 << 20
GLU_OFF = 1024
IN_PAD = GLU_OFF + GLU_COLS


def _cp(*dims):
    return pltpu.CompilerParams(dimension_semantics=dims or None, vmem_limit_bytes=VMEM_LIMIT)


def _pick(n, prefs, also=0):
    for p in prefs:
        if n % p == 0 and also % p == 0:
            return p
    raise ValueError((n, prefs, also))


def _row(ts, w, col=0):
    return pl.BlockSpec((ts, w), lambda i: (i, col))


def _vec(w):
    return pl.BlockSpec((1, w), lambda i: (0, 0))


def _colsum(v):
    return jnp.sum(v, axis=0, keepdims=True)


def _sigmoid(v):
    return 0.5 * jnp.tanh(0.5 * v) + 0.5


def _mm(a, b, m, n, k, *, ta=False, tb=False, n0=0, k0=0, add=None, out_dtype=F32, into=None, m0=0,
        out_rows=None, name):
    has_add = add is not None
    if into is not None:
        out_dtype = into.dtype
    sa, sb, so = a.dtype.itemsize, b.dtype.itemsize, jnp.dtype(out_dtype).itemsize
    sadd = add.dtype.itemsize if has_add else 0

    def fits(tm, tn):
        return 2 * (k * (tm * sa + tn * sb) + tm * tn * (so + sadd)) <= MM_VMEM_BUDGET

    tms = [m] if m <= 1024 else [t for t in (1024, 1408, 768, 512, 256, 128) if m % t == 0]
    tns = [t for t in ((1024, 512, 256, 128) if ta else (512, 1408, 256, 128)) if n % t == 0 and n0 % t == 0]
    tm, tn = next((tm, tn) for tm in tms for tn in tns if fits(tm, tn))
    assert k0 % k == 0 or k0 == 0, (k0, k)
    a_spec = (pl.BlockSpec((k, tm), lambda i, j: (0, i)) if ta else pl.BlockSpec((tm, k), lambda i, j: (i, 0)))
    nb0 = n0 // tn
    if tb:
        assert k0 == 0
        b_spec = pl.BlockSpec((tn, k), lambda i, j: (j + nb0, 0))
    else:
        assert k0 % k == 0, (k0, k)
        kb0 = k0 // k
        b_spec = pl.BlockSpec((k, tn), lambda i, j: (kb0, j + nb0))
    assert m0 % tm == 0, (m0, tm)
    mb0 = m0 // tm
    o_spec = pl.BlockSpec((tm, tn), lambda i, j: (i + mb0, j))
    dims = (((0 if ta else 1,), (1 if tb else 0,)), ((), ()))

    def body(*refs):
        a_ref, b_ref, o_ref = refs[0], refs[1], refs[-1]
        res = lax.dot_general(a_ref[...].astype(MXU_DT), b_ref[...].astype(MXU_DT), dims,
                              preferred_element_type=F32)
        if has_add:
            res = res + refs[2][...].astype(F32)
        o_ref[...] = res.astype(o_ref.dtype)

    ins = [a, b] + ([add] if has_add else []) + ([into] if into is not None else [])
    specs = ([a_spec, b_spec] + ([pl.BlockSpec((tm, tn), lambda i, j: (i, j))] if has_add else [])
             + ([pl.BlockSpec(memory_space=pl.ANY)] if into is not None else []))
    out_shape = (jax.ShapeDtypeStruct(into.shape, into.dtype) if into is not None
                 else jax.ShapeDtypeStruct((out_rows or m, n), out_dtype))
    return pl.pallas_call(
        body, name=name, grid=(m // tm, n // tn),
        in_specs=specs, out_specs=o_spec, out_shape=out_shape,
        input_output_aliases={len(ins) - 1: 0} if into is not None else {},
        compiler_params=_cp("parallel", "parallel"),
    )(*ins)


def _rms_stats(xv):
    r = lax.rsqrt(jnp.mean(xv * xv, axis=-1, keepdims=True) + EPS)
    return r, xv * r


def _prenorm(x, nw, sc, sh, *, name):
    rows, d = x.shape
    ts = min(rows, 256)

    def body(x_ref, nw_ref, sc_ref, sh_ref, h_ref):
        _, xn = _rms_stats(x_ref[...])
        h_ref[...] = ((xn * nw_ref[...]) * (1.0 + sc_ref[...]) + sh_ref[...]).astype(h_ref.dtype)

    return pl.pallas_call(
        body, name=name, grid=(rows // ts,),
        in_specs=[_row(ts, d), _vec(d), _vec(d), _vec(d)], out_specs=_row(ts, d),
        out_shape=jax.ShapeDtypeStruct((rows, d), MXU_DT), compiler_params=_cp("parallel"),
    )(x, nw, sc, sh)


def _resid_prenorm(x, mix, g1, nw, sc, sh, *, name):
    rows, d = x.shape
    ts = min(rows, 256)

    def body(x_ref, mix_ref, g_ref, nw_ref, sc_ref, sh_ref, x1_ref, h_ref):
        x1 = x_ref[...] + g_ref[...] * mix_ref[...]
        x1_ref[...] = x1
        _, xn = _rms_stats(x1)
        h_ref[...] = ((xn * nw_ref[...]) * (1.0 + sc_ref[...]) + sh_ref[...]).astype(h_ref.dtype)

    return pl.pallas_call(
        body, name=name, grid=(rows // ts,),
        in_specs=[_row(ts, d), _row(ts, d), _vec(d), _vec(d), _vec(d), _vec(d)],
        out_specs=[_row(ts, d), _row(ts, d)],
        out_shape=[jax.ShapeDtypeStruct((rows, d), F32), jax.ShapeDtypeStruct((rows, d), MXU_DT)],
        compiler_params=_cp("parallel"),
    )(x, mix, g1, nw, sc, sh)


def _loss_head(x1, ffn, g2, target, *, name):
    rows, d = x1.shape
    ts = min(rows, 256)

    def body(x1_ref, f_ref, g_ref, t_ref, loss_ref, dy_ref, dffn_ref, dg_ref):
        i = pl.program_id(0)
        f = f_ref[...]
        e = x1_ref[...] + g_ref[...] * f - t_ref[...]
        per_tok = jnp.mean(e * e, axis=-1, keepdims=True)
        part = 0.5 * jnp.sum(per_tok, axis=0, keepdims=True)
        dy = e * (1.0 / d)
        dy_ref[...] = dy
        dffn_ref[...] = (dy * g_ref[...]).astype(dffn_ref.dtype)

        @pl.when(i == 0)
        def _():
            loss_ref[...] = jnp.zeros_like(loss_ref)
            dg_ref[...] = jnp.zeros_like(dg_ref)

        loss_ref[...] += jnp.broadcast_to(part, loss_ref.shape)
        dg_ref[...] += _colsum(dy * f)

    return pl.pallas_call(
        body, name=name, grid=(rows // ts,),
        in_specs=[_row(ts, d), _row(ts, d), _vec(d), _row(ts, d)],
        out_specs=[pl.BlockSpec((8, LANES), lambda i: (0, 0)), _row(ts, d), _row(ts, d), _vec(d)],
        out_shape=[jax.ShapeDtypeStruct((8, LANES), F32), jax.ShapeDtypeStruct((rows, d), F32),
                   jax.ShapeDtypeStruct((rows, d), MXU_DT), jax.ShapeDtypeStruct((1, d), F32)],
        compiler_params=_cp("arbitrary"),
    )(x1, ffn, g2, target)


def _norm_bwd(dh, xin, nw, sc, *, res=None, gate=None, want_dx=True, carry=None, name):
    rows, d = xin.shape
    ts = min(rows, 256)
    has_res, has_gate = res is not None, gate is not None

    def body(*refs):
        it = iter(refs)
        dh_ref, x_ref, nw_ref, sc_ref = next(it), next(it), next(it), next(it)
        res_ref = next(it) if has_res else None
        gated_ref, g_ref = (next(it), next(it)) if has_gate else (None, None)
        dx_ref = next(it) if want_dx else None
        dgx_ref = next(it) if has_gate else None
        dsh_ref, dsc_ref, dnw_ref = next(it), next(it), next(it)
        dg_ref = next(it) if has_gate else None
        i = pl.program_id(0)
        dhv = dh_ref[...]
        r, xn = _rms_stats(x_ref[...])
        dn = dhv * (1.0 + sc_ref[...])

        @pl.when(i == 0)
        def _():
            dsh_ref[...] = jnp.zeros_like(dsh_ref)
            dsc_ref[...] = jnp.zeros_like(dsc_ref)
            dnw_ref[...] = jnp.zeros_like(dnw_ref)
            if has_gate:
                dg_ref[...] = jnp.zeros_like(dg_ref)

        dsh_ref[...] += _colsum(dhv)
        dsc_ref[...] += _colsum(dhv * (xn * nw_ref[...]))
        dnw_ref[...] += _colsum(dn * xn)
        if want_dx:
            dxn = dn * nw_ref[...]
            dx = r * (dxn - xn * jnp.mean(dxn * xn, axis=-1, keepdims=True))
            if has_res:
                dx = dx + res_ref[...]
            dx_ref[...] = dx
            if has_gate:
                dgx_ref[...] = (dx * g_ref[...]).astype(dgx_ref.dtype)
                dg_ref[...] += _colsum(dx * gated_ref[...])

    ins = [dh, xin, nw, sc] + ([res] if has_res else []) + (list(gate) if has_gate else [])
    in_specs = ([_row(ts, d), _row(ts, d), _vec(d), _vec(d)] + ([_row(ts, d)] if has_res else [])
                + ([_row(ts, d), _vec(d)] if has_gate else []))
    out_specs, out_shape = [], []
    if want_dx:
        out_specs.append(_row(ts, d)); out_shape.append(jax.ShapeDtypeStruct((rows, d), F32))
    if has_gate:
        out_specs.append(_row(ts, d)); out_shape.append(jax.ShapeDtypeStruct((rows, d), MXU_DT))
    for _ in range(3 + int(has_gate)):
        out_specs.append(_vec(d)); out_shape.append(jax.ShapeDtypeStruct((1, d), F32))
    return _grid_call(body, ins, carry, name=name, grid=(rows // ts,), in_specs=in_specs, out_specs=out_specs,
                      out_shape=out_shape, dims=("arbitrary",))


def _group_sum(v, g):
    hi = v.astype(MXU_DT)
    lo = (v - hi.astype(F32)).astype(MXU_DT)
    return (jnp.dot(hi, g, preferred_element_type=F32) + jnp.dot(lo, g, preferred_element_type=F32))


def _rot(v):
    lane = lax.broadcasted_iota(jnp.int32, v.shape, 1)
    first = (lane & 31) < 16
    return jnp.where(first, -pltpu.roll(v, LANES - 16, 1), pltpu.roll(v, 16, 1))


def _head_group_matrix():
    r = jnp.arange(LANES) // HEAD_DIM
    return (r[:, None] == r[None, :]).astype(MXU_DT)


def _qk_prep(xin, width, cos, sin, w, *, name):
    rows = xin.shape[0]
    ts = min(rows, 256)
    nch = width // LANES

    def body(x_ref, cos_ref, sin_ref, w_ref, g_ref, o_ref):
        cs, sn, g = cos_ref[...], sin_ref[...], g_ref[...]
        for ch in range(nch):
            sl = slice(ch * LANES, (ch + 1) * LANES)
            xv = x_ref[:, sl]
            r = lax.rsqrt(_group_sum(xv * xv, g) * (1.0 / HEAD_DIM) + EPS)
            yw = (xv * r) * w_ref[:, sl]
            o_ref[:, sl] = (yw * cs + _rot(yw) * sn).astype(o_ref.dtype)

    return pl.pallas_call(
        body, name=name, grid=(rows // ts,),
        in_specs=[_row(ts, width), _row(ts, LANES), _row(ts, LANES), _vec(width),
                  pl.BlockSpec((LANES, LANES), lambda i: (0, 0))],
        out_specs=_row(ts, width),
        out_shape=jax.ShapeDtypeStruct((rows, width), MXU_DT), compiler_params=_cp("parallel"),
    )(xin, cos, sin, w, _head_group_matrix())


def _qk_prep_bwd(xin, dout, width, cos, sin, w, *, tail=None, name):
    rows = xin.shape[0]
    ts = min(rows, 256)
    nch = width // LANES
    has_tail = tail is not None

    def body(*refs):
        if has_tail:
            x_ref, d_ref, t_ref, cos_ref, sin_ref, w_ref, g_ref, dx_ref, dw_ref = refs
        else:
            x_ref, d_ref, cos_ref, sin_ref, w_ref, g_ref, dx_ref, dw_ref = refs
        i = pl.program_id(0)
        cs, sn, g = cos_ref[...], sin_ref[...], g_ref[...]

        @pl.when(i == 0)
        def _():
            dw_ref[...] = jnp.zeros_like(dw_ref)

        if has_tail:
            dx_ref[:, width:width + LANES] = t_ref[...].astype(dx_ref.dtype)
            dx_ref[:, width + LANES:] = jnp.zeros((ts, GLU_OFF - width - LANES), dx_ref.dtype)

        for ch in range(nch):
            sl = slice(ch * LANES, (ch + 1) * LANES)
            xv = x_ref[:, sl]
            dv = d_ref[:, sl].astype(F32)
            r = lax.rsqrt(_group_sum(xv * xv, g) * (1.0 / HEAD_DIM) + EPS)
            n = xv * r
            dyw = dv * cs - _rot(dv * sn)
            dw_ref[:, sl] += _colsum(dyw * n)
            dn = dyw * w_ref[:, sl]
            gm = _group_sum(dn * n, g) * (1.0 / HEAD_DIM)
            dx_ref[:, sl] = (r * (dn - n * gm)).astype(dx_ref.dtype)

    ins = [xin, dout] + ([tail] if has_tail else []) + [cos, sin, w, _head_group_matrix()]
    in_specs = ([_row(ts, width), _row(ts, width)] + ([_row(ts, LANES)] if has_tail else [])
                + [_row(ts, LANES), _row(ts, LANES), _vec(width), pl.BlockSpec((LANES, LANES), lambda i: (0, 0))])
    out_w, arr_w = (GLU_OFF, IN_PAD) if has_tail else (width, width)
    return pl.pallas_call(
        body, name=name, grid=(rows // ts,), in_specs=in_specs,
        out_specs=[_row(ts, out_w), _vec(width)],
        out_shape=[jax.ShapeDtypeStruct((rows, arr_w), MXU_DT), jax.ShapeDtypeStruct((1, width), F32)],
        compiler_params=_cp("arbitrary"),
    )(*ins)


def _attn_mask(i, seq):
    shape = (GQA_GROUP * BLOCK, 3 * BLOCK)
    r = lax.broadcasted_iota(jnp.int32, shape, 0) & (BLOCK - 1)
    col = lax.broadcasted_iota(jnp.int32, shape, 1)
    diff = col - r
    kpos = col + (i - 1) * BLOCK
    return (diff >= 0) & (diff <= 2 * WINDOW) & (kpos >= 0) & (kpos < seq)


def _sink_col(sink_ref, kv):
    return jnp.concatenate(
        [jnp.full((BLOCK, 1), sink_ref[0, kv * GQA_GROUP + g], F32) for g in range(GQA_GROUP)], axis=0)


_NT = (((1,), (1,)), ((), ()))
_TN = (((0,), (0,)), ((), ()))


def _attn_specs(seq, n_ctx):
    qs = pl.BlockSpec((N_Q_HEADS, BLOCK, HEAD_DIM), lambda i: (0, i, 0))
    kvs = pl.BlockSpec((N_KV_HEADS, seq + 2 * BLOCK, HEAD_DIM), lambda i: (0, 0, 0))
    cs = pl.BlockSpec((N_KV_HEADS, n_ctx, HEAD_DIM), lambda i: (0, 0, 0))
    ls = pl.BlockSpec((N_Q_HEADS, BLOCK, 1), lambda i: (0, i, 0))
    return qs, kvs, cs, ls


def _attn_fwd(sink, q, k, v, kc, vc, *, carry=None, name):
    seq, n_ctx = q.shape[1], kc.shape[1]
    scale = 1.0 / math.sqrt(HEAD_DIM)
    gb = GQA_GROUP * BLOCK

    def body(sink_ref, q_ref, k_ref, v_ref, kc_ref, vc_ref, o_ref, lse_ref):
        i = pl.program_id(0)
        valid = _attn_mask(i, seq)
        start = pl.multiple_of(i * BLOCK, BLOCK)
        for kv in range(N_KV_HEADS):
            hs = slice(kv * GQA_GROUP, (kv + 1) * GQA_GROUP)
            qs = q_ref[hs].reshape(gb, HEAD_DIM)
            kw = k_ref[kv, pl.ds(start, 3 * BLOCK), :]
            vw = v_ref[kv, pl.ds(start, 3 * BLOCK), :]
            s_loc = lax.dot_general(qs, kw, _NT, preferred_element_type=F32) * scale
            s_loc = jnp.where(valid, s_loc, NEG_INF)
            s_ctx = lax.dot_general(qs, kc_ref[kv], _NT, preferred_element_type=F32) * scale
            sk = _sink_col(sink_ref, kv)
            m = jnp.maximum(jnp.maximum(jnp.max(s_loc, axis=-1, keepdims=True),
                                        jnp.max(s_ctx, axis=-1, keepdims=True)), sk)
            p_loc = jnp.exp(s_loc - m)
            p_ctx = jnp.exp(s_ctx - m)
            l = (jnp.sum(p_loc, axis=-1, keepdims=True) + jnp.sum(p_ctx, axis=-1, keepdims=True)
                 + jnp.exp(sk - m))
            o = (jnp.dot(p_loc.astype(MXU_DT), vw, preferred_element_type=F32)
                 + jnp.dot(p_ctx.astype(MXU_DT), vc_ref[kv], preferred_element_type=F32)) / l
            o_ref[hs] = o.reshape(GQA_GROUP, BLOCK, HEAD_DIM).astype(o_ref.dtype)
            lse_ref[hs] = (m + jnp.log(l)).reshape(GQA_GROUP, BLOCK, 1)

    qs, kvs, cs, ls = _attn_specs(seq, n_ctx)
    return _grid_call(
        body, [sink, q, k, v, kc, vc], carry, name=name, grid=(seq // BLOCK,),
        in_specs=[pl.BlockSpec(memory_space=pltpu.SMEM), qs, kvs, kvs, cs, cs],
        out_specs=[qs, ls],
        out_shape=[jax.ShapeDtypeStruct(q.shape, MXU_DT), jax.ShapeDtypeStruct((N_Q_HEADS, seq, 1), F32)],
        dims=("parallel",))


def _attn_bwd(sink, q, k, v, kc, vc, o, lse, do, *, carry=None, name):
    seq, n_ctx = q.shape[1], kc.shape[1]
    scale = 1.0 / math.sqrt(HEAD_DIM)
    gb = GQA_GROUP * BLOCK

    def body(sink_ref, q_ref, k_ref, v_ref, kc_ref, vc_ref, o_ref, lse_ref, do_ref,
             dq_ref, dk_ref, dv_ref, dkc_ref, dvc_ref, ds_ref):
        i = pl.program_id(0)

        @pl.when(i == 0)
        def _():
            dk_ref[...] = jnp.zeros_like(dk_ref)
            dv_ref[...] = jnp.zeros_like(dv_ref)
            dkc_ref[...] = jnp.zeros_like(dkc_ref)
            dvc_ref[...] = jnp.zeros_like(dvc_ref)
            ds_ref[...] = jnp.zeros_like(ds_ref)

        valid = _attn_mask(i, seq)
        start = pl.multiple_of(i * BLOCK, BLOCK)
        win = pl.ds(start, 3 * BLOCK)
        for kv in range(N_KV_HEADS):
            hs = slice(kv * GQA_GROUP, (kv + 1) * GQA_GROUP)
            qs = q_ref[hs].reshape(gb, HEAD_DIM)
            dos = do_ref[hs].reshape(gb, HEAD_DIM)
            lse_s = lse_ref[hs].reshape(gb, 1)
            delta = jnp.sum(dos.astype(F32) * o_ref[hs].reshape(gb, HEAD_DIM).astype(F32),
                            axis=-1, keepdims=True)
            kw, vw = k_ref[kv, win, :], v_ref[kv, win, :]
            kcv, vcv = kc_ref[kv], vc_ref[kv]
            s_loc = lax.dot_general(qs, kw, _NT, preferred_element_type=F32) * scale
            s_loc = jnp.where(valid, s_loc, NEG_INF)
            s_ctx = lax.dot_general(qs, kcv, _NT, preferred_element_type=F32) * scale
            p_loc = jnp.exp(s_loc - lse_s)
            p_ctx = jnp.exp(s_ctx - lse_s)
            p_sink = jnp.exp(_sink_col(sink_ref, kv) - lse_s)
            dp_loc = lax.dot_general(dos, vw, _NT, preferred_element_type=F32)
            dp_ctx = lax.dot_general(dos, vcv, _NT, preferred_element_type=F32)
            ds_loc = (p_loc * (dp_loc - delta) * scale).astype(MXU_DT)
            ds_ctx = (p_ctx * (dp_ctx - delta) * scale).astype(MXU_DT)
            dq = (jnp.dot(ds_loc, kw, preferred_element_type=F32)
                  + jnp.dot(ds_ctx, kcv, preferred_element_type=F32))
            dq_ref[hs] = dq.reshape(GQA_GROUP, BLOCK, HEAD_DIM).astype(dq_ref.dtype)
            dk_ref[kv, win, :] += lax.dot_general(ds_loc, qs, _TN, preferred_element_type=F32)
            dv_ref[kv, win, :] += lax.dot_general(p_loc.astype(MXU_DT), dos, _TN, preferred_element_type=F32)
            dkc_ref[kv] += lax.dot_general(ds_ctx, qs, _TN, preferred_element_type=F32)
            dvc_ref[kv] += lax.dot_general(p_ctx.astype(MXU_DT), dos, _TN, preferred_element_type=F32)
            ds_ref[hs] += (-(p_sink * delta)).reshape(GQA_GROUP, BLOCK, 1)

    qs, kvs, cs, ls = _attn_specs(seq, n_ctx)
    sk = pl.BlockSpec((N_Q_HEADS, BLOCK, 1), lambda i: (0, 0, 0))
    return _grid_call(
        body, [sink, q, k, v, kc, vc, o, lse, do], carry, name=name, grid=(seq // BLOCK,),
        in_specs=[pl.BlockSpec(memory_space=pltpu.SMEM), qs, kvs, kvs, cs, cs, qs, ls, qs],
        out_specs=[qs, kvs, kvs, cs, cs, sk],
        out_shape=[jax.ShapeDtypeStruct(q.shape, F32), jax.ShapeDtypeStruct(k.shape, F32),
                   jax.ShapeDtypeStruct(v.shape, F32), jax.ShapeDtypeStruct(kc.shape, F32),
                   jax.ShapeDtypeStruct(vc.shape, F32), jax.ShapeDtypeStruct((N_Q_HEADS, BLOCK, 1), F32)],
        dims=("arbitrary",))


def _halo_specs(ts, w, rows, col=0):
    per = ts // HALO
    last = rows // HALO - 1
    return [pl.BlockSpec((HALO, w), lambda i: (jnp.maximum(i * per - 1, 0), col)),
            pl.BlockSpec((ts, w), lambda i: (i, col)),
            pl.BlockSpec((HALO, w), lambda i: (jnp.minimum((i + 1) * per, last), col))]


def _glu(v):
    return v[:, :CONV_CH] * _sigmoid(v[:, CONV_CH:])


def _ln_stats(u):
    mu = jnp.mean(u, axis=-1, keepdims=True)
    xc = u - mu
    rstd = lax.rsqrt(jnp.mean(xc * xc, axis=-1, keepdims=True) + EPS)
    return xc * rstd, rstd


def _conv_fwd(glu, col, cw, cb, lw, lb, *, name):
    rows = glu.shape[0]
    ts = min(rows, 256)
    nt = rows // ts

    def body(gp_ref, g_ref, gn_ref, cw_ref, cb_ref, lw_ref, lb_ref, u3_ref, u1_ref, ext_ref):
        i = pl.program_id(0)
        ext_ref[0:HALO, :] = jnp.where(i > 0, _glu(gp_ref[...]), 0.0)
        ext_ref[HALO:HALO + ts, :] = _glu(g_ref[...])
        ext_ref[HALO + ts:, :] = jnp.where(i < nt - 1, _glu(gn_ref[...]), 0.0)
        acc = jnp.broadcast_to(cb_ref[...], (ts, CONV_CH))
        for j in range(CONV_K):
            acc = acc + cw_ref[j:j + 1, :] * ext_ref[HALO - CONV_PAD + j:HALO - CONV_PAD + j + ts, :]
        u1_ref[...] = acc
        xh, _ = _ln_stats(acc)
        u2 = xh * lw_ref[...] + lb_ref[...]
        u3_ref[...] = (u2 * _sigmoid(u2)).astype(u3_ref.dtype)

    full = lambda shape: pl.BlockSpec(shape, lambda i: (0,) * len(shape))
    return pl.pallas_call(
        body, name=name, grid=(nt,),
        in_specs=_halo_specs(ts, GLU_COLS, rows, col) + [full((CONV_K, CONV_CH))] + [_vec(CONV_CH)] * 3,
        out_specs=[_row(ts, CONV_CH), _row(ts, CONV_CH)],
        out_shape=[jax.ShapeDtypeStruct((rows, CONV_CH), MXU_DT), jax.ShapeDtypeStruct((rows, CONV_CH), F32)],
        scratch_shapes=[pltpu.VMEM((ts + 2 * HALO, CONV_CH), F32)],
        compiler_params=_cp("parallel"),
    )(glu, glu, glu, cw, cb, lw, lb)


def _conv_bwd(glu, col, u1, du3, dproj, cw, lw, lb, *, carry=None, name):
    rows = glu.shape[0]
    ts = min(rows, 256)
    nt = rows // ts
    te = ts + 2 * HALO

    def du1_of(u1v, du3v, lw_v, lb_v):
        xh, rstd = _ln_stats(u1v)
        u2 = xh * lw_v + lb_v
        sg = _sigmoid(u2)
        du2 = du3v * (sg * (1.0 + u2 * (1.0 - sg)))
        dxh = du2 * lw_v
        du1 = rstd * (dxh - jnp.mean(dxh, axis=-1, keepdims=True)
                      - xh * jnp.mean(dxh * xh, axis=-1, keepdims=True))
        return du1, du2, xh

    def body(gp_ref, g_ref, gn_ref, up_ref, u_ref, un_ref, dp_ref, d_ref, dn_ref, cw_ref, lw_ref, lb_ref,
             _, dglu_ref, dcw_ref, dvec_ref, u0_ref, du1_ref):
        i = pl.program_id(0)
        lw_v, lb_v = lw_ref[...], lb_ref[...]

        @pl.when(i == 0)
        def _():
            dcw_ref[...] = jnp.zeros_like(dcw_ref)
            dvec_ref[...] = jnp.zeros_like(dvec_ref)

        gv = g_ref[...]
        u0_ref[0:HALO, :] = jnp.where(i > 0, _glu(gp_ref[...]), 0.0)
        u0_ref[HALO:HALO + ts, :] = _glu(gv)
        u0_ref[HALO + ts:, :] = jnp.where(i < nt - 1, _glu(gn_ref[...]), 0.0)
        d_prev, _, _ = du1_of(up_ref[...], dp_ref[...], lw_v, lb_v)
        d_main, du2, xh = du1_of(u_ref[...], d_ref[...], lw_v, lb_v)
        d_next, _, _ = du1_of(un_ref[...], dn_ref[...], lw_v, lb_v)
        du1_ref[0:HALO, :] = jnp.where(i > 0, d_prev, 0.0)
        du1_ref[HALO:HALO + ts, :] = d_main
        du1_ref[HALO + ts:, :] = jnp.where(i < nt - 1, d_next, 0.0)

        rid = lax.broadcasted_iota(jnp.int32, (8, CONV_CH), 0)
        dvec_ref[...] += (jnp.where(rid == 0, _colsum(d_main), 0.0)
                          + jnp.where(rid == 1, _colsum(du2 * xh), 0.0)
                          + jnp.where(rid == 2, _colsum(du2), 0.0))
        du0 = jnp.zeros((ts, CONV_CH), F32)
        for j in range(CONV_K):
            lo = HALO + CONV_PAD - j
            du0 = du0 + cw_ref[j:j + 1, :] * du1_ref[lo:lo + ts, :]
            prod = d_main * u0_ref[HALO - CONV_PAD + j:HALO - CONV_PAD + j + ts, :]
            dcw_ref[j] += jnp.sum(prod.reshape(ts // 8, 8, CONV_CH), axis=0)
        ga, sg = gv[:, :CONV_CH], _sigmoid(gv[:, CONV_CH:])
        dglu_ref[:, :CONV_CH] = (du0 * sg).astype(dglu_ref.dtype)
        dglu_ref[:, CONV_CH:] = (du0 * ga * sg * (1.0 - sg)).astype(dglu_ref.dtype)

    full = lambda shape: pl.BlockSpec(shape, lambda i: (0,) * len(shape))
    return _grid_call(
        body, [glu, glu, glu, u1, u1, u1, du3, du3, du3, cw, lw, lb, dproj], carry, name=name, grid=(nt,),
        in_specs=(_halo_specs(ts, GLU_COLS, rows, col) + _halo_specs(ts, CONV_CH, rows)
                  + _halo_specs(ts, CONV_CH, rows) + [full((CONV_K, CONV_CH)), _vec(CONV_CH), _vec(CONV_CH)]
                  + [pl.BlockSpec(memory_space=pl.ANY)]),
        out_specs=[_row(ts, GLU_COLS, col), full((CONV_K, 8, CONV_CH)), full((8, CONV_CH))],
        out_shape=[jax.ShapeDtypeStruct(dproj.shape, dproj.dtype),
                   jax.ShapeDtypeStruct((CONV_K, 8, CONV_CH), F32), jax.ShapeDtypeStruct((8, CONV_CH), F32)],
        scratch_shapes=[pltpu.VMEM((te, CONV_CH), F32), pltpu.VMEM((te, CONV_CH), F32)],
        input_output_aliases={12: 0}, dims=("arbitrary",))


FFN_CW = 1408
FFN_NJ = FFN_H // FFN_CW
FFN_LO = HALO - 8


def _ffn_halo_specs(ts, rows, col_of, inner_rows):
    per = ts // HALO
    last = rows // HALO - 1
    if inner_rows:
        return [pl.BlockSpec((HALO, FFN_CW), lambda j, i: (jnp.maximum(i * per - 1, 0), col_of(j))),
                pl.BlockSpec((ts, FFN_CW), lambda j, i: (i, col_of(j))),
                pl.BlockSpec((HALO, FFN_CW), lambda j, i: (jnp.minimum((i + 1) * per, last), col_of(j)))]
    return [pl.BlockSpec((HALO, FFN_CW), lambda i, j: (jnp.maximum(i * per - 1, 0), col_of(j))),
            pl.BlockSpec((ts, FFN_CW), lambda i, j: (i, col_of(j))),
            pl.BlockSpec((HALO, FFN_CW), lambda i, j: (jnp.minimum((i + 1) * per, last), col_of(j)))]


def _ffn_fill(x_ref, p_ref, m_ref, n_ref, sl, i, nt, ts):
    x_ref[0:HALO, :] = jnp.where(i > 0, p_ref[:, sl].astype(F32), 0.0)
    x_ref[HALO:HALO + ts, :] = m_ref[:, sl].astype(F32)
    x_ref[HALO + ts:, :] = jnp.where(i < nt - 1, n_ref[:, sl].astype(F32), 0.0)


def _conv3_at(x_ref, lo, n, w, b):
    return (w[0:1, :] * x_ref[lo - 1:lo - 1 + n, :] + w[1:2, :] * x_ref[lo:lo + n, :]
            + w[2:3, :] * x_ref[lo + 1:lo + 1 + n, :] + b)


def _ffn_act(up0, w3, b3, *, name):
    rows = up0.shape[0]
    ts = min(rows, 256)
    nt = rows // ts

    def body(gp, g, gn, vp, v, vn, wg, wv, bg, bv, a_ref, xg_ref, xv_ref):
        i = pl.program_id(0)
        for ch in range(FFN_CW // LANES):
            sl = slice(ch * LANES, (ch + 1) * LANES)
            _ffn_fill(xg_ref, gp, g, gn, sl, i, nt, ts)
            _ffn_fill(xv_ref, vp, v, vn, sl, i, nt, ts)
            gate = _conv3_at(xg_ref, HALO, ts, wg[:, sl], bg[:, sl])
            val = _conv3_at(xv_ref, HALO, ts, wv[:, sl], bv[:, sl])
            a_ref[:, sl] = (gate * _sigmoid(gate) * val).astype(a_ref.dtype)

    gcol, vcol = (lambda j: j), (lambda j: j + FFN_NJ)
    wspec = lambda col_of: pl.BlockSpec((FFN_K, FFN_CW), lambda i, j: (0, col_of(j)))
    bspec = lambda col_of: pl.BlockSpec((1, FFN_CW), lambda i, j: (0, col_of(j)))
    return pl.pallas_call(
        body, name=name, grid=(nt, FFN_NJ),
        in_specs=(_ffn_halo_specs(ts, rows, gcol, False) + _ffn_halo_specs(ts, rows, vcol, False)
                  + [wspec(gcol), wspec(vcol), bspec(gcol), bspec(vcol)]),
        out_specs=pl.BlockSpec((ts, FFN_CW), lambda i, j: (i, j)),
        out_shape=jax.ShapeDtypeStruct((rows, FFN_H), MXU_DT),
        scratch_shapes=[pltpu.VMEM((ts + 2 * HALO, LANES), F32)] * 2,
        compiler_params=_cp("parallel", "parallel"),
    )(up0, up0, up0, up0, up0, up0, w3, w3, b3, b3)


def _ffn_act_bwd(up0, da, w3, b3, *, carry=None, name):
    rows = up0.shape[0]
    ts = min(rows, 256)
    nt = rows // ts

    ne = ts + 16
    off = HALO - FFN_LO

    def body(gp, g, gn, vp, v, vn, ap, a, an, wg, wv, bg, bv, dg_ref, dv_ref, sg_ref, sv_ref,
             xg_ref, xv_ref, xa_ref, eg_ref, ev_ref):
        i = pl.program_id(1)

        @pl.when(i == 0)
        def _():
            sg_ref[...] = jnp.zeros_like(sg_ref)
            sv_ref[...] = jnp.zeros_like(sv_ref)

        rid = lax.broadcasted_iota(jnp.int32, (8, LANES), 0)
        for ch in range(FFN_CW // LANES):
            sl = slice(ch * LANES, (ch + 1) * LANES)
            _ffn_fill(xg_ref, gp, g, gn, sl, i, nt, ts)
            _ffn_fill(xv_ref, vp, v, vn, sl, i, nt, ts)
            _ffn_fill(xa_ref, ap, a, an, sl, i, nt, ts)
            wgv, wvv = wg[:, sl], wv[:, sl]
            gate = _conv3_at(xg_ref, FFN_LO, ne, wgv, bg[:, sl])
            val = _conv3_at(xv_ref, FFN_LO, ne, wvv, bv[:, sl])
            da_e = xa_ref[FFN_LO:FFN_LO + ne, :]
            sgm = _sigmoid(gate)
            eg_ref[...] = da_e * val * (sgm * (1.0 + gate * (1.0 - sgm)))
            ev_ref[...] = da_e * (gate * sgm)
            for e_ref, x_ref, w, d_ref, s_ref in ((eg_ref, xg_ref, wgv, dg_ref, sg_ref),
                                                  (ev_ref, xv_ref, wvv, dv_ref, sv_ref)):
                dm = e_ref[off:off + ts, :]
                d0 = (w[0:1, :] * e_ref[off + 1:off + 1 + ts, :] + w[1:2, :] * dm
                      + w[2:3, :] * e_ref[off - 1:off - 1 + ts, :])
                d_ref[:, sl] = d0.astype(d_ref.dtype)
                s_ref[:, sl] += (jnp.where(rid == 0, _colsum(dm * x_ref[HALO - 1:HALO - 1 + ts, :]), 0.0)
                                 + jnp.where(rid == 1, _colsum(dm * x_ref[HALO:HALO + ts, :]), 0.0)
                                 + jnp.where(rid == 2, _colsum(dm * x_ref[HALO + 1:HALO + 1 + ts, :]), 0.0)
                                 + jnp.where(rid == 3, _colsum(dm), 0.0))

    gcol, vcol = (lambda j: j), (lambda j: j + FFN_NJ)
    wspec = lambda col_of: pl.BlockSpec((FFN_K, FFN_CW), lambda j, i: (0, col_of(j)))
    bspec = lambda col_of: pl.BlockSpec((1, FFN_CW), lambda j, i: (0, col_of(j)))
    ospec = pl.BlockSpec((ts, FFN_CW), lambda j, i: (i, j))
    sspec = pl.BlockSpec((8, FFN_CW), lambda j, i: (0, j))
    return _grid_call(
        body, [up0, up0, up0, up0, up0, up0, da, da, da, w3, w3, b3, b3], carry, name=name, grid=(FFN_NJ, nt),
        in_specs=(_ffn_halo_specs(ts, rows, gcol, True) + _ffn_halo_specs(ts, rows, vcol, True)
                  + _ffn_halo_specs(ts, rows, gcol, True)
                  + [wspec(gcol), wspec(vcol), bspec(gcol), bspec(vcol)]),
        out_specs=[ospec, ospec, sspec, sspec],
        out_shape=[jax.ShapeDtypeStruct((rows, FFN_H), MXU_DT), jax.ShapeDtypeStruct((rows, FFN_H), MXU_DT),
                   jax.ShapeDtypeStruct((8, FFN_H), F32), jax.ShapeDtypeStruct((8, FFN_H), F32)],
        scratch_shapes=([pltpu.VMEM((ts + 2 * HALO, LANES), F32)] * 3 + [pltpu.VMEM((ne, LANES), F32)] * 2),
        dims=("parallel", "arbitrary"))


def _adam_math(w, g, m, v):
    m = ADAM_B1 * m + (1.0 - ADAM_B1) * g
    v = ADAM_B2 * v + (1.0 - ADAM_B2) * (g * g)
    m_hat = m / (1.0 - ADAM_B1 ** ADAM_STEP)
    v_hat = v / (1.0 - ADAM_B2 ** ADAM_STEP)
    delta = -ADAM_LR * (m_hat / (jnp.sqrt(v_hat) + ADAM_EPS) + ADAM_WD * w)
    return delta, m, v


def _adam(w, m, v, parts, *, name):
    rows, cols = w.shape
    nparts = parts.shape[0]
    tr = rows if rows <= 256 else _pick(rows, (256, 128, 64, 32, 16, 8))

    def body(w_ref, m_ref, v_ref, p_ref, g_ref, d_ref, nm_ref, nv_ref):
        g = p_ref[0].astype(F32)
        for p in range(1, nparts):
            g = g + p_ref[p].astype(F32)
        g_ref[...] = g
        d_ref[...], nm_ref[...], nv_ref[...] = _adam_math(w_ref[...], g, m_ref[...], v_ref[...])

    spec = _row(tr, cols)
    return pl.pallas_call(
        body, name=name, grid=(rows // tr,),
        in_specs=[spec, spec, spec, pl.BlockSpec((nparts, tr, cols), lambda i: (0, i, 0))],
        out_specs=[spec] * 4, out_shape=[jax.ShapeDtypeStruct((rows, cols), F32)] * 4,
        compiler_params=_cp("parallel"),
    )(w, m, v, parts)


def _adam_many(ws, ms, vs, gs, *, name):
    n = len(ws)

    def body(*refs):
        ins, outs = refs[:4 * n], refs[4 * n:]
        for k in range(n):
            delta, new_m, new_v = _adam_math(ins[k][...], ins[3 * n + k][...], ins[n + k][...], ins[2 * n + k][...])
            outs[k][...], outs[n + k][...], outs[2 * n + k][...] = delta, new_m, new_v

    vm = pl.BlockSpec(memory_space=pltpu.VMEM)
    res = pl.pallas_call(
        body, name=name, in_specs=[vm] * (4 * n), out_specs=[vm] * (3 * n),
        out_shape=[jax.ShapeDtypeStruct(w.shape, F32) for w in ws] * 3,
        compiler_params=pltpu.CompilerParams(vmem_limit_bytes=VMEM_LIMIT),
    )(*ws, *ms, *vs, *gs)
    return res[:n], res[n:2 * n], res[2 * n:]


def _sum_parts(parts, *, name):
    nparts, rows, cols = parts.shape
    tr = rows if rows <= 256 else _pick(rows, (256, 128, 64, 32, 16, 8))

    def body(p_ref, o_ref):
        g = p_ref[0].astype(F32)
        for p in range(1, nparts):
            g = g + p_ref[p].astype(F32)
        o_ref[...] = g

    return pl.pallas_call(
        body, name=name, grid=(rows // tr,),
        in_specs=[pl.BlockSpec((nparts, tr, cols), lambda i: (0, i, 0))], out_specs=_row(tr, cols),
        out_shape=jax.ShapeDtypeStruct((rows, cols), F32), compiler_params=_cp("parallel"),
    )(parts)


def _my_place():
    return lax.axis_index("x"), lax.axis_index("y"), lax.axis_index("c")


def _dev_index(p):
    return 4 * p[0] + 2 * p[1] + p[2]


def _all_gather(xs, *, hbm, name):
    return _run_comm(_gather_plan(xs), pl.ANY if hbm else pltpu.VMEM, name)


class _Comm(NamedTuple):
    ins: list
    outs: list
    n_remote: int
    n_local: int
    start: Callable
    finish: Callable


def _comm_scratch(plan):
    return [pltpu.SemaphoreType.DMA((plan.n_remote,)), pltpu.SemaphoreType.DMA((plan.n_remote,)),
            pltpu.SemaphoreType.DMA((plan.n_local,))]


def _run_comm(plan, space, name):
    n_in, n_out = len(plan.ins), len(plan.outs)

    def body(*refs):
        args = (refs[:n_in], refs[n_in:n_in + n_out], *refs[n_in + n_out:])
        plan.start(*args)
        plan.finish(*args)

    return pl.pallas_call(
        body, name=name, out_shape=plan.outs,
        in_specs=[pl.BlockSpec(memory_space=space)] * n_in, out_specs=[pl.BlockSpec(memory_space=space)] * n_out,
        scratch_shapes=_comm_scratch(plan),
        compiler_params=pltpu.CompilerParams(vmem_limit_bytes=VMEM_LIMIT),
    )(*plan.ins)


def _grid_call(body, ins, carry, *, name, grid, in_specs, out_specs, out_shape, dims, scratch_shapes=(),
               input_output_aliases=None):
    if carry is None:
        res = pl.pallas_call(
            body, name=name, grid=grid, in_specs=list(in_specs), out_specs=list(out_specs),
            out_shape=list(out_shape), scratch_shapes=list(scratch_shapes),
            input_output_aliases=input_output_aliases or {}, compiler_params=_cp(*dims))(*ins)
        return list(res), None

    def at(pos):
        conds = [pl.program_id(k) == p for k, p in enumerate(pos)]
        out = conds[0]
        for cnd in conds[1:]:
            out = jnp.logical_and(out, cnd)
        return out

    return _carried_call(body, carry, lambda: at([0] * len(grid)), lambda: at([g - 1 for g in grid]), ins,
                         name=name, grid=grid, in_specs=in_specs, out_specs=out_specs, out_shape=out_shape,
                         scratch_shapes=scratch_shapes, input_output_aliases=input_output_aliases)


def _carried_call(body, plan, first, last, ins, *, name, grid, in_specs, out_specs, out_shape, scratch_shapes=(),
                  input_output_aliases=None):
    in_specs, out_specs, out_shape = list(in_specs), list(out_specs), list(out_shape)
    n_in, n_out, n_scr = len(in_specs), len(out_specs), len(scratch_shapes)
    c_in, c_out = len(plan.ins), len(plan.outs)
    hbm = pl.BlockSpec(memory_space=pl.ANY)

    def full_body(*refs):
        ins, c_ins = refs[:n_in], refs[n_in:n_in + c_in]
        outs = refs[n_in + c_in:n_in + c_in + n_out]
        c_outs = refs[n_in + c_in + n_out:n_in + c_in + n_out + c_out]
        scr = refs[n_in + c_in + n_out + c_out:]
        sems = scr[n_scr:]

        @pl.when(first())
        def _():
            plan.start(c_ins, c_outs, *sems)

        body(*ins, *outs, *scr[:n_scr])

        @pl.when(last())
        def _():
            plan.finish(c_ins, c_outs, *sems)

    res = pl.pallas_call(
        full_body, name=name, grid=grid,
        in_specs=in_specs + [hbm] * c_in, out_specs=out_specs + [hbm] * c_out,
        out_shape=out_shape + list(plan.outs),
        scratch_shapes=list(scratch_shapes) + _comm_scratch(plan),
        input_output_aliases=input_output_aliases or {},
        compiler_params=_cp(*(["arbitrary"] * len(grid))),
    )(*ins, *plan.ins)
    return list(res[:n_out]), list(res[n_out:])


def _gather_plan(xs):
    n = len(xs)
    ms = [v.shape[0] for v in xs]

    def tools(x_refs, o_refs, send_sems, recv_sems, local_sems):
        x, y, c = _my_place()
        me, sib = (x, y, c), (x, y, 1 - c)
        chips = [(1 - x, y), (x, 1 - y), (1 - x, 1 - y)]

        def rows(a, p):
            return o_refs[a].at[pl.ds(pl.multiple_of(_dev_index(p) * ms[a], 8), ms[a])]

        def copy(a, k, block, to, src=None):
            return pltpu.make_async_remote_copy(
                src_ref=rows(a, block) if src is None else src, dst_ref=rows(a, block),
                send_sem=send_sems.at[a * 7 + k], recv_sem=recv_sems.at[a * 7 + k],
                device_id=to, device_id_type=MESH)

        mine = [pltpu.make_async_copy(x_refs[a], rows(a, me), local_sems.at[a]) for a in range(n)]
        first = []
        for a in range(n):
            first.append(copy(a, 0, me, sib, src=x_refs[a]))
            first += [copy(a, 1 + j, me, (*chip, c), src=x_refs[a]) for j, chip in enumerate(chips)]
        return me, sib, chips, c, copy, mine, first

    def start(*refs):
        _, _, _, _, _, mine, first = tools(*refs)
        for cp in mine + first:
            cp.start()

    def finish(*refs):
        me, sib, chips, c, copy, mine, first = tools(*refs)
        passed = []
        for j, chip in enumerate(chips):
            for a in range(n):
                copy(a, 1 + j, (*chip, c), me).wait_recv()
                cp = copy(a, 4 + j, (*chip, c), sib)
                cp.start()
                passed.append(cp)
        for a in range(n):
            copy(a, 0, sib, me).wait_recv()
            for j, chip in enumerate(chips):
                copy(a, 4 + j, (*chip, 1 - c), me).wait_recv()
        for cp in first + passed:
            cp.wait_send()
        for cp in mine:
            cp.wait()

    outs = [jax.ShapeDtypeStruct((N_DEV * v.shape[0], v.shape[1]), v.dtype) for v in xs]
    return _Comm(list(xs), outs, 7 * n, n, start, finish)


def _exchange_plan(gs):
    n = len(gs)
    rs = [v.shape[0] // N_DEV for v in gs]
    flips = [(bx, by, bc) for bx in (0, 1) for by in (0, 1) for bc in (0, 1)][1:]

    def tools(g_refs, o_refs, send_sems, recv_sems, local_sems):
        x, y, c = _my_place()
        me = (x, y, c)

        def block(ref, a, p):
            return ref.at[pl.ds(_dev_index(p) * rs[a], rs[a])]

        def peer(f):
            return (1 - x if f[0] else x, 1 - y if f[1] else y, 1 - c if f[2] else c)

        def copy(a, k, to):
            return pltpu.make_async_remote_copy(
                src_ref=block(g_refs[a], a, to), dst_ref=block(o_refs[a], a, me),
                send_sem=send_sems.at[a * 7 + k], recv_sem=recv_sems.at[a * 7 + k],
                device_id=to, device_id_type=MESH)

        def arrival(a, k, frm):
            return pltpu.make_async_remote_copy(
                src_ref=block(g_refs[a], a, frm), dst_ref=block(o_refs[a], a, frm),
                send_sem=send_sems.at[a * 7 + k], recv_sem=recv_sems.at[a * 7 + k],
                device_id=frm, device_id_type=MESH)

        mine = [pltpu.make_async_copy(block(g_refs[a], a, me), block(o_refs[a], a, me), local_sems.at[a])
                for a in range(n)]
        sends = [copy(a, k, peer(f)) for a in range(n) for k, f in enumerate(flips)]
        arrivals = [arrival(a, k, peer(f)) for a in range(n) for k, f in enumerate(flips)]
        return mine, sends, arrivals

    def start(*refs):
        mine, sends, _ = tools(*refs)
        for cp in mine + sends:
            cp.start()

    def finish(*refs):
        mine, sends, arrivals = tools(*refs)
        for cp in arrivals:
            cp.wait_recv()
        for cp in sends:
            cp.wait_send()
        for cp in mine:
            cp.wait()

    outs = [jax.ShapeDtypeStruct(v.shape, v.dtype) for v in gs]
    return _Comm(list(gs), outs, 7 * n, n, start, finish)


def _rope_tables(seq):
    t = jnp.arange(seq)
    row, col = t // GRID_W, t % GRID_W
    half = HEAD_DIM // 2
    inv = ROPE_BASE ** (-jnp.arange(0, half, 2, dtype=F32) / half)
    ang_r = row.astype(F32)[:, None] * inv
    ang_c = col.astype(F32)[:, None] * inv
    ang = jnp.concatenate([ang_r, ang_r, ang_c, ang_c], axis=-1)
    return jnp.tile(jnp.cos(ang), (1, 2)), jnp.tile(jnp.sin(ang), (1, 2))


def _to_heads(v, nh):
    return v.reshape(v.shape[0], nh, HEAD_DIM).transpose(1, 0, 2)


def _from_heads(v):
    return v.transpose(1, 0, 2).reshape(v.shape[1], v.shape[0] * HEAD_DIM)


def _pad_rows(v):
    return jnp.pad(v, ((0, 0), (BLOCK, BLOCK), (0, 0)))


def _pack(vs):
    flat = jnp.concatenate([v.reshape(-1).astype(F32) for v in vs])
    total = -(-flat.shape[0] // (8 * LANES)) * (8 * LANES)
    return jnp.pad(flat, (0, total - flat.shape[0])).reshape(-1, LANES)


def _unpack(packed, like):
    flat, out, off = packed.reshape(-1), [], 0
    for v in like:
        size = math.prod(v.shape)
        out.append(flat[off:off + size].reshape(v.shape))
        off += size
    return out


def _silu(v):
    return v * jax.nn.sigmoid(v)


def kernel(x, c, ctx, c_ctx, w_mod, b_mod, norm_mix_w, w_in, q_norm_w, k_norm_w, sink_logit, conv_w, conv_b, conv_norm_w, conv_norm_b, w_out, norm_ffn_w, w_up, ffn_conv_w, ffn_conv_b, w_down, loss_target, m_c_ctx, m_w_mod, m_b_mod, m_norm_mix_w, m_w_in, m_q_norm_w, m_k_norm_w, m_sink_logit, m_conv_w, m_conv_b, m_conv_norm_w, m_conv_norm_b, m_w_out, m_norm_ffn_w, m_w_up, m_ffn_conv_w, m_ffn_conv_b, m_w_down, v_c_ctx, v_w_mod, v_b_mod, v_norm_mix_w, v_w_in, v_q_norm_w, v_k_norm_w, v_sink_logit, v_conv_w, v_conv_b, v_conv_norm_w, v_conv_norm_b, v_w_out, v_norm_ffn_w, v_w_up, v_ffn_conv_w, v_ffn_conv_b, v_w_down):
    d = D_MODEL
    seq, n_ctx = x.shape[1], ctx.shape[1]
    me = _dev_index(_my_place())
    xs, ctxs, tgt = x[0], ctx[0], loss_target[0]

    small = _pack([c[0], conv_w[0], ffn_conv_w[0]])
    w_in_t, small_all = _all_gather([w_in[0].T.astype(MXU_DT), small], hbm=True, name="gather_first")
    small_all = small_all.reshape(N_DEV, -1)
    w_in_p = jnp.concatenate([w_in_t[:QKV_COLS], jnp.zeros((GLU_OFF - QKV_COLS, d), MXU_DT), w_in_t[QKV_COLS:]])
    n_cw, n_fw = conv_w[0].size, ffn_conv_w[0].size
    c_all = small_all[:, :d]
    cw_all = small_all[:, d:d + n_cw].reshape(N_DEV, CONV_K, -1)
    fw_all = small_all[:, d + n_cw:d + n_cw + n_fw].reshape(N_DEV, FFN_K, -1)
    conv_w_f = cw_all.transpose(1, 0, 2).reshape(CONV_K, CONV_CH)
    ffn_w_f = fw_all.transpose(1, 0, 2).reshape(FFN_K, 2 * FFN_H)

    mcols = w_mod.shape[2]
    act = jnp.zeros((16, d), F32).at[:N_DEV].set(_silu(c_all)).at[N_DEV].set(_silu(c_ctx))
    mod_part = _mm(act, w_mod[0], 16, mcols, d, name="mod_fwd")
    mod_all = _all_gather([mod_part], hbm=False, name="gather_mod")[0]
    mod_all = mod_all.reshape(N_DEV, 16, mcols).transpose(1, 0, 2).reshape(16, 6 * d) + b_mod
    mod = lax.dynamic_slice_in_dim(mod_all, me, 1, axis=0)
    sh1, sc1, g1, sh2, sc2, g2 = [mod[:, k * d:(k + 1) * d] for k in range(6)]
    sh1c, sc1c = mod_all[N_DEV:N_DEV + 1, :d], mod_all[N_DEV:N_DEV + 1, d:2 * d]

    cos, sin = _rope_tables(seq)
    ones_c, zeros_c = jnp.ones((n_ctx, LANES), F32), jnp.zeros((n_ctx, LANES), F32)
    qk_w = jnp.concatenate([jnp.tile(q_norm_w, (1, N_Q_HEADS)), jnp.tile(k_norm_w, (1, N_KV_HEADS))], axis=1)
    kc_w = jnp.tile(k_norm_w, (1, N_KV_HEADS))

    h = _prenorm(xs, norm_mix_w, sc1, sh1, name="prenorm_mix")
    hc = _prenorm(ctxs, norm_mix_w, sc1c, sh1c, name="prenorm_ctx")
    proj = _mm(h, w_in_p, seq, IN_PAD, d, tb=True, name="proj_in")
    kv_ctx = _mm(hc, w_in_p, n_ctx, 2 * KV_COLS, d, tb=True, n0=Q_COLS, name="proj_ctx")
    qk_r = _qk_prep(proj, QK_COLS, cos, sin, qk_w, name="qk_prep")
    kc_n = _qk_prep(kv_ctx, KV_COLS, ones_c, zeros_c, kc_w, name="k_ctx_prep")
    q_h = _to_heads(qk_r[:, :Q_COLS], N_Q_HEADS)
    k_h = _pad_rows(_to_heads(qk_r[:, Q_COLS:], N_KV_HEADS))
    v_h = _pad_rows(_to_heads(proj[:, QK_COLS:QKV_COLS].astype(MXU_DT), N_KV_HEADS))
    kc_h = _to_heads(kc_n, N_KV_HEADS)
    vc_h = _to_heads(kv_ctx[:, KV_COLS:].astype(MXU_DT), N_KV_HEADS)
    later = _gather_plan([w_up[0].T.astype(MXU_DT), w_out[0].astype(MXU_DT), w_down[0].astype(MXU_DT)])
    (o_h, lse), (w_up_t, w_out_f, w_down_f) = _attn_fwd(sink_logit, q_h, k_h, v_h, kc_h, vc_h, carry=later,
                                                        name="attn_fwd")
    attn_o = _from_heads(o_h)
    u3, u1 = _conv_fwd(proj, GLU_OFF // GLU_COLS, conv_w_f, conv_b, conv_norm_w, conv_norm_b, name="conv_fwd")
    mix = _mm(attn_o, w_out_f, seq, d, Q_COLS, name="out_attn")
    mix = _mm(u3, w_out_f, seq, d, CONV_CH, k0=Q_COLS, add=mix, name="out_conv")

    x1, h2 = _resid_prenorm(xs, mix, g1, norm_ffn_w, sc2, sh2, name="prenorm_ffn")
    up0 = _mm(h2, w_up_t, seq, 2 * FFN_H, d, tb=True, out_dtype=MXU_DT, name="ffn_up")
    act_a = _ffn_act(up0, ffn_w_f, ffn_conv_b, name="ffn_act")
    ffn = _mm(act_a, w_down_f, seq, d, FFN_H, name="ffn_down")
    loss_p, dy, dffn, dg2 = _loss_head(x1, ffn, g2, tgt, name="loss_head")

    da = _mm(dffn, w_down_f, seq, FFN_H, d, tb=True, out_dtype=MXU_DT, name="ffn_down_dx")
    gw_down = _mm(act_a, dffn, FFN_H, d, seq, ta=True, out_dtype=MXU_DT, name="ffn_down_dw")
    (dgate0, dval0, s_gate, s_val), (rx_down,) = _ffn_act_bwd(
        up0, da, ffn_w_f, ffn_conv_b, carry=_exchange_plan([gw_down]), name="ffn_act_bwd")
    dh2 = _mm(dgate0, w_up_t, seq, d, FFN_H, name="ffn_up_dx_gate")
    dh2 = _mm(dval0, w_up_t, seq, d, FFN_H, k0=FFN_H, add=dh2, name="ffn_up_dx_val")
    gw_up_t = _mm(dgate0, h2, FFN_H, d, seq, ta=True, out_dtype=MXU_DT, out_rows=2 * FFN_H, name="ffn_up_dw_gate")
    gw_up_t = _mm(dval0, h2, FFN_H, d, seq, ta=True, into=gw_up_t, m0=FFN_H, name="ffn_up_dw_val")
    (dx1, dmix, dsh2, dsc2, dnw2, dg1), _ = _norm_bwd(
        dh2, x1, norm_ffn_w, sc2, res=dy, gate=(mix, g1), name="prenorm_ffn_bwd")

    dattn = _mm(dmix, w_out_f, seq, Q_COLS, d, tb=True, out_dtype=MXU_DT, name="out_dx_attn")
    du3 = _mm(dmix, w_out_f, seq, CONV_CH, d, tb=True, n0=Q_COLS, name="out_dx_conv")
    gw_out = _mm(attn_o, dmix, Q_COLS, d, seq, ta=True, out_dtype=MXU_DT, out_rows=Q_COLS + CONV_CH,
                 name="out_dw_attn")
    gw_out = _mm(u3, dmix, CONV_CH, d, seq, ta=True, into=gw_out, m0=Q_COLS, name="out_dw_conv")
    (dq_h, dk_h, dv_h, dkc_h, dvc_h, dsink_rows), (rx_out,) = _attn_bwd(
        sink_logit, q_h, k_h, v_h, kc_h, vc_h, o_h, lse, _to_heads(dattn, N_Q_HEADS),
        carry=_exchange_plan([gw_out]), name="attn_bwd")
    dqk_r = jnp.concatenate([_from_heads(dq_h), _from_heads(dk_h[:, BLOCK:BLOCK + seq])], axis=1)
    dproj, dqk_w = _qk_prep_bwd(proj, dqk_r, QK_COLS, cos, sin, qk_w,
                                tail=_from_heads(dv_h[:, BLOCK:BLOCK + seq]), name="qk_prep_bwd")
    (dproj, dcw8, dvec), (rx_up,) = _conv_bwd(
        proj, GLU_OFF // GLU_COLS, u1, du3, dproj, conv_w_f, conv_norm_w, conv_norm_b,
        carry=_exchange_plan([gw_up_t]), name="conv_bwd")
    dkc, dkc_w = _qk_prep_bwd(kv_ctx, _from_heads(dkc_h), KV_COLS, ones_c, zeros_c, kc_w, name="k_ctx_prep_bwd")
    dkv_ctx = jnp.concatenate([dkc, _from_heads(dvc_h).astype(MXU_DT)], axis=1)
    dh = _mm(dproj, w_in_p, seq, d, IN_PAD, name="proj_dx")
    dhc = _mm(dkv_ctx, w_in_p, n_ctx, d, 2 * KV_COLS, k0=Q_COLS, name="proj_dx_ctx")
    gw_p = _mm(dproj, h, IN_PAD, d, seq, ta=True, name="proj_dw")
    gw_ctx = _mm(dkv_ctx, hc, 2 * KV_COLS, d, n_ctx, ta=True, name="proj_dw_ctx")
    gw_in_t = jnp.concatenate([gw_p[:Q_COLS], gw_p[Q_COLS:QKV_COLS] + gw_ctx, gw_p[GLU_OFF:]], axis=0).astype(MXU_DT)
    (grad_x, dsh1, dsc1, dnw1), (rx_in,) = _norm_bwd(dh, xs, norm_mix_w, sc1, res=dx1,
                                                     carry=_exchange_plan([gw_in_t]), name="prenorm_mix_bwd")
    (dsh1c, dsc1c, dnw1c), _ = _norm_bwd(dhc, ctxs, norm_mix_w, sc1c, want_dx=False, name="prenorm_ctx_bwd")

    dmod = jnp.concatenate([dsh1, dsc1, dg1, dsh2, dsc2, dg2], axis=1)
    dmod_ctx = jnp.concatenate([dsh1c, dsc1c], axis=1)
    d_qn = dqk_w[0, :Q_COLS].reshape(N_Q_HEADS, HEAD_DIM).sum(0)
    d_kn = (dqk_w[0, Q_COLS:].reshape(N_KV_HEADS, HEAD_DIM).sum(0)
            + dkc_w[0].reshape(N_KV_HEADS, HEAD_DIM).sum(0))
    d_ffn_w = jnp.concatenate([s_gate[:FFN_K], s_val[:FFN_K]], axis=1)
    d_ffn_b = jnp.concatenate([s_gate[FFN_K], s_val[FFN_K]])
    summed_like = [(dnw1 + dnw1c), d_qn[None], d_kn[None], dsink_rows.sum((1, 2))[None], dvec[0:1], dvec[1:2],
                   dvec[2:3], dnw2, d_ffn_b[None], dcw8.sum(1), d_ffn_w, loss_p[0:1, 0:1]]
    pack = _pack([dmod, dmod_ctx] + summed_like)
    pack_all = _all_gather([pack], hbm=False, name="gather_small_grads")[0]
    pack_all = pack_all.reshape(N_DEV, pack.shape[0], LANES)
    tot = _sum_parts(pack_all, name="sum_small_grads")
    (dmod_sum, dmc_sum, g_nmix, g_qn, g_kn, g_sink, g_cb, g_lw, g_lb, g_nffn, g_fb, g_cw_f, g_fw_f,
     loss_sum) = _unpack(tot, [dmod, dmod_ctx] + summed_like)
    loss = loss_sum[0, 0]
    dmod_all = pack_all.reshape(N_DEV, -1)[:, :6 * d]
    g_b_mod = dmod_sum.at[:, :2 * d].add(dmc_sum)

    lo = me * mcols
    dm_rows = jnp.zeros((16, 6 * d), F32).at[:N_DEV].set(dmod_all).at[N_DEV, :2 * d].set(dmc_sum[0])
    dm_mine = lax.dynamic_slice_in_dim(dm_rows, lo, mcols, axis=1)
    parts_mod = _mm(act, dm_mine, d, mcols, 16, ta=True, name="mod_dw")[None]
    dact_part = _mm(dm_mine[N_DEV:N_DEV + 8], w_mod[0], 8, d, mcols, tb=True, name="mod_dx_ctx")
    dact_all = _all_gather([dact_part], hbm=False, name="gather_c_ctx_grad")[0].reshape(N_DEV, 8, d)
    dact = _sum_parts(dact_all, name="sum_c_ctx_grad")[0]
    sg = jax.nn.sigmoid(c_ctx)
    g_c_ctx = dact * (sg * (1.0 + c_ctx * (1.0 - sg)))

    def stacked(rx):
        return rx.reshape(N_DEV, rx.shape[0] // N_DEV, rx.shape[1])

    g_w_in = _sum_parts(stacked(rx_in), name="sum_w_in").T[None]
    g_w_up = _sum_parts(stacked(rx_up), name="sum_w_up").T[None]
    big = {}
    big["w_in"] = _adam(w_in[0], m_w_in[0], v_w_in[0], g_w_in, name="adam_w_in")
    big["w_up"] = _adam(w_up[0], m_w_up[0], v_w_up[0], g_w_up, name="adam_w_up")
    big["w_out"] = _adam(w_out[0], m_w_out[0], v_w_out[0], stacked(rx_out), name="adam_w_out")
    big["w_down"] = _adam(w_down[0], m_w_down[0], v_w_down[0], stacked(rx_down), name="adam_w_down")
    big["w_mod"] = _adam(w_mod[0], m_w_mod[0], v_w_mod[0], parts_mod, name="adam_w_mod")

    ccols, fcols = conv_w.shape[2], ffn_conv_w.shape[2]
    g_conv_w = lax.dynamic_slice_in_dim(g_cw_f, me * ccols, ccols, axis=1)[None]
    g_ffn_w = lax.dynamic_slice_in_dim(g_fw_f, me * fcols, fcols, axis=1)[None]
    names = ["c_ctx", "b_mod", "norm_mix_w", "q_norm_w", "k_norm_w", "sink_logit", "conv_w", "conv_b",
             "conv_norm_w", "conv_norm_b", "norm_ffn_w", "ffn_conv_w", "ffn_conv_b"]
    ws = [c_ctx, b_mod, norm_mix_w, q_norm_w, k_norm_w, sink_logit, conv_w, conv_b, conv_norm_w, conv_norm_b,
          norm_ffn_w, ffn_conv_w, ffn_conv_b]
    msm = [m_c_ctx, m_b_mod, m_norm_mix_w, m_q_norm_w, m_k_norm_w, m_sink_logit, m_conv_w, m_conv_b,
           m_conv_norm_w, m_conv_norm_b, m_norm_ffn_w, m_ffn_conv_w, m_ffn_conv_b]
    vsm = [v_c_ctx, v_b_mod, v_norm_mix_w, v_q_norm_w, v_k_norm_w, v_sink_logit, v_conv_w, v_conv_b,
           v_conv_norm_w, v_conv_norm_b, v_norm_ffn_w, v_ffn_conv_w, v_ffn_conv_b]
    gsm = [g_c_ctx, g_b_mod, g_nmix, g_qn, g_kn, g_sink, g_conv_w, g_cb, g_lw, g_lb, g_nffn, g_ffn_w, g_fb]
    deltas, new_ms, new_vs = _adam_many(ws, msm, vsm, gsm, name="adam_small")
    sm = {nm: vals for nm, vals in zip(names, zip(gsm, deltas, new_ms, new_vs))}

    def out4(nm):
        if nm in sm:
            return sm[nm]
        return tuple(t[None] for t in big[nm])

    order = ["c_ctx", "w_mod", "b_mod", "norm_mix_w", "w_in", "q_norm_w", "k_norm_w", "sink_logit", "conv_w",
             "conv_b", "conv_norm_w", "conv_norm_b", "w_out", "norm_ffn_w", "w_up", "ffn_conv_w", "ffn_conv_b",
             "w_down"]
    quads = [out4(nm) for nm in order]
    return (loss, grad_x[None], *[q[0] for q in quads], *[q[1] for q in quads],
            *[q[2] for q in quads], *[q[3] for q in quads])
```

```python
import math
from typing import Callable, NamedTuple

import jax
import jax.numpy as jnp
from jax import lax
from jax.experimental import pallas as pl
from jax.experimental.pallas import tpu as pltpu

F32 = jnp.float32
MXU_DT = jnp.bfloat16

D_MODEL = 1024
GRID_W = 64
HEAD_DIM = 64
N_Q_HEADS = 8
N_KV_HEADS = 2
GQA_GROUP = 4
WINDOW = 128
BLOCK = 128
Q_COLS = 512
KV_COLS = 128
QK_COLS = Q_COLS + KV_COLS
QKV_COLS = Q_COLS + 2 * KV_COLS
CONV_CH = 512
GLU_COLS = 2 * CONV_CH
IN_COLS = QKV_COLS + GLU_COLS
CONV_K = 31
CONV_PAD = 15
FFN_H = 2816
FFN_K = 3
ROPE_BASE = 10000.0
EPS = 1e-6
NEG_INF = -1e30
N_DEV = 8
HALO = 16
LANES = 128

ADAM_LR = 0.001
ADAM_B1 = 0.9
ADAM_B2 = 0.999
ADAM_EPS = 1e-08
ADAM_WD = 0.01
ADAM_STEP = 10

MESH_AXES = ("x", "y", "c")
MESH = pl.DeviceIdType.MESH
VMEM_LIMIT = 56 << 20
MM_VMEM_BUDGET = 44 << 20
GLU_OFF = 1024
IN_PAD = GLU_OFF + GLU_COLS


def _cp(*dims):
    return pltpu.CompilerParams(dimension_semantics=dims or None, vmem_limit_bytes=VMEM_LIMIT)


def _pick(n, prefs, also=0):
    for p in prefs:
        if n % p == 0 and also % p == 0:
            return p
    raise ValueError((n, prefs, also))


def _row(ts, w, col=0):
    return pl.BlockSpec((ts, w), lambda i: (i, col))


def _vec(w):
    return pl.BlockSpec((1, w), lambda i: (0, 0))


def _colsum(v):
    return jnp.sum(v, axis=0, keepdims=True)


def _sigmoid(v):
    return 0.5 * jnp.tanh(0.5 * v) + 0.5


def _mm(a, b, m, n, k, *, ta=False, tb=False, n0=0, k0=0, add=None, out_dtype=F32, into=None, m0=0,
        out_rows=None, carry=None, name):
    has_add = add is not None
    if into is not None:
        out_dtype = into.dtype
    sa, sb, so = a.dtype.itemsize, b.dtype.itemsize, jnp.dtype(out_dtype).itemsize
    sadd = add.dtype.itemsize if has_add else 0

    def fits(tm, tn):
        return 2 * (k * (tm * sa + tn * sb) + tm * tn * (so + sadd)) <= MM_VMEM_BUDGET

    tms = [m] if m <= 1024 else [t for t in (1024, 1408, 768, 512, 256, 128) if m % t == 0]
    tns = [t for t in ((1024, 512, 256, 128) if ta else (512, 1408, 256, 128)) if n % t == 0 and n0 % t == 0]
    tm, tn = next((tm, tn) for tm in tms for tn in tns if fits(tm, tn))
    assert k0 % k == 0 or k0 == 0, (k0, k)
    a_spec = (pl.BlockSpec((k, tm), lambda i, j: (0, i)) if ta else pl.BlockSpec((tm, k), lambda i, j: (i, 0)))
    nb0 = n0 // tn
    if tb:
        assert k0 == 0
        b_spec = pl.BlockSpec((tn, k), lambda i, j: (j + nb0, 0))
    else:
        assert k0 % k == 0, (k0, k)
        kb0 = k0 // k
        b_spec = pl.BlockSpec((k, tn), lambda i, j: (kb0, j + nb0))
    assert m0 % tm == 0, (m0, tm)
    mb0 = m0 // tm
    o_spec = pl.BlockSpec((tm, tn), lambda i, j: (i + mb0, j))
    dims = (((0 if ta else 1,), (1 if tb else 0,)), ((), ()))

    def body(*refs):
        a_ref, b_ref, o_ref = refs[0], refs[1], refs[-1]
        res = lax.dot_general(a_ref[...].astype(MXU_DT), b_ref[...].astype(MXU_DT), dims,
                              preferred_element_type=F32)
        if has_add:
            res = res + refs[2][...].astype(F32)
        o_ref[...] = res.astype(o_ref.dtype)

    ins = [a, b] + ([add] if has_add else []) + ([into] if into is not None else [])
    specs = ([a_spec, b_spec] + ([pl.BlockSpec((tm, tn), lambda i, j: (i, j))] if has_add else [])
             + ([pl.BlockSpec(memory_space=pl.ANY)] if into is not None else []))
    out_shape = (jax.ShapeDtypeStruct(into.shape, into.dtype) if into is not None
                 else jax.ShapeDtypeStruct((out_rows or m, n), out_dtype))
    (out,), carried = _grid_call(
        body, ins, carry, name=name, grid=(m // tm, n // tn), in_specs=specs, out_specs=[o_spec],
        out_shape=[out_shape], input_output_aliases={len(ins) - 1: 0} if into is not None else None,
        dims=("parallel", "parallel"))
    return out if carry is None else (out, carried)


def _rms_stats(xv):
    r = lax.rsqrt(jnp.mean(xv * xv, axis=-1, keepdims=True) + EPS)
    return r, xv * r


def _prenorm(x, nw, sc, sh, *, name):
    rows, d = x.shape
    ts = min(rows, 256)

    def body(x_ref, nw_ref, sc_ref, sh_ref, h_ref):
        _, xn = _rms_stats(x_ref[...])
        h_ref[...] = ((xn * nw_ref[...]) * (1.0 + sc_ref[...]) + sh_ref[...]).astype(h_ref.dtype)

    return pl.pallas_call(
        body, name=name, grid=(rows // ts,),
        in_specs=[_row(ts, d), _vec(d), _vec(d), _vec(d)], out_specs=_row(ts, d),
        out_shape=jax.ShapeDtypeStruct((rows, d), MXU_DT), compiler_params=_cp("parallel"),
    )(x, nw, sc, sh)


def _resid_prenorm(x, mix, g1, nw, sc, sh, *, name):
    rows, d = x.shape
    ts = min(rows, 256)

    def body(x_ref, mix_ref, g_ref, nw_ref, sc_ref, sh_ref, x1_ref, h_ref):
        x1 = x_ref[...] + g_ref[...] * mix_ref[...]
        x1_ref[...] = x1
        _, xn = _rms_stats(x1)
        h_ref[...] = ((xn * nw_ref[...]) * (1.0 + sc_ref[...]) + sh_ref[...]).astype(h_ref.dtype)

    return pl.pallas_call(
        body, name=name, grid=(rows // ts,),
        in_specs=[_row(ts, d), _row(ts, d), _vec(d), _vec(d), _vec(d), _vec(d)],
        out_specs=[_row(ts, d), _row(ts, d)],
        out_shape=[jax.ShapeDtypeStruct((rows, d), F32), jax.ShapeDtypeStruct((rows, d), MXU_DT)],
        compiler_params=_cp("parallel"),
    )(x, mix, g1, nw, sc, sh)


def _loss_head(x1, ffn, g2, target, *, name):
    rows, d = x1.shape
    ts = min(rows, 256)

    def body(x1_ref, f_ref, g_ref, t_ref, loss_ref, dy_ref, dffn_ref, dg_ref):
        i = pl.program_id(0)
        f = f_ref[...]
        e = x1_ref[...] + g_ref[...] * f - t_ref[...]
        per_tok = jnp.mean(e * e, axis=-1, keepdims=True)
        part = 0.5 * jnp.sum(per_tok, axis=0, keepdims=True)
        dy = e * (1.0 / d)
        dy_ref[...] = dy
        dffn_ref[...] = (dy * g_ref[...]).astype(dffn_ref.dtype)

        @pl.when(i == 0)
        def _():
            loss_ref[...] = jnp.zeros_like(loss_ref)
            dg_ref[...] = jnp.zeros_like(dg_ref)

        loss_ref[...] += jnp.broadcast_to(part, loss_ref.shape)
        dg_ref[...] += _colsum(dy * f)

    return pl.pallas_call(
        body, name=name, grid=(rows // ts,),
        in_specs=[_row(ts, d), _row(ts, d), _vec(d), _row(ts, d)],
        out_specs=[pl.BlockSpec((8, LANES), lambda i: (0, 0)), _row(ts, d), _row(ts, d), _vec(d)],
        out_shape=[jax.ShapeDtypeStruct((8, LANES), F32), jax.ShapeDtypeStruct((rows, d), F32),
                   jax.ShapeDtypeStruct((rows, d), MXU_DT), jax.ShapeDtypeStruct((1, d), F32)],
        compiler_params=_cp("arbitrary"),
    )(x1, ffn, g2, target)


def _norm_bwd(dh, xin, nw, sc, *, res=None, gate=None, want_dx=True, carry=None, name):
    rows, d = xin.shape
    ts = min(rows, 256)
    has_res, has_gate = res is not None, gate is not None

    def body(*refs):
        it = iter(refs)
        dh_ref, x_ref, nw_ref, sc_ref = next(it), next(it), next(it), next(it)
        res_ref = next(it) if has_res else None
        gated_ref, g_ref = (next(it), next(it)) if has_gate else (None, None)
        dx_ref = next(it) if want_dx else None
        dgx_ref = next(it) if has_gate else None
        dsh_ref, dsc_ref, dnw_ref = next(it), next(it), next(it)
        dg_ref = next(it) if has_gate else None
        i = pl.program_id(0)
        dhv = dh_ref[...]
        r, xn = _rms_stats(x_ref[...])
        dn = dhv * (1.0 + sc_ref[...])

        @pl.when(i == 0)
        def _():
            dsh_ref[...] = jnp.zeros_like(dsh_ref)
            dsc_ref[...] = jnp.zeros_like(dsc_ref)
            dnw_ref[...] = jnp.zeros_like(dnw_ref)
            if has_gate:
                dg_ref[...] = jnp.zeros_like(dg_ref)

        dsh_ref[...] += _colsum(dhv)
        dsc_ref[...] += _colsum(dhv * (xn * nw_ref[...]))
        dnw_ref[...] += _colsum(dn * xn)
        if want_dx:
            dxn = dn * nw_ref[...]
            dx = r * (dxn - xn * jnp.mean(dxn * xn, axis=-1, keepdims=True))
            if has_res:
                dx = dx + res_ref[...]
            dx_ref[...] = dx
            if has_gate:
                dgx_ref[...] = (dx * g_ref[...]).astype(dgx_ref.dtype)
                dg_ref[...] += _colsum(dx * gated_ref[...])

    ins = [dh, xin, nw, sc] + ([res] if has_res else []) + (list(gate) if has_gate else [])
    in_specs = ([_row(ts, d), _row(ts, d), _vec(d), _vec(d)] + ([_row(ts, d)] if has_res else [])
                + ([_row(ts, d), _vec(d)] if has_gate else []))
    out_specs, out_shape = [], []
    if want_dx:
        out_specs.append(_row(ts, d)); out_shape.append(jax.ShapeDtypeStruct((rows, d), F32))
    if has_gate:
        out_specs.append(_row(ts, d)); out_shape.append(jax.ShapeDtypeStruct((rows, d), MXU_DT))
    for _ in range(3 + int(has_gate)):
        out_specs.append(_vec(d)); out_shape.append(jax.ShapeDtypeStruct((1, d), F32))
    return _grid_call(body, ins, carry, name=name, grid=(rows // ts,), in_specs=in_specs, out_specs=out_specs,
                      out_shape=out_shape, dims=("arbitrary",))


def _group_sum(v, g):
    hi = v.astype(MXU_DT)
    lo = (v - hi.astype(F32)).astype(MXU_DT)
    return (jnp.dot(hi, g, preferred_element_type=F32) + jnp.dot(lo, g, preferred_element_type=F32))


def _rot(v):
    lane = lax.broadcasted_iota(jnp.int32, v.shape, 1)
    first = (lane & 31) < 16
    return jnp.where(first, -pltpu.roll(v, LANES - 16, 1), pltpu.roll(v, 16, 1))


def _head_group_matrix():
    r = jnp.arange(LANES) // HEAD_DIM
    return (r[:, None] == r[None, :]).astype(MXU_DT)


def _qk_prep(xin, width, cos, sin, w, *, name):
    rows = xin.shape[0]
    ts = min(rows, 256)
    nch = width // LANES

    def body(x_ref, cos_ref, sin_ref, w_ref, g_ref, o_ref):
        cs, sn, g = cos_ref[...], sin_ref[...], g_ref[...]
        for ch in range(nch):
            sl = slice(ch * LANES, (ch + 1) * LANES)
            xv = x_ref[:, sl]
            r = lax.rsqrt(_group_sum(xv * xv, g) * (1.0 / HEAD_DIM) + EPS)
            yw = (xv * r) * w_ref[:, sl]
            o_ref[:, sl] = (yw * cs + _rot(yw) * sn).astype(o_ref.dtype)

    return pl.pallas_call(
        body, name=name, grid=(rows // ts,),
        in_specs=[_row(ts, width), _row(ts, LANES), _row(ts, LANES), _vec(width),
                  pl.BlockSpec((LANES, LANES), lambda i: (0, 0))],
        out_specs=_row(ts, width),
        out_shape=jax.ShapeDtypeStruct((rows, width), MXU_DT), compiler_params=_cp("parallel"),
    )(xin, cos, sin, w, _head_group_matrix())


def _qk_prep_bwd(xin, douts, width, cos, sin, w, *, tail=None, name):
    rows = xin.shape[0]
    ts = min(rows, 256)
    nch = width // LANES
    has_tail = tail is not None
    nd = len(douts)
    assert sum(v.shape[1] for v in douts) == width
    src = [(k, c) for k, v in enumerate(douts) for c in range(v.shape[1] // LANES)]

    def body(*refs):
        x_ref, d_refs = refs[0], refs[1:1 + nd]
        t_ref = refs[1 + nd] if has_tail else None
        cos_ref, sin_ref, w_ref, g_ref, dx_ref, dw_ref = refs[1 + nd + int(has_tail):]
        i = pl.program_id(0)
        cs, sn, g = cos_ref[...], sin_ref[...], g_ref[...]

        @pl.when(i == 0)
        def _():
            dw_ref[...] = jnp.zeros_like(dw_ref)

        if has_tail:
            dx_ref[:, width:width + LANES] = t_ref[...].astype(dx_ref.dtype)
            dx_ref[:, width + LANES:] = jnp.zeros((ts, GLU_OFF - width - LANES), dx_ref.dtype)

        for ch in range(nch):
            sl = slice(ch * LANES, (ch + 1) * LANES)
            xv = x_ref[:, sl]
            dv = d_refs[src[ch][0]][:, src[ch][1] * LANES:(src[ch][1] + 1) * LANES].astype(F32)
            r = lax.rsqrt(_group_sum(xv * xv, g) * (1.0 / HEAD_DIM) + EPS)
            n = xv * r
            dyw = dv * cs - _rot(dv * sn)
            dw_ref[:, sl] += _colsum(dyw * n)
            dn = dyw * w_ref[:, sl]
            gm = _group_sum(dn * n, g) * (1.0 / HEAD_DIM)
            dx_ref[:, sl] = (r * (dn - n * gm)).astype(dx_ref.dtype)

    ins = [xin] + list(douts) + ([tail] if has_tail else []) + [cos, sin, w, _head_group_matrix()]
    in_specs = ([_row(ts, width)] + [_row(ts, v.shape[1]) for v in douts] + ([_row(ts, LANES)] if has_tail else [])
                + [_row(ts, LANES), _row(ts, LANES), _vec(width), pl.BlockSpec((LANES, LANES), lambda i: (0, 0))])
    out_w, arr_w = (GLU_OFF, IN_PAD) if has_tail else (width, width)
    return pl.pallas_call(
        body, name=name, grid=(rows // ts,), in_specs=in_specs,
        out_specs=[_row(ts, out_w), _vec(width)],
        out_shape=[jax.ShapeDtypeStruct((rows, arr_w), MXU_DT), jax.ShapeDtypeStruct((1, width), F32)],
        compiler_params=_cp("arbitrary"),
    )(*ins)


GB = GQA_GROUP * BLOCK
WIN = 3 * BLOCK


def _win_start(i, seq):
    return pl.multiple_of(jnp.clip((i - 1) * BLOCK, 0, seq - WIN), BLOCK)


def _attn_mask(i, start):
    qpos = i * BLOCK + (lax.broadcasted_iota(jnp.int32, (GB, WIN), 0) & (BLOCK - 1))
    kpos = start + lax.broadcasted_iota(jnp.int32, (GB, WIN), 1)
    return jnp.abs(qpos - kpos) <= WINDOW


def _sink_col(sink_ref, kv):
    return jnp.concatenate(
        [jnp.full((BLOCK, 1), sink_ref[0, kv * GQA_GROUP + g], F32) for g in range(GQA_GROUP)], axis=0)


def _stack_slots(ref):
    return jnp.concatenate([ref[:, g * LANES:(g + 1) * LANES] for g in range(GQA_GROUP)], axis=0)


def _kv_lanes(kv):
    lane = lax.broadcasted_iota(jnp.int32, (1, LANES), 1)
    return (lane < HEAD_DIM) if kv == 0 else (lane >= HEAD_DIM)


def _keep(mask, v):
    return jnp.where(mask, v, jnp.zeros_like(v))


_NT = (((1,), (1,)), ((), ()))
_TN = (((0,), (0,)), ((), ()))


def _attn_specs(seq, n_ctx):
    qs = pl.BlockSpec((BLOCK, Q_COLS), lambda i: (i, 0))
    ks = pl.BlockSpec((seq, KV_COLS), lambda i: (0, Q_COLS // KV_COLS))
    vs = pl.BlockSpec((seq, KV_COLS), lambda i: (0, QK_COLS // KV_COLS))
    kcs = pl.BlockSpec((n_ctx, KV_COLS), lambda i: (0, 0))
    vcs = pl.BlockSpec((n_ctx, KV_COLS), lambda i: (0, 1))
    ls = pl.BlockSpec((1, N_KV_HEADS * GB, 1), lambda i: (i, 0, 0))
    return qs, ks, vs, kcs, vcs, ls


def _attn_fwd(sink, qk, proj, kc, kv_ctx, *, carry=None, name):
    seq, n_ctx = qk.shape[0], kc.shape[0]
    scale = 1.0 / math.sqrt(HEAD_DIM)

    def body(sink_ref, q_ref, k_ref, v_ref, kc_ref, vc_ref, o_ref, lse_ref):
        i = pl.program_id(0)
        start = _win_start(i, seq)
        valid = _attn_mask(i, start)
        qs = _stack_slots(q_ref)
        kw, vw = k_ref[pl.ds(start, WIN), :], v_ref[pl.ds(start, WIN), :].astype(MXU_DT)
        kcw, vcw = kc_ref[...], vc_ref[...].astype(MXU_DT)
        o_all = jnp.zeros((GB, LANES), F32)
        for kv in range(N_KV_HEADS):
            mine = _kv_lanes(kv)
            s_loc = lax.dot_general(qs, _keep(mine, kw), _NT, preferred_element_type=F32) * scale
            s_loc = jnp.where(valid, s_loc, NEG_INF)
            s_ctx = lax.dot_general(qs, _keep(mine, kcw), _NT, preferred_element_type=F32) * scale
            sk = _sink_col(sink_ref, kv)
            m = jnp.maximum(jnp.maximum(jnp.max(s_loc, axis=-1, keepdims=True),
                                        jnp.max(s_ctx, axis=-1, keepdims=True)), sk)
            p_loc = jnp.exp(s_loc - m)
            p_ctx = jnp.exp(s_ctx - m)
            l = (jnp.sum(p_loc, axis=-1, keepdims=True) + jnp.sum(p_ctx, axis=-1, keepdims=True)
                 + jnp.exp(sk - m))
            o_all = o_all + (jnp.dot(p_loc.astype(MXU_DT), _keep(mine, vw), preferred_element_type=F32)
                             + jnp.dot(p_ctx.astype(MXU_DT), _keep(mine, vcw), preferred_element_type=F32)) / l
            lse_ref[0, kv * GB:(kv + 1) * GB, :] = m + jnp.log(l)
        for g in range(GQA_GROUP):
            o_ref[:, g * LANES:(g + 1) * LANES] = o_all[g * BLOCK:(g + 1) * BLOCK].astype(o_ref.dtype)

    qs, ks, vs, kcs, vcs, ls = _attn_specs(seq, n_ctx)
    return _grid_call(
        body, [sink, qk, qk, proj, kc, kv_ctx], carry, name=name, grid=(seq // BLOCK,),
        in_specs=[pl.BlockSpec(memory_space=pltpu.SMEM), qs, ks, vs, kcs, vcs],
        out_specs=[qs, ls],
        out_shape=[jax.ShapeDtypeStruct((seq, Q_COLS), MXU_DT),
                   jax.ShapeDtypeStruct((seq // BLOCK, N_KV_HEADS * GB, 1), F32)],
        dims=("parallel",))


def _attn_bwd(sink, qk, proj, kc, kv_ctx, o, lse, do, *, carry=None, name):
    seq, n_ctx = qk.shape[0], kc.shape[0]
    scale = 1.0 / math.sqrt(HEAD_DIM)

    def body(sink_ref, q_ref, k_ref, v_ref, kc_ref, vc_ref, o_ref, lse_ref, do_ref,
             dq_ref, dk_ref, dv_ref, dkc_ref, dvc_ref, ds_ref):
        i = pl.program_id(0)

        @pl.when(i == 0)
        def _():
            dk_ref[...] = jnp.zeros_like(dk_ref)
            dv_ref[...] = jnp.zeros_like(dv_ref)
            dkc_ref[...] = jnp.zeros_like(dkc_ref)
            dvc_ref[...] = jnp.zeros_like(dvc_ref)
            ds_ref[...] = jnp.zeros_like(ds_ref)

        start = _win_start(i, seq)
        valid = _attn_mask(i, start)
        win = pl.ds(start, WIN)
        qs, dos = _stack_slots(q_ref), _stack_slots(do_ref)
        do_o = dos.astype(F32) * _stack_slots(o_ref).astype(F32)
        kw, vw = k_ref[win, :], v_ref[win, :].astype(MXU_DT)
        kcw, vcw = kc_ref[...], vc_ref[...].astype(MXU_DT)
        dq = jnp.zeros((GB, LANES), F32)
        dk, dv = jnp.zeros((WIN, LANES), F32), jnp.zeros((WIN, LANES), F32)
        dkc, dvc = jnp.zeros((n_ctx, LANES), F32), jnp.zeros((n_ctx, LANES), F32)
        for kv in range(N_KV_HEADS):
            mine = _kv_lanes(kv)
            rows = slice(kv * GB, (kv + 1) * GB)
            lse_s = lse_ref[0, rows, :]
            delta = jnp.sum(_keep(mine, do_o), axis=-1, keepdims=True)
            kz, vz, kcz, vcz = _keep(mine, kw), _keep(mine, vw), _keep(mine, kcw), _keep(mine, vcw)
            s_loc = lax.dot_general(qs, kz, _NT, preferred_element_type=F32) * scale
            s_loc = jnp.where(valid, s_loc, NEG_INF)
            s_ctx = lax.dot_general(qs, kcz, _NT, preferred_element_type=F32) * scale
            p_loc = jnp.exp(s_loc - lse_s)
            p_ctx = jnp.exp(s_ctx - lse_s)
            p_sink = jnp.exp(_sink_col(sink_ref, kv) - lse_s)
            dp_loc = lax.dot_general(dos, vz, _NT, preferred_element_type=F32)
            dp_ctx = lax.dot_general(dos, vcz, _NT, preferred_element_type=F32)
            ds_loc = (p_loc * (dp_loc - delta) * scale).astype(MXU_DT)
            ds_ctx = (p_ctx * (dp_ctx - delta) * scale).astype(MXU_DT)
            dq = dq + (jnp.dot(ds_loc, kz, preferred_element_type=F32)
                       + jnp.dot(ds_ctx, kcz, preferred_element_type=F32))
            dk = dk + _keep(mine, lax.dot_general(ds_loc, qs, _TN, preferred_element_type=F32))
            dv = dv + _keep(mine, lax.dot_general(p_loc.astype(MXU_DT), dos, _TN, preferred_element_type=F32))
            dkc = dkc + _keep(mine, lax.dot_general(ds_ctx, qs, _TN, preferred_element_type=F32))
            dvc = dvc + _keep(mine, lax.dot_general(p_ctx.astype(MXU_DT), dos, _TN, preferred_element_type=F32))
            ds_ref[rows, :] += -(p_sink * delta)
        for g in range(GQA_GROUP):
            dq_ref[:, g * LANES:(g + 1) * LANES] = dq[g * BLOCK:(g + 1) * BLOCK]
        dk_ref[win, :] += dk
        dv_ref[win, :] += dv
        dkc_ref[...] += dkc
        dvc_ref[...] += dvc

    qs, ks, vs, kcs, vcs, ls = _attn_specs(seq, n_ctx)
    whole = lambda r, c: pl.BlockSpec((r, c), lambda i: (0, 0))
    return _grid_call(
        body, [sink, qk, qk, proj, kc, kv_ctx, o, lse, do], carry, name=name, grid=(seq // BLOCK,),
        in_specs=[pl.BlockSpec(memory_space=pltpu.SMEM), qs, ks, vs, kcs, vcs, qs, ls, qs],
        out_specs=[qs, whole(seq, KV_COLS), whole(seq, KV_COLS), whole(n_ctx, KV_COLS), whole(n_ctx, KV_COLS),
                   whole(N_KV_HEADS * GB, 1)],
        out_shape=[jax.ShapeDtypeStruct((seq, Q_COLS), F32), jax.ShapeDtypeStruct((seq, KV_COLS), F32),
                   jax.ShapeDtypeStruct((seq, KV_COLS), F32), jax.ShapeDtypeStruct((n_ctx, KV_COLS), F32),
                   jax.ShapeDtypeStruct((n_ctx, KV_COLS), F32), jax.ShapeDtypeStruct((N_KV_HEADS * GB, 1), F32)],
        dims=("arbitrary",))


def _halo_specs(ts, w, rows, col=0):
    per = ts // HALO
    last = rows // HALO - 1
    return [pl.BlockSpec((HALO, w), lambda i: (jnp.maximum(i * per - 1, 0), col)),
            pl.BlockSpec((ts, w), lambda i: (i, col)),
            pl.BlockSpec((HALO, w), lambda i: (jnp.minimum((i + 1) * per, last), col))]


def _glu(v):
    return v[:, :CONV_CH] * _sigmoid(v[:, CONV_CH:])


def _ln_stats(u):
    mu = jnp.mean(u, axis=-1, keepdims=True)
    xc = u - mu
    rstd = lax.rsqrt(jnp.mean(xc * xc, axis=-1, keepdims=True) + EPS)
    return xc * rstd, rstd


def _phases(ext_ref, ph_ref):
    n = ph_ref.shape[1]
    for b in range(1, 8):
        ph_ref[b - 1] = ext_ref[b:b + n, :]


def _window(ext_ref, ph_ref, o, n, cs):
    a, b = divmod(o, 8)
    src = ext_ref if b == 0 else ph_ref.at[b - 1]
    return src[8 * a:8 * a + n, cs]


def _conv_fwd(glu, col, cw, cb, lw, lb, *, name):
    rows = glu.shape[0]
    ts = min(rows, 256)
    nt = rows // ts

    te = ts + 2 * HALO

    def body(gp_ref, g_ref, gn_ref, cw_ref, cb_ref, lw_ref, lb_ref, u3_ref, u1_ref, ext_ref, ph_ref):
        i = pl.program_id(0)
        ext_ref[0:HALO, :] = jnp.where(i > 0, _glu(gp_ref[...]), 0.0)
        ext_ref[HALO:HALO + ts, :] = _glu(g_ref[...])
        ext_ref[HALO + ts:, :] = jnp.where(i < nt - 1, _glu(gn_ref[...]), 0.0)
        _phases(ext_ref, ph_ref)
        for c in range(CONV_CH // LANES):
            cs = slice(c * LANES, (c + 1) * LANES)
            acc = jnp.broadcast_to(cb_ref[:, cs], (ts, LANES))
            for j in range(CONV_K):
                acc = acc + cw_ref[j:j + 1, cs] * _window(ext_ref, ph_ref, HALO - CONV_PAD + j, ts, cs)
            u1_ref[:, cs] = acc
        xh, _ = _ln_stats(u1_ref[...])
        u2 = xh * lw_ref[...] + lb_ref[...]
        u3_ref[...] = (u2 * _sigmoid(u2)).astype(u3_ref.dtype)

    full = lambda shape: pl.BlockSpec(shape, lambda i: (0,) * len(shape))
    return pl.pallas_call(
        body, name=name, grid=(nt,),
        in_specs=_halo_specs(ts, GLU_COLS, rows, col) + [full((CONV_K, CONV_CH))] + [_vec(CONV_CH)] * 3,
        out_specs=[_row(ts, CONV_CH), _row(ts, CONV_CH)],
        out_shape=[jax.ShapeDtypeStruct((rows, CONV_CH), MXU_DT), jax.ShapeDtypeStruct((rows, CONV_CH), F32)],
        scratch_shapes=[pltpu.VMEM((te, CONV_CH), F32), pltpu.VMEM((7, te - 8, CONV_CH), F32)],
        compiler_params=_cp("parallel"),
    )(glu, glu, glu, cw, cb, lw, lb)


def _conv_bwd(glu, col, u1, du3, dproj, cw, lw, lb, *, carry=None, name):
    rows = glu.shape[0]
    ts = min(rows, 256)
    nt = rows // ts
    te = ts + 2 * HALO

    def du1_of(u1v, du3v, lw_v, lb_v):
        xh, rstd = _ln_stats(u1v)
        u2 = xh * lw_v + lb_v
        sg = _sigmoid(u2)
        du2 = du3v * (sg * (1.0 + u2 * (1.0 - sg)))
        dxh = du2 * lw_v
        du1 = rstd * (dxh - jnp.mean(dxh, axis=-1, keepdims=True)
                      - xh * jnp.mean(dxh * xh, axis=-1, keepdims=True))
        return du1, du2, xh

    half = ts // 2

    def body(gp_ref, g_ref, gn_ref, up_ref, u_ref, un_ref, dp_ref, d_ref, dn_ref, cw_ref, lw_ref, lb_ref,
             _, dglu_ref, dcw_ref, dvec_ref, u0_ref, du1_ref, pu_ref, pd_ref, du0_ref):
        i = pl.program_id(0)
        lw_v, lb_v = lw_ref[...], lb_ref[...]

        @pl.when(i == 0)
        def _():
            dcw_ref[...] = jnp.zeros_like(dcw_ref)
            dvec_ref[...] = jnp.zeros_like(dvec_ref)

        gv = g_ref[...]
        u0_ref[0:HALO, :] = jnp.where(i > 0, _glu(gp_ref[...]), 0.0)
        u0_ref[HALO:HALO + ts, :] = _glu(gv)
        u0_ref[HALO + ts:, :] = jnp.where(i < nt - 1, _glu(gn_ref[...]), 0.0)
        d_prev, _, _ = du1_of(up_ref[...], dp_ref[...], lw_v, lb_v)
        d_main, du2, xh = du1_of(u_ref[...], d_ref[...], lw_v, lb_v)
        d_next, _, _ = du1_of(un_ref[...], dn_ref[...], lw_v, lb_v)
        du1_ref[0:HALO, :] = jnp.where(i > 0, d_prev, 0.0)
        du1_ref[HALO:HALO + ts, :] = d_main
        du1_ref[HALO + ts:, :] = jnp.where(i < nt - 1, d_next, 0.0)

        rid = lax.broadcasted_iota(jnp.int32, (8, CONV_CH), 0)
        dvec_ref[...] += (jnp.where(rid == 0, _colsum(d_main), 0.0)
                          + jnp.where(rid == 1, _colsum(du2 * xh), 0.0)
                          + jnp.where(rid == 2, _colsum(du2), 0.0))
        _phases(u0_ref, pu_ref)
        _phases(du1_ref, pd_ref)
        for c in range(CONV_CH // LANES):
            cs = slice(c * LANES, (c + 1) * LANES)
            for r0 in (0, half):
                dm = du1_ref[HALO + r0:HALO + r0 + half, cs]
                acc = jnp.zeros((half, LANES), F32)
                for j in range(CONV_K):
                    acc = acc + cw_ref[j:j + 1, cs] * _window(du1_ref, pd_ref, r0 + HALO + CONV_PAD - j, half, cs)
                    prod = dm * _window(u0_ref, pu_ref, r0 + HALO - CONV_PAD + j, half, cs)
                    dcw_ref[j, :, cs] += jnp.sum(prod.reshape(half // 8, 8, LANES), axis=0)
                du0_ref[r0:r0 + half, cs] = acc
        du0 = du0_ref[...]
        ga, sg = gv[:, :CONV_CH], _sigmoid(gv[:, CONV_CH:])
        dglu_ref[:, :CONV_CH] = (du0 * sg).astype(dglu_ref.dtype)
        dglu_ref[:, CONV_CH:] = (du0 * ga * sg * (1.0 - sg)).astype(dglu_ref.dtype)

    full = lambda shape: pl.BlockSpec(shape, lambda i: (0,) * len(shape))
    return _grid_call(
        body, [glu, glu, glu, u1, u1, u1, du3, du3, du3, cw, lw, lb, dproj], carry, name=name, grid=(nt,),
        in_specs=(_halo_specs(ts, GLU_COLS, rows, col) + _halo_specs(ts, CONV_CH, rows)
                  + _halo_specs(ts, CONV_CH, rows) + [full((CONV_K, CONV_CH)), _vec(CONV_CH), _vec(CONV_CH)]
                  + [pl.BlockSpec(memory_space=pl.ANY)]),
        out_specs=[_row(ts, GLU_COLS, col), full((CONV_K, 8, CONV_CH)), full((8, CONV_CH))],
        out_shape=[jax.ShapeDtypeStruct(dproj.shape, dproj.dtype),
                   jax.ShapeDtypeStruct((CONV_K, 8, CONV_CH), F32), jax.ShapeDtypeStruct((8, CONV_CH), F32)],
        scratch_shapes=[pltpu.VMEM((te, CONV_CH), F32), pltpu.VMEM((te, CONV_CH), F32),
                        pltpu.VMEM((7, te - 8, CONV_CH), F32), pltpu.VMEM((7, te - 8, CONV_CH), F32),
                        pltpu.VMEM((ts, CONV_CH), F32)],
        input_output_aliases={12: 0}, dims=("arbitrary",))


FFN_CW = 1408
FFN_NJ = FFN_H // FFN_CW


def _ffn_halo_specs(ts, rows, col_of, inner_rows):
    per = ts // HALO
    last = rows // HALO - 1
    if inner_rows:
        return [pl.BlockSpec((HALO, FFN_CW), lambda j, i: (jnp.maximum(i * per - 1, 0), col_of(j))),
                pl.BlockSpec((ts, FFN_CW), lambda j, i: (i, col_of(j))),
                pl.BlockSpec((HALO, FFN_CW), lambda j, i: (jnp.minimum((i + 1) * per, last), col_of(j)))]
    return [pl.BlockSpec((HALO, FFN_CW), lambda i, j: (jnp.maximum(i * per - 1, 0), col_of(j))),
            pl.BlockSpec((ts, FFN_CW), lambda i, j: (i, col_of(j))),
            pl.BlockSpec((HALO, FFN_CW), lambda i, j: (jnp.minimum((i + 1) * per, last), col_of(j)))]


def _ffn_ext(p_ref, m_ref, n_ref, sl, i, nt):
    return jnp.concatenate([jnp.where(i > 0, p_ref[:, sl].astype(F32), 0.0), m_ref[:, sl].astype(F32),
                            jnp.where(i < nt - 1, n_ref[:, sl].astype(F32), 0.0)], axis=0)


def _prev_row(v):
    return pltpu.roll(v, 1, 0)


def _next_row(v):
    return pltpu.roll(v, v.shape[0] - 1, 0)


def _ffn_act(up0, w3, b3, *, name):
    rows = up0.shape[0]
    ts = min(rows, 256)
    nt = rows // ts
    main = slice(HALO, HALO + ts)

    def body(gp, g, gn, vp, v, vn, wg, wv, bg, bv, a_ref):
        i = pl.program_id(0)
        for ch in range(FFN_CW // LANES):
            sl = slice(ch * LANES, (ch + 1) * LANES)
            xg, xv = _ffn_ext(gp, g, gn, sl, i, nt), _ffn_ext(vp, v, vn, sl, i, nt)
            wgv, wvv = wg[:, sl], wv[:, sl]
            gate = (wgv[0:1] * _prev_row(xg) + wgv[1:2] * xg + wgv[2:3] * _next_row(xg))[main] + bg[:, sl]
            val = (wvv[0:1] * _prev_row(xv) + wvv[1:2] * xv + wvv[2:3] * _next_row(xv))[main] + bv[:, sl]
            a_ref[:, sl] = (gate * _sigmoid(gate) * val).astype(a_ref.dtype)

    gcol, vcol = (lambda j: j), (lambda j: j + FFN_NJ)
    wspec = lambda col_of: pl.BlockSpec((FFN_K, FFN_CW), lambda i, j: (0, col_of(j)))
    bspec = lambda col_of: pl.BlockSpec((1, FFN_CW), lambda i, j: (0, col_of(j)))
    return pl.pallas_call(
        body, name=name, grid=(nt, FFN_NJ),
        in_specs=(_ffn_halo_specs(ts, rows, gcol, False) + _ffn_halo_specs(ts, rows, vcol, False)
                  + [wspec(gcol), wspec(vcol), bspec(gcol), bspec(vcol)]),
        out_specs=pl.BlockSpec((ts, FFN_CW), lambda i, j: (i, j)),
        out_shape=jax.ShapeDtypeStruct((rows, FFN_H), MXU_DT),
        compiler_params=_cp("parallel", "parallel"),
    )(up0, up0, up0, up0, up0, up0, w3, w3, b3, b3)


def _ffn_act_bwd(up0, da, w3, b3, *, carry=None, name):
    rows = up0.shape[0]
    ts = min(rows, 256)
    nt = rows // ts
    main = slice(HALO, HALO + ts)

    def body(gp, g, gn, vp, v, vn, ap, a, an, wg, wv, bg, bv, dg_ref, dv_ref, sg_ref, sv_ref):
        i = pl.program_id(1)

        @pl.when(i == 0)
        def _():
            sg_ref[...] = jnp.zeros_like(sg_ref)
            sv_ref[...] = jnp.zeros_like(sv_ref)

        rid = lax.broadcasted_iota(jnp.int32, (8, LANES), 0)
        for ch in range(FFN_CW // LANES):
            sl = slice(ch * LANES, (ch + 1) * LANES)
            xg, xv, da_e = _ffn_ext(gp, g, gn, sl, i, nt), _ffn_ext(vp, v, vn, sl, i, nt), _ffn_ext(ap, a, an, sl, i, nt)
            wgv, wvv = wg[:, sl], wv[:, sl]
            xg_p, xg_n, xv_p, xv_n = _prev_row(xg), _next_row(xg), _prev_row(xv), _next_row(xv)
            gate = wgv[0:1] * xg_p + wgv[1:2] * xg + wgv[2:3] * xg_n + bg[:, sl]
            val = wvv[0:1] * xv_p + wvv[1:2] * xv + wvv[2:3] * xv_n + bv[:, sl]
            sgm = _sigmoid(gate)
            eg = da_e * val * (sgm * (1.0 + gate * (1.0 - sgm)))
            ev = da_e * (gate * sgm)
            for e, taps, w, d_ref, s_ref in ((eg, (xg_p, xg, xg_n), wgv, dg_ref, sg_ref),
                                             (ev, (xv_p, xv, xv_n), wvv, dv_ref, sv_ref)):
                d0 = w[0:1] * _next_row(e) + w[1:2] * e + w[2:3] * _prev_row(e)
                d_ref[:, sl] = d0[main].astype(d_ref.dtype)
                dm = e[main]
                s_ref[:, sl] += (jnp.where(rid == 0, _colsum(dm * taps[0][main]), 0.0)
                                 + jnp.where(rid == 1, _colsum(dm * taps[1][main]), 0.0)
                                 + jnp.where(rid == 2, _colsum(dm * taps[2][main]), 0.0)
                                 + jnp.where(rid == 3, _colsum(dm), 0.0))

    gcol, vcol = (lambda j: j), (lambda j: j + FFN_NJ)
    wspec = lambda col_of: pl.BlockSpec((FFN_K, FFN_CW), lambda j, i: (0, col_of(j)))
    bspec = lambda col_of: pl.BlockSpec((1, FFN_CW), lambda j, i: (0, col_of(j)))
    ospec = pl.BlockSpec((ts, FFN_CW), lambda j, i: (i, j))
    sspec = pl.BlockSpec((8, FFN_CW), lambda j, i: (0, j))
    return _grid_call(
        body, [up0, up0, up0, up0, up0, up0, da, da, da, w3, w3, b3, b3], carry, name=name, grid=(FFN_NJ, nt),
        in_specs=(_ffn_halo_specs(ts, rows, gcol, True) + _ffn_halo_specs(ts, rows, vcol, True)
                  + _ffn_halo_specs(ts, rows, gcol, True)
                  + [wspec(gcol), wspec(vcol), bspec(gcol), bspec(vcol)]),
        out_specs=[ospec, ospec, sspec, sspec],
        out_shape=[jax.ShapeDtypeStruct((rows, FFN_H), MXU_DT), jax.ShapeDtypeStruct((rows, FFN_H), MXU_DT),
                   jax.ShapeDtypeStruct((8, FFN_H), F32), jax.ShapeDtypeStruct((8, FFN_H), F32)],
        dims=("parallel", "arbitrary"))


def _adam_math(w, g, m, v):
    m = ADAM_B1 * m + (1.0 - ADAM_B1) * g
    v = ADAM_B2 * v + (1.0 - ADAM_B2) * (g * g)
    m_hat = m / (1.0 - ADAM_B1 ** ADAM_STEP)
    v_hat = v / (1.0 - ADAM_B2 ** ADAM_STEP)
    delta = -ADAM_LR * (m_hat / (jnp.sqrt(v_hat) + ADAM_EPS) + ADAM_WD * w)
    return delta, m, v


def _adam(w, m, v, parts, *, name):
    rows, cols = w.shape
    nparts = parts.shape[0]
    tr = rows if rows <= 256 else _pick(rows, (256, 128, 64, 32, 16, 8))

    def body(w_ref, m_ref, v_ref, p_ref, g_ref, d_ref, nm_ref, nv_ref):
        g = p_ref[0].astype(F32)
        for p in range(1, nparts):
            g = g + p_ref[p].astype(F32)
        g_ref[...] = g
        d_ref[...], nm_ref[...], nv_ref[...] = _adam_math(w_ref[...], g, m_ref[...], v_ref[...])

    spec = _row(tr, cols)
    return pl.pallas_call(
        body, name=name, grid=(rows // tr,),
        in_specs=[spec, spec, spec, pl.BlockSpec((nparts, tr, cols), lambda i: (0, i, 0))],
        out_specs=[spec] * 4, out_shape=[jax.ShapeDtypeStruct((rows, cols), F32)] * 4,
        compiler_params=_cp("parallel"),
    )(w, m, v, parts)


def _adam_many(ws, ms, vs, gs, *, name):
    n = len(ws)

    def body(*refs):
        ins, outs = refs[:4 * n], refs[4 * n:]
        for k in range(n):
            delta, new_m, new_v = _adam_math(ins[k][...], ins[3 * n + k][...], ins[n + k][...], ins[2 * n + k][...])
            outs[k][...], outs[n + k][...], outs[2 * n + k][...] = delta, new_m, new_v

    vm = pl.BlockSpec(memory_space=pltpu.VMEM)
    res = pl.pallas_call(
        body, name=name, in_specs=[vm] * (4 * n), out_specs=[vm] * (3 * n),
        out_shape=[jax.ShapeDtypeStruct(w.shape, F32) for w in ws] * 3,
        compiler_params=pltpu.CompilerParams(vmem_limit_bytes=VMEM_LIMIT),
    )(*ws, *ms, *vs, *gs)
    return res[:n], res[n:2 * n], res[2 * n:]


def _sum_parts(parts, *, name):
    nparts, rows, cols = parts.shape
    tr = rows if rows <= 256 else _pick(rows, (256, 128, 64, 32, 16, 8))

    def body(p_ref, o_ref):
        g = p_ref[0].astype(F32)
        for p in range(1, nparts):
            g = g + p_ref[p].astype(F32)
        o_ref[...] = g

    return pl.pallas_call(
        body, name=name, grid=(rows // tr,),
        in_specs=[pl.BlockSpec((nparts, tr, cols), lambda i: (0, i, 0))], out_specs=_row(tr, cols),
        out_shape=jax.ShapeDtypeStruct((rows, cols), F32), compiler_params=_cp("parallel"),
    )(parts)


def _my_place():
    return lax.axis_index("x"), lax.axis_index("y"), lax.axis_index("c")


def _dev_index(p):
    return 4 * p[0] + 2 * p[1] + p[2]


def _all_gather(xs, *, hbm, name):
    return _run_comm(_gather_plan(xs), pl.ANY if hbm else pltpu.VMEM, name)


class _Comm(NamedTuple):
    ins: list
    outs: list
    n_remote: int
    n_local: int
    start: Callable
    finish: Callable


def _comm_scratch(plan):
    return [pltpu.SemaphoreType.DMA((plan.n_remote,)), pltpu.SemaphoreType.DMA((plan.n_remote,)),
            pltpu.SemaphoreType.DMA((plan.n_local,))]


def _run_comm(plan, space, name):
    n_in, n_out = len(plan.ins), len(plan.outs)

    def body(*refs):
        args = (refs[:n_in], refs[n_in:n_in + n_out], *refs[n_in + n_out:])
        plan.start(*args)
        plan.finish(*args)

    return pl.pallas_call(
        body, name=name, out_shape=plan.outs,
        in_specs=[pl.BlockSpec(memory_space=space)] * n_in, out_specs=[pl.BlockSpec(memory_space=space)] * n_out,
        scratch_shapes=_comm_scratch(plan),
        compiler_params=pltpu.CompilerParams(vmem_limit_bytes=VMEM_LIMIT),
    )(*plan.ins)


def _grid_call(body, ins, carry, *, name, grid, in_specs, out_specs, out_shape, dims, scratch_shapes=(),
               input_output_aliases=None):
    if carry is None:
        res = pl.pallas_call(
            body, name=name, grid=grid, in_specs=list(in_specs), out_specs=list(out_specs),
            out_shape=list(out_shape), scratch_shapes=list(scratch_shapes),
            input_output_aliases=input_output_aliases or {}, compiler_params=_cp(*dims))(*ins)
        return list(res), None

    def at(pos):
        conds = [pl.program_id(k) == p for k, p in enumerate(pos)]
        out = conds[0]
        for cnd in conds[1:]:
            out = jnp.logical_and(out, cnd)
        return out

    return _carried_call(body, carry, lambda: at([0] * len(grid)), lambda: at([g - 1 for g in grid]), ins,
                         name=name, grid=grid, in_specs=in_specs, out_specs=out_specs, out_shape=out_shape,
                         scratch_shapes=scratch_shapes, input_output_aliases=input_output_aliases)


def _carried_call(body, plan, first, last, ins, *, name, grid, in_specs, out_specs, out_shape, scratch_shapes=(),
                  input_output_aliases=None):
    in_specs, out_specs, out_shape = list(in_specs), list(out_specs), list(out_shape)
    n_in, n_out, n_scr = len(in_specs), len(out_specs), len(scratch_shapes)
    c_in, c_out = len(plan.ins), len(plan.outs)
    hbm = pl.BlockSpec(memory_space=pl.ANY)

    def full_body(*refs):
        ins, c_ins = refs[:n_in], refs[n_in:n_in + c_in]
        outs = refs[n_in + c_in:n_in + c_in + n_out]
        c_outs = refs[n_in + c_in + n_out:n_in + c_in + n_out + c_out]
        scr = refs[n_in + c_in + n_out + c_out:]
        sems = scr[n_scr:]

        @pl.when(first())
        def _():
            plan.start(c_ins, c_outs, *sems)

        body(*ins, *outs, *scr[:n_scr])

        @pl.when(last())
        def _():
            plan.finish(c_ins, c_outs, *sems)

    res = pl.pallas_call(
        full_body, name=name, grid=grid,
        in_specs=in_specs + [hbm] * c_in, out_specs=out_specs + [hbm] * c_out,
        out_shape=out_shape + list(plan.outs),
        scratch_shapes=list(scratch_shapes) + _comm_scratch(plan),
        input_output_aliases=input_output_aliases or {},
        compiler_params=_cp(*(["arbitrary"] * len(grid))),
    )(*ins, *plan.ins)
    return list(res[:n_out]), list(res[n_out:])


def _gather_plan(xs):
    n = len(xs)
    ms = [v.shape[0] for v in xs]

    def tools(x_refs, o_refs, send_sems, recv_sems, local_sems):
        x, y, c = _my_place()
        me, sib = (x, y, c), (x, y, 1 - c)
        chips = [(1 - x, y), (x, 1 - y), (1 - x, 1 - y)]

        def rows(a, p):
            return o_refs[a].at[pl.ds(pl.multiple_of(_dev_index(p) * ms[a], 8), ms[a])]

        def copy(a, k, block, to, src=None):
            return pltpu.make_async_remote_copy(
                src_ref=rows(a, block) if src is None else src, dst_ref=rows(a, block),
                send_sem=send_sems.at[a * 7 + k], recv_sem=recv_sems.at[a * 7 + k],
                device_id=to, device_id_type=MESH)

        mine = [pltpu.make_async_copy(x_refs[a], rows(a, me), local_sems.at[a]) for a in range(n)]
        first = []
        for a in range(n):
            first.append(copy(a, 0, me, sib, src=x_refs[a]))
            first += [copy(a, 1 + j, me, (*chip, c), src=x_refs[a]) for j, chip in enumerate(chips)]
        return me, sib, chips, c, copy, mine, first

    def start(*refs):
        _, _, _, _, _, mine, first = tools(*refs)
        for cp in mine + first:
            cp.start()

    def finish(*refs):
        me, sib, chips, c, copy, mine, first = tools(*refs)
        passed = []
        for j, chip in enumerate(chips):
            for a in range(n):
                copy(a, 1 + j, (*chip, c), me).wait_recv()
                cp = copy(a, 4 + j, (*chip, c), sib)
                cp.start()
                passed.append(cp)
        for a in range(n):
            copy(a, 0, sib, me).wait_recv()
            for j, chip in enumerate(chips):
                copy(a, 4 + j, (*chip, 1 - c), me).wait_recv()
        for cp in first + passed:
            cp.wait_send()
        for cp in mine:
            cp.wait()

    outs = [jax.ShapeDtypeStruct((N_DEV * v.shape[0], v.shape[1]), v.dtype) for v in xs]
    return _Comm(list(xs), outs, 7 * n, n, start, finish)


def _exchange_plan(gs):
    n = len(gs)
    rs = [v.shape[0] // N_DEV for v in gs]
    flips = [(bx, by, bc) for bx in (0, 1) for by in (0, 1) for bc in (0, 1)][1:]

    def tools(g_refs, o_refs, send_sems, recv_sems, local_sems):
        x, y, c = _my_place()
        me = (x, y, c)

        def block(ref, a, p):
            return ref.at[pl.ds(_dev_index(p) * rs[a], rs[a])]

        def peer(f):
            return (1 - x if f[0] else x, 1 - y if f[1] else y, 1 - c if f[2] else c)

        def copy(a, k, to):
            return pltpu.make_async_remote_copy(
                src_ref=block(g_refs[a], a, to), dst_ref=block(o_refs[a], a, me),
                send_sem=send_sems.at[a * 7 + k], recv_sem=recv_sems.at[a * 7 + k],
                device_id=to, device_id_type=MESH)

        def arrival(a, k, frm):
            return pltpu.make_async_remote_copy(
                src_ref=block(g_refs[a], a, frm), dst_ref=block(o_refs[a], a, frm),
                send_sem=send_sems.at[a * 7 + k], recv_sem=recv_sems.at[a * 7 + k],
                device_id=frm, device_id_type=MESH)

        mine = [pltpu.make_async_copy(block(g_refs[a], a, me), block(o_refs[a], a, me), local_sems.at[a])
                for a in range(n)]
        sends = [copy(a, k, peer(f)) for a in range(n) for k, f in enumerate(flips)]
        arrivals = [arrival(a, k, peer(f)) for a in range(n) for k, f in enumerate(flips)]
        return mine, sends, arrivals

    def start(*refs):
        mine, sends, _ = tools(*refs)
        for cp in mine + sends:
            cp.start()

    def finish(*refs):
        mine, sends, arrivals = tools(*refs)
        for cp in arrivals:
            cp.wait_recv()
        for cp in sends:
            cp.wait_send()
        for cp in mine:
            cp.wait()

    outs = [jax.ShapeDtypeStruct(v.shape, v.dtype) for v in gs]
    return _Comm(list(gs), outs, 7 * n, n, start, finish)


def _rope_tables(seq):
    t = jnp.arange(seq)
    row, col = t // GRID_W, t % GRID_W
    half = HEAD_DIM // 2
    inv = ROPE_BASE ** (-jnp.arange(0, half, 2, dtype=F32) / half)
    ang_r = row.astype(F32)[:, None] * inv
    ang_c = col.astype(F32)[:, None] * inv
    ang = jnp.concatenate([ang_r, ang_r, ang_c, ang_c], axis=-1)
    return jnp.tile(jnp.cos(ang), (1, 2)), jnp.tile(jnp.sin(ang), (1, 2))


def _to_slots(w):
    return w.reshape(N_KV_HEADS, GQA_GROUP, HEAD_DIM, w.shape[1]).transpose(1, 0, 2, 3).reshape(w.shape)


def _from_slots(w):
    return w.reshape(GQA_GROUP, N_KV_HEADS, HEAD_DIM, w.shape[1]).transpose(1, 0, 2, 3).reshape(w.shape)


def _pack(vs):
    flat = jnp.concatenate([v.reshape(-1).astype(F32) for v in vs])
    total = -(-flat.shape[0] // (8 * LANES)) * (8 * LANES)
    return jnp.pad(flat, (0, total - flat.shape[0])).reshape(-1, LANES)


def _unpack(packed, like):
    flat, out, off = packed.reshape(-1), [], 0
    for v in like:
        size = math.prod(v.shape)
        out.append(flat[off:off + size].reshape(v.shape))
        off += size
    return out


def _silu(v):
    return v * jax.nn.sigmoid(v)


def kernel(x, c, ctx, c_ctx, w_mod, b_mod, norm_mix_w, w_in, q_norm_w, k_norm_w, sink_logit, conv_w, conv_b, conv_norm_w, conv_norm_b, w_out, norm_ffn_w, w_up, ffn_conv_w, ffn_conv_b, w_down, loss_target, m_c_ctx, m_w_mod, m_b_mod, m_norm_mix_w, m_w_in, m_q_norm_w, m_k_norm_w, m_sink_logit, m_conv_w, m_conv_b, m_conv_norm_w, m_conv_norm_b, m_w_out, m_norm_ffn_w, m_w_up, m_ffn_conv_w, m_ffn_conv_b, m_w_down, v_c_ctx, v_w_mod, v_b_mod, v_norm_mix_w, v_w_in, v_q_norm_w, v_k_norm_w, v_sink_logit, v_conv_w, v_conv_b, v_conv_norm_w, v_conv_norm_b, v_w_out, v_norm_ffn_w, v_w_up, v_ffn_conv_w, v_ffn_conv_b, v_w_down):
    d = D_MODEL
    seq, n_ctx = x.shape[1], ctx.shape[1]
    me = _dev_index(_my_place())
    xs, ctxs, tgt = x[0], ctx[0], loss_target[0]

    small = _pack([c[0], conv_w[0], ffn_conv_w[0]])
    w_in_t, small_all = _all_gather([w_in[0].T.astype(MXU_DT), small], hbm=True, name="gather_first")
    small_all = small_all.reshape(N_DEV, -1)
    w_in_p = jnp.concatenate([_to_slots(w_in_t[:Q_COLS]), w_in_t[Q_COLS:QKV_COLS],
                              jnp.zeros((GLU_OFF - QKV_COLS, d), MXU_DT), w_in_t[QKV_COLS:]])
    n_cw, n_fw = conv_w[0].size, ffn_conv_w[0].size
    c_all = small_all[:, :d]
    cw_all = small_all[:, d:d + n_cw].reshape(N_DEV, CONV_K, -1)
    fw_all = small_all[:, d + n_cw:d + n_cw + n_fw].reshape(N_DEV, FFN_K, -1)
    conv_w_f = cw_all.transpose(1, 0, 2).reshape(CONV_K, CONV_CH)
    ffn_w_f = fw_all.transpose(1, 0, 2).reshape(FFN_K, 2 * FFN_H)

    mcols = w_mod.shape[2]
    act = jnp.zeros((16, d), F32).at[:N_DEV].set(_silu(c_all)).at[N_DEV].set(_silu(c_ctx))
    mod_part = _mm(act, w_mod[0], 16, mcols, d, name="mod_fwd")
    mod_all = _all_gather([mod_part], hbm=False, name="gather_mod")[0]
    mod_all = mod_all.reshape(N_DEV, 16, mcols).transpose(1, 0, 2).reshape(16, 6 * d) + b_mod
    mod = lax.dynamic_slice_in_dim(mod_all, me, 1, axis=0)
    sh1, sc1, g1, sh2, sc2, g2 = [mod[:, k * d:(k + 1) * d] for k in range(6)]
    sh1c, sc1c = mod_all[N_DEV:N_DEV + 1, :d], mod_all[N_DEV:N_DEV + 1, d:2 * d]

    cos, sin = _rope_tables(seq)
    ones_c, zeros_c = jnp.ones((n_ctx, LANES), F32), jnp.zeros((n_ctx, LANES), F32)
    qk_w = jnp.concatenate([jnp.tile(q_norm_w, (1, N_Q_HEADS)), jnp.tile(k_norm_w, (1, N_KV_HEADS))], axis=1)
    kc_w = jnp.tile(k_norm_w, (1, N_KV_HEADS))

    h = _prenorm(xs, norm_mix_w, sc1, sh1, name="prenorm_mix")
    hc = _prenorm(ctxs, norm_mix_w, sc1c, sh1c, name="prenorm_ctx")
    proj = _mm(h, w_in_p, seq, IN_PAD, d, tb=True, name="proj_in")
    kv_ctx = _mm(hc, w_in_p, n_ctx, 2 * KV_COLS, d, tb=True, n0=Q_COLS, name="proj_ctx")
    qk_r = _qk_prep(proj, QK_COLS, cos, sin, qk_w, name="qk_prep")
    kc_n = _qk_prep(kv_ctx, KV_COLS, ones_c, zeros_c, kc_w, name="k_ctx_prep")
    later = _gather_plan([w_up[0].T.astype(MXU_DT), w_out[0].astype(MXU_DT), w_down[0].astype(MXU_DT)])
    (attn_o, lse), (w_up_t, w_out_f, w_down_f) = _attn_fwd(sink_logit, qk_r, proj, kc_n, kv_ctx, carry=later,
                                                           name="attn_fwd")
    w_out_f = jnp.concatenate([_to_slots(w_out_f[:Q_COLS]), w_out_f[Q_COLS:]])
    u3, u1 = _conv_fwd(proj, GLU_OFF // GLU_COLS, conv_w_f, conv_b, conv_norm_w, conv_norm_b, name="conv_fwd")
    mix = _mm(attn_o, w_out_f, seq, d, Q_COLS, name="out_attn")
    mix = _mm(u3, w_out_f, seq, d, CONV_CH, k0=Q_COLS, add=mix, name="out_conv")

    x1, h2 = _resid_prenorm(xs, mix, g1, norm_ffn_w, sc2, sh2, name="prenorm_ffn")
    up0 = _mm(h2, w_up_t, seq, 2 * FFN_H, d, tb=True, out_dtype=MXU_DT, name="ffn_up")
    act_a = _ffn_act(up0, ffn_w_f, ffn_conv_b, name="ffn_act")
    ffn = _mm(act_a, w_down_f, seq, d, FFN_H, name="ffn_down")
    loss_p, dy, dffn, dg2 = _loss_head(x1, ffn, g2, tgt, name="loss_head")

    da = _mm(dffn, w_down_f, seq, FFN_H, d, tb=True, out_dtype=MXU_DT, name="ffn_down_dx")
    gw_down = _mm(act_a, dffn, FFN_H, d, seq, ta=True, out_dtype=MXU_DT, name="ffn_down_dw")
    (dgate0, dval0, s_gate, s_val), (rx_down,) = _ffn_act_bwd(
        up0, da, ffn_w_f, ffn_conv_b, carry=_exchange_plan([gw_down]), name="ffn_act_bwd")
    dh2 = _mm(dgate0, w_up_t, seq, d, FFN_H, name="ffn_up_dx_gate")
    dh2 = _mm(dval0, w_up_t, seq, d, FFN_H, k0=FFN_H, add=dh2, name="ffn_up_dx_val")
    gw_up_t = _mm(dgate0, h2, FFN_H, d, seq, ta=True, out_dtype=MXU_DT, out_rows=2 * FFN_H, name="ffn_up_dw_gate")
    gw_up_t = _mm(dval0, h2, FFN_H, d, seq, ta=True, into=gw_up_t, m0=FFN_H, name="ffn_up_dw_val")
    (dx1, dmix, dsh2, dsc2, dnw2, dg1), _ = _norm_bwd(
        dh2, x1, norm_ffn_w, sc2, res=dy, gate=(mix, g1), name="prenorm_ffn_bwd")

    dattn = _mm(dmix, w_out_f, seq, Q_COLS, d, tb=True, out_dtype=MXU_DT, name="out_dx_attn")
    du3 = _mm(dmix, w_out_f, seq, CONV_CH, d, tb=True, n0=Q_COLS, name="out_dx_conv")
    gw_out = _mm(attn_o, dmix, Q_COLS, d, seq, ta=True, out_dtype=MXU_DT, out_rows=Q_COLS + CONV_CH,
                 name="out_dw_attn")
    gw_out = _mm(u3, dmix, CONV_CH, d, seq, ta=True, into=gw_out, m0=Q_COLS, name="out_dw_conv")
    gw_out = jnp.concatenate([_from_slots(gw_out[:Q_COLS]), gw_out[Q_COLS:]])
    (dq, dk, dv, dkc_r, dvc, dsink_rows), (rx_out,) = _attn_bwd(
        sink_logit, qk_r, proj, kc_n, kv_ctx, attn_o, lse, dattn, carry=_exchange_plan([gw_out]), name="attn_bwd")
    dproj, dqk_w = _qk_prep_bwd(proj, [dq, dk], QK_COLS, cos, sin, qk_w, tail=dv, name="qk_prep_bwd")
    (dproj, dcw8, dvec), (rx_up,) = _conv_bwd(
        proj, GLU_OFF // GLU_COLS, u1, du3, dproj, conv_w_f, conv_norm_w, conv_norm_b,
        carry=_exchange_plan([gw_up_t]), name="conv_bwd")
    dkc, dkc_w = _qk_prep_bwd(kv_ctx, [dkc_r], KV_COLS, ones_c, zeros_c, kc_w, name="k_ctx_prep_bwd")
    dkv_ctx = jnp.concatenate([dkc, dvc.astype(MXU_DT)], axis=1)
    gw_p = _mm(dproj, h, IN_PAD, d, seq, ta=True, name="proj_dw")
    gw_ctx = _mm(dkv_ctx, hc, 2 * KV_COLS, d, n_ctx, ta=True, name="proj_dw_ctx")
    gw_in_t = jnp.concatenate([_from_slots(gw_p[:Q_COLS]), gw_p[Q_COLS:QKV_COLS] + gw_ctx, gw_p[GLU_OFF:]],
                              axis=0).astype(MXU_DT)
    dh, (rx_in,) = _mm(dproj, w_in_p, seq, d, IN_PAD, carry=_exchange_plan([gw_in_t]), name="proj_dx")
    dhc = _mm(dkv_ctx, w_in_p, n_ctx, d, 2 * KV_COLS, k0=Q_COLS, name="proj_dx_ctx")
    (grad_x, dsh1, dsc1, dnw1), _ = _norm_bwd(dh, xs, norm_mix_w, sc1, res=dx1, name="prenorm_mix_bwd")
    (dsh1c, dsc1c, dnw1c), _ = _norm_bwd(dhc, ctxs, norm_mix_w, sc1c, want_dx=False, name="prenorm_ctx_bwd")

    dmod = jnp.concatenate([dsh1, dsc1, dg1, dsh2, dsc2, dg2], axis=1)
    dmod_ctx = jnp.concatenate([dsh1c, dsc1c], axis=1)
    d_qn = dqk_w[0, :Q_COLS].reshape(N_Q_HEADS, HEAD_DIM).sum(0)
    d_kn = (dqk_w[0, Q_COLS:].reshape(N_KV_HEADS, HEAD_DIM).sum(0)
            + dkc_w[0].reshape(N_KV_HEADS, HEAD_DIM).sum(0))
    d_ffn_w = jnp.concatenate([s_gate[:FFN_K], s_val[:FFN_K]], axis=1)
    d_ffn_b = jnp.concatenate([s_gate[FFN_K], s_val[FFN_K]])
    d_sink = dsink_rows.reshape(N_Q_HEADS, BLOCK).sum(1)
    summed_like = [(dnw1 + dnw1c), d_qn[None], d_kn[None], d_sink[None], dvec[0:1], dvec[1:2],
                   dvec[2:3], dnw2, d_ffn_b[None], dcw8.sum(1), d_ffn_w, loss_p[0:1, 0:1]]
    pack = _pack([dmod, dmod_ctx] + summed_like)
    pack_all = _all_gather([pack], hbm=False, name="gather_small_grads")[0]
    pack_all = pack_all.reshape(N_DEV, pack.shape[0], LANES)
    tot = _sum_parts(pack_all, name="sum_small_grads")
    (dmod_sum, dmc_sum, g_nmix, g_qn, g_kn, g_sink, g_cb, g_lw, g_lb, g_nffn, g_fb, g_cw_f, g_fw_f,
     loss_sum) = _unpack(tot, [dmod, dmod_ctx] + summed_like)
    loss = loss_sum[0, 0]
    dmod_all = pack_all.reshape(N_DEV, -1)[:, :6 * d]
    g_b_mod = dmod_sum.at[:, :2 * d].add(dmc_sum)

    lo = me * mcols
    dm_rows = jnp.zeros((16, 6 * d), F32).at[:N_DEV].set(dmod_all).at[N_DEV, :2 * d].set(dmc_sum[0])
    dm_mine = lax.dynamic_slice_in_dim(dm_rows, lo, mcols, axis=1)
    parts_mod = _mm(act, dm_mine, d, mcols, 16, ta=True, name="mod_dw")[None]
    dact_part = _mm(dm_mine[N_DEV:N_DEV + 8], w_mod[0], 8, d, mcols, tb=True, name="mod_dx_ctx")
    dact_all = _all_gather([dact_part], hbm=False, name="gather_c_ctx_grad")[0].reshape(N_DEV, 8, d)
    dact = _sum_parts(dact_all, name="sum_c_ctx_grad")[0]
    sg = jax.nn.sigmoid(c_ctx)
    g_c_ctx = dact * (sg * (1.0 + c_ctx * (1.0 - sg)))

    def stacked(rx):
        return rx.reshape(N_DEV, rx.shape[0] // N_DEV, rx.shape[1])

    g_w_in = _sum_parts(stacked(rx_in), name="sum_w_in").T[None]
    g_w_up = _sum_parts(stacked(rx_up), name="sum_w_up").T[None]
    big = {}
    big["w_in"] = _adam(w_in[0], m_w_in[0], v_w_in[0], g_w_in, name="adam_w_in")
    big["w_up"] = _adam(w_up[0], m_w_up[0], v_w_up[0], g_w_up, name="adam_w_up")
    big["w_out"] = _adam(w_out[0], m_w_out[0], v_w_out[0], stacked(rx_out), name="adam_w_out")
    big["w_down"] = _adam(w_down[0], m_w_down[0], v_w_down[0], stacked(rx_down), name="adam_w_down")
    big["w_mod"] = _adam(w_mod[0], m_w_mod[0], v_w_mod[0], parts_mod, name="adam_w_mod")

    ccols, fcols = conv_w.shape[2], ffn_conv_w.shape[2]
    g_conv_w = lax.dynamic_slice_in_dim(g_cw_f, me * ccols, ccols, axis=1)[None]
    g_ffn_w = lax.dynamic_slice_in_dim(g_fw_f, me * fcols, fcols, axis=1)[None]
    names = ["c_ctx", "b_mod", "norm_mix_w", "q_norm_w", "k_norm_w", "sink_logit", "conv_w", "conv_b",
             "conv_norm_w", "conv_norm_b", "norm_ffn_w", "ffn_conv_w", "ffn_conv_b"]
    ws = [c_ctx, b_mod, norm_mix_w, q_norm_w, k_norm_w, sink_logit, conv_w, conv_b, conv_norm_w, conv_norm_b,
          norm_ffn_w, ffn_conv_w, ffn_conv_b]
    msm = [m_c_ctx, m_b_mod, m_norm_mix_w, m_q_norm_w, m_k_norm_w, m_sink_logit, m_conv_w, m_conv_b,
           m_conv_norm_w, m_conv_norm_b, m_norm_ffn_w, m_ffn_conv_w, m_ffn_conv_b]
    vsm = [v_c_ctx, v_b_mod, v_norm_mix_w, v_q_norm_w, v_k_norm_w, v_sink_logit, v_conv_w, v_conv_b,
           v_conv_norm_w, v_conv_norm_b, v_norm_ffn_w, v_ffn_conv_w, v_ffn_conv_b]
    gsm = [g_c_ctx, g_b_mod, g_nmix, g_qn, g_kn, g_sink, g_conv_w, g_cb, g_lw, g_lb, g_nffn, g_ffn_w, g_fb]
    deltas, new_ms, new_vs = _adam_many(ws, msm, vsm, gsm, name="adam_small")
    sm = {nm: vals for nm, vals in zip(names, zip(gsm, deltas, new_ms, new_vs))}

    def out4(nm):
        if nm in sm:
            return sm[nm]
        return tuple(t[None] for t in big[nm])

    order = ["c_ctx", "w_mod", "b_mod", "norm_mix_w", "w_in", "q_norm_w", "k_norm_w", "sink_logit", "conv_w",
             "conv_b", "conv_norm_w", "conv_norm_b", "w_out", "norm_ffn_w", "w_up", "ffn_conv_w", "ffn_conv_b",
             "w_down"]
    quads = [out4(nm) for nm in order]
    return (loss, grad_x[None], *[q[0] for q in quads], *[q[1] for q in quads],
            *[q[2] for q in quads], *[q[3] for q in quads])
```

```python
import math
from typing import Callable, NamedTuple

import jax
import jax.numpy as jnp
from jax import lax
from jax.experimental import pallas as pl
from jax.experimental.pallas import tpu as pltpu

F32 = jnp.float32
MXU_DT = jnp.bfloat16

D_MODEL = 1024
GRID_W = 64
HEAD_DIM = 64
N_Q_HEADS = 8
N_KV_HEADS = 2
GQA_GROUP = 4
WINDOW = 128
BLOCK = 128
Q_COLS = 512
KV_COLS = 128
QK_COLS = Q_COLS + KV_COLS
QKV_COLS = Q_COLS + 2 * KV_COLS
CONV_CH = 512
GLU_COLS = 2 * CONV_CH
IN_COLS = QKV_COLS + GLU_COLS
CONV_K = 31
CONV_PAD = 15
FFN_H = 2816
FFN_K = 3
ROPE_BASE = 10000.0
EPS = 1e-6
NEG_INF = -1e30
N_DEV = 8
HALO = 16
LANES = 128

ADAM_LR = 0.001
ADAM_B1 = 0.9
ADAM_B2 = 0.999
ADAM_EPS = 1e-08
ADAM_WD = 0.01
ADAM_STEP = 10

MESH_AXES = ("x", "y", "c")
MESH = pl.DeviceIdType.MESH
VMEM_LIMIT = 56 << 20
MM_VMEM_BUDGET = 44 << 20
GLU_OFF = 1024
IN_PAD = GLU_OFF + GLU_COLS


def _cp(*dims):
    return pltpu.CompilerParams(dimension_semantics=dims or None, vmem_limit_bytes=VMEM_LIMIT)


def _pick(n, prefs, also=0):
    for p in prefs:
        if n % p == 0 and also % p == 0:
            return p
    raise ValueError((n, prefs, also))


def _row(ts, w, col=0):
    return pl.BlockSpec((ts, w), lambda i: (i, col))


def _vec(w):
    return pl.BlockSpec((1, w), lambda i: (0, 0))


def _colsum(v):
    return jnp.sum(v, axis=0, keepdims=True)


def _sigmoid(v):
    return 0.5 * jnp.tanh(0.5 * v) + 0.5


def _mm(a, b, m, n, k, *, ta=False, tb=False, n0=0, k0=0, add=None, out_dtype=F32, into=None, m0=0,
        out_rows=None, a2=None, carry=None, name):
    has_add, has_a2 = add is not None, a2 is not None
    assert not (has_a2 and (ta or tb))
    if into is not None:
        out_dtype = into.dtype
    sa, sb, so = a.dtype.itemsize, b.dtype.itemsize, jnp.dtype(out_dtype).itemsize
    sadd = add.dtype.itemsize if has_add else 0
    na = 2 if has_a2 else 1

    def fits(tm, tn):
        return 2 * (na * k * (tm * sa + tn * sb) + tm * tn * (so + sadd)) <= MM_VMEM_BUDGET

    tms = [m] if m <= 1024 else [t for t in (1024, 1408, 768, 512, 256, 128) if m % t == 0]
    tns = [t for t in ((1024, 512, 256, 128) if ta else (1408, 512, 256, 128)) if n % t == 0 and n0 % t == 0]
    tm, tn = next((tm, tn) for tm in tms for tn in tns if fits(tm, tn))
    a_spec = (pl.BlockSpec((k, tm), lambda i, j: (0, i)) if ta else pl.BlockSpec((tm, k), lambda i, j: (i, 0)))
    nb0 = n0 // tn
    if tb:
        assert k0 == 0
        b_spec = pl.BlockSpec((tn, k), lambda i, j: (j + nb0, 0))
    else:
        assert k0 % (na * k) == 0, (k0, k)
        kb0 = k0 // (na * k)
        b_spec = pl.BlockSpec((na * k, tn), lambda i, j: (kb0, j + nb0))
    assert m0 % tm == 0, (m0, tm)
    mb0 = m0 // tm
    o_spec = pl.BlockSpec((tm, tn), lambda i, j: (i + mb0, j))
    dims = (((0 if ta else 1,), (1 if tb else 0,)), ((), ()))

    def body(*refs):
        a_ref, b_ref, o_ref = refs[0], refs[na], refs[-1]
        if has_a2:
            res = (jnp.dot(a_ref[...].astype(MXU_DT), b_ref[0:k, :].astype(MXU_DT), preferred_element_type=F32)
                   + jnp.dot(refs[1][...].astype(MXU_DT), b_ref[k:2 * k, :].astype(MXU_DT),
                             preferred_element_type=F32))
        else:
            res = lax.dot_general(a_ref[...].astype(MXU_DT), b_ref[...].astype(MXU_DT), dims,
                                  preferred_element_type=F32)
        if has_add:
            res = res + refs[na + 1][...].astype(F32)
        o_ref[...] = res.astype(o_ref.dtype)

    ins = [a] + ([a2] if has_a2 else []) + [b] + ([add] if has_add else []) + ([into] if into is not None else [])
    specs = ([a_spec] * na + [b_spec] + ([pl.BlockSpec((tm, tn), lambda i, j: (i, j))] if has_add else [])
             + ([pl.BlockSpec(memory_space=pl.ANY)] if into is not None else []))
    out_shape = (jax.ShapeDtypeStruct(into.shape, into.dtype) if into is not None
                 else jax.ShapeDtypeStruct((out_rows or m, n), out_dtype))
    (out,), carried = _grid_call(
        body, ins, carry, name=name, grid=(m // tm, n // tn), in_specs=specs, out_specs=[o_spec],
        out_shape=[out_shape], input_output_aliases={len(ins) - 1: 0} if into is not None else None,
        dims=("parallel", "parallel"))
    return out if carry is None else (out, carried)


def _rms_stats(xv):
    r = lax.rsqrt(jnp.mean(xv * xv, axis=-1, keepdims=True) + EPS)
    return r, xv * r


def _prenorm(x, nw, sc, sh, *, name):
    rows, d = x.shape
    ts = min(rows, 256)

    def body(x_ref, nw_ref, sc_ref, sh_ref, h_ref):
        _, xn = _rms_stats(x_ref[...])
        h_ref[...] = ((xn * nw_ref[...]) * (1.0 + sc_ref[...]) + sh_ref[...]).astype(h_ref.dtype)

    return pl.pallas_call(
        body, name=name, grid=(rows // ts,),
        in_specs=[_row(ts, d), _vec(d), _vec(d), _vec(d)], out_specs=_row(ts, d),
        out_shape=jax.ShapeDtypeStruct((rows, d), MXU_DT), compiler_params=_cp("parallel"),
    )(x, nw, sc, sh)


def _resid_prenorm(x, mix, g1, nw, sc, sh, *, name):
    rows, d = x.shape
    ts = min(rows, 256)

    def body(x_ref, mix_ref, g_ref, nw_ref, sc_ref, sh_ref, x1_ref, h_ref):
        x1 = x_ref[...] + g_ref[...] * mix_ref[...]
        x1_ref[...] = x1
        _, xn = _rms_stats(x1)
        h_ref[...] = ((xn * nw_ref[...]) * (1.0 + sc_ref[...]) + sh_ref[...]).astype(h_ref.dtype)

    return pl.pallas_call(
        body, name=name, grid=(rows // ts,),
        in_specs=[_row(ts, d), _row(ts, d), _vec(d), _vec(d), _vec(d), _vec(d)],
        out_specs=[_row(ts, d), _row(ts, d)],
        out_shape=[jax.ShapeDtypeStruct((rows, d), F32), jax.ShapeDtypeStruct((rows, d), MXU_DT)],
        compiler_params=_cp("parallel"),
    )(x, mix, g1, nw, sc, sh)


def _loss_head(x1, ffn, g2, target, *, name):
    rows, d = x1.shape
    ts = min(rows, 256)

    def body(x1_ref, f_ref, g_ref, t_ref, loss_ref, dy_ref, dffn_ref, dg_ref):
        i = pl.program_id(0)
        f = f_ref[...]
        e = x1_ref[...] + g_ref[...] * f - t_ref[...]
        per_tok = jnp.mean(e * e, axis=-1, keepdims=True)
        part = 0.5 * jnp.sum(per_tok, axis=0, keepdims=True)
        dy = e * (1.0 / d)
        dy_ref[...] = dy
        dffn_ref[...] = (dy * g_ref[...]).astype(dffn_ref.dtype)

        @pl.when(i == 0)
        def _():
            loss_ref[...] = jnp.zeros_like(loss_ref)
            dg_ref[...] = jnp.zeros_like(dg_ref)

        loss_ref[...] += jnp.broadcast_to(part, loss_ref.shape)
        dg_ref[...] += _colsum(dy * f)

    return pl.pallas_call(
        body, name=name, grid=(rows // ts,),
        in_specs=[_row(ts, d), _row(ts, d), _vec(d), _row(ts, d)],
        out_specs=[pl.BlockSpec((8, LANES), lambda i: (0, 0)), _row(ts, d), _row(ts, d), _vec(d)],
        out_shape=[jax.ShapeDtypeStruct((8, LANES), F32), jax.ShapeDtypeStruct((rows, d), F32),
                   jax.ShapeDtypeStruct((rows, d), MXU_DT), jax.ShapeDtypeStruct((1, d), F32)],
        compiler_params=_cp("arbitrary"),
    )(x1, ffn, g2, target)


def _norm_bwd(dh, xin, nw, sc, *, res=None, gate=None, want_dx=True, carry=None, name):
    rows, d = xin.shape
    ts = min(rows, 256)
    has_res, has_gate = res is not None, gate is not None

    def body(*refs):
        it = iter(refs)
        dh_ref, x_ref, nw_ref, sc_ref = next(it), next(it), next(it), next(it)
        res_ref = next(it) if has_res else None
        gated_ref, g_ref = (next(it), next(it)) if has_gate else (None, None)
        dx_ref = next(it) if want_dx else None
        dgx_ref = next(it) if has_gate else None
        dsh_ref, dsc_ref, dnw_ref = next(it), next(it), next(it)
        dg_ref = next(it) if has_gate else None
        i = pl.program_id(0)
        dhv = dh_ref[...]
        r, xn = _rms_stats(x_ref[...])
        dn = dhv * (1.0 + sc_ref[...])

        @pl.when(i == 0)
        def _():
            dsh_ref[...] = jnp.zeros_like(dsh_ref)
            dsc_ref[...] = jnp.zeros_like(dsc_ref)
            dnw_ref[...] = jnp.zeros_like(dnw_ref)
            if has_gate:
                dg_ref[...] = jnp.zeros_like(dg_ref)

        dsh_ref[...] += _colsum(dhv)
        dsc_ref[...] += _colsum(dhv * (xn * nw_ref[...]))
        dnw_ref[...] += _colsum(dn * xn)
        if want_dx:
            dxn = dn * nw_ref[...]
            dx = r * (dxn - xn * jnp.mean(dxn * xn, axis=-1, keepdims=True))
            if has_res:
                dx = dx + res_ref[...]
            dx_ref[...] = dx
            if has_gate:
                dgx_ref[...] = (dx * g_ref[...]).astype(dgx_ref.dtype)
                dg_ref[...] += _colsum(dx * gated_ref[...])

    ins = [dh, xin, nw, sc] + ([res] if has_res else []) + (list(gate) if has_gate else [])
    in_specs = ([_row(ts, d), _row(ts, d), _vec(d), _vec(d)] + ([_row(ts, d)] if has_res else [])
                + ([_row(ts, d), _vec(d)] if has_gate else []))
    out_specs, out_shape = [], []
    if want_dx:
        out_specs.append(_row(ts, d)); out_shape.append(jax.ShapeDtypeStruct((rows, d), F32))
    if has_gate:
        out_specs.append(_row(ts, d)); out_shape.append(jax.ShapeDtypeStruct((rows, d), MXU_DT))
    for _ in range(3 + int(has_gate)):
        out_specs.append(_vec(d)); out_shape.append(jax.ShapeDtypeStruct((1, d), F32))
    return _grid_call(body, ins, carry, name=name, grid=(rows // ts,), in_specs=in_specs, out_specs=out_specs,
                      out_shape=out_shape, dims=("arbitrary",))


def _group_sum(v, g):
    hi = v.astype(MXU_DT)
    lo = (v - hi.astype(F32)).astype(MXU_DT)
    return (jnp.dot(hi, g, preferred_element_type=F32) + jnp.dot(lo, g, preferred_element_type=F32))


def _rot(v):
    lane = lax.broadcasted_iota(jnp.int32, v.shape, 1)
    first = (lane & 31) < 16
    return jnp.where(first, -pltpu.roll(v, LANES - 16, 1), pltpu.roll(v, 16, 1))


def _head_group_matrix():
    r = jnp.arange(LANES) // HEAD_DIM
    return (r[:, None] == r[None, :]).astype(MXU_DT)


def _qk_prep(xin, width, cos, sin, w, *, name):
    rows = xin.shape[0]
    ts = min(rows, 256)
    nch = width // LANES

    def body(x_ref, cos_ref, sin_ref, w_ref, g_ref, o_ref):
        cs, sn, g = cos_ref[...], sin_ref[...], g_ref[...]
        for ch in range(nch):
            sl = slice(ch * LANES, (ch + 1) * LANES)
            xv = x_ref[:, sl]
            r = lax.rsqrt(_group_sum(xv * xv, g) * (1.0 / HEAD_DIM) + EPS)
            yw = (xv * r) * w_ref[:, sl]
            o_ref[:, sl] = (yw * cs + _rot(yw) * sn).astype(o_ref.dtype)

    return pl.pallas_call(
        body, name=name, grid=(rows // ts,),
        in_specs=[_row(ts, width), _row(ts, LANES), _row(ts, LANES), _vec(width),
                  pl.BlockSpec((LANES, LANES), lambda i: (0, 0))],
        out_specs=_row(ts, width),
        out_shape=jax.ShapeDtypeStruct((rows, width), MXU_DT), compiler_params=_cp("parallel"),
    )(xin, cos, sin, w, _head_group_matrix())


def _qk_prep_bwd(xin, douts, width, cos, sin, w, *, tail=None, name):
    rows = xin.shape[0]
    ts = min(rows, 256)
    nch = width // LANES
    has_tail = tail is not None
    nd = len(douts)
    assert sum(v.shape[1] for v in douts) == width
    src = [(k, c) for k, v in enumerate(douts) for c in range(v.shape[1] // LANES)]

    def body(*refs):
        x_ref, d_refs = refs[0], refs[1:1 + nd]
        t_ref = refs[1 + nd] if has_tail else None
        cos_ref, sin_ref, w_ref, g_ref, dx_ref, dw_ref = refs[1 + nd + int(has_tail):]
        i = pl.program_id(0)
        cs, sn, g = cos_ref[...], sin_ref[...], g_ref[...]

        @pl.when(i == 0)
        def _():
            dw_ref[...] = jnp.zeros_like(dw_ref)

        if has_tail:
            dx_ref[:, width:width + LANES] = t_ref[...].astype(dx_ref.dtype)
            dx_ref[:, width + LANES:] = jnp.zeros((ts, GLU_OFF - width - LANES), dx_ref.dtype)

        for ch in range(nch):
            sl = slice(ch * LANES, (ch + 1) * LANES)
            xv = x_ref[:, sl]
            dv = d_refs[src[ch][0]][:, src[ch][1] * LANES:(src[ch][1] + 1) * LANES].astype(F32)
            r = lax.rsqrt(_group_sum(xv * xv, g) * (1.0 / HEAD_DIM) + EPS)
            n = xv * r
            dyw = dv * cs - _rot(dv * sn)
            dw_ref[:, sl] += _colsum(dyw * n)
            dn = dyw * w_ref[:, sl]
            gm = _group_sum(dn * n, g) * (1.0 / HEAD_DIM)
            dx_ref[:, sl] = (r * (dn - n * gm)).astype(dx_ref.dtype)

    ins = [xin] + list(douts) + ([tail] if has_tail else []) + [cos, sin, w, _head_group_matrix()]
    in_specs = ([_row(ts, width)] + [_row(ts, v.shape[1]) for v in douts] + ([_row(ts, LANES)] if has_tail else [])
                + [_row(ts, LANES), _row(ts, LANES), _vec(width), pl.BlockSpec((LANES, LANES), lambda i: (0, 0))])
    out_w, arr_w = (GLU_OFF, IN_PAD) if has_tail else (width, width)
    return pl.pallas_call(
        body, name=name, grid=(rows // ts,), in_specs=in_specs,
        out_specs=[_row(ts, out_w), _vec(width)],
        out_shape=[jax.ShapeDtypeStruct((rows, arr_w), MXU_DT), jax.ShapeDtypeStruct((1, width), F32)],
        compiler_params=_cp("arbitrary"),
    )(*ins)


GB = GQA_GROUP * BLOCK
WIN = 3 * BLOCK
ATT_STEP = 2


def _win_start(i, seq):
    return pl.multiple_of(jnp.clip((i - 1) * BLOCK, 0, seq - WIN), BLOCK)


def _attn_mask(i, start):
    qpos = i * BLOCK + (lax.broadcasted_iota(jnp.int32, (GB, WIN), 0) & (BLOCK - 1))
    kpos = start + lax.broadcasted_iota(jnp.int32, (GB, WIN), 1)
    return jnp.abs(qpos - kpos) <= WINDOW


def _sink_col(sink_ref, kv):
    return jnp.concatenate(
        [jnp.full((BLOCK, 1), sink_ref[0, kv * GQA_GROUP + g], F32) for g in range(GQA_GROUP)], axis=0)


def _stack_slots(ref, rows):
    return jnp.concatenate([ref[rows, g * LANES:(g + 1) * LANES] for g in range(GQA_GROUP)], axis=0)


def _kv_lanes(kv):
    lane = lax.broadcasted_iota(jnp.int32, (1, LANES), 1)
    return (lane < HEAD_DIM) if kv == 0 else (lane >= HEAD_DIM)


def _keep(mask, v):
    return jnp.where(mask, v, jnp.zeros_like(v))


_NT = (((1,), (1,)), ((), ()))
_TN = (((0,), (0,)), ((), ()))


def _attn_specs(seq, n_ctx):
    qs = pl.BlockSpec((ATT_STEP * BLOCK, Q_COLS), lambda i: (i, 0))
    ks = pl.BlockSpec((seq, KV_COLS), lambda i: (0, Q_COLS // KV_COLS))
    vs = pl.BlockSpec((seq, KV_COLS), lambda i: (0, QK_COLS // KV_COLS))
    kcs = pl.BlockSpec((n_ctx, KV_COLS), lambda i: (0, 0))
    vcs = pl.BlockSpec((n_ctx, KV_COLS), lambda i: (0, 1))
    ls = pl.BlockSpec((ATT_STEP, N_KV_HEADS * GB, 1), lambda i: (i, 0, 0))
    return qs, ks, vs, kcs, vcs, ls


def _attn_fwd(sink, qk, proj, kc, kv_ctx, *, carry=None, name):
    seq, n_ctx = qk.shape[0], kc.shape[0]
    scale = 1.0 / math.sqrt(HEAD_DIM)

    def one_block(blk, sub, sink_ref, q_ref, k_ref, v_ref, kcw, vcw, o_ref, lse_ref):
        rows = slice(sub * BLOCK, (sub + 1) * BLOCK)
        start = _win_start(blk, seq)
        valid = _attn_mask(blk, start)
        qs = _stack_slots(q_ref, rows)
        kw, vw = k_ref[pl.ds(start, WIN), :], v_ref[pl.ds(start, WIN), :].astype(MXU_DT)
        o_all = jnp.zeros((GB, LANES), F32)
        for kv in range(N_KV_HEADS):
            mine = _kv_lanes(kv)
            s_loc = lax.dot_general(qs, _keep(mine, kw), _NT, preferred_element_type=F32) * scale
            s_loc = jnp.where(valid, s_loc, NEG_INF)
            s_ctx = lax.dot_general(qs, _keep(mine, kcw), _NT, preferred_element_type=F32) * scale
            sk = _sink_col(sink_ref, kv)
            m = jnp.maximum(jnp.maximum(jnp.max(s_loc, axis=-1, keepdims=True),
                                        jnp.max(s_ctx, axis=-1, keepdims=True)), sk)
            p_loc = jnp.exp(s_loc - m)
            p_ctx = jnp.exp(s_ctx - m)
            l = (jnp.sum(p_loc, axis=-1, keepdims=True) + jnp.sum(p_ctx, axis=-1, keepdims=True)
                 + jnp.exp(sk - m))
            o_all = o_all + (jnp.dot(p_loc.astype(MXU_DT), _keep(mine, vw), preferred_element_type=F32)
                             + jnp.dot(p_ctx.astype(MXU_DT), _keep(mine, vcw), preferred_element_type=F32)) / l
            lse_ref[sub, kv * GB:(kv + 1) * GB, :] = m + jnp.log(l)
        for g in range(GQA_GROUP):
            o_ref[rows, g * LANES:(g + 1) * LANES] = o_all[g * BLOCK:(g + 1) * BLOCK].astype(o_ref.dtype)

    def body(sink_ref, q_ref, k_ref, v_ref, kc_ref, vc_ref, o_ref, lse_ref):
        i = pl.program_id(0)
        kcw, vcw = kc_ref[...], vc_ref[...].astype(MXU_DT)
        for sub in range(ATT_STEP):
            one_block(i * ATT_STEP + sub, sub, sink_ref, q_ref, k_ref, v_ref, kcw, vcw, o_ref, lse_ref)

    qs, ks, vs, kcs, vcs, ls = _attn_specs(seq, n_ctx)
    return _grid_call(
        body, [sink, qk, qk, proj, kc, kv_ctx], carry, name=name, grid=(seq // (ATT_STEP * BLOCK),),
        in_specs=[pl.BlockSpec(memory_space=pltpu.SMEM), qs, ks, vs, kcs, vcs],
        out_specs=[qs, ls],
        out_shape=[jax.ShapeDtypeStruct((seq, Q_COLS), MXU_DT),
                   jax.ShapeDtypeStruct((seq // BLOCK, N_KV_HEADS * GB, 1), F32)],
        dims=("parallel",))


def _attn_bwd(sink, qk, proj, kc, kv_ctx, o, lse, do, *, carry=None, name):
    seq, n_ctx = qk.shape[0], kc.shape[0]
    scale = 1.0 / math.sqrt(HEAD_DIM)

    def body(sink_ref, q_ref, k_ref, v_ref, kc_ref, vc_ref, o_ref, lse_ref, do_ref,
             dq_ref, dk_ref, dv_ref, dkc_ref, dvc_ref, ds_ref):
        i = pl.program_id(0)

        @pl.when(i == 0)
        def _():
            dk_ref[...] = jnp.zeros_like(dk_ref)
            dv_ref[...] = jnp.zeros_like(dv_ref)
            dkc_ref[...] = jnp.zeros_like(dkc_ref)
            dvc_ref[...] = jnp.zeros_like(dvc_ref)
            ds_ref[...] = jnp.zeros_like(ds_ref)

        kcw, vcw = kc_ref[...], vc_ref[...].astype(MXU_DT)
        dkc, dvc = jnp.zeros((n_ctx, LANES), F32), jnp.zeros((n_ctx, LANES), F32)
        for sub in range(ATT_STEP):
            dkc_s, dvc_s = one_block(i * ATT_STEP + sub, sub, sink_ref, q_ref, k_ref, v_ref, kcw, vcw, o_ref,
                                     lse_ref, do_ref, dq_ref, dk_ref, dv_ref, ds_ref)
            dkc, dvc = dkc + dkc_s, dvc + dvc_s
        dkc_ref[...] += dkc
        dvc_ref[...] += dvc

    def one_block(blk, sub, sink_ref, q_ref, k_ref, v_ref, kcw, vcw, o_ref, lse_ref, do_ref,
                  dq_ref, dk_ref, dv_ref, ds_ref):
        qrows = slice(sub * BLOCK, (sub + 1) * BLOCK)
        start = _win_start(blk, seq)
        valid = _attn_mask(blk, start)
        win = pl.ds(start, WIN)
        qs, dos = _stack_slots(q_ref, qrows), _stack_slots(do_ref, qrows)
        do_o = dos.astype(F32) * _stack_slots(o_ref, qrows).astype(F32)
        kw, vw = k_ref[win, :], v_ref[win, :].astype(MXU_DT)
        dq = jnp.zeros((GB, LANES), F32)
        dk, dv = jnp.zeros((WIN, LANES), F32), jnp.zeros((WIN, LANES), F32)
        dkc, dvc = jnp.zeros((n_ctx, LANES), F32), jnp.zeros((n_ctx, LANES), F32)
        for kv in range(N_KV_HEADS):
            mine = _kv_lanes(kv)
            rows = slice(kv * GB, (kv + 1) * GB)
            lse_s = lse_ref[sub, rows, :]
            delta = jnp.sum(_keep(mine, do_o), axis=-1, keepdims=True)
            kz, vz, kcz, vcz = _keep(mine, kw), _keep(mine, vw), _keep(mine, kcw), _keep(mine, vcw)
            s_loc = lax.dot_general(qs, kz, _NT, preferred_element_type=F32) * scale
            s_loc = jnp.where(valid, s_loc, NEG_INF)
            s_ctx = lax.dot_general(qs, kcz, _NT, preferred_element_type=F32) * scale
            p_loc = jnp.exp(s_loc - lse_s)
            p_ctx = jnp.exp(s_ctx - lse_s)
            p_sink = jnp.exp(_sink_col(sink_ref, kv) - lse_s)
            dp_loc = lax.dot_general(dos, vz, _NT, preferred_element_type=F32)
            dp_ctx = lax.dot_general(dos, vcz, _NT, preferred_element_type=F32)
            ds_loc = (p_loc * (dp_loc - delta) * scale).astype(MXU_DT)
            ds_ctx = (p_ctx * (dp_ctx - delta) * scale).astype(MXU_DT)
            dq = dq + (jnp.dot(ds_loc, kz, preferred_element_type=F32)
                       + jnp.dot(ds_ctx, kcz, preferred_element_type=F32))
            dk = dk + _keep(mine, lax.dot_general(ds_loc, qs, _TN, preferred_element_type=F32))
            dv = dv + _keep(mine, lax.dot_general(p_loc.astype(MXU_DT), dos, _TN, preferred_element_type=F32))
            dkc = dkc + _keep(mine, lax.dot_general(ds_ctx, qs, _TN, preferred_element_type=F32))
            dvc = dvc + _keep(mine, lax.dot_general(p_ctx.astype(MXU_DT), dos, _TN, preferred_element_type=F32))
            ds_ref[rows, :] += -(p_sink * delta)
        for g in range(GQA_GROUP):
            dq_ref[qrows, g * LANES:(g + 1) * LANES] = dq[g * BLOCK:(g + 1) * BLOCK]
        dk_ref[win, :] += dk
        dv_ref[win, :] += dv
        return dkc, dvc

    qs, ks, vs, kcs, vcs, ls = _attn_specs(seq, n_ctx)
    whole = lambda r, c: pl.BlockSpec((r, c), lambda i: (0, 0))
    return _grid_call(
        body, [sink, qk, qk, proj, kc, kv_ctx, o, lse, do], carry, name=name, grid=(seq // (ATT_STEP * BLOCK),),
        in_specs=[pl.BlockSpec(memory_space=pltpu.SMEM), qs, ks, vs, kcs, vcs, qs, ls, qs],
        out_specs=[qs, whole(seq, KV_COLS), whole(seq, KV_COLS), whole(n_ctx, KV_COLS), whole(n_ctx, KV_COLS),
                   whole(N_KV_HEADS * GB, 1)],
        out_shape=[jax.ShapeDtypeStruct((seq, Q_COLS), F32), jax.ShapeDtypeStruct((seq, KV_COLS), F32),
                   jax.ShapeDtypeStruct((seq, KV_COLS), F32), jax.ShapeDtypeStruct((n_ctx, KV_COLS), F32),
                   jax.ShapeDtypeStruct((n_ctx, KV_COLS), F32), jax.ShapeDtypeStruct((N_KV_HEADS * GB, 1), F32)],
        dims=("arbitrary",))


def _halo_specs(ts, w, rows, col=0):
    per = ts // HALO
    last = rows // HALO - 1
    return [pl.BlockSpec((HALO, w), lambda i: (jnp.maximum(i * per - 1, 0), col)),
            pl.BlockSpec((ts, w), lambda i: (i, col)),
            pl.BlockSpec((HALO, w), lambda i: (jnp.minimum((i + 1) * per, last), col))]


def _glu(v):
    return v[:, :CONV_CH] * _sigmoid(v[:, CONV_CH:])


def _ln_stats(u):
    mu = jnp.mean(u, axis=-1, keepdims=True)
    xc = u - mu
    rstd = lax.rsqrt(jnp.mean(xc * xc, axis=-1, keepdims=True) + EPS)
    return xc * rstd, rstd


def _phases(ext_ref, ph_ref):
    n = ph_ref.shape[1]
    for b in range(1, 8):
        ph_ref[b - 1] = ext_ref[b:b + n, :]


def _window(ext_ref, ph_ref, o, n, cs):
    a, b = divmod(o, 8)
    src = ext_ref if b == 0 else ph_ref.at[b - 1]
    return src[8 * a:8 * a + n, cs]


def _conv_fwd(glu, col, cw, cb, lw, lb, *, carry=None, name):
    rows = glu.shape[0]
    ts = min(rows, 256)
    nt = rows // ts

    te = ts + 2 * HALO

    def body(gp_ref, g_ref, gn_ref, cw_ref, cb_ref, lw_ref, lb_ref, u3_ref, u1_ref, ext_ref, ph_ref):
        i = pl.program_id(0)
        ext_ref[0:HALO, :] = jnp.where(i > 0, _glu(gp_ref[...]), 0.0)
        ext_ref[HALO:HALO + ts, :] = _glu(g_ref[...])
        ext_ref[HALO + ts:, :] = jnp.where(i < nt - 1, _glu(gn_ref[...]), 0.0)
        _phases(ext_ref, ph_ref)
        for c in range(CONV_CH // LANES):
            cs = slice(c * LANES, (c + 1) * LANES)
            acc = jnp.broadcast_to(cb_ref[:, cs], (ts, LANES))
            for j in range(CONV_K):
                acc = acc + cw_ref[j:j + 1, cs] * _window(ext_ref, ph_ref, HALO - CONV_PAD + j, ts, cs)
            u1_ref[:, cs] = acc
        xh, _ = _ln_stats(u1_ref[...])
        u2 = xh * lw_ref[...] + lb_ref[...]
        u3_ref[...] = (u2 * _sigmoid(u2)).astype(u3_ref.dtype)

    full = lambda shape: pl.BlockSpec(shape, lambda i: (0,) * len(shape))
    return _grid_call(
        body, [glu, glu, glu, cw, cb, lw, lb], carry, name=name, grid=(nt,),
        in_specs=_halo_specs(ts, GLU_COLS, rows, col) + [full((CONV_K, CONV_CH))] + [_vec(CONV_CH)] * 3,
        out_specs=[_row(ts, CONV_CH), _row(ts, CONV_CH)],
        out_shape=[jax.ShapeDtypeStruct((rows, CONV_CH), MXU_DT), jax.ShapeDtypeStruct((rows, CONV_CH), F32)],
        scratch_shapes=[pltpu.VMEM((te, CONV_CH), F32), pltpu.VMEM((7, te - 8, CONV_CH), F32)],
        dims=("parallel",))


def _conv_bwd(glu, col, u1, du3, dproj, cw, lw, lb, *, carry=None, name):
    rows = glu.shape[0]
    ts = min(rows, 256)
    nt = rows // ts
    te = ts + 2 * HALO

    def du1_of(u1v, du3v, lw_v, lb_v):
        xh, rstd = _ln_stats(u1v)
        u2 = xh * lw_v + lb_v
        sg = _sigmoid(u2)
        du2 = du3v * (sg * (1.0 + u2 * (1.0 - sg)))
        dxh = du2 * lw_v
        du1 = rstd * (dxh - jnp.mean(dxh, axis=-1, keepdims=True)
                      - xh * jnp.mean(dxh * xh, axis=-1, keepdims=True))
        return du1, du2, xh

    half = ts // 2

    def body(gp_ref, g_ref, gn_ref, up_ref, u_ref, un_ref, dp_ref, d_ref, dn_ref, cw_ref, lw_ref, lb_ref,
             _, dglu_ref, dcw_ref, dvec_ref, u0_ref, du1_ref, pu_ref, pd_ref, du0_ref):
        i = pl.program_id(0)
        lw_v, lb_v = lw_ref[...], lb_ref[...]

        @pl.when(i == 0)
        def _():
            dcw_ref[...] = jnp.zeros_like(dcw_ref)
            dvec_ref[...] = jnp.zeros_like(dvec_ref)

        gv = g_ref[...]
        u0_ref[0:HALO, :] = jnp.where(i > 0, _glu(gp_ref[...]), 0.0)
        u0_ref[HALO:HALO + ts, :] = _glu(gv)
        u0_ref[HALO + ts:, :] = jnp.where(i < nt - 1, _glu(gn_ref[...]), 0.0)
        d_prev, _, _ = du1_of(up_ref[...], dp_ref[...], lw_v, lb_v)
        d_main, du2, xh = du1_of(u_ref[...], d_ref[...], lw_v, lb_v)
        d_next, _, _ = du1_of(un_ref[...], dn_ref[...], lw_v, lb_v)
        du1_ref[0:HALO, :] = jnp.where(i > 0, d_prev, 0.0)
        du1_ref[HALO:HALO + ts, :] = d_main
        du1_ref[HALO + ts:, :] = jnp.where(i < nt - 1, d_next, 0.0)

        rid = lax.broadcasted_iota(jnp.int32, (8, CONV_CH), 0)
        dvec_ref[...] += (jnp.where(rid == 0, _colsum(d_main), 0.0)
                          + jnp.where(rid == 1, _colsum(du2 * xh), 0.0)
                          + jnp.where(rid == 2, _colsum(du2), 0.0))
        _phases(u0_ref, pu_ref)
        _phases(du1_ref, pd_ref)
        for c in range(CONV_CH // LANES):
            cs = slice(c * LANES, (c + 1) * LANES)
            for r0 in (0, half):
                dm = du1_ref[HALO + r0:HALO + r0 + half, cs]
                acc = jnp.zeros((half, LANES), F32)
                for j in range(CONV_K):
                    acc = acc + cw_ref[j:j + 1, cs] * _window(du1_ref, pd_ref, r0 + HALO + CONV_PAD - j, half, cs)
                    prod = dm * _window(u0_ref, pu_ref, r0 + HALO - CONV_PAD + j, half, cs)
                    dcw_ref[j, :, cs] += jnp.sum(prod.reshape(half // 8, 8, LANES), axis=0)
                du0_ref[r0:r0 + half, cs] = acc
        du0 = du0_ref[...]
        ga, sg = gv[:, :CONV_CH], _sigmoid(gv[:, CONV_CH:])
        dglu_ref[:, :CONV_CH] = (du0 * sg).astype(dglu_ref.dtype)
        dglu_ref[:, CONV_CH:] = (du0 * ga * sg * (1.0 - sg)).astype(dglu_ref.dtype)

    full = lambda shape: pl.BlockSpec(shape, lambda i: (0,) * len(shape))
    return _grid_call(
        body, [glu, glu, glu, u1, u1, u1, du3, du3, du3, cw, lw, lb, dproj], carry, name=name, grid=(nt,),
        in_specs=(_halo_specs(ts, GLU_COLS, rows, col) + _halo_specs(ts, CONV_CH, rows)
                  + _halo_specs(ts, CONV_CH, rows) + [full((CONV_K, CONV_CH)), _vec(CONV_CH), _vec(CONV_CH)]
                  + [pl.BlockSpec(memory_space=pl.ANY)]),
        out_specs=[_row(ts, GLU_COLS, col), full((CONV_K, 8, CONV_CH)), full((8, CONV_CH))],
        out_shape=[jax.ShapeDtypeStruct(dproj.shape, dproj.dtype),
                   jax.ShapeDtypeStruct((CONV_K, 8, CONV_CH), F32), jax.ShapeDtypeStruct((8, CONV_CH), F32)],
        scratch_shapes=[pltpu.VMEM((te, CONV_CH), F32), pltpu.VMEM((te, CONV_CH), F32),
                        pltpu.VMEM((7, te - 8, CONV_CH), F32), pltpu.VMEM((7, te - 8, CONV_CH), F32),
                        pltpu.VMEM((ts, CONV_CH), F32)],
        input_output_aliases={12: 0}, dims=("arbitrary",))


FFN_CW = 1408
FFN_NJ = FFN_H // FFN_CW


def _ffn_halo_specs(ts, rows, col_of, inner_rows):
    per = ts // HALO
    last = rows // HALO - 1
    if inner_rows:
        return [pl.BlockSpec((HALO, FFN_CW), lambda j, i: (jnp.maximum(i * per - 1, 0), col_of(j))),
                pl.BlockSpec((ts, FFN_CW), lambda j, i: (i, col_of(j))),
                pl.BlockSpec((HALO, FFN_CW), lambda j, i: (jnp.minimum((i + 1) * per, last), col_of(j)))]
    return [pl.BlockSpec((HALO, FFN_CW), lambda i, j: (jnp.maximum(i * per - 1, 0), col_of(j))),
            pl.BlockSpec((ts, FFN_CW), lambda i, j: (i, col_of(j))),
            pl.BlockSpec((HALO, FFN_CW), lambda i, j: (jnp.minimum((i + 1) * per, last), col_of(j)))]


def _ffn_ext(p_ref, m_ref, n_ref, sl, i, nt):
    return jnp.concatenate([jnp.where(i > 0, p_ref[:, sl].astype(F32), 0.0), m_ref[:, sl].astype(F32),
                            jnp.where(i < nt - 1, n_ref[:, sl].astype(F32), 0.0)], axis=0)


def _prev_row(v):
    return pltpu.roll(v, 1, 0)


def _next_row(v):
    return pltpu.roll(v, v.shape[0] - 1, 0)


def _ffn_act(up0, w3, b3, *, name):
    rows = up0.shape[0]
    ts = min(rows, 256)
    nt = rows // ts
    main = slice(HALO, HALO + ts)

    def body(gp, g, gn, vp, v, vn, wg, wv, bg, bv, a_ref):
        i = pl.program_id(0)
        for ch in range(FFN_CW // LANES):
            sl = slice(ch * LANES, (ch + 1) * LANES)
            xg, xv = _ffn_ext(gp, g, gn, sl, i, nt), _ffn_ext(vp, v, vn, sl, i, nt)
            wgv, wvv = wg[:, sl], wv[:, sl]
            gate = (wgv[0:1] * _prev_row(xg) + wgv[1:2] * xg + wgv[2:3] * _next_row(xg))[main] + bg[:, sl]
            val = (wvv[0:1] * _prev_row(xv) + wvv[1:2] * xv + wvv[2:3] * _next_row(xv))[main] + bv[:, sl]
            a_ref[:, sl] = (gate * _sigmoid(gate) * val).astype(a_ref.dtype)

    gcol, vcol = (lambda j: j), (lambda j: j + FFN_NJ)
    wspec = lambda col_of: pl.BlockSpec((FFN_K, FFN_CW), lambda i, j: (0, col_of(j)))
    bspec = lambda col_of: pl.BlockSpec((1, FFN_CW), lambda i, j: (0, col_of(j)))
    return pl.pallas_call(
        body, name=name, grid=(nt, FFN_NJ),
        in_specs=(_ffn_halo_specs(ts, rows, gcol, False) + _ffn_halo_specs(ts, rows, vcol, False)
                  + [wspec(gcol), wspec(vcol), bspec(gcol), bspec(vcol)]),
        out_specs=pl.BlockSpec((ts, FFN_CW), lambda i, j: (i, j)),
        out_shape=jax.ShapeDtypeStruct((rows, FFN_H), MXU_DT),
        compiler_params=_cp("parallel", "parallel"),
    )(up0, up0, up0, up0, up0, up0, w3, w3, b3, b3)


def _ffn_act_bwd(up0, da, w3, b3, *, carry=None, name):
    rows = up0.shape[0]
    ts = min(rows, 256)
    nt = rows // ts
    main = slice(HALO, HALO + ts)

    def body(gp, g, gn, vp, v, vn, ap, a, an, wg, wv, bg, bv, dg_ref, dv_ref, sg_ref, sv_ref):
        i = pl.program_id(1)

        @pl.when(i == 0)
        def _():
            sg_ref[...] = jnp.zeros_like(sg_ref)
            sv_ref[...] = jnp.zeros_like(sv_ref)

        rid = lax.broadcasted_iota(jnp.int32, (8, LANES), 0)
        for ch in range(FFN_CW // LANES):
            sl = slice(ch * LANES, (ch + 1) * LANES)
            xg, xv, da_e = _ffn_ext(gp, g, gn, sl, i, nt), _ffn_ext(vp, v, vn, sl, i, nt), _ffn_ext(ap, a, an, sl, i, nt)
            wgv, wvv = wg[:, sl], wv[:, sl]
            xg_p, xg_n, xv_p, xv_n = _prev_row(xg), _next_row(xg), _prev_row(xv), _next_row(xv)
            gate = wgv[0:1] * xg_p + wgv[1:2] * xg + wgv[2:3] * xg_n + bg[:, sl]
            val = wvv[0:1] * xv_p + wvv[1:2] * xv + wvv[2:3] * xv_n + bv[:, sl]
            sgm = _sigmoid(gate)
            eg = da_e * val * (sgm * (1.0 + gate * (1.0 - sgm)))
            ev = da_e * (gate * sgm)
            for e, taps, w, d_ref, s_ref in ((eg, (xg_p, xg, xg_n), wgv, dg_ref, sg_ref),
                                             (ev, (xv_p, xv, xv_n), wvv, dv_ref, sv_ref)):
                d0 = w[0:1] * _next_row(e) + w[1:2] * e + w[2:3] * _prev_row(e)
                d_ref[:, sl] = d0[main].astype(d_ref.dtype)
                dm = e[main]
                s_ref[:, sl] += (jnp.where(rid == 0, _colsum(dm * taps[0][main]), 0.0)
                                 + jnp.where(rid == 1, _colsum(dm * taps[1][main]), 0.0)
                                 + jnp.where(rid == 2, _colsum(dm * taps[2][main]), 0.0)
                                 + jnp.where(rid == 3, _colsum(dm), 0.0))

    gcol, vcol = (lambda j: j), (lambda j: j + FFN_NJ)
    wspec = lambda col_of: pl.BlockSpec((FFN_K, FFN_CW), lambda j, i: (0, col_of(j)))
    bspec = lambda col_of: pl.BlockSpec((1, FFN_CW), lambda j, i: (0, col_of(j)))
    ospec = pl.BlockSpec((ts, FFN_CW), lambda j, i: (i, j))
    sspec = pl.BlockSpec((8, FFN_CW), lambda j, i: (0, j))
    return _grid_call(
        body, [up0, up0, up0, up0, up0, up0, da, da, da, w3, w3, b3, b3], carry, name=name, grid=(FFN_NJ, nt),
        in_specs=(_ffn_halo_specs(ts, rows, gcol, True) + _ffn_halo_specs(ts, rows, vcol, True)
                  + _ffn_halo_specs(ts, rows, gcol, True)
                  + [wspec(gcol), wspec(vcol), bspec(gcol), bspec(vcol)]),
        out_specs=[ospec, ospec, sspec, sspec],
        out_shape=[jax.ShapeDtypeStruct((rows, FFN_H), MXU_DT), jax.ShapeDtypeStruct((rows, FFN_H), MXU_DT),
                   jax.ShapeDtypeStruct((8, FFN_H), F32), jax.ShapeDtypeStruct((8, FFN_H), F32)],
        dims=("parallel", "arbitrary"))


def _adam_math(w, g, m, v):
    m = ADAM_B1 * m + (1.0 - ADAM_B1) * g
    v = ADAM_B2 * v + (1.0 - ADAM_B2) * (g * g)
    m_hat = m / (1.0 - ADAM_B1 ** ADAM_STEP)
    v_hat = v / (1.0 - ADAM_B2 ** ADAM_STEP)
    delta = -ADAM_LR * (m_hat / (jnp.sqrt(v_hat) + ADAM_EPS) + ADAM_WD * w)
    return delta, m, v


def _adam(w, m, v, parts, *, name):
    rows, cols = w.shape
    nparts = parts.shape[0]
    tr = rows if rows <= 256 else _pick(rows, (256, 128, 64, 32, 16, 8))

    def body(w_ref, m_ref, v_ref, p_ref, g_ref, d_ref, nm_ref, nv_ref):
        g = p_ref[0].astype(F32)
        for p in range(1, nparts):
            g = g + p_ref[p].astype(F32)
        g_ref[...] = g
        d_ref[...], nm_ref[...], nv_ref[...] = _adam_math(w_ref[...], g, m_ref[...], v_ref[...])

    spec = _row(tr, cols)
    return pl.pallas_call(
        body, name=name, grid=(rows // tr,),
        in_specs=[spec, spec, spec, pl.BlockSpec((nparts, tr, cols), lambda i: (0, i, 0))],
        out_specs=[spec] * 4, out_shape=[jax.ShapeDtypeStruct((rows, cols), F32)] * 4,
        compiler_params=_cp("parallel"),
    )(w, m, v, parts)


def _adam_many(ws, ms, vs, gs, *, name):
    n = len(ws)

    def body(*refs):
        ins, outs = refs[:4 * n], refs[4 * n:]
        for k in range(n):
            delta, new_m, new_v = _adam_math(ins[k][...], ins[3 * n + k][...], ins[n + k][...], ins[2 * n + k][...])
            outs[k][...], outs[n + k][...], outs[2 * n + k][...] = delta, new_m, new_v

    vm = pl.BlockSpec(memory_space=pltpu.VMEM)
    res = pl.pallas_call(
        body, name=name, in_specs=[vm] * (4 * n), out_specs=[vm] * (3 * n),
        out_shape=[jax.ShapeDtypeStruct(w.shape, F32) for w in ws] * 3,
        compiler_params=pltpu.CompilerParams(vmem_limit_bytes=VMEM_LIMIT),
    )(*ws, *ms, *vs, *gs)
    return res[:n], res[n:2 * n], res[2 * n:]


def _sum_parts(parts, *, name):
    nparts, rows, cols = parts.shape
    tr = rows if rows <= 256 else _pick(rows, (256, 128, 64, 32, 16, 8))

    def body(p_ref, o_ref):
        g = p_ref[0].astype(F32)
        for p in range(1, nparts):
            g = g + p_ref[p].astype(F32)
        o_ref[...] = g

    return pl.pallas_call(
        body, name=name, grid=(rows // tr,),
        in_specs=[pl.BlockSpec((nparts, tr, cols), lambda i: (0, i, 0))], out_specs=_row(tr, cols),
        out_shape=jax.ShapeDtypeStruct((rows, cols), F32), compiler_params=_cp("parallel"),
    )(parts)


def _my_place():
    return lax.axis_index("x"), lax.axis_index("y"), lax.axis_index("c")


def _dev_index(p):
    return 4 * p[0] + 2 * p[1] + p[2]


def _all_gather(xs, *, hbm, name):
    return _run_comm(_gather_plan(xs), pl.ANY if hbm else pltpu.VMEM, name)


class _Comm(NamedTuple):
    ins: list
    outs: list
    n_remote: int
    n_local: int
    start: Callable
    finish: Callable


def _comm_scratch(plan):
    return [pltpu.SemaphoreType.DMA((plan.n_remote,)), pltpu.SemaphoreType.DMA((plan.n_remote,)),
            pltpu.SemaphoreType.DMA((plan.n_local,))]


def _run_comm(plan, space, name):
    n_in, n_out = len(plan.ins), len(plan.outs)

    def body(*refs):
        args = (refs[:n_in], refs[n_in:n_in + n_out], *refs[n_in + n_out:])
        plan.start(*args)
        plan.finish(*args)

    return pl.pallas_call(
        body, name=name, out_shape=plan.outs,
        in_specs=[pl.BlockSpec(memory_space=space)] * n_in, out_specs=[pl.BlockSpec(memory_space=space)] * n_out,
        scratch_shapes=_comm_scratch(plan),
        compiler_params=pltpu.CompilerParams(vmem_limit_bytes=VMEM_LIMIT),
    )(*plan.ins)


def _grid_call(body, ins, carry, *, name, grid, in_specs, out_specs, out_shape, dims, scratch_shapes=(),
               input_output_aliases=None):
    if carry is None:
        res = pl.pallas_call(
            body, name=name, grid=grid, in_specs=list(in_specs), out_specs=list(out_specs),
            out_shape=list(out_shape), scratch_shapes=list(scratch_shapes),
            input_output_aliases=input_output_aliases or {}, compiler_params=_cp(*dims))(*ins)
        return list(res), None

    def at(pos):
        conds = [pl.program_id(k) == p for k, p in enumerate(pos)]
        out = conds[0]
        for cnd in conds[1:]:
            out = jnp.logical_and(out, cnd)
        return out

    return _carried_call(body, carry, lambda: at([0] * len(grid)), lambda: at([g - 1 for g in grid]), ins,
                         name=name, grid=grid, in_specs=in_specs, out_specs=out_specs, out_shape=out_shape,
                         scratch_shapes=scratch_shapes, input_output_aliases=input_output_aliases)


def _carried_call(body, plan, first, last, ins, *, name, grid, in_specs, out_specs, out_shape, scratch_shapes=(),
                  input_output_aliases=None):
    in_specs, out_specs, out_shape = list(in_specs), list(out_specs), list(out_shape)
    n_in, n_out, n_scr = len(in_specs), len(out_specs), len(scratch_shapes)
    c_in, c_out = len(plan.ins), len(plan.outs)
    hbm = pl.BlockSpec(memory_space=pl.ANY)

    def full_body(*refs):
        ins, c_ins = refs[:n_in], refs[n_in:n_in + c_in]
        outs = refs[n_in + c_in:n_in + c_in + n_out]
        c_outs = refs[n_in + c_in + n_out:n_in + c_in + n_out + c_out]
        scr = refs[n_in + c_in + n_out + c_out:]
        sems = scr[n_scr:]

        @pl.when(first())
        def _():
            plan.start(c_ins, c_outs, *sems)

        body(*ins, *outs, *scr[:n_scr])

        @pl.when(last())
        def _():
            plan.finish(c_ins, c_outs, *sems)

    res = pl.pallas_call(
        full_body, name=name, grid=grid,
        in_specs=in_specs + [hbm] * c_in, out_specs=out_specs + [hbm] * c_out,
        out_shape=out_shape + list(plan.outs),
        scratch_shapes=list(scratch_shapes) + _comm_scratch(plan),
        input_output_aliases=input_output_aliases or {},
        compiler_params=_cp(*(["arbitrary"] * len(grid))),
    )(*ins, *plan.ins)
    return list(res[:n_out]), list(res[n_out:])


def _gather_plan(xs):
    n = len(xs)
    ms = [v.shape[0] for v in xs]

    def tools(x_refs, o_refs, send_sems, recv_sems, local_sems):
        x, y, c = _my_place()
        me, sib = (x, y, c), (x, y, 1 - c)
        chips = [(1 - x, y), (x, 1 - y), (1 - x, 1 - y)]

        def rows(a, p):
            return o_refs[a].at[pl.ds(pl.multiple_of(_dev_index(p) * ms[a], 8), ms[a])]

        def copy(a, k, block, to, src=None):
            return pltpu.make_async_remote_copy(
                src_ref=rows(a, block) if src is None else src, dst_ref=rows(a, block),
                send_sem=send_sems.at[a * 7 + k], recv_sem=recv_sems.at[a * 7 + k],
                device_id=to, device_id_type=MESH)

        mine = [pltpu.make_async_copy(x_refs[a], rows(a, me), local_sems.at[a]) for a in range(n)]
        first = []
        for a in range(n):
            first.append(copy(a, 0, me, sib, src=x_refs[a]))
            first += [copy(a, 1 + j, me, (*chip, c), src=x_refs[a]) for j, chip in enumerate(chips)]
        return me, sib, chips, c, copy, mine, first

    def start(*refs):
        _, _, _, _, _, mine, first = tools(*refs)
        for cp in mine + first:
            cp.start()

    def finish(*refs):
        me, sib, chips, c, copy, mine, first = tools(*refs)
        passed = []
        for j, chip in enumerate(chips):
            for a in range(n):
                copy(a, 1 + j, (*chip, c), me).wait_recv()
                cp = copy(a, 4 + j, (*chip, c), sib)
                cp.start()
                passed.append(cp)
        for a in range(n):
            copy(a, 0, sib, me).wait_recv()
            for j, chip in enumerate(chips):
                copy(a, 4 + j, (*chip, 1 - c), me).wait_recv()
        for cp in first + passed:
            cp.wait_send()
        for cp in mine:
            cp.wait()

    outs = [jax.ShapeDtypeStruct((N_DEV * v.shape[0], v.shape[1]), v.dtype) for v in xs]
    return _Comm(list(xs), outs, 7 * n, n, start, finish)


def _exchange_plan(gs, cols=None):
    n = len(gs)
    rs = [v.shape[0] // N_DEV for v in gs]
    flips = [(bx, by, bc) for bx in (0, 1) for by in (0, 1) for bc in (0, 1)][1:]

    def tools(g_refs, o_refs, send_sems, recv_sems, local_sems):
        x, y, c = _my_place()
        me = (x, y, c)

        def block(ref, a, p):
            return ref.at[pl.ds(_dev_index(p) * rs[a], rs[a])]

        def src(a, p):
            rows = pl.ds(_dev_index(p) * rs[a], rs[a])
            return g_refs[a].at[rows] if cols is None else g_refs[a].at[rows, pl.ds(cols[0], cols[1])]

        def peer(f):
            return (1 - x if f[0] else x, 1 - y if f[1] else y, 1 - c if f[2] else c)

        def copy(a, k, to):
            return pltpu.make_async_remote_copy(
                src_ref=src(a, to), dst_ref=block(o_refs[a], a, me),
                send_sem=send_sems.at[a * 7 + k], recv_sem=recv_sems.at[a * 7 + k],
                device_id=to, device_id_type=MESH)

        def arrival(a, k, frm):
            return pltpu.make_async_remote_copy(
                src_ref=src(a, frm), dst_ref=block(o_refs[a], a, frm),
                send_sem=send_sems.at[a * 7 + k], recv_sem=recv_sems.at[a * 7 + k],
                device_id=frm, device_id_type=MESH)

        mine = [pltpu.make_async_copy(src(a, me), block(o_refs[a], a, me), local_sems.at[a])
                for a in range(n)]
        sends = [copy(a, k, peer(f)) for a in range(n) for k, f in enumerate(flips)]
        arrivals = [arrival(a, k, peer(f)) for a in range(n) for k, f in enumerate(flips)]
        return mine, sends, arrivals

    def start(*refs):
        mine, sends, _ = tools(*refs)
        for cp in mine + sends:
            cp.start()

    def finish(*refs):
        mine, sends, arrivals = tools(*refs)
        for cp in arrivals:
            cp.wait_recv()
        for cp in sends:
            cp.wait_send()
        for cp in mine:
            cp.wait()

    outs = [jax.ShapeDtypeStruct((v.shape[0], v.shape[1] if cols is None else cols[1]), v.dtype) for v in gs]
    return _Comm(list(gs), outs, 7 * n, n, start, finish)


def _rope_tables(seq):
    t = jnp.arange(seq)
    row, col = t // GRID_W, t % GRID_W
    half = HEAD_DIM // 2
    inv = ROPE_BASE ** (-jnp.arange(0, half, 2, dtype=F32) / half)
    ang_r = row.astype(F32)[:, None] * inv
    ang_c = col.astype(F32)[:, None] * inv
    ang = jnp.concatenate([ang_r, ang_r, ang_c, ang_c], axis=-1)
    return jnp.tile(jnp.cos(ang), (1, 2)), jnp.tile(jnp.sin(ang), (1, 2))


def _to_slots(w):
    return w.reshape(N_KV_HEADS, GQA_GROUP, HEAD_DIM, w.shape[1]).transpose(1, 0, 2, 3).reshape(w.shape)


def _from_slots(w):
    return w.reshape(GQA_GROUP, N_KV_HEADS, HEAD_DIM, w.shape[1]).transpose(1, 0, 2, 3).reshape(w.shape)


def _pack(vs):
    flat = jnp.concatenate([v.reshape(-1).astype(F32) for v in vs])
    total = -(-flat.shape[0] // (8 * LANES)) * (8 * LANES)
    return jnp.pad(flat, (0, total - flat.shape[0])).reshape(-1, LANES)


def _unpack(packed, like):
    flat, out, off = packed.reshape(-1), [], 0
    for v in like:
        size = math.prod(v.shape)
        out.append(flat[off:off + size].reshape(v.shape))
        off += size
    return out


def _silu(v):
    return v * jax.nn.sigmoid(v)


def kernel(x, c, ctx, c_ctx, w_mod, b_mod, norm_mix_w, w_in, q_norm_w, k_norm_w, sink_logit, conv_w, conv_b, conv_norm_w, conv_norm_b, w_out, norm_ffn_w, w_up, ffn_conv_w, ffn_conv_b, w_down, loss_target, m_c_ctx, m_w_mod, m_b_mod, m_norm_mix_w, m_w_in, m_q_norm_w, m_k_norm_w, m_sink_logit, m_conv_w, m_conv_b, m_conv_norm_w, m_conv_norm_b, m_w_out, m_norm_ffn_w, m_w_up, m_ffn_conv_w, m_ffn_conv_b, m_w_down, v_c_ctx, v_w_mod, v_b_mod, v_norm_mix_w, v_w_in, v_q_norm_w, v_k_norm_w, v_sink_logit, v_conv_w, v_conv_b, v_conv_norm_w, v_conv_norm_b, v_w_out, v_norm_ffn_w, v_w_up, v_ffn_conv_w, v_ffn_conv_b, v_w_down):
    d = D_MODEL
    seq, n_ctx = x.shape[1], ctx.shape[1]
    me = _dev_index(_my_place())
    xs, ctxs, tgt = x[0], ctx[0], loss_target[0]

    small = _pack([c[0], conv_w[0], ffn_conv_w[0]])
    w_in_t, small_all = _all_gather([w_in[0].T.astype(MXU_DT), small], hbm=True, name="gather_first")
    small_all = small_all.reshape(N_DEV, -1)
    w_in_p = jnp.concatenate([_to_slots(w_in_t[:Q_COLS]), w_in_t[Q_COLS:QKV_COLS],
                              jnp.zeros((GLU_OFF - QKV_COLS, d), MXU_DT), w_in_t[QKV_COLS:]])
    n_cw, n_fw = conv_w[0].size, ffn_conv_w[0].size
    c_all = small_all[:, :d]
    cw_all = small_all[:, d:d + n_cw].reshape(N_DEV, CONV_K, -1)
    fw_all = small_all[:, d + n_cw:d + n_cw + n_fw].reshape(N_DEV, FFN_K, -1)
    conv_w_f = cw_all.transpose(1, 0, 2).reshape(CONV_K, CONV_CH)
    ffn_w_f = fw_all.transpose(1, 0, 2).reshape(FFN_K, 2 * FFN_H)

    mcols = w_mod.shape[2]
    act = jnp.zeros((16, d), F32).at[:N_DEV].set(_silu(c_all)).at[N_DEV].set(_silu(c_ctx))
    mod_part = _mm(act, w_mod[0], 16, mcols, d, name="mod_fwd")
    mod_all = _all_gather([mod_part], hbm=False, name="gather_mod")[0]
    mod_all = mod_all.reshape(N_DEV, 16, mcols).transpose(1, 0, 2).reshape(16, 6 * d) + b_mod
    mod = lax.dynamic_slice_in_dim(mod_all, me, 1, axis=0)
    sh1, sc1, g1, sh2, sc2, g2 = [mod[:, k * d:(k + 1) * d] for k in range(6)]
    sh1c, sc1c = mod_all[N_DEV:N_DEV + 1, :d], mod_all[N_DEV:N_DEV + 1, d:2 * d]

    cos, sin = _rope_tables(seq)
    ones_c, zeros_c = jnp.ones((n_ctx, LANES), F32), jnp.zeros((n_ctx, LANES), F32)
    qk_w = jnp.concatenate([jnp.tile(q_norm_w, (1, N_Q_HEADS)), jnp.tile(k_norm_w, (1, N_KV_HEADS))], axis=1)
    kc_w = jnp.tile(k_norm_w, (1, N_KV_HEADS))

    h = _prenorm(xs, norm_mix_w, sc1, sh1, name="prenorm_mix")
    hc = _prenorm(ctxs, norm_mix_w, sc1c, sh1c, name="prenorm_ctx")
    proj = _mm(h, w_in_p, seq, IN_PAD, d, tb=True, name="proj_in")
    kv_ctx = _mm(hc, w_in_p, n_ctx, 2 * KV_COLS, d, tb=True, n0=Q_COLS, name="proj_ctx")
    qk_r = _qk_prep(proj, QK_COLS, cos, sin, qk_w, name="qk_prep")
    kc_n = _qk_prep(kv_ctx, KV_COLS, ones_c, zeros_c, kc_w, name="k_ctx_prep")
    (attn_o, lse), (w_up_t, w_out_f) = _attn_fwd(
        sink_logit, qk_r, proj, kc_n, kv_ctx, name="attn_fwd",
        carry=_gather_plan([w_up[0].T.astype(MXU_DT), w_out[0].astype(MXU_DT)]))
    w_out_f = jnp.concatenate([_to_slots(w_out_f[:Q_COLS]), w_out_f[Q_COLS:]])
    (u3, u1), (w_down_f,) = _conv_fwd(proj, GLU_OFF // GLU_COLS, conv_w_f, conv_b, conv_norm_w, conv_norm_b,
                                      carry=_gather_plan([w_down[0].astype(MXU_DT)]), name="conv_fwd")
    mix = _mm(attn_o, w_out_f, seq, d, Q_COLS, name="out_attn")
    mix = _mm(u3, w_out_f, seq, d, CONV_CH, k0=Q_COLS, add=mix, name="out_conv")

    x1, h2 = _resid_prenorm(xs, mix, g1, norm_ffn_w, sc2, sh2, name="prenorm_ffn")
    up0 = _mm(h2, w_up_t, seq, 2 * FFN_H, d, tb=True, out_dtype=MXU_DT, name="ffn_up")
    act_a = _ffn_act(up0, ffn_w_f, ffn_conv_b, name="ffn_act")
    ffn = _mm(act_a, w_down_f, seq, d, FFN_H, name="ffn_down")
    loss_p, dy, dffn, dg2 = _loss_head(x1, ffn, g2, tgt, name="loss_head")

    da = _mm(dffn, w_down_f, seq, FFN_H, d, tb=True, out_dtype=MXU_DT, name="ffn_down_dx")
    gw_down = _mm(act_a, dffn, FFN_H, d, seq, ta=True, out_dtype=MXU_DT, name="ffn_down_dw")
    (dgate0, dval0, s_gate, s_val), (rx_down,) = _ffn_act_bwd(
        up0, da, ffn_w_f, ffn_conv_b, carry=_exchange_plan([gw_down]), name="ffn_act_bwd")
    gw_up_t = _mm(dgate0, h2, FFN_H, d, seq, ta=True, out_dtype=MXU_DT, out_rows=2 * FFN_H, name="ffn_up_dw_gate")
    gw_up_t = _mm(dval0, h2, FFN_H, d, seq, ta=True, into=gw_up_t, m0=FFN_H, name="ffn_up_dw_val")
    half_d = d // 2
    dh2, (rx_up_lo,) = _mm(dgate0, w_up_t, seq, d, FFN_H, a2=dval0, name="ffn_up_dx",
                           carry=_exchange_plan([gw_up_t], cols=(0, half_d)))
    (dx1, dmix, dsh2, dsc2, dnw2, dg1), _ = _norm_bwd(
        dh2, x1, norm_ffn_w, sc2, res=dy, gate=(mix, g1), name="prenorm_ffn_bwd")

    dattn = _mm(dmix, w_out_f, seq, Q_COLS, d, tb=True, out_dtype=MXU_DT, name="out_dx_attn")
    du3 = _mm(dmix, w_out_f, seq, CONV_CH, d, tb=True, n0=Q_COLS, name="out_dx_conv")
    gw_out = _mm(attn_o, dmix, Q_COLS, d, seq, ta=True, out_dtype=MXU_DT, out_rows=Q_COLS + CONV_CH,
                 name="out_dw_attn")
    gw_out = _mm(u3, dmix, CONV_CH, d, seq, ta=True, into=gw_out, m0=Q_COLS, name="out_dw_conv")
    gw_out = jnp.concatenate([_from_slots(gw_out[:Q_COLS]), gw_out[Q_COLS:]])
    (dq, dk, dv, dkc_r, dvc, dsink_rows), (rx_out,) = _attn_bwd(
        sink_logit, qk_r, proj, kc_n, kv_ctx, attn_o, lse, dattn, carry=_exchange_plan([gw_out]), name="attn_bwd")
    dproj, dqk_w = _qk_prep_bwd(proj, [dq, dk], QK_COLS, cos, sin, qk_w, tail=dv, name="qk_prep_bwd")
    (dproj, dcw8, dvec), (rx_up_hi,) = _conv_bwd(
        proj, GLU_OFF // GLU_COLS, u1, du3, dproj, conv_w_f, conv_norm_w, conv_norm_b,
        carry=_exchange_plan([gw_up_t], cols=(half_d, half_d)), name="conv_bwd")
    dkc, dkc_w = _qk_prep_bwd(kv_ctx, [dkc_r], KV_COLS, ones_c, zeros_c, kc_w, name="k_ctx_prep_bwd")
    dkv_ctx = jnp.concatenate([dkc, dvc.astype(MXU_DT)], axis=1)
    gw_p = _mm(dproj, h, IN_PAD, d, seq, ta=True, name="proj_dw")
    gw_ctx = _mm(dkv_ctx, hc, 2 * KV_COLS, d, n_ctx, ta=True, name="proj_dw_ctx")
    gw_in_t = jnp.concatenate([_from_slots(gw_p[:Q_COLS]), gw_p[Q_COLS:QKV_COLS] + gw_ctx, gw_p[GLU_OFF:]],
                              axis=0).astype(MXU_DT)
    dh, (rx_in,) = _mm(dproj, w_in_p, seq, d, IN_PAD, carry=_exchange_plan([gw_in_t]), name="proj_dx")
    dhc = _mm(dkv_ctx, w_in_p, n_ctx, d, 2 * KV_COLS, k0=Q_COLS, name="proj_dx_ctx")
    (grad_x, dsh1, dsc1, dnw1), _ = _norm_bwd(dh, xs, norm_mix_w, sc1, res=dx1, name="prenorm_mix_bwd")
    (dsh1c, dsc1c, dnw1c), _ = _norm_bwd(dhc, ctxs, norm_mix_w, sc1c, want_dx=False, name="prenorm_ctx_bwd")

    dmod = jnp.concatenate([dsh1, dsc1, dg1, dsh2, dsc2, dg2], axis=1)
    dmod_ctx = jnp.concatenate([dsh1c, dsc1c], axis=1)
    d_qn = dqk_w[0, :Q_COLS].reshape(N_Q_HEADS, HEAD_DIM).sum(0)
    d_kn = (dqk_w[0, Q_COLS:].reshape(N_KV_HEADS, HEAD_DIM).sum(0)
            + dkc_w[0].reshape(N_KV_HEADS, HEAD_DIM).sum(0))
    d_ffn_w = jnp.concatenate([s_gate[:FFN_K], s_val[:FFN_K]], axis=1)
    d_ffn_b = jnp.concatenate([s_gate[FFN_K], s_val[FFN_K]])
    d_sink = dsink_rows.reshape(N_Q_HEADS, BLOCK).sum(1)
    summed_like = [(dnw1 + dnw1c), d_qn[None], d_kn[None], d_sink[None], dvec[0:1], dvec[1:2],
                   dvec[2:3], dnw2, d_ffn_b[None], dcw8.sum(1), d_ffn_w, loss_p[0:1, 0:1]]
    pack = _pack([dmod, dmod_ctx] + summed_like)
    pack_all = _all_gather([pack], hbm=False, name="gather_small_grads")[0]
    pack_all = pack_all.reshape(N_DEV, pack.shape[0], LANES)
    tot = _sum_parts(pack_all, name="sum_small_grads")
    (dmod_sum, dmc_sum, g_nmix, g_qn, g_kn, g_sink, g_cb, g_lw, g_lb, g_nffn, g_fb, g_cw_f, g_fw_f,
     loss_sum) = _unpack(tot, [dmod, dmod_ctx] + summed_like)
    loss = loss_sum[0, 0]
    dmod_all = pack_all.reshape(N_DEV, -1)[:, :6 * d]
    g_b_mod = dmod_sum.at[:, :2 * d].add(dmc_sum)

    lo = me * mcols
    dm_rows = jnp.zeros((16, 6 * d), F32).at[:N_DEV].set(dmod_all).at[N_DEV, :2 * d].set(dmc_sum[0])
    dm_mine = lax.dynamic_slice_in_dim(dm_rows, lo, mcols, axis=1)
    parts_mod = _mm(act, dm_mine, d, mcols, 16, ta=True, name="mod_dw")[None]
    dact_part = _mm(dm_mine[N_DEV:N_DEV + 8], w_mod[0], 8, d, mcols, tb=True, name="mod_dx_ctx")
    dact_all = _all_gather([dact_part], hbm=False, name="gather_c_ctx_grad")[0].reshape(N_DEV, 8, d)
    dact = _sum_parts(dact_all, name="sum_c_ctx_grad")[0]
    sg = jax.nn.sigmoid(c_ctx)
    g_c_ctx = dact * (sg * (1.0 + c_ctx * (1.0 - sg)))

    def stacked(rx):
        return rx.reshape(N_DEV, rx.shape[0] // N_DEV, rx.shape[1])

    g_w_in = _sum_parts(stacked(rx_in), name="sum_w_in").T[None]
    g_w_up = jnp.concatenate([_sum_parts(stacked(rx_up_lo), name="sum_w_up_lo"),
                              _sum_parts(stacked(rx_up_hi), name="sum_w_up_hi")], axis=1).T[None]
    big = {}
    big["w_in"] = _adam(w_in[0], m_w_in[0], v_w_in[0], g_w_in, name="adam_w_in")
    big["w_up"] = _adam(w_up[0], m_w_up[0], v_w_up[0], g_w_up, name="adam_w_up")
    big["w_out"] = _adam(w_out[0], m_w_out[0], v_w_out[0], stacked(rx_out), name="adam_w_out")
    big["w_down"] = _adam(w_down[0], m_w_down[0], v_w_down[0], stacked(rx_down), name="adam_w_down")
    big["w_mod"] = _adam(w_mod[0], m_w_mod[0], v_w_mod[0], parts_mod, name="adam_w_mod")

    ccols, fcols = conv_w.shape[2], ffn_conv_w.shape[2]
    g_conv_w = lax.dynamic_slice_in_dim(g_cw_f, me * ccols, ccols, axis=1)[None]
    g_ffn_w = lax.dynamic_slice_in_dim(g_fw_f, me * fcols, fcols, axis=1)[None]
    names = ["c_ctx", "b_mod", "norm_mix_w", "q_norm_w", "k_norm_w", "sink_logit", "conv_w", "conv_b",
             "conv_norm_w", "conv_norm_b", "norm_ffn_w", "ffn_conv_w", "ffn_conv_b"]
    ws = [c_ctx, b_mod, norm_mix_w, q_norm_w, k_norm_w, sink_logit, conv_w, conv_b, conv_norm_w, conv_norm_b,
          norm_ffn_w, ffn_conv_w, ffn_conv_b]
    msm = [m_c_ctx, m_b_mod, m_norm_mix_w, m_q_norm_w, m_k_norm_w, m_sink_logit, m_conv_w, m_conv_b,
           m_conv_norm_w, m_conv_norm_b, m_norm_ffn_w, m_ffn_conv_w, m_ffn_conv_b]
    vsm = [v_c_ctx, v_b_mod, v_norm_mix_w, v_q_norm_w, v_k_norm_w, v_sink_logit, v_conv_w, v_conv_b,
           v_conv_norm_w, v_conv_norm_b, v_norm_ffn_w, v_ffn_conv_w, v_ffn_conv_b]
    gsm = [g_c_ctx, g_b_mod, g_nmix, g_qn, g_kn, g_sink, g_conv_w, g_cb, g_lw, g_lb, g_nffn, g_ffn_w, g_fb]
    deltas, new_ms, new_vs = _adam_many(ws, msm, vsm, gsm, name="adam_small")
    sm = {nm: vals for nm, vals in zip(names, zip(gsm, deltas, new_ms, new_vs))}

    def out4(nm):
        if nm in sm:
            return sm[nm]
        return tuple(t[None] for t in big[nm])

    order = ["c_ctx", "w_mod", "b_mod", "norm_mix_w", "w_in", "q_norm_w", "k_norm_w", "sink_logit", "conv_w",
             "conv_b", "conv_norm_w", "conv_norm_b", "w_out", "norm_ffn_w", "w_up", "ffn_conv_w", "ffn_conv_b",
             "w_down"]
    quads = [out4(nm) for nm in order]
    return (loss, grad_x[None], *[q[0] for q in quads], *[q[1] for q in quads],
            *[q[2] for q in quads], *[q[3] for q in quads])
```

```python
import math
from typing import Callable, NamedTuple

import jax
import jax.numpy as jnp
from jax import lax
from jax.experimental import pallas as pl
from jax.experimental.pallas import tpu as pltpu

F32 = jnp.float32
MXU_DT = jnp.bfloat16

D_MODEL = 1024
GRID_W = 64
HEAD_DIM = 64
N_Q_HEADS = 8
N_KV_HEADS = 2
GQA_GROUP = 4
WINDOW = 128
BLOCK = 128
Q_COLS = 512
KV_COLS = 128
QK_COLS = Q_COLS + KV_COLS
QKV_COLS = Q_COLS + 2 * KV_COLS
CONV_CH = 512
GLU_COLS = 2 * CONV_CH
IN_COLS = QKV_COLS + GLU_COLS
CONV_K = 31
CONV_PAD = 15
FFN_H = 2816
FFN_K = 3
ROPE_BASE = 10000.0
EPS = 1e-6
NEG_INF = -1e30
N_DEV = 8
HALO = 16
LANES = 128
ROW_TS = 512

ADAM_LR = 0.001
ADAM_B1 = 0.9
ADAM_B2 = 0.999
ADAM_EPS = 1e-08
ADAM_WD = 0.01
ADAM_STEP = 10

MESH_AXES = ("x", "y", "c")
MESH = pl.DeviceIdType.MESH
VMEM_LIMIT = 56 << 20
MM_VMEM_BUDGET = 44 << 20
GLU_OFF = 1024
IN_PAD = GLU_OFF + GLU_COLS


def _cp(*dims):
    return pltpu.CompilerParams(dimension_semantics=dims or None, vmem_limit_bytes=VMEM_LIMIT)


def _pick(n, prefs, also=0):
    for p in prefs:
        if n % p == 0 and also % p == 0:
            return p
    raise ValueError((n, prefs, also))


def _row(ts, w, col=0):
    return pl.BlockSpec((ts, w), lambda i: (i, col))


def _vec(w):
    return pl.BlockSpec((1, w), lambda i: (0, 0))


def _colsum(v):
    return jnp.sum(v, axis=0, keepdims=True)


def _sigmoid(v):
    return 0.5 * jnp.tanh(0.5 * v) + 0.5


def _mm(a, b, m, n, k, *, ta=False, tb=False, n0=0, k0=0, add=None, out_dtype=F32, into=None, m0=0,
        out_rows=None, a2=None, carry=None, name):
    has_add, has_a2 = add is not None, a2 is not None
    assert not (has_a2 and (ta or tb))
    if into is not None:
        out_dtype = into.dtype
    sa, sb, so = a.dtype.itemsize, b.dtype.itemsize, jnp.dtype(out_dtype).itemsize
    sadd = add.dtype.itemsize if has_add else 0
    na = 2 if has_a2 else 1

    def fits(tm, tn):
        return 2 * (na * k * (tm * sa + tn * sb) + tm * tn * (so + sadd)) <= MM_VMEM_BUDGET

    tms = [m] if m <= 1024 else [t for t in (1024, 1408, 768, 512, 256, 128) if m % t == 0]
    tns = [t for t in ((1024, 512, 256, 128) if ta else (1408, 512, 256, 128)) if n % t == 0 and n0 % t == 0]
    tm, tn = next((tm, tn) for tm in tms for tn in tns if fits(tm, tn))
    a_spec = (pl.BlockSpec((k, tm), lambda i, j: (0, i)) if ta else pl.BlockSpec((tm, k), lambda i, j: (i, 0)))
    nb0 = n0 // tn
    if tb:
        assert k0 == 0
        b_spec = pl.BlockSpec((tn, k), lambda i, j: (j + nb0, 0))
    else:
        assert k0 % (na * k) == 0, (k0, k)
        kb0 = k0 // (na * k)
        b_spec = pl.BlockSpec((na * k, tn), lambda i, j: (kb0, j + nb0))
    assert m0 % tm == 0, (m0, tm)
    mb0 = m0 // tm
    o_spec = pl.BlockSpec((tm, tn), lambda i, j: (i + mb0, j))
    dims = (((0 if ta else 1,), (1 if tb else 0,)), ((), ()))

    def body(*refs):
        a_ref, b_ref, o_ref = refs[0], refs[na], refs[-1]
        if has_a2:
            res = (jnp.dot(a_ref[...].astype(MXU_DT), b_ref[0:k, :].astype(MXU_DT), preferred_element_type=F32)
                   + jnp.dot(refs[1][...].astype(MXU_DT), b_ref[k:2 * k, :].astype(MXU_DT),
                             preferred_element_type=F32))
        else:
            res = lax.dot_general(a_ref[...].astype(MXU_DT), b_ref[...].astype(MXU_DT), dims,
                                  preferred_element_type=F32)
        if has_add:
            res = res + refs[na + 1][...].astype(F32)
        o_ref[...] = res.astype(o_ref.dtype)

    ins = [a] + ([a2] if has_a2 else []) + [b] + ([add] if has_add else []) + ([into] if into is not None else [])
    specs = ([a_spec] * na + [b_spec] + ([pl.BlockSpec((tm, tn), lambda i, j: (i, j))] if has_add else [])
             + ([pl.BlockSpec(memory_space=pl.ANY)] if into is not None else []))
    out_shape = (jax.ShapeDtypeStruct(into.shape, into.dtype) if into is not None
                 else jax.ShapeDtypeStruct((out_rows or m, n), out_dtype))
    (out,), carried = _grid_call(
        body, ins, carry, name=name, grid=(m // tm, n // tn), in_specs=specs, out_specs=[o_spec],
        out_shape=[out_shape], input_output_aliases={len(ins) - 1: 0} if into is not None else None,
        dims=("parallel", "parallel"))
    return out if carry is None else (out, carried)


def _rms_stats(xv):
    r = lax.rsqrt(jnp.mean(xv * xv, axis=-1, keepdims=True) + EPS)
    return r, xv * r


def _prenorm(x, nw, sc, sh, *, carry=None, name):
    rows, d = x.shape
    ts = min(rows, ROW_TS)

    def body(x_ref, nw_ref, sc_ref, sh_ref, h_ref):
        _, xn = _rms_stats(x_ref[...])
        h_ref[...] = ((xn * nw_ref[...]) * (1.0 + sc_ref[...]) + sh_ref[...]).astype(h_ref.dtype)

    (h,), carried = _grid_call(
        body, [x, nw, sc, sh], carry, name=name, grid=(rows // ts,),
        in_specs=[_row(ts, d), _vec(d), _vec(d), _vec(d)], out_specs=[_row(ts, d)],
        out_shape=[jax.ShapeDtypeStruct((rows, d), MXU_DT)], dims=("parallel",))
    return h, carried


def _resid_prenorm(x, mix, g1, nw, sc, sh, *, name):
    rows, d = x.shape
    ts = min(rows, ROW_TS)

    def body(x_ref, mix_ref, g_ref, nw_ref, sc_ref, sh_ref, x1_ref, h_ref):
        x1 = x_ref[...] + g_ref[...] * mix_ref[...]
        x1_ref[...] = x1
        _, xn = _rms_stats(x1)
        h_ref[...] = ((xn * nw_ref[...]) * (1.0 + sc_ref[...]) + sh_ref[...]).astype(h_ref.dtype)

    return pl.pallas_call(
        body, name=name, grid=(rows // ts,),
        in_specs=[_row(ts, d), _row(ts, d), _vec(d), _vec(d), _vec(d), _vec(d)],
        out_specs=[_row(ts, d), _row(ts, d)],
        out_shape=[jax.ShapeDtypeStruct((rows, d), F32), jax.ShapeDtypeStruct((rows, d), MXU_DT)],
        compiler_params=_cp("parallel"),
    )(x, mix, g1, nw, sc, sh)


def _loss_head(x1, ffn, g2, target, *, name):
    rows, d = x1.shape
    ts = min(rows, ROW_TS)

    def body(x1_ref, f_ref, g_ref, t_ref, loss_ref, dy_ref, dffn_ref, dg_ref):
        i = pl.program_id(0)
        f = f_ref[...]
        e = x1_ref[...] + g_ref[...] * f - t_ref[...]
        per_tok = jnp.mean(e * e, axis=-1, keepdims=True)
        part = 0.5 * jnp.sum(per_tok, axis=0, keepdims=True)
        dy = e * (1.0 / d)
        dy_ref[...] = dy
        dffn_ref[...] = (dy * g_ref[...]).astype(dffn_ref.dtype)

        @pl.when(i == 0)
        def _():
            loss_ref[...] = jnp.zeros_like(loss_ref)
            dg_ref[...] = jnp.zeros_like(dg_ref)

        loss_ref[...] += jnp.broadcast_to(part, loss_ref.shape)
        dg_ref[...] += _colsum(dy * f)

    return pl.pallas_call(
        body, name=name, grid=(rows // ts,),
        in_specs=[_row(ts, d), _row(ts, d), _vec(d), _row(ts, d)],
        out_specs=[pl.BlockSpec((8, LANES), lambda i: (0, 0)), _row(ts, d), _row(ts, d), _vec(d)],
        out_shape=[jax.ShapeDtypeStruct((8, LANES), F32), jax.ShapeDtypeStruct((rows, d), F32),
                   jax.ShapeDtypeStruct((rows, d), MXU_DT), jax.ShapeDtypeStruct((1, d), F32)],
        compiler_params=_cp("arbitrary"),
    )(x1, ffn, g2, target)


def _norm_bwd(dh, xin, nw, sc, *, res=None, gate=None, want_dx=True, carry=None, name):
    rows, d = xin.shape
    ts = min(rows, ROW_TS)
    has_res, has_gate = res is not None, gate is not None

    def body(*refs):
        it = iter(refs)
        dh_ref, x_ref, nw_ref, sc_ref = next(it), next(it), next(it), next(it)
        res_ref = next(it) if has_res else None
        gated_ref, g_ref = (next(it), next(it)) if has_gate else (None, None)
        dx_ref = next(it) if want_dx else None
        dgx_ref = next(it) if has_gate else None
        dsh_ref, dsc_ref, dnw_ref = next(it), next(it), next(it)
        dg_ref = next(it) if has_gate else None
        i = pl.program_id(0)
        dhv = dh_ref[...]
        r, xn = _rms_stats(x_ref[...])
        dn = dhv * (1.0 + sc_ref[...])

        @pl.when(i == 0)
        def _():
            dsh_ref[...] = jnp.zeros_like(dsh_ref)
            dsc_ref[...] = jnp.zeros_like(dsc_ref)
            dnw_ref[...] = jnp.zeros_like(dnw_ref)
            if has_gate:
                dg_ref[...] = jnp.zeros_like(dg_ref)

        dsh_ref[...] += _colsum(dhv)
        dsc_ref[...] += _colsum(dhv * (xn * nw_ref[...]))
        dnw_ref[...] += _colsum(dn * xn)
        if want_dx:
            dxn = dn * nw_ref[...]
            dx = r * (dxn - xn * jnp.mean(dxn * xn, axis=-1, keepdims=True))
            if has_res:
                dx = dx + res_ref[...]
            dx_ref[...] = dx
            if has_gate:
                dgx_ref[...] = (dx * g_ref[...]).astype(dgx_ref.dtype)
                dg_ref[...] += _colsum(dx * gated_ref[...])

    ins = [dh, xin, nw, sc] + ([res] if has_res else []) + (list(gate) if has_gate else [])
    in_specs = ([_row(ts, d), _row(ts, d), _vec(d), _vec(d)] + ([_row(ts, d)] if has_res else [])
                + ([_row(ts, d), _vec(d)] if has_gate else []))
    out_specs, out_shape = [], []
    if want_dx:
        out_specs.append(_row(ts, d)); out_shape.append(jax.ShapeDtypeStruct((rows, d), F32))
    if has_gate:
        out_specs.append(_row(ts, d)); out_shape.append(jax.ShapeDtypeStruct((rows, d), MXU_DT))
    for _ in range(3 + int(has_gate)):
        out_specs.append(_vec(d)); out_shape.append(jax.ShapeDtypeStruct((1, d), F32))
    return _grid_call(body, ins, carry, name=name, grid=(rows // ts,), in_specs=in_specs, out_specs=out_specs,
                      out_shape=out_shape, dims=("arbitrary",))


def _group_sum(v, g):
    hi = v.astype(MXU_DT)
    lo = (v - hi.astype(F32)).astype(MXU_DT)
    return (jnp.dot(hi, g, preferred_element_type=F32) + jnp.dot(lo, g, preferred_element_type=F32))


def _rot(v):
    lane = lax.broadcasted_iota(jnp.int32, v.shape, 1)
    first = (lane & 31) < 16
    return jnp.where(first, -pltpu.roll(v, LANES - 16, 1), pltpu.roll(v, 16, 1))


def _head_group_matrix():
    r = jnp.arange(LANES) // HEAD_DIM
    return (r[:, None] == r[None, :]).astype(MXU_DT)


def _qk_prep(xin, width, cos, sin, w, *, name):
    rows = xin.shape[0]
    ts = min(rows, 256)
    nch = width // LANES

    def body(x_ref, cos_ref, sin_ref, w_ref, g_ref, o_ref):
        cs, sn, g = cos_ref[...], sin_ref[...], g_ref[...]
        for ch in range(nch):
            sl = slice(ch * LANES, (ch + 1) * LANES)
            xv = x_ref[:, sl]
            r = lax.rsqrt(_group_sum(xv * xv, g) * (1.0 / HEAD_DIM) + EPS)
            yw = (xv * r) * w_ref[:, sl]
            o_ref[:, sl] = (yw * cs + _rot(yw) * sn).astype(o_ref.dtype)

    return pl.pallas_call(
        body, name=name, grid=(rows // ts,),
        in_specs=[_row(ts, width), _row(ts, LANES), _row(ts, LANES), _vec(width),
                  pl.BlockSpec((LANES, LANES), lambda i: (0, 0))],
        out_specs=_row(ts, width),
        out_shape=jax.ShapeDtypeStruct((rows, width), MXU_DT), compiler_params=_cp("parallel"),
    )(xin, cos, sin, w, _head_group_matrix())


def _qk_prep_bwd(xin, douts, width, cos, sin, w, *, tail=None, name):
    rows = xin.shape[0]
    ts = min(rows, 256)
    nch = width // LANES
    has_tail = tail is not None
    nd = len(douts)
    assert sum(v.shape[1] for v in douts) == width
    src = [(k, c) for k, v in enumerate(douts) for c in range(v.shape[1] // LANES)]

    def body(*refs):
        x_ref, d_refs = refs[0], refs[1:1 + nd]
        t_ref = refs[1 + nd] if has_tail else None
        cos_ref, sin_ref, w_ref, g_ref, dx_ref, dw_ref = refs[1 + nd + int(has_tail):]
        i = pl.program_id(0)
        cs, sn, g = cos_ref[...], sin_ref[...], g_ref[...]

        @pl.when(i == 0)
        def _():
            dw_ref[...] = jnp.zeros_like(dw_ref)

        if has_tail:
            dx_ref[:, width:width + LANES] = t_ref[...].astype(dx_ref.dtype)
            dx_ref[:, width + LANES:] = jnp.zeros((ts, GLU_OFF - width - LANES), dx_ref.dtype)

        for ch in range(nch):
            sl = slice(ch * LANES, (ch + 1) * LANES)
            xv = x_ref[:, sl]
            dv = d_refs[src[ch][0]][:, src[ch][1] * LANES:(src[ch][1] + 1) * LANES].astype(F32)
            r = lax.rsqrt(_group_sum(xv * xv, g) * (1.0 / HEAD_DIM) + EPS)
            n = xv * r
            dyw = dv * cs - _rot(dv * sn)
            dw_ref[:, sl] += _colsum(dyw * n)
            dn = dyw * w_ref[:, sl]
            gm = _group_sum(dn * n, g) * (1.0 / HEAD_DIM)
            dx_ref[:, sl] = (r * (dn - n * gm)).astype(dx_ref.dtype)

    ins = [xin] + list(douts) + ([tail] if has_tail else []) + [cos, sin, w, _head_group_matrix()]
    in_specs = ([_row(ts, width)] + [_row(ts, v.shape[1]) for v in douts] + ([_row(ts, LANES)] if has_tail else [])
                + [_row(ts, LANES), _row(ts, LANES), _vec(width), pl.BlockSpec((LANES, LANES), lambda i: (0, 0))])
    out_w, arr_w = (GLU_OFF, IN_PAD) if has_tail else (width, width)
    return pl.pallas_call(
        body, name=name, grid=(rows // ts,), in_specs=in_specs,
        out_specs=[_row(ts, out_w), _vec(width)],
        out_shape=[jax.ShapeDtypeStruct((rows, arr_w), MXU_DT), jax.ShapeDtypeStruct((1, width), F32)],
        compiler_params=_cp("arbitrary"),
    )(*ins)


GB = GQA_GROUP * BLOCK
WIN = 3 * BLOCK
ATT_STEP = 2


def _win_start(i, seq):
    return pl.multiple_of(jnp.clip((i - 1) * BLOCK, 0, seq - WIN), BLOCK)


def _attn_mask(i, start):
    qpos = i * BLOCK + (lax.broadcasted_iota(jnp.int32, (GB, WIN), 0) & (BLOCK - 1))
    kpos = start + lax.broadcasted_iota(jnp.int32, (GB, WIN), 1)
    return jnp.abs(qpos - kpos) <= WINDOW


def _sink_col(sink_ref, kv):
    return jnp.concatenate(
        [jnp.full((BLOCK, 1), sink_ref[0, kv * GQA_GROUP + g], F32) for g in range(GQA_GROUP)], axis=0)


def _stack_slots(ref, rows):
    return jnp.concatenate([ref[rows, g * LANES:(g + 1) * LANES] for g in range(GQA_GROUP)], axis=0)


def _kv_lanes(kv):
    lane = lax.broadcasted_iota(jnp.int32, (1, LANES), 1)
    return (lane < HEAD_DIM) if kv == 0 else (lane >= HEAD_DIM)


def _keep(mask, v):
    return jnp.where(mask, v, jnp.zeros_like(v))


_NT = (((1,), (1,)), ((), ()))
_TN = (((0,), (0,)), ((), ()))


def _attn_specs(seq, n_ctx):
    qs = pl.BlockSpec((ATT_STEP * BLOCK, Q_COLS), lambda i: (i, 0))
    ks = pl.BlockSpec((seq, KV_COLS), lambda i: (0, Q_COLS // KV_COLS))
    vs = pl.BlockSpec((seq, KV_COLS), lambda i: (0, QK_COLS // KV_COLS))
    kcs = pl.BlockSpec((n_ctx, KV_COLS), lambda i: (0, 0))
    vcs = pl.BlockSpec((n_ctx, KV_COLS), lambda i: (0, 1))
    ls = pl.BlockSpec((ATT_STEP, N_KV_HEADS * GB, 1), lambda i: (i, 0, 0))
    return qs, ks, vs, kcs, vcs, ls


def _attn_fwd(sink, qk, proj, kc, kv_ctx, *, carry=None, name):
    seq, n_ctx = qk.shape[0], kc.shape[0]
    scale = 1.0 / math.sqrt(HEAD_DIM)

    def one_block(blk, sub, sink_ref, q_ref, k_ref, v_ref, kcw, vcw, o_ref, lse_ref):
        rows = slice(sub * BLOCK, (sub + 1) * BLOCK)
        start = _win_start(blk, seq)
        valid = _attn_mask(blk, start)
        qs = _stack_slots(q_ref, rows)
        kw, vw = k_ref[pl.ds(start, WIN), :], v_ref[pl.ds(start, WIN), :].astype(MXU_DT)
        o_all = jnp.zeros((GB, LANES), F32)
        for kv in range(N_KV_HEADS):
            mine = _kv_lanes(kv)
            s_loc = lax.dot_general(qs, _keep(mine, kw), _NT, preferred_element_type=F32) * scale
            s_loc = jnp.where(valid, s_loc, NEG_INF)
            s_ctx = lax.dot_general(qs, _keep(mine, kcw), _NT, preferred_element_type=F32) * scale
            sk = _sink_col(sink_ref, kv)
            m = jnp.maximum(jnp.maximum(jnp.max(s_loc, axis=-1, keepdims=True),
                                        jnp.max(s_ctx, axis=-1, keepdims=True)), sk)
            p_loc = jnp.exp(s_loc - m)
            p_ctx = jnp.exp(s_ctx - m)
            l = (jnp.sum(p_loc, axis=-1, keepdims=True) + jnp.sum(p_ctx, axis=-1, keepdims=True)
                 + jnp.exp(sk - m))
            o_all = o_all + (jnp.dot(p_loc.astype(MXU_DT), _keep(mine, vw), preferred_element_type=F32)
                             + jnp.dot(p_ctx.astype(MXU_DT), _keep(mine, vcw), preferred_element_type=F32)) / l
            lse_ref[sub, kv * GB:(kv + 1) * GB, :] = m + jnp.log(l)
        for g in range(GQA_GROUP):
            o_ref[rows, g * LANES:(g + 1) * LANES] = o_all[g * BLOCK:(g + 1) * BLOCK].astype(o_ref.dtype)

    def body(sink_ref, q_ref, k_ref, v_ref, kc_ref, vc_ref, o_ref, lse_ref):
        i = pl.program_id(0)
        kcw, vcw = kc_ref[...], vc_ref[...].astype(MXU_DT)
        for sub in range(ATT_STEP):
            one_block(i * ATT_STEP + sub, sub, sink_ref, q_ref, k_ref, v_ref, kcw, vcw, o_ref, lse_ref)

    qs, ks, vs, kcs, vcs, ls = _attn_specs(seq, n_ctx)
    return _grid_call(
        body, [sink, qk, qk, proj, kc, kv_ctx], carry, name=name, grid=(seq // (ATT_STEP * BLOCK),),
        in_specs=[pl.BlockSpec(memory_space=pltpu.SMEM), qs, ks, vs, kcs, vcs],
        out_specs=[qs, ls],
        out_shape=[jax.ShapeDtypeStruct((seq, Q_COLS), MXU_DT),
                   jax.ShapeDtypeStruct((seq // BLOCK, N_KV_HEADS * GB, 1), F32)],
        dims=("parallel",))


def _attn_bwd(sink, qk, proj, kc, kv_ctx, o, lse, do, *, carry=None, name):
    seq, n_ctx = qk.shape[0], kc.shape[0]
    scale = 1.0 / math.sqrt(HEAD_DIM)

    def body(sink_ref, q_ref, k_ref, v_ref, kc_ref, vc_ref, o_ref, lse_ref, do_ref,
             dq_ref, dk_ref, dv_ref, dkc_ref, dvc_ref, ds_ref):
        i = pl.program_id(0)

        @pl.when(i == 0)
        def _():
            dk_ref[...] = jnp.zeros_like(dk_ref)
            dv_ref[...] = jnp.zeros_like(dv_ref)
            dkc_ref[...] = jnp.zeros_like(dkc_ref)
            dvc_ref[...] = jnp.zeros_like(dvc_ref)
            ds_ref[...] = jnp.zeros_like(ds_ref)

        kcw, vcw = kc_ref[...], vc_ref[...].astype(MXU_DT)
        dkc, dvc = jnp.zeros((n_ctx, LANES), F32), jnp.zeros((n_ctx, LANES), F32)
        for sub in range(ATT_STEP):
            dkc_s, dvc_s = one_block(i * ATT_STEP + sub, sub, sink_ref, q_ref, k_ref, v_ref, kcw, vcw, o_ref,
                                     lse_ref, do_ref, dq_ref, dk_ref, dv_ref, ds_ref)
            dkc, dvc = dkc + dkc_s, dvc + dvc_s
        dkc_ref[...] += dkc
        dvc_ref[...] += dvc

    def one_block(blk, sub, sink_ref, q_ref, k_ref, v_ref, kcw, vcw, o_ref, lse_ref, do_ref,
                  dq_ref, dk_ref, dv_ref, ds_ref):
        qrows = slice(sub * BLOCK, (sub + 1) * BLOCK)
        start = _win_start(blk, seq)
        valid = _attn_mask(blk, start)
        win = pl.ds(start, WIN)
        qs, dos = _stack_slots(q_ref, qrows), _stack_slots(do_ref, qrows)
        do_o = dos.astype(F32) * _stack_slots(o_ref, qrows).astype(F32)
        kw, vw = k_ref[win, :], v_ref[win, :].astype(MXU_DT)
        dq = jnp.zeros((GB, LANES), F32)
        dk, dv = jnp.zeros((WIN, LANES), F32), jnp.zeros((WIN, LANES), F32)
        dkc, dvc = jnp.zeros((n_ctx, LANES), F32), jnp.zeros((n_ctx, LANES), F32)
        for kv in range(N_KV_HEADS):
            mine = _kv_lanes(kv)
            rows = slice(kv * GB, (kv + 1) * GB)
            lse_s = lse_ref[sub, rows, :]
            delta = jnp.sum(_keep(mine, do_o), axis=-1, keepdims=True)
            kz, vz, kcz, vcz = _keep(mine, kw), _keep(mine, vw), _keep(mine, kcw), _keep(mine, vcw)
            s_loc = lax.dot_general(qs, kz, _NT, preferred_element_type=F32) * scale
            s_loc = jnp.where(valid, s_loc, NEG_INF)
            s_ctx = lax.dot_general(qs, kcz, _NT, preferred_element_type=F32) * scale
            p_loc = jnp.exp(s_loc - lse_s)
            p_ctx = jnp.exp(s_ctx - lse_s)
            p_sink = jnp.exp(_sink_col(sink_ref, kv) - lse_s)
            dp_loc = lax.dot_general(dos, vz, _NT, preferred_element_type=F32)
            dp_ctx = lax.dot_general(dos, vcz, _NT, preferred_element_type=F32)
            ds_loc = (p_loc * (dp_loc - delta) * scale).astype(MXU_DT)
            ds_ctx = (p_ctx * (dp_ctx - delta) * scale).astype(MXU_DT)
            dq = dq + (jnp.dot(ds_loc, kz, preferred_element_type=F32)
                       + jnp.dot(ds_ctx, kcz, preferred_element_type=F32))
            dk = dk + _keep(mine, lax.dot_general(ds_loc, qs, _TN, preferred_element_type=F32))
            dv = dv + _keep(mine, lax.dot_general(p_loc.astype(MXU_DT), dos, _TN, preferred_element_type=F32))
            dkc = dkc + _keep(mine, lax.dot_general(ds_ctx, qs, _TN, preferred_element_type=F32))
            dvc = dvc + _keep(mine, lax.dot_general(p_ctx.astype(MXU_DT), dos, _TN, preferred_element_type=F32))
            ds_ref[rows, :] += -(p_sink * delta)
        for g in range(GQA_GROUP):
            dq_ref[qrows, g * LANES:(g + 1) * LANES] = dq[g * BLOCK:(g + 1) * BLOCK]
        dk_ref[win, :] += dk
        dv_ref[win, :] += dv
        return dkc, dvc

    qs, ks, vs, kcs, vcs, ls = _attn_specs(seq, n_ctx)
    whole = lambda r, c: pl.BlockSpec((r, c), lambda i: (0, 0))
    return _grid_call(
        body, [sink, qk, qk, proj, kc, kv_ctx, o, lse, do], carry, name=name, grid=(seq // (ATT_STEP * BLOCK),),
        in_specs=[pl.BlockSpec(memory_space=pltpu.SMEM), qs, ks, vs, kcs, vcs, qs, ls, qs],
        out_specs=[qs, whole(seq, KV_COLS), whole(seq, KV_COLS), whole(n_ctx, KV_COLS), whole(n_ctx, KV_COLS),
                   whole(N_KV_HEADS * GB, 1)],
        out_shape=[jax.ShapeDtypeStruct((seq, Q_COLS), F32), jax.ShapeDtypeStruct((seq, KV_COLS), F32),
                   jax.ShapeDtypeStruct((seq, KV_COLS), F32), jax.ShapeDtypeStruct((n_ctx, KV_COLS), F32),
                   jax.ShapeDtypeStruct((n_ctx, KV_COLS), F32), jax.ShapeDtypeStruct((N_KV_HEADS * GB, 1), F32)],
        dims=("arbitrary",))


def _halo_specs(ts, w, rows, col=0):
    per = ts // HALO
    last = rows // HALO - 1
    return [pl.BlockSpec((HALO, w), lambda i: (jnp.maximum(i * per - 1, 0), col)),
            pl.BlockSpec((ts, w), lambda i: (i, col)),
            pl.BlockSpec((HALO, w), lambda i: (jnp.minimum((i + 1) * per, last), col))]


def _glu(v):
    return v[:, :CONV_CH] * _sigmoid(v[:, CONV_CH:])


def _ln_stats(u):
    mu = jnp.mean(u, axis=-1, keepdims=True)
    xc = u - mu
    rstd = lax.rsqrt(jnp.mean(xc * xc, axis=-1, keepdims=True) + EPS)
    return xc * rstd, rstd


def _phases(ext_ref, ph_ref):
    n = ph_ref.shape[1]
    for b in range(1, 8):
        ph_ref[b - 1] = ext_ref[b:b + n, :]


def _window(ext_ref, ph_ref, o, n, cs):
    a, b = divmod(o, 8)
    src = ext_ref if b == 0 else ph_ref.at[b - 1]
    return src[8 * a:8 * a + n, cs]


def _conv_fwd(glu, col, cw, cb, lw, lb, *, carry=None, name):
    rows = glu.shape[0]
    ts = min(rows, 256)
    nt = rows // ts

    te = ts + 2 * HALO

    def body(gp_ref, g_ref, gn_ref, cw_ref, cb_ref, lw_ref, lb_ref, u3_ref, u1_ref, ext_ref, ph_ref):
        i = pl.program_id(0)
        ext_ref[0:HALO, :] = jnp.where(i > 0, _glu(gp_ref[...]), 0.0)
        ext_ref[HALO:HALO + ts, :] = _glu(g_ref[...])
        ext_ref[HALO + ts:, :] = jnp.where(i < nt - 1, _glu(gn_ref[...]), 0.0)
        _phases(ext_ref, ph_ref)
        for c in range(CONV_CH // LANES):
            cs = slice(c * LANES, (c + 1) * LANES)
            acc = jnp.broadcast_to(cb_ref[:, cs], (ts, LANES))
            for j in range(CONV_K):
                acc = acc + cw_ref[j:j + 1, cs] * _window(ext_ref, ph_ref, HALO - CONV_PAD + j, ts, cs)
            u1_ref[:, cs] = acc
        xh, _ = _ln_stats(u1_ref[...])
        u2 = xh * lw_ref[...] + lb_ref[...]
        u3_ref[...] = (u2 * _sigmoid(u2)).astype(u3_ref.dtype)

    full = lambda shape: pl.BlockSpec(shape, lambda i: (0,) * len(shape))
    return _grid_call(
        body, [glu, glu, glu, cw, cb, lw, lb], carry, name=name, grid=(nt,),
        in_specs=_halo_specs(ts, GLU_COLS, rows, col) + [full((CONV_K, CONV_CH))] + [_vec(CONV_CH)] * 3,
        out_specs=[_row(ts, CONV_CH), _row(ts, CONV_CH)],
        out_shape=[jax.ShapeDtypeStruct((rows, CONV_CH), MXU_DT), jax.ShapeDtypeStruct((rows, CONV_CH), F32)],
        scratch_shapes=[pltpu.VMEM((te, CONV_CH), F32), pltpu.VMEM((7, te - 8, CONV_CH), F32)],
        dims=("parallel",))


def _conv_bwd(glu, col, u1, du3, dproj, cw, lw, lb, *, carry=None, name):
    rows = glu.shape[0]
    ts = min(rows, 256)
    nt = rows // ts
    te = ts + 2 * HALO

    def du1_of(u1v, du3v, lw_v, lb_v):
        xh, rstd = _ln_stats(u1v)
        u2 = xh * lw_v + lb_v
        sg = _sigmoid(u2)
        du2 = du3v * (sg * (1.0 + u2 * (1.0 - sg)))
        dxh = du2 * lw_v
        du1 = rstd * (dxh - jnp.mean(dxh, axis=-1, keepdims=True)
                      - xh * jnp.mean(dxh * xh, axis=-1, keepdims=True))
        return du1, du2, xh

    half = ts // 2

    def body(gp_ref, g_ref, gn_ref, up_ref, u_ref, un_ref, dp_ref, d_ref, dn_ref, cw_ref, lw_ref, lb_ref,
             _, dglu_ref, dcw_ref, dvec_ref, u0_ref, du1_ref, pu_ref, pd_ref, du0_ref):
        i = pl.program_id(0)
        lw_v, lb_v = lw_ref[...], lb_ref[...]

        @pl.when(i == 0)
        def _():
            dcw_ref[...] = jnp.zeros_like(dcw_ref)
            dvec_ref[...] = jnp.zeros_like(dvec_ref)

        gv = g_ref[...]
        u0_ref[0:HALO, :] = jnp.where(i > 0, _glu(gp_ref[...]), 0.0)
        u0_ref[HALO:HALO + ts, :] = _glu(gv)
        u0_ref[HALO + ts:, :] = jnp.where(i < nt - 1, _glu(gn_ref[...]), 0.0)
        d_prev, _, _ = du1_of(up_ref[...], dp_ref[...], lw_v, lb_v)
        d_main, du2, xh = du1_of(u_ref[...], d_ref[...], lw_v, lb_v)
        d_next, _, _ = du1_of(un_ref[...], dn_ref[...], lw_v, lb_v)
        du1_ref[0:HALO, :] = jnp.where(i > 0, d_prev, 0.0)
        du1_ref[HALO:HALO + ts, :] = d_main
        du1_ref[HALO + ts:, :] = jnp.where(i < nt - 1, d_next, 0.0)

        rid = lax.broadcasted_iota(jnp.int32, (8, CONV_CH), 0)
        dvec_ref[...] += (jnp.where(rid == 0, _colsum(d_main), 0.0)
                          + jnp.where(rid == 1, _colsum(du2 * xh), 0.0)
                          + jnp.where(rid == 2, _colsum(du2), 0.0))
        _phases(u0_ref, pu_ref)
        _phases(du1_ref, pd_ref)
        for c in range(CONV_CH // LANES):
            cs = slice(c * LANES, (c + 1) * LANES)
            for r0 in (0, half):
                dm = du1_ref[HALO + r0:HALO + r0 + half, cs]
                acc = jnp.zeros((half, LANES), F32)
                for j in range(CONV_K):
                    acc = acc + cw_ref[j:j + 1, cs] * _window(du1_ref, pd_ref, r0 + HALO + CONV_PAD - j, half, cs)
                    prod = dm * _window(u0_ref, pu_ref, r0 + HALO - CONV_PAD + j, half, cs)
                    dcw_ref[j, :, cs] += jnp.sum(prod.reshape(half // 8, 8, LANES), axis=0)
                du0_ref[r0:r0 + half, cs] = acc
        du0 = du0_ref[...]
        ga, sg = gv[:, :CONV_CH], _sigmoid(gv[:, CONV_CH:])
        dglu_ref[:, :CONV_CH] = (du0 * sg).astype(dglu_ref.dtype)
        dglu_ref[:, CONV_CH:] = (du0 * ga * sg * (1.0 - sg)).astype(dglu_ref.dtype)

    full = lambda shape: pl.BlockSpec(shape, lambda i: (0,) * len(shape))
    return _grid_call(
        body, [glu, glu, glu, u1, u1, u1, du3, du3, du3, cw, lw, lb, dproj], carry, name=name, grid=(nt,),
        in_specs=(_halo_specs(ts, GLU_COLS, rows, col) + _halo_specs(ts, CONV_CH, rows)
                  + _halo_specs(ts, CONV_CH, rows) + [full((CONV_K, CONV_CH)), _vec(CONV_CH), _vec(CONV_CH)]
                  + [pl.BlockSpec(memory_space=pl.ANY)]),
        out_specs=[_row(ts, GLU_COLS, col), full((CONV_K, 8, CONV_CH)), full((8, CONV_CH))],
        out_shape=[jax.ShapeDtypeStruct(dproj.shape, dproj.dtype),
                   jax.ShapeDtypeStruct((CONV_K, 8, CONV_CH), F32), jax.ShapeDtypeStruct((8, CONV_CH), F32)],
        scratch_shapes=[pltpu.VMEM((te, CONV_CH), F32), pltpu.VMEM((te, CONV_CH), F32),
                        pltpu.VMEM((7, te - 8, CONV_CH), F32), pltpu.VMEM((7, te - 8, CONV_CH), F32),
                        pltpu.VMEM((ts, CONV_CH), F32)],
        input_output_aliases={12: 0}, dims=("arbitrary",))


FFN_CW = 1408
FFN_NJ = FFN_H // FFN_CW


def _ffn_halo_specs(ts, rows, col_of, inner_rows):
    per = ts // HALO
    last = rows // HALO - 1
    if inner_rows:
        return [pl.BlockSpec((HALO, FFN_CW), lambda j, i: (jnp.maximum(i * per - 1, 0), col_of(j))),
                pl.BlockSpec((ts, FFN_CW), lambda j, i: (i, col_of(j))),
                pl.BlockSpec((HALO, FFN_CW), lambda j, i: (jnp.minimum((i + 1) * per, last), col_of(j)))]
    return [pl.BlockSpec((HALO, FFN_CW), lambda i, j: (jnp.maximum(i * per - 1, 0), col_of(j))),
            pl.BlockSpec((ts, FFN_CW), lambda i, j: (i, col_of(j))),
            pl.BlockSpec((HALO, FFN_CW), lambda i, j: (jnp.minimum((i + 1) * per, last), col_of(j)))]


def _ffn_ext(p_ref, m_ref, n_ref, sl, i, nt):
    return jnp.concatenate([jnp.where(i > 0, p_ref[:, sl].astype(F32), 0.0), m_ref[:, sl].astype(F32),
                            jnp.where(i < nt - 1, n_ref[:, sl].astype(F32), 0.0)], axis=0)


def _prev_row(v):
    return pltpu.roll(v, 1, 0)


def _next_row(v):
    return pltpu.roll(v, v.shape[0] - 1, 0)


def _ffn_act(up0, w3, b3, *, name):
    rows = up0.shape[0]
    ts = min(rows, 256)
    nt = rows // ts
    main = slice(HALO, HALO + ts)

    def body(gp, g, gn, vp, v, vn, wg, wv, bg, bv, a_ref, go_ref, vo_ref):
        i = pl.program_id(0)
        for ch in range(FFN_CW // LANES):
            sl = slice(ch * LANES, (ch + 1) * LANES)
            xg, xv = _ffn_ext(gp, g, gn, sl, i, nt), _ffn_ext(vp, v, vn, sl, i, nt)
            wgv, wvv = wg[:, sl], wv[:, sl]
            gate = (wgv[0:1] * _prev_row(xg) + wgv[1:2] * xg + wgv[2:3] * _next_row(xg))[main] + bg[:, sl]
            val = (wvv[0:1] * _prev_row(xv) + wvv[1:2] * xv + wvv[2:3] * _next_row(xv))[main] + bv[:, sl]
            a_ref[:, sl] = (gate * _sigmoid(gate) * val).astype(a_ref.dtype)
            go_ref[:, sl] = gate.astype(go_ref.dtype)
            vo_ref[:, sl] = val.astype(vo_ref.dtype)

    gcol, vcol = (lambda j: j), (lambda j: j + FFN_NJ)
    wspec = lambda col_of: pl.BlockSpec((FFN_K, FFN_CW), lambda i, j: (0, col_of(j)))
    bspec = lambda col_of: pl.BlockSpec((1, FFN_CW), lambda i, j: (0, col_of(j)))
    ospec = pl.BlockSpec((ts, FFN_CW), lambda i, j: (i, j))
    return pl.pallas_call(
        body, name=name, grid=(nt, FFN_NJ),
        in_specs=(_ffn_halo_specs(ts, rows, gcol, False) + _ffn_halo_specs(ts, rows, vcol, False)
                  + [wspec(gcol), wspec(vcol), bspec(gcol), bspec(vcol)]),
        out_specs=[ospec] * 3, out_shape=[jax.ShapeDtypeStruct((rows, FFN_H), MXU_DT)] * 3,
        compiler_params=_cp("parallel", "parallel"),
    )(up0, up0, up0, up0, up0, up0, w3, w3, b3, b3)


def _ffn_act_bwd(up0, gate_s, val_s, da, w3, *, carry=None, name):
    rows = up0.shape[0]
    ts = min(rows, 256)
    nt = rows // ts
    main = slice(HALO, HALO + ts)

    def body(gp, g, gn, vp, v, vn, sgp, sg, sgn, svp, sv, svn, ap, a, an, wg, wv,
             dg_ref, dv_ref, sg_ref, sv_ref):
        i = pl.program_id(1)

        @pl.when(i == 0)
        def _():
            sg_ref[...] = jnp.zeros_like(sg_ref)
            sv_ref[...] = jnp.zeros_like(sv_ref)

        rid = lax.broadcasted_iota(jnp.int32, (8, LANES), 0)
        for ch in range(FFN_CW // LANES):
            sl = slice(ch * LANES, (ch + 1) * LANES)
            xg, xv, da_e = _ffn_ext(gp, g, gn, sl, i, nt), _ffn_ext(vp, v, vn, sl, i, nt), _ffn_ext(ap, a, an, sl, i, nt)
            gate, val = _ffn_ext(sgp, sg, sgn, sl, i, nt), _ffn_ext(svp, sv, svn, sl, i, nt)
            wgv, wvv = wg[:, sl], wv[:, sl]
            xg_p, xg_n, xv_p, xv_n = _prev_row(xg), _next_row(xg), _prev_row(xv), _next_row(xv)
            sgm = _sigmoid(gate)
            eg = da_e * val * (sgm * (1.0 + gate * (1.0 - sgm)))
            ev = da_e * (gate * sgm)
            for e, taps, w, d_ref, s_ref in ((eg, (xg_p, xg, xg_n), wgv, dg_ref, sg_ref),
                                             (ev, (xv_p, xv, xv_n), wvv, dv_ref, sv_ref)):
                d0 = w[0:1] * _next_row(e) + w[1:2] * e + w[2:3] * _prev_row(e)
                d_ref[:, sl] = d0[main].astype(d_ref.dtype)
                dm = e[main]
                s_ref[:, sl] += (jnp.where(rid == 0, _colsum(dm * taps[0][main]), 0.0)
                                 + jnp.where(rid == 1, _colsum(dm * taps[1][main]), 0.0)
                                 + jnp.where(rid == 2, _colsum(dm * taps[2][main]), 0.0)
                                 + jnp.where(rid == 3, _colsum(dm), 0.0))

    gcol, vcol = (lambda j: j), (lambda j: j + FFN_NJ)
    wspec = lambda col_of: pl.BlockSpec((FFN_K, FFN_CW), lambda j, i: (0, col_of(j)))
    ospec = pl.BlockSpec((ts, FFN_CW), lambda j, i: (i, j))
    sspec = pl.BlockSpec((8, FFN_CW), lambda j, i: (0, j))
    return _grid_call(
        body, [up0] * 6 + [gate_s] * 3 + [val_s] * 3 + [da] * 3 + [w3, w3], carry, name=name, grid=(FFN_NJ, nt),
        in_specs=(_ffn_halo_specs(ts, rows, gcol, True) + _ffn_halo_specs(ts, rows, vcol, True)
                  + _ffn_halo_specs(ts, rows, gcol, True) * 3 + [wspec(gcol), wspec(vcol)]),
        out_specs=[ospec, ospec, sspec, sspec],
        out_shape=[jax.ShapeDtypeStruct((rows, FFN_H), MXU_DT), jax.ShapeDtypeStruct((rows, FFN_H), MXU_DT),
                   jax.ShapeDtypeStruct((8, FFN_H), F32), jax.ShapeDtypeStruct((8, FFN_H), F32)],
        dims=("parallel", "arbitrary"))


def _adam_math(w, g, m, v):
    m = ADAM_B1 * m + (1.0 - ADAM_B1) * g
    v = ADAM_B2 * v + (1.0 - ADAM_B2) * (g * g)
    m_hat = m / (1.0 - ADAM_B1 ** ADAM_STEP)
    v_hat = v / (1.0 - ADAM_B2 ** ADAM_STEP)
    delta = -ADAM_LR * (m_hat / (jnp.sqrt(v_hat) + ADAM_EPS) + ADAM_WD * w)
    return delta, m, v


ROW_TILE_BYTES = 8 << 20


def _row_tile(rows, row_bytes):
    tiles = [rows] + [rows // k for k in range(2, rows // 16 + 1) if rows % k == 0 and (rows // k) % 16 == 0]
    return next(t for t in tiles if t * row_bytes <= ROW_TILE_BYTES)


def _adam(w, m, v, parts, *, name):
    rows, cols = w.shape
    nparts = parts.shape[0]
    tr = _row_tile(rows, cols * (7 * 4 + nparts * parts.dtype.itemsize))

    def body(w_ref, m_ref, v_ref, p_ref, g_ref, d_ref, nm_ref, nv_ref):
        g = p_ref[0].astype(F32)
        for p in range(1, nparts):
            g = g + p_ref[p].astype(F32)
        g_ref[...] = g
        d_ref[...], nm_ref[...], nv_ref[...] = _adam_math(w_ref[...], g, m_ref[...], v_ref[...])

    spec = _row(tr, cols)
    return pl.pallas_call(
        body, name=name, grid=(rows // tr,),
        in_specs=[spec, spec, spec, pl.BlockSpec((nparts, tr, cols), lambda i: (0, i, 0))],
        out_specs=[spec] * 4, out_shape=[jax.ShapeDtypeStruct((rows, cols), F32)] * 4,
        compiler_params=_cp("parallel"),
    )(w, m, v, parts)


def _adam_many(ws, ms, vs, gs, *, name):
    n = len(ws)

    def body(*refs):
        ins, outs = refs[:4 * n], refs[4 * n:]
        for k in range(n):
            delta, new_m, new_v = _adam_math(ins[k][...], ins[3 * n + k][...], ins[n + k][...], ins[2 * n + k][...])
            outs[k][...], outs[n + k][...], outs[2 * n + k][...] = delta, new_m, new_v

    vm = pl.BlockSpec(memory_space=pltpu.VMEM)
    res = pl.pallas_call(
        body, name=name, in_specs=[vm] * (4 * n), out_specs=[vm] * (3 * n),
        out_shape=[jax.ShapeDtypeStruct(w.shape, F32) for w in ws] * 3,
        compiler_params=pltpu.CompilerParams(vmem_limit_bytes=VMEM_LIMIT),
    )(*ws, *ms, *vs, *gs)
    return res[:n], res[n:2 * n], res[2 * n:]


def _sum_parts(parts, *, name):
    nparts, rows, cols = parts.shape
    tr = _row_tile(rows, cols * (4 + nparts * parts.dtype.itemsize))

    def body(p_ref, o_ref):
        g = p_ref[0].astype(F32)
        for p in range(1, nparts):
            g = g + p_ref[p].astype(F32)
        o_ref[...] = g

    return pl.pallas_call(
        body, name=name, grid=(rows // tr,),
        in_specs=[pl.BlockSpec((nparts, tr, cols), lambda i: (0, i, 0))], out_specs=_row(tr, cols),
        out_shape=jax.ShapeDtypeStruct((rows, cols), F32), compiler_params=_cp("parallel"),
    )(parts)


def _my_place():
    return lax.axis_index("x"), lax.axis_index("y"), lax.axis_index("c")


def _dev_index(p):
    return 4 * p[0] + 2 * p[1] + p[2]


def _all_gather(xs, *, hbm, name):
    return _run_comm(_gather_plan(xs), pl.ANY if hbm else pltpu.VMEM, name)


class _Comm(NamedTuple):
    ins: list
    outs: list
    n_remote: int
    n_local: int
    start: Callable
    finish: Callable


def _comm_scratch(plan):
    return [pltpu.SemaphoreType.DMA((plan.n_remote,)), pltpu.SemaphoreType.DMA((plan.n_remote,)),
            pltpu.SemaphoreType.DMA((plan.n_local,))]


def _run_comm(plan, space, name):
    n_in, n_out = len(plan.ins), len(plan.outs)

    def body(*refs):
        args = (refs[:n_in], refs[n_in:n_in + n_out], *refs[n_in + n_out:])
        plan.start(*args)
        plan.finish(*args)

    return pl.pallas_call(
        body, name=name, out_shape=plan.outs,
        in_specs=[pl.BlockSpec(memory_space=space)] * n_in, out_specs=[pl.BlockSpec(memory_space=space)] * n_out,
        scratch_shapes=_comm_scratch(plan),
        compiler_params=pltpu.CompilerParams(vmem_limit_bytes=VMEM_LIMIT),
    )(*plan.ins)


def _grid_call(body, ins, carry, *, name, grid, in_specs, out_specs, out_shape, dims, scratch_shapes=(),
               input_output_aliases=None):
    if carry is None:
        res = pl.pallas_call(
            body, name=name, grid=grid, in_specs=list(in_specs), out_specs=list(out_specs),
            out_shape=list(out_shape), scratch_shapes=list(scratch_shapes),
            input_output_aliases=input_output_aliases or {}, compiler_params=_cp(*dims))(*ins)
        return list(res), None

    def at(pos):
        conds = [pl.program_id(k) == p for k, p in enumerate(pos)]
        out = conds[0]
        for cnd in conds[1:]:
            out = jnp.logical_and(out, cnd)
        return out

    return _carried_call(body, carry, lambda: at([0] * len(grid)), lambda: at([g - 1 for g in grid]), ins,
                         name=name, grid=grid, in_specs=in_specs, out_specs=out_specs, out_shape=out_shape,
                         scratch_shapes=scratch_shapes, input_output_aliases=input_output_aliases)


def _carried_call(body, plan, first, last, ins, *, name, grid, in_specs, out_specs, out_shape, scratch_shapes=(),
                  input_output_aliases=None):
    in_specs, out_specs, out_shape = list(in_specs), list(out_specs), list(out_shape)
    n_in, n_out, n_scr = len(in_specs), len(out_specs), len(scratch_shapes)
    c_in, c_out = len(plan.ins), len(plan.outs)
    hbm = pl.BlockSpec(memory_space=pl.ANY)

    def full_body(*refs):
        ins, c_ins = refs[:n_in], refs[n_in:n_in + c_in]
        outs = refs[n_in + c_in:n_in + c_in + n_out]
        c_outs = refs[n_in + c_in + n_out:n_in + c_in + n_out + c_out]
        scr = refs[n_in + c_in + n_out + c_out:]
        sems = scr[n_scr:]

        @pl.when(first())
        def _():
            plan.start(c_ins, c_outs, *sems)

        body(*ins, *outs, *scr[:n_scr])

        @pl.when(last())
        def _():
            plan.finish(c_ins, c_outs, *sems)

    res = pl.pallas_call(
        full_body, name=name, grid=grid,
        in_specs=in_specs + [hbm] * c_in, out_specs=out_specs + [hbm] * c_out,
        out_shape=out_shape + list(plan.outs),
        scratch_shapes=list(scratch_shapes) + _comm_scratch(plan),
        input_output_aliases=input_output_aliases or {},
        compiler_params=_cp(*(["arbitrary"] * len(grid))),
    )(*ins, *plan.ins)
    return list(res[:n_out]), list(res[n_out:])


def _gather_plan(xs):
    n = len(xs)
    ms = [v.shape[0] for v in xs]

    def tools(x_refs, o_refs, send_sems, recv_sems, local_sems):
        x, y, c = _my_place()
        me, sib = (x, y, c), (x, y, 1 - c)
        chips = [(1 - x, y), (x, 1 - y), (1 - x, 1 - y)]

        def rows(a, p):
            return o_refs[a].at[pl.ds(pl.multiple_of(_dev_index(p) * ms[a], 8), ms[a])]

        def copy(a, k, block, to, src=None):
            return pltpu.make_async_remote_copy(
                src_ref=rows(a, block) if src is None else src, dst_ref=rows(a, block),
                send_sem=send_sems.at[a * 7 + k], recv_sem=recv_sems.at[a * 7 + k],
                device_id=to, device_id_type=MESH)

        mine = [pltpu.make_async_copy(x_refs[a], rows(a, me), local_sems.at[a]) for a in range(n)]
        first = []
        for a in range(n):
            first.append(copy(a, 0, me, sib, src=x_refs[a]))
            first += [copy(a, 1 + j, me, (*chip, c), src=x_refs[a]) for j, chip in enumerate(chips)]
        return me, sib, chips, c, copy, mine, first

    def start(*refs):
        _, _, _, _, _, mine, first = tools(*refs)
        for cp in mine + first:
            cp.start()

    def finish(*refs):
        me, sib, chips, c, copy, mine, first = tools(*refs)
        passed = []
        for j, chip in enumerate(chips):
            for a in range(n):
                copy(a, 1 + j, (*chip, c), me).wait_recv()
                cp = copy(a, 4 + j, (*chip, c), sib)
                cp.start()
                passed.append(cp)
        for a in range(n):
            copy(a, 0, sib, me).wait_recv()
            for j, chip in enumerate(chips):
                copy(a, 4 + j, (*chip, 1 - c), me).wait_recv()
        for cp in first + passed:
            cp.wait_send()
        for cp in mine:
            cp.wait()

    outs = [jax.ShapeDtypeStruct((N_DEV * v.shape[0], v.shape[1]), v.dtype) for v in xs]
    return _Comm(list(xs), outs, 7 * n, n, start, finish)


def _exchange_plan(gs, cols=None):
    n = len(gs)
    rs = [v.shape[0] // N_DEV for v in gs]
    flips = [(bx, by, bc) for bx in (0, 1) for by in (0, 1) for bc in (0, 1)][1:]

    def tools(g_refs, o_refs, send_sems, recv_sems, local_sems):
        x, y, c = _my_place()
        me = (x, y, c)

        def block(ref, a, p):
            return ref.at[pl.ds(_dev_index(p) * rs[a], rs[a])]

        def src(a, p):
            rows = pl.ds(_dev_index(p) * rs[a], rs[a])
            return g_refs[a].at[rows] if cols is None else g_refs[a].at[rows, pl.ds(cols[0], cols[1])]

        def peer(f):
            return (1 - x if f[0] else x, 1 - y if f[1] else y, 1 - c if f[2] else c)

        def copy(a, k, to):
            return pltpu.make_async_remote_copy(
                src_ref=src(a, to), dst_ref=block(o_refs[a], a, me),
                send_sem=send_sems.at[a * 7 + k], recv_sem=recv_sems.at[a * 7 + k],
                device_id=to, device_id_type=MESH)

        def arrival(a, k, frm):
            return pltpu.make_async_remote_copy(
                src_ref=src(a, frm), dst_ref=block(o_refs[a], a, frm),
                send_sem=send_sems.at[a * 7 + k], recv_sem=recv_sems.at[a * 7 + k],
                device_id=frm, device_id_type=MESH)

        mine = [pltpu.make_async_copy(src(a, me), block(o_refs[a], a, me), local_sems.at[a])
                for a in range(n)]
        sends = [copy(a, k, peer(f)) for a in range(n) for k, f in enumerate(flips)]
        arrivals = [arrival(a, k, peer(f)) for a in range(n) for k, f in enumerate(flips)]
        return mine, sends, arrivals

    def start(*refs):
        mine, sends, _ = tools(*refs)
        for cp in mine + sends:
            cp.start()

    def finish(*refs):
        mine, sends, arrivals = tools(*refs)
        for cp in arrivals:
            cp.wait_recv()
        for cp in sends:
            cp.wait_send()
        for cp in mine:
            cp.wait()

    outs = [jax.ShapeDtypeStruct((v.shape[0], v.shape[1] if cols is None else cols[1]), v.dtype) for v in gs]
    return _Comm(list(gs), outs, 7 * n, n, start, finish)


def _rope_tables(seq):
    t = jnp.arange(seq)
    row, col = t // GRID_W, t % GRID_W
    half = HEAD_DIM // 2
    inv = ROPE_BASE ** (-jnp.arange(0, half, 2, dtype=F32) / half)
    ang_r = row.astype(F32)[:, None] * inv
    ang_c = col.astype(F32)[:, None] * inv
    ang = jnp.concatenate([ang_r, ang_r, ang_c, ang_c], axis=-1)
    return jnp.tile(jnp.cos(ang), (1, 2)), jnp.tile(jnp.sin(ang), (1, 2))


def _to_slots(w):
    return w.reshape(N_KV_HEADS, GQA_GROUP, HEAD_DIM, w.shape[1]).transpose(1, 0, 2, 3).reshape(w.shape)


def _from_slots(w):
    return w.reshape(GQA_GROUP, N_KV_HEADS, HEAD_DIM, w.shape[1]).transpose(1, 0, 2, 3).reshape(w.shape)


def _pack(vs):
    flat = jnp.concatenate([v.reshape(-1).astype(F32) for v in vs])
    total = -(-flat.shape[0] // (8 * LANES)) * (8 * LANES)
    return jnp.pad(flat, (0, total - flat.shape[0])).reshape(-1, LANES)


def _unpack(packed, like):
    flat, out, off = packed.reshape(-1), [], 0
    for v in like:
        size = math.prod(v.shape)
        out.append(flat[off:off + size].reshape(v.shape))
        off += size
    return out


def _silu(v):
    return v * jax.nn.sigmoid(v)


def kernel(x, c, ctx, c_ctx, w_mod, b_mod, norm_mix_w, w_in, q_norm_w, k_norm_w, sink_logit, conv_w, conv_b, conv_norm_w, conv_norm_b, w_out, norm_ffn_w, w_up, ffn_conv_w, ffn_conv_b, w_down, loss_target, m_c_ctx, m_w_mod, m_b_mod, m_norm_mix_w, m_w_in, m_q_norm_w, m_k_norm_w, m_sink_logit, m_conv_w, m_conv_b, m_conv_norm_w, m_conv_norm_b, m_w_out, m_norm_ffn_w, m_w_up, m_ffn_conv_w, m_ffn_conv_b, m_w_down, v_c_ctx, v_w_mod, v_b_mod, v_norm_mix_w, v_w_in, v_q_norm_w, v_k_norm_w, v_sink_logit, v_conv_w, v_conv_b, v_conv_norm_w, v_conv_norm_b, v_w_out, v_norm_ffn_w, v_w_up, v_ffn_conv_w, v_ffn_conv_b, v_w_down):
    d = D_MODEL
    seq, n_ctx = x.shape[1], ctx.shape[1]
    me = _dev_index(_my_place())
    xs, ctxs, tgt = x[0], ctx[0], loss_target[0]

    small = _pack([c[0], conv_w[0], ffn_conv_w[0]])
    small_all = _all_gather([small], hbm=False, name="gather_small")[0].reshape(N_DEV, -1)
    n_cw, n_fw = conv_w[0].size, ffn_conv_w[0].size
    c_all = small_all[:, :d]
    cw_all = small_all[:, d:d + n_cw].reshape(N_DEV, CONV_K, -1)
    fw_all = small_all[:, d + n_cw:d + n_cw + n_fw].reshape(N_DEV, FFN_K, -1)
    conv_w_f = cw_all.transpose(1, 0, 2).reshape(CONV_K, CONV_CH)
    ffn_w_f = fw_all.transpose(1, 0, 2).reshape(FFN_K, 2 * FFN_H)

    mcols = w_mod.shape[2]
    act = jnp.zeros((16, d), F32).at[:N_DEV].set(_silu(c_all)).at[N_DEV].set(_silu(c_ctx))
    mod_part = _mm(act, w_mod[0], 16, mcols, d, name="mod_fwd")
    mod_all = _all_gather([mod_part], hbm=False, name="gather_mod")[0]
    mod_all = mod_all.reshape(N_DEV, 16, mcols).transpose(1, 0, 2).reshape(16, 6 * d) + b_mod
    mod = lax.dynamic_slice_in_dim(mod_all, me, 1, axis=0)
    sh1, sc1, g1, sh2, sc2, g2 = [mod[:, k * d:(k + 1) * d] for k in range(6)]
    sh1c, sc1c = mod_all[N_DEV:N_DEV + 1, :d], mod_all[N_DEV:N_DEV + 1, d:2 * d]

    cos, sin = _rope_tables(seq)
    ones_c, zeros_c = jnp.ones((n_ctx, LANES), F32), jnp.zeros((n_ctx, LANES), F32)
    qk_w = jnp.concatenate([jnp.tile(q_norm_w, (1, N_Q_HEADS)), jnp.tile(k_norm_w, (1, N_KV_HEADS))], axis=1)
    kc_w = jnp.tile(k_norm_w, (1, N_KV_HEADS))

    h, (w_in_t,) = _prenorm(xs, norm_mix_w, sc1, sh1, carry=_gather_plan([w_in[0].T.astype(MXU_DT)]),
                            name="prenorm_mix")
    hc, _ = _prenorm(ctxs, norm_mix_w, sc1c, sh1c, name="prenorm_ctx")
    w_in_p = jnp.concatenate([_to_slots(w_in_t[:Q_COLS]), w_in_t[Q_COLS:QKV_COLS],
                              jnp.zeros((GLU_OFF - QKV_COLS, d), MXU_DT), w_in_t[QKV_COLS:]])
    proj = _mm(h, w_in_p, seq, IN_PAD, d, tb=True, name="proj_in")
    kv_ctx = _mm(hc, w_in_p, n_ctx, 2 * KV_COLS, d, tb=True, n0=Q_COLS, name="proj_ctx")
    qk_r = _qk_prep(proj, QK_COLS, cos, sin, qk_w, name="qk_prep")
    kc_n = _qk_prep(kv_ctx, KV_COLS, ones_c, zeros_c, kc_w, name="k_ctx_prep")
    (attn_o, lse), (w_up_t, w_out_f) = _attn_fwd(
        sink_logit, qk_r, proj, kc_n, kv_ctx, name="attn_fwd",
        carry=_gather_plan([w_up[0].T.astype(MXU_DT), w_out[0].astype(MXU_DT)]))
    w_out_f = jnp.concatenate([_to_slots(w_out_f[:Q_COLS]), w_out_f[Q_COLS:]])
    (u3, u1), (w_down_f,) = _conv_fwd(proj, GLU_OFF // GLU_COLS, conv_w_f, conv_b, conv_norm_w, conv_norm_b,
                                      carry=_gather_plan([w_down[0].astype(MXU_DT)]), name="conv_fwd")
    mix = _mm(attn_o, w_out_f, seq, d, Q_COLS, name="out_attn")
    mix = _mm(u3, w_out_f, seq, d, CONV_CH, k0=Q_COLS, add=mix, name="out_conv")

    x1, h2 = _resid_prenorm(xs, mix, g1, norm_ffn_w, sc2, sh2, name="prenorm_ffn")
    up0 = _mm(h2, w_up_t, seq, 2 * FFN_H, d, tb=True, out_dtype=MXU_DT, name="ffn_up")
    act_a, gate_s, val_s = _ffn_act(up0, ffn_w_f, ffn_conv_b, name="ffn_act")
    ffn = _mm(act_a, w_down_f, seq, d, FFN_H, name="ffn_down")
    loss_p, dy, dffn, dg2 = _loss_head(x1, ffn, g2, tgt, name="loss_head")

    da = _mm(dffn, w_down_f, seq, FFN_H, d, tb=True, out_dtype=MXU_DT, name="ffn_down_dx")
    gw_down = _mm(act_a, dffn, FFN_H, d, seq, ta=True, out_dtype=MXU_DT, name="ffn_down_dw")
    (dgate0, dval0, s_gate, s_val), (rx_down,) = _ffn_act_bwd(
        up0, gate_s, val_s, da, ffn_w_f, carry=_exchange_plan([gw_down]), name="ffn_act_bwd")
    gw_up_t = _mm(dgate0, h2, FFN_H, d, seq, ta=True, out_dtype=MXU_DT, out_rows=2 * FFN_H, name="ffn_up_dw_gate")
    gw_up_t = _mm(dval0, h2, FFN_H, d, seq, ta=True, into=gw_up_t, m0=FFN_H, name="ffn_up_dw_val")
    half_d = d // 2
    dh2, (rx_up_lo,) = _mm(dgate0, w_up_t, seq, d, FFN_H, a2=dval0, name="ffn_up_dx",
                           carry=_exchange_plan([gw_up_t], cols=(0, half_d)))
    (dx1, dmix, dsh2, dsc2, dnw2, dg1), _ = _norm_bwd(
        dh2, x1, norm_ffn_w, sc2, res=dy, gate=(mix, g1), name="prenorm_ffn_bwd")

    dattn = _mm(dmix, w_out_f, seq, Q_COLS, d, tb=True, out_dtype=MXU_DT, name="out_dx_attn")
    du3 = _mm(dmix, w_out_f, seq, CONV_CH, d, tb=True, n0=Q_COLS, name="out_dx_conv")
    gw_out = _mm(attn_o, dmix, Q_COLS, d, seq, ta=True, out_dtype=MXU_DT, out_rows=Q_COLS + CONV_CH,
                 name="out_dw_attn")
    gw_out = _mm(u3, dmix, CONV_CH, d, seq, ta=True, into=gw_out, m0=Q_COLS, name="out_dw_conv")
    gw_out = jnp.concatenate([_from_slots(gw_out[:Q_COLS]), gw_out[Q_COLS:]])
    (dq, dk, dv, dkc_r, dvc, dsink_rows), (rx_out,) = _attn_bwd(
        sink_logit, qk_r, proj, kc_n, kv_ctx, attn_o, lse, dattn, carry=_exchange_plan([gw_out]), name="attn_bwd")
    dproj, dqk_w = _qk_prep_bwd(proj, [dq, dk], QK_COLS, cos, sin, qk_w, tail=dv, name="qk_prep_bwd")
    (dproj, dcw8, dvec), (rx_up_hi,) = _conv_bwd(
        proj, GLU_OFF // GLU_COLS, u1, du3, dproj, conv_w_f, conv_norm_w, conv_norm_b,
        carry=_exchange_plan([gw_up_t], cols=(half_d, half_d)), name="conv_bwd")
    dkc, dkc_w = _qk_prep_bwd(kv_ctx, [dkc_r], KV_COLS, ones_c, zeros_c, kc_w, name="k_ctx_prep_bwd")
    dkv_ctx = jnp.concatenate([dkc, dvc.astype(MXU_DT)], axis=1)
    gw_p = _mm(dproj, h, IN_PAD, d, seq, ta=True, name="proj_dw")
    gw_ctx = _mm(dkv_ctx, hc, 2 * KV_COLS, d, n_ctx, ta=True, name="proj_dw_ctx")
    gw_in_t = jnp.concatenate([_from_slots(gw_p[:Q_COLS]), gw_p[Q_COLS:QKV_COLS] + gw_ctx, gw_p[GLU_OFF:]],
                              axis=0).astype(MXU_DT)
    dh, (rx_in,) = _mm(dproj, w_in_p, seq, d, IN_PAD, carry=_exchange_plan([gw_in_t]), name="proj_dx")
    dhc = _mm(dkv_ctx, w_in_p, n_ctx, d, 2 * KV_COLS, k0=Q_COLS, name="proj_dx_ctx")
    (grad_x, dsh1, dsc1, dnw1), _ = _norm_bwd(dh, xs, norm_mix_w, sc1, res=dx1, name="prenorm_mix_bwd")
    (dsh1c, dsc1c, dnw1c), _ = _norm_bwd(dhc, ctxs, norm_mix_w, sc1c, want_dx=False, name="prenorm_ctx_bwd")

    dmod = jnp.concatenate([dsh1, dsc1, dg1, dsh2, dsc2, dg2], axis=1)
    dmod_ctx = jnp.concatenate([dsh1c, dsc1c], axis=1)
    d_qn = dqk_w[0, :Q_COLS].reshape(N_Q_HEADS, HEAD_DIM).sum(0)
    d_kn = (dqk_w[0, Q_COLS:].reshape(N_KV_HEADS, HEAD_DIM).sum(0)
            + dkc_w[0].reshape(N_KV_HEADS, HEAD_DIM).sum(0))
    d_ffn_w = jnp.concatenate([s_gate[:FFN_K], s_val[:FFN_K]], axis=1)
    d_ffn_b = jnp.concatenate([s_gate[FFN_K], s_val[FFN_K]])
    d_sink = dsink_rows.reshape(N_Q_HEADS, BLOCK).sum(1)
    summed_like = [(dnw1 + dnw1c), d_qn[None], d_kn[None], d_sink[None], dvec[0:1], dvec[1:2],
                   dvec[2:3], dnw2, d_ffn_b[None], dcw8.sum(1), d_ffn_w, loss_p[0:1, 0:1]]
    pack = _pack([dmod, dmod_ctx] + summed_like)
    pack_all = _all_gather([pack], hbm=False, name="gather_small_grads")[0]
    pack_all = pack_all.reshape(N_DEV, pack.shape[0], LANES)
    tot = _sum_parts(pack_all, name="sum_small_grads")
    (dmod_sum, dmc_sum, g_nmix, g_qn, g_kn, g_sink, g_cb, g_lw, g_lb, g_nffn, g_fb, g_cw_f, g_fw_f,
     loss_sum) = _unpack(tot, [dmod, dmod_ctx] + summed_like)
    loss = loss_sum[0, 0]
    dmod_all = pack_all.reshape(N_DEV, -1)[:, :6 * d]
    g_b_mod = dmod_sum.at[:, :2 * d].add(dmc_sum)

    lo = me * mcols
    dm_rows = jnp.zeros((16, 6 * d), F32).at[:N_DEV].set(dmod_all).at[N_DEV, :2 * d].set(dmc_sum[0])
    dm_mine = lax.dynamic_slice_in_dim(dm_rows, lo, mcols, axis=1)
    parts_mod = _mm(act, dm_mine, d, mcols, 16, ta=True, name="mod_dw")[None]
    dact_part = _mm(dm_mine[N_DEV:N_DEV + 8], w_mod[0], 8, d, mcols, tb=True, name="mod_dx_ctx")
    dact_all = _all_gather([dact_part], hbm=False, name="gather_c_ctx_grad")[0].reshape(N_DEV, 8, d)
    dact = _sum_parts(dact_all, name="sum_c_ctx_grad")[0]
    sg = jax.nn.sigmoid(c_ctx)
    g_c_ctx = dact * (sg * (1.0 + c_ctx * (1.0 - sg)))

    def stacked(rx):
        return rx.reshape(N_DEV, rx.shape[0] // N_DEV, rx.shape[1])

    g_w_in = _sum_parts(stacked(rx_in), name="sum_w_in").T[None]
    g_w_up = jnp.concatenate([_sum_parts(stacked(rx_up_lo), name="sum_w_up_lo"),
                              _sum_parts(stacked(rx_up_hi), name="sum_w_up_hi")], axis=1).T[None]
    big = {}
    big["w_in"] = _adam(w_in[0], m_w_in[0], v_w_in[0], g_w_in, name="adam_w_in")
    big["w_up"] = _adam(w_up[0], m_w_up[0], v_w_up[0], g_w_up, name="adam_w_up")
    big["w_out"] = _adam(w_out[0], m_w_out[0], v_w_out[0], stacked(rx_out), name="adam_w_out")
    big["w_down"] = _adam(w_down[0], m_w_down[0], v_w_down[0], stacked(rx_down), name="adam_w_down")
    big["w_mod"] = _adam(w_mod[0], m_w_mod[0], v_w_mod[0], parts_mod, name="adam_w_mod")

    ccols, fcols = conv_w.shape[2], ffn_conv_w.shape[2]
    g_conv_w = lax.dynamic_slice_in_dim(g_cw_f, me * ccols, ccols, axis=1)[None]
    g_ffn_w = lax.dynamic_slice_in_dim(g_fw_f, me * fcols, fcols, axis=1)[None]
    names = ["c_ctx", "b_mod", "norm_mix_w", "q_norm_w", "k_norm_w", "sink_logit", "conv_w", "conv_b",
             "conv_norm_w", "conv_norm_b", "norm_ffn_w", "ffn_conv_w", "ffn_conv_b"]
    ws = [c_ctx, b_mod, norm_mix_w, q_norm_w, k_norm_w, sink_logit, conv_w, conv_b, conv_norm_w, conv_norm_b,
          norm_ffn_w, ffn_conv_w, ffn_conv_b]
    msm = [m_c_ctx, m_b_mod, m_norm_mix_w, m_q_norm_w, m_k_norm_w, m_sink_logit, m_conv_w, m_conv_b,
           m_conv_norm_w, m_conv_norm_b, m_norm_ffn_w, m_ffn_conv_w, m_ffn_conv_b]
    vsm = [v_c_ctx, v_b_mod, v_norm_mix_w, v_q_norm_w, v_k_norm_w, v_sink_logit, v_conv_w, v_conv_b,
           v_conv_norm_w, v_conv_norm_b, v_norm_ffn_w, v_ffn_conv_w, v_ffn_conv_b]
    gsm = [g_c_ctx, g_b_mod, g_nmix, g_qn, g_kn, g_sink, g_conv_w, g_cb, g_lw, g_lb, g_nffn, g_ffn_w, g_fb]
    deltas, new_ms, new_vs = _adam_many(ws, msm, vsm, gsm, name="adam_small")
    sm = {nm: vals for nm, vals in zip(names, zip(gsm, deltas, new_ms, new_vs))}

    def out4(nm):
        if nm in sm:
            return sm[nm]
        return tuple(t[None] for t in big[nm])

    order = ["c_ctx", "w_mod", "b_mod", "norm_mix_w", "w_in", "q_norm_w", "k_norm_w", "sink_logit", "conv_w",
             "conv_b", "conv_norm_w", "conv_norm_b", "w_out", "norm_ffn_w", "w_up", "ffn_conv_w", "ffn_conv_b",
             "w_down"]
    quads = [out4(nm) for nm in order]
    return (loss, grad_x[None], *[q[0] for q in quads], *[q[1] for q in quads],
            *[q[2] for q in quads], *[q[3] for q in quads])
```

```python
import math
from typing import Callable, NamedTuple

import jax
import jax.numpy as jnp
from jax import lax
from jax.experimental import pallas as pl
from jax.experimental.pallas import tpu as pltpu

F32 = jnp.float32
MXU_DT = jnp.bfloat16

D_MODEL = 1024
GRID_W = 64
HEAD_DIM = 64
N_Q_HEADS = 8
N_KV_HEADS = 2
GQA_GROUP = 4
WINDOW = 128
BLOCK = 128
Q_COLS = 512
KV_COLS = 128
QK_COLS = Q_COLS + KV_COLS
QKV_COLS = Q_COLS + 2 * KV_COLS
CONV_CH = 512
GLU_COLS = 2 * CONV_CH
IN_COLS = QKV_COLS + GLU_COLS
CONV_K = 31
CONV_PAD = 15
FFN_H = 2816
FFN_K = 3
ROPE_BASE = 10000.0
EPS = 1e-6
NEG_INF = -1e30
N_DEV = 8
HALO = 16
LANES = 128
ROW_TS = 512

ADAM_LR = 0.001
ADAM_B1 = 0.9
ADAM_B2 = 0.999
ADAM_EPS = 1e-08
ADAM_WD = 0.01
ADAM_STEP = 10

MESH_AXES = ("x", "y", "c")
MESH = pl.DeviceIdType.MESH
VMEM_LIMIT = 56 << 20
MM_VMEM_BUDGET = 44 << 20
GLU_OFF = 1024
IN_PAD = GLU_OFF + GLU_COLS


def _cp(*dims):
    return pltpu.CompilerParams(dimension_semantics=dims or None, vmem_limit_bytes=VMEM_LIMIT)


def _row(ts, w, col=0):
    return pl.BlockSpec((ts, w), lambda i: (i, col))


def _vec(w):
    return pl.BlockSpec((1, w), lambda i: (0, 0))


def _colsum(v):
    return jnp.sum(v, axis=0, keepdims=True)


def _sigmoid(v):
    return 0.5 * jnp.tanh(0.5 * v) + 0.5


def _mm(a, b, m, n, k, *, ta=False, tb=False, n0=0, k0=0, add=None, out_dtype=F32, into=None, m0=0,
        out_rows=None, a2=None, carry=None, name):
    has_add, has_a2 = add is not None, a2 is not None
    assert not (has_a2 and (ta or tb))
    if into is not None:
        out_dtype = into.dtype
    sa, sb, so = a.dtype.itemsize, b.dtype.itemsize, jnp.dtype(out_dtype).itemsize
    sadd = add.dtype.itemsize if has_add else 0
    na = 2 if has_a2 else 1

    def fits(tm, tn):
        return 2 * (na * k * (tm * sa + tn * sb) + tm * tn * (so + sadd)) <= MM_VMEM_BUDGET

    tms = [m] if m <= 1024 else [t for t in (1024, 1408, 768, 512, 256, 128) if m % t == 0]
    tns = [t for t in ((1024, 512, 256, 128) if ta else (1408, 512, 256, 128)) if n % t == 0 and n0 % t == 0]
    tm, tn = next((tm, tn) for tm in tms for tn in tns if fits(tm, tn))
    a_spec = (pl.BlockSpec((k, tm), lambda i, j: (0, i)) if ta else pl.BlockSpec((tm, k), lambda i, j: (i, 0)))
    nb0 = n0 // tn
    if tb:
        assert k0 == 0
        b_spec = pl.BlockSpec((tn, k), lambda i, j: (j + nb0, 0))
    else:
        assert k0 % (na * k) == 0, (k0, k)
        kb0 = k0 // (na * k)
        b_spec = pl.BlockSpec((na * k, tn), lambda i, j: (kb0, j + nb0))
    assert m0 % tm == 0, (m0, tm)
    mb0 = m0 // tm
    o_spec = pl.BlockSpec((tm, tn), lambda i, j: (i + mb0, j))
    dims = (((0 if ta else 1,), (1 if tb else 0,)), ((), ()))

    def body(*refs):
        a_ref, b_ref, o_ref = refs[0], refs[na], refs[-1]
        if has_a2:
            res = (jnp.dot(a_ref[...].astype(MXU_DT), b_ref[0:k, :].astype(MXU_DT), preferred_element_type=F32)
                   + jnp.dot(refs[1][...].astype(MXU_DT), b_ref[k:2 * k, :].astype(MXU_DT),
                             preferred_element_type=F32))
        else:
            res = lax.dot_general(a_ref[...].astype(MXU_DT), b_ref[...].astype(MXU_DT), dims,
                                  preferred_element_type=F32)
        if has_add:
            res = res + refs[na + 1][...].astype(F32)
        o_ref[...] = res.astype(o_ref.dtype)

    ins = [a] + ([a2] if has_a2 else []) + [b] + ([add] if has_add else []) + ([into] if into is not None else [])
    specs = ([a_spec] * na + [b_spec] + ([pl.BlockSpec((tm, tn), lambda i, j: (i, j))] if has_add else [])
             + ([pl.BlockSpec(memory_space=pl.ANY)] if into is not None else []))
    out_shape = (jax.ShapeDtypeStruct(into.shape, into.dtype) if into is not None
                 else jax.ShapeDtypeStruct((out_rows or m, n), out_dtype))
    (out,), carried = _grid_call(
        body, ins, carry, name=name, grid=(m // tm, n // tn), in_specs=specs, out_specs=[o_spec],
        out_shape=[out_shape], input_output_aliases={len(ins) - 1: 0} if into is not None else None,
        dims=("parallel", "parallel"))
    return out if carry is None else (out, carried)


def _rms_stats(xv):
    r = lax.rsqrt(jnp.mean(xv * xv, axis=-1, keepdims=True) + EPS)
    return r, xv * r


def _prenorm(x, nw, sc, sh, *, carry=None, name):
    rows, d = x.shape
    ts = min(rows, ROW_TS)

    def body(x_ref, nw_ref, sc_ref, sh_ref, h_ref):
        _, xn = _rms_stats(x_ref[...])
        h_ref[...] = ((xn * nw_ref[...]) * (1.0 + sc_ref[...]) + sh_ref[...]).astype(h_ref.dtype)

    (h,), carried = _grid_call(
        body, [x, nw, sc, sh], carry, name=name, grid=(rows // ts,),
        in_specs=[_row(ts, d), _vec(d), _vec(d), _vec(d)], out_specs=[_row(ts, d)],
        out_shape=[jax.ShapeDtypeStruct((rows, d), MXU_DT)], dims=("parallel",))
    return h, carried


def _resid_prenorm(x, mix, g1, nw, sc, sh, *, name):
    rows, d = x.shape
    ts = min(rows, ROW_TS)

    def body(x_ref, mix_ref, g_ref, nw_ref, sc_ref, sh_ref, x1_ref, h_ref):
        x1 = x_ref[...] + g_ref[...] * mix_ref[...]
        x1_ref[...] = x1
        _, xn = _rms_stats(x1)
        h_ref[...] = ((xn * nw_ref[...]) * (1.0 + sc_ref[...]) + sh_ref[...]).astype(h_ref.dtype)

    return pl.pallas_call(
        body, name=name, grid=(rows // ts,),
        in_specs=[_row(ts, d), _row(ts, d), _vec(d), _vec(d), _vec(d), _vec(d)],
        out_specs=[_row(ts, d), _row(ts, d)],
        out_shape=[jax.ShapeDtypeStruct((rows, d), F32), jax.ShapeDtypeStruct((rows, d), MXU_DT)],
        compiler_params=_cp("parallel"),
    )(x, mix, g1, nw, sc, sh)


def _loss_head(x1, ffn, g2, target, *, name):
    rows, d = x1.shape
    ts = min(rows, ROW_TS)

    def body(x1_ref, f_ref, g_ref, t_ref, loss_ref, dy_ref, dffn_ref, dg_ref):
        i = pl.program_id(0)
        f = f_ref[...]
        e = x1_ref[...] + g_ref[...] * f - t_ref[...]
        per_tok = jnp.mean(e * e, axis=-1, keepdims=True)
        part = 0.5 * jnp.sum(per_tok, axis=0, keepdims=True)
        dy = e * (1.0 / d)
        dy_ref[...] = dy
        dffn_ref[...] = (dy * g_ref[...]).astype(dffn_ref.dtype)

        @pl.when(i == 0)
        def _():
            loss_ref[...] = jnp.zeros_like(loss_ref)
            dg_ref[...] = jnp.zeros_like(dg_ref)

        loss_ref[...] += jnp.broadcast_to(part, loss_ref.shape)
        dg_ref[...] += _colsum(dy * f)

    return pl.pallas_call(
        body, name=name, grid=(rows // ts,),
        in_specs=[_row(ts, d), _row(ts, d), _vec(d), _row(ts, d)],
        out_specs=[pl.BlockSpec((8, LANES), lambda i: (0, 0)), _row(ts, d), _row(ts, d), _vec(d)],
        out_shape=[jax.ShapeDtypeStruct((8, LANES), F32), jax.ShapeDtypeStruct((rows, d), F32),
                   jax.ShapeDtypeStruct((rows, d), MXU_DT), jax.ShapeDtypeStruct((1, d), F32)],
        compiler_params=_cp("arbitrary"),
    )(x1, ffn, g2, target)


def _norm_bwd(dh, xin, nw, sc, *, res=None, gate=None, want_dx=True, carry=None, name):
    rows, d = xin.shape
    ts = min(rows, ROW_TS)
    has_res, has_gate = res is not None, gate is not None

    def body(*refs):
        it = iter(refs)
        dh_ref, x_ref, nw_ref, sc_ref = next(it), next(it), next(it), next(it)
        res_ref = next(it) if has_res else None
        gated_ref, g_ref = (next(it), next(it)) if has_gate else (None, None)
        dx_ref = next(it) if want_dx else None
        dgx_ref = next(it) if has_gate else None
        dsh_ref, dsc_ref, dnw_ref = next(it), next(it), next(it)
        dg_ref = next(it) if has_gate else None
        i = pl.program_id(0)
        dhv = dh_ref[...]
        r, xn = _rms_stats(x_ref[...])
        dn = dhv * (1.0 + sc_ref[...])

        @pl.when(i == 0)
        def _():
            dsh_ref[...] = jnp.zeros_like(dsh_ref)
            dsc_ref[...] = jnp.zeros_like(dsc_ref)
            dnw_ref[...] = jnp.zeros_like(dnw_ref)
            if has_gate:
                dg_ref[...] = jnp.zeros_like(dg_ref)

        dsh_ref[...] += _colsum(dhv)
        dsc_ref[...] += _colsum(dhv * (xn * nw_ref[...]))
        dnw_ref[...] += _colsum(dn * xn)
        if want_dx:
            dxn = dn * nw_ref[...]
            dx = r * (dxn - xn * jnp.mean(dxn * xn, axis=-1, keepdims=True))
            if has_res:
                dx = dx + res_ref[...]
            dx_ref[...] = dx
            if has_gate:
                dgx_ref[...] = (dx * g_ref[...]).astype(dgx_ref.dtype)
                dg_ref[...] += _colsum(dx * gated_ref[...])

    ins = [dh, xin, nw, sc] + ([res] if has_res else []) + (list(gate) if has_gate else [])
    in_specs = ([_row(ts, d), _row(ts, d), _vec(d), _vec(d)] + ([_row(ts, d)] if has_res else [])
                + ([_row(ts, d), _vec(d)] if has_gate else []))
    out_specs, out_shape = [], []
    if want_dx:
        out_specs.append(_row(ts, d)); out_shape.append(jax.ShapeDtypeStruct((rows, d), F32))
    if has_gate:
        out_specs.append(_row(ts, d)); out_shape.append(jax.ShapeDtypeStruct((rows, d), MXU_DT))
    for _ in range(3 + int(has_gate)):
        out_specs.append(_vec(d)); out_shape.append(jax.ShapeDtypeStruct((1, d), F32))
    return _grid_call(body, ins, carry, name=name, grid=(rows // ts,), in_specs=in_specs, out_specs=out_specs,
                      out_shape=out_shape, dims=("arbitrary",))


def _group_sum(v, g):
    hi = v.astype(MXU_DT)
    lo = (v - hi.astype(F32)).astype(MXU_DT)
    return (jnp.dot(hi, g, preferred_element_type=F32) + jnp.dot(lo, g, preferred_element_type=F32))


def _rot(v):
    lane = lax.broadcasted_iota(jnp.int32, v.shape, 1)
    first = (lane & 31) < 16
    return jnp.where(first, -pltpu.roll(v, LANES - 16, 1), pltpu.roll(v, 16, 1))


def _head_group_matrix():
    r = jnp.arange(LANES) // HEAD_DIM
    return (r[:, None] == r[None, :]).astype(MXU_DT)


def _qk_prep(xin, width, cos, sin, w, *, name):
    rows = xin.shape[0]
    ts = min(rows, 256)
    nch = width // LANES

    def body(x_ref, cos_ref, sin_ref, w_ref, g_ref, o_ref):
        cs, sn, g = cos_ref[...], sin_ref[...], g_ref[...]
        for ch in range(nch):
            sl = slice(ch * LANES, (ch + 1) * LANES)
            xv = x_ref[:, sl]
            r = lax.rsqrt(_group_sum(xv * xv, g) * (1.0 / HEAD_DIM) + EPS)
            yw = (xv * r) * w_ref[:, sl]
            o_ref[:, sl] = (yw * cs + _rot(yw) * sn).astype(o_ref.dtype)

    return pl.pallas_call(
        body, name=name, grid=(rows // ts,),
        in_specs=[_row(ts, width), _row(ts, LANES), _row(ts, LANES), _vec(width),
                  pl.BlockSpec((LANES, LANES), lambda i: (0, 0))],
        out_specs=_row(ts, width),
        out_shape=jax.ShapeDtypeStruct((rows, width), MXU_DT), compiler_params=_cp("parallel"),
    )(xin, cos, sin, w, _head_group_matrix())


def _qk_prep_bwd(xin, douts, width, cos, sin, w, *, tail=None, name):
    rows = xin.shape[0]
    ts = min(rows, 256)
    nch = width // LANES
    has_tail = tail is not None
    nd = len(douts)
    assert sum(v.shape[1] for v in douts) == width
    src = [(k, c) for k, v in enumerate(douts) for c in range(v.shape[1] // LANES)]

    def body(*refs):
        x_ref, d_refs = refs[0], refs[1:1 + nd]
        t_ref = refs[1 + nd] if has_tail else None
        cos_ref, sin_ref, w_ref, g_ref, dx_ref, dw_ref = refs[1 + nd + int(has_tail):]
        i = pl.program_id(0)
        cs, sn, g = cos_ref[...], sin_ref[...], g_ref[...]

        @pl.when(i == 0)
        def _():
            dw_ref[...] = jnp.zeros_like(dw_ref)

        if has_tail:
            dx_ref[:, width:width + LANES] = t_ref[...].astype(dx_ref.dtype)
            dx_ref[:, width + LANES:] = jnp.zeros((ts, GLU_OFF - width - LANES), dx_ref.dtype)

        for ch in range(nch):
            sl = slice(ch * LANES, (ch + 1) * LANES)
            xv = x_ref[:, sl]
            dv = d_refs[src[ch][0]][:, src[ch][1] * LANES:(src[ch][1] + 1) * LANES].astype(F32)
            r = lax.rsqrt(_group_sum(xv * xv, g) * (1.0 / HEAD_DIM) + EPS)
            n = xv * r
            dyw = dv * cs - _rot(dv * sn)
            dw_ref[:, sl] += _colsum(dyw * n)
            dn = dyw * w_ref[:, sl]
            gm = _group_sum(dn * n, g) * (1.0 / HEAD_DIM)
            dx_ref[:, sl] = (r * (dn - n * gm)).astype(dx_ref.dtype)

    ins = [xin] + list(douts) + ([tail] if has_tail else []) + [cos, sin, w, _head_group_matrix()]
    in_specs = ([_row(ts, width)] + [_row(ts, v.shape[1]) for v in douts] + ([_row(ts, LANES)] if has_tail else [])
                + [_row(ts, LANES), _row(ts, LANES), _vec(width), pl.BlockSpec((LANES, LANES), lambda i: (0, 0))])
    out_w, arr_w = (GLU_OFF, IN_PAD) if has_tail else (width, width)
    return pl.pallas_call(
        body, name=name, grid=(rows // ts,), in_specs=in_specs,
        out_specs=[_row(ts, out_w), _vec(width)],
        out_shape=[jax.ShapeDtypeStruct((rows, arr_w), MXU_DT), jax.ShapeDtypeStruct((1, width), F32)],
        compiler_params=_cp("arbitrary"),
    )(*ins)


GB = GQA_GROUP * BLOCK
WIN = 3 * BLOCK
ATT_STEP = 4


def _win_start(i, seq):
    return pl.multiple_of(jnp.clip((i - 1) * BLOCK, 0, seq - WIN), BLOCK)


def _attn_mask(i, start):
    qpos = i * BLOCK + (lax.broadcasted_iota(jnp.int32, (GB, WIN), 0) & (BLOCK - 1))
    kpos = start + lax.broadcasted_iota(jnp.int32, (GB, WIN), 1)
    return jnp.abs(qpos - kpos) <= WINDOW


def _sink_col(sink_ref, kv):
    return jnp.concatenate(
        [jnp.full((BLOCK, 1), sink_ref[0, kv * GQA_GROUP + g], F32) for g in range(GQA_GROUP)], axis=0)


def _stack_slots(ref, rows):
    return jnp.concatenate([ref[rows, g * LANES:(g + 1) * LANES] for g in range(GQA_GROUP)], axis=0)


def _kv_lanes(kv):
    lane = lax.broadcasted_iota(jnp.int32, (1, LANES), 1)
    return (lane < HEAD_DIM) if kv == 0 else (lane >= HEAD_DIM)


def _keep(mask, v):
    return jnp.where(mask, v, jnp.zeros_like(v))


_NT = (((1,), (1,)), ((), ()))
_TN = (((0,), (0,)), ((), ()))


def _attn_specs(seq, n_ctx):
    qs = pl.BlockSpec((ATT_STEP * BLOCK, Q_COLS), lambda i: (i, 0))
    ks = pl.BlockSpec((seq, KV_COLS), lambda i: (0, Q_COLS // KV_COLS))
    vs = pl.BlockSpec((seq, KV_COLS), lambda i: (0, QK_COLS // KV_COLS))
    kcs = pl.BlockSpec((n_ctx, KV_COLS), lambda i: (0, 0))
    vcs = pl.BlockSpec((n_ctx, KV_COLS), lambda i: (0, 1))
    ls = pl.BlockSpec((ATT_STEP, N_KV_HEADS * GB, 1), lambda i: (i, 0, 0))
    return qs, ks, vs, kcs, vcs, ls


def _attn_fwd(sink, qk, proj, kc, kv_ctx, *, carry=None, name):
    seq, n_ctx = qk.shape[0], kc.shape[0]
    scale = 1.0 / math.sqrt(HEAD_DIM)

    def one_block(blk, sub, sink_ref, q_ref, k_ref, v_ref, kcw, vcw, o_ref, lse_ref):
        rows = slice(sub * BLOCK, (sub + 1) * BLOCK)
        start = _win_start(blk, seq)
        valid = _attn_mask(blk, start)
        qs = _stack_slots(q_ref, rows)
        kw, vw = k_ref[pl.ds(start, WIN), :], v_ref[pl.ds(start, WIN), :].astype(MXU_DT)
        o_all = jnp.zeros((GB, LANES), F32)
        for kv in range(N_KV_HEADS):
            mine = _kv_lanes(kv)
            s_loc = lax.dot_general(qs, _keep(mine, kw), _NT, preferred_element_type=F32) * scale
            s_loc = jnp.where(valid, s_loc, NEG_INF)
            s_ctx = lax.dot_general(qs, _keep(mine, kcw), _NT, preferred_element_type=F32) * scale
            sk = _sink_col(sink_ref, kv)
            m = jnp.maximum(jnp.maximum(jnp.max(s_loc, axis=-1, keepdims=True),
                                        jnp.max(s_ctx, axis=-1, keepdims=True)), sk)
            p_loc = jnp.exp(s_loc - m)
            p_ctx = jnp.exp(s_ctx - m)
            l = (jnp.sum(p_loc, axis=-1, keepdims=True) + jnp.sum(p_ctx, axis=-1, keepdims=True)
                 + jnp.exp(sk - m))
            o_all = o_all + (jnp.dot(p_loc.astype(MXU_DT), _keep(mine, vw), preferred_element_type=F32)
                             + jnp.dot(p_ctx.astype(MXU_DT), _keep(mine, vcw), preferred_element_type=F32)) / l
            lse_ref[sub, kv * GB:(kv + 1) * GB, :] = m + jnp.log(l)
        for g in range(GQA_GROUP):
            o_ref[rows, g * LANES:(g + 1) * LANES] = o_all[g * BLOCK:(g + 1) * BLOCK].astype(o_ref.dtype)

    def body(sink_ref, q_ref, k_ref, v_ref, kc_ref, vc_ref, o_ref, lse_ref):
        i = pl.program_id(0)
        kcw, vcw = kc_ref[...], vc_ref[...].astype(MXU_DT)
        for sub in range(ATT_STEP):
            one_block(i * ATT_STEP + sub, sub, sink_ref, q_ref, k_ref, v_ref, kcw, vcw, o_ref, lse_ref)

    qs, ks, vs, kcs, vcs, ls = _attn_specs(seq, n_ctx)
    return _grid_call(
        body, [sink, qk, qk, proj, kc, kv_ctx], carry, name=name, grid=(seq // (ATT_STEP * BLOCK),),
        in_specs=[pl.BlockSpec(memory_space=pltpu.SMEM), qs, ks, vs, kcs, vcs],
        out_specs=[qs, ls],
        out_shape=[jax.ShapeDtypeStruct((seq, Q_COLS), MXU_DT),
                   jax.ShapeDtypeStruct((seq // BLOCK, N_KV_HEADS * GB, 1), F32)],
        dims=("parallel",))


def _attn_bwd(sink, qk, proj, kc, kv_ctx, o, lse, do, *, carry=None, name):
    seq, n_ctx = qk.shape[0], kc.shape[0]
    scale = 1.0 / math.sqrt(HEAD_DIM)

    def body(sink_ref, q_ref, k_ref, v_ref, kc_ref, vc_ref, o_ref, lse_ref, do_ref,
             dq_ref, dk_ref, dv_ref, dkc_ref, dvc_ref, ds_ref):
        i = pl.program_id(0)

        @pl.when(i == 0)
        def _():
            dk_ref[...] = jnp.zeros_like(dk_ref)
            dv_ref[...] = jnp.zeros_like(dv_ref)
            dkc_ref[...] = jnp.zeros_like(dkc_ref)
            dvc_ref[...] = jnp.zeros_like(dvc_ref)
            ds_ref[...] = jnp.zeros_like(ds_ref)

        kcw, vcw = kc_ref[...], vc_ref[...].astype(MXU_DT)
        dkc, dvc = jnp.zeros((n_ctx, LANES), F32), jnp.zeros((n_ctx, LANES), F32)
        for sub in range(ATT_STEP):
            dkc_s, dvc_s = one_block(i * ATT_STEP + sub, sub, sink_ref, q_ref, k_ref, v_ref, kcw, vcw, o_ref,
                                     lse_ref, do_ref, dq_ref, dk_ref, dv_ref, ds_ref)
            dkc, dvc = dkc + dkc_s, dvc + dvc_s
        dkc_ref[...] += dkc
        dvc_ref[...] += dvc

    def one_block(blk, sub, sink_ref, q_ref, k_ref, v_ref, kcw, vcw, o_ref, lse_ref, do_ref,
                  dq_ref, dk_ref, dv_ref, ds_ref):
        qrows = slice(sub * BLOCK, (sub + 1) * BLOCK)
        start = _win_start(blk, seq)
        valid = _attn_mask(blk, start)
        win = pl.ds(start, WIN)
        qs, dos = _stack_slots(q_ref, qrows), _stack_slots(do_ref, qrows)
        do_o = dos.astype(F32) * _stack_slots(o_ref, qrows).astype(F32)
        kw, vw = k_ref[win, :], v_ref[win, :].astype(MXU_DT)
        dq = jnp.zeros((GB, LANES), F32)
        dk, dv = jnp.zeros((WIN, LANES), F32), jnp.zeros((WIN, LANES), F32)
        dkc, dvc = jnp.zeros((n_ctx, LANES), F32), jnp.zeros((n_ctx, LANES), F32)
        for kv in range(N_KV_HEADS):
            mine = _kv_lanes(kv)
            rows = slice(kv * GB, (kv + 1) * GB)
            lse_s = lse_ref[sub, rows, :]
            delta = jnp.sum(_keep(mine, do_o), axis=-1, keepdims=True)
            kz, vz, kcz, vcz = _keep(mine, kw), _keep(mine, vw), _keep(mine, kcw), _keep(mine, vcw)
            s_loc = lax.dot_general(qs, kz, _NT, preferred_element_type=F32) * scale
            s_loc = jnp.where(valid, s_loc, NEG_INF)
            s_ctx = lax.dot_general(qs, kcz, _NT, preferred_element_type=F32) * scale
            p_loc = jnp.exp(s_loc - lse_s)
            p_ctx = jnp.exp(s_ctx - lse_s)
            p_sink = jnp.exp(_sink_col(sink_ref, kv) - lse_s)
            dp_loc = lax.dot_general(dos, vz, _NT, preferred_element_type=F32)
            dp_ctx = lax.dot_general(dos, vcz, _NT, preferred_element_type=F32)
            ds_loc = (p_loc * (dp_loc - delta) * scale).astype(MXU_DT)
            ds_ctx = (p_ctx * (dp_ctx - delta) * scale).astype(MXU_DT)
            dq = dq + (jnp.dot(ds_loc, kz, preferred_element_type=F32)
                       + jnp.dot(ds_ctx, kcz, preferred_element_type=F32))
            dk = dk + _keep(mine, lax.dot_general(ds_loc, qs, _TN, preferred_element_type=F32))
            dv = dv + _keep(mine, lax.dot_general(p_loc.astype(MXU_DT), dos, _TN, preferred_element_type=F32))
            dkc = dkc + _keep(mine, lax.dot_general(ds_ctx, qs, _TN, preferred_element_type=F32))
            dvc = dvc + _keep(mine, lax.dot_general(p_ctx.astype(MXU_DT), dos, _TN, preferred_element_type=F32))
            ds_ref[rows, :] += -(p_sink * delta)
        for g in range(GQA_GROUP):
            dq_ref[qrows, g * LANES:(g + 1) * LANES] = dq[g * BLOCK:(g + 1) * BLOCK]
        dk_ref[win, :] += dk
        dv_ref[win, :] += dv
        return dkc, dvc

    qs, ks, vs, kcs, vcs, ls = _attn_specs(seq, n_ctx)
    whole = lambda r, c: pl.BlockSpec((r, c), lambda i: (0, 0))
    return _grid_call(
        body, [sink, qk, qk, proj, kc, kv_ctx, o, lse, do], carry, name=name, grid=(seq // (ATT_STEP * BLOCK),),
        in_specs=[pl.BlockSpec(memory_space=pltpu.SMEM), qs, ks, vs, kcs, vcs, qs, ls, qs],
        out_specs=[qs, whole(seq, KV_COLS), whole(seq, KV_COLS), whole(n_ctx, KV_COLS), whole(n_ctx, KV_COLS),
                   whole(N_KV_HEADS * GB, 1)],
        out_shape=[jax.ShapeDtypeStruct((seq, Q_COLS), F32), jax.ShapeDtypeStruct((seq, KV_COLS), F32),
                   jax.ShapeDtypeStruct((seq, KV_COLS), F32), jax.ShapeDtypeStruct((n_ctx, KV_COLS), F32),
                   jax.ShapeDtypeStruct((n_ctx, KV_COLS), F32), jax.ShapeDtypeStruct((N_KV_HEADS * GB, 1), F32)],
        dims=("arbitrary",))


def _halo_specs(ts, w, rows, col=0):
    per = ts // HALO
    last = rows // HALO - 1
    return [pl.BlockSpec((HALO, w), lambda i: (jnp.maximum(i * per - 1, 0), col)),
            pl.BlockSpec((ts, w), lambda i: (i, col)),
            pl.BlockSpec((HALO, w), lambda i: (jnp.minimum((i + 1) * per, last), col))]


def _glu(v):
    return v[:, :CONV_CH] * _sigmoid(v[:, CONV_CH:])


def _ln_stats(u):
    mu = jnp.mean(u, axis=-1, keepdims=True)
    xc = u - mu
    rstd = lax.rsqrt(jnp.mean(xc * xc, axis=-1, keepdims=True) + EPS)
    return xc * rstd, rstd


def _phases(ext_ref, ph_ref):
    n = ph_ref.shape[1]
    for b in range(1, 8):
        ph_ref[b - 1] = ext_ref[b:b + n, :]


def _window(ext_ref, ph_ref, o, n, cs):
    a, b = divmod(o, 8)
    src = ext_ref if b == 0 else ph_ref.at[b - 1]
    return src[8 * a:8 * a + n, cs]


def _conv_fwd(glu, col, cw, cb, lw, lb, *, carry=None, name):
    rows = glu.shape[0]
    ts = min(rows, 256)
    nt = rows // ts

    te = ts + 2 * HALO

    def body(gp_ref, g_ref, gn_ref, cw_ref, cb_ref, lw_ref, lb_ref, u3_ref, u1_ref, ext_ref, ph_ref):
        i = pl.program_id(0)
        ext_ref[0:HALO, :] = jnp.where(i > 0, _glu(gp_ref[...]), 0.0)
        ext_ref[HALO:HALO + ts, :] = _glu(g_ref[...])
        ext_ref[HALO + ts:, :] = jnp.where(i < nt - 1, _glu(gn_ref[...]), 0.0)
        _phases(ext_ref, ph_ref)
        for c in range(CONV_CH // LANES):
            cs = slice(c * LANES, (c + 1) * LANES)
            acc = jnp.broadcast_to(cb_ref[:, cs], (ts, LANES))
            for j in range(CONV_K):
                acc = acc + cw_ref[j:j + 1, cs] * _window(ext_ref, ph_ref, HALO - CONV_PAD + j, ts, cs)
            u1_ref[:, cs] = acc
        xh, _ = _ln_stats(u1_ref[...])
        u2 = xh * lw_ref[...] + lb_ref[...]
        u3_ref[...] = (u2 * _sigmoid(u2)).astype(u3_ref.dtype)

    full = lambda shape: pl.BlockSpec(shape, lambda i: (0,) * len(shape))
    return _grid_call(
        body, [glu, glu, glu, cw, cb, lw, lb], carry, name=name, grid=(nt,),
        in_specs=_halo_specs(ts, GLU_COLS, rows, col) + [full((CONV_K, CONV_CH))] + [_vec(CONV_CH)] * 3,
        out_specs=[_row(ts, CONV_CH), _row(ts, CONV_CH)],
        out_shape=[jax.ShapeDtypeStruct((rows, CONV_CH), MXU_DT), jax.ShapeDtypeStruct((rows, CONV_CH), F32)],
        scratch_shapes=[pltpu.VMEM((te, CONV_CH), F32), pltpu.VMEM((7, te - 8, CONV_CH), F32)],
        dims=("parallel",))


def _conv_bwd(glu, col, u1, du3, dproj, cw, lw, lb, *, carry=None, name):
    rows = glu.shape[0]
    ts = min(rows, 256)
    nt = rows // ts
    te = ts + 2 * HALO

    def du1_of(u1v, du3v, lw_v, lb_v):
        xh, rstd = _ln_stats(u1v)
        u2 = xh * lw_v + lb_v
        sg = _sigmoid(u2)
        du2 = du3v * (sg * (1.0 + u2 * (1.0 - sg)))
        dxh = du2 * lw_v
        du1 = rstd * (dxh - jnp.mean(dxh, axis=-1, keepdims=True)
                      - xh * jnp.mean(dxh * xh, axis=-1, keepdims=True))
        return du1, du2, xh

    half = ts // 2

    def body(gp_ref, g_ref, gn_ref, up_ref, u_ref, un_ref, dp_ref, d_ref, dn_ref, cw_ref, lw_ref, lb_ref,
             _, dglu_ref, dcw_ref, dvec_ref, u0_ref, du1_ref, pu_ref, pd_ref, du0_ref):
        i = pl.program_id(0)
        lw_v, lb_v = lw_ref[...], lb_ref[...]

        @pl.when(i == 0)
        def _():
            dcw_ref[...] = jnp.zeros_like(dcw_ref)
            dvec_ref[...] = jnp.zeros_like(dvec_ref)

        gv = g_ref[...]
        u0_ref[0:HALO, :] = jnp.where(i > 0, _glu(gp_ref[...]), 0.0)
        u0_ref[HALO:HALO + ts, :] = _glu(gv)
        u0_ref[HALO + ts:, :] = jnp.where(i < nt - 1, _glu(gn_ref[...]), 0.0)
        d_prev, _, _ = du1_of(up_ref[...], dp_ref[...], lw_v, lb_v)
        d_main, du2, xh = du1_of(u_ref[...], d_ref[...], lw_v, lb_v)
        d_next, _, _ = du1_of(un_ref[...], dn_ref[...], lw_v, lb_v)
        du1_ref[0:HALO, :] = jnp.where(i > 0, d_prev, 0.0)
        du1_ref[HALO:HALO + ts, :] = d_main
        du1_ref[HALO + ts:, :] = jnp.where(i < nt - 1, d_next, 0.0)

        rid = lax.broadcasted_iota(jnp.int32, (8, CONV_CH), 0)
        dvec_ref[...] += (jnp.where(rid == 0, _colsum(d_main), 0.0)
                          + jnp.where(rid == 1, _colsum(du2 * xh), 0.0)
                          + jnp.where(rid == 2, _colsum(du2), 0.0))
        _phases(u0_ref, pu_ref)
        _phases(du1_ref, pd_ref)
        for c in range(CONV_CH // LANES):
            cs = slice(c * LANES, (c + 1) * LANES)
            for r0 in (0, half):
                dm = du1_ref[HALO + r0:HALO + r0 + half, cs]
                acc = jnp.zeros((half, LANES), F32)
                for j in range(CONV_K):
                    acc = acc + cw_ref[j:j + 1, cs] * _window(du1_ref, pd_ref, r0 + HALO + CONV_PAD - j, half, cs)
                    prod = dm * _window(u0_ref, pu_ref, r0 + HALO - CONV_PAD + j, half, cs)
                    dcw_ref[j, :, cs] += jnp.sum(prod.reshape(half // 8, 8, LANES), axis=0)
                du0_ref[r0:r0 + half, cs] = acc
        du0 = du0_ref[...]
        ga, sg = gv[:, :CONV_CH], _sigmoid(gv[:, CONV_CH:])
        dglu_ref[:, :CONV_CH] = (du0 * sg).astype(dglu_ref.dtype)
        dglu_ref[:, CONV_CH:] = (du0 * ga * sg * (1.0 - sg)).astype(dglu_ref.dtype)

    full = lambda shape: pl.BlockSpec(shape, lambda i: (0,) * len(shape))
    return _grid_call(
        body, [glu, glu, glu, u1, u1, u1, du3, du3, du3, cw, lw, lb, dproj], carry, name=name, grid=(nt,),
        in_specs=(_halo_specs(ts, GLU_COLS, rows, col) + _halo_specs(ts, CONV_CH, rows)
                  + _halo_specs(ts, CONV_CH, rows) + [full((CONV_K, CONV_CH)), _vec(CONV_CH), _vec(CONV_CH)]
                  + [pl.BlockSpec(memory_space=pl.ANY)]),
        out_specs=[_row(ts, GLU_COLS, col), full((CONV_K, 8, CONV_CH)), full((8, CONV_CH))],
        out_shape=[jax.ShapeDtypeStruct(dproj.shape, dproj.dtype),
                   jax.ShapeDtypeStruct((CONV_K, 8, CONV_CH), F32), jax.ShapeDtypeStruct((8, CONV_CH), F32)],
        scratch_shapes=[pltpu.VMEM((te, CONV_CH), F32), pltpu.VMEM((te, CONV_CH), F32),
                        pltpu.VMEM((7, te - 8, CONV_CH), F32), pltpu.VMEM((7, te - 8, CONV_CH), F32),
                        pltpu.VMEM((ts, CONV_CH), F32)],
        input_output_aliases={12: 0}, dims=("arbitrary",))


FFN_CW = 1408
FFN_NJ = FFN_H // FFN_CW


def _ffn_halo_specs(ts, rows, col_of, inner_rows):
    per = ts // HALO
    last = rows // HALO - 1
    if inner_rows:
        return [pl.BlockSpec((HALO, FFN_CW), lambda j, i: (jnp.maximum(i * per - 1, 0), col_of(j))),
                pl.BlockSpec((ts, FFN_CW), lambda j, i: (i, col_of(j))),
                pl.BlockSpec((HALO, FFN_CW), lambda j, i: (jnp.minimum((i + 1) * per, last), col_of(j)))]
    return [pl.BlockSpec((HALO, FFN_CW), lambda i, j: (jnp.maximum(i * per - 1, 0), col_of(j))),
            pl.BlockSpec((ts, FFN_CW), lambda i, j: (i, col_of(j))),
            pl.BlockSpec((HALO, FFN_CW), lambda i, j: (jnp.minimum((i + 1) * per, last), col_of(j)))]


def _ffn_ext(p_ref, m_ref, n_ref, sl, i, nt):
    return jnp.concatenate([jnp.where(i > 0, p_ref[:, sl].astype(F32), 0.0), m_ref[:, sl].astype(F32),
                            jnp.where(i < nt - 1, n_ref[:, sl].astype(F32), 0.0)], axis=0)


def _prev_row(v):
    return pltpu.roll(v, 1, 0)


def _next_row(v):
    return pltpu.roll(v, v.shape[0] - 1, 0)


def _ffn_act(up0, w3, b3, *, name):
    rows = up0.shape[0]
    ts = min(rows, 256)
    nt = rows // ts
    main = slice(HALO, HALO + ts)

    def body(gp, g, gn, vp, v, vn, wg, wv, bg, bv, a_ref, go_ref, vo_ref):
        i = pl.program_id(0)
        for ch in range(FFN_CW // LANES):
            sl = slice(ch * LANES, (ch + 1) * LANES)
            xg, xv = _ffn_ext(gp, g, gn, sl, i, nt), _ffn_ext(vp, v, vn, sl, i, nt)
            wgv, wvv = wg[:, sl], wv[:, sl]
            gate = (wgv[0:1] * _prev_row(xg) + wgv[1:2] * xg + wgv[2:3] * _next_row(xg))[main] + bg[:, sl]
            val = (wvv[0:1] * _prev_row(xv) + wvv[1:2] * xv + wvv[2:3] * _next_row(xv))[main] + bv[:, sl]
            a_ref[:, sl] = (gate * _sigmoid(gate) * val).astype(a_ref.dtype)
            go_ref[:, sl] = gate.astype(go_ref.dtype)
            vo_ref[:, sl] = val.astype(vo_ref.dtype)

    gcol, vcol = (lambda j: j), (lambda j: j + FFN_NJ)
    wspec = lambda col_of: pl.BlockSpec((FFN_K, FFN_CW), lambda i, j: (0, col_of(j)))
    bspec = lambda col_of: pl.BlockSpec((1, FFN_CW), lambda i, j: (0, col_of(j)))
    ospec = pl.BlockSpec((ts, FFN_CW), lambda i, j: (i, j))
    return pl.pallas_call(
        body, name=name, grid=(nt, FFN_NJ),
        in_specs=(_ffn_halo_specs(ts, rows, gcol, False) + _ffn_halo_specs(ts, rows, vcol, False)
                  + [wspec(gcol), wspec(vcol), bspec(gcol), bspec(vcol)]),
        out_specs=[ospec] * 3, out_shape=[jax.ShapeDtypeStruct((rows, FFN_H), MXU_DT)] * 3,
        compiler_params=_cp("parallel", "parallel"),
    )(up0, up0, up0, up0, up0, up0, w3, w3, b3, b3)


def _ffn_act_bwd(up0, gate_s, val_s, da, w3, *, carry=None, name):
    rows = up0.shape[0]
    ts = min(rows, 256)
    nt = rows // ts
    main = slice(HALO, HALO + ts)

    def body(gp, g, gn, vp, v, vn, sgp, sg, sgn, svp, sv, svn, ap, a, an, wg, wv,
             dg_ref, dv_ref, sg_ref, sv_ref):
        i = pl.program_id(1)

        @pl.when(i == 0)
        def _():
            sg_ref[...] = jnp.zeros_like(sg_ref)
            sv_ref[...] = jnp.zeros_like(sv_ref)

        rid = lax.broadcasted_iota(jnp.int32, (8, LANES), 0)
        for ch in range(FFN_CW // LANES):
            sl = slice(ch * LANES, (ch + 1) * LANES)
            xg, xv, da_e = _ffn_ext(gp, g, gn, sl, i, nt), _ffn_ext(vp, v, vn, sl, i, nt), _ffn_ext(ap, a, an, sl, i, nt)
            gate, val = _ffn_ext(sgp, sg, sgn, sl, i, nt), _ffn_ext(svp, sv, svn, sl, i, nt)
            wgv, wvv = wg[:, sl], wv[:, sl]
            xg_p, xg_n, xv_p, xv_n = _prev_row(xg), _next_row(xg), _prev_row(xv), _next_row(xv)
            sgm = _sigmoid(gate)
            eg = da_e * val * (sgm * (1.0 + gate * (1.0 - sgm)))
            ev = da_e * (gate * sgm)
            for e, taps, w, d_ref, s_ref in ((eg, (xg_p, xg, xg_n), wgv, dg_ref, sg_ref),
                                             (ev, (xv_p, xv, xv_n), wvv, dv_ref, sv_ref)):
                d0 = w[0:1] * _next_row(e) + w[1:2] * e + w[2:3] * _prev_row(e)
                d_ref[:, sl] = d0[main].astype(d_ref.dtype)
                dm = e[main]
                s_ref[:, sl] += (jnp.where(rid == 0, _colsum(dm * taps[0][main]), 0.0)
                                 + jnp.where(rid == 1, _colsum(dm * taps[1][main]), 0.0)
                                 + jnp.where(rid == 2, _colsum(dm * taps[2][main]), 0.0)
                                 + jnp.where(rid == 3, _colsum(dm), 0.0))

    gcol, vcol = (lambda j: j), (lambda j: j + FFN_NJ)
    wspec = lambda col_of: pl.BlockSpec((FFN_K, FFN_CW), lambda j, i: (0, col_of(j)))
    ospec = pl.BlockSpec((ts, FFN_CW), lambda j, i: (i, j))
    sspec = pl.BlockSpec((8, FFN_CW), lambda j, i: (0, j))
    return _grid_call(
        body, [up0] * 6 + [gate_s] * 3 + [val_s] * 3 + [da] * 3 + [w3, w3], carry, name=name, grid=(FFN_NJ, nt),
        in_specs=(_ffn_halo_specs(ts, rows, gcol, True) + _ffn_halo_specs(ts, rows, vcol, True)
                  + _ffn_halo_specs(ts, rows, gcol, True) * 3 + [wspec(gcol), wspec(vcol)]),
        out_specs=[ospec, ospec, sspec, sspec],
        out_shape=[jax.ShapeDtypeStruct((rows, FFN_H), MXU_DT), jax.ShapeDtypeStruct((rows, FFN_H), MXU_DT),
                   jax.ShapeDtypeStruct((8, FFN_H), F32), jax.ShapeDtypeStruct((8, FFN_H), F32)],
        dims=("parallel", "arbitrary"))


def _adam_math(w, g, m, v):
    m = ADAM_B1 * m + (1.0 - ADAM_B1) * g
    v = ADAM_B2 * v + (1.0 - ADAM_B2) * (g * g)
    m_hat = m / (1.0 - ADAM_B1 ** ADAM_STEP)
    v_hat = v / (1.0 - ADAM_B2 ** ADAM_STEP)
    delta = -ADAM_LR * (m_hat / (jnp.sqrt(v_hat) + ADAM_EPS) + ADAM_WD * w)
    return delta, m, v


ROW_TILE_BYTES = 8 << 20


def _row_tile(rows, row_bytes):
    tiles = [rows] + [rows // k for k in range(2, rows // 16 + 1) if rows % k == 0 and (rows // k) % 16 == 0]
    return next(t for t in tiles if t * row_bytes <= ROW_TILE_BYTES)


def _adam(w, m, v, parts, *, name):
    rows, cols = w.shape
    nparts = parts.shape[0]
    tr = _row_tile(rows, cols * (7 * 4 + nparts * parts.dtype.itemsize))

    def body(w_ref, m_ref, v_ref, p_ref, g_ref, d_ref, nm_ref, nv_ref):
        g = p_ref[0].astype(F32)
        for p in range(1, nparts):
            g = g + p_ref[p].astype(F32)
        g_ref[...] = g
        d_ref[...], nm_ref[...], nv_ref[...] = _adam_math(w_ref[...], g, m_ref[...], v_ref[...])

    spec = _row(tr, cols)
    return pl.pallas_call(
        body, name=name, grid=(rows // tr,),
        in_specs=[spec, spec, spec, pl.BlockSpec((nparts, tr, cols), lambda i: (0, i, 0))],
        out_specs=[spec] * 4, out_shape=[jax.ShapeDtypeStruct((rows, cols), F32)] * 4,
        compiler_params=_cp("parallel"),
    )(w, m, v, parts)


def _adam_many(ws, ms, vs, gs, *, name):
    n = len(ws)

    def body(*refs):
        ins, outs = refs[:4 * n], refs[4 * n:]
        for k in range(n):
            delta, new_m, new_v = _adam_math(ins[k][...], ins[3 * n + k][...], ins[n + k][...], ins[2 * n + k][...])
            outs[k][...], outs[n + k][...], outs[2 * n + k][...] = delta, new_m, new_v

    vm = pl.BlockSpec(memory_space=pltpu.VMEM)
    res = pl.pallas_call(
        body, name=name, in_specs=[vm] * (4 * n), out_specs=[vm] * (3 * n),
        out_shape=[jax.ShapeDtypeStruct(w.shape, F32) for w in ws] * 3,
        compiler_params=pltpu.CompilerParams(vmem_limit_bytes=VMEM_LIMIT),
    )(*ws, *ms, *vs, *gs)
    return res[:n], res[n:2 * n], res[2 * n:]


def _sum_parts(parts, *, name):
    nparts, rows, cols = parts.shape
    tr = _row_tile(rows, cols * (4 + nparts * parts.dtype.itemsize))

    def body(p_ref, o_ref):
        g = p_ref[0].astype(F32)
        for p in range(1, nparts):
            g = g + p_ref[p].astype(F32)
        o_ref[...] = g

    return pl.pallas_call(
        body, name=name, grid=(rows // tr,),
        in_specs=[pl.BlockSpec((nparts, tr, cols), lambda i: (0, i, 0))], out_specs=_row(tr, cols),
        out_shape=jax.ShapeDtypeStruct((rows, cols), F32), compiler_params=_cp("parallel"),
    )(parts)


def _my_place():
    return lax.axis_index("x"), lax.axis_index("y"), lax.axis_index("c")


def _dev_index(p):
    return 4 * p[0] + 2 * p[1] + p[2]


def _all_gather(xs, *, hbm, name):
    return _run_comm(_gather_plan(xs), pl.ANY if hbm else pltpu.VMEM, name)


class _Comm(NamedTuple):
    ins: list
    outs: list
    n_remote: int
    n_local: int
    start: Callable
    finish: Callable


def _join_plans(*plans):
    def split(in_refs, out_refs, send_sems, recv_sems, local_sems):
        i = o = r = l = 0
        for p in plans:
            ni, no = len(p.ins), len(p.outs)
            yield p, (in_refs[i:i + ni], out_refs[o:o + no], send_sems.at[pl.ds(r, p.n_remote)],
                      recv_sems.at[pl.ds(r, p.n_remote)], local_sems.at[pl.ds(l, p.n_local)])
            i, o, r, l = i + ni, o + no, r + p.n_remote, l + p.n_local

    def start(*refs):
        for p, part in split(*refs):
            p.start(*part)

    def finish(*refs):
        for p, part in split(*refs):
            p.finish(*part)

    return _Comm([v for p in plans for v in p.ins], [v for p in plans for v in p.outs],
                 sum(p.n_remote for p in plans), sum(p.n_local for p in plans), start, finish)


def _comm_scratch(plan):
    return [pltpu.SemaphoreType.DMA((plan.n_remote,)), pltpu.SemaphoreType.DMA((plan.n_remote,)),
            pltpu.SemaphoreType.DMA((plan.n_local,))]


def _run_comm(plan, space, name):
    n_in, n_out = len(plan.ins), len(plan.outs)

    def body(*refs):
        args = (refs[:n_in], refs[n_in:n_in + n_out], *refs[n_in + n_out:])
        plan.start(*args)
        plan.finish(*args)

    return pl.pallas_call(
        body, name=name, out_shape=plan.outs,
        in_specs=[pl.BlockSpec(memory_space=space)] * n_in, out_specs=[pl.BlockSpec(memory_space=space)] * n_out,
        scratch_shapes=_comm_scratch(plan),
        compiler_params=pltpu.CompilerParams(vmem_limit_bytes=VMEM_LIMIT),
    )(*plan.ins)


def _grid_call(body, ins, carry, *, name, grid, in_specs, out_specs, out_shape, dims, scratch_shapes=(),
               input_output_aliases=None):
    if carry is None:
        res = pl.pallas_call(
            body, name=name, grid=grid, in_specs=list(in_specs), out_specs=list(out_specs),
            out_shape=list(out_shape), scratch_shapes=list(scratch_shapes),
            input_output_aliases=input_output_aliases or {}, compiler_params=_cp(*dims))(*ins)
        return list(res), None

    def at(pos):
        conds = [pl.program_id(k) == p for k, p in enumerate(pos)]
        out = conds[0]
        for cnd in conds[1:]:
            out = jnp.logical_and(out, cnd)
        return out

    return _carried_call(body, carry, lambda: at([0] * len(grid)), lambda: at([g - 1 for g in grid]), ins,
                         name=name, grid=grid, in_specs=in_specs, out_specs=out_specs, out_shape=out_shape,
                         scratch_shapes=scratch_shapes, input_output_aliases=input_output_aliases)


def _carried_call(body, plan, first, last, ins, *, name, grid, in_specs, out_specs, out_shape, scratch_shapes=(),
                  input_output_aliases=None):
    in_specs, out_specs, out_shape = list(in_specs), list(out_specs), list(out_shape)
    n_in, n_out, n_scr = len(in_specs), len(out_specs), len(scratch_shapes)
    c_in, c_out = len(plan.ins), len(plan.outs)
    hbm = pl.BlockSpec(memory_space=pl.ANY)

    def full_body(*refs):
        ins, c_ins = refs[:n_in], refs[n_in:n_in + c_in]
        outs = refs[n_in + c_in:n_in + c_in + n_out]
        c_outs = refs[n_in + c_in + n_out:n_in + c_in + n_out + c_out]
        scr = refs[n_in + c_in + n_out + c_out:]
        sems = scr[n_scr:]

        @pl.when(first())
        def _():
            plan.start(c_ins, c_outs, *sems)

        body(*ins, *outs, *scr[:n_scr])

        @pl.when(last())
        def _():
            plan.finish(c_ins, c_outs, *sems)

    res = pl.pallas_call(
        full_body, name=name, grid=grid,
        in_specs=in_specs + [hbm] * c_in, out_specs=out_specs + [hbm] * c_out,
        out_shape=out_shape + list(plan.outs),
        scratch_shapes=list(scratch_shapes) + _comm_scratch(plan),
        input_output_aliases=input_output_aliases or {},
        compiler_params=_cp(*(["arbitrary"] * len(grid))),
    )(*ins, *plan.ins)
    return list(res[:n_out]), list(res[n_out:])


def _gather_plan(xs):
    n = len(xs)
    ms = [v.shape[0] for v in xs]

    def tools(x_refs, o_refs, send_sems, recv_sems, local_sems):
        x, y, c = _my_place()
        me, sib = (x, y, c), (x, y, 1 - c)
        chips = [(1 - x, y), (x, 1 - y), (1 - x, 1 - y)]

        def rows(a, p):
            return o_refs[a].at[pl.ds(pl.multiple_of(_dev_index(p) * ms[a], 8), ms[a])]

        def copy(a, k, block, to, src=None):
            return pltpu.make_async_remote_copy(
                src_ref=rows(a, block) if src is None else src, dst_ref=rows(a, block),
                send_sem=send_sems.at[a * 7 + k], recv_sem=recv_sems.at[a * 7 + k],
                device_id=to, device_id_type=MESH)

        mine = [pltpu.make_async_copy(x_refs[a], rows(a, me), local_sems.at[a]) for a in range(n)]
        first = []
        for a in range(n):
            first.append(copy(a, 0, me, sib, src=x_refs[a]))
            first += [copy(a, 1 + j, me, (*chip, c), src=x_refs[a]) for j, chip in enumerate(chips)]
        return me, sib, chips, c, copy, mine, first

    def start(*refs):
        _, _, _, _, _, mine, first = tools(*refs)
        for cp in mine + first:
            cp.start()

    def finish(*refs):
        me, sib, chips, c, copy, mine, first = tools(*refs)
        passed = []
        for j, chip in enumerate(chips):
            for a in range(n):
                copy(a, 1 + j, (*chip, c), me).wait_recv()
                cp = copy(a, 4 + j, (*chip, c), sib)
                cp.start()
                passed.append(cp)
        for a in range(n):
            copy(a, 0, sib, me).wait_recv()
            for j, chip in enumerate(chips):
                copy(a, 4 + j, (*chip, 1 - c), me).wait_recv()
        for cp in first + passed:
            cp.wait_send()
        for cp in mine:
            cp.wait()

    outs = [jax.ShapeDtypeStruct((N_DEV * v.shape[0], v.shape[1]), v.dtype) for v in xs]
    return _Comm(list(xs), outs, 7 * n, n, start, finish)


def _exchange_plan(gs, cols=None):
    n = len(gs)
    rs = [v.shape[0] // N_DEV for v in gs]
    flips = [(bx, by, bc) for bx in (0, 1) for by in (0, 1) for bc in (0, 1)][1:]

    def tools(g_refs, o_refs, send_sems, recv_sems, local_sems):
        x, y, c = _my_place()
        me = (x, y, c)

        def block(ref, a, p):
            return ref.at[pl.ds(_dev_index(p) * rs[a], rs[a])]

        def src(a, p):
            rows = pl.ds(_dev_index(p) * rs[a], rs[a])
            return g_refs[a].at[rows] if cols is None else g_refs[a].at[rows, pl.ds(cols[0], cols[1])]

        def peer(f):
            return (1 - x if f[0] else x, 1 - y if f[1] else y, 1 - c if f[2] else c)

        def copy(a, k, to):
            return pltpu.make_async_remote_copy(
                src_ref=src(a, to), dst_ref=block(o_refs[a], a, me),
                send_sem=send_sems.at[a * 7 + k], recv_sem=recv_sems.at[a * 7 + k],
                device_id=to, device_id_type=MESH)

        def arrival(a, k, frm):
            return pltpu.make_async_remote_copy(
                src_ref=src(a, frm), dst_ref=block(o_refs[a], a, frm),
                send_sem=send_sems.at[a * 7 + k], recv_sem=recv_sems.at[a * 7 + k],
                device_id=frm, device_id_type=MESH)

        mine = [pltpu.make_async_copy(src(a, me), block(o_refs[a], a, me), local_sems.at[a])
                for a in range(n)]
        sends = [copy(a, k, peer(f)) for a in range(n) for k, f in enumerate(flips)]
        arrivals = [arrival(a, k, peer(f)) for a in range(n) for k, f in enumerate(flips)]
        return mine, sends, arrivals

    def start(*refs):
        mine, sends, _ = tools(*refs)
        for cp in mine + sends:
            cp.start()

    def finish(*refs):
        mine, sends, arrivals = tools(*refs)
        for cp in arrivals:
            cp.wait_recv()
        for cp in sends:
            cp.wait_send()
        for cp in mine:
            cp.wait()

    outs = [jax.ShapeDtypeStruct((v.shape[0], v.shape[1] if cols is None else cols[1]), v.dtype) for v in gs]
    return _Comm(list(gs), outs, 7 * n, n, start, finish)


def _rope_tables(seq):
    t = jnp.arange(seq)
    row, col = t // GRID_W, t % GRID_W
    half = HEAD_DIM // 2
    inv = ROPE_BASE ** (-jnp.arange(0, half, 2, dtype=F32) / half)
    ang_r = row.astype(F32)[:, None] * inv
    ang_c = col.astype(F32)[:, None] * inv
    ang = jnp.concatenate([ang_r, ang_r, ang_c, ang_c], axis=-1)
    return jnp.tile(jnp.cos(ang), (1, 2)), jnp.tile(jnp.sin(ang), (1, 2))


def _to_slots(w):
    return w.reshape(N_KV_HEADS, GQA_GROUP, HEAD_DIM, w.shape[1]).transpose(1, 0, 2, 3).reshape(w.shape)


def _from_slots(w):
    return w.reshape(GQA_GROUP, N_KV_HEADS, HEAD_DIM, w.shape[1]).transpose(1, 0, 2, 3).reshape(w.shape)


def _pack(vs):
    flat = jnp.concatenate([v.reshape(-1).astype(F32) for v in vs])
    total = -(-flat.shape[0] // (8 * LANES)) * (8 * LANES)
    return jnp.pad(flat, (0, total - flat.shape[0])).reshape(-1, LANES)


def _unpack(packed, like):
    flat, out, off = packed.reshape(-1), [], 0
    for v in like:
        size = math.prod(v.shape)
        out.append(flat[off:off + size].reshape(v.shape))
        off += size
    return out


def _silu(v):
    return v * jax.nn.sigmoid(v)


def kernel(x, c, ctx, c_ctx, w_mod, b_mod, norm_mix_w, w_in, q_norm_w, k_norm_w, sink_logit, conv_w, conv_b, conv_norm_w, conv_norm_b, w_out, norm_ffn_w, w_up, ffn_conv_w, ffn_conv_b, w_down, loss_target, m_c_ctx, m_w_mod, m_b_mod, m_norm_mix_w, m_w_in, m_q_norm_w, m_k_norm_w, m_sink_logit, m_conv_w, m_conv_b, m_conv_norm_w, m_conv_norm_b, m_w_out, m_norm_ffn_w, m_w_up, m_ffn_conv_w, m_ffn_conv_b, m_w_down, v_c_ctx, v_w_mod, v_b_mod, v_norm_mix_w, v_w_in, v_q_norm_w, v_k_norm_w, v_sink_logit, v_conv_w, v_conv_b, v_conv_norm_w, v_conv_norm_b, v_w_out, v_norm_ffn_w, v_w_up, v_ffn_conv_w, v_ffn_conv_b, v_w_down):
    d = D_MODEL
    seq, n_ctx = x.shape[1], ctx.shape[1]
    me = _dev_index(_my_place())
    xs, ctxs, tgt = x[0], ctx[0], loss_target[0]

    small = _pack([c[0], conv_w[0], ffn_conv_w[0]])
    small_all = _all_gather([small], hbm=False, name="gather_small")[0].reshape(N_DEV, -1)
    n_cw, n_fw = conv_w[0].size, ffn_conv_w[0].size
    c_all = small_all[:, :d]
    cw_all = small_all[:, d:d + n_cw].reshape(N_DEV, CONV_K, -1)
    fw_all = small_all[:, d + n_cw:d + n_cw + n_fw].reshape(N_DEV, FFN_K, -1)
    conv_w_f = cw_all.transpose(1, 0, 2).reshape(CONV_K, CONV_CH)
    ffn_w_f = fw_all.transpose(1, 0, 2).reshape(FFN_K, 2 * FFN_H)

    mcols = w_mod.shape[2]
    act = jnp.zeros((16, d), F32).at[:N_DEV].set(_silu(c_all)).at[N_DEV].set(_silu(c_ctx))
    mod_part = _mm(act, w_mod[0], 16, mcols, d, name="mod_fwd")
    mod_all = _all_gather([mod_part], hbm=False, name="gather_mod")[0]
    mod_all = mod_all.reshape(N_DEV, 16, mcols).transpose(1, 0, 2).reshape(16, 6 * d) + b_mod
    mod = lax.dynamic_slice_in_dim(mod_all, me, 1, axis=0)
    sh1, sc1, g1, sh2, sc2, g2 = [mod[:, k * d:(k + 1) * d] for k in range(6)]
    sh1c, sc1c = mod_all[N_DEV:N_DEV + 1, :d], mod_all[N_DEV:N_DEV + 1, d:2 * d]

    cos, sin = _rope_tables(seq)
    ones_c, zeros_c = jnp.ones((n_ctx, LANES), F32), jnp.zeros((n_ctx, LANES), F32)
    qk_w = jnp.concatenate([jnp.tile(q_norm_w, (1, N_Q_HEADS)), jnp.tile(k_norm_w, (1, N_KV_HEADS))], axis=1)
    kc_w = jnp.tile(k_norm_w, (1, N_KV_HEADS))

    h, (w_in_t,) = _prenorm(xs, norm_mix_w, sc1, sh1, carry=_gather_plan([w_in[0].T.astype(MXU_DT)]),
                            name="prenorm_mix")
    hc, _ = _prenorm(ctxs, norm_mix_w, sc1c, sh1c, name="prenorm_ctx")
    w_in_p = jnp.concatenate([_to_slots(w_in_t[:Q_COLS]), w_in_t[Q_COLS:QKV_COLS],
                              jnp.zeros((GLU_OFF - QKV_COLS, d), MXU_DT), w_in_t[QKV_COLS:]])
    proj = _mm(h, w_in_p, seq, IN_PAD, d, tb=True, name="proj_in")
    kv_ctx = _mm(hc, w_in_p, n_ctx, 2 * KV_COLS, d, tb=True, n0=Q_COLS, name="proj_ctx")
    qk_r = _qk_prep(proj, QK_COLS, cos, sin, qk_w, name="qk_prep")
    kc_n = _qk_prep(kv_ctx, KV_COLS, ones_c, zeros_c, kc_w, name="k_ctx_prep")
    (attn_o, lse), (w_up_t, w_out_f) = _attn_fwd(
        sink_logit, qk_r, proj, kc_n, kv_ctx, name="attn_fwd",
        carry=_gather_plan([w_up[0].T.astype(MXU_DT), w_out[0].astype(MXU_DT)]))
    w_out_f = jnp.concatenate([_to_slots(w_out_f[:Q_COLS]), w_out_f[Q_COLS:]])
    (u3, u1), (w_down_f,) = _conv_fwd(proj, GLU_OFF // GLU_COLS, conv_w_f, conv_b, conv_norm_w, conv_norm_b,
                                      carry=_gather_plan([w_down[0].astype(MXU_DT)]), name="conv_fwd")
    mix = _mm(attn_o, w_out_f, seq, d, Q_COLS, name="out_attn")
    mix = _mm(u3, w_out_f, seq, d, CONV_CH, k0=Q_COLS, add=mix, name="out_conv")

    x1, h2 = _resid_prenorm(xs, mix, g1, norm_ffn_w, sc2, sh2, name="prenorm_ffn")
    up0 = _mm(h2, w_up_t, seq, 2 * FFN_H, d, tb=True, out_dtype=MXU_DT, name="ffn_up")
    act_a, gate_s, val_s = _ffn_act(up0, ffn_w_f, ffn_conv_b, name="ffn_act")
    ffn = _mm(act_a, w_down_f, seq, d, FFN_H, name="ffn_down")
    loss_p, dy, dffn, dg2 = _loss_head(x1, ffn, g2, tgt, name="loss_head")

    da = _mm(dffn, w_down_f, seq, FFN_H, d, tb=True, out_dtype=MXU_DT, name="ffn_down_dx")
    gw_down = _mm(act_a, dffn, FFN_H, d, seq, ta=True, out_dtype=MXU_DT, name="ffn_down_dw")
    (dgate0, dval0, s_gate, s_val), (rx_down,) = _ffn_act_bwd(
        up0, gate_s, val_s, da, ffn_w_f, carry=_exchange_plan([gw_down]), name="ffn_act_bwd")
    gw_up_t = _mm(dgate0, h2, FFN_H, d, seq, ta=True, out_dtype=MXU_DT, out_rows=2 * FFN_H, name="ffn_up_dw_gate")
    gw_up_t = _mm(dval0, h2, FFN_H, d, seq, ta=True, into=gw_up_t, m0=FFN_H, name="ffn_up_dw_val")
    qd = d // 4
    dh2, (rx_up_a,) = _mm(dgate0, w_up_t, seq, d, FFN_H, a2=dval0, name="ffn_up_dx",
                          carry=_exchange_plan([gw_up_t], cols=(0, qd)))
    (dx1, dmix, dsh2, dsc2, dnw2, dg1), _ = _norm_bwd(
        dh2, x1, norm_ffn_w, sc2, res=dy, gate=(mix, g1), name="prenorm_ffn_bwd")

    dattn = _mm(dmix, w_out_f, seq, Q_COLS, d, tb=True, out_dtype=MXU_DT, name="out_dx_attn")
    du3 = _mm(dmix, w_out_f, seq, CONV_CH, d, tb=True, n0=Q_COLS, name="out_dx_conv")
    gw_out = _mm(attn_o, dmix, Q_COLS, d, seq, ta=True, out_dtype=MXU_DT, out_rows=Q_COLS + CONV_CH,
                 name="out_dw_attn")
    gw_out = _mm(u3, dmix, CONV_CH, d, seq, ta=True, into=gw_out, m0=Q_COLS, name="out_dw_conv")
    gw_out = jnp.concatenate([_from_slots(gw_out[:Q_COLS]), gw_out[Q_COLS:]])
    (dq, dk, dv, dkc_r, dvc, dsink_rows), (rx_out, rx_up_b) = _attn_bwd(
        sink_logit, qk_r, proj, kc_n, kv_ctx, attn_o, lse, dattn, name="attn_bwd",
        carry=_join_plans(_exchange_plan([gw_out]), _exchange_plan([gw_up_t], cols=(qd, qd))))
    dproj, dqk_w = _qk_prep_bwd(proj, [dq, dk], QK_COLS, cos, sin, qk_w, tail=dv, name="qk_prep_bwd")
    (dproj, dcw8, dvec), (rx_up_c,) = _conv_bwd(
        proj, GLU_OFF // GLU_COLS, u1, du3, dproj, conv_w_f, conv_norm_w, conv_norm_b,
        carry=_exchange_plan([gw_up_t], cols=(2 * qd, 2 * qd)), name="conv_bwd")
    dkc, dkc_w = _qk_prep_bwd(kv_ctx, [dkc_r], KV_COLS, ones_c, zeros_c, kc_w, name="k_ctx_prep_bwd")
    dkv_ctx = jnp.concatenate([dkc, dvc.astype(MXU_DT)], axis=1)
    gw_p = _mm(dproj, h, IN_PAD, d, seq, ta=True, name="proj_dw")
    gw_ctx = _mm(dkv_ctx, hc, 2 * KV_COLS, d, n_ctx, ta=True, name="proj_dw_ctx")
    gw_in_t = jnp.concatenate([_from_slots(gw_p[:Q_COLS]), gw_p[Q_COLS:QKV_COLS] + gw_ctx, gw_p[GLU_OFF:]],
                              axis=0).astype(MXU_DT)
    dh, (rx_in,) = _mm(dproj, w_in_p, seq, d, IN_PAD, carry=_exchange_plan([gw_in_t]), name="proj_dx")
    dhc = _mm(dkv_ctx, w_in_p, n_ctx, d, 2 * KV_COLS, k0=Q_COLS, name="proj_dx_ctx")
    (grad_x, dsh1, dsc1, dnw1), _ = _norm_bwd(dh, xs, norm_mix_w, sc1, res=dx1, name="prenorm_mix_bwd")
    (dsh1c, dsc1c, dnw1c), _ = _norm_bwd(dhc, ctxs, norm_mix_w, sc1c, want_dx=False, name="prenorm_ctx_bwd")

    dmod = jnp.concatenate([dsh1, dsc1, dg1, dsh2, dsc2, dg2], axis=1)
    dmod_ctx = jnp.concatenate([dsh1c, dsc1c], axis=1)
    d_qn = dqk_w[0, :Q_COLS].reshape(N_Q_HEADS, HEAD_DIM).sum(0)
    d_kn = (dqk_w[0, Q_COLS:].reshape(N_KV_HEADS, HEAD_DIM).sum(0)
            + dkc_w[0].reshape(N_KV_HEADS, HEAD_DIM).sum(0))
    d_ffn_w = jnp.concatenate([s_gate[:FFN_K], s_val[:FFN_K]], axis=1)
    d_ffn_b = jnp.concatenate([s_gate[FFN_K], s_val[FFN_K]])
    d_sink = dsink_rows.reshape(N_Q_HEADS, BLOCK).sum(1)
    summed_like = [(dnw1 + dnw1c), d_qn[None], d_kn[None], d_sink[None], dvec[0:1], dvec[1:2],
                   dvec[2:3], dnw2, d_ffn_b[None], dcw8.sum(1), d_ffn_w, loss_p[0:1, 0:1]]
    pack = _pack([dmod, dmod_ctx] + summed_like)
    pack_all = _all_gather([pack], hbm=False, name="gather_small_grads")[0]
    pack_all = pack_all.reshape(N_DEV, pack.shape[0], LANES)
    tot = _sum_parts(pack_all, name="sum_small_grads")
    (dmod_sum, dmc_sum, g_nmix, g_qn, g_kn, g_sink, g_cb, g_lw, g_lb, g_nffn, g_fb, g_cw_f, g_fw_f,
     loss_sum) = _unpack(tot, [dmod, dmod_ctx] + summed_like)
    loss = loss_sum[0, 0]
    dmod_all = pack_all.reshape(N_DEV, -1)[:, :6 * d]
    g_b_mod = dmod_sum.at[:, :2 * d].add(dmc_sum)

    lo = me * mcols
    dm_rows = jnp.zeros((16, 6 * d), F32).at[:N_DEV].set(dmod_all).at[N_DEV, :2 * d].set(dmc_sum[0])
    dm_mine = lax.dynamic_slice_in_dim(dm_rows, lo, mcols, axis=1)
    parts_mod = _mm(act, dm_mine, d, mcols, 16, ta=True, name="mod_dw")[None]
    dact_part = _mm(dm_mine[N_DEV:N_DEV + 8], w_mod[0], 8, d, mcols, tb=True, name="mod_dx_ctx")
    dact_all = _all_gather([dact_part], hbm=False, name="gather_c_ctx_grad")[0].reshape(N_DEV, 8, d)
    dact = _sum_parts(dact_all, name="sum_c_ctx_grad")[0]
    sg = jax.nn.sigmoid(c_ctx)
    g_c_ctx = dact * (sg * (1.0 + c_ctx * (1.0 - sg)))

    def stacked(rx):
        return rx.reshape(N_DEV, rx.shape[0] // N_DEV, rx.shape[1])

    g_w_in = _sum_parts(stacked(rx_in), name="sum_w_in").T[None]
    g_w_up = jnp.concatenate([_sum_parts(stacked(rx_up_a), name="sum_w_up_a"),
                              _sum_parts(stacked(rx_up_b), name="sum_w_up_b"),
                              _sum_parts(stacked(rx_up_c), name="sum_w_up_c")], axis=1).T[None]
    big = {}
    big["w_in"] = _adam(w_in[0], m_w_in[0], v_w_in[0], g_w_in, name="adam_w_in")
    big["w_up"] = _adam(w_up[0], m_w_up[0], v_w_up[0], g_w_up, name="adam_w_up")
    big["w_out"] = _adam(w_out[0], m_w_out[0], v_w_out[0], stacked(rx_out), name="adam_w_out")
    big["w_down"] = _adam(w_down[0], m_w_down[0], v_w_down[0], stacked(rx_down), name="adam_w_down")
    big["w_mod"] = _adam(w_mod[0], m_w_mod[0], v_w_mod[0], parts_mod, name="adam_w_mod")

    ccols, fcols = conv_w.shape[2], ffn_conv_w.shape[2]
    g_conv_w = lax.dynamic_slice_in_dim(g_cw_f, me * ccols, ccols, axis=1)[None]
    g_ffn_w = lax.dynamic_slice_in_dim(g_fw_f, me * fcols, fcols, axis=1)[None]
    names = ["c_ctx", "b_mod", "norm_mix_w", "q_norm_w", "k_norm_w", "sink_logit", "conv_w", "conv_b",
             "conv_norm_w", "conv_norm_b", "norm_ffn_w", "ffn_conv_w", "ffn_conv_b"]
    ws = [c_ctx, b_mod, norm_mix_w, q_norm_w, k_norm_w, sink_logit, conv_w, conv_b, conv_norm_w, conv_norm_b,
          norm_ffn_w, ffn_conv_w, ffn_conv_b]
    msm = [m_c_ctx, m_b_mod, m_norm_mix_w, m_q_norm_w, m_k_norm_w, m_sink_logit, m_conv_w, m_conv_b,
           m_conv_norm_w, m_conv_norm_b, m_norm_ffn_w, m_ffn_conv_w, m_ffn_conv_b]
    vsm = [v_c_ctx, v_b_mod, v_norm_mix_w, v_q_norm_w, v_k_norm_w, v_sink_logit, v_conv_w, v_conv_b,
           v_conv_norm_w, v_conv_norm_b, v_norm_ffn_w, v_ffn_conv_w, v_ffn_conv_b]
    gsm = [g_c_ctx, g_b_mod, g_nmix, g_qn, g_kn, g_sink, g_conv_w, g_cb, g_lw, g_lb, g_nffn, g_ffn_w, g_fb]
    deltas, new_ms, new_vs = _adam_many(ws, msm, vsm, gsm, name="adam_small")
    sm = {nm: vals for nm, vals in zip(names, zip(gsm, deltas, new_ms, new_vs))}

    def out4(nm):
        if nm in sm:
            return sm[nm]
        return tuple(t[None] for t in big[nm])

    order = ["c_ctx", "w_mod", "b_mod", "norm_mix_w", "w_in", "q_norm_w", "k_norm_w", "sink_logit", "conv_w",
             "conv_b", "conv_norm_w", "conv_norm_b", "w_out", "norm_ffn_w", "w_up", "ffn_conv_w", "ffn_conv_b",
             "w_down"]
    quads = [out4(nm) for nm in order]
    return (loss, grad_x[None], *[q[0] for q in quads], *[q[1] for q in quads],
            *[q[2] for q in quads], *[q[3] for q in quads])
```

```python
import math
from typing import Callable, NamedTuple

import jax
import jax.numpy as jnp
from jax import lax
from jax.experimental import pallas as pl
from jax.experimental.pallas import tpu as pltpu

F32 = jnp.float32
MXU_DT = jnp.bfloat16

D_MODEL = 1024
GRID_W = 64
HEAD_DIM = 64
N_Q_HEADS = 8
N_KV_HEADS = 2
GQA_GROUP = 4
WINDOW = 128
BLOCK = 128
Q_COLS = 512
KV_COLS = 128
QK_COLS = Q_COLS + KV_COLS
QKV_COLS = Q_COLS + 2 * KV_COLS
CONV_CH = 512
GLU_COLS = 2 * CONV_CH
IN_COLS = QKV_COLS + GLU_COLS
CONV_K = 31
CONV_PAD = 15
FFN_H = 2816
FFN_K = 3
ROPE_BASE = 10000.0
EPS = 1e-6
NEG_INF = -1e30
N_DEV = 8
HALO = 16
LANES = 128
ROW_TS = 512

ADAM_LR = 0.001
ADAM_B1 = 0.9
ADAM_B2 = 0.999
ADAM_EPS = 1e-08
ADAM_WD = 0.01
ADAM_STEP = 10

MESH = pl.DeviceIdType.MESH
VMEM_LIMIT = 56 << 20
MM_VMEM_BUDGET = 44 << 20
GLU_OFF = 1024
IN_PAD = GLU_OFF + GLU_COLS


def _cp(*dims):
    return pltpu.CompilerParams(dimension_semantics=dims or None, vmem_limit_bytes=VMEM_LIMIT)


def _row(ts, w, col=0):
    return pl.BlockSpec((ts, w), lambda i: (i, col))


def _vec(w):
    return pl.BlockSpec((1, w), lambda i: (0, 0))


def _colsum(v):
    return jnp.sum(v, axis=0, keepdims=True)


def _sigmoid(v):
    return 0.5 * jnp.tanh(0.5 * v) + 0.5


def _mm(a, b, m, n, k, *, ta=False, tb=False, n0=0, k0=0, add=None, out_dtype=F32, into=None, m0=0,
        out_rows=None, a2=None, carry=None, name):
    has_add, has_a2 = add is not None, a2 is not None
    assert not (has_a2 and (ta or tb))
    if into is not None:
        out_dtype = into.dtype
    sa, sb, so = a.dtype.itemsize, b.dtype.itemsize, jnp.dtype(out_dtype).itemsize
    sadd = add.dtype.itemsize if has_add else 0
    na = 2 if has_a2 else 1

    def fits(tm, tn):
        return 2 * (na * k * (tm * sa + tn * sb) + tm * tn * (so + sadd)) <= MM_VMEM_BUDGET

    tms = [m] if m <= 1024 else [t for t in (1024, 1408, 768, 512, 256, 128) if m % t == 0]
    tns = [t for t in ((1024, 512, 256, 128) if ta else (1408, 512, 256, 128)) if n % t == 0 and n0 % t == 0]
    tm, tn = next((tm, tn) for tm in tms for tn in tns if fits(tm, tn))
    a_spec = (pl.BlockSpec((k, tm), lambda i, j: (0, i)) if ta else pl.BlockSpec((tm, k), lambda i, j: (i, 0)))
    nb0 = n0 // tn
    if tb:
        assert k0 == 0
        b_spec = pl.BlockSpec((tn, k), lambda i, j: (j + nb0, 0))
    else:
        assert k0 % (na * k) == 0, (k0, k)
        kb0 = k0 // (na * k)
        b_spec = pl.BlockSpec((na * k, tn), lambda i, j: (kb0, j + nb0))
    assert m0 % tm == 0, (m0, tm)
    mb0 = m0 // tm
    o_spec = pl.BlockSpec((tm, tn), lambda i, j: (i + mb0, j))
    dims = (((0 if ta else 1,), (1 if tb else 0,)), ((), ()))

    def body(*refs):
        a_ref, b_ref, o_ref = refs[0], refs[na], refs[-1]
        if has_a2:
            res = (jnp.dot(a_ref[...].astype(MXU_DT), b_ref[0:k, :].astype(MXU_DT), preferred_element_type=F32)
                   + jnp.dot(refs[1][...].astype(MXU_DT), b_ref[k:2 * k, :].astype(MXU_DT),
                             preferred_element_type=F32))
        else:
            res = lax.dot_general(a_ref[...].astype(MXU_DT), b_ref[...].astype(MXU_DT), dims,
                                  preferred_element_type=F32)
        if has_add:
            res = res + refs[na + 1][...].astype(F32)
        o_ref[...] = res.astype(o_ref.dtype)

    ins = [a] + ([a2] if has_a2 else []) + [b] + ([add] if has_add else []) + ([into] if into is not None else [])
    specs = ([a_spec] * na + [b_spec] + ([pl.BlockSpec((tm, tn), lambda i, j: (i, j))] if has_add else [])
             + ([pl.BlockSpec(memory_space=pl.ANY)] if into is not None else []))
    out_shape = (jax.ShapeDtypeStruct(into.shape, into.dtype) if into is not None
                 else jax.ShapeDtypeStruct((out_rows or m, n), out_dtype))
    (out,), carried = _grid_call(
        body, ins, carry, name=name, grid=(m // tm, n // tn), in_specs=specs, out_specs=[o_spec],
        out_shape=[out_shape], input_output_aliases={len(ins) - 1: 0} if into is not None else None,
        dims=("parallel", "parallel"))
    return out if carry is None else (out, carried)


def _rms_stats(xv):
    r = lax.rsqrt(jnp.mean(xv * xv, axis=-1, keepdims=True) + EPS)
    return r, xv * r


def _prenorm(x, nw, sc, sh, *, carry=None, name):
    rows, d = x.shape
    ts = min(rows, ROW_TS)

    def body(x_ref, nw_ref, sc_ref, sh_ref, h_ref):
        _, xn = _rms_stats(x_ref[...])
        h_ref[...] = ((xn * nw_ref[...]) * (1.0 + sc_ref[...]) + sh_ref[...]).astype(h_ref.dtype)

    (h,), carried = _grid_call(
        body, [x, nw, sc, sh], carry, name=name, grid=(rows // ts,),
        in_specs=[_row(ts, d), _vec(d), _vec(d), _vec(d)], out_specs=[_row(ts, d)],
        out_shape=[jax.ShapeDtypeStruct((rows, d), MXU_DT)], dims=("parallel",))
    return h, carried


def _resid_prenorm(x, mix, g1, nw, sc, sh, *, name):
    rows, d = x.shape
    ts = min(rows, ROW_TS)

    def body(x_ref, mix_ref, g_ref, nw_ref, sc_ref, sh_ref, x1_ref, h_ref):
        x1 = x_ref[...] + g_ref[...] * mix_ref[...]
        x1_ref[...] = x1
        _, xn = _rms_stats(x1)
        h_ref[...] = ((xn * nw_ref[...]) * (1.0 + sc_ref[...]) + sh_ref[...]).astype(h_ref.dtype)

    return pl.pallas_call(
        body, name=name, grid=(rows // ts,),
        in_specs=[_row(ts, d), _row(ts, d), _vec(d), _vec(d), _vec(d), _vec(d)],
        out_specs=[_row(ts, d), _row(ts, d)],
        out_shape=[jax.ShapeDtypeStruct((rows, d), F32), jax.ShapeDtypeStruct((rows, d), MXU_DT)],
        compiler_params=_cp("parallel"),
    )(x, mix, g1, nw, sc, sh)


def _loss_head(x1, ffn, g2, target, *, name):
    rows, d = x1.shape
    ts = min(rows, ROW_TS)

    def body(x1_ref, f_ref, g_ref, t_ref, loss_ref, dy_ref, dffn_ref, dg_ref):
        i = pl.program_id(0)
        f = f_ref[...]
        e = x1_ref[...] + g_ref[...] * f - t_ref[...]
        per_tok = jnp.mean(e * e, axis=-1, keepdims=True)
        part = 0.5 * jnp.sum(per_tok, axis=0, keepdims=True)
        dy = e * (1.0 / d)
        dy_ref[...] = dy
        dffn_ref[...] = (dy * g_ref[...]).astype(dffn_ref.dtype)

        @pl.when(i == 0)
        def _():
            loss_ref[...] = jnp.zeros_like(loss_ref)
            dg_ref[...] = jnp.zeros_like(dg_ref)

        loss_ref[...] += jnp.broadcast_to(part, loss_ref.shape)
        dg_ref[...] += _colsum(dy * f)

    return pl.pallas_call(
        body, name=name, grid=(rows // ts,),
        in_specs=[_row(ts, d), _row(ts, d), _vec(d), _row(ts, d)],
        out_specs=[pl.BlockSpec((8, LANES), lambda i: (0, 0)), _row(ts, d), _row(ts, d), _vec(d)],
        out_shape=[jax.ShapeDtypeStruct((8, LANES), F32), jax.ShapeDtypeStruct((rows, d), F32),
                   jax.ShapeDtypeStruct((rows, d), MXU_DT), jax.ShapeDtypeStruct((1, d), F32)],
        compiler_params=_cp("arbitrary"),
    )(x1, ffn, g2, target)


def _norm_bwd(dh, xin, nw, sc, *, res=None, gate=None, want_dx=True, carry=None, name):
    rows, d = xin.shape
    ts = min(rows, ROW_TS)
    has_res, has_gate = res is not None, gate is not None

    def body(*refs):
        it = iter(refs)
        dh_ref, x_ref, nw_ref, sc_ref = next(it), next(it), next(it), next(it)
        res_ref = next(it) if has_res else None
        gated_ref, g_ref = (next(it), next(it)) if has_gate else (None, None)
        dx_ref = next(it) if want_dx else None
        dgx_ref = next(it) if has_gate else None
        dsh_ref, dsc_ref, dnw_ref = next(it), next(it), next(it)
        dg_ref = next(it) if has_gate else None
        i = pl.program_id(0)
        dhv = dh_ref[...]
        r, xn = _rms_stats(x_ref[...])
        dn = dhv * (1.0 + sc_ref[...])

        @pl.when(i == 0)
        def _():
            dsh_ref[...] = jnp.zeros_like(dsh_ref)
            dsc_ref[...] = jnp.zeros_like(dsc_ref)
            dnw_ref[...] = jnp.zeros_like(dnw_ref)
            if has_gate:
                dg_ref[...] = jnp.zeros_like(dg_ref)

        dsh_ref[...] += _colsum(dhv)
        dsc_ref[...] += _colsum(dhv * (xn * nw_ref[...]))
        dnw_ref[...] += _colsum(dn * xn)
        if want_dx:
            dxn = dn * nw_ref[...]
            dx = r * (dxn - xn * jnp.mean(dxn * xn, axis=-1, keepdims=True))
            if has_res:
                dx = dx + res_ref[...]
            dx_ref[...] = dx
            if has_gate:
                dgx_ref[...] = (dx * g_ref[...]).astype(dgx_ref.dtype)
                dg_ref[...] += _colsum(dx * gated_ref[...])

    ins = [dh, xin, nw, sc] + ([res] if has_res else []) + (list(gate) if has_gate else [])
    in_specs = ([_row(ts, d), _row(ts, d), _vec(d), _vec(d)] + ([_row(ts, d)] if has_res else [])
                + ([_row(ts, d), _vec(d)] if has_gate else []))
    out_specs, out_shape = [], []
    if want_dx:
        out_specs.append(_row(ts, d)); out_shape.append(jax.ShapeDtypeStruct((rows, d), F32))
    if has_gate:
        out_specs.append(_row(ts, d)); out_shape.append(jax.ShapeDtypeStruct((rows, d), MXU_DT))
    for _ in range(3 + int(has_gate)):
        out_specs.append(_vec(d)); out_shape.append(jax.ShapeDtypeStruct((1, d), F32))
    return _grid_call(body, ins, carry, name=name, grid=(rows // ts,), in_specs=in_specs, out_specs=out_specs,
                      out_shape=out_shape, dims=("arbitrary",))


def _group_sum(v, g):
    hi = v.astype(MXU_DT)
    lo = (v - hi.astype(F32)).astype(MXU_DT)
    return (jnp.dot(hi, g, preferred_element_type=F32) + jnp.dot(lo, g, preferred_element_type=F32))


def _rot(v):
    lane = lax.broadcasted_iota(jnp.int32, v.shape, 1)
    first = (lane & 31) < 16
    return jnp.where(first, -pltpu.roll(v, LANES - 16, 1), pltpu.roll(v, 16, 1))


def _head_group_matrix():
    r = jnp.arange(LANES) // HEAD_DIM
    return (r[:, None] == r[None, :]).astype(MXU_DT)


def _qk_prep(xin, width, cos, sin, w, *, name):
    rows = xin.shape[0]
    ts = min(rows, 256)
    nch = width // LANES

    def body(x_ref, cos_ref, sin_ref, w_ref, g_ref, o_ref):
        cs, sn, g = cos_ref[...], sin_ref[...], g_ref[...]
        for ch in range(nch):
            sl = slice(ch * LANES, (ch + 1) * LANES)
            xv = x_ref[:, sl]
            r = lax.rsqrt(_group_sum(xv * xv, g) * (1.0 / HEAD_DIM) + EPS)
            yw = (xv * r) * w_ref[:, sl]
            o_ref[:, sl] = (yw * cs + _rot(yw) * sn).astype(o_ref.dtype)

    return pl.pallas_call(
        body, name=name, grid=(rows // ts,),
        in_specs=[_row(ts, width), _row(ts, LANES), _row(ts, LANES), _vec(width),
                  pl.BlockSpec((LANES, LANES), lambda i: (0, 0))],
        out_specs=_row(ts, width),
        out_shape=jax.ShapeDtypeStruct((rows, width), MXU_DT), compiler_params=_cp("parallel"),
    )(xin, cos, sin, w, _head_group_matrix())


def _qk_prep_bwd(xin, douts, width, cos, sin, w, *, tail=None, name):
    rows = xin.shape[0]
    ts = min(rows, 256)
    nch = width // LANES
    has_tail = tail is not None
    nd = len(douts)
    assert sum(v.shape[1] for v in douts) == width
    src = [(k, c) for k, v in enumerate(douts) for c in range(v.shape[1] // LANES)]

    def body(*refs):
        x_ref, d_refs = refs[0], refs[1:1 + nd]
        t_ref = refs[1 + nd] if has_tail else None
        cos_ref, sin_ref, w_ref, g_ref, dx_ref, dw_ref = refs[1 + nd + int(has_tail):]
        i = pl.program_id(0)
        cs, sn, g = cos_ref[...], sin_ref[...], g_ref[...]

        @pl.when(i == 0)
        def _():
            dw_ref[...] = jnp.zeros_like(dw_ref)

        if has_tail:
            dx_ref[:, width:width + LANES] = t_ref[...].astype(dx_ref.dtype)
            dx_ref[:, width + LANES:] = jnp.zeros((ts, GLU_OFF - width - LANES), dx_ref.dtype)

        for ch in range(nch):
            sl = slice(ch * LANES, (ch + 1) * LANES)
            xv = x_ref[:, sl]
            dv = d_refs[src[ch][0]][:, src[ch][1] * LANES:(src[ch][1] + 1) * LANES].astype(F32)
            r = lax.rsqrt(_group_sum(xv * xv, g) * (1.0 / HEAD_DIM) + EPS)
            n = xv * r
            dyw = dv * cs - _rot(dv * sn)
            dw_ref[:, sl] += _colsum(dyw * n)
            dn = dyw * w_ref[:, sl]
            gm = _group_sum(dn * n, g) * (1.0 / HEAD_DIM)
            dx_ref[:, sl] = (r * (dn - n * gm)).astype(dx_ref.dtype)

    ins = [xin] + list(douts) + ([tail] if has_tail else []) + [cos, sin, w, _head_group_matrix()]
    in_specs = ([_row(ts, width)] + [_row(ts, v.shape[1]) for v in douts] + ([_row(ts, LANES)] if has_tail else [])
                + [_row(ts, LANES), _row(ts, LANES), _vec(width), pl.BlockSpec((LANES, LANES), lambda i: (0, 0))])
    out_w, arr_w = (GLU_OFF, IN_PAD) if has_tail else (width, width)
    return pl.pallas_call(
        body, name=name, grid=(rows // ts,), in_specs=in_specs,
        out_specs=[_row(ts, out_w), _vec(width)],
        out_shape=[jax.ShapeDtypeStruct((rows, arr_w), MXU_DT), jax.ShapeDtypeStruct((1, width), F32)],
        compiler_params=_cp("arbitrary"),
    )(*ins)


GB = GQA_GROUP * BLOCK
WIN = 3 * BLOCK
ATT_STEP = 4


def _win_start(i, seq):
    return pl.multiple_of(jnp.clip((i - 1) * BLOCK, 0, seq - WIN), BLOCK)


def _attn_mask(i, start):
    qpos = i * BLOCK + (lax.broadcasted_iota(jnp.int32, (GB, WIN), 0) & (BLOCK - 1))
    kpos = start + lax.broadcasted_iota(jnp.int32, (GB, WIN), 1)
    return jnp.abs(qpos - kpos) <= WINDOW


def _sink_col(sink_ref, kv):
    return jnp.concatenate(
        [jnp.full((BLOCK, 1), sink_ref[0, kv * GQA_GROUP + g], F32) for g in range(GQA_GROUP)], axis=0)


def _stack_slots(ref, rows):
    return jnp.concatenate([ref[rows, g * LANES:(g + 1) * LANES] for g in range(GQA_GROUP)], axis=0)


def _kv_lanes(kv):
    lane = lax.broadcasted_iota(jnp.int32, (1, LANES), 1)
    return (lane < HEAD_DIM) if kv == 0 else (lane >= HEAD_DIM)


def _keep(mask, v):
    return jnp.where(mask, v, jnp.zeros_like(v))


_NT = (((1,), (1,)), ((), ()))
_TN = (((0,), (0,)), ((), ()))


def _attn_specs(seq, n_ctx):
    qs = pl.BlockSpec((ATT_STEP * BLOCK, Q_COLS), lambda i: (i, 0))
    ks = pl.BlockSpec((seq, KV_COLS), lambda i: (0, Q_COLS // KV_COLS))
    vs = pl.BlockSpec((seq, KV_COLS), lambda i: (0, QK_COLS // KV_COLS))
    kcs = pl.BlockSpec((n_ctx, KV_COLS), lambda i: (0, 0))
    vcs = pl.BlockSpec((n_ctx, KV_COLS), lambda i: (0, 1))
    ls = pl.BlockSpec((ATT_STEP, N_KV_HEADS * GB, 1), lambda i: (i, 0, 0))
    return qs, ks, vs, kcs, vcs, ls


def _attn_fwd(sink, qk, proj, kc, kv_ctx, *, carry=None, name):
    seq, n_ctx = qk.shape[0], kc.shape[0]
    scale = 1.0 / math.sqrt(HEAD_DIM)

    def one_block(blk, sub, sink_ref, q_ref, k_ref, v_ref, kcw, vcw, o_ref, lse_ref):
        rows = slice(sub * BLOCK, (sub + 1) * BLOCK)
        start = _win_start(blk, seq)
        valid = _attn_mask(blk, start)
        qs = _stack_slots(q_ref, rows)
        kw, vw = k_ref[pl.ds(start, WIN), :], v_ref[pl.ds(start, WIN), :].astype(MXU_DT)
        o_all = jnp.zeros((GB, LANES), F32)
        for kv in range(N_KV_HEADS):
            mine = _kv_lanes(kv)
            s_loc = lax.dot_general(qs, _keep(mine, kw), _NT, preferred_element_type=F32) * scale
            s_loc = jnp.where(valid, s_loc, NEG_INF)
            s_ctx = lax.dot_general(qs, _keep(mine, kcw), _NT, preferred_element_type=F32) * scale
            sk = _sink_col(sink_ref, kv)
            m = jnp.maximum(jnp.maximum(jnp.max(s_loc, axis=-1, keepdims=True),
                                        jnp.max(s_ctx, axis=-1, keepdims=True)), sk)
            p_loc = jnp.exp(s_loc - m)
            p_ctx = jnp.exp(s_ctx - m)
            l = (jnp.sum(p_loc, axis=-1, keepdims=True) + jnp.sum(p_ctx, axis=-1, keepdims=True)
                 + jnp.exp(sk - m))
            o_all = o_all + (jnp.dot(p_loc.astype(MXU_DT), _keep(mine, vw), preferred_element_type=F32)
                             + jnp.dot(p_ctx.astype(MXU_DT), _keep(mine, vcw), preferred_element_type=F32)) / l
            lse_ref[sub, kv * GB:(kv + 1) * GB, :] = m + jnp.log(l)
        for g in range(GQA_GROUP):
            o_ref[rows, g * LANES:(g + 1) * LANES] = o_all[g * BLOCK:(g + 1) * BLOCK].astype(o_ref.dtype)

    def body(sink_ref, q_ref, k_ref, v_ref, kc_ref, vc_ref, o_ref, lse_ref):
        i = pl.program_id(0)
        kcw, vcw = kc_ref[...], vc_ref[...].astype(MXU_DT)
        for sub in range(ATT_STEP):
            one_block(i * ATT_STEP + sub, sub, sink_ref, q_ref, k_ref, v_ref, kcw, vcw, o_ref, lse_ref)

    qs, ks, vs, kcs, vcs, ls = _attn_specs(seq, n_ctx)
    return _grid_call(
        body, [sink, qk, qk, proj, kc, kv_ctx], carry, name=name, grid=(seq // (ATT_STEP * BLOCK),),
        in_specs=[pl.BlockSpec(memory_space=pltpu.SMEM), qs, ks, vs, kcs, vcs],
        out_specs=[qs, ls],
        out_shape=[jax.ShapeDtypeStruct((seq, Q_COLS), MXU_DT),
                   jax.ShapeDtypeStruct((seq // BLOCK, N_KV_HEADS * GB, 1), F32)],
        dims=("parallel",))


def _attn_bwd(sink, qk, proj, kc, kv_ctx, o, lse, do, *, carry=None, name):
    seq, n_ctx = qk.shape[0], kc.shape[0]
    scale = 1.0 / math.sqrt(HEAD_DIM)

    def body(sink_ref, q_ref, k_ref, v_ref, kc_ref, vc_ref, o_ref, lse_ref, do_ref,
             dq_ref, dk_ref, dv_ref, dkc_ref, dvc_ref, ds_ref):
        i = pl.program_id(0)

        @pl.when(i == 0)
        def _():
            dk_ref[...] = jnp.zeros_like(dk_ref)
            dv_ref[...] = jnp.zeros_like(dv_ref)
            dkc_ref[...] = jnp.zeros_like(dkc_ref)
            dvc_ref[...] = jnp.zeros_like(dvc_ref)
            ds_ref[...] = jnp.zeros_like(ds_ref)

        kcw, vcw = kc_ref[...], vc_ref[...].astype(MXU_DT)
        dkc, dvc = jnp.zeros((n_ctx, LANES), F32), jnp.zeros((n_ctx, LANES), F32)
        for sub in range(ATT_STEP):
            dkc_s, dvc_s = one_block(i * ATT_STEP + sub, sub, sink_ref, q_ref, k_ref, v_ref, kcw, vcw, o_ref,
                                     lse_ref, do_ref, dq_ref, dk_ref, dv_ref, ds_ref)
            dkc, dvc = dkc + dkc_s, dvc + dvc_s
        dkc_ref[...] += dkc
        dvc_ref[...] += dvc

    def one_block(blk, sub, sink_ref, q_ref, k_ref, v_ref, kcw, vcw, o_ref, lse_ref, do_ref,
                  dq_ref, dk_ref, dv_ref, ds_ref):
        qrows = slice(sub * BLOCK, (sub + 1) * BLOCK)
        start = _win_start(blk, seq)
        valid = _attn_mask(blk, start)
        win = pl.ds(start, WIN)
        qs, dos = _stack_slots(q_ref, qrows), _stack_slots(do_ref, qrows)
        do_o = dos.astype(F32) * _stack_slots(o_ref, qrows).astype(F32)
        kw, vw = k_ref[win, :], v_ref[win, :].astype(MXU_DT)
        dq = jnp.zeros((GB, LANES), F32)
        dk, dv = jnp.zeros((WIN, LANES), F32), jnp.zeros((WIN, LANES), F32)
        dkc, dvc = jnp.zeros((n_ctx, LANES), F32), jnp.zeros((n_ctx, LANES), F32)
        for kv in range(N_KV_HEADS):
            mine = _kv_lanes(kv)
            rows = slice(kv * GB, (kv + 1) * GB)
            lse_s = lse_ref[sub, rows, :]
            delta = jnp.sum(_keep(mine, do_o), axis=-1, keepdims=True)
            kz, vz, kcz, vcz = _keep(mine, kw), _keep(mine, vw), _keep(mine, kcw), _keep(mine, vcw)
            s_loc = lax.dot_general(qs, kz, _NT, preferred_element_type=F32) * scale
            s_loc = jnp.where(valid, s_loc, NEG_INF)
            s_ctx = lax.dot_general(qs, kcz, _NT, preferred_element_type=F32) * scale
            p_loc = jnp.exp(s_loc - lse_s)
            p_ctx = jnp.exp(s_ctx - lse_s)
            p_sink = jnp.exp(_sink_col(sink_ref, kv) - lse_s)
            dp_loc = lax.dot_general(dos, vz, _NT, preferred_element_type=F32)
            dp_ctx = lax.dot_general(dos, vcz, _NT, preferred_element_type=F32)
            ds_loc = (p_loc * (dp_loc - delta) * scale).astype(MXU_DT)
            ds_ctx = (p_ctx * (dp_ctx - delta) * scale).astype(MXU_DT)
            dq = dq + (jnp.dot(ds_loc, kz, preferred_element_type=F32)
                       + jnp.dot(ds_ctx, kcz, preferred_element_type=F32))
            dk = dk + _keep(mine, lax.dot_general(ds_loc, qs, _TN, preferred_element_type=F32))
            dv = dv + _keep(mine, lax.dot_general(p_loc.astype(MXU_DT), dos, _TN, preferred_element_type=F32))
            dkc = dkc + _keep(mine, lax.dot_general(ds_ctx, qs, _TN, preferred_element_type=F32))
            dvc = dvc + _keep(mine, lax.dot_general(p_ctx.astype(MXU_DT), dos, _TN, preferred_element_type=F32))
            ds_ref[rows, :] += -(p_sink * delta)
        for g in range(GQA_GROUP):
            dq_ref[qrows, g * LANES:(g + 1) * LANES] = dq[g * BLOCK:(g + 1) * BLOCK]
        dk_ref[win, :] += dk
        dv_ref[win, :] += dv
        return dkc, dvc

    qs, ks, vs, kcs, vcs, ls = _attn_specs(seq, n_ctx)
    whole = lambda r, c: pl.BlockSpec((r, c), lambda i: (0, 0))
    return _grid_call(
        body, [sink, qk, qk, proj, kc, kv_ctx, o, lse, do], carry, name=name, grid=(seq // (ATT_STEP * BLOCK),),
        in_specs=[pl.BlockSpec(memory_space=pltpu.SMEM), qs, ks, vs, kcs, vcs, qs, ls, qs],
        out_specs=[qs, whole(seq, KV_COLS), whole(seq, KV_COLS), whole(n_ctx, KV_COLS), whole(n_ctx, KV_COLS),
                   whole(N_KV_HEADS * GB, 1)],
        out_shape=[jax.ShapeDtypeStruct((seq, Q_COLS), F32), jax.ShapeDtypeStruct((seq, KV_COLS), F32),
                   jax.ShapeDtypeStruct((seq, KV_COLS), F32), jax.ShapeDtypeStruct((n_ctx, KV_COLS), F32),
                   jax.ShapeDtypeStruct((n_ctx, KV_COLS), F32), jax.ShapeDtypeStruct((N_KV_HEADS * GB, 1), F32)],
        dims=("arbitrary",))


def _halo_specs(ts, w, rows, col=0):
    per = ts // HALO
    last = rows // HALO - 1
    return [pl.BlockSpec((HALO, w), lambda i: (jnp.maximum(i * per - 1, 0), col)),
            pl.BlockSpec((ts, w), lambda i: (i, col)),
            pl.BlockSpec((HALO, w), lambda i: (jnp.minimum((i + 1) * per, last), col))]


def _glu(v):
    return v[:, :CONV_CH] * _sigmoid(v[:, CONV_CH:])


def _ln_stats(u):
    mu = jnp.mean(u, axis=-1, keepdims=True)
    xc = u - mu
    rstd = lax.rsqrt(jnp.mean(xc * xc, axis=-1, keepdims=True) + EPS)
    return xc * rstd, rstd


CONV_BWD_TS = 128


def _phases(ext_ref, ph_ref):
    n = ph_ref.shape[1]
    for b in range(1, 8):
        ph_ref[b - 1] = ext_ref[b:b + n, :]


def _window(ext_ref, ph_ref, o, n, cs):
    a, b = divmod(o, 8)
    src = ext_ref if b == 0 else ph_ref.at[b - 1]
    return src[8 * a:8 * a + n, cs]


def _conv_fwd(glu, col, cw, cb, lw, lb, *, carry=None, name):
    rows = glu.shape[0]
    ts = min(rows, 256)
    nt = rows // ts

    te = ts + 2 * HALO

    def body(gp_ref, g_ref, gn_ref, cw_ref, cb_ref, lw_ref, lb_ref, u3_ref, u1_ref, ext_ref, ph_ref):
        i = pl.program_id(0)
        ext_ref[0:HALO, :] = jnp.where(i > 0, _glu(gp_ref[...]), 0.0)
        ext_ref[HALO:HALO + ts, :] = _glu(g_ref[...])
        ext_ref[HALO + ts:, :] = jnp.where(i < nt - 1, _glu(gn_ref[...]), 0.0)
        _phases(ext_ref, ph_ref)
        for c in range(CONV_CH // LANES):
            cs = slice(c * LANES, (c + 1) * LANES)
            acc = jnp.broadcast_to(cb_ref[:, cs], (ts, LANES))
            for j in range(CONV_K):
                acc = acc + cw_ref[j:j + 1, cs] * _window(ext_ref, ph_ref, HALO - CONV_PAD + j, ts, cs)
            u1_ref[:, cs] = acc
        xh, _ = _ln_stats(u1_ref[...])
        u2 = xh * lw_ref[...] + lb_ref[...]
        u3_ref[...] = (u2 * _sigmoid(u2)).astype(u3_ref.dtype)

    full = lambda shape: pl.BlockSpec(shape, lambda i: (0,) * len(shape))
    return _grid_call(
        body, [glu, glu, glu, cw, cb, lw, lb], carry, name=name, grid=(nt,),
        in_specs=_halo_specs(ts, GLU_COLS, rows, col) + [full((CONV_K, CONV_CH))] + [_vec(CONV_CH)] * 3,
        out_specs=[_row(ts, CONV_CH), _row(ts, CONV_CH)],
        out_shape=[jax.ShapeDtypeStruct((rows, CONV_CH), MXU_DT), jax.ShapeDtypeStruct((rows, CONV_CH), F32)],
        scratch_shapes=[pltpu.VMEM((te, CONV_CH), F32), pltpu.VMEM((7, te - 8, CONV_CH), F32)],
        dims=("parallel",))


def _conv_bwd(glu, col, u1, du3, dproj, cw, lw, lb, *, carry=None, name):
    rows = glu.shape[0]
    ts = min(rows, CONV_BWD_TS)
    nt = rows // ts
    te = ts + 2 * HALO

    def du1_of(u1v, du3v, lw_v, lb_v):
        xh, rstd = _ln_stats(u1v)
        u2 = xh * lw_v + lb_v
        sg = _sigmoid(u2)
        du2 = du3v * (sg * (1.0 + u2 * (1.0 - sg)))
        dxh = du2 * lw_v
        du1 = rstd * (dxh - jnp.mean(dxh, axis=-1, keepdims=True)
                      - xh * jnp.mean(dxh * xh, axis=-1, keepdims=True))
        return du1, du2, xh

    half = ts // 2

    def body(gp_ref, g_ref, gn_ref, up_ref, u_ref, un_ref, dp_ref, d_ref, dn_ref, cw_ref, lw_ref, lb_ref,
             _, dglu_ref, dcw_ref, dvec_ref, u0_ref, du1_ref, pu_ref, pd_ref, du0_ref):
        i = pl.program_id(0)
        lw_v, lb_v = lw_ref[...], lb_ref[...]

        @pl.when(i == 0)
        def _():
            dcw_ref[...] = jnp.zeros_like(dcw_ref)
            dvec_ref[...] = jnp.zeros_like(dvec_ref)

        gv = g_ref[...]
        u0_ref[0:HALO, :] = jnp.where(i > 0, _glu(gp_ref[...]), 0.0)
        u0_ref[HALO:HALO + ts, :] = _glu(gv)
        u0_ref[HALO + ts:, :] = jnp.where(i < nt - 1, _glu(gn_ref[...]), 0.0)
        d_prev, _, _ = du1_of(up_ref[...], dp_ref[...], lw_v, lb_v)
        d_main, du2, xh = du1_of(u_ref[...], d_ref[...], lw_v, lb_v)
        d_next, _, _ = du1_of(un_ref[...], dn_ref[...], lw_v, lb_v)
        du1_ref[0:HALO, :] = jnp.where(i > 0, d_prev, 0.0)
        du1_ref[HALO:HALO + ts, :] = d_main
        du1_ref[HALO + ts:, :] = jnp.where(i < nt - 1, d_next, 0.0)

        rid = lax.broadcasted_iota(jnp.int32, (8, CONV_CH), 0)
        dvec_ref[...] += (jnp.where(rid == 0, _colsum(d_main), 0.0)
                          + jnp.where(rid == 1, _colsum(du2 * xh), 0.0)
                          + jnp.where(rid == 2, _colsum(du2), 0.0))
        _phases(u0_ref, pu_ref)
        _phases(du1_ref, pd_ref)
        for c in range(CONV_CH // LANES):
            cs = slice(c * LANES, (c + 1) * LANES)
            for r0 in (0, half):
                dm = du1_ref[HALO + r0:HALO + r0 + half, cs]
                acc = jnp.zeros((half, LANES), F32)
                for j in range(CONV_K):
                    acc = acc + cw_ref[j:j + 1, cs] * _window(du1_ref, pd_ref, r0 + HALO + CONV_PAD - j, half, cs)
                    prod = dm * _window(u0_ref, pu_ref, r0 + HALO - CONV_PAD + j, half, cs)
                    dcw_ref[j, :, cs] += jnp.sum(prod.reshape(half // 8, 8, LANES), axis=0)
                du0_ref[r0:r0 + half, cs] = acc
        du0 = du0_ref[...]
        ga, sg = gv[:, :CONV_CH], _sigmoid(gv[:, CONV_CH:])
        dglu_ref[:, :CONV_CH] = (du0 * sg).astype(dglu_ref.dtype)
        dglu_ref[:, CONV_CH:] = (du0 * ga * sg * (1.0 - sg)).astype(dglu_ref.dtype)

    full = lambda shape: pl.BlockSpec(shape, lambda i: (0,) * len(shape))
    return _grid_call(
        body, [glu, glu, glu, u1, u1, u1, du3, du3, du3, cw, lw, lb, dproj], carry, name=name, grid=(nt,),
        in_specs=(_halo_specs(ts, GLU_COLS, rows, col) + _halo_specs(ts, CONV_CH, rows)
                  + _halo_specs(ts, CONV_CH, rows) + [full((CONV_K, CONV_CH)), _vec(CONV_CH), _vec(CONV_CH)]
                  + [pl.BlockSpec(memory_space=pl.ANY)]),
        out_specs=[_row(ts, GLU_COLS, col), full((CONV_K, 8, CONV_CH)), full((8, CONV_CH))],
        out_shape=[jax.ShapeDtypeStruct(dproj.shape, dproj.dtype),
                   jax.ShapeDtypeStruct((CONV_K, 8, CONV_CH), F32), jax.ShapeDtypeStruct((8, CONV_CH), F32)],
        scratch_shapes=[pltpu.VMEM((te, CONV_CH), F32), pltpu.VMEM((te, CONV_CH), F32),
                        pltpu.VMEM((7, te - 8, CONV_CH), F32), pltpu.VMEM((7, te - 8, CONV_CH), F32),
                        pltpu.VMEM((ts, CONV_CH), F32)],
        input_output_aliases={12: 0}, dims=("arbitrary",))


FFN_CW = 1408
FFN_NJ = FFN_H // FFN_CW


def _ffn_halo_specs(ts, rows, col_of, inner_rows):
    per = ts // HALO
    last = rows // HALO - 1
    if inner_rows:
        return [pl.BlockSpec((HALO, FFN_CW), lambda j, i: (jnp.maximum(i * per - 1, 0), col_of(j))),
                pl.BlockSpec((ts, FFN_CW), lambda j, i: (i, col_of(j))),
                pl.BlockSpec((HALO, FFN_CW), lambda j, i: (jnp.minimum((i + 1) * per, last), col_of(j)))]
    return [pl.BlockSpec((HALO, FFN_CW), lambda i, j: (jnp.maximum(i * per - 1, 0), col_of(j))),
            pl.BlockSpec((ts, FFN_CW), lambda i, j: (i, col_of(j))),
            pl.BlockSpec((HALO, FFN_CW), lambda i, j: (jnp.minimum((i + 1) * per, last), col_of(j)))]


def _ffn_ext(p_ref, m_ref, n_ref, sl, i, nt):
    return jnp.concatenate([jnp.where(i > 0, p_ref[:, sl].astype(F32), 0.0), m_ref[:, sl].astype(F32),
                            jnp.where(i < nt - 1, n_ref[:, sl].astype(F32), 0.0)], axis=0)


def _prev_row(v):
    return pltpu.roll(v, 1, 0)


def _next_row(v):
    return pltpu.roll(v, v.shape[0] - 1, 0)


def _ffn_act(up0, w3, b3, *, name):
    rows = up0.shape[0]
    ts = min(rows, 256)
    nt = rows // ts
    main = slice(HALO, HALO + ts)

    def body(gp, g, gn, vp, v, vn, wg, wv, bg, bv, a_ref, go_ref, vo_ref):
        i = pl.program_id(0)
        for ch in range(FFN_CW // LANES):
            sl = slice(ch * LANES, (ch + 1) * LANES)
            xg, xv = _ffn_ext(gp, g, gn, sl, i, nt), _ffn_ext(vp, v, vn, sl, i, nt)
            wgv, wvv = wg[:, sl], wv[:, sl]
            gate = (wgv[0:1] * _prev_row(xg) + wgv[1:2] * xg + wgv[2:3] * _next_row(xg))[main] + bg[:, sl]
            val = (wvv[0:1] * _prev_row(xv) + wvv[1:2] * xv + wvv[2:3] * _next_row(xv))[main] + bv[:, sl]
            a_ref[:, sl] = (gate * _sigmoid(gate) * val).astype(a_ref.dtype)
            go_ref[:, sl] = gate.astype(go_ref.dtype)
            vo_ref[:, sl] = val.astype(vo_ref.dtype)

    gcol, vcol = (lambda j: j), (lambda j: j + FFN_NJ)
    wspec = lambda col_of: pl.BlockSpec((FFN_K, FFN_CW), lambda i, j: (0, col_of(j)))
    bspec = lambda col_of: pl.BlockSpec((1, FFN_CW), lambda i, j: (0, col_of(j)))
    ospec = pl.BlockSpec((ts, FFN_CW), lambda i, j: (i, j))
    return pl.pallas_call(
        body, name=name, grid=(nt, FFN_NJ),
        in_specs=(_ffn_halo_specs(ts, rows, gcol, False) + _ffn_halo_specs(ts, rows, vcol, False)
                  + [wspec(gcol), wspec(vcol), bspec(gcol), bspec(vcol)]),
        out_specs=[ospec] * 3, out_shape=[jax.ShapeDtypeStruct((rows, FFN_H), MXU_DT)] * 3,
        compiler_params=_cp("parallel", "parallel"),
    )(up0, up0, up0, up0, up0, up0, w3, w3, b3, b3)


def _ffn_act_bwd(up0, gate_s, val_s, da, w3, *, carry=None, name):
    rows = up0.shape[0]
    ts = min(rows, 256)
    nt = rows // ts
    main = slice(HALO, HALO + ts)

    def body(gp, g, gn, vp, v, vn, sgp, sg, sgn, svp, sv, svn, ap, a, an, wg, wv,
             dg_ref, dv_ref, sg_ref, sv_ref):
        i = pl.program_id(1)

        @pl.when(i == 0)
        def _():
            sg_ref[...] = jnp.zeros_like(sg_ref)
            sv_ref[...] = jnp.zeros_like(sv_ref)

        rid = lax.broadcasted_iota(jnp.int32, (8, LANES), 0)
        for ch in range(FFN_CW // LANES):
            sl = slice(ch * LANES, (ch + 1) * LANES)
            xg, xv, da_e = _ffn_ext(gp, g, gn, sl, i, nt), _ffn_ext(vp, v, vn, sl, i, nt), _ffn_ext(ap, a, an, sl, i, nt)
            gate, val = _ffn_ext(sgp, sg, sgn, sl, i, nt), _ffn_ext(svp, sv, svn, sl, i, nt)
            wgv, wvv = wg[:, sl], wv[:, sl]
            xg_p, xg_n, xv_p, xv_n = _prev_row(xg), _next_row(xg), _prev_row(xv), _next_row(xv)
            sgm = _sigmoid(gate)
            eg = da_e * val * (sgm * (1.0 + gate * (1.0 - sgm)))
            ev = da_e * (gate * sgm)
            for e, taps, w, d_ref, s_ref in ((eg, (xg_p, xg, xg_n), wgv, dg_ref, sg_ref),
                                             (ev, (xv_p, xv, xv_n), wvv, dv_ref, sv_ref)):
                d0 = w[0:1] * _next_row(e) + w[1:2] * e + w[2:3] * _prev_row(e)
                d_ref[:, sl] = d0[main].astype(d_ref.dtype)
                dm = e[main]
                s_ref[:, sl] += (jnp.where(rid == 0, _colsum(dm * taps[0][main]), 0.0)
                                 + jnp.where(rid == 1, _colsum(dm * taps[1][main]), 0.0)
                                 + jnp.where(rid == 2, _colsum(dm * taps[2][main]), 0.0)
                                 + jnp.where(rid == 3, _colsum(dm), 0.0))

    gcol, vcol = (lambda j: j), (lambda j: j + FFN_NJ)
    wspec = lambda col_of: pl.BlockSpec((FFN_K, FFN_CW), lambda j, i: (0, col_of(j)))
    ospec = pl.BlockSpec((ts, FFN_CW), lambda j, i: (i, j))
    sspec = pl.BlockSpec((8, FFN_CW), lambda j, i: (0, j))
    return _grid_call(
        body, [up0] * 6 + [gate_s] * 3 + [val_s] * 3 + [da] * 3 + [w3, w3], carry, name=name, grid=(FFN_NJ, nt),
        in_specs=(_ffn_halo_specs(ts, rows, gcol, True) + _ffn_halo_specs(ts, rows, vcol, True)
                  + _ffn_halo_specs(ts, rows, gcol, True) * 3 + [wspec(gcol), wspec(vcol)]),
        out_specs=[ospec, ospec, sspec, sspec],
        out_shape=[jax.ShapeDtypeStruct((rows, FFN_H), MXU_DT), jax.ShapeDtypeStruct((rows, FFN_H), MXU_DT),
                   jax.ShapeDtypeStruct((8, FFN_H), F32), jax.ShapeDtypeStruct((8, FFN_H), F32)],
        dims=("parallel", "arbitrary"))


def _adam_math(w, g, m, v):
    m = ADAM_B1 * m + (1.0 - ADAM_B1) * g
    v = ADAM_B2 * v + (1.0 - ADAM_B2) * (g * g)
    m_hat = m / (1.0 - ADAM_B1 ** ADAM_STEP)
    v_hat = v / (1.0 - ADAM_B2 ** ADAM_STEP)
    delta = -ADAM_LR * (m_hat / (jnp.sqrt(v_hat) + ADAM_EPS) + ADAM_WD * w)
    return delta, m, v


ROW_TILE_BYTES = 8 << 20


def _row_tile(rows, row_bytes):
    tiles = [rows] + [rows // k for k in range(2, rows // 16 + 1) if rows % k == 0 and (rows // k) % 16 == 0]
    return next(t for t in tiles if t * row_bytes <= ROW_TILE_BYTES)


def _adam(w, m, v, parts, *, name):
    rows, cols = w.shape
    nparts = parts.shape[0]
    tr = _row_tile(rows, cols * (7 * 4 + nparts * parts.dtype.itemsize))

    def body(w_ref, m_ref, v_ref, p_ref, g_ref, d_ref, nm_ref, nv_ref):
        g = p_ref[0].astype(F32)
        for p in range(1, nparts):
            g = g + p_ref[p].astype(F32)
        g_ref[...] = g
        d_ref[...], nm_ref[...], nv_ref[...] = _adam_math(w_ref[...], g, m_ref[...], v_ref[...])

    spec = _row(tr, cols)
    return pl.pallas_call(
        body, name=name, grid=(rows // tr,),
        in_specs=[spec, spec, spec, pl.BlockSpec((nparts, tr, cols), lambda i: (0, i, 0))],
        out_specs=[spec] * 4, out_shape=[jax.ShapeDtypeStruct((rows, cols), F32)] * 4,
        compiler_params=_cp("parallel"),
    )(w, m, v, parts)


def _adam_many(ws, ms, vs, gs, *, name):
    n = len(ws)

    def body(*refs):
        ins, outs = refs[:4 * n], refs[4 * n:]
        for k in range(n):
            delta, new_m, new_v = _adam_math(ins[k][...], ins[3 * n + k][...], ins[n + k][...], ins[2 * n + k][...])
            outs[k][...], outs[n + k][...], outs[2 * n + k][...] = delta, new_m, new_v

    vm = pl.BlockSpec(memory_space=pltpu.VMEM)
    res = pl.pallas_call(
        body, name=name, in_specs=[vm] * (4 * n), out_specs=[vm] * (3 * n),
        out_shape=[jax.ShapeDtypeStruct(w.shape, F32) for w in ws] * 3,
        compiler_params=pltpu.CompilerParams(vmem_limit_bytes=VMEM_LIMIT),
    )(*ws, *ms, *vs, *gs)
    return res[:n], res[n:2 * n], res[2 * n:]


def _sum_parts(parts, *, name):
    nparts, rows, cols = parts.shape
    tr = _row_tile(rows, cols * (4 + nparts * parts.dtype.itemsize))

    def body(p_ref, o_ref):
        g = p_ref[0].astype(F32)
        for p in range(1, nparts):
            g = g + p_ref[p].astype(F32)
        o_ref[...] = g

    return pl.pallas_call(
        body, name=name, grid=(rows // tr,),
        in_specs=[pl.BlockSpec((nparts, tr, cols), lambda i: (0, i, 0))], out_specs=_row(tr, cols),
        out_shape=jax.ShapeDtypeStruct((rows, cols), F32), compiler_params=_cp("parallel"),
    )(parts)


def _my_place():
    return lax.axis_index("x"), lax.axis_index("y"), lax.axis_index("c")


def _dev_index(p):
    return 4 * p[0] + 2 * p[1] + p[2]


def _all_gather(xs, *, name):
    return _run_comm(_gather_plan(xs), pltpu.VMEM, name)


class _Comm(NamedTuple):
    ins: list
    outs: list
    n_remote: int
    n_local: int
    start: Callable
    finish: Callable


def _join_plans(*plans):
    def split(in_refs, out_refs, send_sems, recv_sems, local_sems):
        i = o = r = l = 0
        for p in plans:
            ni, no = len(p.ins), len(p.outs)
            yield p, (in_refs[i:i + ni], out_refs[o:o + no], send_sems.at[pl.ds(r, p.n_remote)],
                      recv_sems.at[pl.ds(r, p.n_remote)], local_sems.at[pl.ds(l, p.n_local)])
            i, o, r, l = i + ni, o + no, r + p.n_remote, l + p.n_local

    def start(*refs):
        for p, part in split(*refs):
            p.start(*part)

    def finish(*refs):
        for p, part in split(*refs):
            p.finish(*part)

    return _Comm([v for p in plans for v in p.ins], [v for p in plans for v in p.outs],
                 sum(p.n_remote for p in plans), sum(p.n_local for p in plans), start, finish)


def _comm_scratch(plan):
    return [pltpu.SemaphoreType.DMA((plan.n_remote,)), pltpu.SemaphoreType.DMA((plan.n_remote,)),
            pltpu.SemaphoreType.DMA((plan.n_local,))]


def _run_comm(plan, space, name):
    n_in, n_out = len(plan.ins), len(plan.outs)

    def body(*refs):
        args = (refs[:n_in], refs[n_in:n_in + n_out], *refs[n_in + n_out:])
        plan.start(*args)
        plan.finish(*args)

    return pl.pallas_call(
        body, name=name, out_shape=plan.outs,
        in_specs=[pl.BlockSpec(memory_space=space)] * n_in, out_specs=[pl.BlockSpec(memory_space=space)] * n_out,
        scratch_shapes=_comm_scratch(plan),
        compiler_params=pltpu.CompilerParams(vmem_limit_bytes=VMEM_LIMIT),
    )(*plan.ins)


def _grid_call(body, ins, carry, *, name, grid, in_specs, out_specs, out_shape, dims, scratch_shapes=(),
               input_output_aliases=None):
    if carry is None:
        res = pl.pallas_call(
            body, name=name, grid=grid, in_specs=list(in_specs), out_specs=list(out_specs),
            out_shape=list(out_shape), scratch_shapes=list(scratch_shapes),
            input_output_aliases=input_output_aliases or {}, compiler_params=_cp(*dims))(*ins)
        return list(res), None

    def at(pos):
        conds = [pl.program_id(k) == p for k, p in enumerate(pos)]
        out = conds[0]
        for cnd in conds[1:]:
            out = jnp.logical_and(out, cnd)
        return out

    return _carried_call(body, carry, lambda: at([0] * len(grid)), lambda: at([g - 1 for g in grid]), ins,
                         name=name, grid=grid, in_specs=in_specs, out_specs=out_specs, out_shape=out_shape,
                         scratch_shapes=scratch_shapes, input_output_aliases=input_output_aliases)


def _carried_call(body, plan, first, last, ins, *, name, grid, in_specs, out_specs, out_shape, scratch_shapes=(),
                  input_output_aliases=None):
    in_specs, out_specs, out_shape = list(in_specs), list(out_specs), list(out_shape)
    n_in, n_out, n_scr = len(in_specs), len(out_specs), len(scratch_shapes)
    c_in, c_out = len(plan.ins), len(plan.outs)
    hbm = pl.BlockSpec(memory_space=pl.ANY)

    def full_body(*refs):
        ins, c_ins = refs[:n_in], refs[n_in:n_in + c_in]
        outs = refs[n_in + c_in:n_in + c_in + n_out]
        c_outs = refs[n_in + c_in + n_out:n_in + c_in + n_out + c_out]
        scr = refs[n_in + c_in + n_out + c_out:]
        sems = scr[n_scr:]

        @pl.when(first())
        def _():
            plan.start(c_ins, c_outs, *sems)

        body(*ins, *outs, *scr[:n_scr])

        @pl.when(last())
        def _():
            plan.finish(c_ins, c_outs, *sems)

    res = pl.pallas_call(
        full_body, name=name, grid=grid,
        in_specs=in_specs + [hbm] * c_in, out_specs=out_specs + [hbm] * c_out,
        out_shape=out_shape + list(plan.outs),
        scratch_shapes=list(scratch_shapes) + _comm_scratch(plan),
        input_output_aliases=input_output_aliases or {},
        compiler_params=_cp(*(["arbitrary"] * len(grid))),
    )(*ins, *plan.ins)
    return list(res[:n_out]), list(res[n_out:])


def _gather_plan(xs):
    n = len(xs)
    ms = [v.shape[0] for v in xs]

    def tools(x_refs, o_refs, send_sems, recv_sems, local_sems):
        x, y, c = _my_place()
        me, sib = (x, y, c), (x, y, 1 - c)
        chips = [(1 - x, y), (x, 1 - y), (1 - x, 1 - y)]

        def rows(a, p):
            return o_refs[a].at[pl.ds(pl.multiple_of(_dev_index(p) * ms[a], 8), ms[a])]

        def copy(a, k, block, to, src=None):
            return pltpu.make_async_remote_copy(
                src_ref=rows(a, block) if src is None else src, dst_ref=rows(a, block),
                send_sem=send_sems.at[a * 7 + k], recv_sem=recv_sems.at[a * 7 + k],
                device_id=to, device_id_type=MESH)

        mine = [pltpu.make_async_copy(x_refs[a], rows(a, me), local_sems.at[a]) for a in range(n)]
        first = []
        for a in range(n):
            first.append(copy(a, 0, me, sib, src=x_refs[a]))
            first += [copy(a, 1 + j, me, (*chip, c), src=x_refs[a]) for j, chip in enumerate(chips)]
        return me, sib, chips, c, copy, mine, first

    def start(*refs):
        _, _, _, _, _, mine, first = tools(*refs)
        for cp in mine + first:
            cp.start()

    def finish(*refs):
        me, sib, chips, c, copy, mine, first = tools(*refs)
        passed = []
        for j, chip in enumerate(chips):
            for a in range(n):
                copy(a, 1 + j, (*chip, c), me).wait_recv()
                cp = copy(a, 4 + j, (*chip, c), sib)
                cp.start()
                passed.append(cp)
        for a in range(n):
            copy(a, 0, sib, me).wait_recv()
            for j, chip in enumerate(chips):
                copy(a, 4 + j, (*chip, 1 - c), me).wait_recv()
        for cp in first + passed:
            cp.wait_send()
        for cp in mine:
            cp.wait()

    outs = [jax.ShapeDtypeStruct((N_DEV * v.shape[0], v.shape[1]), v.dtype) for v in xs]
    return _Comm(list(xs), outs, 7 * n, n, start, finish)


def _exchange_plan(gs, cols=None):
    n = len(gs)
    rs = [v.shape[0] // N_DEV for v in gs]
    flips = [(bx, by, bc) for bx in (0, 1) for by in (0, 1) for bc in (0, 1)][1:]

    def tools(g_refs, o_refs, send_sems, recv_sems, local_sems):
        x, y, c = _my_place()
        me = (x, y, c)

        def block(ref, a, p):
            return ref.at[pl.ds(_dev_index(p) * rs[a], rs[a])]

        def src(a, p):
            rows = pl.ds(_dev_index(p) * rs[a], rs[a])
            return g_refs[a].at[rows] if cols is None else g_refs[a].at[rows, pl.ds(cols[0], cols[1])]

        def peer(f):
            return (1 - x if f[0] else x, 1 - y if f[1] else y, 1 - c if f[2] else c)

        def copy(a, k, to):
            return pltpu.make_async_remote_copy(
                src_ref=src(a, to), dst_ref=block(o_refs[a], a, me),
                send_sem=send_sems.at[a * 7 + k], recv_sem=recv_sems.at[a * 7 + k],
                device_id=to, device_id_type=MESH)

        def arrival(a, k, frm):
            return pltpu.make_async_remote_copy(
                src_ref=src(a, frm), dst_ref=block(o_refs[a], a, frm),
                send_sem=send_sems.at[a * 7 + k], recv_sem=recv_sems.at[a * 7 + k],
                device_id=frm, device_id_type=MESH)

        mine = [pltpu.make_async_copy(src(a, me), block(o_refs[a], a, me), local_sems.at[a])
                for a in range(n)]
        sends = [copy(a, k, peer(f)) for a in range(n) for k, f in enumerate(flips)]
        arrivals = [arrival(a, k, peer(f)) for a in range(n) for k, f in enumerate(flips)]
        return mine, sends, arrivals

    def start(*refs):
        mine, sends, _ = tools(*refs)
        for cp in mine + sends:
            cp.start()

    def finish(*refs):
        mine, sends, arrivals = tools(*refs)
        for cp in arrivals:
            cp.wait_recv()
        for cp in sends:
            cp.wait_send()
        for cp in mine:
            cp.wait()

    outs = [jax.ShapeDtypeStruct((v.shape[0], v.shape[1] if cols is None else cols[1]), v.dtype) for v in gs]
    return _Comm(list(gs), outs, 7 * n, n, start, finish)


def _rope_tables(seq):
    t = jnp.arange(seq)
    row, col = t // GRID_W, t % GRID_W
    half = HEAD_DIM // 2
    inv = ROPE_BASE ** (-jnp.arange(0, half, 2, dtype=F32) / half)
    ang_r = row.astype(F32)[:, None] * inv
    ang_c = col.astype(F32)[:, None] * inv
    ang = jnp.concatenate([ang_r, ang_r, ang_c, ang_c], axis=-1)
    return jnp.tile(jnp.cos(ang), (1, 2)), jnp.tile(jnp.sin(ang), (1, 2))


def _to_slots(w):
    return w.reshape(N_KV_HEADS, GQA_GROUP, HEAD_DIM, w.shape[1]).transpose(1, 0, 2, 3).reshape(w.shape)


def _from_slots(w):
    return w.reshape(GQA_GROUP, N_KV_HEADS, HEAD_DIM, w.shape[1]).transpose(1, 0, 2, 3).reshape(w.shape)


def _pack(vs):
    flat = jnp.concatenate([v.reshape(-1).astype(F32) for v in vs])
    total = -(-flat.shape[0] // (8 * LANES)) * (8 * LANES)
    return jnp.pad(flat, (0, total - flat.shape[0])).reshape(-1, LANES)


def _unpack(packed, like):
    flat, out, off = packed.reshape(-1), [], 0
    for v in like:
        size = math.prod(v.shape)
        out.append(flat[off:off + size].reshape(v.shape))
        off += size
    return out


def _silu(v):
    return v * jax.nn.sigmoid(v)


def kernel(x, c, ctx, c_ctx, w_mod, b_mod, norm_mix_w, w_in, q_norm_w, k_norm_w, sink_logit, conv_w, conv_b, conv_norm_w, conv_norm_b, w_out, norm_ffn_w, w_up, ffn_conv_w, ffn_conv_b, w_down, loss_target, m_c_ctx, m_w_mod, m_b_mod, m_norm_mix_w, m_w_in, m_q_norm_w, m_k_norm_w, m_sink_logit, m_conv_w, m_conv_b, m_conv_norm_w, m_conv_norm_b, m_w_out, m_norm_ffn_w, m_w_up, m_ffn_conv_w, m_ffn_conv_b, m_w_down, v_c_ctx, v_w_mod, v_b_mod, v_norm_mix_w, v_w_in, v_q_norm_w, v_k_norm_w, v_sink_logit, v_conv_w, v_conv_b, v_conv_norm_w, v_conv_norm_b, v_w_out, v_norm_ffn_w, v_w_up, v_ffn_conv_w, v_ffn_conv_b, v_w_down):
    d = D_MODEL
    seq, n_ctx = x.shape[1], ctx.shape[1]
    me = _dev_index(_my_place())
    xs, ctxs, tgt = x[0], ctx[0], loss_target[0]

    small = _pack([c[0], conv_w[0], ffn_conv_w[0]])
    small_all = _all_gather([small], name="gather_small")[0].reshape(N_DEV, -1)
    n_cw, n_fw = conv_w[0].size, ffn_conv_w[0].size
    c_all = small_all[:, :d]
    cw_all = small_all[:, d:d + n_cw].reshape(N_DEV, CONV_K, -1)
    fw_all = small_all[:, d + n_cw:d + n_cw + n_fw].reshape(N_DEV, FFN_K, -1)
    conv_w_f = cw_all.transpose(1, 0, 2).reshape(CONV_K, CONV_CH)
    ffn_w_f = fw_all.transpose(1, 0, 2).reshape(FFN_K, 2 * FFN_H)

    mcols = w_mod.shape[2]
    act = jnp.zeros((16, d), F32).at[:N_DEV].set(_silu(c_all)).at[N_DEV].set(_silu(c_ctx))
    mod_part = _mm(act, w_mod[0], 16, mcols, d, name="mod_fwd")
    mod_all = _all_gather([mod_part], name="gather_mod")[0]
    mod_all = mod_all.reshape(N_DEV, 16, mcols).transpose(1, 0, 2).reshape(16, 6 * d) + b_mod
    mod = lax.dynamic_slice_in_dim(mod_all, me, 1, axis=0)
    sh1, sc1, g1, sh2, sc2, g2 = [mod[:, k * d:(k + 1) * d] for k in range(6)]
    sh1c, sc1c = mod_all[N_DEV:N_DEV + 1, :d], mod_all[N_DEV:N_DEV + 1, d:2 * d]

    cos, sin = _rope_tables(seq)
    ones_c, zeros_c = jnp.ones((n_ctx, LANES), F32), jnp.zeros((n_ctx, LANES), F32)
    qk_w = jnp.concatenate([jnp.tile(q_norm_w, (1, N_Q_HEADS)), jnp.tile(k_norm_w, (1, N_KV_HEADS))], axis=1)
    kc_w = jnp.tile(k_norm_w, (1, N_KV_HEADS))

    h, (w_in_t,) = _prenorm(xs, norm_mix_w, sc1, sh1, carry=_gather_plan([w_in[0].T.astype(MXU_DT)]),
                            name="prenorm_mix")
    hc, _ = _prenorm(ctxs, norm_mix_w, sc1c, sh1c, name="prenorm_ctx")
    w_in_p = jnp.concatenate([_to_slots(w_in_t[:Q_COLS]), w_in_t[Q_COLS:QKV_COLS],
                              jnp.zeros((GLU_OFF - QKV_COLS, d), MXU_DT), w_in_t[QKV_COLS:]])
    proj = _mm(h, w_in_p, seq, IN_PAD, d, tb=True, name="proj_in")
    kv_ctx = _mm(hc, w_in_p, n_ctx, 2 * KV_COLS, d, tb=True, n0=Q_COLS, name="proj_ctx")
    qk_r = _qk_prep(proj, QK_COLS, cos, sin, qk_w, name="qk_prep")
    kc_n = _qk_prep(kv_ctx, KV_COLS, ones_c, zeros_c, kc_w, name="k_ctx_prep")
    (attn_o, lse), (w_up_t, w_out_f) = _attn_fwd(
        sink_logit, qk_r, proj, kc_n, kv_ctx, name="attn_fwd",
        carry=_gather_plan([w_up[0].T.astype(MXU_DT), w_out[0].astype(MXU_DT)]))
    w_out_f = jnp.concatenate([_to_slots(w_out_f[:Q_COLS]), w_out_f[Q_COLS:]])
    (u3, u1), (w_down_f,) = _conv_fwd(proj, GLU_OFF // GLU_COLS, conv_w_f, conv_b, conv_norm_w, conv_norm_b,
                                      carry=_gather_plan([w_down[0].astype(MXU_DT)]), name="conv_fwd")
    mix = _mm(attn_o, w_out_f, seq, d, Q_COLS, name="out_attn")
    mix = _mm(u3, w_out_f, seq, d, CONV_CH, k0=Q_COLS, add=mix, name="out_conv")

    x1, h2 = _resid_prenorm(xs, mix, g1, norm_ffn_w, sc2, sh2, name="prenorm_ffn")
    up0 = _mm(h2, w_up_t, seq, 2 * FFN_H, d, tb=True, out_dtype=MXU_DT, name="ffn_up")
    act_a, gate_s, val_s = _ffn_act(up0, ffn_w_f, ffn_conv_b, name="ffn_act")
    ffn = _mm(act_a, w_down_f, seq, d, FFN_H, name="ffn_down")
    loss_p, dy, dffn, dg2 = _loss_head(x1, ffn, g2, tgt, name="loss_head")

    da = _mm(dffn, w_down_f, seq, FFN_H, d, tb=True, out_dtype=MXU_DT, name="ffn_down_dx")
    gw_down = _mm(act_a, dffn, FFN_H, d, seq, ta=True, out_dtype=MXU_DT, name="ffn_down_dw")
    (dgate0, dval0, s_gate, s_val), (rx_down,) = _ffn_act_bwd(
        up0, gate_s, val_s, da, ffn_w_f, carry=_exchange_plan([gw_down]), name="ffn_act_bwd")
    gw_up_t = _mm(dgate0, h2, FFN_H, d, seq, ta=True, out_dtype=MXU_DT, out_rows=2 * FFN_H, name="ffn_up_dw_gate")
    gw_up_t = _mm(dval0, h2, FFN_H, d, seq, ta=True, into=gw_up_t, m0=FFN_H, name="ffn_up_dw_val")
    qd = d // 4
    dh2, (rx_up_a,) = _mm(dgate0, w_up_t, seq, d, FFN_H, a2=dval0, name="ffn_up_dx",
                          carry=_exchange_plan([gw_up_t], cols=(0, qd)))
    (dx1, dmix, dsh2, dsc2, dnw2, dg1), _ = _norm_bwd(
        dh2, x1, norm_ffn_w, sc2, res=dy, gate=(mix, g1), name="prenorm_ffn_bwd")

    dattn = _mm(dmix, w_out_f, seq, Q_COLS, d, tb=True, out_dtype=MXU_DT, name="out_dx_attn")
    du3 = _mm(dmix, w_out_f, seq, CONV_CH, d, tb=True, n0=Q_COLS, name="out_dx_conv")
    gw_out = _mm(attn_o, dmix, Q_COLS, d, seq, ta=True, out_dtype=MXU_DT, out_rows=Q_COLS + CONV_CH,
                 name="out_dw_attn")
    gw_out = _mm(u3, dmix, CONV_CH, d, seq, ta=True, into=gw_out, m0=Q_COLS, name="out_dw_conv")
    gw_out = jnp.concatenate([_from_slots(gw_out[:Q_COLS]), gw_out[Q_COLS:]])
    (dq, dk, dv, dkc_r, dvc, dsink_rows), (rx_out, rx_up_b) = _attn_bwd(
        sink_logit, qk_r, proj, kc_n, kv_ctx, attn_o, lse, dattn, name="attn_bwd",
        carry=_join_plans(_exchange_plan([gw_out]), _exchange_plan([gw_up_t], cols=(qd, qd))))
    dproj, dqk_w = _qk_prep_bwd(proj, [dq, dk], QK_COLS, cos, sin, qk_w, tail=dv, name="qk_prep_bwd")
    (dproj, dcw8, dvec), (rx_up_c,) = _conv_bwd(
        proj, GLU_OFF // GLU_COLS, u1, du3, dproj, conv_w_f, conv_norm_w, conv_norm_b,
        carry=_exchange_plan([gw_up_t], cols=(2 * qd, 2 * qd)), name="conv_bwd")
    dkc, dkc_w = _qk_prep_bwd(kv_ctx, [dkc_r], KV_COLS, ones_c, zeros_c, kc_w, name="k_ctx_prep_bwd")
    dkv_ctx = jnp.concatenate([dkc, dvc.astype(MXU_DT)], axis=1)
    gw_p = _mm(dproj, h, IN_PAD, d, seq, ta=True, name="proj_dw")
    gw_ctx = _mm(dkv_ctx, hc, 2 * KV_COLS, d, n_ctx, ta=True, name="proj_dw_ctx")
    gw_in_t = jnp.concatenate([_from_slots(gw_p[:Q_COLS]), gw_p[Q_COLS:QKV_COLS] + gw_ctx, gw_p[GLU_OFF:]],
                              axis=0).astype(MXU_DT)
    dh, (rx_in,) = _mm(dproj, w_in_p, seq, d, IN_PAD, carry=_exchange_plan([gw_in_t]), name="proj_dx")
    dhc = _mm(dkv_ctx, w_in_p, n_ctx, d, 2 * KV_COLS, k0=Q_COLS, name="proj_dx_ctx")
    (grad_x, dsh1, dsc1, dnw1), _ = _norm_bwd(dh, xs, norm_mix_w, sc1, res=dx1, name="prenorm_mix_bwd")
    (dsh1c, dsc1c, dnw1c), _ = _norm_bwd(dhc, ctxs, norm_mix_w, sc1c, want_dx=False, name="prenorm_ctx_bwd")

    dmod = jnp.concatenate([dsh1, dsc1, dg1, dsh2, dsc2, dg2], axis=1)
    dmod_ctx = jnp.concatenate([dsh1c, dsc1c], axis=1)
    d_qn = dqk_w[0, :Q_COLS].reshape(N_Q_HEADS, HEAD_DIM).sum(0)
    d_kn = (dqk_w[0, Q_COLS:].reshape(N_KV_HEADS, HEAD_DIM).sum(0)
            + dkc_w[0].reshape(N_KV_HEADS, HEAD_DIM).sum(0))
    d_ffn_w = jnp.concatenate([s_gate[:FFN_K], s_val[:FFN_K]], axis=1)
    d_ffn_b = jnp.concatenate([s_gate[FFN_K], s_val[FFN_K]])
    d_sink = dsink_rows.reshape(N_Q_HEADS, BLOCK).sum(1)
    summed_like = [(dnw1 + dnw1c), d_qn[None], d_kn[None], d_sink[None], dvec[0:1], dvec[1:2],
                   dvec[2:3], dnw2, d_ffn_b[None], dcw8.sum(1), d_ffn_w, loss_p[0:1, 0:1]]
    pack = _pack([dmod, dmod_ctx] + summed_like)
    pack_all = _all_gather([pack], name="gather_small_grads")[0]
    pack_all = pack_all.reshape(N_DEV, pack.shape[0], LANES)
    tot = _sum_parts(pack_all, name="sum_small_grads")
    (dmod_sum, dmc_sum, g_nmix, g_qn, g_kn, g_sink, g_cb, g_lw, g_lb, g_nffn, g_fb, g_cw_f, g_fw_f,
     loss_sum) = _unpack(tot, [dmod, dmod_ctx] + summed_like)
    loss = loss_sum[0, 0]
    dmod_all = pack_all.reshape(N_DEV, -1)[:, :6 * d]
    g_b_mod = dmod_sum.at[:, :2 * d].add(dmc_sum)

    lo = me * mcols
    dm_rows = jnp.zeros((16, 6 * d), F32).at[:N_DEV].set(dmod_all).at[N_DEV, :2 * d].set(dmc_sum[0])
    dm_mine = lax.dynamic_slice_in_dim(dm_rows, lo, mcols, axis=1)
    parts_mod = _mm(act, dm_mine, d, mcols, 16, ta=True, name="mod_dw")[None]
    dact_part = _mm(dm_mine[N_DEV:N_DEV + 8], w_mod[0], 8, d, mcols, tb=True, name="mod_dx_ctx")
    dact_all = _all_gather([dact_part], name="gather_c_ctx_grad")[0].reshape(N_DEV, 8, d)
    dact = _sum_parts(dact_all, name="sum_c_ctx_grad")[0]
    sg = jax.nn.sigmoid(c_ctx)
    g_c_ctx = dact * (sg * (1.0 + c_ctx * (1.0 - sg)))

    def stacked(rx):
        return rx.reshape(N_DEV, rx.shape[0] // N_DEV, rx.shape[1])

    g_w_in = _sum_parts(stacked(rx_in), name="sum_w_in").T[None]
    g_w_up = jnp.concatenate([_sum_parts(stacked(rx_up_a), name="sum_w_up_a"),
                              _sum_parts(stacked(rx_up_b), name="sum_w_up_b"),
                              _sum_parts(stacked(rx_up_c), name="sum_w_up_c")], axis=1).T[None]
    big = {}
    big["w_in"] = _adam(w_in[0], m_w_in[0], v_w_in[0], g_w_in, name="adam_w_in")
    big["w_up"] = _adam(w_up[0], m_w_up[0], v_w_up[0], g_w_up, name="adam_w_up")
    big["w_out"] = _adam(w_out[0], m_w_out[0], v_w_out[0], stacked(rx_out), name="adam_w_out")
    big["w_down"] = _adam(w_down[0], m_w_down[0], v_w_down[0], stacked(rx_down), name="adam_w_down")
    big["w_mod"] = _adam(w_mod[0], m_w_mod[0], v_w_mod[0], parts_mod, name="adam_w_mod")

    ccols, fcols = conv_w.shape[2], ffn_conv_w.shape[2]
    g_conv_w = lax.dynamic_slice_in_dim(g_cw_f, me * ccols, ccols, axis=1)[None]
    g_ffn_w = lax.dynamic_slice_in_dim(g_fw_f, me * fcols, fcols, axis=1)[None]
    names = ["c_ctx", "b_mod", "norm_mix_w", "q_norm_w", "k_norm_w", "sink_logit", "conv_w", "conv_b",
             "conv_norm_w", "conv_norm_b", "norm_ffn_w", "ffn_conv_w", "ffn_conv_b"]
    ws = [c_ctx, b_mod, norm_mix_w, q_norm_w, k_norm_w, sink_logit, conv_w, conv_b, conv_norm_w, conv_norm_b,
          norm_ffn_w, ffn_conv_w, ffn_conv_b]
    msm = [m_c_ctx, m_b_mod, m_norm_mix_w, m_q_norm_w, m_k_norm_w, m_sink_logit, m_conv_w, m_conv_b,
           m_conv_norm_w, m_conv_norm_b, m_norm_ffn_w, m_ffn_conv_w, m_ffn_conv_b]
    vsm = [v_c_ctx, v_b_mod, v_norm_mix_w, v_q_norm_w, v_k_norm_w, v_sink_logit, v_conv_w, v_conv_b,
           v_conv_norm_w, v_conv_norm_b, v_norm_ffn_w, v_ffn_conv_w, v_ffn_conv_b]
    gsm = [g_c_ctx, g_b_mod, g_nmix, g_qn, g_kn, g_sink, g_conv_w, g_cb, g_lw, g_lb, g_nffn, g_ffn_w, g_fb]
    deltas, new_ms, new_vs = _adam_many(ws, msm, vsm, gsm, name="adam_small")
    sm = {nm: vals for nm, vals in zip(names, zip(gsm, deltas, new_ms, new_vs))}

    def out4(nm):
        if nm in sm:
            return sm[nm]
        return tuple(t[None] for t in big[nm])

    order = ["c_ctx", "w_mod", "b_mod", "norm_mix_w", "w_in", "q_norm_w", "k_norm_w", "sink_logit", "conv_w",
             "conv_b", "conv_norm_w", "conv_norm_b", "w_out", "norm_ffn_w", "w_up", "ffn_conv_w", "ffn_conv_b",
             "w_down"]
    quads = [out4(nm) for nm in order]
    return (loss, grad_x[None], *[q[0] for q in quads], *[q[1] for q in quads],
            *[q[2] for q in quads], *[q[3] for q in quads])
```

```python
import math
from typing import Callable, NamedTuple

import jax
import jax.numpy as jnp
from jax import lax
from jax.experimental import pallas as pl
from jax.experimental.pallas import tpu as pltpu

F32 = jnp.float32
MXU_DT = jnp.bfloat16

D_MODEL = 1024
GRID_W = 64
HEAD_DIM = 64
N_Q_HEADS = 8
N_KV_HEADS = 2
GQA_GROUP = 4
WINDOW = 128
BLOCK = 128
Q_COLS = 512
KV_COLS = 128
QK_COLS = Q_COLS + KV_COLS
QKV_COLS = Q_COLS + 2 * KV_COLS
CONV_CH = 512
GLU_COLS = 2 * CONV_CH
IN_COLS = QKV_COLS + GLU_COLS
CONV_K = 31
CONV_PAD = 15
FFN_H = 2816
FFN_K = 3
ROPE_BASE = 10000.0
EPS = 1e-6
NEG_INF = -1e30
N_DEV = 8
HALO = 16
LANES = 128
ROW_TS = 512

ADAM_LR = 0.001
ADAM_B1 = 0.9
ADAM_B2 = 0.999
ADAM_EPS = 1e-08
ADAM_WD = 0.01
ADAM_STEP = 10

MESH = pl.DeviceIdType.MESH
VMEM_LIMIT = 56 << 20
MM_VMEM_BUDGET = 44 << 20
GLU_OFF = 1024
IN_PAD = GLU_OFF + GLU_COLS


def _cp(*dims):
    return pltpu.CompilerParams(dimension_semantics=dims or None, vmem_limit_bytes=VMEM_LIMIT)


def _row(ts, w, col=0):
    return pl.BlockSpec((ts, w), lambda i: (i, col))


def _vec(w):
    return pl.BlockSpec((1, w), lambda i: (0, 0))


def _colsum(v):
    return jnp.sum(v, axis=0, keepdims=True)


def _sigmoid(v):
    return 0.5 * jnp.tanh(0.5 * v) + 0.5


def _mm(a, b, m, n, k, *, ta=False, tb=False, n0=0, k0=0, add=None, out_dtype=F32, into=None, m0=0,
        out_rows=None, a2=None, carry=None, name):
    has_add, has_a2 = add is not None, a2 is not None
    assert not (has_a2 and (ta or tb))
    if into is not None:
        out_dtype = into.dtype
    sa, sb, so = a.dtype.itemsize, b.dtype.itemsize, jnp.dtype(out_dtype).itemsize
    sadd = add.dtype.itemsize if has_add else 0
    na = 2 if has_a2 else 1

    def fits(tm, tn):
        return 2 * (na * k * (tm * sa + tn * sb) + tm * tn * (so + sadd)) <= MM_VMEM_BUDGET

    tms = [m] if m <= 1024 else [t for t in (1024, 1408, 768, 512, 256, 128) if m % t == 0]
    tns = [t for t in ((1024, 512, 256, 128) if ta else (1408, 512, 256, 128)) if n % t == 0 and n0 % t == 0]
    tm, tn = next((tm, tn) for tm in tms for tn in tns if fits(tm, tn))
    a_spec = (pl.BlockSpec((k, tm), lambda i, j: (0, i)) if ta else pl.BlockSpec((tm, k), lambda i, j: (i, 0)))
    nb0 = n0 // tn
    if tb:
        assert k0 == 0
        b_spec = pl.BlockSpec((tn, k), lambda i, j: (j + nb0, 0))
    else:
        assert k0 % (na * k) == 0, (k0, k)
        kb0 = k0 // (na * k)
        b_spec = pl.BlockSpec((na * k, tn), lambda i, j: (kb0, j + nb0))
    assert m0 % tm == 0, (m0, tm)
    mb0 = m0 // tm
    o_spec = pl.BlockSpec((tm, tn), lambda i, j: (i + mb0, j))
    dims = (((0 if ta else 1,), (1 if tb else 0,)), ((), ()))

    def body(*refs):
        a_ref, b_ref, o_ref = refs[0], refs[na], refs[-1]
        if has_a2:
            res = (jnp.dot(a_ref[...].astype(MXU_DT), b_ref[0:k, :].astype(MXU_DT), preferred_element_type=F32)
                   + jnp.dot(refs[1][...].astype(MXU_DT), b_ref[k:2 * k, :].astype(MXU_DT),
                             preferred_element_type=F32))
        else:
            res = lax.dot_general(a_ref[...].astype(MXU_DT), b_ref[...].astype(MXU_DT), dims,
                                  preferred_element_type=F32)
        if has_add:
            res = res + refs[na + 1][...].astype(F32)
        o_ref[...] = res.astype(o_ref.dtype)

    ins = [a] + ([a2] if has_a2 else []) + [b] + ([add] if has_add else []) + ([into] if into is not None else [])
    specs = ([a_spec] * na + [b_spec] + ([pl.BlockSpec((tm, tn), lambda i, j: (i, j))] if has_add else [])
             + ([pl.BlockSpec(memory_space=pl.ANY)] if into is not None else []))
    out_shape = (jax.ShapeDtypeStruct(into.shape, into.dtype) if into is not None
                 else jax.ShapeDtypeStruct((out_rows or m, n), out_dtype))
    (out,), carried = _grid_call(
        body, ins, carry, name=name, grid=(m // tm, n // tn), in_specs=specs, out_specs=[o_spec],
        out_shape=[out_shape], input_output_aliases={len(ins) - 1: 0} if into is not None else None,
        dims=("parallel", "parallel"))
    return out if carry is None else (out, carried)


def _rms_stats(xv):
    r = lax.rsqrt(jnp.mean(xv * xv, axis=-1, keepdims=True) + EPS)
    return r, xv * r


def _prenorm(x, nw, sc, sh, *, carry=None, name):
    rows, d = x.shape
    ts = min(rows, ROW_TS)

    def body(x_ref, nw_ref, sc_ref, sh_ref, h_ref):
        _, xn = _rms_stats(x_ref[...])
        h_ref[...] = ((xn * nw_ref[...]) * (1.0 + sc_ref[...]) + sh_ref[...]).astype(h_ref.dtype)

    (h,), carried = _grid_call(
        body, [x, nw, sc, sh], carry, name=name, grid=(rows // ts,),
        in_specs=[_row(ts, d), _vec(d), _vec(d), _vec(d)], out_specs=[_row(ts, d)],
        out_shape=[jax.ShapeDtypeStruct((rows, d), MXU_DT)], dims=("parallel",))
    return h, carried


def _resid_prenorm(x, mix, g1, nw, sc, sh, *, name):
    rows, d = x.shape
    ts = min(rows, ROW_TS)

    def body(x_ref, mix_ref, g_ref, nw_ref, sc_ref, sh_ref, x1_ref, h_ref):
        x1 = x_ref[...] + g_ref[...] * mix_ref[...]
        x1_ref[...] = x1
        _, xn = _rms_stats(x1)
        h_ref[...] = ((xn * nw_ref[...]) * (1.0 + sc_ref[...]) + sh_ref[...]).astype(h_ref.dtype)

    return pl.pallas_call(
        body, name=name, grid=(rows // ts,),
        in_specs=[_row(ts, d), _row(ts, d), _vec(d), _vec(d), _vec(d), _vec(d)],
        out_specs=[_row(ts, d), _row(ts, d)],
        out_shape=[jax.ShapeDtypeStruct((rows, d), F32), jax.ShapeDtypeStruct((rows, d), MXU_DT)],
        compiler_params=_cp("parallel"),
    )(x, mix, g1, nw, sc, sh)


def _ffn_down_loss(act, w_down, x1, g2, target, *, name):
    rows, d = x1.shape
    k = act.shape[1]
    tm, tn = min(rows, 1024), 512

    def body(a_ref, b_ref, x1_ref, g_ref, t_ref, loss_ref, dy_ref, dffn_ref, dg_ref):
        j, i = pl.program_id(0), pl.program_id(1)
        f = jnp.dot(a_ref[...].astype(MXU_DT), b_ref[...].astype(MXU_DT), preferred_element_type=F32)
        e = x1_ref[...] + g_ref[...] * f - t_ref[...]
        part = (0.5 / d) * jnp.sum(jnp.sum(e * e, axis=-1, keepdims=True), axis=0, keepdims=True)
        dy = e * (1.0 / d)
        dy_ref[...] = dy
        dffn_ref[...] = (dy * g_ref[...]).astype(dffn_ref.dtype)

        @pl.when((i == 0) & (j == 0))
        def _():
            loss_ref[...] = jnp.zeros_like(loss_ref)

        @pl.when(i == 0)
        def _():
            dg_ref[...] = jnp.zeros_like(dg_ref)

        loss_ref[...] += jnp.broadcast_to(part, loss_ref.shape)
        dg_ref[...] += _colsum(dy * f)

    tile = pl.BlockSpec((tm, tn), lambda j, i: (i, j))
    vec = pl.BlockSpec((1, tn), lambda j, i: (0, j))
    return pl.pallas_call(
        body, name=name, grid=(d // tn, rows // tm),
        in_specs=[pl.BlockSpec((tm, k), lambda j, i: (i, 0)), pl.BlockSpec((k, tn), lambda j, i: (0, j)),
                  tile, vec, tile],
        out_specs=[pl.BlockSpec((8, LANES), lambda j, i: (0, 0)), tile, tile, vec],
        out_shape=[jax.ShapeDtypeStruct((8, LANES), F32), jax.ShapeDtypeStruct((rows, d), F32),
                   jax.ShapeDtypeStruct((rows, d), MXU_DT), jax.ShapeDtypeStruct((1, d), F32)],
        compiler_params=_cp("arbitrary", "arbitrary"),
    )(act, w_down, x1, g2, target)


def _norm_bwd(dh, xin, nw, sc, *, res=None, gate=None, want_dx=True, carry=None, name):
    rows, d = xin.shape
    ts = min(rows, ROW_TS)
    has_res, has_gate = res is not None, gate is not None

    def body(*refs):
        it = iter(refs)
        dh_ref, x_ref, nw_ref, sc_ref = next(it), next(it), next(it), next(it)
        res_ref = next(it) if has_res else None
        gated_ref, g_ref = (next(it), next(it)) if has_gate else (None, None)
        dx_ref = next(it) if want_dx else None
        dgx_ref = next(it) if has_gate else None
        dsh_ref, dsc_ref, dnw_ref = next(it), next(it), next(it)
        dg_ref = next(it) if has_gate else None
        i = pl.program_id(0)
        dhv = dh_ref[...].astype(F32)
        r, xn = _rms_stats(x_ref[...])
        dn = dhv * (1.0 + sc_ref[...])

        @pl.when(i == 0)
        def _():
            dsh_ref[...] = jnp.zeros_like(dsh_ref)
            dsc_ref[...] = jnp.zeros_like(dsc_ref)
            dnw_ref[...] = jnp.zeros_like(dnw_ref)
            if has_gate:
                dg_ref[...] = jnp.zeros_like(dg_ref)

        dsh_ref[...] += _colsum(dhv)
        dsc_ref[...] += _colsum(dhv * (xn * nw_ref[...]))
        dnw_ref[...] += _colsum(dn * xn)
        if want_dx:
            dxn = dn * nw_ref[...]
            dx = r * (dxn - xn * jnp.mean(dxn * xn, axis=-1, keepdims=True))
            if has_res:
                dx = dx + res_ref[...]
            dx_ref[...] = dx
            if has_gate:
                dgx_ref[...] = (dx * g_ref[...]).astype(dgx_ref.dtype)
                dg_ref[...] += _colsum(dx * gated_ref[...])

    ins = [dh, xin, nw, sc] + ([res] if has_res else []) + (list(gate) if has_gate else [])
    in_specs = ([_row(ts, d), _row(ts, d), _vec(d), _vec(d)] + ([_row(ts, d)] if has_res else [])
                + ([_row(ts, d), _vec(d)] if has_gate else []))
    out_specs, out_shape = [], []
    if want_dx:
        out_specs.append(_row(ts, d)); out_shape.append(jax.ShapeDtypeStruct((rows, d), F32))
    if has_gate:
        out_specs.append(_row(ts, d)); out_shape.append(jax.ShapeDtypeStruct((rows, d), MXU_DT))
    for _ in range(3 + int(has_gate)):
        out_specs.append(_vec(d)); out_shape.append(jax.ShapeDtypeStruct((1, d), F32))
    return _grid_call(body, ins, carry, name=name, grid=(rows // ts,), in_specs=in_specs, out_specs=out_specs,
                      out_shape=out_shape, dims=("arbitrary",))


def _group_sum(v, g):
    hi = v.astype(MXU_DT)
    lo = (v - hi.astype(F32)).astype(MXU_DT)
    return (jnp.dot(hi, g, preferred_element_type=F32) + jnp.dot(lo, g, preferred_element_type=F32))


def _rot(v):
    lane = lax.broadcasted_iota(jnp.int32, v.shape, 1)
    first = (lane & 31) < 16
    return jnp.where(first, -pltpu.roll(v, LANES - 16, 1), pltpu.roll(v, 16, 1))


def _head_group_matrix():
    r = jnp.arange(LANES) // HEAD_DIM
    return (r[:, None] == r[None, :]).astype(MXU_DT)


def _qk_prep(xin, width, cos, sin, w, *, name):
    rows = xin.shape[0]
    ts = min(rows, 256)
    nch = width // LANES

    def body(x_ref, cos_ref, sin_ref, w_ref, g_ref, o_ref):
        cs, sn, g = cos_ref[...], sin_ref[...], g_ref[...]
        for ch in range(nch):
            sl = slice(ch * LANES, (ch + 1) * LANES)
            xv = x_ref[:, sl]
            r = lax.rsqrt(_group_sum(xv * xv, g) * (1.0 / HEAD_DIM) + EPS)
            yw = (xv * r) * w_ref[:, sl]
            o_ref[:, sl] = (yw * cs + _rot(yw) * sn).astype(o_ref.dtype)

    return pl.pallas_call(
        body, name=name, grid=(rows // ts,),
        in_specs=[_row(ts, width), _row(ts, LANES), _row(ts, LANES), _vec(width),
                  pl.BlockSpec((LANES, LANES), lambda i: (0, 0))],
        out_specs=_row(ts, width),
        out_shape=jax.ShapeDtypeStruct((rows, width), MXU_DT), compiler_params=_cp("parallel"),
    )(xin, cos, sin, w, _head_group_matrix())


def _qk_prep_bwd(xin, douts, width, cos, sin, w, *, tail=None, name):
    rows = xin.shape[0]
    ts = min(rows, 256)
    nch = width // LANES
    has_tail = tail is not None
    nd = len(douts)
    assert sum(v.shape[1] for v in douts) == width
    src = [(k, c) for k, v in enumerate(douts) for c in range(v.shape[1] // LANES)]

    def body(*refs):
        x_ref, d_refs = refs[0], refs[1:1 + nd]
        t_ref = refs[1 + nd] if has_tail else None
        cos_ref, sin_ref, w_ref, g_ref, dx_ref, dw_ref = refs[1 + nd + int(has_tail):]
        i = pl.program_id(0)
        cs, sn, g = cos_ref[...], sin_ref[...], g_ref[...]

        @pl.when(i == 0)
        def _():
            dw_ref[...] = jnp.zeros_like(dw_ref)

        if has_tail:
            dx_ref[:, width:width + LANES] = t_ref[...].astype(dx_ref.dtype)
            dx_ref[:, width + LANES:] = jnp.zeros((ts, GLU_OFF - width - LANES), dx_ref.dtype)

        for ch in range(nch):
            sl = slice(ch * LANES, (ch + 1) * LANES)
            xv = x_ref[:, sl]
            dv = d_refs[src[ch][0]][:, src[ch][1] * LANES:(src[ch][1] + 1) * LANES].astype(F32)
            r = lax.rsqrt(_group_sum(xv * xv, g) * (1.0 / HEAD_DIM) + EPS)
            n = xv * r
            dyw = dv * cs - _rot(dv * sn)
            dw_ref[:, sl] += _colsum(dyw * n)
            dn = dyw * w_ref[:, sl]
            gm = _group_sum(dn * n, g) * (1.0 / HEAD_DIM)
            dx_ref[:, sl] = (r * (dn - n * gm)).astype(dx_ref.dtype)

    ins = [xin] + list(douts) + ([tail] if has_tail else []) + [cos, sin, w, _head_group_matrix()]
    in_specs = ([_row(ts, width)] + [_row(ts, v.shape[1]) for v in douts] + ([_row(ts, LANES)] if has_tail else [])
                + [_row(ts, LANES), _row(ts, LANES), _vec(width), pl.BlockSpec((LANES, LANES), lambda i: (0, 0))])
    out_w, arr_w = (GLU_OFF, IN_PAD) if has_tail else (width, width)
    return pl.pallas_call(
        body, name=name, grid=(rows // ts,), in_specs=in_specs,
        out_specs=[_row(ts, out_w), _vec(width)],
        out_shape=[jax.ShapeDtypeStruct((rows, arr_w), MXU_DT), jax.ShapeDtypeStruct((1, width), F32)],
        compiler_params=_cp("arbitrary"),
    )(*ins)


GB = GQA_GROUP * BLOCK
WIN = 3 * BLOCK
ATT_STEP = 4


def _win_start(i, seq):
    return pl.multiple_of(jnp.clip((i - 1) * BLOCK, 0, seq - WIN), BLOCK)


def _attn_mask(i, start):
    qpos = i * BLOCK + (lax.broadcasted_iota(jnp.int32, (GB, WIN), 0) & (BLOCK - 1))
    kpos = start + lax.broadcasted_iota(jnp.int32, (GB, WIN), 1)
    return jnp.abs(qpos - kpos) <= WINDOW


def _sink_col(sink_ref, kv):
    return jnp.concatenate(
        [jnp.full((BLOCK, 1), sink_ref[0, kv * GQA_GROUP + g], F32) for g in range(GQA_GROUP)], axis=0)


def _stack_slots(ref, rows):
    return jnp.concatenate([ref[rows, g * LANES:(g + 1) * LANES] for g in range(GQA_GROUP)], axis=0)


def _kv_lanes(kv):
    lane = lax.broadcasted_iota(jnp.int32, (1, LANES), 1)
    return (lane < HEAD_DIM) if kv == 0 else (lane >= HEAD_DIM)


def _keep(mask, v):
    return jnp.where(mask, v, jnp.zeros_like(v))


_NT = (((1,), (1,)), ((), ()))
_TN = (((0,), (0,)), ((), ()))


def _attn_specs(seq, n_ctx):
    qs = pl.BlockSpec((ATT_STEP * BLOCK, Q_COLS), lambda i: (i, 0))
    ks = pl.BlockSpec((seq, KV_COLS), lambda i: (0, Q_COLS // KV_COLS))
    vs = pl.BlockSpec((seq, KV_COLS), lambda i: (0, QK_COLS // KV_COLS))
    kcs = pl.BlockSpec((n_ctx, KV_COLS), lambda i: (0, 0))
    vcs = pl.BlockSpec((n_ctx, KV_COLS), lambda i: (0, 1))
    ls = pl.BlockSpec((ATT_STEP, N_KV_HEADS * GB, 1), lambda i: (i, 0, 0))
    return qs, ks, vs, kcs, vcs, ls


def _attn_fwd(sink, qk, proj, kc, kv_ctx, *, carry=None, name):
    seq, n_ctx = qk.shape[0], kc.shape[0]
    scale = 1.0 / math.sqrt(HEAD_DIM)

    def one_block(blk, sub, sink_ref, q_ref, k_ref, v_ref, kcw, vcw, o_ref, lse_ref):
        rows = slice(sub * BLOCK, (sub + 1) * BLOCK)
        start = _win_start(blk, seq)
        valid = _attn_mask(blk, start)
        qs = _stack_slots(q_ref, rows)
        kw, vw = k_ref[pl.ds(start, WIN), :], v_ref[pl.ds(start, WIN), :].astype(MXU_DT)
        o_all = jnp.zeros((GB, LANES), F32)
        for kv in range(N_KV_HEADS):
            mine = _kv_lanes(kv)
            s_loc = lax.dot_general(qs, _keep(mine, kw), _NT, preferred_element_type=F32) * scale
            s_loc = jnp.where(valid, s_loc, NEG_INF)
            s_ctx = lax.dot_general(qs, _keep(mine, kcw), _NT, preferred_element_type=F32) * scale
            sk = _sink_col(sink_ref, kv)
            m = jnp.maximum(jnp.maximum(jnp.max(s_loc, axis=-1, keepdims=True),
                                        jnp.max(s_ctx, axis=-1, keepdims=True)), sk)
            p_loc = jnp.exp(s_loc - m)
            p_ctx = jnp.exp(s_ctx - m)
            l = (jnp.sum(p_loc, axis=-1, keepdims=True) + jnp.sum(p_ctx, axis=-1, keepdims=True)
                 + jnp.exp(sk - m))
            o_all = o_all + (jnp.dot(p_loc.astype(MXU_DT), _keep(mine, vw), preferred_element_type=F32)
                             + jnp.dot(p_ctx.astype(MXU_DT), _keep(mine, vcw), preferred_element_type=F32)) / l
            lse_ref[sub, kv * GB:(kv + 1) * GB, :] = m + jnp.log(l)
        for g in range(GQA_GROUP):
            o_ref[rows, g * LANES:(g + 1) * LANES] = o_all[g * BLOCK:(g + 1) * BLOCK].astype(o_ref.dtype)

    def body(sink_ref, q_ref, k_ref, v_ref, kc_ref, vc_ref, o_ref, lse_ref):
        i = pl.program_id(0)
        kcw, vcw = kc_ref[...], vc_ref[...].astype(MXU_DT)
        for sub in range(ATT_STEP):
            one_block(i * ATT_STEP + sub, sub, sink_ref, q_ref, k_ref, v_ref, kcw, vcw, o_ref, lse_ref)

    qs, ks, vs, kcs, vcs, ls = _attn_specs(seq, n_ctx)
    return _grid_call(
        body, [sink, qk, qk, proj, kc, kv_ctx], carry, name=name, grid=(seq // (ATT_STEP * BLOCK),),
        in_specs=[pl.BlockSpec(memory_space=pltpu.SMEM), qs, ks, vs, kcs, vcs],
        out_specs=[qs, ls],
        out_shape=[jax.ShapeDtypeStruct((seq, Q_COLS), MXU_DT),
                   jax.ShapeDtypeStruct((seq // BLOCK, N_KV_HEADS * GB, 1), F32)],
        dims=("parallel",))


def _attn_bwd(sink, qk, proj, kc, kv_ctx, o, lse, do, *, carry=None, name):
    seq, n_ctx = qk.shape[0], kc.shape[0]
    scale = 1.0 / math.sqrt(HEAD_DIM)

    def body(sink_ref, q_ref, k_ref, v_ref, kc_ref, vc_ref, o_ref, lse_ref, do_ref,
             dq_ref, dk_ref, dv_ref, dkc_ref, dvc_ref, ds_ref):
        i = pl.program_id(0)

        @pl.when(i == 0)
        def _():
            dk_ref[...] = jnp.zeros_like(dk_ref)
            dv_ref[...] = jnp.zeros_like(dv_ref)
            dkc_ref[...] = jnp.zeros_like(dkc_ref)
            dvc_ref[...] = jnp.zeros_like(dvc_ref)
            ds_ref[...] = jnp.zeros_like(ds_ref)

        kcw, vcw = kc_ref[...], vc_ref[...].astype(MXU_DT)
        dkc, dvc = jnp.zeros((n_ctx, LANES), F32), jnp.zeros((n_ctx, LANES), F32)
        for sub in range(ATT_STEP):
            dkc_s, dvc_s = one_block(i * ATT_STEP + sub, sub, sink_ref, q_ref, k_ref, v_ref, kcw, vcw, o_ref,
                                     lse_ref, do_ref, dq_ref, dk_ref, dv_ref, ds_ref)
            dkc, dvc = dkc + dkc_s, dvc + dvc_s
        dkc_ref[...] += dkc
        dvc_ref[...] += dvc

    def one_block(blk, sub, sink_ref, q_ref, k_ref, v_ref, kcw, vcw, o_ref, lse_ref, do_ref,
                  dq_ref, dk_ref, dv_ref, ds_ref):
        qrows = slice(sub * BLOCK, (sub + 1) * BLOCK)
        start = _win_start(blk, seq)
        valid = _attn_mask(blk, start)
        win = pl.ds(start, WIN)
        qs, dos = _stack_slots(q_ref, qrows), _stack_slots(do_ref, qrows)
        do_o = dos.astype(F32) * _stack_slots(o_ref, qrows).astype(F32)
        kw, vw = k_ref[win, :], v_ref[win, :].astype(MXU_DT)
        dq = jnp.zeros((GB, LANES), F32)
        dk, dv = jnp.zeros((WIN, LANES), F32), jnp.zeros((WIN, LANES), F32)
        dkc, dvc = jnp.zeros((n_ctx, LANES), F32), jnp.zeros((n_ctx, LANES), F32)
        for kv in range(N_KV_HEADS):
            mine = _kv_lanes(kv)
            rows = slice(kv * GB, (kv + 1) * GB)
            lse_s = lse_ref[sub, rows, :]
            delta = jnp.sum(_keep(mine, do_o), axis=-1, keepdims=True)
            kz, vz, kcz, vcz = _keep(mine, kw), _keep(mine, vw), _keep(mine, kcw), _keep(mine, vcw)
            s_loc = lax.dot_general(qs, kz, _NT, preferred_element_type=F32) * scale
            s_loc = jnp.where(valid, s_loc, NEG_INF)
            s_ctx = lax.dot_general(qs, kcz, _NT, preferred_element_type=F32) * scale
            p_loc = jnp.exp(s_loc - lse_s)
            p_ctx = jnp.exp(s_ctx - lse_s)
            p_sink = jnp.exp(_sink_col(sink_ref, kv) - lse_s)
            dp_loc = lax.dot_general(dos, vz, _NT, preferred_element_type=F32)
            dp_ctx = lax.dot_general(dos, vcz, _NT, preferred_element_type=F32)
            ds_loc = (p_loc * (dp_loc - delta) * scale).astype(MXU_DT)
            ds_ctx = (p_ctx * (dp_ctx - delta) * scale).astype(MXU_DT)
            dq = dq + (jnp.dot(ds_loc, kz, preferred_element_type=F32)
                       + jnp.dot(ds_ctx, kcz, preferred_element_type=F32))
            dk = dk + _keep(mine, lax.dot_general(ds_loc, qs, _TN, preferred_element_type=F32))
            dv = dv + _keep(mine, lax.dot_general(p_loc.astype(MXU_DT), dos, _TN, preferred_element_type=F32))
            dkc = dkc + _keep(mine, lax.dot_general(ds_ctx, qs, _TN, preferred_element_type=F32))
            dvc = dvc + _keep(mine, lax.dot_general(p_ctx.astype(MXU_DT), dos, _TN, preferred_element_type=F32))
            ds_ref[rows, :] += -(p_sink * delta)
        for g in range(GQA_GROUP):
            dq_ref[qrows, g * LANES:(g + 1) * LANES] = dq[g * BLOCK:(g + 1) * BLOCK]
        dk_ref[win, :] += dk
        dv_ref[win, :] += dv
        return dkc, dvc

    qs, ks, vs, kcs, vcs, ls = _attn_specs(seq, n_ctx)
    whole = lambda r, c: pl.BlockSpec((r, c), lambda i: (0, 0))
    return _grid_call(
        body, [sink, qk, qk, proj, kc, kv_ctx, o, lse, do], carry, name=name, grid=(seq // (ATT_STEP * BLOCK),),
        in_specs=[pl.BlockSpec(memory_space=pltpu.SMEM), qs, ks, vs, kcs, vcs, qs, ls, qs],
        out_specs=[qs, whole(seq, KV_COLS), whole(seq, KV_COLS), whole(n_ctx, KV_COLS), whole(n_ctx, KV_COLS),
                   whole(N_KV_HEADS * GB, 1)],
        out_shape=[jax.ShapeDtypeStruct((seq, Q_COLS), F32), jax.ShapeDtypeStruct((seq, KV_COLS), F32),
                   jax.ShapeDtypeStruct((seq, KV_COLS), F32), jax.ShapeDtypeStruct((n_ctx, KV_COLS), F32),
                   jax.ShapeDtypeStruct((n_ctx, KV_COLS), F32), jax.ShapeDtypeStruct((N_KV_HEADS * GB, 1), F32)],
        dims=("arbitrary",))


def _halo_specs(ts, w, rows, col=0):
    per = ts // HALO
    last = rows // HALO - 1
    return [pl.BlockSpec((HALO, w), lambda i: (jnp.maximum(i * per - 1, 0), col)),
            pl.BlockSpec((ts, w), lambda i: (i, col)),
            pl.BlockSpec((HALO, w), lambda i: (jnp.minimum((i + 1) * per, last), col))]


def _glu(v):
    return v[:, :CONV_CH] * _sigmoid(v[:, CONV_CH:])


def _ln_stats(u):
    mu = jnp.mean(u, axis=-1, keepdims=True)
    xc = u - mu
    rstd = lax.rsqrt(jnp.mean(xc * xc, axis=-1, keepdims=True) + EPS)
    return xc * rstd, rstd


CONV_BWD_TS = 128


def _phases(ext_ref, ph_ref):
    n = ph_ref.shape[1]
    for b in range(1, 8):
        ph_ref[b - 1] = ext_ref[b:b + n, :]


def _window(ext_ref, ph_ref, o, n, cs):
    a, b = divmod(o, 8)
    src = ext_ref if b == 0 else ph_ref.at[b - 1]
    return src[8 * a:8 * a + n, cs]


def _conv_fwd(glu, col, cw, cb, lw, lb, *, carry=None, name):
    rows = glu.shape[0]
    ts = min(rows, 256)
    nt = rows // ts

    te = ts + 2 * HALO

    def body(gp_ref, g_ref, gn_ref, cw_ref, cb_ref, lw_ref, lb_ref, u3_ref, u1_ref, ext_ref, ph_ref):
        i = pl.program_id(0)
        ext_ref[0:HALO, :] = jnp.where(i > 0, _glu(gp_ref[...]), 0.0)
        ext_ref[HALO:HALO + ts, :] = _glu(g_ref[...])
        ext_ref[HALO + ts:, :] = jnp.where(i < nt - 1, _glu(gn_ref[...]), 0.0)
        _phases(ext_ref, ph_ref)
        for c in range(CONV_CH // LANES):
            cs = slice(c * LANES, (c + 1) * LANES)
            acc = jnp.broadcast_to(cb_ref[:, cs], (ts, LANES))
            for j in range(CONV_K):
                acc = acc + cw_ref[j:j + 1, cs] * _window(ext_ref, ph_ref, HALO - CONV_PAD + j, ts, cs)
            u1_ref[:, cs] = acc
        xh, _ = _ln_stats(u1_ref[...])
        u2 = xh * lw_ref[...] + lb_ref[...]
        u3_ref[...] = (u2 * _sigmoid(u2)).astype(u3_ref.dtype)

    full = lambda shape: pl.BlockSpec(shape, lambda i: (0,) * len(shape))
    return _grid_call(
        body, [glu, glu, glu, cw, cb, lw, lb], carry, name=name, grid=(nt,),
        in_specs=_halo_specs(ts, GLU_COLS, rows, col) + [full((CONV_K, CONV_CH))] + [_vec(CONV_CH)] * 3,
        out_specs=[_row(ts, CONV_CH), _row(ts, CONV_CH)],
        out_shape=[jax.ShapeDtypeStruct((rows, CONV_CH), MXU_DT), jax.ShapeDtypeStruct((rows, CONV_CH), F32)],
        scratch_shapes=[pltpu.VMEM((te, CONV_CH), F32), pltpu.VMEM((7, te - 8, CONV_CH), F32)],
        dims=("parallel",))


def _conv_bwd(glu, col, u1, du3, dproj, cw, lw, lb, *, carry=None, name):
    rows = glu.shape[0]
    ts = min(rows, CONV_BWD_TS)
    nt = rows // ts
    te = ts + 2 * HALO

    def du1_of(u1v, du3v, lw_v, lb_v):
        xh, rstd = _ln_stats(u1v)
        u2 = xh * lw_v + lb_v
        sg = _sigmoid(u2)
        du2 = du3v * (sg * (1.0 + u2 * (1.0 - sg)))
        dxh = du2 * lw_v
        du1 = rstd * (dxh - jnp.mean(dxh, axis=-1, keepdims=True)
                      - xh * jnp.mean(dxh * xh, axis=-1, keepdims=True))
        return du1, du2, xh

    half = ts // 2

    def body(gp_ref, g_ref, gn_ref, up_ref, u_ref, un_ref, dp_ref, d_ref, dn_ref, cw_ref, lw_ref, lb_ref,
             _, dglu_ref, dcw_ref, dvec_ref, u0_ref, du1_ref, pu_ref, pd_ref, du0_ref):
        i = pl.program_id(0)
        lw_v, lb_v = lw_ref[...], lb_ref[...]

        @pl.when(i == 0)
        def _():
            dcw_ref[...] = jnp.zeros_like(dcw_ref)
            dvec_ref[...] = jnp.zeros_like(dvec_ref)

        gv = g_ref[...]
        u0_ref[0:HALO, :] = jnp.where(i > 0, _glu(gp_ref[...]), 0.0)
        u0_ref[HALO:HALO + ts, :] = _glu(gv)
        u0_ref[HALO + ts:, :] = jnp.where(i < nt - 1, _glu(gn_ref[...]), 0.0)
        d_prev, _, _ = du1_of(up_ref[...], dp_ref[...], lw_v, lb_v)
        d_main, du2, xh = du1_of(u_ref[...], d_ref[...], lw_v, lb_v)
        d_next, _, _ = du1_of(un_ref[...], dn_ref[...], lw_v, lb_v)
        du1_ref[0:HALO, :] = jnp.where(i > 0, d_prev, 0.0)
        du1_ref[HALO:HALO + ts, :] = d_main
        du1_ref[HALO + ts:, :] = jnp.where(i < nt - 1, d_next, 0.0)

        rid = lax.broadcasted_iota(jnp.int32, (8, CONV_CH), 0)
        dvec_ref[...] += (jnp.where(rid == 0, _colsum(d_main), 0.0)
                          + jnp.where(rid == 1, _colsum(du2 * xh), 0.0)
                          + jnp.where(rid == 2, _colsum(du2), 0.0))
        _phases(u0_ref, pu_ref)
        _phases(du1_ref, pd_ref)
        for c in range(CONV_CH // LANES):
            cs = slice(c * LANES, (c + 1) * LANES)
            for r0 in (0, half):
                dm = du1_ref[HALO + r0:HALO + r0 + half, cs]
                acc = jnp.zeros((half, LANES), F32)
                for j in range(CONV_K):
                    acc = acc + cw_ref[j:j + 1, cs] * _window(du1_ref, pd_ref, r0 + HALO + CONV_PAD - j, half, cs)
                    prod = dm * _window(u0_ref, pu_ref, r0 + HALO - CONV_PAD + j, half, cs)
                    dcw_ref[j, :, cs] += jnp.sum(prod.reshape(half // 8, 8, LANES), axis=0)
                du0_ref[r0:r0 + half, cs] = acc
        du0 = du0_ref[...]
        ga, sg = gv[:, :CONV_CH], _sigmoid(gv[:, CONV_CH:])
        dglu_ref[:, :CONV_CH] = (du0 * sg).astype(dglu_ref.dtype)
        dglu_ref[:, CONV_CH:] = (du0 * ga * sg * (1.0 - sg)).astype(dglu_ref.dtype)

    full = lambda shape: pl.BlockSpec(shape, lambda i: (0,) * len(shape))
    return _grid_call(
        body, [glu, glu, glu, u1, u1, u1, du3, du3, du3, cw, lw, lb, dproj], carry, name=name, grid=(nt,),
        in_specs=(_halo_specs(ts, GLU_COLS, rows, col) + _halo_specs(ts, CONV_CH, rows)
                  + _halo_specs(ts, CONV_CH, rows) + [full((CONV_K, CONV_CH)), _vec(CONV_CH), _vec(CONV_CH)]
                  + [pl.BlockSpec(memory_space=pl.ANY)]),
        out_specs=[_row(ts, GLU_COLS, col), full((CONV_K, 8, CONV_CH)), full((8, CONV_CH))],
        out_shape=[jax.ShapeDtypeStruct(dproj.shape, dproj.dtype),
                   jax.ShapeDtypeStruct((CONV_K, 8, CONV_CH), F32), jax.ShapeDtypeStruct((8, CONV_CH), F32)],
        scratch_shapes=[pltpu.VMEM((te, CONV_CH), F32), pltpu.VMEM((te, CONV_CH), F32),
                        pltpu.VMEM((7, te - 8, CONV_CH), F32), pltpu.VMEM((7, te - 8, CONV_CH), F32),
                        pltpu.VMEM((ts, CONV_CH), F32)],
        input_output_aliases={12: 0}, dims=("arbitrary",))


FFN_CW = 1408
FFN_NJ = FFN_H // FFN_CW


def _ffn_halo_specs(ts, rows, col_of, inner_rows):
    per = ts // HALO
    last = rows // HALO - 1
    if inner_rows:
        return [pl.BlockSpec((HALO, FFN_CW), lambda j, i: (jnp.maximum(i * per - 1, 0), col_of(j))),
                pl.BlockSpec((ts, FFN_CW), lambda j, i: (i, col_of(j))),
                pl.BlockSpec((HALO, FFN_CW), lambda j, i: (jnp.minimum((i + 1) * per, last), col_of(j)))]
    return [pl.BlockSpec((HALO, FFN_CW), lambda i, j: (jnp.maximum(i * per - 1, 0), col_of(j))),
            pl.BlockSpec((ts, FFN_CW), lambda i, j: (i, col_of(j))),
            pl.BlockSpec((HALO, FFN_CW), lambda i, j: (jnp.minimum((i + 1) * per, last), col_of(j)))]


def _ffn_ext(p_ref, m_ref, n_ref, sl, i, nt):
    return jnp.concatenate([jnp.where(i > 0, p_ref[:, sl].astype(F32), 0.0), m_ref[:, sl].astype(F32),
                            jnp.where(i < nt - 1, n_ref[:, sl].astype(F32), 0.0)], axis=0)


def _prev_row(v):
    return pltpu.roll(v, 1, 0)


def _next_row(v):
    return pltpu.roll(v, v.shape[0] - 1, 0)


def _ffn_act(up0, w3, b3, *, name):
    rows = up0.shape[0]
    ts = min(rows, 256)
    nt = rows // ts
    main = slice(HALO, HALO + ts)

    def body(gp, g, gn, vp, v, vn, wg, wv, bg, bv, a_ref, go_ref, vo_ref):
        i = pl.program_id(0)
        for ch in range(FFN_CW // LANES):
            sl = slice(ch * LANES, (ch + 1) * LANES)
            xg, xv = _ffn_ext(gp, g, gn, sl, i, nt), _ffn_ext(vp, v, vn, sl, i, nt)
            wgv, wvv = wg[:, sl], wv[:, sl]
            gate = (wgv[0:1] * _prev_row(xg) + wgv[1:2] * xg + wgv[2:3] * _next_row(xg))[main] + bg[:, sl]
            val = (wvv[0:1] * _prev_row(xv) + wvv[1:2] * xv + wvv[2:3] * _next_row(xv))[main] + bv[:, sl]
            a_ref[:, sl] = (gate * _sigmoid(gate) * val).astype(a_ref.dtype)
            go_ref[:, sl] = gate.astype(go_ref.dtype)
            vo_ref[:, sl] = val.astype(vo_ref.dtype)

    gcol, vcol = (lambda j: j), (lambda j: j + FFN_NJ)
    wspec = lambda col_of: pl.BlockSpec((FFN_K, FFN_CW), lambda i, j: (0, col_of(j)))
    bspec = lambda col_of: pl.BlockSpec((1, FFN_CW), lambda i, j: (0, col_of(j)))
    ospec = pl.BlockSpec((ts, FFN_CW), lambda i, j: (i, j))
    return pl.pallas_call(
        body, name=name, grid=(nt, FFN_NJ),
        in_specs=(_ffn_halo_specs(ts, rows, gcol, False) + _ffn_halo_specs(ts, rows, vcol, False)
                  + [wspec(gcol), wspec(vcol), bspec(gcol), bspec(vcol)]),
        out_specs=[ospec] * 3, out_shape=[jax.ShapeDtypeStruct((rows, FFN_H), MXU_DT)] * 3,
        compiler_params=_cp("parallel", "parallel"),
    )(up0, up0, up0, up0, up0, up0, w3, w3, b3, b3)


def _ffn_act_bwd(up0, gate_s, val_s, da, w3, *, carry=None, name):
    rows = up0.shape[0]
    ts = min(rows, 256)
    nt = rows // ts
    main = slice(HALO, HALO + ts)

    def body(gp, g, gn, vp, v, vn, sgp, sg, sgn, svp, sv, svn, ap, a, an, wg, wv,
             dg_ref, dv_ref, sg_ref, sv_ref):
        i = pl.program_id(1)

        @pl.when(i == 0)
        def _():
            sg_ref[...] = jnp.zeros_like(sg_ref)
            sv_ref[...] = jnp.zeros_like(sv_ref)

        rid = lax.broadcasted_iota(jnp.int32, (8, LANES), 0)
        for ch in range(FFN_CW // LANES):
            sl = slice(ch * LANES, (ch + 1) * LANES)
            xg, xv, da_e = _ffn_ext(gp, g, gn, sl, i, nt), _ffn_ext(vp, v, vn, sl, i, nt), _ffn_ext(ap, a, an, sl, i, nt)
            gate, val = _ffn_ext(sgp, sg, sgn, sl, i, nt), _ffn_ext(svp, sv, svn, sl, i, nt)
            wgv, wvv = wg[:, sl], wv[:, sl]
            xg_p, xg_n, xv_p, xv_n = _prev_row(xg), _next_row(xg), _prev_row(xv), _next_row(xv)
            sgm = _sigmoid(gate)
            eg = da_e * val * (sgm * (1.0 + gate * (1.0 - sgm)))
            ev = da_e * (gate * sgm)
            for e, taps, w, d_ref, s_ref in ((eg, (xg_p, xg, xg_n), wgv, dg_ref, sg_ref),
                                             (ev, (xv_p, xv, xv_n), wvv, dv_ref, sv_ref)):
                d0 = w[0:1] * _next_row(e) + w[1:2] * e + w[2:3] * _prev_row(e)
                d_ref[:, sl] = d0[main].astype(d_ref.dtype)
                dm = e[main]
                s_ref[:, sl] += (jnp.where(rid == 0, _colsum(dm * taps[0][main]), 0.0)
                                 + jnp.where(rid == 1, _colsum(dm * taps[1][main]), 0.0)
                                 + jnp.where(rid == 2, _colsum(dm * taps[2][main]), 0.0)
                                 + jnp.where(rid == 3, _colsum(dm), 0.0))

    gcol, vcol = (lambda j: j), (lambda j: j + FFN_NJ)
    wspec = lambda col_of: pl.BlockSpec((FFN_K, FFN_CW), lambda j, i: (0, col_of(j)))
    ospec = pl.BlockSpec((ts, FFN_CW), lambda j, i: (i, j))
    sspec = pl.BlockSpec((8, FFN_CW), lambda j, i: (0, j))
    return _grid_call(
        body, [up0] * 6 + [gate_s] * 3 + [val_s] * 3 + [da] * 3 + [w3, w3], carry, name=name, grid=(FFN_NJ, nt),
        in_specs=(_ffn_halo_specs(ts, rows, gcol, True) + _ffn_halo_specs(ts, rows, vcol, True)
                  + _ffn_halo_specs(ts, rows, gcol, True) * 3 + [wspec(gcol), wspec(vcol)]),
        out_specs=[ospec, ospec, sspec, sspec],
        out_shape=[jax.ShapeDtypeStruct((rows, FFN_H), MXU_DT), jax.ShapeDtypeStruct((rows, FFN_H), MXU_DT),
                   jax.ShapeDtypeStruct((8, FFN_H), F32), jax.ShapeDtypeStruct((8, FFN_H), F32)],
        dims=("parallel", "arbitrary"))


def _adam_math(w, g, m, v):
    m = ADAM_B1 * m + (1.0 - ADAM_B1) * g
    v = ADAM_B2 * v + (1.0 - ADAM_B2) * (g * g)
    m_hat = m / (1.0 - ADAM_B1 ** ADAM_STEP)
    v_hat = v / (1.0 - ADAM_B2 ** ADAM_STEP)
    delta = -ADAM_LR * (m_hat / (jnp.sqrt(v_hat) + ADAM_EPS) + ADAM_WD * w)
    return delta, m, v


ROW_TILE_BYTES = 8 << 20


def _row_tile(rows, row_bytes):
    tiles = [rows] + [rows // k for k in range(2, rows // 16 + 1) if rows % k == 0 and (rows // k) % 16 == 0]
    return next(t for t in tiles if t * row_bytes <= ROW_TILE_BYTES)


def _adam(w, m, v, parts, *, name):
    rows, cols = w.shape
    nparts = parts.shape[0]
    tr = _row_tile(rows, cols * (7 * 4 + nparts * parts.dtype.itemsize))

    def body(w_ref, m_ref, v_ref, p_ref, g_ref, d_ref, nm_ref, nv_ref):
        g = p_ref[0].astype(F32)
        for p in range(1, nparts):
            g = g + p_ref[p].astype(F32)
        g_ref[...] = g
        d_ref[...], nm_ref[...], nv_ref[...] = _adam_math(w_ref[...], g, m_ref[...], v_ref[...])

    spec = _row(tr, cols)
    return pl.pallas_call(
        body, name=name, grid=(rows // tr,),
        in_specs=[spec, spec, spec, pl.BlockSpec((nparts, tr, cols), lambda i: (0, i, 0))],
        out_specs=[spec] * 4, out_shape=[jax.ShapeDtypeStruct((rows, cols), F32)] * 4,
        compiler_params=_cp("parallel"),
    )(w, m, v, parts)


def _adam_many(ws, ms, vs, gs, *, name):
    n = len(ws)

    def body(*refs):
        ins, outs = refs[:4 * n], refs[4 * n:]
        for k in range(n):
            delta, new_m, new_v = _adam_math(ins[k][...], ins[3 * n + k][...], ins[n + k][...], ins[2 * n + k][...])
            outs[k][...], outs[n + k][...], outs[2 * n + k][...] = delta, new_m, new_v

    vm = pl.BlockSpec(memory_space=pltpu.VMEM)
    res = pl.pallas_call(
        body, name=name, in_specs=[vm] * (4 * n), out_specs=[vm] * (3 * n),
        out_shape=[jax.ShapeDtypeStruct(w.shape, F32) for w in ws] * 3,
        compiler_params=pltpu.CompilerParams(vmem_limit_bytes=VMEM_LIMIT),
    )(*ws, *ms, *vs, *gs)
    return res[:n], res[n:2 * n], res[2 * n:]


def _sum_parts(parts, *, name):
    nparts, rows, cols = parts.shape
    tr = _row_tile(rows, cols * (4 + nparts * parts.dtype.itemsize))

    def body(p_ref, o_ref):
        g = p_ref[0].astype(F32)
        for p in range(1, nparts):
            g = g + p_ref[p].astype(F32)
        o_ref[...] = g

    return pl.pallas_call(
        body, name=name, grid=(rows // tr,),
        in_specs=[pl.BlockSpec((nparts, tr, cols), lambda i: (0, i, 0))], out_specs=_row(tr, cols),
        out_shape=jax.ShapeDtypeStruct((rows, cols), F32), compiler_params=_cp("parallel"),
    )(parts)


def _my_place():
    return lax.axis_index("x"), lax.axis_index("y"), lax.axis_index("c")


def _dev_index(p):
    return 4 * p[0] + 2 * p[1] + p[2]


def _all_gather(xs, *, name):
    return _run_comm(_gather_plan(xs), pltpu.VMEM, name)


class _Comm(NamedTuple):
    ins: list
    outs: list
    n_remote: int
    n_local: int
    start: Callable
    finish: Callable


def _join_plans(*plans):
    def split(in_refs, out_refs, send_sems, recv_sems, local_sems):
        i = o = r = l = 0
        for p in plans:
            ni, no = len(p.ins), len(p.outs)
            yield p, (in_refs[i:i + ni], out_refs[o:o + no], send_sems.at[pl.ds(r, p.n_remote)],
                      recv_sems.at[pl.ds(r, p.n_remote)], local_sems.at[pl.ds(l, p.n_local)])
            i, o, r, l = i + ni, o + no, r + p.n_remote, l + p.n_local

    def start(*refs):
        for p, part in split(*refs):
            p.start(*part)

    def finish(*refs):
        for p, part in split(*refs):
            p.finish(*part)

    return _Comm([v for p in plans for v in p.ins], [v for p in plans for v in p.outs],
                 sum(p.n_remote for p in plans), sum(p.n_local for p in plans), start, finish)


def _comm_scratch(plan):
    return [pltpu.SemaphoreType.DMA((plan.n_remote,)), pltpu.SemaphoreType.DMA((plan.n_remote,)),
            pltpu.SemaphoreType.DMA((plan.n_local,))]


def _run_comm(plan, space, name):
    n_in, n_out = len(plan.ins), len(plan.outs)

    def body(*refs):
        args = (refs[:n_in], refs[n_in:n_in + n_out], *refs[n_in + n_out:])
        plan.start(*args)
        plan.finish(*args)

    return pl.pallas_call(
        body, name=name, out_shape=plan.outs,
        in_specs=[pl.BlockSpec(memory_space=space)] * n_in, out_specs=[pl.BlockSpec(memory_space=space)] * n_out,
        scratch_shapes=_comm_scratch(plan),
        compiler_params=pltpu.CompilerParams(vmem_limit_bytes=VMEM_LIMIT),
    )(*plan.ins)


def _grid_call(body, ins, carry, *, name, grid, in_specs, out_specs, out_shape, dims, scratch_shapes=(),
               input_output_aliases=None):
    if carry is None:
        res = pl.pallas_call(
            body, name=name, grid=grid, in_specs=list(in_specs), out_specs=list(out_specs),
            out_shape=list(out_shape), scratch_shapes=list(scratch_shapes),
            input_output_aliases=input_output_aliases or {}, compiler_params=_cp(*dims))(*ins)
        return list(res), None

    def at(pos):
        conds = [pl.program_id(k) == p for k, p in enumerate(pos)]
        out = conds[0]
        for cnd in conds[1:]:
            out = jnp.logical_and(out, cnd)
        return out

    return _carried_call(body, carry, lambda: at([0] * len(grid)), lambda: at([g - 1 for g in grid]), ins,
                         name=name, grid=grid, in_specs=in_specs, out_specs=out_specs, out_shape=out_shape,
                         scratch_shapes=scratch_shapes, input_output_aliases=input_output_aliases)


def _carried_call(body, plan, first, last, ins, *, name, grid, in_specs, out_specs, out_shape, scratch_shapes=(),
                  input_output_aliases=None):
    in_specs, out_specs, out_shape = list(in_specs), list(out_specs), list(out_shape)
    n_in, n_out, n_scr = len(in_specs), len(out_specs), len(scratch_shapes)
    c_in, c_out = len(plan.ins), len(plan.outs)
    hbm = pl.BlockSpec(memory_space=pl.ANY)

    def full_body(*refs):
        ins, c_ins = refs[:n_in], refs[n_in:n_in + c_in]
        outs = refs[n_in + c_in:n_in + c_in + n_out]
        c_outs = refs[n_in + c_in + n_out:n_in + c_in + n_out + c_out]
        scr = refs[n_in + c_in + n_out + c_out:]
        sems = scr[n_scr:]

        @pl.when(first())
        def _():
            plan.start(c_ins, c_outs, *sems)

        body(*ins, *outs, *scr[:n_scr])

        @pl.when(last())
        def _():
            plan.finish(c_ins, c_outs, *sems)

    res = pl.pallas_call(
        full_body, name=name, grid=grid,
        in_specs=in_specs + [hbm] * c_in, out_specs=out_specs + [hbm] * c_out,
        out_shape=out_shape + list(plan.outs),
        scratch_shapes=list(scratch_shapes) + _comm_scratch(plan),
        input_output_aliases=input_output_aliases or {},
        compiler_params=_cp(*(["arbitrary"] * len(grid))),
    )(*ins, *plan.ins)
    return list(res[:n_out]), list(res[n_out:])


def _gather_plan(xs):
    n = len(xs)
    ms = [v.shape[0] for v in xs]

    def tools(x_refs, o_refs, send_sems, recv_sems, local_sems):
        x, y, c = _my_place()
        me, sib = (x, y, c), (x, y, 1 - c)
        chips = [(1 - x, y), (x, 1 - y), (1 - x, 1 - y)]

        def rows(a, p):
            return o_refs[a].at[pl.ds(pl.multiple_of(_dev_index(p) * ms[a], 8), ms[a])]

        def copy(a, k, block, to, src=None):
            return pltpu.make_async_remote_copy(
                src_ref=rows(a, block) if src is None else src, dst_ref=rows(a, block),
                send_sem=send_sems.at[a * 7 + k], recv_sem=recv_sems.at[a * 7 + k],
                device_id=to, device_id_type=MESH)

        mine = [pltpu.make_async_copy(x_refs[a], rows(a, me), local_sems.at[a]) for a in range(n)]
        first = []
        for a in range(n):
            first.append(copy(a, 0, me, sib, src=x_refs[a]))
            first += [copy(a, 1 + j, me, (*chip, c), src=x_refs[a]) for j, chip in enumerate(chips)]
        return me, sib, chips, c, copy, mine, first

    def start(*refs):
        _, _, _, _, _, mine, first = tools(*refs)
        for cp in mine + first:
            cp.start()

    def finish(*refs):
        me, sib, chips, c, copy, mine, first = tools(*refs)
        passed = []
        for j, chip in enumerate(chips):
            for a in range(n):
                copy(a, 1 + j, (*chip, c), me).wait_recv()
                cp = copy(a, 4 + j, (*chip, c), sib)
                cp.start()
                passed.append(cp)
        for a in range(n):
            copy(a, 0, sib, me).wait_recv()
            for j, chip in enumerate(chips):
                copy(a, 4 + j, (*chip, 1 - c), me).wait_recv()
        for cp in first + passed:
            cp.wait_send()
        for cp in mine:
            cp.wait()

    outs = [jax.ShapeDtypeStruct((N_DEV * v.shape[0], v.shape[1]), v.dtype) for v in xs]
    return _Comm(list(xs), outs, 7 * n, n, start, finish)


def _exchange_plan(gs, cols=None):
    n = len(gs)
    rs = [v.shape[0] // N_DEV for v in gs]
    flips = [(bx, by, bc) for bx in (0, 1) for by in (0, 1) for bc in (0, 1)][1:]

    def tools(g_refs, o_refs, send_sems, recv_sems, local_sems):
        x, y, c = _my_place()
        me = (x, y, c)

        def block(ref, a, p):
            return ref.at[pl.ds(_dev_index(p) * rs[a], rs[a])]

        def src(a, p):
            rows = pl.ds(_dev_index(p) * rs[a], rs[a])
            return g_refs[a].at[rows] if cols is None else g_refs[a].at[rows, pl.ds(cols[0], cols[1])]

        def peer(f):
            return (1 - x if f[0] else x, 1 - y if f[1] else y, 1 - c if f[2] else c)

        def copy(a, k, to):
            return pltpu.make_async_remote_copy(
                src_ref=src(a, to), dst_ref=block(o_refs[a], a, me),
                send_sem=send_sems.at[a * 7 + k], recv_sem=recv_sems.at[a * 7 + k],
                device_id=to, device_id_type=MESH)

        def arrival(a, k, frm):
            return pltpu.make_async_remote_copy(
                src_ref=src(a, frm), dst_ref=block(o_refs[a], a, frm),
                send_sem=send_sems.at[a * 7 + k], recv_sem=recv_sems.at[a * 7 + k],
                device_id=frm, device_id_type=MESH)

        mine = [pltpu.make_async_copy(src(a, me), block(o_refs[a], a, me), local_sems.at[a])
                for a in range(n)]
        sends = [copy(a, k, peer(f)) for a in range(n) for k, f in enumerate(flips)]
        arrivals = [arrival(a, k, peer(f)) for a in range(n) for k, f in enumerate(flips)]
        return mine, sends, arrivals

    def start(*refs):
        mine, sends, _ = tools(*refs)
        for cp in mine + sends:
            cp.start()

    def finish(*refs):
        mine, sends, arrivals = tools(*refs)
        for cp in arrivals:
            cp.wait_recv()
        for cp in sends:
            cp.wait_send()
        for cp in mine:
            cp.wait()

    outs = [jax.ShapeDtypeStruct((v.shape[0], v.shape[1] if cols is None else cols[1]), v.dtype) for v in gs]
    return _Comm(list(gs), outs, 7 * n, n, start, finish)


def _rope_tables(seq):
    t = jnp.arange(seq)
    row, col = t // GRID_W, t % GRID_W
    half = HEAD_DIM // 2
    inv = ROPE_BASE ** (-jnp.arange(0, half, 2, dtype=F32) / half)
    ang_r = row.astype(F32)[:, None] * inv
    ang_c = col.astype(F32)[:, None] * inv
    ang = jnp.concatenate([ang_r, ang_r, ang_c, ang_c], axis=-1)
    return jnp.tile(jnp.cos(ang), (1, 2)), jnp.tile(jnp.sin(ang), (1, 2))


def _to_slots(w):
    return w.reshape(N_KV_HEADS, GQA_GROUP, HEAD_DIM, w.shape[1]).transpose(1, 0, 2, 3).reshape(w.shape)


def _from_slots(w):
    return w.reshape(GQA_GROUP, N_KV_HEADS, HEAD_DIM, w.shape[1]).transpose(1, 0, 2, 3).reshape(w.shape)


def _pack(vs):
    flat = jnp.concatenate([v.reshape(-1).astype(F32) for v in vs])
    total = -(-flat.shape[0] // (8 * LANES)) * (8 * LANES)
    return jnp.pad(flat, (0, total - flat.shape[0])).reshape(-1, LANES)


def _unpack(packed, like):
    flat, out, off = packed.reshape(-1), [], 0
    for v in like:
        size = math.prod(v.shape)
        out.append(flat[off:off + size].reshape(v.shape))
        off += size
    return out


def _silu(v):
    return v * jax.nn.sigmoid(v)


def kernel(x, c, ctx, c_ctx, w_mod, b_mod, norm_mix_w, w_in, q_norm_w, k_norm_w, sink_logit, conv_w, conv_b, conv_norm_w, conv_norm_b, w_out, norm_ffn_w, w_up, ffn_conv_w, ffn_conv_b, w_down, loss_target, m_c_ctx, m_w_mod, m_b_mod, m_norm_mix_w, m_w_in, m_q_norm_w, m_k_norm_w, m_sink_logit, m_conv_w, m_conv_b, m_conv_norm_w, m_conv_norm_b, m_w_out, m_norm_ffn_w, m_w_up, m_ffn_conv_w, m_ffn_conv_b, m_w_down, v_c_ctx, v_w_mod, v_b_mod, v_norm_mix_w, v_w_in, v_q_norm_w, v_k_norm_w, v_sink_logit, v_conv_w, v_conv_b, v_conv_norm_w, v_conv_norm_b, v_w_out, v_norm_ffn_w, v_w_up, v_ffn_conv_w, v_ffn_conv_b, v_w_down):
    d = D_MODEL
    seq, n_ctx = x.shape[1], ctx.shape[1]
    me = _dev_index(_my_place())
    xs, ctxs, tgt = x[0], ctx[0], loss_target[0]

    small = _pack([c[0], conv_w[0], ffn_conv_w[0]])
    small_all = _all_gather([small], name="gather_small")[0].reshape(N_DEV, -1)
    n_cw, n_fw = conv_w[0].size, ffn_conv_w[0].size
    c_all = small_all[:, :d]
    cw_all = small_all[:, d:d + n_cw].reshape(N_DEV, CONV_K, -1)
    fw_all = small_all[:, d + n_cw:d + n_cw + n_fw].reshape(N_DEV, FFN_K, -1)
    conv_w_f = cw_all.transpose(1, 0, 2).reshape(CONV_K, CONV_CH)
    ffn_w_f = fw_all.transpose(1, 0, 2).reshape(FFN_K, 2 * FFN_H)

    mcols = w_mod.shape[2]
    act = jnp.zeros((16, d), F32).at[:N_DEV].set(_silu(c_all)).at[N_DEV].set(_silu(c_ctx))
    mod_part = _mm(act, w_mod[0], 16, mcols, d, name="mod_fwd")
    mod_all = _all_gather([mod_part], name="gather_mod")[0]
    mod_all = mod_all.reshape(N_DEV, 16, mcols).transpose(1, 0, 2).reshape(16, 6 * d) + b_mod
    mod = lax.dynamic_slice_in_dim(mod_all, me, 1, axis=0)
    sh1, sc1, g1, sh2, sc2, g2 = [mod[:, k * d:(k + 1) * d] for k in range(6)]
    sh1c, sc1c = mod_all[N_DEV:N_DEV + 1, :d], mod_all[N_DEV:N_DEV + 1, d:2 * d]

    cos, sin = _rope_tables(seq)
    ones_c, zeros_c = jnp.ones((n_ctx, LANES), F32), jnp.zeros((n_ctx, LANES), F32)
    qk_w = jnp.concatenate([jnp.tile(q_norm_w, (1, N_Q_HEADS)), jnp.tile(k_norm_w, (1, N_KV_HEADS))], axis=1)
    kc_w = jnp.tile(k_norm_w, (1, N_KV_HEADS))

    h, (w_in_t,) = _prenorm(xs, norm_mix_w, sc1, sh1, carry=_gather_plan([w_in[0].T.astype(MXU_DT)]),
                            name="prenorm_mix")
    hc, _ = _prenorm(ctxs, norm_mix_w, sc1c, sh1c, name="prenorm_ctx")
    w_in_p = jnp.concatenate([_to_slots(w_in_t[:Q_COLS]), w_in_t[Q_COLS:QKV_COLS],
                              jnp.zeros((GLU_OFF - QKV_COLS, d), MXU_DT), w_in_t[QKV_COLS:]])
    proj = _mm(h, w_in_p, seq, IN_PAD, d, tb=True, name="proj_in")
    kv_ctx = _mm(hc, w_in_p, n_ctx, 2 * KV_COLS, d, tb=True, n0=Q_COLS, name="proj_ctx")
    qk_r = _qk_prep(proj, QK_COLS, cos, sin, qk_w, name="qk_prep")
    kc_n = _qk_prep(kv_ctx, KV_COLS, ones_c, zeros_c, kc_w, name="k_ctx_prep")
    (attn_o, lse), (w_up_t, w_out_f) = _attn_fwd(
        sink_logit, qk_r, proj, kc_n, kv_ctx, name="attn_fwd",
        carry=_gather_plan([w_up[0].T.astype(MXU_DT), w_out[0].astype(MXU_DT)]))
    w_out_f = jnp.concatenate([_to_slots(w_out_f[:Q_COLS]), w_out_f[Q_COLS:]])
    (u3, u1), (w_down_f,) = _conv_fwd(proj, GLU_OFF // GLU_COLS, conv_w_f, conv_b, conv_norm_w, conv_norm_b,
                                      carry=_gather_plan([w_down[0].astype(MXU_DT)]), name="conv_fwd")
    mix = _mm(attn_o, w_out_f, seq, d, Q_COLS, name="out_attn")
    mix = _mm(u3, w_out_f, seq, d, CONV_CH, k0=Q_COLS, add=mix, name="out_conv")

    x1, h2 = _resid_prenorm(xs, mix, g1, norm_ffn_w, sc2, sh2, name="prenorm_ffn")
    up0 = _mm(h2, w_up_t, seq, 2 * FFN_H, d, tb=True, out_dtype=MXU_DT, name="ffn_up")
    act_a, gate_s, val_s = _ffn_act(up0, ffn_w_f, ffn_conv_b, name="ffn_act")
    loss_p, dy, dffn, dg2 = _ffn_down_loss(act_a, w_down_f, x1, g2, tgt, name="ffn_down_loss")

    da = _mm(dffn, w_down_f, seq, FFN_H, d, tb=True, out_dtype=MXU_DT, name="ffn_down_dx")
    gw_down = _mm(act_a, dffn, FFN_H, d, seq, ta=True, out_dtype=MXU_DT, name="ffn_down_dw")
    (dgate0, dval0, s_gate, s_val), (rx_down,) = _ffn_act_bwd(
        up0, gate_s, val_s, da, ffn_w_f, carry=_exchange_plan([gw_down]), name="ffn_act_bwd")
    gw_up_t = _mm(dgate0, h2, FFN_H, d, seq, ta=True, out_dtype=MXU_DT, out_rows=2 * FFN_H, name="ffn_up_dw_gate")
    gw_up_t = _mm(dval0, h2, FFN_H, d, seq, ta=True, into=gw_up_t, m0=FFN_H, name="ffn_up_dw_val")
    qd = d // 4
    dh2, (rx_up_a,) = _mm(dgate0, w_up_t, seq, d, FFN_H, a2=dval0, out_dtype=MXU_DT, name="ffn_up_dx",
                          carry=_exchange_plan([gw_up_t], cols=(0, qd)))
    (dx1, dmix, dsh2, dsc2, dnw2, dg1), _ = _norm_bwd(
        dh2, x1, norm_ffn_w, sc2, res=dy, gate=(mix, g1), name="prenorm_ffn_bwd")

    dattn = _mm(dmix, w_out_f, seq, Q_COLS, d, tb=True, out_dtype=MXU_DT, name="out_dx_attn")
    du3 = _mm(dmix, w_out_f, seq, CONV_CH, d, tb=True, n0=Q_COLS, name="out_dx_conv")
    gw_out = _mm(attn_o, dmix, Q_COLS, d, seq, ta=True, out_dtype=MXU_DT, out_rows=Q_COLS + CONV_CH,
                 name="out_dw_attn")
    gw_out = _mm(u3, dmix, CONV_CH, d, seq, ta=True, into=gw_out, m0=Q_COLS, name="out_dw_conv")
    gw_out = jnp.concatenate([_from_slots(gw_out[:Q_COLS]), gw_out[Q_COLS:]])
    (dq, dk, dv, dkc_r, dvc, dsink_rows), (rx_out, rx_up_b) = _attn_bwd(
        sink_logit, qk_r, proj, kc_n, kv_ctx, attn_o, lse, dattn, name="attn_bwd",
        carry=_join_plans(_exchange_plan([gw_out]), _exchange_plan([gw_up_t], cols=(qd, qd))))
    dproj, dqk_w = _qk_prep_bwd(proj, [dq, dk], QK_COLS, cos, sin, qk_w, tail=dv, name="qk_prep_bwd")
    (dproj, dcw8, dvec), (rx_up_c,) = _conv_bwd(
        proj, GLU_OFF // GLU_COLS, u1, du3, dproj, conv_w_f, conv_norm_w, conv_norm_b,
        carry=_exchange_plan([gw_up_t], cols=(2 * qd, 2 * qd)), name="conv_bwd")
    dkc, dkc_w = _qk_prep_bwd(kv_ctx, [dkc_r], KV_COLS, ones_c, zeros_c, kc_w, name="k_ctx_prep_bwd")
    dkv_ctx = jnp.concatenate([dkc, dvc.astype(MXU_DT)], axis=1)
    gw_p = _mm(dproj, h, IN_PAD, d, seq, ta=True, name="proj_dw")
    gw_ctx = _mm(dkv_ctx, hc, 2 * KV_COLS, d, n_ctx, ta=True, name="proj_dw_ctx")
    gw_in_t = jnp.concatenate([_from_slots(gw_p[:Q_COLS]), gw_p[Q_COLS:QKV_COLS] + gw_ctx, gw_p[GLU_OFF:]],
                              axis=0).astype(MXU_DT)
    dh, (rx_in,) = _mm(dproj, w_in_p, seq, d, IN_PAD, out_dtype=MXU_DT, carry=_exchange_plan([gw_in_t]),
                       name="proj_dx")
    dhc = _mm(dkv_ctx, w_in_p, n_ctx, d, 2 * KV_COLS, k0=Q_COLS, name="proj_dx_ctx")
    (grad_x, dsh1, dsc1, dnw1), _ = _norm_bwd(dh, xs, norm_mix_w, sc1, res=dx1, name="prenorm_mix_bwd")
    (dsh1c, dsc1c, dnw1c), _ = _norm_bwd(dhc, ctxs, norm_mix_w, sc1c, want_dx=False, name="prenorm_ctx_bwd")

    dmod = jnp.concatenate([dsh1, dsc1, dg1, dsh2, dsc2, dg2], axis=1)
    dmod_ctx = jnp.concatenate([dsh1c, dsc1c], axis=1)
    d_qn = dqk_w[0, :Q_COLS].reshape(N_Q_HEADS, HEAD_DIM).sum(0)
    d_kn = (dqk_w[0, Q_COLS:].reshape(N_KV_HEADS, HEAD_DIM).sum(0)
            + dkc_w[0].reshape(N_KV_HEADS, HEAD_DIM).sum(0))
    d_ffn_w = jnp.concatenate([s_gate[:FFN_K], s_val[:FFN_K]], axis=1)
    d_ffn_b = jnp.concatenate([s_gate[FFN_K], s_val[FFN_K]])
    d_sink = dsink_rows.reshape(N_Q_HEADS, BLOCK).sum(1)
    summed_like = [(dnw1 + dnw1c), d_qn[None], d_kn[None], d_sink[None], dvec[0:1], dvec[1:2],
                   dvec[2:3], dnw2, d_ffn_b[None], dcw8.sum(1), d_ffn_w, loss_p[0:1, 0:1]]
    pack = _pack([dmod, dmod_ctx] + summed_like)
    pack_all = _all_gather([pack], name="gather_small_grads")[0]
    pack_all = pack_all.reshape(N_DEV, pack.shape[0], LANES)
    tot = _sum_parts(pack_all, name="sum_small_grads")
    (dmod_sum, dmc_sum, g_nmix, g_qn, g_kn, g_sink, g_cb, g_lw, g_lb, g_nffn, g_fb, g_cw_f, g_fw_f,
     loss_sum) = _unpack(tot, [dmod, dmod_ctx] + summed_like)
    loss = loss_sum[0, 0]
    dmod_all = pack_all.reshape(N_DEV, -1)[:, :6 * d]
    g_b_mod = dmod_sum.at[:, :2 * d].add(dmc_sum)

    lo = me * mcols
    dm_rows = jnp.zeros((16, 6 * d), F32).at[:N_DEV].set(dmod_all).at[N_DEV, :2 * d].set(dmc_sum[0])
    dm_mine = lax.dynamic_slice_in_dim(dm_rows, lo, mcols, axis=1)
    parts_mod = _mm(act, dm_mine, d, mcols, 16, ta=True, name="mod_dw")[None]
    dact_part = _mm(dm_mine[N_DEV:N_DEV + 8], w_mod[0], 8, d, mcols, tb=True, name="mod_dx_ctx")
    dact_all = _all_gather([dact_part], name="gather_c_ctx_grad")[0].reshape(N_DEV, 8, d)
    dact = _sum_parts(dact_all, name="sum_c_ctx_grad")[0]
    sg = jax.nn.sigmoid(c_ctx)
    g_c_ctx = dact * (sg * (1.0 + c_ctx * (1.0 - sg)))

    def stacked(rx):
        return rx.reshape(N_DEV, rx.shape[0] // N_DEV, rx.shape[1])

    g_w_in = _sum_parts(stacked(rx_in), name="sum_w_in").T[None]
    g_w_up = jnp.concatenate([_sum_parts(stacked(rx_up_a), name="sum_w_up_a"),
                              _sum_parts(stacked(rx_up_b), name="sum_w_up_b"),
                              _sum_parts(stacked(rx_up_c), name="sum_w_up_c")], axis=1).T[None]
    big = {}
    big["w_in"] = _adam(w_in[0], m_w_in[0], v_w_in[0], g_w_in, name="adam_w_in")
    big["w_up"] = _adam(w_up[0], m_w_up[0], v_w_up[0], g_w_up, name="adam_w_up")
    big["w_out"] = _adam(w_out[0], m_w_out[0], v_w_out[0], stacked(rx_out), name="adam_w_out")
    big["w_down"] = _adam(w_down[0], m_w_down[0], v_w_down[0], stacked(rx_down), name="adam_w_down")
    big["w_mod"] = _adam(w_mod[0], m_w_mod[0], v_w_mod[0], parts_mod, name="adam_w_mod")

    ccols, fcols = conv_w.shape[2], ffn_conv_w.shape[2]
    g_conv_w = lax.dynamic_slice_in_dim(g_cw_f, me * ccols, ccols, axis=1)[None]
    g_ffn_w = lax.dynamic_slice_in_dim(g_fw_f, me * fcols, fcols, axis=1)[None]
    names = ["c_ctx", "b_mod", "norm_mix_w", "q_norm_w", "k_norm_w", "sink_logit", "conv_w", "conv_b",
             "conv_norm_w", "conv_norm_b", "norm_ffn_w", "ffn_conv_w", "ffn_conv_b"]
    ws = [c_ctx, b_mod, norm_mix_w, q_norm_w, k_norm_w, sink_logit, conv_w, conv_b, conv_norm_w, conv_norm_b,
          norm_ffn_w, ffn_conv_w, ffn_conv_b]
    msm = [m_c_ctx, m_b_mod, m_norm_mix_w, m_q_norm_w, m_k_norm_w, m_sink_logit, m_conv_w, m_conv_b,
           m_conv_norm_w, m_conv_norm_b, m_norm_ffn_w, m_ffn_conv_w, m_ffn_conv_b]
    vsm = [v_c_ctx, v_b_mod, v_norm_mix_w, v_q_norm_w, v_k_norm_w, v_sink_logit, v_conv_w, v_conv_b,
           v_conv_norm_w, v_conv_norm_b, v_norm_ffn_w, v_ffn_conv_w, v_ffn_conv_b]
    gsm = [g_c_ctx, g_b_mod, g_nmix, g_qn, g_kn, g_sink, g_conv_w, g_cb, g_lw, g_lb, g_nffn, g_ffn_w, g_fb]
    deltas, new_ms, new_vs = _adam_many(ws, msm, vsm, gsm, name="adam_small")
    sm = {nm: vals for nm, vals in zip(names, zip(gsm, deltas, new_ms, new_vs))}

    def out4(nm):
        if nm in sm:
            return sm[nm]
        return tuple(t[None] for t in big[nm])

    order = ["c_ctx", "w_mod", "b_mod", "norm_mix_w", "w_in", "q_norm_w", "k_norm_w", "sink_logit", "conv_w",
             "conv_b", "conv_norm_w", "conv_norm_b", "w_out", "norm_ffn_w", "w_up", "ffn_conv_w", "ffn_conv_b",
             "w_down"]
    quads = [out4(nm) for nm in order]
    return (loss, grad_x[None], *[q[0] for q in quads], *[q[1] for q in quads],
            *[q[2] for q in quads], *[q[3] for q in quads])
```

```python
import math
from typing import Callable, NamedTuple

import jax
import jax.numpy as jnp
from jax import lax
from jax.experimental import pallas as pl
from jax.experimental.pallas import tpu as pltpu

F32 = jnp.float32
MXU_DT = jnp.bfloat16

D_MODEL = 1024
GRID_W = 64
HEAD_DIM = 64
N_Q_HEADS = 8
N_KV_HEADS = 2
GQA_GROUP = 4
WINDOW = 128
BLOCK = 128
Q_COLS = 512
KV_COLS = 128
QK_COLS = Q_COLS + KV_COLS
QKV_COLS = Q_COLS + 2 * KV_COLS
CONV_CH = 512
GLU_COLS = 2 * CONV_CH
IN_COLS = QKV_COLS + GLU_COLS
CONV_K = 31
CONV_PAD = 15
FFN_H = 2816
FFN_K = 3
ROPE_BASE = 10000.0
EPS = 1e-6
NEG_INF = -1e30
N_DEV = 8
HALO = 16
LANES = 128
ROW_TS = 512

ADAM_LR = 0.001
ADAM_B1 = 0.9
ADAM_B2 = 0.999
ADAM_EPS = 1e-08
ADAM_WD = 0.01
ADAM_STEP = 10

MESH = pl.DeviceIdType.MESH
VMEM_LIMIT = 56 << 20
MM_VMEM_BUDGET = 44 << 20
GLU_OFF = 1024
IN_PAD = GLU_OFF + GLU_COLS


def _cp(*dims):
    return pltpu.CompilerParams(dimension_semantics=dims or None, vmem_limit_bytes=VMEM_LIMIT)


def _row(ts, w, col=0):
    return pl.BlockSpec((ts, w), lambda i: (i, col))


def _vec(w):
    return pl.BlockSpec((1, w), lambda i: (0, 0))


def _colsum(v):
    return jnp.sum(v, axis=0, keepdims=True)


def _sigmoid(v):
    return 0.5 * jnp.tanh(0.5 * v) + 0.5


def _mm(a, b, m, n, k, *, ta=False, tb=False, n0=0, k0=0, add=None, out_dtype=F32, into=None, m0=0,
        out_rows=None, a2=None, carry=None, name):
    has_add, has_a2 = add is not None, a2 is not None
    assert not (has_a2 and (ta or tb))
    if into is not None:
        out_dtype = into.dtype
    sa, sb, so = a.dtype.itemsize, b.dtype.itemsize, jnp.dtype(out_dtype).itemsize
    sadd = add.dtype.itemsize if has_add else 0
    na = 2 if has_a2 else 1

    def fits(tm, tn):
        return 2 * (na * k * (tm * sa + tn * sb) + tm * tn * (so + sadd)) <= MM_VMEM_BUDGET

    tms = [m] if m <= 1024 else [t for t in (1024, 1408, 768, 512, 256, 128) if m % t == 0]
    tns = [t for t in ((1024, 512, 256, 128) if ta else (1408, 512, 256, 128)) if n % t == 0 and n0 % t == 0]
    tm, tn = next((tm, tn) for tm in tms for tn in tns if fits(tm, tn))
    a_spec = (pl.BlockSpec((k, tm), lambda i, j: (0, i)) if ta else pl.BlockSpec((tm, k), lambda i, j: (i, 0)))
    nb0 = n0 // tn
    if tb:
        assert k0 == 0
        b_spec = pl.BlockSpec((tn, k), lambda i, j: (j + nb0, 0))
    else:
        assert k0 % (na * k) == 0, (k0, k)
        kb0 = k0 // (na * k)
        b_spec = pl.BlockSpec((na * k, tn), lambda i, j: (kb0, j + nb0))
    assert m0 % tm == 0, (m0, tm)
    mb0 = m0 // tm
    o_spec = pl.BlockSpec((tm, tn), lambda i, j: (i + mb0, j))
    dims = (((0 if ta else 1,), (1 if tb else 0,)), ((), ()))

    def body(*refs):
        a_ref, b_ref, o_ref = refs[0], refs[na], refs[-1]
        if has_a2:
            res = (jnp.dot(a_ref[...].astype(MXU_DT), b_ref[0:k, :].astype(MXU_DT), preferred_element_type=F32)
                   + jnp.dot(refs[1][...].astype(MXU_DT), b_ref[k:2 * k, :].astype(MXU_DT),
                             preferred_element_type=F32))
        else:
            res = lax.dot_general(a_ref[...].astype(MXU_DT), b_ref[...].astype(MXU_DT), dims,
                                  preferred_element_type=F32)
        if has_add:
            res = res + refs[na + 1][...].astype(F32)
        o_ref[...] = res.astype(o_ref.dtype)

    ins = [a] + ([a2] if has_a2 else []) + [b] + ([add] if has_add else []) + ([into] if into is not None else [])
    specs = ([a_spec] * na + [b_spec] + ([pl.BlockSpec((tm, tn), lambda i, j: (i, j))] if has_add else [])
             + ([pl.BlockSpec(memory_space=pl.ANY)] if into is not None else []))
    out_shape = (jax.ShapeDtypeStruct(into.shape, into.dtype) if into is not None
                 else jax.ShapeDtypeStruct((out_rows or m, n), out_dtype))
    (out,), carried = _grid_call(
        body, ins, carry, name=name, grid=(m // tm, n // tn), in_specs=specs, out_specs=[o_spec],
        out_shape=[out_shape], input_output_aliases={len(ins) - 1: 0} if into is not None else None,
        dims=("parallel", "parallel"))
    return out if carry is None else (out, carried)


def _rms_stats(xv):
    r = lax.rsqrt(jnp.mean(xv * xv, axis=-1, keepdims=True) + EPS)
    return r, xv * r


def _prenorm(x, nw, sc, sh, *, carry=None, name):
    rows, d = x.shape
    ts = min(rows, ROW_TS)

    def body(x_ref, nw_ref, sc_ref, sh_ref, h_ref):
        _, xn = _rms_stats(x_ref[...])
        h_ref[...] = ((xn * nw_ref[...]) * (1.0 + sc_ref[...]) + sh_ref[...]).astype(h_ref.dtype)

    (h,), carried = _grid_call(
        body, [x, nw, sc, sh], carry, name=name, grid=(rows // ts,),
        in_specs=[_row(ts, d), _vec(d), _vec(d), _vec(d)], out_specs=[_row(ts, d)],
        out_shape=[jax.ShapeDtypeStruct((rows, d), MXU_DT)], dims=("parallel",))
    return h, carried


def _out_proj_prenorm(attn_o, u3, w_out, x, g1, nw, sc, sh, *, name):
    rows, d = x.shape
    ka, ku = attn_o.shape[1], u3.shape[1]
    tm = min(rows, ROW_TS)

    def body(a_ref, u_ref, w_ref, x_ref, g_ref, nw_ref, sc_ref, sh_ref, mix_ref, x1_ref, h_ref):
        mix = (jnp.dot(a_ref[...].astype(MXU_DT), w_ref[0:ka, :].astype(MXU_DT), preferred_element_type=F32)
               + jnp.dot(u_ref[...].astype(MXU_DT), w_ref[ka:ka + ku, :].astype(MXU_DT),
                         preferred_element_type=F32))
        mix_ref[...] = mix
        x1 = x_ref[...] + g_ref[...] * mix
        x1_ref[...] = x1
        _, xn = _rms_stats(x1)
        h_ref[...] = ((xn * nw_ref[...]) * (1.0 + sc_ref[...]) + sh_ref[...]).astype(h_ref.dtype)

    return pl.pallas_call(
        body, name=name, grid=(rows // tm,),
        in_specs=[_row(tm, ka), _row(tm, ku), pl.BlockSpec((ka + ku, d), lambda i: (0, 0)), _row(tm, d),
                  _vec(d), _vec(d), _vec(d), _vec(d)],
        out_specs=[_row(tm, d), _row(tm, d), _row(tm, d)],
        out_shape=[jax.ShapeDtypeStruct((rows, d), F32), jax.ShapeDtypeStruct((rows, d), F32),
                   jax.ShapeDtypeStruct((rows, d), MXU_DT)],
        compiler_params=_cp("parallel"),
    )(attn_o, u3, w_out, x, g1, nw, sc, sh)


def _ffn_down_loss(act, w_down, x1, g2, target, *, name):
    rows, d = x1.shape
    k = act.shape[1]
    tm, tn = min(rows, 1024), 512

    def body(a_ref, b_ref, x1_ref, g_ref, t_ref, loss_ref, dy_ref, dffn_ref, dg_ref):
        j, i = pl.program_id(0), pl.program_id(1)
        f = jnp.dot(a_ref[...].astype(MXU_DT), b_ref[...].astype(MXU_DT), preferred_element_type=F32)
        e = x1_ref[...] + g_ref[...] * f - t_ref[...]
        part = (0.5 / d) * jnp.sum(jnp.sum(e * e, axis=-1, keepdims=True), axis=0, keepdims=True)
        dy = e * (1.0 / d)
        dy_ref[...] = dy
        dffn_ref[...] = (dy * g_ref[...]).astype(dffn_ref.dtype)

        @pl.when((i == 0) & (j == 0))
        def _():
            loss_ref[...] = jnp.zeros_like(loss_ref)

        @pl.when(i == 0)
        def _():
            dg_ref[...] = jnp.zeros_like(dg_ref)

        loss_ref[...] += jnp.broadcast_to(part, loss_ref.shape)
        dg_ref[...] += _colsum(dy * f)

    tile = pl.BlockSpec((tm, tn), lambda j, i: (i, j))
    vec = pl.BlockSpec((1, tn), lambda j, i: (0, j))
    return pl.pallas_call(
        body, name=name, grid=(d // tn, rows // tm),
        in_specs=[pl.BlockSpec((tm, k), lambda j, i: (i, 0)), pl.BlockSpec((k, tn), lambda j, i: (0, j)),
                  tile, vec, tile],
        out_specs=[pl.BlockSpec((8, LANES), lambda j, i: (0, 0)), tile, tile, vec],
        out_shape=[jax.ShapeDtypeStruct((8, LANES), F32), jax.ShapeDtypeStruct((rows, d), F32),
                   jax.ShapeDtypeStruct((rows, d), MXU_DT), jax.ShapeDtypeStruct((1, d), F32)],
        compiler_params=_cp("arbitrary", "arbitrary"),
    )(act, w_down, x1, g2, target)


def _norm_bwd(dh, xin, nw, sc, *, res=None, gate=None, want_dx=True, carry=None, name):
    rows, d = xin.shape
    ts = min(rows, ROW_TS)
    has_res, has_gate = res is not None, gate is not None

    def body(*refs):
        it = iter(refs)
        dh_ref, x_ref, nw_ref, sc_ref = next(it), next(it), next(it), next(it)
        res_ref = next(it) if has_res else None
        gated_ref, g_ref = (next(it), next(it)) if has_gate else (None, None)
        dx_ref = next(it) if want_dx else None
        dgx_ref = next(it) if has_gate else None
        dsh_ref, dsc_ref, dnw_ref = next(it), next(it), next(it)
        dg_ref = next(it) if has_gate else None
        i = pl.program_id(0)
        dhv = dh_ref[...].astype(F32)
        r, xn = _rms_stats(x_ref[...])
        dn = dhv * (1.0 + sc_ref[...])

        @pl.when(i == 0)
        def _():
            dsh_ref[...] = jnp.zeros_like(dsh_ref)
            dsc_ref[...] = jnp.zeros_like(dsc_ref)
            dnw_ref[...] = jnp.zeros_like(dnw_ref)
            if has_gate:
                dg_ref[...] = jnp.zeros_like(dg_ref)

        dsh_ref[...] += _colsum(dhv)
        dsc_ref[...] += _colsum(dhv * (xn * nw_ref[...]))
        dnw_ref[...] += _colsum(dn * xn)
        if want_dx:
            dxn = dn * nw_ref[...]
            dx = r * (dxn - xn * jnp.mean(dxn * xn, axis=-1, keepdims=True))
            if has_res:
                dx = dx + res_ref[...]
            dx_ref[...] = dx
            if has_gate:
                dgx_ref[...] = (dx * g_ref[...]).astype(dgx_ref.dtype)
                dg_ref[...] += _colsum(dx * gated_ref[...])

    ins = [dh, xin, nw, sc] + ([res] if has_res else []) + (list(gate) if has_gate else [])
    in_specs = ([_row(ts, d), _row(ts, d), _vec(d), _vec(d)] + ([_row(ts, d)] if has_res else [])
                + ([_row(ts, d), _vec(d)] if has_gate else []))
    out_specs, out_shape = [], []
    if want_dx:
        out_specs.append(_row(ts, d)); out_shape.append(jax.ShapeDtypeStruct((rows, d), F32))
    if has_gate:
        out_specs.append(_row(ts, d)); out_shape.append(jax.ShapeDtypeStruct((rows, d), MXU_DT))
    for _ in range(3 + int(has_gate)):
        out_specs.append(_vec(d)); out_shape.append(jax.ShapeDtypeStruct((1, d), F32))
    return _grid_call(body, ins, carry, name=name, grid=(rows // ts,), in_specs=in_specs, out_specs=out_specs,
                      out_shape=out_shape, dims=("arbitrary",))


def _group_sum(v, g):
    hi = v.astype(MXU_DT)
    lo = (v - hi.astype(F32)).astype(MXU_DT)
    return (jnp.dot(hi, g, preferred_element_type=F32) + jnp.dot(lo, g, preferred_element_type=F32))


def _rot(v):
    lane = lax.broadcasted_iota(jnp.int32, v.shape, 1)
    first = (lane & 31) < 16
    return jnp.where(first, -pltpu.roll(v, LANES - 16, 1), pltpu.roll(v, 16, 1))


def _head_group_matrix():
    r = jnp.arange(LANES) // HEAD_DIM
    return (r[:, None] == r[None, :]).astype(MXU_DT)


def _qk_prep(xin, width, cos, sin, w, *, name):
    rows = xin.shape[0]
    ts = min(rows, 256)
    nch = width // LANES

    def body(x_ref, cos_ref, sin_ref, w_ref, g_ref, o_ref):
        cs, sn, g = cos_ref[...], sin_ref[...], g_ref[...]
        for ch in range(nch):
            sl = slice(ch * LANES, (ch + 1) * LANES)
            xv = x_ref[:, sl]
            r = lax.rsqrt(_group_sum(xv * xv, g) * (1.0 / HEAD_DIM) + EPS)
            yw = (xv * r) * w_ref[:, sl]
            o_ref[:, sl] = (yw * cs + _rot(yw) * sn).astype(o_ref.dtype)

    return pl.pallas_call(
        body, name=name, grid=(rows // ts,),
        in_specs=[_row(ts, width), _row(ts, LANES), _row(ts, LANES), _vec(width),
                  pl.BlockSpec((LANES, LANES), lambda i: (0, 0))],
        out_specs=_row(ts, width),
        out_shape=jax.ShapeDtypeStruct((rows, width), MXU_DT), compiler_params=_cp("parallel"),
    )(xin, cos, sin, w, _head_group_matrix())


def _qk_prep_bwd(xin, douts, width, cos, sin, w, *, tail=None, name):
    rows = xin.shape[0]
    ts = min(rows, 256)
    nch = width // LANES
    has_tail = tail is not None
    nd = len(douts)
    assert sum(v.shape[1] for v in douts) == width
    src = [(k, c) for k, v in enumerate(douts) for c in range(v.shape[1] // LANES)]

    def body(*refs):
        x_ref, d_refs = refs[0], refs[1:1 + nd]
        t_ref = refs[1 + nd] if has_tail else None
        cos_ref, sin_ref, w_ref, g_ref, dx_ref, dw_ref = refs[1 + nd + int(has_tail):]
        i = pl.program_id(0)
        cs, sn, g = cos_ref[...], sin_ref[...], g_ref[...]

        @pl.when(i == 0)
        def _():
            dw_ref[...] = jnp.zeros_like(dw_ref)

        if has_tail:
            dx_ref[:, width:width + LANES] = t_ref[...].astype(dx_ref.dtype)
            dx_ref[:, width + LANES:] = jnp.zeros((ts, GLU_OFF - width - LANES), dx_ref.dtype)

        for ch in range(nch):
            sl = slice(ch * LANES, (ch + 1) * LANES)
            xv = x_ref[:, sl]
            dv = d_refs[src[ch][0]][:, src[ch][1] * LANES:(src[ch][1] + 1) * LANES].astype(F32)
            r = lax.rsqrt(_group_sum(xv * xv, g) * (1.0 / HEAD_DIM) + EPS)
            n = xv * r
            dyw = dv * cs - _rot(dv * sn)
            dw_ref[:, sl] += _colsum(dyw * n)
            dn = dyw * w_ref[:, sl]
            gm = _group_sum(dn * n, g) * (1.0 / HEAD_DIM)
            dx_ref[:, sl] = (r * (dn - n * gm)).astype(dx_ref.dtype)

    ins = [xin] + list(douts) + ([tail] if has_tail else []) + [cos, sin, w, _head_group_matrix()]
    in_specs = ([_row(ts, width)] + [_row(ts, v.shape[1]) for v in douts] + ([_row(ts, LANES)] if has_tail else [])
                + [_row(ts, LANES), _row(ts, LANES), _vec(width), pl.BlockSpec((LANES, LANES), lambda i: (0, 0))])
    out_w, arr_w = (GLU_OFF, IN_PAD) if has_tail else (width, width)
    return pl.pallas_call(
        body, name=name, grid=(rows // ts,), in_specs=in_specs,
        out_specs=[_row(ts, out_w), _vec(width)],
        out_shape=[jax.ShapeDtypeStruct((rows, arr_w), MXU_DT), jax.ShapeDtypeStruct((1, width), F32)],
        compiler_params=_cp("arbitrary"),
    )(*ins)


GB = GQA_GROUP * BLOCK
WIN = 3 * BLOCK
ATT_STEP = 4


def _win_start(i, seq):
    return pl.multiple_of(jnp.clip((i - 1) * BLOCK, 0, seq - WIN), BLOCK)


def _attn_mask(i, start):
    qpos = i * BLOCK + (lax.broadcasted_iota(jnp.int32, (GB, WIN), 0) & (BLOCK - 1))
    kpos = start + lax.broadcasted_iota(jnp.int32, (GB, WIN), 1)
    return jnp.abs(qpos - kpos) <= WINDOW


def _sink_col(sink_ref, kv):
    return jnp.concatenate(
        [jnp.full((BLOCK, 1), sink_ref[0, kv * GQA_GROUP + g], F32) for g in range(GQA_GROUP)], axis=0)


def _stack_slots(ref, rows):
    return jnp.concatenate([ref[rows, g * LANES:(g + 1) * LANES] for g in range(GQA_GROUP)], axis=0)


def _kv_lanes(kv):
    lane = lax.broadcasted_iota(jnp.int32, (1, LANES), 1)
    return (lane < HEAD_DIM) if kv == 0 else (lane >= HEAD_DIM)


def _keep(mask, v):
    return jnp.where(mask, v, jnp.zeros_like(v))


_NT = (((1,), (1,)), ((), ()))
_TN = (((0,), (0,)), ((), ()))


def _attn_specs(seq, n_ctx):
    qs = pl.BlockSpec((ATT_STEP * BLOCK, Q_COLS), lambda i: (i, 0))
    ks = pl.BlockSpec((seq, KV_COLS), lambda i: (0, Q_COLS // KV_COLS))
    vs = pl.BlockSpec((seq, KV_COLS), lambda i: (0, QK_COLS // KV_COLS))
    kcs = pl.BlockSpec((n_ctx, KV_COLS), lambda i: (0, 0))
    vcs = pl.BlockSpec((n_ctx, KV_COLS), lambda i: (0, 1))
    ls = pl.BlockSpec((ATT_STEP, N_KV_HEADS * GB, 1), lambda i: (i, 0, 0))
    return qs, ks, vs, kcs, vcs, ls


def _attn_fwd(sink, qk, proj, kc, kv_ctx, *, carry=None, name):
    seq, n_ctx = qk.shape[0], kc.shape[0]
    scale = 1.0 / math.sqrt(HEAD_DIM)

    def one_block(blk, sub, sink_ref, q_ref, k_ref, v_ref, kcw, vcw, o_ref, lse_ref):
        rows = slice(sub * BLOCK, (sub + 1) * BLOCK)
        start = _win_start(blk, seq)
        valid = _attn_mask(blk, start)
        qs = _stack_slots(q_ref, rows)
        kw, vw = k_ref[pl.ds(start, WIN), :], v_ref[pl.ds(start, WIN), :].astype(MXU_DT)
        o_all = jnp.zeros((GB, LANES), F32)
        for kv in range(N_KV_HEADS):
            mine = _kv_lanes(kv)
            s_loc = lax.dot_general(qs, _keep(mine, kw), _NT, preferred_element_type=F32) * scale
            s_loc = jnp.where(valid, s_loc, NEG_INF)
            s_ctx = lax.dot_general(qs, _keep(mine, kcw), _NT, preferred_element_type=F32) * scale
            sk = _sink_col(sink_ref, kv)
            m = jnp.maximum(jnp.maximum(jnp.max(s_loc, axis=-1, keepdims=True),
                                        jnp.max(s_ctx, axis=-1, keepdims=True)), sk)
            p_loc = jnp.exp(s_loc - m)
            p_ctx = jnp.exp(s_ctx - m)
            l = (jnp.sum(p_loc, axis=-1, keepdims=True) + jnp.sum(p_ctx, axis=-1, keepdims=True)
                 + jnp.exp(sk - m))
            o_all = o_all + (jnp.dot(p_loc.astype(MXU_DT), _keep(mine, vw), preferred_element_type=F32)
                             + jnp.dot(p_ctx.astype(MXU_DT), _keep(mine, vcw), preferred_element_type=F32)) / l
            lse_ref[sub, kv * GB:(kv + 1) * GB, :] = m + jnp.log(l)
        for g in range(GQA_GROUP):
            o_ref[rows, g * LANES:(g + 1) * LANES] = o_all[g * BLOCK:(g + 1) * BLOCK].astype(o_ref.dtype)

    def body(sink_ref, q_ref, k_ref, v_ref, kc_ref, vc_ref, o_ref, lse_ref):
        i = pl.program_id(0)
        kcw, vcw = kc_ref[...], vc_ref[...].astype(MXU_DT)
        for sub in range(ATT_STEP):
            one_block(i * ATT_STEP + sub, sub, sink_ref, q_ref, k_ref, v_ref, kcw, vcw, o_ref, lse_ref)

    qs, ks, vs, kcs, vcs, ls = _attn_specs(seq, n_ctx)
    return _grid_call(
        body, [sink, qk, qk, proj, kc, kv_ctx], carry, name=name, grid=(seq // (ATT_STEP * BLOCK),),
        in_specs=[pl.BlockSpec(memory_space=pltpu.SMEM), qs, ks, vs, kcs, vcs],
        out_specs=[qs, ls],
        out_shape=[jax.ShapeDtypeStruct((seq, Q_COLS), MXU_DT),
                   jax.ShapeDtypeStruct((seq // BLOCK, N_KV_HEADS * GB, 1), F32)],
        dims=("parallel",))


def _attn_bwd(sink, qk, proj, kc, kv_ctx, o, lse, do, *, carry=None, name):
    seq, n_ctx = qk.shape[0], kc.shape[0]
    scale = 1.0 / math.sqrt(HEAD_DIM)

    def body(sink_ref, q_ref, k_ref, v_ref, kc_ref, vc_ref, o_ref, lse_ref, do_ref,
             dq_ref, dk_ref, dv_ref, dkc_ref, dvc_ref, ds_ref):
        i = pl.program_id(0)

        @pl.when(i == 0)
        def _():
            dk_ref[...] = jnp.zeros_like(dk_ref)
            dv_ref[...] = jnp.zeros_like(dv_ref)
            dkc_ref[...] = jnp.zeros_like(dkc_ref)
            dvc_ref[...] = jnp.zeros_like(dvc_ref)
            ds_ref[...] = jnp.zeros_like(ds_ref)

        kcw, vcw = kc_ref[...], vc_ref[...].astype(MXU_DT)
        dkc, dvc = jnp.zeros((n_ctx, LANES), F32), jnp.zeros((n_ctx, LANES), F32)
        for sub in range(ATT_STEP):
            dkc_s, dvc_s = one_block(i * ATT_STEP + sub, sub, sink_ref, q_ref, k_ref, v_ref, kcw, vcw, o_ref,
                                     lse_ref, do_ref, dq_ref, dk_ref, dv_ref, ds_ref)
            dkc, dvc = dkc + dkc_s, dvc + dvc_s
        dkc_ref[...] += dkc
        dvc_ref[...] += dvc

    def one_block(blk, sub, sink_ref, q_ref, k_ref, v_ref, kcw, vcw, o_ref, lse_ref, do_ref,
                  dq_ref, dk_ref, dv_ref, ds_ref):
        qrows = slice(sub * BLOCK, (sub + 1) * BLOCK)
        start = _win_start(blk, seq)
        valid = _attn_mask(blk, start)
        win = pl.ds(start, WIN)
        qs, dos = _stack_slots(q_ref, qrows), _stack_slots(do_ref, qrows)
        do_o = dos.astype(F32) * _stack_slots(o_ref, qrows).astype(F32)
        kw, vw = k_ref[win, :], v_ref[win, :].astype(MXU_DT)
        dq = jnp.zeros((GB, LANES), F32)
        dk, dv = jnp.zeros((WIN, LANES), F32), jnp.zeros((WIN, LANES), F32)
        dkc, dvc = jnp.zeros((n_ctx, LANES), F32), jnp.zeros((n_ctx, LANES), F32)
        for kv in range(N_KV_HEADS):
            mine = _kv_lanes(kv)
            rows = slice(kv * GB, (kv + 1) * GB)
            lse_s = lse_ref[sub, rows, :]
            delta = jnp.sum(_keep(mine, do_o), axis=-1, keepdims=True)
            kz, vz, kcz, vcz = _keep(mine, kw), _keep(mine, vw), _keep(mine, kcw), _keep(mine, vcw)
            s_loc = lax.dot_general(qs, kz, _NT, preferred_element_type=F32) * scale
            s_loc = jnp.where(valid, s_loc, NEG_INF)
            s_ctx = lax.dot_general(qs, kcz, _NT, preferred_element_type=F32) * scale
            p_loc = jnp.exp(s_loc - lse_s)
            p_ctx = jnp.exp(s_ctx - lse_s)
            p_sink = jnp.exp(_sink_col(sink_ref, kv) - lse_s)
            dp_loc = lax.dot_general(dos, vz, _NT, preferred_element_type=F32)
            dp_ctx = lax.dot_general(dos, vcz, _NT, preferred_element_type=F32)
            ds_loc = (p_loc * (dp_loc - delta) * scale).astype(MXU_DT)
            ds_ctx = (p_ctx * (dp_ctx - delta) * scale).astype(MXU_DT)
            dq = dq + (jnp.dot(ds_loc, kz, preferred_element_type=F32)
                       + jnp.dot(ds_ctx, kcz, preferred_element_type=F32))
            dk = dk + _keep(mine, lax.dot_general(ds_loc, qs, _TN, preferred_element_type=F32))
            dv = dv + _keep(mine, lax.dot_general(p_loc.astype(MXU_DT), dos, _TN, preferred_element_type=F32))
            dkc = dkc + _keep(mine, lax.dot_general(ds_ctx, qs, _TN, preferred_element_type=F32))
            dvc = dvc + _keep(mine, lax.dot_general(p_ctx.astype(MXU_DT), dos, _TN, preferred_element_type=F32))
            ds_ref[rows, :] += -(p_sink * delta)
        for g in range(GQA_GROUP):
            dq_ref[qrows, g * LANES:(g + 1) * LANES] = dq[g * BLOCK:(g + 1) * BLOCK]
        dk_ref[win, :] += dk
        dv_ref[win, :] += dv
        return dkc, dvc

    qs, ks, vs, kcs, vcs, ls = _attn_specs(seq, n_ctx)
    whole = lambda r, c: pl.BlockSpec((r, c), lambda i: (0, 0))
    return _grid_call(
        body, [sink, qk, qk, proj, kc, kv_ctx, o, lse, do], carry, name=name, grid=(seq // (ATT_STEP * BLOCK),),
        in_specs=[pl.BlockSpec(memory_space=pltpu.SMEM), qs, ks, vs, kcs, vcs, qs, ls, qs],
        out_specs=[qs, whole(seq, KV_COLS), whole(seq, KV_COLS), whole(n_ctx, KV_COLS), whole(n_ctx, KV_COLS),
                   whole(N_KV_HEADS * GB, 1)],
        out_shape=[jax.ShapeDtypeStruct((seq, Q_COLS), F32), jax.ShapeDtypeStruct((seq, KV_COLS), F32),
                   jax.ShapeDtypeStruct((seq, KV_COLS), F32), jax.ShapeDtypeStruct((n_ctx, KV_COLS), F32),
                   jax.ShapeDtypeStruct((n_ctx, KV_COLS), F32), jax.ShapeDtypeStruct((N_KV_HEADS * GB, 1), F32)],
        dims=("arbitrary",))


def _halo_specs(ts, w, rows, col=0):
    per = ts // HALO
    last = rows // HALO - 1
    return [pl.BlockSpec((HALO, w), lambda i: (jnp.maximum(i * per - 1, 0), col)),
            pl.BlockSpec((ts, w), lambda i: (i, col)),
            pl.BlockSpec((HALO, w), lambda i: (jnp.minimum((i + 1) * per, last), col))]


def _glu(v):
    return v[:, :CONV_CH] * _sigmoid(v[:, CONV_CH:])


def _ln_stats(u):
    mu = jnp.mean(u, axis=-1, keepdims=True)
    xc = u - mu
    rstd = lax.rsqrt(jnp.mean(xc * xc, axis=-1, keepdims=True) + EPS)
    return xc * rstd, rstd


CONV_BWD_TS = 128


def _phases(ext_ref, ph_ref):
    n = ph_ref.shape[1]
    for b in range(1, 8):
        ph_ref[b - 1] = ext_ref[b:b + n, :]


def _window(ext_ref, ph_ref, o, n, cs):
    a, b = divmod(o, 8)
    src = ext_ref if b == 0 else ph_ref.at[b - 1]
    return src[8 * a:8 * a + n, cs]


def _conv_fwd(glu, col, cw, cb, lw, lb, *, carry=None, name):
    rows = glu.shape[0]
    ts = min(rows, 256)
    nt = rows // ts

    te = ts + 2 * HALO

    def body(gp_ref, g_ref, gn_ref, cw_ref, cb_ref, lw_ref, lb_ref, u3_ref, u1_ref, ext_ref, ph_ref):
        i = pl.program_id(0)
        ext_ref[0:HALO, :] = jnp.where(i > 0, _glu(gp_ref[...]), 0.0)
        ext_ref[HALO:HALO + ts, :] = _glu(g_ref[...])
        ext_ref[HALO + ts:, :] = jnp.where(i < nt - 1, _glu(gn_ref[...]), 0.0)
        _phases(ext_ref, ph_ref)
        for c in range(CONV_CH // LANES):
            cs = slice(c * LANES, (c + 1) * LANES)
            acc = jnp.broadcast_to(cb_ref[:, cs], (ts, LANES))
            for j in range(CONV_K):
                acc = acc + cw_ref[j:j + 1, cs] * _window(ext_ref, ph_ref, HALO - CONV_PAD + j, ts, cs)
            u1_ref[:, cs] = acc
        xh, _ = _ln_stats(u1_ref[...])
        u2 = xh * lw_ref[...] + lb_ref[...]
        u3_ref[...] = (u2 * _sigmoid(u2)).astype(u3_ref.dtype)

    full = lambda shape: pl.BlockSpec(shape, lambda i: (0,) * len(shape))
    return _grid_call(
        body, [glu, glu, glu, cw, cb, lw, lb], carry, name=name, grid=(nt,),
        in_specs=_halo_specs(ts, GLU_COLS, rows, col) + [full((CONV_K, CONV_CH))] + [_vec(CONV_CH)] * 3,
        out_specs=[_row(ts, CONV_CH), _row(ts, CONV_CH)],
        out_shape=[jax.ShapeDtypeStruct((rows, CONV_CH), MXU_DT), jax.ShapeDtypeStruct((rows, CONV_CH), F32)],
        scratch_shapes=[pltpu.VMEM((te, CONV_CH), F32), pltpu.VMEM((7, te - 8, CONV_CH), F32)],
        dims=("parallel",))


def _conv_bwd(glu, col, u1, du3, dproj, cw, lw, lb, *, carry=None, name):
    rows = glu.shape[0]
    ts = min(rows, CONV_BWD_TS)
    nt = rows // ts
    te = ts + 2 * HALO

    def du1_of(u1v, du3v, lw_v, lb_v):
        xh, rstd = _ln_stats(u1v)
        u2 = xh * lw_v + lb_v
        sg = _sigmoid(u2)
        du2 = du3v * (sg * (1.0 + u2 * (1.0 - sg)))
        dxh = du2 * lw_v
        du1 = rstd * (dxh - jnp.mean(dxh, axis=-1, keepdims=True)
                      - xh * jnp.mean(dxh * xh, axis=-1, keepdims=True))
        return du1, du2, xh

    half = ts // 2

    def body(gp_ref, g_ref, gn_ref, up_ref, u_ref, un_ref, dp_ref, d_ref, dn_ref, cw_ref, lw_ref, lb_ref,
             _, dglu_ref, dcw_ref, dvec_ref, u0_ref, du1_ref, pu_ref, pd_ref, du0_ref):
        i = pl.program_id(0)
        lw_v, lb_v = lw_ref[...], lb_ref[...]

        @pl.when(i == 0)
        def _():
            dcw_ref[...] = jnp.zeros_like(dcw_ref)
            dvec_ref[...] = jnp.zeros_like(dvec_ref)

        gv = g_ref[...]
        u0_ref[0:HALO, :] = jnp.where(i > 0, _glu(gp_ref[...]), 0.0)
        u0_ref[HALO:HALO + ts, :] = _glu(gv)
        u0_ref[HALO + ts:, :] = jnp.where(i < nt - 1, _glu(gn_ref[...]), 0.0)
        d_prev, _, _ = du1_of(up_ref[...], dp_ref[...], lw_v, lb_v)
        d_main, du2, xh = du1_of(u_ref[...], d_ref[...], lw_v, lb_v)
        d_next, _, _ = du1_of(un_ref[...], dn_ref[...], lw_v, lb_v)
        du1_ref[0:HALO, :] = jnp.where(i > 0, d_prev, 0.0)
        du1_ref[HALO:HALO + ts, :] = d_main
        du1_ref[HALO + ts:, :] = jnp.where(i < nt - 1, d_next, 0.0)

        rid = lax.broadcasted_iota(jnp.int32, (8, CONV_CH), 0)
        dvec_ref[...] += (jnp.where(rid == 0, _colsum(d_main), 0.0)
                          + jnp.where(rid == 1, _colsum(du2 * xh), 0.0)
                          + jnp.where(rid == 2, _colsum(du2), 0.0))
        _phases(u0_ref, pu_ref)
        _phases(du1_ref, pd_ref)
        for c in range(CONV_CH // LANES):
            cs = slice(c * LANES, (c + 1) * LANES)
            for r0 in (0, half):
                dm = du1_ref[HALO + r0:HALO + r0 + half, cs]
                acc = jnp.zeros((half, LANES), F32)
                for j in range(CONV_K):
                    acc = acc + cw_ref[j:j + 1, cs] * _window(du1_ref, pd_ref, r0 + HALO + CONV_PAD - j, half, cs)
                    prod = dm * _window(u0_ref, pu_ref, r0 + HALO - CONV_PAD + j, half, cs)
                    dcw_ref[j, :, cs] += jnp.sum(prod.reshape(half // 8, 8, LANES), axis=0)
                du0_ref[r0:r0 + half, cs] = acc
        du0 = du0_ref[...]
        ga, sg = gv[:, :CONV_CH], _sigmoid(gv[:, CONV_CH:])
        dglu_ref[:, :CONV_CH] = (du0 * sg).astype(dglu_ref.dtype)
        dglu_ref[:, CONV_CH:] = (du0 * ga * sg * (1.0 - sg)).astype(dglu_ref.dtype)

    full = lambda shape: pl.BlockSpec(shape, lambda i: (0,) * len(shape))
    return _grid_call(
        body, [glu, glu, glu, u1, u1, u1, du3, du3, du3, cw, lw, lb, dproj], carry, name=name, grid=(nt,),
        in_specs=(_halo_specs(ts, GLU_COLS, rows, col) + _halo_specs(ts, CONV_CH, rows)
                  + _halo_specs(ts, CONV_CH, rows) + [full((CONV_K, CONV_CH)), _vec(CONV_CH), _vec(CONV_CH)]
                  + [pl.BlockSpec(memory_space=pl.ANY)]),
        out_specs=[_row(ts, GLU_COLS, col), full((CONV_K, 8, CONV_CH)), full((8, CONV_CH))],
        out_shape=[jax.ShapeDtypeStruct(dproj.shape, dproj.dtype),
                   jax.ShapeDtypeStruct((CONV_K, 8, CONV_CH), F32), jax.ShapeDtypeStruct((8, CONV_CH), F32)],
        scratch_shapes=[pltpu.VMEM((te, CONV_CH), F32), pltpu.VMEM((te, CONV_CH), F32),
                        pltpu.VMEM((7, te - 8, CONV_CH), F32), pltpu.VMEM((7, te - 8, CONV_CH), F32),
                        pltpu.VMEM((ts, CONV_CH), F32)],
        input_output_aliases={12: 0}, dims=("arbitrary",))


FFN_CW = 1408
FFN_NJ = FFN_H // FFN_CW


def _ffn_halo_specs(ts, rows, col_of, inner_rows):
    per = ts // HALO
    last = rows // HALO - 1
    if inner_rows:
        return [pl.BlockSpec((HALO, FFN_CW), lambda j, i: (jnp.maximum(i * per - 1, 0), col_of(j))),
                pl.BlockSpec((ts, FFN_CW), lambda j, i: (i, col_of(j))),
                pl.BlockSpec((HALO, FFN_CW), lambda j, i: (jnp.minimum((i + 1) * per, last), col_of(j)))]
    return [pl.BlockSpec((HALO, FFN_CW), lambda i, j: (jnp.maximum(i * per - 1, 0), col_of(j))),
            pl.BlockSpec((ts, FFN_CW), lambda i, j: (i, col_of(j))),
            pl.BlockSpec((HALO, FFN_CW), lambda i, j: (jnp.minimum((i + 1) * per, last), col_of(j)))]


def _ffn_ext(p_ref, m_ref, n_ref, sl, i, nt):
    return jnp.concatenate([jnp.where(i > 0, p_ref[:, sl].astype(F32), 0.0), m_ref[:, sl].astype(F32),
                            jnp.where(i < nt - 1, n_ref[:, sl].astype(F32), 0.0)], axis=0)


def _prev_row(v):
    return pltpu.roll(v, 1, 0)


def _next_row(v):
    return pltpu.roll(v, v.shape[0] - 1, 0)


def _ffn_act(up0, w3, b3, *, name):
    rows = up0.shape[0]
    ts = min(rows, 256)
    nt = rows // ts
    main = slice(HALO, HALO + ts)

    def body(gp, g, gn, vp, v, vn, wg, wv, bg, bv, a_ref, go_ref, vo_ref):
        i = pl.program_id(0)
        for ch in range(FFN_CW // LANES):
            sl = slice(ch * LANES, (ch + 1) * LANES)
            xg, xv = _ffn_ext(gp, g, gn, sl, i, nt), _ffn_ext(vp, v, vn, sl, i, nt)
            wgv, wvv = wg[:, sl], wv[:, sl]
            gate = (wgv[0:1] * _prev_row(xg) + wgv[1:2] * xg + wgv[2:3] * _next_row(xg))[main] + bg[:, sl]
            val = (wvv[0:1] * _prev_row(xv) + wvv[1:2] * xv + wvv[2:3] * _next_row(xv))[main] + bv[:, sl]
            a_ref[:, sl] = (gate * _sigmoid(gate) * val).astype(a_ref.dtype)
            go_ref[:, sl] = gate.astype(go_ref.dtype)
            vo_ref[:, sl] = val.astype(vo_ref.dtype)

    gcol, vcol = (lambda j: j), (lambda j: j + FFN_NJ)
    wspec = lambda col_of: pl.BlockSpec((FFN_K, FFN_CW), lambda i, j: (0, col_of(j)))
    bspec = lambda col_of: pl.BlockSpec((1, FFN_CW), lambda i, j: (0, col_of(j)))
    ospec = pl.BlockSpec((ts, FFN_CW), lambda i, j: (i, j))
    return pl.pallas_call(
        body, name=name, grid=(nt, FFN_NJ),
        in_specs=(_ffn_halo_specs(ts, rows, gcol, False) + _ffn_halo_specs(ts, rows, vcol, False)
                  + [wspec(gcol), wspec(vcol), bspec(gcol), bspec(vcol)]),
        out_specs=[ospec] * 3, out_shape=[jax.ShapeDtypeStruct((rows, FFN_H), MXU_DT)] * 3,
        compiler_params=_cp("parallel", "parallel"),
    )(up0, up0, up0, up0, up0, up0, w3, w3, b3, b3)


def _ffn_act_bwd(up0, gate_s, val_s, da, w3, *, carry=None, name):
    rows = up0.shape[0]
    ts = min(rows, 256)
    nt = rows // ts
    main = slice(HALO, HALO + ts)

    def body(gp, g, gn, vp, v, vn, sgp, sg, sgn, svp, sv, svn, ap, a, an, wg, wv,
             dg_ref, dv_ref, sg_ref, sv_ref):
        i = pl.program_id(1)

        @pl.when(i == 0)
        def _():
            sg_ref[...] = jnp.zeros_like(sg_ref)
            sv_ref[...] = jnp.zeros_like(sv_ref)

        rid = lax.broadcasted_iota(jnp.int32, (8, LANES), 0)
        for ch in range(FFN_CW // LANES):
            sl = slice(ch * LANES, (ch + 1) * LANES)
            xg, xv, da_e = _ffn_ext(gp, g, gn, sl, i, nt), _ffn_ext(vp, v, vn, sl, i, nt), _ffn_ext(ap, a, an, sl, i, nt)
            gate, val = _ffn_ext(sgp, sg, sgn, sl, i, nt), _ffn_ext(svp, sv, svn, sl, i, nt)
            wgv, wvv = wg[:, sl], wv[:, sl]
            xg_p, xg_n, xv_p, xv_n = _prev_row(xg), _next_row(xg), _prev_row(xv), _next_row(xv)
            sgm = _sigmoid(gate)
            eg = da_e * val * (sgm * (1.0 + gate * (1.0 - sgm)))
            ev = da_e * (gate * sgm)
            for e, taps, w, d_ref, s_ref in ((eg, (xg_p, xg, xg_n), wgv, dg_ref, sg_ref),
                                             (ev, (xv_p, xv, xv_n), wvv, dv_ref, sv_ref)):
                d0 = w[0:1] * _next_row(e) + w[1:2] * e + w[2:3] * _prev_row(e)
                d_ref[:, sl] = d0[main].astype(d_ref.dtype)
                dm = e[main]
                s_ref[:, sl] += (jnp.where(rid == 0, _colsum(dm * taps[0][main]), 0.0)
                                 + jnp.where(rid == 1, _colsum(dm * taps[1][main]), 0.0)
                                 + jnp.where(rid == 2, _colsum(dm * taps[2][main]), 0.0)
                                 + jnp.where(rid == 3, _colsum(dm), 0.0))

    gcol, vcol = (lambda j: j), (lambda j: j + FFN_NJ)
    wspec = lambda col_of: pl.BlockSpec((FFN_K, FFN_CW), lambda j, i: (0, col_of(j)))
    ospec = pl.BlockSpec((ts, FFN_CW), lambda j, i: (i, j))
    sspec = pl.BlockSpec((8, FFN_CW), lambda j, i: (0, j))
    return _grid_call(
        body, [up0] * 6 + [gate_s] * 3 + [val_s] * 3 + [da] * 3 + [w3, w3], carry, name=name, grid=(FFN_NJ, nt),
        in_specs=(_ffn_halo_specs(ts, rows, gcol, True) + _ffn_halo_specs(ts, rows, vcol, True)
                  + _ffn_halo_specs(ts, rows, gcol, True) * 3 + [wspec(gcol), wspec(vcol)]),
        out_specs=[ospec, ospec, sspec, sspec],
        out_shape=[jax.ShapeDtypeStruct((rows, FFN_H), MXU_DT), jax.ShapeDtypeStruct((rows, FFN_H), MXU_DT),
                   jax.ShapeDtypeStruct((8, FFN_H), F32), jax.ShapeDtypeStruct((8, FFN_H), F32)],
        dims=("parallel", "arbitrary"))


def _adam_math(w, g, m, v):
    m = ADAM_B1 * m + (1.0 - ADAM_B1) * g
    v = ADAM_B2 * v + (1.0 - ADAM_B2) * (g * g)
    m_hat = m / (1.0 - ADAM_B1 ** ADAM_STEP)
    v_hat = v / (1.0 - ADAM_B2 ** ADAM_STEP)
    delta = -ADAM_LR * (m_hat / (jnp.sqrt(v_hat) + ADAM_EPS) + ADAM_WD * w)
    return delta, m, v


ROW_TILE_BYTES = 8 << 20


def _row_tile(rows, row_bytes):
    tiles = [rows] + [rows // k for k in range(2, rows // 16 + 1) if rows % k == 0 and (rows // k) % 16 == 0]
    return next(t for t in tiles if t * row_bytes <= ROW_TILE_BYTES)


def _adam(w, m, v, parts, *, name):
    rows, cols = w.shape
    nparts = parts.shape[0]
    tr = _row_tile(rows, cols * (7 * 4 + nparts * parts.dtype.itemsize))

    def body(w_ref, m_ref, v_ref, p_ref, g_ref, d_ref, nm_ref, nv_ref):
        g = p_ref[0].astype(F32)
        for p in range(1, nparts):
            g = g + p_ref[p].astype(F32)
        g_ref[...] = g
        d_ref[...], nm_ref[...], nv_ref[...] = _adam_math(w_ref[...], g, m_ref[...], v_ref[...])

    spec = _row(tr, cols)
    return pl.pallas_call(
        body, name=name, grid=(rows // tr,),
        in_specs=[spec, spec, spec, pl.BlockSpec((nparts, tr, cols), lambda i: (0, i, 0))],
        out_specs=[spec] * 4, out_shape=[jax.ShapeDtypeStruct((rows, cols), F32)] * 4,
        compiler_params=_cp("parallel"),
    )(w, m, v, parts)


def _adam_many(ws, ms, vs, gs, *, name):
    n = len(ws)

    def body(*refs):
        ins, outs = refs[:4 * n], refs[4 * n:]
        for k in range(n):
            delta, new_m, new_v = _adam_math(ins[k][...], ins[3 * n + k][...], ins[n + k][...], ins[2 * n + k][...])
            outs[k][...], outs[n + k][...], outs[2 * n + k][...] = delta, new_m, new_v

    vm = pl.BlockSpec(memory_space=pltpu.VMEM)
    res = pl.pallas_call(
        body, name=name, in_specs=[vm] * (4 * n), out_specs=[vm] * (3 * n),
        out_shape=[jax.ShapeDtypeStruct(w.shape, F32) for w in ws] * 3,
        compiler_params=pltpu.CompilerParams(vmem_limit_bytes=VMEM_LIMIT),
    )(*ws, *ms, *vs, *gs)
    return res[:n], res[n:2 * n], res[2 * n:]


def _sum_parts(parts, *, name):
    nparts, rows, cols = parts.shape
    tr = _row_tile(rows, cols * (4 + nparts * parts.dtype.itemsize))

    def body(p_ref, o_ref):
        g = p_ref[0].astype(F32)
        for p in range(1, nparts):
            g = g + p_ref[p].astype(F32)
        o_ref[...] = g

    return pl.pallas_call(
        body, name=name, grid=(rows // tr,),
        in_specs=[pl.BlockSpec((nparts, tr, cols), lambda i: (0, i, 0))], out_specs=_row(tr, cols),
        out_shape=jax.ShapeDtypeStruct((rows, cols), F32), compiler_params=_cp("parallel"),
    )(parts)


def _my_place():
    return lax.axis_index("x"), lax.axis_index("y"), lax.axis_index("c")


def _dev_index(p):
    return 4 * p[0] + 2 * p[1] + p[2]


def _all_gather(xs, *, name):
    return _run_comm(_gather_plan(xs), pltpu.VMEM, name)


class _Comm(NamedTuple):
    ins: list
    outs: list
    n_remote: int
    n_local: int
    start: Callable
    finish: Callable


def _join_plans(*plans):
    def split(in_refs, out_refs, send_sems, recv_sems, local_sems):
        i = o = r = l = 0
        for p in plans:
            ni, no = len(p.ins), len(p.outs)
            yield p, (in_refs[i:i + ni], out_refs[o:o + no], send_sems.at[pl.ds(r, p.n_remote)],
                      recv_sems.at[pl.ds(r, p.n_remote)], local_sems.at[pl.ds(l, p.n_local)])
            i, o, r, l = i + ni, o + no, r + p.n_remote, l + p.n_local

    def start(*refs):
        for p, part in split(*refs):
            p.start(*part)

    def finish(*refs):
        for p, part in split(*refs):
            p.finish(*part)

    return _Comm([v for p in plans for v in p.ins], [v for p in plans for v in p.outs],
                 sum(p.n_remote for p in plans), sum(p.n_local for p in plans), start, finish)


def _comm_scratch(plan):
    return [pltpu.SemaphoreType.DMA((plan.n_remote,)), pltpu.SemaphoreType.DMA((plan.n_remote,)),
            pltpu.SemaphoreType.DMA((plan.n_local,))]


def _run_comm(plan, space, name):
    n_in, n_out = len(plan.ins), len(plan.outs)

    def body(*refs):
        args = (refs[:n_in], refs[n_in:n_in + n_out], *refs[n_in + n_out:])
        plan.start(*args)
        plan.finish(*args)

    return pl.pallas_call(
        body, name=name, out_shape=plan.outs,
        in_specs=[pl.BlockSpec(memory_space=space)] * n_in, out_specs=[pl.BlockSpec(memory_space=space)] * n_out,
        scratch_shapes=_comm_scratch(plan),
        compiler_params=pltpu.CompilerParams(vmem_limit_bytes=VMEM_LIMIT),
    )(*plan.ins)


def _grid_call(body, ins, carry, *, name, grid, in_specs, out_specs, out_shape, dims, scratch_shapes=(),
               input_output_aliases=None):
    if carry is None:
        res = pl.pallas_call(
            body, name=name, grid=grid, in_specs=list(in_specs), out_specs=list(out_specs),
            out_shape=list(out_shape), scratch_shapes=list(scratch_shapes),
            input_output_aliases=input_output_aliases or {}, compiler_params=_cp(*dims))(*ins)
        return list(res), None

    def at(pos):
        conds = [pl.program_id(k) == p for k, p in enumerate(pos)]
        out = conds[0]
        for cnd in conds[1:]:
            out = jnp.logical_and(out, cnd)
        return out

    return _carried_call(body, carry, lambda: at([0] * len(grid)), lambda: at([g - 1 for g in grid]), ins,
                         name=name, grid=grid, in_specs=in_specs, out_specs=out_specs, out_shape=out_shape,
                         scratch_shapes=scratch_shapes, input_output_aliases=input_output_aliases)


def _carried_call(body, plan, first, last, ins, *, name, grid, in_specs, out_specs, out_shape, scratch_shapes=(),
                  input_output_aliases=None):
    in_specs, out_specs, out_shape = list(in_specs), list(out_specs), list(out_shape)
    n_in, n_out, n_scr = len(in_specs), len(out_specs), len(scratch_shapes)
    c_in, c_out = len(plan.ins), len(plan.outs)
    hbm = pl.BlockSpec(memory_space=pl.ANY)

    def full_body(*refs):
        ins, c_ins = refs[:n_in], refs[n_in:n_in + c_in]
        outs = refs[n_in + c_in:n_in + c_in + n_out]
        c_outs = refs[n_in + c_in + n_out:n_in + c_in + n_out + c_out]
        scr = refs[n_in + c_in + n_out + c_out:]
        sems = scr[n_scr:]

        @pl.when(first())
        def _():
            plan.start(c_ins, c_outs, *sems)

        body(*ins, *outs, *scr[:n_scr])

        @pl.when(last())
        def _():
            plan.finish(c_ins, c_outs, *sems)

    res = pl.pallas_call(
        full_body, name=name, grid=grid,
        in_specs=in_specs + [hbm] * c_in, out_specs=out_specs + [hbm] * c_out,
        out_shape=out_shape + list(plan.outs),
        scratch_shapes=list(scratch_shapes) + _comm_scratch(plan),
        input_output_aliases=input_output_aliases or {},
        compiler_params=_cp(*(["arbitrary"] * len(grid))),
    )(*ins, *plan.ins)
    return list(res[:n_out]), list(res[n_out:])


def _gather_plan(xs):
    n = len(xs)
    ms = [v.shape[0] for v in xs]

    def tools(x_refs, o_refs, send_sems, recv_sems, local_sems):
        x, y, c = _my_place()
        me, sib = (x, y, c), (x, y, 1 - c)
        chips = [(1 - x, y), (x, 1 - y), (1 - x, 1 - y)]

        def rows(a, p):
            return o_refs[a].at[pl.ds(pl.multiple_of(_dev_index(p) * ms[a], 8), ms[a])]

        def copy(a, k, block, to, src=None):
            return pltpu.make_async_remote_copy(
                src_ref=rows(a, block) if src is None else src, dst_ref=rows(a, block),
                send_sem=send_sems.at[a * 7 + k], recv_sem=recv_sems.at[a * 7 + k],
                device_id=to, device_id_type=MESH)

        mine = [pltpu.make_async_copy(x_refs[a], rows(a, me), local_sems.at[a]) for a in range(n)]
        first = []
        for a in range(n):
            first.append(copy(a, 0, me, sib, src=x_refs[a]))
            first += [copy(a, 1 + j, me, (*chip, c), src=x_refs[a]) for j, chip in enumerate(chips)]
        return me, sib, chips, c, copy, mine, first

    def start(*refs):
        _, _, _, _, _, mine, first = tools(*refs)
        for cp in mine + first:
            cp.start()

    def finish(*refs):
        me, sib, chips, c, copy, mine, first = tools(*refs)
        passed = []
        for j, chip in enumerate(chips):
            for a in range(n):
                copy(a, 1 + j, (*chip, c), me).wait_recv()
                cp = copy(a, 4 + j, (*chip, c), sib)
                cp.start()
                passed.append(cp)
        for a in range(n):
            copy(a, 0, sib, me).wait_recv()
            for j, chip in enumerate(chips):
                copy(a, 4 + j, (*chip, 1 - c), me).wait_recv()
        for cp in first + passed:
            cp.wait_send()
        for cp in mine:
            cp.wait()

    outs = [jax.ShapeDtypeStruct((N_DEV * v.shape[0], v.shape[1]), v.dtype) for v in xs]
    return _Comm(list(xs), outs, 7 * n, n, start, finish)


def _exchange_plan(gs, cols=None):
    n = len(gs)
    rs = [v.shape[0] // N_DEV for v in gs]
    flips = [(bx, by, bc) for bx in (0, 1) for by in (0, 1) for bc in (0, 1)][1:]

    def tools(g_refs, o_refs, send_sems, recv_sems, local_sems):
        x, y, c = _my_place()
        me = (x, y, c)

        def block(ref, a, p):
            return ref.at[pl.ds(_dev_index(p) * rs[a], rs[a])]

        def src(a, p):
            rows = pl.ds(_dev_index(p) * rs[a], rs[a])
            return g_refs[a].at[rows] if cols is None else g_refs[a].at[rows, pl.ds(cols[0], cols[1])]

        def peer(f):
            return (1 - x if f[0] else x, 1 - y if f[1] else y, 1 - c if f[2] else c)

        def copy(a, k, to):
            return pltpu.make_async_remote_copy(
                src_ref=src(a, to), dst_ref=block(o_refs[a], a, me),
                send_sem=send_sems.at[a * 7 + k], recv_sem=recv_sems.at[a * 7 + k],
                device_id=to, device_id_type=MESH)

        def arrival(a, k, frm):
            return pltpu.make_async_remote_copy(
                src_ref=src(a, frm), dst_ref=block(o_refs[a], a, frm),
                send_sem=send_sems.at[a * 7 + k], recv_sem=recv_sems.at[a * 7 + k],
                device_id=frm, device_id_type=MESH)

        mine = [pltpu.make_async_copy(src(a, me), block(o_refs[a], a, me), local_sems.at[a])
                for a in range(n)]
        sends = [copy(a, k, peer(f)) for a in range(n) for k, f in enumerate(flips)]
        arrivals = [arrival(a, k, peer(f)) for a in range(n) for k, f in enumerate(flips)]
        return mine, sends, arrivals

    def start(*refs):
        mine, sends, _ = tools(*refs)
        for cp in mine + sends:
            cp.start()

    def finish(*refs):
        mine, sends, arrivals = tools(*refs)
        for cp in arrivals:
            cp.wait_recv()
        for cp in sends:
            cp.wait_send()
        for cp in mine:
            cp.wait()

    outs = [jax.ShapeDtypeStruct((v.shape[0], v.shape[1] if cols is None else cols[1]), v.dtype) for v in gs]
    return _Comm(list(gs), outs, 7 * n, n, start, finish)


def _rope_tables(seq):
    t = jnp.arange(seq)
    row, col = t // GRID_W, t % GRID_W
    half = HEAD_DIM // 2
    inv = ROPE_BASE ** (-jnp.arange(0, half, 2, dtype=F32) / half)
    ang_r = row.astype(F32)[:, None] * inv
    ang_c = col.astype(F32)[:, None] * inv
    ang = jnp.concatenate([ang_r, ang_r, ang_c, ang_c], axis=-1)
    return jnp.tile(jnp.cos(ang), (1, 2)), jnp.tile(jnp.sin(ang), (1, 2))


def _to_slots(w):
    return w.reshape(N_KV_HEADS, GQA_GROUP, HEAD_DIM, w.shape[1]).transpose(1, 0, 2, 3).reshape(w.shape)


def _from_slots(w):
    return w.reshape(GQA_GROUP, N_KV_HEADS, HEAD_DIM, w.shape[1]).transpose(1, 0, 2, 3).reshape(w.shape)


def _pack(vs):
    flat = jnp.concatenate([v.reshape(-1).astype(F32) for v in vs])
    total = -(-flat.shape[0] // (8 * LANES)) * (8 * LANES)
    return jnp.pad(flat, (0, total - flat.shape[0])).reshape(-1, LANES)


def _unpack(packed, like):
    flat, out, off = packed.reshape(-1), [], 0
    for v in like:
        size = math.prod(v.shape)
        out.append(flat[off:off + size].reshape(v.shape))
        off += size
    return out


def _silu(v):
    return v * jax.nn.sigmoid(v)


def kernel(x, c, ctx, c_ctx, w_mod, b_mod, norm_mix_w, w_in, q_norm_w, k_norm_w, sink_logit, conv_w, conv_b, conv_norm_w, conv_norm_b, w_out, norm_ffn_w, w_up, ffn_conv_w, ffn_conv_b, w_down, loss_target, m_c_ctx, m_w_mod, m_b_mod, m_norm_mix_w, m_w_in, m_q_norm_w, m_k_norm_w, m_sink_logit, m_conv_w, m_conv_b, m_conv_norm_w, m_conv_norm_b, m_w_out, m_norm_ffn_w, m_w_up, m_ffn_conv_w, m_ffn_conv_b, m_w_down, v_c_ctx, v_w_mod, v_b_mod, v_norm_mix_w, v_w_in, v_q_norm_w, v_k_norm_w, v_sink_logit, v_conv_w, v_conv_b, v_conv_norm_w, v_conv_norm_b, v_w_out, v_norm_ffn_w, v_w_up, v_ffn_conv_w, v_ffn_conv_b, v_w_down):
    d = D_MODEL
    seq, n_ctx = x.shape[1], ctx.shape[1]
    me = _dev_index(_my_place())
    xs, ctxs, tgt = x[0], ctx[0], loss_target[0]

    small = _pack([c[0], conv_w[0], ffn_conv_w[0]])
    small_all = _all_gather([small], name="gather_small")[0].reshape(N_DEV, -1)
    n_cw, n_fw = conv_w[0].size, ffn_conv_w[0].size
    c_all = small_all[:, :d]
    cw_all = small_all[:, d:d + n_cw].reshape(N_DEV, CONV_K, -1)
    fw_all = small_all[:, d + n_cw:d + n_cw + n_fw].reshape(N_DEV, FFN_K, -1)
    conv_w_f = cw_all.transpose(1, 0, 2).reshape(CONV_K, CONV_CH)
    ffn_w_f = fw_all.transpose(1, 0, 2).reshape(FFN_K, 2 * FFN_H)

    mcols = w_mod.shape[2]
    act = jnp.zeros((16, d), F32).at[:N_DEV].set(_silu(c_all)).at[N_DEV].set(_silu(c_ctx))
    mod_part = _mm(act, w_mod[0], 16, mcols, d, name="mod_fwd")
    mod_all = _all_gather([mod_part], name="gather_mod")[0]
    mod_all = mod_all.reshape(N_DEV, 16, mcols).transpose(1, 0, 2).reshape(16, 6 * d) + b_mod
    mod = lax.dynamic_slice_in_dim(mod_all, me, 1, axis=0)
    sh1, sc1, g1, sh2, sc2, g2 = [mod[:, k * d:(k + 1) * d] for k in range(6)]
    sh1c, sc1c = mod_all[N_DEV:N_DEV + 1, :d], mod_all[N_DEV:N_DEV + 1, d:2 * d]

    cos, sin = _rope_tables(seq)
    ones_c, zeros_c = jnp.ones((n_ctx, LANES), F32), jnp.zeros((n_ctx, LANES), F32)
    qk_w = jnp.concatenate([jnp.tile(q_norm_w, (1, N_Q_HEADS)), jnp.tile(k_norm_w, (1, N_KV_HEADS))], axis=1)
    kc_w = jnp.tile(k_norm_w, (1, N_KV_HEADS))

    h, (w_in_t,) = _prenorm(xs, norm_mix_w, sc1, sh1, carry=_gather_plan([w_in[0].T.astype(MXU_DT)]),
                            name="prenorm_mix")
    hc, _ = _prenorm(ctxs, norm_mix_w, sc1c, sh1c, name="prenorm_ctx")
    w_in_p = jnp.concatenate([_to_slots(w_in_t[:Q_COLS]), w_in_t[Q_COLS:QKV_COLS],
                              jnp.zeros((GLU_OFF - QKV_COLS, d), MXU_DT), w_in_t[QKV_COLS:]])
    proj = _mm(h, w_in_p, seq, IN_PAD, d, tb=True, name="proj_in")
    kv_ctx = _mm(hc, w_in_p, n_ctx, 2 * KV_COLS, d, tb=True, n0=Q_COLS, name="proj_ctx")
    qk_r = _qk_prep(proj, QK_COLS, cos, sin, qk_w, name="qk_prep")
    kc_n = _qk_prep(kv_ctx, KV_COLS, ones_c, zeros_c, kc_w, name="k_ctx_prep")
    (attn_o, lse), (w_up_t, w_out_f) = _attn_fwd(
        sink_logit, qk_r, proj, kc_n, kv_ctx, name="attn_fwd",
        carry=_gather_plan([w_up[0].T.astype(MXU_DT), w_out[0].astype(MXU_DT)]))
    w_out_f = jnp.concatenate([_to_slots(w_out_f[:Q_COLS]), w_out_f[Q_COLS:]])
    (u3, u1), (w_down_f,) = _conv_fwd(proj, GLU_OFF // GLU_COLS, conv_w_f, conv_b, conv_norm_w, conv_norm_b,
                                      carry=_gather_plan([w_down[0].astype(MXU_DT)]), name="conv_fwd")
    mix, x1, h2 = _out_proj_prenorm(attn_o, u3, w_out_f, xs, g1, norm_ffn_w, sc2, sh2, name="out_proj")

    up0 = _mm(h2, w_up_t, seq, 2 * FFN_H, d, tb=True, out_dtype=MXU_DT, name="ffn_up")
    act_a, gate_s, val_s = _ffn_act(up0, ffn_w_f, ffn_conv_b, name="ffn_act")
    loss_p, dy, dffn, dg2 = _ffn_down_loss(act_a, w_down_f, x1, g2, tgt, name="ffn_down_loss")

    da = _mm(dffn, w_down_f, seq, FFN_H, d, tb=True, out_dtype=MXU_DT, name="ffn_down_dx")
    gw_down = _mm(act_a, dffn, FFN_H, d, seq, ta=True, out_dtype=MXU_DT, name="ffn_down_dw")
    (dgate0, dval0, s_gate, s_val), (rx_down,) = _ffn_act_bwd(
        up0, gate_s, val_s, da, ffn_w_f, carry=_exchange_plan([gw_down]), name="ffn_act_bwd")
    gw_up_t = _mm(dgate0, h2, FFN_H, d, seq, ta=True, out_dtype=MXU_DT, out_rows=2 * FFN_H, name="ffn_up_dw_gate")
    gw_up_t = _mm(dval0, h2, FFN_H, d, seq, ta=True, into=gw_up_t, m0=FFN_H, name="ffn_up_dw_val")
    qd = d // 4
    dh2, (rx_up_a,) = _mm(dgate0, w_up_t, seq, d, FFN_H, a2=dval0, out_dtype=MXU_DT, name="ffn_up_dx",
                          carry=_exchange_plan([gw_up_t], cols=(0, qd)))
    (dx1, dmix, dsh2, dsc2, dnw2, dg1), _ = _norm_bwd(
        dh2, x1, norm_ffn_w, sc2, res=dy, gate=(mix, g1), name="prenorm_ffn_bwd")

    dattn = _mm(dmix, w_out_f, seq, Q_COLS, d, tb=True, out_dtype=MXU_DT, name="out_dx_attn")
    du3 = _mm(dmix, w_out_f, seq, CONV_CH, d, tb=True, n0=Q_COLS, name="out_dx_conv")
    gw_out = _mm(attn_o, dmix, Q_COLS, d, seq, ta=True, out_dtype=MXU_DT, out_rows=Q_COLS + CONV_CH,
                 name="out_dw_attn")
    gw_out = _mm(u3, dmix, CONV_CH, d, seq, ta=True, into=gw_out, m0=Q_COLS, name="out_dw_conv")
    gw_out = jnp.concatenate([_from_slots(gw_out[:Q_COLS]), gw_out[Q_COLS:]])
    (dq, dk, dv, dkc_r, dvc, dsink_rows), (rx_out, rx_up_b) = _attn_bwd(
        sink_logit, qk_r, proj, kc_n, kv_ctx, attn_o, lse, dattn, name="attn_bwd",
        carry=_join_plans(_exchange_plan([gw_out]), _exchange_plan([gw_up_t], cols=(qd, qd))))
    dproj, dqk_w = _qk_prep_bwd(proj, [dq, dk], QK_COLS, cos, sin, qk_w, tail=dv, name="qk_prep_bwd")
    (dproj, dcw8, dvec), (rx_up_c,) = _conv_bwd(
        proj, GLU_OFF // GLU_COLS, u1, du3, dproj, conv_w_f, conv_norm_w, conv_norm_b,
        carry=_exchange_plan([gw_up_t], cols=(2 * qd, 2 * qd)), name="conv_bwd")
    dkc, dkc_w = _qk_prep_bwd(kv_ctx, [dkc_r], KV_COLS, ones_c, zeros_c, kc_w, name="k_ctx_prep_bwd")
    dkv_ctx = jnp.concatenate([dkc, dvc.astype(MXU_DT)], axis=1)
    gw_p = _mm(dproj, h, IN_PAD, d, seq, ta=True, name="proj_dw")
    gw_ctx = _mm(dkv_ctx, hc, 2 * KV_COLS, d, n_ctx, ta=True, name="proj_dw_ctx")
    gw_in_t = jnp.concatenate([_from_slots(gw_p[:Q_COLS]), gw_p[Q_COLS:QKV_COLS] + gw_ctx, gw_p[GLU_OFF:]],
                              axis=0).astype(MXU_DT)
    dh, (rx_in,) = _mm(dproj, w_in_p, seq, d, IN_PAD, out_dtype=MXU_DT, carry=_exchange_plan([gw_in_t]),
                       name="proj_dx")
    dhc = _mm(dkv_ctx, w_in_p, n_ctx, d, 2 * KV_COLS, k0=Q_COLS, name="proj_dx_ctx")
    (grad_x, dsh1, dsc1, dnw1), _ = _norm_bwd(dh, xs, norm_mix_w, sc1, res=dx1, name="prenorm_mix_bwd")
    (dsh1c, dsc1c, dnw1c), _ = _norm_bwd(dhc, ctxs, norm_mix_w, sc1c, want_dx=False, name="prenorm_ctx_bwd")

    dmod = jnp.concatenate([dsh1, dsc1, dg1, dsh2, dsc2, dg2], axis=1)
    dmod_ctx = jnp.concatenate([dsh1c, dsc1c], axis=1)
    d_qn = dqk_w[0, :Q_COLS].reshape(N_Q_HEADS, HEAD_DIM).sum(0)
    d_kn = (dqk_w[0, Q_COLS:].reshape(N_KV_HEADS, HEAD_DIM).sum(0)
            + dkc_w[0].reshape(N_KV_HEADS, HEAD_DIM).sum(0))
    d_ffn_w = jnp.concatenate([s_gate[:FFN_K], s_val[:FFN_K]], axis=1)
    d_ffn_b = jnp.concatenate([s_gate[FFN_K], s_val[FFN_K]])
    d_sink = dsink_rows.reshape(N_Q_HEADS, BLOCK).sum(1)
    summed_like = [(dnw1 + dnw1c), d_qn[None], d_kn[None], d_sink[None], dvec[0:1], dvec[1:2],
                   dvec[2:3], dnw2, d_ffn_b[None], dcw8.sum(1), d_ffn_w, loss_p[0:1, 0:1]]
    pack = _pack([dmod, dmod_ctx] + summed_like)
    pack_all = _all_gather([pack], name="gather_small_grads")[0]
    pack_all = pack_all.reshape(N_DEV, pack.shape[0], LANES)
    tot = _sum_parts(pack_all, name="sum_small_grads")
    (dmod_sum, dmc_sum, g_nmix, g_qn, g_kn, g_sink, g_cb, g_lw, g_lb, g_nffn, g_fb, g_cw_f, g_fw_f,
     loss_sum) = _unpack(tot, [dmod, dmod_ctx] + summed_like)
    loss = loss_sum[0, 0]
    dmod_all = pack_all.reshape(N_DEV, -1)[:, :6 * d]
    g_b_mod = dmod_sum.at[:, :2 * d].add(dmc_sum)

    lo = me * mcols
    dm_rows = jnp.zeros((16, 6 * d), F32).at[:N_DEV].set(dmod_all).at[N_DEV, :2 * d].set(dmc_sum[0])
    dm_mine = lax.dynamic_slice_in_dim(dm_rows, lo, mcols, axis=1)
    parts_mod = _mm(act, dm_mine, d, mcols, 16, ta=True, name="mod_dw")[None]
    dact_part = _mm(dm_mine[N_DEV:N_DEV + 8], w_mod[0], 8, d, mcols, tb=True, name="mod_dx_ctx")
    dact_all = _all_gather([dact_part], name="gather_c_ctx_grad")[0].reshape(N_DEV, 8, d)
    dact = _sum_parts(dact_all, name="sum_c_ctx_grad")[0]
    sg = jax.nn.sigmoid(c_ctx)
    g_c_ctx = dact * (sg * (1.0 + c_ctx * (1.0 - sg)))

    def stacked(rx):
        return rx.reshape(N_DEV, rx.shape[0] // N_DEV, rx.shape[1])

    g_w_in = _sum_parts(stacked(rx_in), name="sum_w_in").T[None]
    g_w_up = jnp.concatenate([_sum_parts(stacked(rx_up_a), name="sum_w_up_a"),
                              _sum_parts(stacked(rx_up_b), name="sum_w_up_b"),
                              _sum_parts(stacked(rx_up_c), name="sum_w_up_c")], axis=1).T[None]
    big = {}
    big["w_in"] = _adam(w_in[0], m_w_in[0], v_w_in[0], g_w_in, name="adam_w_in")
    big["w_up"] = _adam(w_up[0], m_w_up[0], v_w_up[0], g_w_up, name="adam_w_up")
    big["w_out"] = _adam(w_out[0], m_w_out[0], v_w_out[0], stacked(rx_out), name="adam_w_out")
    big["w_down"] = _adam(w_down[0], m_w_down[0], v_w_down[0], stacked(rx_down), name="adam_w_down")
    big["w_mod"] = _adam(w_mod[0], m_w_mod[0], v_w_mod[0], parts_mod, name="adam_w_mod")

    ccols, fcols = conv_w.shape[2], ffn_conv_w.shape[2]
    g_conv_w = lax.dynamic_slice_in_dim(g_cw_f, me * ccols, ccols, axis=1)[None]
    g_ffn_w = lax.dynamic_slice_in_dim(g_fw_f, me * fcols, fcols, axis=1)[None]
    names = ["c_ctx", "b_mod", "norm_mix_w", "q_norm_w", "k_norm_w", "sink_logit", "conv_w", "conv_b",
             "conv_norm_w", "conv_norm_b", "norm_ffn_w", "ffn_conv_w", "ffn_conv_b"]
    ws = [c_ctx, b_mod, norm_mix_w, q_norm_w, k_norm_w, sink_logit, conv_w, conv_b, conv_norm_w, conv_norm_b,
          norm_ffn_w, ffn_conv_w, ffn_conv_b]
    msm = [m_c_ctx, m_b_mod, m_norm_mix_w, m_q_norm_w, m_k_norm_w, m_sink_logit, m_conv_w, m_conv_b,
           m_conv_norm_w, m_conv_norm_b, m_norm_ffn_w, m_ffn_conv_w, m_ffn_conv_b]
    vsm = [v_c_ctx, v_b_mod, v_norm_mix_w, v_q_norm_w, v_k_norm_w, v_sink_logit, v_conv_w, v_conv_b,
           v_conv_norm_w, v_conv_norm_b, v_norm_ffn_w, v_ffn_conv_w, v_ffn_conv_b]
    gsm = [g_c_ctx, g_b_mod, g_nmix, g_qn, g_kn, g_sink, g_conv_w, g_cb, g_lw, g_lb, g_nffn, g_ffn_w, g_fb]
    deltas, new_ms, new_vs = _adam_many(ws, msm, vsm, gsm, name="adam_small")
    sm = {nm: vals for nm, vals in zip(names, zip(gsm, deltas, new_ms, new_vs))}

    def out4(nm):
        if nm in sm:
            return sm[nm]
        return tuple(t[None] for t in big[nm])

    order = ["c_ctx", "w_mod", "b_mod", "norm_mix_w", "w_in", "q_norm_w", "k_norm_w", "sink_logit", "conv_w",
             "conv_b", "conv_norm_w", "conv_norm_b", "w_out", "norm_ffn_w", "w_up", "ffn_conv_w", "ffn_conv_b",
             "w_down"]
    quads = [out4(nm) for nm in order]
    return (loss, grad_x[None], *[q[0] for q in quads], *[q[1] for q in quads],
            *[q[2] for q in quads], *[q[3] for q in quads])
```

```python
import math
from typing import Callable, NamedTuple

import jax
import jax.numpy as jnp
from jax import lax
from jax.experimental import pallas as pl
from jax.experimental.pallas import tpu as pltpu

F32 = jnp.float32
MXU_DT = jnp.bfloat16

D_MODEL = 1024
GRID_W = 64
HEAD_DIM = 64
N_Q_HEADS = 8
N_KV_HEADS = 2
GQA_GROUP = 4
WINDOW = 128
BLOCK = 128
Q_COLS = 512
KV_COLS = 128
QK_COLS = Q_COLS + KV_COLS
QKV_COLS = Q_COLS + 2 * KV_COLS
CONV_CH = 512
GLU_COLS = 2 * CONV_CH
IN_COLS = QKV_COLS + GLU_COLS
CONV_K = 31
CONV_PAD = 15
FFN_H = 2816
FFN_K = 3
ROPE_BASE = 10000.0
EPS = 1e-6
NEG_INF = -1e30
N_DEV = 8
HALO = 16
LANES = 128
ROW_TS = 512

ADAM_LR = 0.001
ADAM_B1 = 0.9
ADAM_B2 = 0.999
ADAM_EPS = 1e-08
ADAM_WD = 0.01
ADAM_STEP = 10

MESH = pl.DeviceIdType.MESH
VMEM_LIMIT = 56 << 20
MM_VMEM_BUDGET = 44 << 20
GLU_OFF = 1024
IN_PAD = GLU_OFF + GLU_COLS


def _cp(*dims):
    return pltpu.CompilerParams(dimension_semantics=dims or None, vmem_limit_bytes=VMEM_LIMIT)


def _row(ts, w, col=0):
    return pl.BlockSpec((ts, w), lambda i: (i, col))


def _vec(w):
    return pl.BlockSpec((1, w), lambda i: (0, 0))


def _colsum(v):
    return jnp.sum(v, axis=0, keepdims=True)


def _sigmoid(v):
    return 0.5 * jnp.tanh(0.5 * v) + 0.5


def _mm(a, b, m, n, k, *, ta=False, tb=False, n0=0, k0=0, add=None, out_dtype=F32, into=None, m0=0,
        out_rows=None, a2=None, carry=None, name):
    has_add, has_a2 = add is not None, a2 is not None
    assert not (has_a2 and (ta or tb))
    if into is not None:
        out_dtype = into.dtype
    sa, sb, so = a.dtype.itemsize, b.dtype.itemsize, jnp.dtype(out_dtype).itemsize
    sadd = add.dtype.itemsize if has_add else 0
    na = 2 if has_a2 else 1

    def fits(tm, tn):
        return 2 * (na * k * (tm * sa + tn * sb) + tm * tn * (so + sadd)) <= MM_VMEM_BUDGET

    tms = [m] if m <= 1024 else [t for t in (1024, 1408, 768, 512, 256, 128) if m % t == 0]
    tns = [t for t in ((1024, 512, 256, 128) if ta else (1408, 512, 256, 128)) if n % t == 0 and n0 % t == 0]
    tm, tn = next((tm, tn) for tm in tms for tn in tns if fits(tm, tn))
    a_spec = (pl.BlockSpec((k, tm), lambda i, j: (0, i)) if ta else pl.BlockSpec((tm, k), lambda i, j: (i, 0)))
    nb0 = n0 // tn
    if tb:
        assert k0 == 0
        b_spec = pl.BlockSpec((tn, k), lambda i, j: (j + nb0, 0))
    else:
        assert k0 % (na * k) == 0, (k0, k)
        kb0 = k0 // (na * k)
        b_spec = pl.BlockSpec((na * k, tn), lambda i, j: (kb0, j + nb0))
    assert m0 % tm == 0, (m0, tm)
    mb0 = m0 // tm
    o_spec = pl.BlockSpec((tm, tn), lambda i, j: (i + mb0, j))
    dims = (((0 if ta else 1,), (1 if tb else 0,)), ((), ()))

    def body(*refs):
        a_ref, b_ref, o_ref = refs[0], refs[na], refs[-1]
        if has_a2:
            res = (jnp.dot(a_ref[...].astype(MXU_DT), b_ref[0:k, :].astype(MXU_DT), preferred_element_type=F32)
                   + jnp.dot(refs[1][...].astype(MXU_DT), b_ref[k:2 * k, :].astype(MXU_DT),
                             preferred_element_type=F32))
        else:
            res = lax.dot_general(a_ref[...].astype(MXU_DT), b_ref[...].astype(MXU_DT), dims,
                                  preferred_element_type=F32)
        if has_add:
            res = res + refs[na + 1][...].astype(F32)
        o_ref[...] = res.astype(o_ref.dtype)

    ins = [a] + ([a2] if has_a2 else []) + [b] + ([add] if has_add else []) + ([into] if into is not None else [])
    specs = ([a_spec] * na + [b_spec] + ([pl.BlockSpec((tm, tn), lambda i, j: (i, j))] if has_add else [])
             + ([pl.BlockSpec(memory_space=pl.ANY)] if into is not None else []))
    out_shape = (jax.ShapeDtypeStruct(into.shape, into.dtype) if into is not None
                 else jax.ShapeDtypeStruct((out_rows or m, n), out_dtype))
    (out,), carried = _grid_call(
        body, ins, carry, name=name, grid=(m // tm, n // tn), in_specs=specs, out_specs=[o_spec],
        out_shape=[out_shape], input_output_aliases={len(ins) - 1: 0} if into is not None else None,
        dims=("parallel", "parallel"))
    return out if carry is None else (out, carried)


def _rms_stats(xv):
    r = lax.rsqrt(jnp.mean(xv * xv, axis=-1, keepdims=True) + EPS)
    return r, xv * r


def _prenorm(x, nw, sc, sh, *, carry=None, name):
    rows, d = x.shape
    ts = min(rows, ROW_TS)

    def body(x_ref, nw_ref, sc_ref, sh_ref, h_ref):
        _, xn = _rms_stats(x_ref[...])
        h_ref[...] = ((xn * nw_ref[...]) * (1.0 + sc_ref[...]) + sh_ref[...]).astype(h_ref.dtype)

    (h,), carried = _grid_call(
        body, [x, nw, sc, sh], carry, name=name, grid=(rows // ts,),
        in_specs=[_row(ts, d), _vec(d), _vec(d), _vec(d)], out_specs=[_row(ts, d)],
        out_shape=[jax.ShapeDtypeStruct((rows, d), MXU_DT)], dims=("parallel",))
    return h, carried


def _out_proj_prenorm(attn_o, u3, w_out, x, g1, nw, sc, sh, *, name):
    rows, d = x.shape
    ka, ku = attn_o.shape[1], u3.shape[1]
    tm = min(rows, ROW_TS)

    def body(a_ref, u_ref, w_ref, x_ref, g_ref, nw_ref, sc_ref, sh_ref, mix_ref, x1_ref, h_ref):
        mix = (jnp.dot(a_ref[...].astype(MXU_DT), w_ref[0:ka, :].astype(MXU_DT), preferred_element_type=F32)
               + jnp.dot(u_ref[...].astype(MXU_DT), w_ref[ka:ka + ku, :].astype(MXU_DT),
                         preferred_element_type=F32))
        mix_ref[...] = mix
        x1 = x_ref[...] + g_ref[...] * mix
        x1_ref[...] = x1
        _, xn = _rms_stats(x1)
        h_ref[...] = ((xn * nw_ref[...]) * (1.0 + sc_ref[...]) + sh_ref[...]).astype(h_ref.dtype)

    return pl.pallas_call(
        body, name=name, grid=(rows // tm,),
        in_specs=[_row(tm, ka), _row(tm, ku), pl.BlockSpec((ka + ku, d), lambda i: (0, 0)), _row(tm, d),
                  _vec(d), _vec(d), _vec(d), _vec(d)],
        out_specs=[_row(tm, d), _row(tm, d), _row(tm, d)],
        out_shape=[jax.ShapeDtypeStruct((rows, d), F32), jax.ShapeDtypeStruct((rows, d), F32),
                   jax.ShapeDtypeStruct((rows, d), MXU_DT)],
        compiler_params=_cp("parallel"),
    )(attn_o, u3, w_out, x, g1, nw, sc, sh)


def _ffn_down_loss(act, w_down, x1, g2, target, *, name):
    rows, d = x1.shape
    k = act.shape[1]
    tm, tn = min(rows, 1024), 512

    def body(a_ref, b_ref, x1_ref, g_ref, t_ref, loss_ref, dy_ref, dffn_ref, dg_ref):
        j, i = pl.program_id(0), pl.program_id(1)
        f = jnp.dot(a_ref[...].astype(MXU_DT), b_ref[...].astype(MXU_DT), preferred_element_type=F32)
        e = x1_ref[...] + g_ref[...] * f - t_ref[...]
        part = (0.5 / d) * jnp.sum(jnp.sum(e * e, axis=-1, keepdims=True), axis=0, keepdims=True)
        dy = e * (1.0 / d)
        dy_ref[...] = dy
        dffn_ref[...] = (dy * g_ref[...]).astype(dffn_ref.dtype)

        @pl.when((i == 0) & (j == 0))
        def _():
            loss_ref[...] = jnp.zeros_like(loss_ref)

        @pl.when(i == 0)
        def _():
            dg_ref[...] = jnp.zeros_like(dg_ref)

        loss_ref[...] += jnp.broadcast_to(part, loss_ref.shape)
        dg_ref[...] += _colsum(dy * f)

    tile = pl.BlockSpec((tm, tn), lambda j, i: (i, j))
    vec = pl.BlockSpec((1, tn), lambda j, i: (0, j))
    return pl.pallas_call(
        body, name=name, grid=(d // tn, rows // tm),
        in_specs=[pl.BlockSpec((tm, k), lambda j, i: (i, 0)), pl.BlockSpec((k, tn), lambda j, i: (0, j)),
                  tile, vec, tile],
        out_specs=[pl.BlockSpec((8, LANES), lambda j, i: (0, 0)), tile, tile, vec],
        out_shape=[jax.ShapeDtypeStruct((8, LANES), F32), jax.ShapeDtypeStruct((rows, d), F32),
                   jax.ShapeDtypeStruct((rows, d), MXU_DT), jax.ShapeDtypeStruct((1, d), F32)],
        compiler_params=_cp("arbitrary", "arbitrary"),
    )(act, w_down, x1, g2, target)


def _norm_bwd(dh, xin, nw, sc, *, res=None, gate=None, want_dx=True, carry=None, name):
    rows, d = xin.shape
    ts = min(rows, ROW_TS)
    has_res, has_gate = res is not None, gate is not None

    def body(*refs):
        it = iter(refs)
        dh_ref, x_ref, nw_ref, sc_ref = next(it), next(it), next(it), next(it)
        res_ref = next(it) if has_res else None
        gated_ref, g_ref = (next(it), next(it)) if has_gate else (None, None)
        dx_ref = next(it) if want_dx else None
        dgx_ref = next(it) if has_gate else None
        dsh_ref, dsc_ref, dnw_ref = next(it), next(it), next(it)
        dg_ref = next(it) if has_gate else None
        i = pl.program_id(0)
        dhv = dh_ref[...].astype(F32)
        r, xn = _rms_stats(x_ref[...])
        dn = dhv * (1.0 + sc_ref[...])

        @pl.when(i == 0)
        def _():
            dsh_ref[...] = jnp.zeros_like(dsh_ref)
            dsc_ref[...] = jnp.zeros_like(dsc_ref)
            dnw_ref[...] = jnp.zeros_like(dnw_ref)
            if has_gate:
                dg_ref[...] = jnp.zeros_like(dg_ref)

        dsh_ref[...] += _colsum(dhv)
        dsc_ref[...] += _colsum(dhv * (xn * nw_ref[...]))
        dnw_ref[...] += _colsum(dn * xn)
        if want_dx:
            dxn = dn * nw_ref[...]
            dx = r * (dxn - xn * jnp.mean(dxn * xn, axis=-1, keepdims=True))
            if has_res:
                dx = dx + res_ref[...]
            dx_ref[...] = dx
            if has_gate:
                dgx_ref[...] = (dx * g_ref[...]).astype(dgx_ref.dtype)
                dg_ref[...] += _colsum(dx * gated_ref[...])

    ins = [dh, xin, nw, sc] + ([res] if has_res else []) + (list(gate) if has_gate else [])
    in_specs = ([_row(ts, d), _row(ts, d), _vec(d), _vec(d)] + ([_row(ts, d)] if has_res else [])
                + ([_row(ts, d), _vec(d)] if has_gate else []))
    out_specs, out_shape = [], []
    if want_dx:
        out_specs.append(_row(ts, d)); out_shape.append(jax.ShapeDtypeStruct((rows, d), F32))
    if has_gate:
        out_specs.append(_row(ts, d)); out_shape.append(jax.ShapeDtypeStruct((rows, d), MXU_DT))
    for _ in range(3 + int(has_gate)):
        out_specs.append(_vec(d)); out_shape.append(jax.ShapeDtypeStruct((1, d), F32))
    return _grid_call(body, ins, carry, name=name, grid=(rows // ts,), in_specs=in_specs, out_specs=out_specs,
                      out_shape=out_shape, dims=("arbitrary",))


def _group_sum(v, g):
    hi = v.astype(MXU_DT)
    lo = (v - hi.astype(F32)).astype(MXU_DT)
    return (jnp.dot(hi, g, preferred_element_type=F32) + jnp.dot(lo, g, preferred_element_type=F32))


def _rot(v):
    lane = lax.broadcasted_iota(jnp.int32, v.shape, 1)
    first = (lane & 31) < 16
    return jnp.where(first, -pltpu.roll(v, LANES - 16, 1), pltpu.roll(v, 16, 1))


def _head_group_matrix():
    r = jnp.arange(LANES) // HEAD_DIM
    return (r[:, None] == r[None, :]).astype(MXU_DT)


def _qk_prep(xin, width, cos, sin, w, *, name):
    rows = xin.shape[0]
    ts = min(rows, ROW_TS)
    nch = width // LANES

    def body(x_ref, cos_ref, sin_ref, w_ref, g_ref, o_ref):
        cs, sn, g = cos_ref[...], sin_ref[...], g_ref[...]
        for ch in range(nch):
            sl = slice(ch * LANES, (ch + 1) * LANES)
            xv = x_ref[:, sl]
            r = lax.rsqrt(_group_sum(xv * xv, g) * (1.0 / HEAD_DIM) + EPS)
            yw = (xv * r) * w_ref[:, sl]
            o_ref[:, sl] = (yw * cs + _rot(yw) * sn).astype(o_ref.dtype)

    return pl.pallas_call(
        body, name=name, grid=(rows // ts,),
        in_specs=[_row(ts, width), _row(ts, LANES), _row(ts, LANES), _vec(width),
                  pl.BlockSpec((LANES, LANES), lambda i: (0, 0))],
        out_specs=_row(ts, width),
        out_shape=jax.ShapeDtypeStruct((rows, width), MXU_DT), compiler_params=_cp("parallel"),
    )(xin, cos, sin, w, _head_group_matrix())


def _qk_prep_bwd(xin, douts, width, cos, sin, w, *, tail=None, name):
    rows = xin.shape[0]
    ts = min(rows, ROW_TS)
    nch = width // LANES
    has_tail = tail is not None
    nd = len(douts)
    assert sum(v.shape[1] for v in douts) == width
    src = [(k, c) for k, v in enumerate(douts) for c in range(v.shape[1] // LANES)]

    def body(*refs):
        x_ref, d_refs = refs[0], refs[1:1 + nd]
        t_ref = refs[1 + nd] if has_tail else None
        cos_ref, sin_ref, w_ref, g_ref, dx_ref, dw_ref = refs[1 + nd + int(has_tail):]
        i = pl.program_id(0)
        cs, sn, g = cos_ref[...], sin_ref[...], g_ref[...]

        @pl.when(i == 0)
        def _():
            dw_ref[...] = jnp.zeros_like(dw_ref)

        if has_tail:
            dx_ref[:, width:width + LANES] = t_ref[...].astype(dx_ref.dtype)
            dx_ref[:, width + LANES:] = jnp.zeros((ts, GLU_OFF - width - LANES), dx_ref.dtype)

        for ch in range(nch):
            sl = slice(ch * LANES, (ch + 1) * LANES)
            xv = x_ref[:, sl]
            dv = d_refs[src[ch][0]][:, src[ch][1] * LANES:(src[ch][1] + 1) * LANES].astype(F32)
            r = lax.rsqrt(_group_sum(xv * xv, g) * (1.0 / HEAD_DIM) + EPS)
            n = xv * r
            dyw = dv * cs - _rot(dv * sn)
            dw_ref[:, sl] += _colsum(dyw * n)
            dn = dyw * w_ref[:, sl]
            gm = _group_sum(dn * n, g) * (1.0 / HEAD_DIM)
            dx_ref[:, sl] = (r * (dn - n * gm)).astype(dx_ref.dtype)

    ins = [xin] + list(douts) + ([tail] if has_tail else []) + [cos, sin, w, _head_group_matrix()]
    in_specs = ([_row(ts, width)] + [_row(ts, v.shape[1]) for v in douts] + ([_row(ts, LANES)] if has_tail else [])
                + [_row(ts, LANES), _row(ts, LANES), _vec(width), pl.BlockSpec((LANES, LANES), lambda i: (0, 0))])
    out_w, arr_w = (GLU_OFF, IN_PAD) if has_tail else (width, width)
    return pl.pallas_call(
        body, name=name, grid=(rows // ts,), in_specs=in_specs,
        out_specs=[_row(ts, out_w), _vec(width)],
        out_shape=[jax.ShapeDtypeStruct((rows, arr_w), MXU_DT), jax.ShapeDtypeStruct((1, width), F32)],
        compiler_params=_cp("arbitrary"),
    )(*ins)


GB = GQA_GROUP * BLOCK
WIN = 3 * BLOCK
ATT_STEP = 8


def _win_start(i, seq):
    return pl.multiple_of(jnp.clip((i - 1) * BLOCK, 0, seq - WIN), BLOCK)


def _attn_mask(i, start):
    qpos = i * BLOCK + (lax.broadcasted_iota(jnp.int32, (GB, WIN), 0) & (BLOCK - 1))
    kpos = start + lax.broadcasted_iota(jnp.int32, (GB, WIN), 1)
    return jnp.abs(qpos - kpos) <= WINDOW


def _sink_col(sink_ref, kv):
    return jnp.concatenate(
        [jnp.full((BLOCK, 1), sink_ref[0, kv * GQA_GROUP + g], F32) for g in range(GQA_GROUP)], axis=0)


def _stack_slots(ref, rows):
    return jnp.concatenate([ref[rows, g * LANES:(g + 1) * LANES] for g in range(GQA_GROUP)], axis=0)


def _kv_lanes(kv):
    lane = lax.broadcasted_iota(jnp.int32, (1, LANES), 1)
    return (lane < HEAD_DIM) if kv == 0 else (lane >= HEAD_DIM)


def _keep(mask, v):
    return jnp.where(mask, v, jnp.zeros_like(v))


_NT = (((1,), (1,)), ((), ()))
_TN = (((0,), (0,)), ((), ()))


def _attn_specs(seq, n_ctx):
    qs = pl.BlockSpec((ATT_STEP * BLOCK, Q_COLS), lambda i: (i, 0))
    ks = pl.BlockSpec((seq, KV_COLS), lambda i: (0, Q_COLS // KV_COLS))
    vs = pl.BlockSpec((seq, KV_COLS), lambda i: (0, QK_COLS // KV_COLS))
    kcs = pl.BlockSpec((n_ctx, KV_COLS), lambda i: (0, 0))
    vcs = pl.BlockSpec((n_ctx, KV_COLS), lambda i: (0, 1))
    ls = pl.BlockSpec((ATT_STEP, N_KV_HEADS * GB, 1), lambda i: (i, 0, 0))
    return qs, ks, vs, kcs, vcs, ls


def _attn_fwd(sink, qk, proj, kc, kv_ctx, *, carry=None, name):
    seq, n_ctx = qk.shape[0], kc.shape[0]
    scale = 1.0 / math.sqrt(HEAD_DIM)

    def one_block(blk, sub, sink_ref, q_ref, k_ref, v_ref, kcw, vcw, o_ref, lse_ref):
        rows = slice(sub * BLOCK, (sub + 1) * BLOCK)
        start = _win_start(blk, seq)
        valid = _attn_mask(blk, start)
        qs = _stack_slots(q_ref, rows)
        kw, vw = k_ref[pl.ds(start, WIN), :], v_ref[pl.ds(start, WIN), :].astype(MXU_DT)
        o_all = jnp.zeros((GB, LANES), F32)
        for kv in range(N_KV_HEADS):
            mine = _kv_lanes(kv)
            s_loc = lax.dot_general(qs, _keep(mine, kw), _NT, preferred_element_type=F32) * scale
            s_loc = jnp.where(valid, s_loc, NEG_INF)
            s_ctx = lax.dot_general(qs, _keep(mine, kcw), _NT, preferred_element_type=F32) * scale
            sk = _sink_col(sink_ref, kv)
            m = jnp.maximum(jnp.maximum(jnp.max(s_loc, axis=-1, keepdims=True),
                                        jnp.max(s_ctx, axis=-1, keepdims=True)), sk)
            p_loc = jnp.exp(s_loc - m)
            p_ctx = jnp.exp(s_ctx - m)
            l = (jnp.sum(p_loc, axis=-1, keepdims=True) + jnp.sum(p_ctx, axis=-1, keepdims=True)
                 + jnp.exp(sk - m))
            o_all = o_all + (jnp.dot(p_loc.astype(MXU_DT), _keep(mine, vw), preferred_element_type=F32)
                             + jnp.dot(p_ctx.astype(MXU_DT), _keep(mine, vcw), preferred_element_type=F32)) / l
            lse_ref[sub, kv * GB:(kv + 1) * GB, :] = m + jnp.log(l)
        for g in range(GQA_GROUP):
            o_ref[rows, g * LANES:(g + 1) * LANES] = o_all[g * BLOCK:(g + 1) * BLOCK].astype(o_ref.dtype)

    def body(sink_ref, q_ref, k_ref, v_ref, kc_ref, vc_ref, o_ref, lse_ref):
        i = pl.program_id(0)
        kcw, vcw = kc_ref[...], vc_ref[...].astype(MXU_DT)
        for sub in range(ATT_STEP):
            one_block(i * ATT_STEP + sub, sub, sink_ref, q_ref, k_ref, v_ref, kcw, vcw, o_ref, lse_ref)

    qs, ks, vs, kcs, vcs, ls = _attn_specs(seq, n_ctx)
    return _grid_call(
        body, [sink, qk, qk, proj, kc, kv_ctx], carry, name=name, grid=(seq // (ATT_STEP * BLOCK),),
        in_specs=[pl.BlockSpec(memory_space=pltpu.SMEM), qs, ks, vs, kcs, vcs],
        out_specs=[qs, ls],
        out_shape=[jax.ShapeDtypeStruct((seq, Q_COLS), MXU_DT),
                   jax.ShapeDtypeStruct((seq // BLOCK, N_KV_HEADS * GB, 1), F32)],
        dims=("parallel",))


def _attn_bwd(sink, qk, proj, kc, kv_ctx, o, lse, do, *, carry=None, name):
    seq, n_ctx = qk.shape[0], kc.shape[0]
    scale = 1.0 / math.sqrt(HEAD_DIM)

    def body(sink_ref, q_ref, k_ref, v_ref, kc_ref, vc_ref, o_ref, lse_ref, do_ref,
             dq_ref, dk_ref, dv_ref, dkc_ref, dvc_ref, ds_ref):
        i = pl.program_id(0)

        @pl.when(i == 0)
        def _():
            dk_ref[...] = jnp.zeros_like(dk_ref)
            dv_ref[...] = jnp.zeros_like(dv_ref)
            dkc_ref[...] = jnp.zeros_like(dkc_ref)
            dvc_ref[...] = jnp.zeros_like(dvc_ref)
            ds_ref[...] = jnp.zeros_like(ds_ref)

        kcw, vcw = kc_ref[...], vc_ref[...].astype(MXU_DT)
        dkc, dvc = jnp.zeros((n_ctx, LANES), F32), jnp.zeros((n_ctx, LANES), F32)
        for sub in range(ATT_STEP):
            dkc_s, dvc_s = one_block(i * ATT_STEP + sub, sub, sink_ref, q_ref, k_ref, v_ref, kcw, vcw, o_ref,
                                     lse_ref, do_ref, dq_ref, dk_ref, dv_ref, ds_ref)
            dkc, dvc = dkc + dkc_s, dvc + dvc_s
        dkc_ref[...] += dkc
        dvc_ref[...] += dvc

    def one_block(blk, sub, sink_ref, q_ref, k_ref, v_ref, kcw, vcw, o_ref, lse_ref, do_ref,
                  dq_ref, dk_ref, dv_ref, ds_ref):
        qrows = slice(sub * BLOCK, (sub + 1) * BLOCK)
        start = _win_start(blk, seq)
        valid = _attn_mask(blk, start)
        win = pl.ds(start, WIN)
        qs, dos = _stack_slots(q_ref, qrows), _stack_slots(do_ref, qrows)
        do_o = dos.astype(F32) * _stack_slots(o_ref, qrows).astype(F32)
        kw, vw = k_ref[win, :], v_ref[win, :].astype(MXU_DT)
        dq = jnp.zeros((GB, LANES), F32)
        dk, dv = jnp.zeros((WIN, LANES), F32), jnp.zeros((WIN, LANES), F32)
        dkc, dvc = jnp.zeros((n_ctx, LANES), F32), jnp.zeros((n_ctx, LANES), F32)
        for kv in range(N_KV_HEADS):
            mine = _kv_lanes(kv)
            rows = slice(kv * GB, (kv + 1) * GB)
            lse_s = lse_ref[sub, rows, :]
            delta = jnp.sum(_keep(mine, do_o), axis=-1, keepdims=True)
            kz, vz, kcz, vcz = _keep(mine, kw), _keep(mine, vw), _keep(mine, kcw), _keep(mine, vcw)
            s_loc = lax.dot_general(qs, kz, _NT, preferred_element_type=F32) * scale
            s_loc = jnp.where(valid, s_loc, NEG_INF)
            s_ctx = lax.dot_general(qs, kcz, _NT, preferred_element_type=F32) * scale
            p_loc = jnp.exp(s_loc - lse_s)
            p_ctx = jnp.exp(s_ctx - lse_s)
            p_sink = jnp.exp(_sink_col(sink_ref, kv) - lse_s)
            dp_loc = lax.dot_general(dos, vz, _NT, preferred_element_type=F32)
            dp_ctx = lax.dot_general(dos, vcz, _NT, preferred_element_type=F32)
            ds_loc = (p_loc * (dp_loc - delta) * scale).astype(MXU_DT)
            ds_ctx = (p_ctx * (dp_ctx - delta) * scale).astype(MXU_DT)
            dq = dq + (jnp.dot(ds_loc, kz, preferred_element_type=F32)
                       + jnp.dot(ds_ctx, kcz, preferred_element_type=F32))
            dk = dk + _keep(mine, lax.dot_general(ds_loc, qs, _TN, preferred_element_type=F32))
            dv = dv + _keep(mine, lax.dot_general(p_loc.astype(MXU_DT), dos, _TN, preferred_element_type=F32))
            dkc = dkc + _keep(mine, lax.dot_general(ds_ctx, qs, _TN, preferred_element_type=F32))
            dvc = dvc + _keep(mine, lax.dot_general(p_ctx.astype(MXU_DT), dos, _TN, preferred_element_type=F32))
            ds_ref[rows, :] += -(p_sink * delta)
        for g in range(GQA_GROUP):
            dq_ref[qrows, g * LANES:(g + 1) * LANES] = dq[g * BLOCK:(g + 1) * BLOCK]
        dk_ref[win, :] += dk
        dv_ref[win, :] += dv
        return dkc, dvc

    qs, ks, vs, kcs, vcs, ls = _attn_specs(seq, n_ctx)
    whole = lambda r, c: pl.BlockSpec((r, c), lambda i: (0, 0))
    return _grid_call(
        body, [sink, qk, qk, proj, kc, kv_ctx, o, lse, do], carry, name=name, grid=(seq // (ATT_STEP * BLOCK),),
        in_specs=[pl.BlockSpec(memory_space=pltpu.SMEM), qs, ks, vs, kcs, vcs, qs, ls, qs],
        out_specs=[qs, whole(seq, KV_COLS), whole(seq, KV_COLS), whole(n_ctx, KV_COLS), whole(n_ctx, KV_COLS),
                   whole(N_KV_HEADS * GB, 1)],
        out_shape=[jax.ShapeDtypeStruct((seq, Q_COLS), F32), jax.ShapeDtypeStruct((seq, KV_COLS), F32),
                   jax.ShapeDtypeStruct((seq, KV_COLS), F32), jax.ShapeDtypeStruct((n_ctx, KV_COLS), F32),
                   jax.ShapeDtypeStruct((n_ctx, KV_COLS), F32), jax.ShapeDtypeStruct((N_KV_HEADS * GB, 1), F32)],
        dims=("arbitrary",))


def _halo_specs(ts, w, rows, col=0):
    per = ts // HALO
    last = rows // HALO - 1
    return [pl.BlockSpec((HALO, w), lambda i: (jnp.maximum(i * per - 1, 0), col)),
            pl.BlockSpec((ts, w), lambda i: (i, col)),
            pl.BlockSpec((HALO, w), lambda i: (jnp.minimum((i + 1) * per, last), col))]


def _glu(v):
    return v[:, :CONV_CH] * _sigmoid(v[:, CONV_CH:])


def _ln_stats(u):
    mu = jnp.mean(u, axis=-1, keepdims=True)
    xc = u - mu
    rstd = lax.rsqrt(jnp.mean(xc * xc, axis=-1, keepdims=True) + EPS)
    return xc * rstd, rstd


CONV_BWD_TS = 128


def _phases(ext_ref, ph_ref):
    n = ph_ref.shape[1]
    for b in range(1, 8):
        ph_ref[b - 1] = ext_ref[b:b + n, :]


def _window(ext_ref, ph_ref, o, n, cs):
    a, b = divmod(o, 8)
    src = ext_ref if b == 0 else ph_ref.at[b - 1]
    return src[8 * a:8 * a + n, cs]


def _conv_fwd(glu, col, cw, cb, lw, lb, *, carry=None, name):
    rows = glu.shape[0]
    ts = min(rows, 256)
    nt = rows // ts

    te = ts + 2 * HALO

    def body(gp_ref, g_ref, gn_ref, cw_ref, cb_ref, lw_ref, lb_ref, u3_ref, u1_ref, ext_ref, ph_ref):
        i = pl.program_id(0)
        ext_ref[0:HALO, :] = jnp.where(i > 0, _glu(gp_ref[...]), 0.0)
        ext_ref[HALO:HALO + ts, :] = _glu(g_ref[...])
        ext_ref[HALO + ts:, :] = jnp.where(i < nt - 1, _glu(gn_ref[...]), 0.0)
        _phases(ext_ref, ph_ref)
        for c in range(CONV_CH // LANES):
            cs = slice(c * LANES, (c + 1) * LANES)
            acc = jnp.broadcast_to(cb_ref[:, cs], (ts, LANES))
            for j in range(CONV_K):
                acc = acc + cw_ref[j:j + 1, cs] * _window(ext_ref, ph_ref, HALO - CONV_PAD + j, ts, cs)
            u1_ref[:, cs] = acc
        xh, _ = _ln_stats(u1_ref[...])
        u2 = xh * lw_ref[...] + lb_ref[...]
        u3_ref[...] = (u2 * _sigmoid(u2)).astype(u3_ref.dtype)

    full = lambda shape: pl.BlockSpec(shape, lambda i: (0,) * len(shape))
    return _grid_call(
        body, [glu, glu, glu, cw, cb, lw, lb], carry, name=name, grid=(nt,),
        in_specs=_halo_specs(ts, GLU_COLS, rows, col) + [full((CONV_K, CONV_CH))] + [_vec(CONV_CH)] * 3,
        out_specs=[_row(ts, CONV_CH), _row(ts, CONV_CH)],
        out_shape=[jax.ShapeDtypeStruct((rows, CONV_CH), MXU_DT), jax.ShapeDtypeStruct((rows, CONV_CH), F32)],
        scratch_shapes=[pltpu.VMEM((te, CONV_CH), F32), pltpu.VMEM((7, te - 8, CONV_CH), F32)],
        dims=("parallel",))


def _conv_bwd(glu, col, u1, du3, dproj, cw, lw, lb, *, carry=None, name):
    rows = glu.shape[0]
    ts = min(rows, CONV_BWD_TS)
    nt = rows // ts
    te = ts + 2 * HALO

    def du1_of(u1v, du3v, lw_v, lb_v):
        xh, rstd = _ln_stats(u1v)
        u2 = xh * lw_v + lb_v
        sg = _sigmoid(u2)
        du2 = du3v * (sg * (1.0 + u2 * (1.0 - sg)))
        dxh = du2 * lw_v
        du1 = rstd * (dxh - jnp.mean(dxh, axis=-1, keepdims=True)
                      - xh * jnp.mean(dxh * xh, axis=-1, keepdims=True))
        return du1, du2, xh

    half = ts // 2

    def body(gp_ref, g_ref, gn_ref, up_ref, u_ref, un_ref, dp_ref, d_ref, dn_ref, cw_ref, lw_ref, lb_ref,
             _, dglu_ref, dcw_ref, dvec_ref, u0_ref, du1_ref, pu_ref, pd_ref, du0_ref):
        i = pl.program_id(0)
        lw_v, lb_v = lw_ref[...], lb_ref[...]

        @pl.when(i == 0)
        def _():
            dcw_ref[...] = jnp.zeros_like(dcw_ref)
            dvec_ref[...] = jnp.zeros_like(dvec_ref)

        gv = g_ref[...]
        u0_ref[0:HALO, :] = jnp.where(i > 0, _glu(gp_ref[...]), 0.0)
        u0_ref[HALO:HALO + ts, :] = _glu(gv)
        u0_ref[HALO + ts:, :] = jnp.where(i < nt - 1, _glu(gn_ref[...]), 0.0)
        d_prev, _, _ = du1_of(up_ref[...], dp_ref[...], lw_v, lb_v)
        d_main, du2, xh = du1_of(u_ref[...], d_ref[...], lw_v, lb_v)
        d_next, _, _ = du1_of(un_ref[...], dn_ref[...], lw_v, lb_v)
        du1_ref[0:HALO, :] = jnp.where(i > 0, d_prev, 0.0)
        du1_ref[HALO:HALO + ts, :] = d_main
        du1_ref[HALO + ts:, :] = jnp.where(i < nt - 1, d_next, 0.0)

        rid = lax.broadcasted_iota(jnp.int32, (8, CONV_CH), 0)
        dvec_ref[...] += (jnp.where(rid == 0, _colsum(d_main), 0.0)
                          + jnp.where(rid == 1, _colsum(du2 * xh), 0.0)
                          + jnp.where(rid == 2, _colsum(du2), 0.0))
        _phases(u0_ref, pu_ref)
        _phases(du1_ref, pd_ref)
        for c in range(CONV_CH // LANES):
            cs = slice(c * LANES, (c + 1) * LANES)
            for r0 in (0, half):
                dm = du1_ref[HALO + r0:HALO + r0 + half, cs]
                acc = jnp.zeros((half, LANES), F32)
                for j in range(CONV_K):
                    acc = acc + cw_ref[j:j + 1, cs] * _window(du1_ref, pd_ref, r0 + HALO + CONV_PAD - j, half, cs)
                    prod = dm * _window(u0_ref, pu_ref, r0 + HALO - CONV_PAD + j, half, cs)
                    dcw_ref[j, :, cs] += jnp.sum(prod.reshape(half // 8, 8, LANES), axis=0)
                du0_ref[r0:r0 + half, cs] = acc
        du0 = du0_ref[...]
        ga, sg = gv[:, :CONV_CH], _sigmoid(gv[:, CONV_CH:])
        dglu_ref[:, :CONV_CH] = (du0 * sg).astype(dglu_ref.dtype)
        dglu_ref[:, CONV_CH:] = (du0 * ga * sg * (1.0 - sg)).astype(dglu_ref.dtype)

    full = lambda shape: pl.BlockSpec(shape, lambda i: (0,) * len(shape))
    return _grid_call(
        body, [glu, glu, glu, u1, u1, u1, du3, du3, du3, cw, lw, lb, dproj], carry, name=name, grid=(nt,),
        in_specs=(_halo_specs(ts, GLU_COLS, rows, col) + _halo_specs(ts, CONV_CH, rows)
                  + _halo_specs(ts, CONV_CH, rows) + [full((CONV_K, CONV_CH)), _vec(CONV_CH), _vec(CONV_CH)]
                  + [pl.BlockSpec(memory_space=pl.ANY)]),
        out_specs=[_row(ts, GLU_COLS, col), full((CONV_K, 8, CONV_CH)), full((8, CONV_CH))],
        out_shape=[jax.ShapeDtypeStruct(dproj.shape, dproj.dtype),
                   jax.ShapeDtypeStruct((CONV_K, 8, CONV_CH), F32), jax.ShapeDtypeStruct((8, CONV_CH), F32)],
        scratch_shapes=[pltpu.VMEM((te, CONV_CH), F32), pltpu.VMEM((te, CONV_CH), F32),
                        pltpu.VMEM((7, te - 8, CONV_CH), F32), pltpu.VMEM((7, te - 8, CONV_CH), F32),
                        pltpu.VMEM((ts, CONV_CH), F32)],
        input_output_aliases={12: 0}, dims=("arbitrary",))


FFN_CW = 1408
FFN_NJ = FFN_H // FFN_CW


def _ffn_halo_specs(ts, rows, col_of, inner_rows):
    per = ts // HALO
    last = rows // HALO - 1
    if inner_rows:
        return [pl.BlockSpec((HALO, FFN_CW), lambda j, i: (jnp.maximum(i * per - 1, 0), col_of(j))),
                pl.BlockSpec((ts, FFN_CW), lambda j, i: (i, col_of(j))),
                pl.BlockSpec((HALO, FFN_CW), lambda j, i: (jnp.minimum((i + 1) * per, last), col_of(j)))]
    return [pl.BlockSpec((HALO, FFN_CW), lambda i, j: (jnp.maximum(i * per - 1, 0), col_of(j))),
            pl.BlockSpec((ts, FFN_CW), lambda i, j: (i, col_of(j))),
            pl.BlockSpec((HALO, FFN_CW), lambda i, j: (jnp.minimum((i + 1) * per, last), col_of(j)))]


def _ffn_ext(p_ref, m_ref, n_ref, sl, i, nt):
    return jnp.concatenate([jnp.where(i > 0, p_ref[:, sl].astype(F32), 0.0), m_ref[:, sl].astype(F32),
                            jnp.where(i < nt - 1, n_ref[:, sl].astype(F32), 0.0)], axis=0)


def _prev_row(v):
    return pltpu.roll(v, 1, 0)


def _next_row(v):
    return pltpu.roll(v, v.shape[0] - 1, 0)


def _ffn_act(up0, w3, b3, *, name):
    rows = up0.shape[0]
    ts = min(rows, 256)
    nt = rows // ts
    main = slice(HALO, HALO + ts)

    def body(gp, g, gn, vp, v, vn, wg, wv, bg, bv, a_ref, go_ref, vo_ref):
        i = pl.program_id(0)
        for ch in range(FFN_CW // LANES):
            sl = slice(ch * LANES, (ch + 1) * LANES)
            xg, xv = _ffn_ext(gp, g, gn, sl, i, nt), _ffn_ext(vp, v, vn, sl, i, nt)
            wgv, wvv = wg[:, sl], wv[:, sl]
            gate = (wgv[0:1] * _prev_row(xg) + wgv[1:2] * xg + wgv[2:3] * _next_row(xg))[main] + bg[:, sl]
            val = (wvv[0:1] * _prev_row(xv) + wvv[1:2] * xv + wvv[2:3] * _next_row(xv))[main] + bv[:, sl]
            a_ref[:, sl] = (gate * _sigmoid(gate) * val).astype(a_ref.dtype)
            go_ref[:, sl] = gate.astype(go_ref.dtype)
            vo_ref[:, sl] = val.astype(vo_ref.dtype)

    gcol, vcol = (lambda j: j), (lambda j: j + FFN_NJ)
    wspec = lambda col_of: pl.BlockSpec((FFN_K, FFN_CW), lambda i, j: (0, col_of(j)))
    bspec = lambda col_of: pl.BlockSpec((1, FFN_CW), lambda i, j: (0, col_of(j)))
    ospec = pl.BlockSpec((ts, FFN_CW), lambda i, j: (i, j))
    return pl.pallas_call(
        body, name=name, grid=(nt, FFN_NJ),
        in_specs=(_ffn_halo_specs(ts, rows, gcol, False) + _ffn_halo_specs(ts, rows, vcol, False)
                  + [wspec(gcol), wspec(vcol), bspec(gcol), bspec(vcol)]),
        out_specs=[ospec] * 3, out_shape=[jax.ShapeDtypeStruct((rows, FFN_H), MXU_DT)] * 3,
        compiler_params=_cp("parallel", "parallel"),
    )(up0, up0, up0, up0, up0, up0, w3, w3, b3, b3)


def _ffn_act_bwd(up0, gate_s, val_s, da, w3, *, carry=None, name):
    rows = up0.shape[0]
    ts = min(rows, 256)
    nt = rows // ts
    main = slice(HALO, HALO + ts)

    def body(gp, g, gn, vp, v, vn, sgp, sg, sgn, svp, sv, svn, ap, a, an, wg, wv,
             dg_ref, dv_ref, sg_ref, sv_ref):
        i = pl.program_id(1)

        @pl.when(i == 0)
        def _():
            sg_ref[...] = jnp.zeros_like(sg_ref)
            sv_ref[...] = jnp.zeros_like(sv_ref)

        rid = lax.broadcasted_iota(jnp.int32, (8, LANES), 0)
        for ch in range(FFN_CW // LANES):
            sl = slice(ch * LANES, (ch + 1) * LANES)
            xg, xv, da_e = _ffn_ext(gp, g, gn, sl, i, nt), _ffn_ext(vp, v, vn, sl, i, nt), _ffn_ext(ap, a, an, sl, i, nt)
            gate, val = _ffn_ext(sgp, sg, sgn, sl, i, nt), _ffn_ext(svp, sv, svn, sl, i, nt)
            wgv, wvv = wg[:, sl], wv[:, sl]
            xg_p, xg_n, xv_p, xv_n = _prev_row(xg), _next_row(xg), _prev_row(xv), _next_row(xv)
            sgm = _sigmoid(gate)
            eg = da_e * val * (sgm * (1.0 + gate * (1.0 - sgm)))
            ev = da_e * (gate * sgm)
            for e, taps, w, d_ref, s_ref in ((eg, (xg_p, xg, xg_n), wgv, dg_ref, sg_ref),
                                             (ev, (xv_p, xv, xv_n), wvv, dv_ref, sv_ref)):
                d0 = w[0:1] * _next_row(e) + w[1:2] * e + w[2:3] * _prev_row(e)
                d_ref[:, sl] = d0[main].astype(d_ref.dtype)
                dm = e[main]
                s_ref[:, sl] += (jnp.where(rid == 0, _colsum(dm * taps[0][main]), 0.0)
                                 + jnp.where(rid == 1, _colsum(dm * taps[1][main]), 0.0)
                                 + jnp.where(rid == 2, _colsum(dm * taps[2][main]), 0.0)
                                 + jnp.where(rid == 3, _colsum(dm), 0.0))

    gcol, vcol = (lambda j: j), (lambda j: j + FFN_NJ)
    wspec = lambda col_of: pl.BlockSpec((FFN_K, FFN_CW), lambda j, i: (0, col_of(j)))
    ospec = pl.BlockSpec((ts, FFN_CW), lambda j, i: (i, j))
    sspec = pl.BlockSpec((8, FFN_CW), lambda j, i: (0, j))
    return _grid_call(
        body, [up0] * 6 + [gate_s] * 3 + [val_s] * 3 + [da] * 3 + [w3, w3], carry, name=name, grid=(FFN_NJ, nt),
        in_specs=(_ffn_halo_specs(ts, rows, gcol, True) + _ffn_halo_specs(ts, rows, vcol, True)
                  + _ffn_halo_specs(ts, rows, gcol, True) * 3 + [wspec(gcol), wspec(vcol)]),
        out_specs=[ospec, ospec, sspec, sspec],
        out_shape=[jax.ShapeDtypeStruct((rows, FFN_H), MXU_DT), jax.ShapeDtypeStruct((rows, FFN_H), MXU_DT),
                   jax.ShapeDtypeStruct((8, FFN_H), F32), jax.ShapeDtypeStruct((8, FFN_H), F32)],
        dims=("parallel", "arbitrary"))


def _adam_math(w, g, m, v):
    m = ADAM_B1 * m + (1.0 - ADAM_B1) * g
    v = ADAM_B2 * v + (1.0 - ADAM_B2) * (g * g)
    m_hat = m / (1.0 - ADAM_B1 ** ADAM_STEP)
    v_hat = v / (1.0 - ADAM_B2 ** ADAM_STEP)
    delta = -ADAM_LR * (m_hat / (jnp.sqrt(v_hat) + ADAM_EPS) + ADAM_WD * w)
    return delta, m, v


ROW_TILE_BYTES = 8 << 20


def _row_tile(rows, row_bytes):
    tiles = [rows] + [rows // k for k in range(2, rows // 16 + 1) if rows % k == 0 and (rows // k) % 16 == 0]
    return next(t for t in tiles if t * row_bytes <= ROW_TILE_BYTES)


def _adam(w, m, v, parts, *, name):
    rows, cols = w.shape
    nparts = parts.shape[0]
    tr = _row_tile(rows, cols * (7 * 4 + nparts * parts.dtype.itemsize))

    def body(w_ref, m_ref, v_ref, p_ref, g_ref, d_ref, nm_ref, nv_ref):
        g = p_ref[0].astype(F32)
        for p in range(1, nparts):
            g = g + p_ref[p].astype(F32)
        g_ref[...] = g
        d_ref[...], nm_ref[...], nv_ref[...] = _adam_math(w_ref[...], g, m_ref[...], v_ref[...])

    spec = _row(tr, cols)
    return pl.pallas_call(
        body, name=name, grid=(rows // tr,),
        in_specs=[spec, spec, spec, pl.BlockSpec((nparts, tr, cols), lambda i: (0, i, 0))],
        out_specs=[spec] * 4, out_shape=[jax.ShapeDtypeStruct((rows, cols), F32)] * 4,
        compiler_params=_cp("parallel"),
    )(w, m, v, parts)


def _adam_many(ws, ms, vs, gs, *, name):
    n = len(ws)

    def body(*refs):
        ins, outs = refs[:4 * n], refs[4 * n:]
        for k in range(n):
            delta, new_m, new_v = _adam_math(ins[k][...], ins[3 * n + k][...], ins[n + k][...], ins[2 * n + k][...])
            outs[k][...], outs[n + k][...], outs[2 * n + k][...] = delta, new_m, new_v

    vm = pl.BlockSpec(memory_space=pltpu.VMEM)
    res = pl.pallas_call(
        body, name=name, in_specs=[vm] * (4 * n), out_specs=[vm] * (3 * n),
        out_shape=[jax.ShapeDtypeStruct(w.shape, F32) for w in ws] * 3,
        compiler_params=pltpu.CompilerParams(vmem_limit_bytes=VMEM_LIMIT),
    )(*ws, *ms, *vs, *gs)
    return res[:n], res[n:2 * n], res[2 * n:]


def _sum_parts(parts, *, name):
    nparts, rows, cols = parts.shape
    tr = _row_tile(rows, cols * (4 + nparts * parts.dtype.itemsize))

    def body(p_ref, o_ref):
        g = p_ref[0].astype(F32)
        for p in range(1, nparts):
            g = g + p_ref[p].astype(F32)
        o_ref[...] = g

    return pl.pallas_call(
        body, name=name, grid=(rows // tr,),
        in_specs=[pl.BlockSpec((nparts, tr, cols), lambda i: (0, i, 0))], out_specs=_row(tr, cols),
        out_shape=jax.ShapeDtypeStruct((rows, cols), F32), compiler_params=_cp("parallel"),
    )(parts)


def _my_place():
    return lax.axis_index("x"), lax.axis_index("y"), lax.axis_index("c")


def _dev_index(p):
    return 4 * p[0] + 2 * p[1] + p[2]


def _all_gather(xs, *, name):
    return _run_comm(_gather_plan(xs), pltpu.VMEM, name)


class _Comm(NamedTuple):
    ins: list
    outs: list
    n_remote: int
    n_local: int
    start: Callable
    finish: Callable


def _join_plans(*plans):
    def split(in_refs, out_refs, send_sems, recv_sems, local_sems):
        i = o = r = l = 0
        for p in plans:
            ni, no = len(p.ins), len(p.outs)
            yield p, (in_refs[i:i + ni], out_refs[o:o + no], send_sems.at[pl.ds(r, p.n_remote)],
                      recv_sems.at[pl.ds(r, p.n_remote)], local_sems.at[pl.ds(l, p.n_local)])
            i, o, r, l = i + ni, o + no, r + p.n_remote, l + p.n_local

    def start(*refs):
        for p, part in split(*refs):
            p.start(*part)

    def finish(*refs):
        for p, part in split(*refs):
            p.finish(*part)

    return _Comm([v for p in plans for v in p.ins], [v for p in plans for v in p.outs],
                 sum(p.n_remote for p in plans), sum(p.n_local for p in plans), start, finish)


def _comm_scratch(plan):
    return [pltpu.SemaphoreType.DMA((plan.n_remote,)), pltpu.SemaphoreType.DMA((plan.n_remote,)),
            pltpu.SemaphoreType.DMA((plan.n_local,))]


def _run_comm(plan, space, name):
    n_in, n_out = len(plan.ins), len(plan.outs)

    def body(*refs):
        args = (refs[:n_in], refs[n_in:n_in + n_out], *refs[n_in + n_out:])
        plan.start(*args)
        plan.finish(*args)

    return pl.pallas_call(
        body, name=name, out_shape=plan.outs,
        in_specs=[pl.BlockSpec(memory_space=space)] * n_in, out_specs=[pl.BlockSpec(memory_space=space)] * n_out,
        scratch_shapes=_comm_scratch(plan),
        compiler_params=pltpu.CompilerParams(vmem_limit_bytes=VMEM_LIMIT),
    )(*plan.ins)


def _grid_call(body, ins, carry, *, name, grid, in_specs, out_specs, out_shape, dims, scratch_shapes=(),
               input_output_aliases=None):
    if carry is None:
        res = pl.pallas_call(
            body, name=name, grid=grid, in_specs=list(in_specs), out_specs=list(out_specs),
            out_shape=list(out_shape), scratch_shapes=list(scratch_shapes),
            input_output_aliases=input_output_aliases or {}, compiler_params=_cp(*dims))(*ins)
        return list(res), None

    def at(pos):
        conds = [pl.program_id(k) == p for k, p in enumerate(pos)]
        out = conds[0]
        for cnd in conds[1:]:
            out = jnp.logical_and(out, cnd)
        return out

    return _carried_call(body, carry, lambda: at([0] * len(grid)), lambda: at([g - 1 for g in grid]), ins,
                         name=name, grid=grid, in_specs=in_specs, out_specs=out_specs, out_shape=out_shape,
                         scratch_shapes=scratch_shapes, input_output_aliases=input_output_aliases)


def _carried_call(body, plan, first, last, ins, *, name, grid, in_specs, out_specs, out_shape, scratch_shapes=(),
                  input_output_aliases=None):
    in_specs, out_specs, out_shape = list(in_specs), list(out_specs), list(out_shape)
    n_in, n_out, n_scr = len(in_specs), len(out_specs), len(scratch_shapes)
    c_in, c_out = len(plan.ins), len(plan.outs)
    hbm = pl.BlockSpec(memory_space=pl.ANY)

    def full_body(*refs):
        ins, c_ins = refs[:n_in], refs[n_in:n_in + c_in]
        outs = refs[n_in + c_in:n_in + c_in + n_out]
        c_outs = refs[n_in + c_in + n_out:n_in + c_in + n_out + c_out]
        scr = refs[n_in + c_in + n_out + c_out:]
        sems = scr[n_scr:]

        @pl.when(first())
        def _():
            plan.start(c_ins, c_outs, *sems)

        body(*ins, *outs, *scr[:n_scr])

        @pl.when(last())
        def _():
            plan.finish(c_ins, c_outs, *sems)

    res = pl.pallas_call(
        full_body, name=name, grid=grid,
        in_specs=in_specs + [hbm] * c_in, out_specs=out_specs + [hbm] * c_out,
        out_shape=out_shape + list(plan.outs),
        scratch_shapes=list(scratch_shapes) + _comm_scratch(plan),
        input_output_aliases=input_output_aliases or {},
        compiler_params=_cp(*(["arbitrary"] * len(grid))),
    )(*ins, *plan.ins)
    return list(res[:n_out]), list(res[n_out:])


def _gather_plan(xs):
    n = len(xs)
    ms = [v.shape[0] for v in xs]

    def tools(x_refs, o_refs, send_sems, recv_sems, local_sems):
        x, y, c = _my_place()
        me, sib = (x, y, c), (x, y, 1 - c)
        chips = [(1 - x, y), (x, 1 - y), (1 - x, 1 - y)]

        def rows(a, p):
            return o_refs[a].at[pl.ds(pl.multiple_of(_dev_index(p) * ms[a], 8), ms[a])]

        def copy(a, k, block, to, src=None):
            return pltpu.make_async_remote_copy(
                src_ref=rows(a, block) if src is None else src, dst_ref=rows(a, block),
                send_sem=send_sems.at[a * 7 + k], recv_sem=recv_sems.at[a * 7 + k],
                device_id=to, device_id_type=MESH)

        mine = [pltpu.make_async_copy(x_refs[a], rows(a, me), local_sems.at[a]) for a in range(n)]
        first = []
        for a in range(n):
            first.append(copy(a, 0, me, sib, src=x_refs[a]))
            first += [copy(a, 1 + j, me, (*chip, c), src=x_refs[a]) for j, chip in enumerate(chips)]
        return me, sib, chips, c, copy, mine, first

    def start(*refs):
        _, _, _, _, _, mine, first = tools(*refs)
        for cp in mine + first:
            cp.start()

    def finish(*refs):
        me, sib, chips, c, copy, mine, first = tools(*refs)
        passed = []
        for j, chip in enumerate(chips):
            for a in range(n):
                copy(a, 1 + j, (*chip, c), me).wait_recv()
                cp = copy(a, 4 + j, (*chip, c), sib)
                cp.start()
                passed.append(cp)
        for a in range(n):
            copy(a, 0, sib, me).wait_recv()
            for j, chip in enumerate(chips):
                copy(a, 4 + j, (*chip, 1 - c), me).wait_recv()
        for cp in first + passed:
            cp.wait_send()
        for cp in mine:
            cp.wait()

    outs = [jax.ShapeDtypeStruct((N_DEV * v.shape[0], v.shape[1]), v.dtype) for v in xs]
    return _Comm(list(xs), outs, 7 * n, n, start, finish)


def _exchange_plan(gs, cols=None):
    n = len(gs)
    rs = [v.shape[0] // N_DEV for v in gs]
    flips = [(bx, by, bc) for bx in (0, 1) for by in (0, 1) for bc in (0, 1)][1:]

    def tools(g_refs, o_refs, send_sems, recv_sems, local_sems):
        x, y, c = _my_place()
        me = (x, y, c)

        def block(ref, a, p):
            return ref.at[pl.ds(_dev_index(p) * rs[a], rs[a])]

        def src(a, p):
            rows = pl.ds(_dev_index(p) * rs[a], rs[a])
            return g_refs[a].at[rows] if cols is None else g_refs[a].at[rows, pl.ds(cols[0], cols[1])]

        def peer(f):
            return (1 - x if f[0] else x, 1 - y if f[1] else y, 1 - c if f[2] else c)

        def copy(a, k, to):
            return pltpu.make_async_remote_copy(
                src_ref=src(a, to), dst_ref=block(o_refs[a], a, me),
                send_sem=send_sems.at[a * 7 + k], recv_sem=recv_sems.at[a * 7 + k],
                device_id=to, device_id_type=MESH)

        def arrival(a, k, frm):
            return pltpu.make_async_remote_copy(
                src_ref=src(a, frm), dst_ref=block(o_refs[a], a, frm),
                send_sem=send_sems.at[a * 7 + k], recv_sem=recv_sems.at[a * 7 + k],
                device_id=frm, device_id_type=MESH)

        mine = [pltpu.make_async_copy(src(a, me), block(o_refs[a], a, me), local_sems.at[a])
                for a in range(n)]
        sends = [copy(a, k, peer(f)) for a in range(n) for k, f in enumerate(flips)]
        arrivals = [arrival(a, k, peer(f)) for a in range(n) for k, f in enumerate(flips)]
        return mine, sends, arrivals

    def start(*refs):
        mine, sends, _ = tools(*refs)
        for cp in mine + sends:
            cp.start()

    def finish(*refs):
        mine, sends, arrivals = tools(*refs)
        for cp in arrivals:
            cp.wait_recv()
        for cp in sends:
            cp.wait_send()
        for cp in mine:
            cp.wait()

    outs = [jax.ShapeDtypeStruct((v.shape[0], v.shape[1] if cols is None else cols[1]), v.dtype) for v in gs]
    return _Comm(list(gs), outs, 7 * n, n, start, finish)


def _rope_tables(seq):
    t = jnp.arange(seq)
    row, col = t // GRID_W, t % GRID_W
    half = HEAD_DIM // 2
    inv = ROPE_BASE ** (-jnp.arange(0, half, 2, dtype=F32) / half)
    ang_r = row.astype(F32)[:, None] * inv
    ang_c = col.astype(F32)[:, None] * inv
    ang = jnp.concatenate([ang_r, ang_r, ang_c, ang_c], axis=-1)
    return jnp.tile(jnp.cos(ang), (1, 2)), jnp.tile(jnp.sin(ang), (1, 2))


def _to_slots(w):
    return w.reshape(N_KV_HEADS, GQA_GROUP, HEAD_DIM, w.shape[1]).transpose(1, 0, 2, 3).reshape(w.shape)


def _from_slots(w):
    return w.reshape(GQA_GROUP, N_KV_HEADS, HEAD_DIM, w.shape[1]).transpose(1, 0, 2, 3).reshape(w.shape)


def _pack(vs):
    flat = jnp.concatenate([v.reshape(-1).astype(F32) for v in vs])
    total = -(-flat.shape[0] // (8 * LANES)) * (8 * LANES)
    return jnp.pad(flat, (0, total - flat.shape[0])).reshape(-1, LANES)


def _unpack(packed, like):
    flat, out, off = packed.reshape(-1), [], 0
    for v in like:
        size = math.prod(v.shape)
        out.append(flat[off:off + size].reshape(v.shape))
        off += size
    return out


def _silu(v):
    return v * jax.nn.sigmoid(v)


def kernel(x, c, ctx, c_ctx, w_mod, b_mod, norm_mix_w, w_in, q_norm_w, k_norm_w, sink_logit, conv_w, conv_b, conv_norm_w, conv_norm_b, w_out, norm_ffn_w, w_up, ffn_conv_w, ffn_conv_b, w_down, loss_target, m_c_ctx, m_w_mod, m_b_mod, m_norm_mix_w, m_w_in, m_q_norm_w, m_k_norm_w, m_sink_logit, m_conv_w, m_conv_b, m_conv_norm_w, m_conv_norm_b, m_w_out, m_norm_ffn_w, m_w_up, m_ffn_conv_w, m_ffn_conv_b, m_w_down, v_c_ctx, v_w_mod, v_b_mod, v_norm_mix_w, v_w_in, v_q_norm_w, v_k_norm_w, v_sink_logit, v_conv_w, v_conv_b, v_conv_norm_w, v_conv_norm_b, v_w_out, v_norm_ffn_w, v_w_up, v_ffn_conv_w, v_ffn_conv_b, v_w_down):
    d = D_MODEL
    seq, n_ctx = x.shape[1], ctx.shape[1]
    me = _dev_index(_my_place())
    xs, ctxs, tgt = x[0], ctx[0], loss_target[0]

    small = _pack([c[0], conv_w[0], ffn_conv_w[0]])
    small_all = _all_gather([small], name="gather_small")[0].reshape(N_DEV, -1)
    n_cw, n_fw = conv_w[0].size, ffn_conv_w[0].size
    c_all = small_all[:, :d]
    cw_all = small_all[:, d:d + n_cw].reshape(N_DEV, CONV_K, -1)
    fw_all = small_all[:, d + n_cw:d + n_cw + n_fw].reshape(N_DEV, FFN_K, -1)
    conv_w_f = cw_all.transpose(1, 0, 2).reshape(CONV_K, CONV_CH)
    ffn_w_f = fw_all.transpose(1, 0, 2).reshape(FFN_K, 2 * FFN_H)

    mcols = w_mod.shape[2]
    act = jnp.zeros((16, d), F32).at[:N_DEV].set(_silu(c_all)).at[N_DEV].set(_silu(c_ctx))
    mod_part = _mm(act, w_mod[0], 16, mcols, d, name="mod_fwd")
    mod_all = _all_gather([mod_part], name="gather_mod")[0]
    mod_all = mod_all.reshape(N_DEV, 16, mcols).transpose(1, 0, 2).reshape(16, 6 * d) + b_mod
    mod = lax.dynamic_slice_in_dim(mod_all, me, 1, axis=0)
    sh1, sc1, g1, sh2, sc2, g2 = [mod[:, k * d:(k + 1) * d] for k in range(6)]
    sh1c, sc1c = mod_all[N_DEV:N_DEV + 1, :d], mod_all[N_DEV:N_DEV + 1, d:2 * d]

    cos, sin = _rope_tables(seq)
    ones_c, zeros_c = jnp.ones((n_ctx, LANES), F32), jnp.zeros((n_ctx, LANES), F32)
    qk_w = jnp.concatenate([jnp.tile(q_norm_w, (1, N_Q_HEADS)), jnp.tile(k_norm_w, (1, N_KV_HEADS))], axis=1)
    kc_w = jnp.tile(k_norm_w, (1, N_KV_HEADS))

    h, (w_in_t,) = _prenorm(xs, norm_mix_w, sc1, sh1, carry=_gather_plan([w_in[0].T.astype(MXU_DT)]),
                            name="prenorm_mix")
    hc, _ = _prenorm(ctxs, norm_mix_w, sc1c, sh1c, name="prenorm_ctx")
    w_in_p = jnp.concatenate([_to_slots(w_in_t[:Q_COLS]), w_in_t[Q_COLS:QKV_COLS],
                              jnp.zeros((GLU_OFF - QKV_COLS, d), MXU_DT), w_in_t[QKV_COLS:]])
    proj = _mm(h, w_in_p, seq, IN_PAD, d, tb=True, name="proj_in")
    kv_ctx = _mm(hc, w_in_p, n_ctx, 2 * KV_COLS, d, tb=True, n0=Q_COLS, name="proj_ctx")
    qk_r = _qk_prep(proj, QK_COLS, cos, sin, qk_w, name="qk_prep")
    kc_n = _qk_prep(kv_ctx, KV_COLS, ones_c, zeros_c, kc_w, name="k_ctx_prep")
    (attn_o, lse), (w_up_t, w_out_f) = _attn_fwd(
        sink_logit, qk_r, proj, kc_n, kv_ctx, name="attn_fwd",
        carry=_gather_plan([w_up[0].T.astype(MXU_DT), w_out[0].astype(MXU_DT)]))
    w_out_f = jnp.concatenate([_to_slots(w_out_f[:Q_COLS]), w_out_f[Q_COLS:]])
    (u3, u1), (w_down_f,) = _conv_fwd(proj, GLU_OFF // GLU_COLS, conv_w_f, conv_b, conv_norm_w, conv_norm_b,
                                      carry=_gather_plan([w_down[0].astype(MXU_DT)]), name="conv_fwd")
    mix, x1, h2 = _out_proj_prenorm(attn_o, u3, w_out_f, xs, g1, norm_ffn_w, sc2, sh2, name="out_proj")

    up0 = _mm(h2, w_up_t, seq, 2 * FFN_H, d, tb=True, out_dtype=MXU_DT, name="ffn_up")
    act_a, gate_s, val_s = _ffn_act(up0, ffn_w_f, ffn_conv_b, name="ffn_act")
    loss_p, dy, dffn, dg2 = _ffn_down_loss(act_a, w_down_f, x1, g2, tgt, name="ffn_down_loss")

    da = _mm(dffn, w_down_f, seq, FFN_H, d, tb=True, out_dtype=MXU_DT, name="ffn_down_dx")
    gw_down = _mm(act_a, dffn, FFN_H, d, seq, ta=True, out_dtype=MXU_DT, name="ffn_down_dw")
    (dgate0, dval0, s_gate, s_val), (rx_down,) = _ffn_act_bwd(
        up0, gate_s, val_s, da, ffn_w_f, carry=_exchange_plan([gw_down]), name="ffn_act_bwd")
    gw_up_t = _mm(dgate0, h2, FFN_H, d, seq, ta=True, out_dtype=MXU_DT, out_rows=2 * FFN_H, name="ffn_up_dw_gate")
    gw_up_t = _mm(dval0, h2, FFN_H, d, seq, ta=True, into=gw_up_t, m0=FFN_H, name="ffn_up_dw_val")
    qd = d // 4
    dh2, (rx_up_a,) = _mm(dgate0, w_up_t, seq, d, FFN_H, a2=dval0, out_dtype=MXU_DT, name="ffn_up_dx",
                          carry=_exchange_plan([gw_up_t], cols=(0, qd)))
    (dx1, dmix, dsh2, dsc2, dnw2, dg1), _ = _norm_bwd(
        dh2, x1, norm_ffn_w, sc2, res=dy, gate=(mix, g1), name="prenorm_ffn_bwd")

    dattn = _mm(dmix, w_out_f, seq, Q_COLS, d, tb=True, out_dtype=MXU_DT, name="out_dx_attn")
    du3 = _mm(dmix, w_out_f, seq, CONV_CH, d, tb=True, n0=Q_COLS, name="out_dx_conv")
    gw_out = _mm(attn_o, dmix, Q_COLS, d, seq, ta=True, out_dtype=MXU_DT, out_rows=Q_COLS + CONV_CH,
                 name="out_dw_attn")
    gw_out = _mm(u3, dmix, CONV_CH, d, seq, ta=True, into=gw_out, m0=Q_COLS, name="out_dw_conv")
    gw_out = jnp.concatenate([_from_slots(gw_out[:Q_COLS]), gw_out[Q_COLS:]])
    (dq, dk, dv, dkc_r, dvc, dsink_rows), (rx_out, rx_up_b) = _attn_bwd(
        sink_logit, qk_r, proj, kc_n, kv_ctx, attn_o, lse, dattn, name="attn_bwd",
        carry=_join_plans(_exchange_plan([gw_out]), _exchange_plan([gw_up_t], cols=(qd, qd))))
    dproj, dqk_w = _qk_prep_bwd(proj, [dq, dk], QK_COLS, cos, sin, qk_w, tail=dv, name="qk_prep_bwd")
    (dproj, dcw8, dvec), (rx_up_c,) = _conv_bwd(
        proj, GLU_OFF // GLU_COLS, u1, du3, dproj, conv_w_f, conv_norm_w, conv_norm_b,
        carry=_exchange_plan([gw_up_t], cols=(2 * qd, 2 * qd)), name="conv_bwd")
    dkc, dkc_w = _qk_prep_bwd(kv_ctx, [dkc_r], KV_COLS, ones_c, zeros_c, kc_w, name="k_ctx_prep_bwd")
    dkv_ctx = jnp.concatenate([dkc, dvc.astype(MXU_DT)], axis=1)
    gw_p = _mm(dproj, h, IN_PAD, d, seq, ta=True, name="proj_dw")
    gw_ctx = _mm(dkv_ctx, hc, 2 * KV_COLS, d, n_ctx, ta=True, name="proj_dw_ctx")
    gw_in_t = jnp.concatenate([_from_slots(gw_p[:Q_COLS]), gw_p[Q_COLS:QKV_COLS] + gw_ctx, gw_p[GLU_OFF:]],
                              axis=0).astype(MXU_DT)
    dh, (rx_in,) = _mm(dproj, w_in_p, seq, d, IN_PAD, out_dtype=MXU_DT, carry=_exchange_plan([gw_in_t]),
                       name="proj_dx")
    dhc = _mm(dkv_ctx, w_in_p, n_ctx, d, 2 * KV_COLS, k0=Q_COLS, name="proj_dx_ctx")
    (grad_x, dsh1, dsc1, dnw1), _ = _norm_bwd(dh, xs, norm_mix_w, sc1, res=dx1, name="prenorm_mix_bwd")
    (dsh1c, dsc1c, dnw1c), _ = _norm_bwd(dhc, ctxs, norm_mix_w, sc1c, want_dx=False, name="prenorm_ctx_bwd")

    dmod = jnp.concatenate([dsh1, dsc1, dg1, dsh2, dsc2, dg2], axis=1)
    dmod_ctx = jnp.concatenate([dsh1c, dsc1c], axis=1)
    d_qn = dqk_w[0, :Q_COLS].reshape(N_Q_HEADS, HEAD_DIM).sum(0)
    d_kn = (dqk_w[0, Q_COLS:].reshape(N_KV_HEADS, HEAD_DIM).sum(0)
            + dkc_w[0].reshape(N_KV_HEADS, HEAD_DIM).sum(0))
    d_ffn_w = jnp.concatenate([s_gate[:FFN_K], s_val[:FFN_K]], axis=1)
    d_ffn_b = jnp.concatenate([s_gate[FFN_K], s_val[FFN_K]])
    d_sink = dsink_rows.reshape(N_Q_HEADS, BLOCK).sum(1)
    summed_like = [(dnw1 + dnw1c), d_qn[None], d_kn[None], d_sink[None], dvec[0:1], dvec[1:2],
                   dvec[2:3], dnw2, d_ffn_b[None], dcw8.sum(1), d_ffn_w, loss_p[0:1, 0:1]]
    pack = _pack([dmod, dmod_ctx] + summed_like)
    pack_all = _all_gather([pack], name="gather_small_grads")[0]
    pack_all = pack_all.reshape(N_DEV, pack.shape[0], LANES)
    tot = _sum_parts(pack_all, name="sum_small_grads")
    (dmod_sum, dmc_sum, g_nmix, g_qn, g_kn, g_sink, g_cb, g_lw, g_lb, g_nffn, g_fb, g_cw_f, g_fw_f,
     loss_sum) = _unpack(tot, [dmod, dmod_ctx] + summed_like)
    loss = loss_sum[0, 0]
    dmod_all = pack_all.reshape(N_DEV, -1)[:, :6 * d]
    g_b_mod = dmod_sum.at[:, :2 * d].add(dmc_sum)

    lo = me * mcols
    dm_rows = jnp.zeros((16, 6 * d), F32).at[:N_DEV].set(dmod_all).at[N_DEV, :2 * d].set(dmc_sum[0])
    dm_mine = lax.dynamic_slice_in_dim(dm_rows, lo, mcols, axis=1)
    parts_mod = _mm(act, dm_mine, d, mcols, 16, ta=True, name="mod_dw")[None]
    dact_part = _mm(dm_mine[N_DEV:N_DEV + 8], w_mod[0], 8, d, mcols, tb=True, name="mod_dx_ctx")
    dact_all = _all_gather([dact_part], name="gather_c_ctx_grad")[0].reshape(N_DEV, 8, d)
    dact = _sum_parts(dact_all, name="sum_c_ctx_grad")[0]
    sg = jax.nn.sigmoid(c_ctx)
    g_c_ctx = dact * (sg * (1.0 + c_ctx * (1.0 - sg)))

    def stacked(rx):
        return rx.reshape(N_DEV, rx.shape[0] // N_DEV, rx.shape[1])

    g_w_in = _sum_parts(stacked(rx_in), name="sum_w_in").T[None]
    g_w_up = jnp.concatenate([_sum_parts(stacked(rx_up_a), name="sum_w_up_a"),
                              _sum_parts(stacked(rx_up_b), name="sum_w_up_b"),
                              _sum_parts(stacked(rx_up_c), name="sum_w_up_c")], axis=1).T[None]
    big = {}
    big["w_in"] = _adam(w_in[0], m_w_in[0], v_w_in[0], g_w_in, name="adam_w_in")
    big["w_up"] = _adam(w_up[0], m_w_up[0], v_w_up[0], g_w_up, name="adam_w_up")
    big["w_out"] = _adam(w_out[0], m_w_out[0], v_w_out[0], stacked(rx_out), name="adam_w_out")
    big["w_down"] = _adam(w_down[0], m_w_down[0], v_w_down[0], stacked(rx_down), name="adam_w_down")
    big["w_mod"] = _adam(w_mod[0], m_w_mod[0], v_w_mod[0], parts_mod, name="adam_w_mod")

    ccols, fcols = conv_w.shape[2], ffn_conv_w.shape[2]
    g_conv_w = lax.dynamic_slice_in_dim(g_cw_f, me * ccols, ccols, axis=1)[None]
    g_ffn_w = lax.dynamic_slice_in_dim(g_fw_f, me * fcols, fcols, axis=1)[None]
    names = ["c_ctx", "b_mod", "norm_mix_w", "q_norm_w", "k_norm_w", "sink_logit", "conv_w", "conv_b",
             "conv_norm_w", "conv_norm_b", "norm_ffn_w", "ffn_conv_w", "ffn_conv_b"]
    ws = [c_ctx, b_mod, norm_mix_w, q_norm_w, k_norm_w, sink_logit, conv_w, conv_b, conv_norm_w, conv_norm_b,
          norm_ffn_w, ffn_conv_w, ffn_conv_b]
    msm = [m_c_ctx, m_b_mod, m_norm_mix_w, m_q_norm_w, m_k_norm_w, m_sink_logit, m_conv_w, m_conv_b,
           m_conv_norm_w, m_conv_norm_b, m_norm_ffn_w, m_ffn_conv_w, m_ffn_conv_b]
    vsm = [v_c_ctx, v_b_mod, v_norm_mix_w, v_q_norm_w, v_k_norm_w, v_sink_logit, v_conv_w, v_conv_b,
           v_conv_norm_w, v_conv_norm_b, v_norm_ffn_w, v_ffn_conv_w, v_ffn_conv_b]
    gsm = [g_c_ctx, g_b_mod, g_nmix, g_qn, g_kn, g_sink, g_conv_w, g_cb, g_lw, g_lb, g_nffn, g_ffn_w, g_fb]
    deltas, new_ms, new_vs = _adam_many(ws, msm, vsm, gsm, name="adam_small")
    sm = {nm: vals for nm, vals in zip(names, zip(gsm, deltas, new_ms, new_vs))}

    def out4(nm):
        if nm in sm:
            return sm[nm]
        return tuple(t[None] for t in big[nm])

    order = ["c_ctx", "w_mod", "b_mod", "norm_mix_w", "w_in", "q_norm_w", "k_norm_w", "sink_logit", "conv_w",
             "conv_b", "conv_norm_w", "conv_norm_b", "w_out", "norm_ffn_w", "w_up", "ffn_conv_w", "ffn_conv_b",
             "w_down"]
    quads = [out4(nm) for nm in order]
    return (loss, grad_x[None], *[q[0] for q in quads], *[q[1] for q in quads],
            *[q[2] for q in quads], *[q[3] for q in quads])
```

```python
import math
from typing import Callable, NamedTuple

import jax
import jax.numpy as jnp
from jax import lax
from jax.experimental import pallas as pl
from jax.experimental.pallas import tpu as pltpu

F32 = jnp.float32
MXU_DT = jnp.bfloat16

D_MODEL = 1024
GRID_W = 64
HEAD_DIM = 64
N_Q_HEADS = 8
N_KV_HEADS = 2
GQA_GROUP = 4
WINDOW = 128
BLOCK = 128
Q_COLS = 512
KV_COLS = 128
QK_COLS = Q_COLS + KV_COLS
QKV_COLS = Q_COLS + 2 * KV_COLS
CONV_CH = 512
GLU_COLS = 2 * CONV_CH
IN_COLS = QKV_COLS + GLU_COLS
CONV_K = 31
CONV_PAD = 15
FFN_H = 2816
FFN_K = 3
ROPE_BASE = 10000.0
EPS = 1e-6
NEG_INF = -1e30
N_DEV = 8
HALO = 16
LANES = 128
ROW_TS = 512

ADAM_LR = 0.001
ADAM_B1 = 0.9
ADAM_B2 = 0.999
ADAM_EPS = 1e-08
ADAM_WD = 0.01
ADAM_STEP = 10

MESH = pl.DeviceIdType.MESH
VMEM_LIMIT = 56 << 20
MM_VMEM_BUDGET = 44 << 20
GLU_OFF = 1024
IN_PAD = GLU_OFF + GLU_COLS


def _cp(*dims):
    return pltpu.CompilerParams(dimension_semantics=dims or None, vmem_limit_bytes=VMEM_LIMIT)


def _row(ts, w, col=0):
    return pl.BlockSpec((ts, w), lambda i: (i, col))


def _vec(w):
    return pl.BlockSpec((1, w), lambda i: (0, 0))


def _colsum(v):
    return jnp.sum(v, axis=0, keepdims=True)


def _sigmoid(v):
    return 0.5 * jnp.tanh(0.5 * v) + 0.5


def _mm(a, b, m, n, k, *, ta=False, tb=False, n0=0, k0=0, add=None, out_dtype=F32, into=None, m0=0,
        out_rows=None, a2=None, carry=None, name):
    has_add, has_a2 = add is not None, a2 is not None
    assert not (has_a2 and (ta or tb))
    if into is not None:
        out_dtype = into.dtype
    sa, sb, so = a.dtype.itemsize, b.dtype.itemsize, jnp.dtype(out_dtype).itemsize
    sadd = add.dtype.itemsize if has_add else 0
    na = 2 if has_a2 else 1

    def fits(tm, tn):
        return 2 * (na * k * (tm * sa + tn * sb) + tm * tn * (so + sadd)) <= MM_VMEM_BUDGET

    tms = [m] if m <= 1024 else [t for t in (1024, 1408, 768, 512, 256, 128) if m % t == 0]
    tns = [t for t in ((1024, 512, 256, 128) if ta else (1408, 512, 256, 128)) if n % t == 0 and n0 % t == 0]
    tm, tn = next((tm, tn) for tm in tms for tn in tns if fits(tm, tn))
    a_spec = (pl.BlockSpec((k, tm), lambda i, j: (0, i)) if ta else pl.BlockSpec((tm, k), lambda i, j: (i, 0)))
    nb0 = n0 // tn
    if tb:
        assert k0 == 0
        b_spec = pl.BlockSpec((tn, k), lambda i, j: (j + nb0, 0))
    else:
        assert k0 % (na * k) == 0, (k0, k)
        kb0 = k0 // (na * k)
        b_spec = pl.BlockSpec((na * k, tn), lambda i, j: (kb0, j + nb0))
    assert m0 % tm == 0, (m0, tm)
    mb0 = m0 // tm
    o_spec = pl.BlockSpec((tm, tn), lambda i, j: (i + mb0, j))
    dims = (((0 if ta else 1,), (1 if tb else 0,)), ((), ()))

    def body(*refs):
        a_ref, b_ref, o_ref = refs[0], refs[na], refs[-1]
        if has_a2:
            res = (jnp.dot(a_ref[...].astype(MXU_DT), b_ref[0:k, :].astype(MXU_DT), preferred_element_type=F32)
                   + jnp.dot(refs[1][...].astype(MXU_DT), b_ref[k:2 * k, :].astype(MXU_DT),
                             preferred_element_type=F32))
        else:
            res = lax.dot_general(a_ref[...].astype(MXU_DT), b_ref[...].astype(MXU_DT), dims,
                                  preferred_element_type=F32)
        if has_add:
            res = res + refs[na + 1][...].astype(F32)
        o_ref[...] = res.astype(o_ref.dtype)

    ins = [a] + ([a2] if has_a2 else []) + [b] + ([add] if has_add else []) + ([into] if into is not None else [])
    specs = ([a_spec] * na + [b_spec] + ([pl.BlockSpec((tm, tn), lambda i, j: (i, j))] if has_add else [])
             + ([pl.BlockSpec(memory_space=pl.ANY)] if into is not None else []))
    out_shape = (jax.ShapeDtypeStruct(into.shape, into.dtype) if into is not None
                 else jax.ShapeDtypeStruct((out_rows or m, n), out_dtype))
    (out,), carried = _grid_call(
        body, ins, carry, name=name, grid=(m // tm, n // tn), in_specs=specs, out_specs=[o_spec],
        out_shape=[out_shape], input_output_aliases={len(ins) - 1: 0} if into is not None else None,
        dims=("parallel", "parallel"))
    return out if carry is None else (out, carried)


def _rms_stats(xv):
    r = lax.rsqrt(jnp.mean(xv * xv, axis=-1, keepdims=True) + EPS)
    return r, xv * r


def _prenorm(x, nw, sc, sh, *, carry=None, name):
    rows, d = x.shape
    ts = min(rows, ROW_TS)

    def body(x_ref, nw_ref, sc_ref, sh_ref, h_ref):
        _, xn = _rms_stats(x_ref[...])
        h_ref[...] = ((xn * nw_ref[...]) * (1.0 + sc_ref[...]) + sh_ref[...]).astype(h_ref.dtype)

    (h,), carried = _grid_call(
        body, [x, nw, sc, sh], carry, name=name, grid=(rows // ts,),
        in_specs=[_row(ts, d), _vec(d), _vec(d), _vec(d)], out_specs=[_row(ts, d)],
        out_shape=[jax.ShapeDtypeStruct((rows, d), MXU_DT)], dims=("parallel",))
    return h, carried


def _out_proj_prenorm(attn_o, u3, w_out, x, g1, nw, sc, sh, *, name):
    rows, d = x.shape
    ka, ku = attn_o.shape[1], u3.shape[1]
    tm = min(rows, ROW_TS)

    def body(a_ref, u_ref, w_ref, x_ref, g_ref, nw_ref, sc_ref, sh_ref, mix_ref, x1_ref, h_ref):
        mix = (jnp.dot(a_ref[...].astype(MXU_DT), w_ref[0:ka, :].astype(MXU_DT), preferred_element_type=F32)
               + jnp.dot(u_ref[...].astype(MXU_DT), w_ref[ka:ka + ku, :].astype(MXU_DT),
                         preferred_element_type=F32))
        mix_ref[...] = mix
        x1 = x_ref[...] + g_ref[...] * mix
        x1_ref[...] = x1
        _, xn = _rms_stats(x1)
        h_ref[...] = ((xn * nw_ref[...]) * (1.0 + sc_ref[...]) + sh_ref[...]).astype(h_ref.dtype)

    return pl.pallas_call(
        body, name=name, grid=(rows // tm,),
        in_specs=[_row(tm, ka), _row(tm, ku), pl.BlockSpec((ka + ku, d), lambda i: (0, 0)), _row(tm, d),
                  _vec(d), _vec(d), _vec(d), _vec(d)],
        out_specs=[_row(tm, d), _row(tm, d), _row(tm, d)],
        out_shape=[jax.ShapeDtypeStruct((rows, d), F32), jax.ShapeDtypeStruct((rows, d), F32),
                   jax.ShapeDtypeStruct((rows, d), MXU_DT)],
        compiler_params=_cp("parallel"),
    )(attn_o, u3, w_out, x, g1, nw, sc, sh)


def _ffn_down_loss(act, w_down, x1, g2, target, *, name):
    rows, d = x1.shape
    k = act.shape[1]
    tm, tn = min(rows, 1024), 512

    def body(a_ref, b_ref, x1_ref, g_ref, t_ref, loss_ref, dy_ref, dffn_ref, dg_ref):
        j, i = pl.program_id(0), pl.program_id(1)
        f = jnp.dot(a_ref[...].astype(MXU_DT), b_ref[...].astype(MXU_DT), preferred_element_type=F32)
        e = x1_ref[...] + g_ref[...] * f - t_ref[...]
        part = (0.5 / d) * jnp.sum(jnp.sum(e * e, axis=-1, keepdims=True), axis=0, keepdims=True)
        dy = e * (1.0 / d)
        dy_ref[...] = dy
        dffn_ref[...] = (dy * g_ref[...]).astype(dffn_ref.dtype)

        @pl.when((i == 0) & (j == 0))
        def _():
            loss_ref[...] = jnp.zeros_like(loss_ref)

        @pl.when(i == 0)
        def _():
            dg_ref[...] = jnp.zeros_like(dg_ref)

        loss_ref[...] += jnp.broadcast_to(part, loss_ref.shape)
        dg_ref[...] += _colsum(dy * f)

    tile = pl.BlockSpec((tm, tn), lambda j, i: (i, j))
    vec = pl.BlockSpec((1, tn), lambda j, i: (0, j))
    return pl.pallas_call(
        body, name=name, grid=(d // tn, rows // tm),
        in_specs=[pl.BlockSpec((tm, k), lambda j, i: (i, 0)), pl.BlockSpec((k, tn), lambda j, i: (0, j)),
                  tile, vec, tile],
        out_specs=[pl.BlockSpec((8, LANES), lambda j, i: (0, 0)), tile, tile, vec],
        out_shape=[jax.ShapeDtypeStruct((8, LANES), F32), jax.ShapeDtypeStruct((rows, d), F32),
                   jax.ShapeDtypeStruct((rows, d), MXU_DT), jax.ShapeDtypeStruct((1, d), F32)],
        compiler_params=_cp("arbitrary", "arbitrary"),
    )(act, w_down, x1, g2, target)


def _norm_bwd(dh, xin, nw, sc, *, res=None, gate=None, want_dx=True, carry=None, name):
    rows, d = xin.shape
    ts = min(rows, ROW_TS)
    has_res, has_gate = res is not None, gate is not None

    def body(*refs):
        it = iter(refs)
        dh_ref, x_ref, nw_ref, sc_ref = next(it), next(it), next(it), next(it)
        res_ref = next(it) if has_res else None
        gated_ref, g_ref = (next(it), next(it)) if has_gate else (None, None)
        dx_ref = next(it) if want_dx else None
        dgx_ref = next(it) if has_gate else None
        dsh_ref, dsc_ref, dnw_ref = next(it), next(it), next(it)
        dg_ref = next(it) if has_gate else None
        i = pl.program_id(0)
        dhv = dh_ref[...].astype(F32)
        r, xn = _rms_stats(x_ref[...])
        dn = dhv * (1.0 + sc_ref[...])

        @pl.when(i == 0)
        def _():
            dsh_ref[...] = jnp.zeros_like(dsh_ref)
            dsc_ref[...] = jnp.zeros_like(dsc_ref)
            dnw_ref[...] = jnp.zeros_like(dnw_ref)
            if has_gate:
                dg_ref[...] = jnp.zeros_like(dg_ref)

        dsh_ref[...] += _colsum(dhv)
        dsc_ref[...] += _colsum(dhv * (xn * nw_ref[...]))
        dnw_ref[...] += _colsum(dn * xn)
        if want_dx:
            dxn = dn * nw_ref[...]
            dx = r * (dxn - xn * jnp.mean(dxn * xn, axis=-1, keepdims=True))
            if has_res:
                dx = dx + res_ref[...]
            dx_ref[...] = dx
            if has_gate:
                dgx_ref[...] = (dx * g_ref[...]).astype(dgx_ref.dtype)
                dg_ref[...] += _colsum(dx * gated_ref[...])

    ins = [dh, xin, nw, sc] + ([res] if has_res else []) + (list(gate) if has_gate else [])
    in_specs = ([_row(ts, d), _row(ts, d), _vec(d), _vec(d)] + ([_row(ts, d)] if has_res else [])
                + ([_row(ts, d), _vec(d)] if has_gate else []))
    out_specs, out_shape = [], []
    if want_dx:
        out_specs.append(_row(ts, d)); out_shape.append(jax.ShapeDtypeStruct((rows, d), F32))
    if has_gate:
        out_specs.append(_row(ts, d)); out_shape.append(jax.ShapeDtypeStruct((rows, d), MXU_DT))
    for _ in range(3 + int(has_gate)):
        out_specs.append(_vec(d)); out_shape.append(jax.ShapeDtypeStruct((1, d), F32))
    return _grid_call(body, ins, carry, name=name, grid=(rows // ts,), in_specs=in_specs, out_specs=out_specs,
                      out_shape=out_shape, dims=("arbitrary",))


def _group_sum(v, g):
    hi = v.astype(MXU_DT)
    lo = (v - hi.astype(F32)).astype(MXU_DT)
    return (jnp.dot(hi, g, preferred_element_type=F32) + jnp.dot(lo, g, preferred_element_type=F32))


def _rot(v):
    lane = lax.broadcasted_iota(jnp.int32, v.shape, 1)
    first = (lane & 31) < 16
    return jnp.where(first, -pltpu.roll(v, LANES - 16, 1), pltpu.roll(v, 16, 1))


def _head_group_matrix():
    r = jnp.arange(LANES) // HEAD_DIM
    return (r[:, None] == r[None, :]).astype(MXU_DT)


def _qk_prep(xin, width, cos, sin, w, *, name):
    rows = xin.shape[0]
    ts = min(rows, ROW_TS)
    nch = width // LANES

    def body(x_ref, cos_ref, sin_ref, w_ref, g_ref, o_ref):
        cs, sn, g = cos_ref[...], sin_ref[...], g_ref[...]
        for ch in range(nch):
            sl = slice(ch * LANES, (ch + 1) * LANES)
            xv = x_ref[:, sl]
            r = lax.rsqrt(_group_sum(xv * xv, g) * (1.0 / HEAD_DIM) + EPS)
            yw = (xv * r) * w_ref[:, sl]
            o_ref[:, sl] = (yw * cs + _rot(yw) * sn).astype(o_ref.dtype)

    return pl.pallas_call(
        body, name=name, grid=(rows // ts,),
        in_specs=[_row(ts, width), _row(ts, LANES), _row(ts, LANES), _vec(width),
                  pl.BlockSpec((LANES, LANES), lambda i: (0, 0))],
        out_specs=_row(ts, width),
        out_shape=jax.ShapeDtypeStruct((rows, width), MXU_DT), compiler_params=_cp("parallel"),
    )(xin, cos, sin, w, _head_group_matrix())


def _qk_prep_bwd(xin, douts, width, cos, sin, w, *, tail=None, name):
    rows = xin.shape[0]
    ts = min(rows, ROW_TS)
    nch = width // LANES
    has_tail = tail is not None
    nd = len(douts)
    assert sum(v.shape[1] for v in douts) == width
    src = [(k, c) for k, v in enumerate(douts) for c in range(v.shape[1] // LANES)]

    def body(*refs):
        x_ref, d_refs = refs[0], refs[1:1 + nd]
        t_ref = refs[1 + nd] if has_tail else None
        cos_ref, sin_ref, w_ref, g_ref, dx_ref, dw_ref = refs[1 + nd + int(has_tail):]
        i = pl.program_id(0)
        cs, sn, g = cos_ref[...], sin_ref[...], g_ref[...]

        @pl.when(i == 0)
        def _():
            dw_ref[...] = jnp.zeros_like(dw_ref)

        if has_tail:
            dx_ref[:, width:width + LANES] = t_ref[...].astype(dx_ref.dtype)
            dx_ref[:, width + LANES:] = jnp.zeros((ts, GLU_OFF - width - LANES), dx_ref.dtype)

        for ch in range(nch):
            sl = slice(ch * LANES, (ch + 1) * LANES)
            xv = x_ref[:, sl]
            dv = d_refs[src[ch][0]][:, src[ch][1] * LANES:(src[ch][1] + 1) * LANES].astype(F32)
            r = lax.rsqrt(_group_sum(xv * xv, g) * (1.0 / HEAD_DIM) + EPS)
            n = xv * r
            dyw = dv * cs - _rot(dv * sn)
            dw_ref[:, sl] += _colsum(dyw * n)
            dn = dyw * w_ref[:, sl]
            gm = _group_sum(dn * n, g) * (1.0 / HEAD_DIM)
            dx_ref[:, sl] = (r * (dn - n * gm)).astype(dx_ref.dtype)

    ins = [xin] + list(douts) + ([tail] if has_tail else []) + [cos, sin, w, _head_group_matrix()]
    in_specs = ([_row(ts, width)] + [_row(ts, v.shape[1]) for v in douts] + ([_row(ts, LANES)] if has_tail else [])
                + [_row(ts, LANES), _row(ts, LANES), _vec(width), pl.BlockSpec((LANES, LANES), lambda i: (0, 0))])
    out_w, arr_w = (GLU_OFF, IN_PAD) if has_tail else (width, width)
    return pl.pallas_call(
        body, name=name, grid=(rows // ts,), in_specs=in_specs,
        out_specs=[_row(ts, out_w), _vec(width)],
        out_shape=[jax.ShapeDtypeStruct((rows, arr_w), MXU_DT), jax.ShapeDtypeStruct((1, width), F32)],
        compiler_params=_cp("arbitrary"),
    )(*ins)


GB = GQA_GROUP * BLOCK
WIN = 3 * BLOCK
ATT_STEP = 8


def _win_start(i, seq):
    return pl.multiple_of(jnp.clip((i - 1) * BLOCK, 0, seq - WIN), BLOCK)


def _attn_mask(i, start):
    qpos = i * BLOCK + (lax.broadcasted_iota(jnp.int32, (GB, WIN), 0) & (BLOCK - 1))
    kpos = start + lax.broadcasted_iota(jnp.int32, (GB, WIN), 1)
    return jnp.abs(qpos - kpos) <= WINDOW


def _sink_col(sink_ref, kv):
    return jnp.concatenate(
        [jnp.full((BLOCK, 1), sink_ref[0, kv * GQA_GROUP + g], F32) for g in range(GQA_GROUP)], axis=0)


def _stack_slots(ref, rows):
    return jnp.concatenate([ref[rows, g * LANES:(g + 1) * LANES] for g in range(GQA_GROUP)], axis=0)


def _kv_lanes(kv):
    lane = lax.broadcasted_iota(jnp.int32, (1, LANES), 1)
    return (lane < HEAD_DIM) if kv == 0 else (lane >= HEAD_DIM)


def _keep(mask, v):
    return jnp.where(mask, v, jnp.zeros_like(v))


_NT = (((1,), (1,)), ((), ()))
_TN = (((0,), (0,)), ((), ()))


def _attn_specs(seq, n_ctx):
    qs = pl.BlockSpec((ATT_STEP * BLOCK, Q_COLS), lambda i: (i, 0))
    ks = pl.BlockSpec((seq, KV_COLS), lambda i: (0, Q_COLS // KV_COLS))
    vs = pl.BlockSpec((seq, KV_COLS), lambda i: (0, QK_COLS // KV_COLS))
    kcs = pl.BlockSpec((n_ctx, KV_COLS), lambda i: (0, 0))
    vcs = pl.BlockSpec((n_ctx, KV_COLS), lambda i: (0, 1))
    ls = pl.BlockSpec((ATT_STEP, N_KV_HEADS * GB, 1), lambda i: (i, 0, 0))
    return qs, ks, vs, kcs, vcs, ls


def _attn_fwd(sink, qk, proj, kc, kv_ctx, *, carry=None, name):
    seq, n_ctx = qk.shape[0], kc.shape[0]
    scale = 1.0 / math.sqrt(HEAD_DIM)

    def one_block(blk, sub, sink_ref, q_ref, k_ref, v_ref, kcw, vcw, o_ref, lse_ref):
        rows = slice(sub * BLOCK, (sub + 1) * BLOCK)
        start = _win_start(blk, seq)
        valid = _attn_mask(blk, start)
        qs = _stack_slots(q_ref, rows)
        kw, vw = k_ref[pl.ds(start, WIN), :], v_ref[pl.ds(start, WIN), :].astype(MXU_DT)
        o_all = jnp.zeros((GB, LANES), F32)
        for kv in range(N_KV_HEADS):
            mine = _kv_lanes(kv)
            s_loc = lax.dot_general(qs, _keep(mine, kw), _NT, preferred_element_type=F32) * scale
            s_loc = jnp.where(valid, s_loc, NEG_INF)
            s_ctx = lax.dot_general(qs, _keep(mine, kcw), _NT, preferred_element_type=F32) * scale
            sk = _sink_col(sink_ref, kv)
            m = jnp.maximum(jnp.maximum(jnp.max(s_loc, axis=-1, keepdims=True),
                                        jnp.max(s_ctx, axis=-1, keepdims=True)), sk)
            p_loc = jnp.exp(s_loc - m)
            p_ctx = jnp.exp(s_ctx - m)
            l = (jnp.sum(p_loc, axis=-1, keepdims=True) + jnp.sum(p_ctx, axis=-1, keepdims=True)
                 + jnp.exp(sk - m))
            o_all = o_all + (jnp.dot(p_loc.astype(MXU_DT), _keep(mine, vw), preferred_element_type=F32)
                             + jnp.dot(p_ctx.astype(MXU_DT), _keep(mine, vcw), preferred_element_type=F32)) / l
            lse_ref[sub, kv * GB:(kv + 1) * GB, :] = m + jnp.log(l)
        for g in range(GQA_GROUP):
            o_ref[rows, g * LANES:(g + 1) * LANES] = o_all[g * BLOCK:(g + 1) * BLOCK].astype(o_ref.dtype)

    def body(sink_ref, q_ref, k_ref, v_ref, kc_ref, vc_ref, o_ref, lse_ref):
        i = pl.program_id(0)
        kcw, vcw = kc_ref[...], vc_ref[...].astype(MXU_DT)
        for sub in range(ATT_STEP):
            one_block(i * ATT_STEP + sub, sub, sink_ref, q_ref, k_ref, v_ref, kcw, vcw, o_ref, lse_ref)

    qs, ks, vs, kcs, vcs, ls = _attn_specs(seq, n_ctx)
    return _grid_call(
        body, [sink, qk, qk, proj, kc, kv_ctx], carry, name=name, grid=(seq // (ATT_STEP * BLOCK),),
        in_specs=[pl.BlockSpec(memory_space=pltpu.SMEM), qs, ks, vs, kcs, vcs],
        out_specs=[qs, ls],
        out_shape=[jax.ShapeDtypeStruct((seq, Q_COLS), MXU_DT),
                   jax.ShapeDtypeStruct((seq // BLOCK, N_KV_HEADS * GB, 1), F32)],
        dims=("parallel",))


def _attn_bwd(sink, qk, proj, kc, kv_ctx, o, lse, do, *, carry=None, name):
    seq, n_ctx = qk.shape[0], kc.shape[0]
    scale = 1.0 / math.sqrt(HEAD_DIM)

    def body(sink_ref, q_ref, k_ref, v_ref, kc_ref, vc_ref, o_ref, lse_ref, do_ref,
             dq_ref, dk_ref, dv_ref, dkc_ref, dvc_ref, ds_ref):
        i = pl.program_id(0)

        @pl.when(i == 0)
        def _():
            dk_ref[...] = jnp.zeros_like(dk_ref)
            dv_ref[...] = jnp.zeros_like(dv_ref)
            dkc_ref[...] = jnp.zeros_like(dkc_ref)
            dvc_ref[...] = jnp.zeros_like(dvc_ref)
            ds_ref[...] = jnp.zeros_like(ds_ref)

        kcw, vcw = kc_ref[...], vc_ref[...].astype(MXU_DT)
        dkc, dvc = jnp.zeros((n_ctx, LANES), F32), jnp.zeros((n_ctx, LANES), F32)
        for sub in range(ATT_STEP):
            dkc_s, dvc_s = one_block(i * ATT_STEP + sub, sub, sink_ref, q_ref, k_ref, v_ref, kcw, vcw, o_ref,
                                     lse_ref, do_ref, dq_ref, dk_ref, dv_ref, ds_ref)
            dkc, dvc = dkc + dkc_s, dvc + dvc_s
        dkc_ref[...] += dkc
        dvc_ref[...] += dvc

    def one_block(blk, sub, sink_ref, q_ref, k_ref, v_ref, kcw, vcw, o_ref, lse_ref, do_ref,
                  dq_ref, dk_ref, dv_ref, ds_ref):
        qrows = slice(sub * BLOCK, (sub + 1) * BLOCK)
        start = _win_start(blk, seq)
        valid = _attn_mask(blk, start)
        win = pl.ds(start, WIN)
        qs, dos = _stack_slots(q_ref, qrows), _stack_slots(do_ref, qrows)
        do_o = dos.astype(F32) * _stack_slots(o_ref, qrows).astype(F32)
        kw, vw = k_ref[win, :], v_ref[win, :].astype(MXU_DT)
        dq = jnp.zeros((GB, LANES), F32)
        dk, dv = jnp.zeros((WIN, LANES), F32), jnp.zeros((WIN, LANES), F32)
        dkc, dvc = jnp.zeros((n_ctx, LANES), F32), jnp.zeros((n_ctx, LANES), F32)
        for kv in range(N_KV_HEADS):
            mine = _kv_lanes(kv)
            rows = slice(kv * GB, (kv + 1) * GB)
            lse_s = lse_ref[sub, rows, :]
            delta = jnp.sum(_keep(mine, do_o), axis=-1, keepdims=True)
            kz, vz, kcz, vcz = _keep(mine, kw), _keep(mine, vw), _keep(mine, kcw), _keep(mine, vcw)
            s_loc = lax.dot_general(qs, kz, _NT, preferred_element_type=F32) * scale
            s_loc = jnp.where(valid, s_loc, NEG_INF)
            s_ctx = lax.dot_general(qs, kcz, _NT, preferred_element_type=F32) * scale
            p_loc = jnp.exp(s_loc - lse_s)
            p_ctx = jnp.exp(s_ctx - lse_s)
            p_sink = jnp.exp(_sink_col(sink_ref, kv) - lse_s)
            dp_loc = lax.dot_general(dos, vz, _NT, preferred_element_type=F32)
            dp_ctx = lax.dot_general(dos, vcz, _NT, preferred_element_type=F32)
            ds_loc = (p_loc * (dp_loc - delta) * scale).astype(MXU_DT)
            ds_ctx = (p_ctx * (dp_ctx - delta) * scale).astype(MXU_DT)
            dq = dq + (jnp.dot(ds_loc, kz, preferred_element_type=F32)
                       + jnp.dot(ds_ctx, kcz, preferred_element_type=F32))
            dk = dk + _keep(mine, lax.dot_general(ds_loc, qs, _TN, preferred_element_type=F32))
            dv = dv + _keep(mine, lax.dot_general(p_loc.astype(MXU_DT), dos, _TN, preferred_element_type=F32))
            dkc = dkc + _keep(mine, lax.dot_general(ds_ctx, qs, _TN, preferred_element_type=F32))
            dvc = dvc + _keep(mine, lax.dot_general(p_ctx.astype(MXU_DT), dos, _TN, preferred_element_type=F32))
            ds_ref[rows, :] += -(p_sink * delta)
        for g in range(GQA_GROUP):
            dq_ref[qrows, g * LANES:(g + 1) * LANES] = dq[g * BLOCK:(g + 1) * BLOCK]
        dk_ref[win, :] += dk
        dv_ref[win, :] += dv
        return dkc, dvc

    qs, ks, vs, kcs, vcs, ls = _attn_specs(seq, n_ctx)
    whole = lambda r, c: pl.BlockSpec((r, c), lambda i: (0, 0))
    return _grid_call(
        body, [sink, qk, qk, proj, kc, kv_ctx, o, lse, do], carry, name=name, grid=(seq // (ATT_STEP * BLOCK),),
        in_specs=[pl.BlockSpec(memory_space=pltpu.SMEM), qs, ks, vs, kcs, vcs, qs, ls, qs],
        out_specs=[qs, whole(seq, KV_COLS), whole(seq, KV_COLS), whole(n_ctx, KV_COLS), whole(n_ctx, KV_COLS),
                   whole(N_KV_HEADS * GB, 1)],
        out_shape=[jax.ShapeDtypeStruct((seq, Q_COLS), F32), jax.ShapeDtypeStruct((seq, KV_COLS), F32),
                   jax.ShapeDtypeStruct((seq, KV_COLS), F32), jax.ShapeDtypeStruct((n_ctx, KV_COLS), F32),
                   jax.ShapeDtypeStruct((n_ctx, KV_COLS), F32), jax.ShapeDtypeStruct((N_KV_HEADS * GB, 1), F32)],
        dims=("arbitrary",))


def _halo_specs(ts, w, rows, col=0):
    per = ts // HALO
    last = rows // HALO - 1
    return [pl.BlockSpec((HALO, w), lambda i: (jnp.maximum(i * per - 1, 0), col)),
            pl.BlockSpec((ts, w), lambda i: (i, col)),
            pl.BlockSpec((HALO, w), lambda i: (jnp.minimum((i + 1) * per, last), col))]


def _glu(v):
    return v[:, :CONV_CH] * _sigmoid(v[:, CONV_CH:])


def _ln_stats(u):
    mu = jnp.mean(u, axis=-1, keepdims=True)
    xc = u - mu
    rstd = lax.rsqrt(jnp.mean(xc * xc, axis=-1, keepdims=True) + EPS)
    return xc * rstd, rstd


CONV_BWD_TS = 128


def _phases(ext_ref, ph_ref):
    n = ph_ref.shape[1]
    for b in range(1, 8):
        ph_ref[b - 1] = ext_ref[b:b + n, :]


def _window(ext_ref, ph_ref, o, n, cs):
    a, b = divmod(o, 8)
    src = ext_ref if b == 0 else ph_ref.at[b - 1]
    return src[8 * a:8 * a + n, cs]


def _conv_fwd(glu, col, cw, cb, lw, lb, *, carry=None, name):
    rows = glu.shape[0]
    ts = min(rows, 256)
    nt = rows // ts

    te = ts + 2 * HALO

    def body(gp_ref, g_ref, gn_ref, cw_ref, cb_ref, lw_ref, lb_ref, u3_ref, u1_ref, ext_ref, ph_ref):
        i = pl.program_id(0)
        ext_ref[0:HALO, :] = jnp.where(i > 0, _glu(gp_ref[...]), 0.0)
        ext_ref[HALO:HALO + ts, :] = _glu(g_ref[...])
        ext_ref[HALO + ts:, :] = jnp.where(i < nt - 1, _glu(gn_ref[...]), 0.0)
        _phases(ext_ref, ph_ref)
        for c in range(CONV_CH // LANES):
            cs = slice(c * LANES, (c + 1) * LANES)
            acc = jnp.broadcast_to(cb_ref[:, cs], (ts, LANES))
            for j in range(CONV_K):
                acc = acc + cw_ref[j:j + 1, cs] * _window(ext_ref, ph_ref, HALO - CONV_PAD + j, ts, cs)
            u1_ref[:, cs] = acc
        xh, _ = _ln_stats(u1_ref[...])
        u2 = xh * lw_ref[...] + lb_ref[...]
        u3_ref[...] = (u2 * _sigmoid(u2)).astype(u3_ref.dtype)

    full = lambda shape: pl.BlockSpec(shape, lambda i: (0,) * len(shape))
    return _grid_call(
        body, [glu, glu, glu, cw, cb, lw, lb], carry, name=name, grid=(nt,),
        in_specs=_halo_specs(ts, GLU_COLS, rows, col) + [full((CONV_K, CONV_CH))] + [_vec(CONV_CH)] * 3,
        out_specs=[_row(ts, CONV_CH), _row(ts, CONV_CH)],
        out_shape=[jax.ShapeDtypeStruct((rows, CONV_CH), MXU_DT), jax.ShapeDtypeStruct((rows, CONV_CH), F32)],
        scratch_shapes=[pltpu.VMEM((te, CONV_CH), F32), pltpu.VMEM((7, te - 8, CONV_CH), F32)],
        dims=("parallel",))


def _conv_bwd(glu, col, u1, du3, dproj, cw, lw, lb, *, carry=None, name):
    rows = glu.shape[0]
    ts = min(rows, CONV_BWD_TS)
    nt = rows // ts
    te = ts + 2 * HALO

    def du1_of(u1v, du3v, lw_v, lb_v):
        xh, rstd = _ln_stats(u1v)
        u2 = xh * lw_v + lb_v
        sg = _sigmoid(u2)
        du2 = du3v * (sg * (1.0 + u2 * (1.0 - sg)))
        dxh = du2 * lw_v
        du1 = rstd * (dxh - jnp.mean(dxh, axis=-1, keepdims=True)
                      - xh * jnp.mean(dxh * xh, axis=-1, keepdims=True))
        return du1, du2, xh

    half = ts // 2

    def body(gp_ref, g_ref, gn_ref, up_ref, u_ref, un_ref, dp_ref, d_ref, dn_ref, cw_ref, lw_ref, lb_ref,
             _, dglu_ref, dcw_ref, dvec_ref, u0_ref, du1_ref, pu_ref, pd_ref, du0_ref):
        i = pl.program_id(0)
        lw_v, lb_v = lw_ref[...], lb_ref[...]

        @pl.when(i == 0)
        def _():
            dcw_ref[...] = jnp.zeros_like(dcw_ref)
            dvec_ref[...] = jnp.zeros_like(dvec_ref)

        gv = g_ref[...]
        u0_ref[0:HALO, :] = jnp.where(i > 0, _glu(gp_ref[...]), 0.0)
        u0_ref[HALO:HALO + ts, :] = _glu(gv)
        u0_ref[HALO + ts:, :] = jnp.where(i < nt - 1, _glu(gn_ref[...]), 0.0)
        d_prev, _, _ = du1_of(up_ref[...], dp_ref[...], lw_v, lb_v)
        d_main, du2, xh = du1_of(u_ref[...], d_ref[...], lw_v, lb_v)
        d_next, _, _ = du1_of(un_ref[...], dn_ref[...], lw_v, lb_v)
        du1_ref[0:HALO, :] = jnp.where(i > 0, d_prev, 0.0)
        du1_ref[HALO:HALO + ts, :] = d_main
        du1_ref[HALO + ts:, :] = jnp.where(i < nt - 1, d_next, 0.0)

        rid = lax.broadcasted_iota(jnp.int32, (8, CONV_CH), 0)
        dvec_ref[...] += (jnp.where(rid == 0, _colsum(d_main), 0.0)
                          + jnp.where(rid == 1, _colsum(du2 * xh), 0.0)
                          + jnp.where(rid == 2, _colsum(du2), 0.0))
        _phases(u0_ref, pu_ref)
        _phases(du1_ref, pd_ref)
        for c in range(CONV_CH // LANES):
            cs = slice(c * LANES, (c + 1) * LANES)
            for r0 in (0, half):
                dm = du1_ref[HALO + r0:HALO + r0 + half, cs]
                acc = jnp.zeros((half, LANES), F32)
                for j in range(CONV_K):
                    acc = acc + cw_ref[j:j + 1, cs] * _window(du1_ref, pd_ref, r0 + HALO + CONV_PAD - j, half, cs)
                    prod = dm * _window(u0_ref, pu_ref, r0 + HALO - CONV_PAD + j, half, cs)
                    dcw_ref[j, :, cs] += jnp.sum(prod.reshape(half // 8, 8, LANES), axis=0)
                du0_ref[r0:r0 + half, cs] = acc
        du0 = du0_ref[...]
        ga, sg = gv[:, :CONV_CH], _sigmoid(gv[:, CONV_CH:])
        dglu_ref[:, :CONV_CH] = (du0 * sg).astype(dglu_ref.dtype)
        dglu_ref[:, CONV_CH:] = (du0 * ga * sg * (1.0 - sg)).astype(dglu_ref.dtype)

    full = lambda shape: pl.BlockSpec(shape, lambda i: (0,) * len(shape))
    return _grid_call(
        body, [glu, glu, glu, u1, u1, u1, du3, du3, du3, cw, lw, lb, dproj], carry, name=name, grid=(nt,),
        in_specs=(_halo_specs(ts, GLU_COLS, rows, col) + _halo_specs(ts, CONV_CH, rows)
                  + _halo_specs(ts, CONV_CH, rows) + [full((CONV_K, CONV_CH)), _vec(CONV_CH), _vec(CONV_CH)]
                  + [pl.BlockSpec(memory_space=pl.ANY)]),
        out_specs=[_row(ts, GLU_COLS, col), full((CONV_K, 8, CONV_CH)), full((8, CONV_CH))],
        out_shape=[jax.ShapeDtypeStruct(dproj.shape, dproj.dtype),
                   jax.ShapeDtypeStruct((CONV_K, 8, CONV_CH), F32), jax.ShapeDtypeStruct((8, CONV_CH), F32)],
        scratch_shapes=[pltpu.VMEM((te, CONV_CH), F32), pltpu.VMEM((te, CONV_CH), F32),
                        pltpu.VMEM((7, te - 8, CONV_CH), F32), pltpu.VMEM((7, te - 8, CONV_CH), F32),
                        pltpu.VMEM((ts, CONV_CH), F32)],
        input_output_aliases={12: 0}, dims=("arbitrary",))


FFN_CW = 1408
FFN_NJ = FFN_H // FFN_CW


def _ffn_halo_specs(ts, rows, col_of, inner_rows):
    per = ts // HALO
    last = rows // HALO - 1
    if inner_rows:
        return [pl.BlockSpec((HALO, FFN_CW), lambda j, i: (jnp.maximum(i * per - 1, 0), col_of(j))),
                pl.BlockSpec((ts, FFN_CW), lambda j, i: (i, col_of(j))),
                pl.BlockSpec((HALO, FFN_CW), lambda j, i: (jnp.minimum((i + 1) * per, last), col_of(j)))]
    return [pl.BlockSpec((HALO, FFN_CW), lambda i, j: (jnp.maximum(i * per - 1, 0), col_of(j))),
            pl.BlockSpec((ts, FFN_CW), lambda i, j: (i, col_of(j))),
            pl.BlockSpec((HALO, FFN_CW), lambda i, j: (jnp.minimum((i + 1) * per, last), col_of(j)))]


def _ffn_ext(p_ref, m_ref, n_ref, sl, i, nt):
    return jnp.concatenate([jnp.where(i > 0, p_ref[:, sl].astype(F32), 0.0), m_ref[:, sl].astype(F32),
                            jnp.where(i < nt - 1, n_ref[:, sl].astype(F32), 0.0)], axis=0)


def _prev_row(v):
    return pltpu.roll(v, 1, 0)


def _next_row(v):
    return pltpu.roll(v, v.shape[0] - 1, 0)


def _ffn_act(up0, w3, b3, *, name):
    rows = up0.shape[0]
    ts = min(rows, 256)
    nt = rows // ts
    main = slice(HALO, HALO + ts)

    def body(gp, g, gn, vp, v, vn, wg, wv, bg, bv, a_ref, go_ref, vo_ref):
        i = pl.program_id(0)
        for ch in range(FFN_CW // LANES):
            sl = slice(ch * LANES, (ch + 1) * LANES)
            xg, xv = _ffn_ext(gp, g, gn, sl, i, nt), _ffn_ext(vp, v, vn, sl, i, nt)
            wgv, wvv = wg[:, sl], wv[:, sl]
            gate = (wgv[0:1] * _prev_row(xg) + wgv[1:2] * xg + wgv[2:3] * _next_row(xg))[main] + bg[:, sl]
            val = (wvv[0:1] * _prev_row(xv) + wvv[1:2] * xv + wvv[2:3] * _next_row(xv))[main] + bv[:, sl]
            a_ref[:, sl] = (gate * _sigmoid(gate) * val).astype(a_ref.dtype)
            go_ref[:, sl] = gate.astype(go_ref.dtype)
            vo_ref[:, sl] = val.astype(vo_ref.dtype)

    gcol, vcol = (lambda j: j), (lambda j: j + FFN_NJ)
    wspec = lambda col_of: pl.BlockSpec((FFN_K, FFN_CW), lambda i, j: (0, col_of(j)))
    bspec = lambda col_of: pl.BlockSpec((1, FFN_CW), lambda i, j: (0, col_of(j)))
    ospec = pl.BlockSpec((ts, FFN_CW), lambda i, j: (i, j))
    return pl.pallas_call(
        body, name=name, grid=(nt, FFN_NJ),
        in_specs=(_ffn_halo_specs(ts, rows, gcol, False) + _ffn_halo_specs(ts, rows, vcol, False)
                  + [wspec(gcol), wspec(vcol), bspec(gcol), bspec(vcol)]),
        out_specs=[ospec] * 3, out_shape=[jax.ShapeDtypeStruct((rows, FFN_H), MXU_DT)] * 3,
        compiler_params=_cp("parallel", "parallel"),
    )(up0, up0, up0, up0, up0, up0, w3, w3, b3, b3)


def _ffn_act_bwd(up0, gate_s, val_s, da, w3, *, carry=None, name):
    rows = up0.shape[0]
    ts = min(rows, 256)
    nt = rows // ts
    main = slice(HALO, HALO + ts)

    def body(gp, g, gn, vp, v, vn, sgp, sg, sgn, svp, sv, svn, ap, a, an, wg, wv,
             dg_ref, dv_ref, sg_ref, sv_ref):
        i = pl.program_id(1)

        @pl.when(i == 0)
        def _():
            sg_ref[...] = jnp.zeros_like(sg_ref)
            sv_ref[...] = jnp.zeros_like(sv_ref)

        rid = lax.broadcasted_iota(jnp.int32, (8, LANES), 0)
        for ch in range(FFN_CW // LANES):
            sl = slice(ch * LANES, (ch + 1) * LANES)
            xg, xv, da_e = _ffn_ext(gp, g, gn, sl, i, nt), _ffn_ext(vp, v, vn, sl, i, nt), _ffn_ext(ap, a, an, sl, i, nt)
            gate, val = _ffn_ext(sgp, sg, sgn, sl, i, nt), _ffn_ext(svp, sv, svn, sl, i, nt)
            wgv, wvv = wg[:, sl], wv[:, sl]
            xg_p, xg_n, xv_p, xv_n = _prev_row(xg), _next_row(xg), _prev_row(xv), _next_row(xv)
            sgm = _sigmoid(gate)
            eg = da_e * val * (sgm * (1.0 + gate * (1.0 - sgm)))
            ev = da_e * (gate * sgm)
            for e, taps, w, d_ref, s_ref in ((eg, (xg_p, xg, xg_n), wgv, dg_ref, sg_ref),
                                             (ev, (xv_p, xv, xv_n), wvv, dv_ref, sv_ref)):
                d0 = w[0:1] * _next_row(e) + w[1:2] * e + w[2:3] * _prev_row(e)
                d_ref[:, sl] = d0[main].astype(d_ref.dtype)
                dm = e[main]
                s_ref[:, sl] += (jnp.where(rid == 0, _colsum(dm * taps[0][main]), 0.0)
                                 + jnp.where(rid == 1, _colsum(dm * taps[1][main]), 0.0)
                                 + jnp.where(rid == 2, _colsum(dm * taps[2][main]), 0.0)
                                 + jnp.where(rid == 3, _colsum(dm), 0.0))

    gcol, vcol = (lambda j: j), (lambda j: j + FFN_NJ)
    wspec = lambda col_of: pl.BlockSpec((FFN_K, FFN_CW), lambda j, i: (0, col_of(j)))
    ospec = pl.BlockSpec((ts, FFN_CW), lambda j, i: (i, j))
    sspec = pl.BlockSpec((8, FFN_CW), lambda j, i: (0, j))
    return _grid_call(
        body, [up0] * 6 + [gate_s] * 3 + [val_s] * 3 + [da] * 3 + [w3, w3], carry, name=name, grid=(FFN_NJ, nt),
        in_specs=(_ffn_halo_specs(ts, rows, gcol, True) + _ffn_halo_specs(ts, rows, vcol, True)
                  + _ffn_halo_specs(ts, rows, gcol, True) * 3 + [wspec(gcol), wspec(vcol)]),
        out_specs=[ospec, ospec, sspec, sspec],
        out_shape=[jax.ShapeDtypeStruct((rows, FFN_H), MXU_DT), jax.ShapeDtypeStruct((rows, FFN_H), MXU_DT),
                   jax.ShapeDtypeStruct((8, FFN_H), F32), jax.ShapeDtypeStruct((8, FFN_H), F32)],
        dims=("parallel", "arbitrary"))


def _adam_math(w, g, m, v):
    m = ADAM_B1 * m + (1.0 - ADAM_B1) * g
    v = ADAM_B2 * v + (1.0 - ADAM_B2) * (g * g)
    m_hat = m / (1.0 - ADAM_B1 ** ADAM_STEP)
    v_hat = v / (1.0 - ADAM_B2 ** ADAM_STEP)
    delta = -ADAM_LR * (m_hat / (jnp.sqrt(v_hat) + ADAM_EPS) + ADAM_WD * w)
    return delta, m, v


ROW_TILE_BYTES = 8 << 20


def _row_tile(rows, row_bytes):
    tiles = [rows] + [rows // k for k in range(2, rows // 16 + 1) if rows % k == 0 and (rows // k) % 16 == 0]
    return next(t for t in tiles if t * row_bytes <= ROW_TILE_BYTES)


def _adam(w, m, v, parts, *, name):
    rows, cols = w.shape
    nparts = parts.shape[0]
    tr = _row_tile(rows, cols * (7 * 4 + nparts * parts.dtype.itemsize))

    def body(w_ref, m_ref, v_ref, p_ref, g_ref, d_ref, nm_ref, nv_ref):
        g = p_ref[0].astype(F32)
        for p in range(1, nparts):
            g = g + p_ref[p].astype(F32)
        g_ref[...] = g
        d_ref[...], nm_ref[...], nv_ref[...] = _adam_math(w_ref[...], g, m_ref[...], v_ref[...])

    spec = _row(tr, cols)
    return pl.pallas_call(
        body, name=name, grid=(rows // tr,),
        in_specs=[spec, spec, spec, pl.BlockSpec((nparts, tr, cols), lambda i: (0, i, 0))],
        out_specs=[spec] * 4, out_shape=[jax.ShapeDtypeStruct((rows, cols), F32)] * 4,
        compiler_params=_cp("parallel"),
    )(w, m, v, parts)


def _adam_many(ws, ms, vs, gs, *, name):
    n = len(ws)

    def body(*refs):
        ins, outs = refs[:4 * n], refs[4 * n:]
        for k in range(n):
            delta, new_m, new_v = _adam_math(ins[k][...], ins[3 * n + k][...], ins[n + k][...], ins[2 * n + k][...])
            outs[k][...], outs[n + k][...], outs[2 * n + k][...] = delta, new_m, new_v

    vm = pl.BlockSpec(memory_space=pltpu.VMEM)
    res = pl.pallas_call(
        body, name=name, in_specs=[vm] * (4 * n), out_specs=[vm] * (3 * n),
        out_shape=[jax.ShapeDtypeStruct(w.shape, F32) for w in ws] * 3,
        compiler_params=pltpu.CompilerParams(vmem_limit_bytes=VMEM_LIMIT),
    )(*ws, *ms, *vs, *gs)
    return res[:n], res[n:2 * n], res[2 * n:]


def _sum_parts(parts, *, name):
    nparts, rows, cols = parts.shape
    tr = _row_tile(rows, cols * (4 + nparts * parts.dtype.itemsize))

    def body(p_ref, o_ref):
        g = p_ref[0].astype(F32)
        for p in range(1, nparts):
            g = g + p_ref[p].astype(F32)
        o_ref[...] = g

    return pl.pallas_call(
        body, name=name, grid=(rows // tr,),
        in_specs=[pl.BlockSpec((nparts, tr, cols), lambda i: (0, i, 0))], out_specs=_row(tr, cols),
        out_shape=jax.ShapeDtypeStruct((rows, cols), F32), compiler_params=_cp("parallel"),
    )(parts)


def _my_place():
    return lax.axis_index("x"), lax.axis_index("y"), lax.axis_index("c")


def _dev_index(p):
    return 4 * p[0] + 2 * p[1] + p[2]


def _all_gather(xs, *, name):
    return _run_comm(_gather_plan(xs), pltpu.VMEM, name)


class _Comm(NamedTuple):
    ins: list
    outs: list
    n_remote: int
    n_local: int
    start: Callable
    finish: Callable


def _join_plans(*plans):
    def split(in_refs, out_refs, send_sems, recv_sems, local_sems):
        i = o = r = l = 0
        for p in plans:
            ni, no = len(p.ins), len(p.outs)
            yield p, (in_refs[i:i + ni], out_refs[o:o + no], send_sems.at[pl.ds(r, p.n_remote)],
                      recv_sems.at[pl.ds(r, p.n_remote)], local_sems.at[pl.ds(l, p.n_local)])
            i, o, r, l = i + ni, o + no, r + p.n_remote, l + p.n_local

    def start(*refs):
        for p, part in split(*refs):
            p.start(*part)

    def finish(*refs):
        for p, part in split(*refs):
            p.finish(*part)

    return _Comm([v for p in plans for v in p.ins], [v for p in plans for v in p.outs],
                 sum(p.n_remote for p in plans), sum(p.n_local for p in plans), start, finish)


def _comm_scratch(plan):
    return [pltpu.SemaphoreType.DMA((plan.n_remote,)), pltpu.SemaphoreType.DMA((plan.n_remote,)),
            pltpu.SemaphoreType.DMA((plan.n_local,))]


def _run_comm(plan, space, name):
    n_in, n_out = len(plan.ins), len(plan.outs)

    def body(*refs):
        args = (refs[:n_in], refs[n_in:n_in + n_out], *refs[n_in + n_out:])
        plan.start(*args)
        plan.finish(*args)

    return pl.pallas_call(
        body, name=name, out_shape=plan.outs,
        in_specs=[pl.BlockSpec(memory_space=space)] * n_in, out_specs=[pl.BlockSpec(memory_space=space)] * n_out,
        scratch_shapes=_comm_scratch(plan),
        compiler_params=pltpu.CompilerParams(vmem_limit_bytes=VMEM_LIMIT),
    )(*plan.ins)


def _grid_call(body, ins, carry, *, name, grid, in_specs, out_specs, out_shape, dims, scratch_shapes=(),
               input_output_aliases=None):
    if carry is None:
        res = pl.pallas_call(
            body, name=name, grid=grid, in_specs=list(in_specs), out_specs=list(out_specs),
            out_shape=list(out_shape), scratch_shapes=list(scratch_shapes),
            input_output_aliases=input_output_aliases or {}, compiler_params=_cp(*dims))(*ins)
        return list(res), None

    def at(pos):
        conds = [pl.program_id(k) == p for k, p in enumerate(pos)]
        out = conds[0]
        for cnd in conds[1:]:
            out = jnp.logical_and(out, cnd)
        return out

    return _carried_call(body, carry, lambda: at([0] * len(grid)), lambda: at([g - 1 for g in grid]), ins,
                         name=name, grid=grid, in_specs=in_specs, out_specs=out_specs, out_shape=out_shape,
                         scratch_shapes=scratch_shapes, input_output_aliases=input_output_aliases)


def _carried_call(body, plan, first, last, ins, *, name, grid, in_specs, out_specs, out_shape, scratch_shapes=(),
                  input_output_aliases=None):
    in_specs, out_specs, out_shape = list(in_specs), list(out_specs), list(out_shape)
    n_in, n_out, n_scr = len(in_specs), len(out_specs), len(scratch_shapes)
    c_in, c_out = len(plan.ins), len(plan.outs)
    hbm = pl.BlockSpec(memory_space=pl.ANY)

    def full_body(*refs):
        ins, c_ins = refs[:n_in], refs[n_in:n_in + c_in]
        outs = refs[n_in + c_in:n_in + c_in + n_out]
        c_outs = refs[n_in + c_in + n_out:n_in + c_in + n_out + c_out]
        scr = refs[n_in + c_in + n_out + c_out:]
        sems = scr[n_scr:]

        @pl.when(first())
        def _():
            plan.start(c_ins, c_outs, *sems)

        body(*ins, *outs, *scr[:n_scr])

        @pl.when(last())
        def _():
            plan.finish(c_ins, c_outs, *sems)

    res = pl.pallas_call(
        full_body, name=name, grid=grid,
        in_specs=in_specs + [hbm] * c_in, out_specs=out_specs + [hbm] * c_out,
        out_shape=out_shape + list(plan.outs),
        scratch_shapes=list(scratch_shapes) + _comm_scratch(plan),
        input_output_aliases=input_output_aliases or {},
        compiler_params=_cp(*(["arbitrary"] * len(grid))),
    )(*ins, *plan.ins)
    return list(res[:n_out]), list(res[n_out:])


def _gather_plan(xs):
    n = len(xs)
    ms = [v.shape[0] for v in xs]

    def tools(x_refs, o_refs, send_sems, recv_sems, local_sems):
        x, y, c = _my_place()
        me, sib = (x, y, c), (x, y, 1 - c)
        chips = [(1 - x, y), (x, 1 - y), (1 - x, 1 - y)]

        def rows(a, p):
            return o_refs[a].at[pl.ds(pl.multiple_of(_dev_index(p) * ms[a], 8), ms[a])]

        def copy(a, k, block, to, src=None):
            return pltpu.make_async_remote_copy(
                src_ref=rows(a, block) if src is None else src, dst_ref=rows(a, block),
                send_sem=send_sems.at[a * 7 + k], recv_sem=recv_sems.at[a * 7 + k],
                device_id=to, device_id_type=MESH)

        mine = [pltpu.make_async_copy(x_refs[a], rows(a, me), local_sems.at[a]) for a in range(n)]
        first = []
        for a in range(n):
            first.append(copy(a, 0, me, sib, src=x_refs[a]))
            first += [copy(a, 1 + j, me, (*chip, c), src=x_refs[a]) for j, chip in enumerate(chips)]
        return me, sib, chips, c, copy, mine, first

    def start(*refs):
        _, _, _, _, _, mine, first = tools(*refs)
        for cp in mine + first:
            cp.start()

    def finish(*refs):
        me, sib, chips, c, copy, mine, first = tools(*refs)
        passed = []
        for j, chip in enumerate(chips):
            for a in range(n):
                copy(a, 1 + j, (*chip, c), me).wait_recv()
                cp = copy(a, 4 + j, (*chip, c), sib)
                cp.start()
                passed.append(cp)
        for a in range(n):
            copy(a, 0, sib, me).wait_recv()
            for j, chip in enumerate(chips):
                copy(a, 4 + j, (*chip, 1 - c), me).wait_recv()
        for cp in first + passed:
            cp.wait_send()
        for cp in mine:
            cp.wait()

    outs = [jax.ShapeDtypeStruct((N_DEV * v.shape[0], v.shape[1]), v.dtype) for v in xs]
    return _Comm(list(xs), outs, 7 * n, n, start, finish)


def _exchange_plan(gs, cols=None):
    n = len(gs)
    rs = [v.shape[0] // N_DEV for v in gs]
    flips = [(bx, by, bc) for bx in (0, 1) for by in (0, 1) for bc in (0, 1)][1:]

    def tools(g_refs, o_refs, send_sems, recv_sems, local_sems):
        x, y, c = _my_place()
        me = (x, y, c)

        def block(ref, a, p):
            return ref.at[pl.ds(_dev_index(p) * rs[a], rs[a])]

        def src(a, p):
            rows = pl.ds(_dev_index(p) * rs[a], rs[a])
            return g_refs[a].at[rows] if cols is None else g_refs[a].at[rows, pl.ds(cols[0], cols[1])]

        def peer(f):
            return (1 - x if f[0] else x, 1 - y if f[1] else y, 1 - c if f[2] else c)

        def copy(a, k, to):
            return pltpu.make_async_remote_copy(
                src_ref=src(a, to), dst_ref=block(o_refs[a], a, me),
                send_sem=send_sems.at[a * 7 + k], recv_sem=recv_sems.at[a * 7 + k],
                device_id=to, device_id_type=MESH)

        def arrival(a, k, frm):
            return pltpu.make_async_remote_copy(
                src_ref=src(a, frm), dst_ref=block(o_refs[a], a, frm),
                send_sem=send_sems.at[a * 7 + k], recv_sem=recv_sems.at[a * 7 + k],
                device_id=frm, device_id_type=MESH)

        mine = [pltpu.make_async_copy(src(a, me), block(o_refs[a], a, me), local_sems.at[a])
                for a in range(n)]
        sends = [copy(a, k, peer(f)) for a in range(n) for k, f in enumerate(flips)]
        arrivals = [arrival(a, k, peer(f)) for a in range(n) for k, f in enumerate(flips)]
        return mine, sends, arrivals

    def start(*refs):
        mine, sends, _ = tools(*refs)
        for cp in mine + sends:
            cp.start()

    def finish(*refs):
        mine, sends, arrivals = tools(*refs)
        for cp in arrivals:
            cp.wait_recv()
        for cp in sends:
            cp.wait_send()
        for cp in mine:
            cp.wait()

    outs = [jax.ShapeDtypeStruct((v.shape[0], v.shape[1] if cols is None else cols[1]), v.dtype) for v in gs]
    return _Comm(list(gs), outs, 7 * n, n, start, finish)


def _rope_tables(seq):
    t = jnp.arange(seq)
    row, col = t // GRID_W, t % GRID_W
    half = HEAD_DIM // 2
    inv = ROPE_BASE ** (-jnp.arange(0, half, 2, dtype=F32) / half)
    ang_r = row.astype(F32)[:, None] * inv
    ang_c = col.astype(F32)[:, None] * inv
    ang = jnp.concatenate([ang_r, ang_r, ang_c, ang_c], axis=-1)
    return jnp.tile(jnp.cos(ang), (1, 2)), jnp.tile(jnp.sin(ang), (1, 2))


def _to_slots(w):
    return w.reshape(N_KV_HEADS, GQA_GROUP, HEAD_DIM, w.shape[1]).transpose(1, 0, 2, 3).reshape(w.shape)


def _from_slots(w):
    return w.reshape(GQA_GROUP, N_KV_HEADS, HEAD_DIM, w.shape[1]).transpose(1, 0, 2, 3).reshape(w.shape)


def _pack(vs):
    flat = jnp.concatenate([v.reshape(-1).astype(F32) for v in vs])
    total = -(-flat.shape[0] // (8 * LANES)) * (8 * LANES)
    return jnp.pad(flat, (0, total - flat.shape[0])).reshape(-1, LANES)


def _unpack(packed, like):
    flat, out, off = packed.reshape(-1), [], 0
    for v in like:
        size = math.prod(v.shape)
        out.append(flat[off:off + size].reshape(v.shape))
        off += size
    return out


def _silu(v):
    return v * jax.nn.sigmoid(v)


def kernel(x, c, ctx, c_ctx, w_mod, b_mod, norm_mix_w, w_in, q_norm_w, k_norm_w, sink_logit, conv_w, conv_b, conv_norm_w, conv_norm_b, w_out, norm_ffn_w, w_up, ffn_conv_w, ffn_conv_b, w_down, loss_target, m_c_ctx, m_w_mod, m_b_mod, m_norm_mix_w, m_w_in, m_q_norm_w, m_k_norm_w, m_sink_logit, m_conv_w, m_conv_b, m_conv_norm_w, m_conv_norm_b, m_w_out, m_norm_ffn_w, m_w_up, m_ffn_conv_w, m_ffn_conv_b, m_w_down, v_c_ctx, v_w_mod, v_b_mod, v_norm_mix_w, v_w_in, v_q_norm_w, v_k_norm_w, v_sink_logit, v_conv_w, v_conv_b, v_conv_norm_w, v_conv_norm_b, v_w_out, v_norm_ffn_w, v_w_up, v_ffn_conv_w, v_ffn_conv_b, v_w_down):
    d = D_MODEL
    seq, n_ctx = x.shape[1], ctx.shape[1]
    me = _dev_index(_my_place())
    xs, ctxs, tgt = x[0], ctx[0], loss_target[0]

    small = _pack([c[0], conv_w[0], ffn_conv_w[0]])
    small_all = _all_gather([small], name="gather_small")[0].reshape(N_DEV, -1)
    n_cw, n_fw = conv_w[0].size, ffn_conv_w[0].size
    c_all = small_all[:, :d]
    cw_all = small_all[:, d:d + n_cw].reshape(N_DEV, CONV_K, -1)
    fw_all = small_all[:, d + n_cw:d + n_cw + n_fw].reshape(N_DEV, FFN_K, -1)
    conv_w_f = cw_all.transpose(1, 0, 2).reshape(CONV_K, CONV_CH)
    ffn_w_f = fw_all.transpose(1, 0, 2).reshape(FFN_K, 2 * FFN_H)

    mcols = w_mod.shape[2]
    act = jnp.zeros((16, d), F32).at[:N_DEV].set(_silu(c_all)).at[N_DEV].set(_silu(c_ctx))
    mod_part = _mm(act, w_mod[0], 16, mcols, d, name="mod_fwd")
    mod_all = _all_gather([mod_part], name="gather_mod")[0]
    mod_all = mod_all.reshape(N_DEV, 16, mcols).transpose(1, 0, 2).reshape(16, 6 * d) + b_mod
    mod = lax.dynamic_slice_in_dim(mod_all, me, 1, axis=0)
    sh1, sc1, g1, sh2, sc2, g2 = [mod[:, k * d:(k + 1) * d] for k in range(6)]
    sh1c, sc1c = mod_all[N_DEV:N_DEV + 1, :d], mod_all[N_DEV:N_DEV + 1, d:2 * d]

    cos, sin = _rope_tables(seq)
    ones_c, zeros_c = jnp.ones((n_ctx, LANES), F32), jnp.zeros((n_ctx, LANES), F32)
    qk_w = jnp.concatenate([jnp.tile(q_norm_w, (1, N_Q_HEADS)), jnp.tile(k_norm_w, (1, N_KV_HEADS))], axis=1)
    kc_w = jnp.tile(k_norm_w, (1, N_KV_HEADS))

    h, (w_in_t,) = _prenorm(xs, norm_mix_w, sc1, sh1, carry=_gather_plan([w_in[0].T.astype(MXU_DT)]),
                            name="prenorm_mix")
    hc, _ = _prenorm(ctxs, norm_mix_w, sc1c, sh1c, name="prenorm_ctx")
    w_in_p = jnp.concatenate([_to_slots(w_in_t[:Q_COLS]), w_in_t[Q_COLS:QKV_COLS],
                              jnp.zeros((GLU_OFF - QKV_COLS, d), MXU_DT), w_in_t[QKV_COLS:]])
    proj = _mm(h, w_in_p, seq, IN_PAD, d, tb=True, name="proj_in")
    kv_ctx = _mm(hc, w_in_p, n_ctx, 2 * KV_COLS, d, tb=True, n0=Q_COLS, name="proj_ctx")
    qk_r = _qk_prep(proj, QK_COLS, cos, sin, qk_w, name="qk_prep")
    kc_n = _qk_prep(kv_ctx, KV_COLS, ones_c, zeros_c, kc_w, name="k_ctx_prep")
    (attn_o, lse), (w_up_t, w_out_f) = _attn_fwd(
        sink_logit, qk_r, proj, kc_n, kv_ctx, name="attn_fwd",
        carry=_gather_plan([w_up[0].T.astype(MXU_DT), w_out[0].astype(MXU_DT)]))
    w_out_f = jnp.concatenate([_to_slots(w_out_f[:Q_COLS]), w_out_f[Q_COLS:]])
    (u3, u1), (w_down_f,) = _conv_fwd(proj, GLU_OFF // GLU_COLS, conv_w_f, conv_b, conv_norm_w, conv_norm_b,
                                      carry=_gather_plan([w_down[0].astype(MXU_DT)]), name="conv_fwd")
    mix, x1, h2 = _out_proj_prenorm(attn_o, u3, w_out_f, xs, g1, norm_ffn_w, sc2, sh2, name="out_proj")

    up0 = _mm(h2, w_up_t, seq, 2 * FFN_H, d, tb=True, out_dtype=MXU_DT, name="ffn_up")
    act_a, gate_s, val_s = _ffn_act(up0, ffn_w_f, ffn_conv_b, name="ffn_act")
    loss_p, dy, dffn, dg2 = _ffn_down_loss(act_a, w_down_f, x1, g2, tgt, name="ffn_down_loss")

    da = _mm(dffn, w_down_f, seq, FFN_H, d, tb=True, out_dtype=MXU_DT, name="ffn_down_dx")
    gw_down = _mm(act_a, dffn, FFN_H, d, seq, ta=True, out_dtype=MXU_DT, name="ffn_down_dw")
    (dgate0, dval0, s_gate, s_val), (rx_down,) = _ffn_act_bwd(
        up0, gate_s, val_s, da, ffn_w_f, carry=_exchange_plan([gw_down]), name="ffn_act_bwd")
    gw_up_t = _mm(dgate0, h2, FFN_H, d, seq, ta=True, out_dtype=MXU_DT, out_rows=2 * FFN_H, name="ffn_up_dw_gate")
    gw_up_t = _mm(dval0, h2, FFN_H, d, seq, ta=True, into=gw_up_t, m0=FFN_H, name="ffn_up_dw_val")
    qd = d // 4
    dh2, (rx_up_a,) = _mm(dgate0, w_up_t, seq, d, FFN_H, a2=dval0, out_dtype=MXU_DT, name="ffn_up_dx",
                          carry=_exchange_plan([gw_up_t], cols=(0, qd)))
    (dx1, dmix, dsh2, dsc2, dnw2, dg1), _ = _norm_bwd(
        dh2, x1, norm_ffn_w, sc2, res=dy, gate=(mix, g1), name="prenorm_ffn_bwd")

    dattn = _mm(dmix, w_out_f, seq, Q_COLS, d, tb=True, out_dtype=MXU_DT, name="out_dx_attn")
    du3 = _mm(dmix, w_out_f, seq, CONV_CH, d, tb=True, n0=Q_COLS, out_dtype=MXU_DT, name="out_dx_conv")
    gw_out = _mm(attn_o, dmix, Q_COLS, d, seq, ta=True, out_dtype=MXU_DT, out_rows=Q_COLS + CONV_CH,
                 name="out_dw_attn")
    gw_out = _mm(u3, dmix, CONV_CH, d, seq, ta=True, into=gw_out, m0=Q_COLS, name="out_dw_conv")
    gw_out = jnp.concatenate([_from_slots(gw_out[:Q_COLS]), gw_out[Q_COLS:]])
    (dq, dk, dv, dkc_r, dvc, dsink_rows), (rx_out, rx_up_b) = _attn_bwd(
        sink_logit, qk_r, proj, kc_n, kv_ctx, attn_o, lse, dattn, name="attn_bwd",
        carry=_join_plans(_exchange_plan([gw_out]), _exchange_plan([gw_up_t], cols=(qd, qd))))
    dproj, dqk_w = _qk_prep_bwd(proj, [dq, dk], QK_COLS, cos, sin, qk_w, tail=dv, name="qk_prep_bwd")
    (dproj, dcw8, dvec), (rx_up_c,) = _conv_bwd(
        proj, GLU_OFF // GLU_COLS, u1, du3, dproj, conv_w_f, conv_norm_w, conv_norm_b,
        carry=_exchange_plan([gw_up_t], cols=(2 * qd, 2 * qd)), name="conv_bwd")
    dkc, dkc_w = _qk_prep_bwd(kv_ctx, [dkc_r], KV_COLS, ones_c, zeros_c, kc_w, name="k_ctx_prep_bwd")
    dkv_ctx = jnp.concatenate([dkc, dvc.astype(MXU_DT)], axis=1)
    gw_p = _mm(dproj, h, IN_PAD, d, seq, ta=True, out_dtype=MXU_DT, name="proj_dw")
    gw_ctx = _mm(dkv_ctx, hc, 2 * KV_COLS, d, n_ctx, ta=True, name="proj_dw_ctx")
    gw_kv = (gw_p[Q_COLS:QKV_COLS].astype(F32) + gw_ctx).astype(MXU_DT)
    gw_in_t = jnp.concatenate([_from_slots(gw_p[:Q_COLS]), gw_kv, gw_p[GLU_OFF:]], axis=0)
    dh, (rx_in,) = _mm(dproj, w_in_p, seq, d, IN_PAD, out_dtype=MXU_DT, carry=_exchange_plan([gw_in_t]),
                       name="proj_dx")
    dhc = _mm(dkv_ctx, w_in_p, n_ctx, d, 2 * KV_COLS, k0=Q_COLS, name="proj_dx_ctx")
    (grad_x, dsh1, dsc1, dnw1), _ = _norm_bwd(dh, xs, norm_mix_w, sc1, res=dx1, name="prenorm_mix_bwd")
    (dsh1c, dsc1c, dnw1c), _ = _norm_bwd(dhc, ctxs, norm_mix_w, sc1c, want_dx=False, name="prenorm_ctx_bwd")

    dmod = jnp.concatenate([dsh1, dsc1, dg1, dsh2, dsc2, dg2], axis=1)
    dmod_ctx = jnp.concatenate([dsh1c, dsc1c], axis=1)
    d_qn = dqk_w[0, :Q_COLS].reshape(N_Q_HEADS, HEAD_DIM).sum(0)
    d_kn = (dqk_w[0, Q_COLS:].reshape(N_KV_HEADS, HEAD_DIM).sum(0)
            + dkc_w[0].reshape(N_KV_HEADS, HEAD_DIM).sum(0))
    d_ffn_w = jnp.concatenate([s_gate[:FFN_K], s_val[:FFN_K]], axis=1)
    d_ffn_b = jnp.concatenate([s_gate[FFN_K], s_val[FFN_K]])
    d_sink = dsink_rows.reshape(N_Q_HEADS, BLOCK).sum(1)
    summed_like = [(dnw1 + dnw1c), d_qn[None], d_kn[None], d_sink[None], dvec[0:1], dvec[1:2],
                   dvec[2:3], dnw2, d_ffn_b[None], dcw8.sum(1), d_ffn_w, loss_p[0:1, 0:1]]
    pack = _pack([dmod, dmod_ctx] + summed_like)
    pack_all = _all_gather([pack], name="gather_small_grads")[0]
    pack_all = pack_all.reshape(N_DEV, pack.shape[0], LANES)
    tot = _sum_parts(pack_all, name="sum_small_grads")
    (dmod_sum, dmc_sum, g_nmix, g_qn, g_kn, g_sink, g_cb, g_lw, g_lb, g_nffn, g_fb, g_cw_f, g_fw_f,
     loss_sum) = _unpack(tot, [dmod, dmod_ctx] + summed_like)
    loss = loss_sum[0, 0]
    dmod_all = pack_all.reshape(N_DEV, -1)[:, :6 * d]
    g_b_mod = dmod_sum.at[:, :2 * d].add(dmc_sum)

    lo = me * mcols
    dm_rows = jnp.zeros((16, 6 * d), F32).at[:N_DEV].set(dmod_all).at[N_DEV, :2 * d].set(dmc_sum[0])
    dm_mine = lax.dynamic_slice_in_dim(dm_rows, lo, mcols, axis=1)
    parts_mod = _mm(act, dm_mine, d, mcols, 16, ta=True, name="mod_dw")[None]
    dact_part = _mm(dm_mine[N_DEV:N_DEV + 8], w_mod[0], 8, d, mcols, tb=True, name="mod_dx_ctx")
    dact_all = _all_gather([dact_part], name="gather_c_ctx_grad")[0].reshape(N_DEV, 8, d)
    dact = _sum_parts(dact_all, name="sum_c_ctx_grad")[0]
    sg = jax.nn.sigmoid(c_ctx)
    g_c_ctx = dact * (sg * (1.0 + c_ctx * (1.0 - sg)))

    def stacked(rx):
        return rx.reshape(N_DEV, rx.shape[0] // N_DEV, rx.shape[1])

    g_w_in = _sum_parts(stacked(rx_in), name="sum_w_in").T[None]
    g_w_up = jnp.concatenate([_sum_parts(stacked(rx_up_a), name="sum_w_up_a"),
                              _sum_parts(stacked(rx_up_b), name="sum_w_up_b"),
                              _sum_parts(stacked(rx_up_c), name="sum_w_up_c")], axis=1).T[None]
    big = {}
    big["w_in"] = _adam(w_in[0], m_w_in[0], v_w_in[0], g_w_in, name="adam_w_in")
    big["w_up"] = _adam(w_up[0], m_w_up[0], v_w_up[0], g_w_up, name="adam_w_up")
    big["w_out"] = _adam(w_out[0], m_w_out[0], v_w_out[0], stacked(rx_out), name="adam_w_out")
    big["w_down"] = _adam(w_down[0], m_w_down[0], v_w_down[0], stacked(rx_down), name="adam_w_down")
    big["w_mod"] = _adam(w_mod[0], m_w_mod[0], v_w_mod[0], parts_mod, name="adam_w_mod")

    ccols, fcols = conv_w.shape[2], ffn_conv_w.shape[2]
    g_conv_w = lax.dynamic_slice_in_dim(g_cw_f, me * ccols, ccols, axis=1)[None]
    g_ffn_w = lax.dynamic_slice_in_dim(g_fw_f, me * fcols, fcols, axis=1)[None]
    names = ["c_ctx", "b_mod", "norm_mix_w", "q_norm_w", "k_norm_w", "sink_logit", "conv_w", "conv_b",
             "conv_norm_w", "conv_norm_b", "norm_ffn_w", "ffn_conv_w", "ffn_conv_b"]
    ws = [c_ctx, b_mod, norm_mix_w, q_norm_w, k_norm_w, sink_logit, conv_w, conv_b, conv_norm_w, conv_norm_b,
          norm_ffn_w, ffn_conv_w, ffn_conv_b]
    msm = [m_c_ctx, m_b_mod, m_norm_mix_w, m_q_norm_w, m_k_norm_w, m_sink_logit, m_conv_w, m_conv_b,
           m_conv_norm_w, m_conv_norm_b, m_norm_ffn_w, m_ffn_conv_w, m_ffn_conv_b]
    vsm = [v_c_ctx, v_b_mod, v_norm_mix_w, v_q_norm_w, v_k_norm_w, v_sink_logit, v_conv_w, v_conv_b,
           v_conv_norm_w, v_conv_norm_b, v_norm_ffn_w, v_ffn_conv_w, v_ffn_conv_b]
    gsm = [g_c_ctx, g_b_mod, g_nmix, g_qn, g_kn, g_sink, g_conv_w, g_cb, g_lw, g_lb, g_nffn, g_ffn_w, g_fb]
    deltas, new_ms, new_vs = _adam_many(ws, msm, vsm, gsm, name="adam_small")
    sm = {nm: vals for nm, vals in zip(names, zip(gsm, deltas, new_ms, new_vs))}

    def out4(nm):
        if nm in sm:
            return sm[nm]
        return tuple(t[None] for t in big[nm])

    order = ["c_ctx", "w_mod", "b_mod", "norm_mix_w", "w_in", "q_norm_w", "k_norm_w", "sink_logit", "conv_w",
             "conv_b", "conv_norm_w", "conv_norm_b", "w_out", "norm_ffn_w", "w_up", "ffn_conv_w", "ffn_conv_b",
             "w_down"]
    quads = [out4(nm) for nm in order]
    return (loss, grad_x[None], *[q[0] for q in quads], *[q[1] for q in quads],
            *[q[2] for q in quads], *[q[3] for q in quads])
```

```python
import math
from typing import Callable, NamedTuple

import jax
import jax.numpy as jnp
from jax import lax
from jax.experimental import pallas as pl
from jax.experimental.pallas import tpu as pltpu

F32 = jnp.float32
MXU_DT = jnp.bfloat16

D_MODEL = 1024
GRID_W = 64
HEAD_DIM = 64
N_Q_HEADS = 8
N_KV_HEADS = 2
GQA_GROUP = 4
WINDOW = 128
BLOCK = 128
Q_COLS = 512
KV_COLS = 128
QK_COLS = Q_COLS + KV_COLS
QKV_COLS = Q_COLS + 2 * KV_COLS
CONV_CH = 512
GLU_COLS = 2 * CONV_CH
IN_COLS = QKV_COLS + GLU_COLS
CONV_K = 31
CONV_PAD = 15
FFN_H = 2816
FFN_K = 3
ROPE_BASE = 10000.0
EPS = 1e-6
NEG_INF = -1e30
N_DEV = 8
HALO = 16
LANES = 128
ROW_TS = 512

ADAM_LR = 0.001
ADAM_B1 = 0.9
ADAM_B2 = 0.999
ADAM_EPS = 1e-08
ADAM_WD = 0.01
ADAM_STEP = 10

MESH = pl.DeviceIdType.MESH
VMEM_LIMIT = 56 << 20
MM_VMEM_BUDGET = 44 << 20
GLU_OFF = 1024
IN_PAD = GLU_OFF + GLU_COLS


def _cp(*dims):
    return pltpu.CompilerParams(dimension_semantics=dims or None, vmem_limit_bytes=VMEM_LIMIT)


def _row(ts, w, col=0):
    return pl.BlockSpec((ts, w), lambda i: (i, col))


def _vec(w):
    return pl.BlockSpec((1, w), lambda i: (0, 0))


def _colsum(v):
    return jnp.sum(v, axis=0, keepdims=True)


def _sigmoid(v):
    return 0.5 * jnp.tanh(0.5 * v) + 0.5


def _mm(a, b, m, n, k, *, ta=False, tb=False, n0=0, k0=0, add=None, out_dtype=F32, into=None, m0=0,
        out_rows=None, a2=None, carry=None, name):
    has_add, has_a2 = add is not None, a2 is not None
    assert not (has_a2 and (ta or tb))
    if into is not None:
        out_dtype = into.dtype
    sa, sb, so = a.dtype.itemsize, b.dtype.itemsize, jnp.dtype(out_dtype).itemsize
    sadd = add.dtype.itemsize if has_add else 0
    na = 2 if has_a2 else 1

    def fits(tm, tn):
        return 2 * (na * k * (tm * sa + tn * sb) + tm * tn * (so + sadd)) <= MM_VMEM_BUDGET

    tms = [m] if m <= 1024 else [t for t in (1024, 1408, 768, 512, 256, 128) if m % t == 0]
    tns = [t for t in ((1024, 512, 256, 128) if ta else (1408, 512, 256, 128)) if n % t == 0 and n0 % t == 0]
    tm, tn = next((tm, tn) for tm in tms for tn in tns if fits(tm, tn))
    a_spec = (pl.BlockSpec((k, tm), lambda i, j: (0, i)) if ta else pl.BlockSpec((tm, k), lambda i, j: (i, 0)))
    nb0 = n0 // tn
    if tb:
        assert k0 == 0
        b_spec = pl.BlockSpec((tn, k), lambda i, j: (j + nb0, 0))
    else:
        assert k0 % (na * k) == 0, (k0, k)
        kb0 = k0 // (na * k)
        b_spec = pl.BlockSpec((na * k, tn), lambda i, j: (kb0, j + nb0))
    assert m0 % tm == 0, (m0, tm)
    mb0 = m0 // tm
    o_spec = pl.BlockSpec((tm, tn), lambda i, j: (i + mb0, j))
    dims = (((0 if ta else 1,), (1 if tb else 0,)), ((), ()))

    def body(*refs):
        a_ref, b_ref, o_ref = refs[0], refs[na], refs[-1]
        if has_a2:
            res = (jnp.dot(a_ref[...].astype(MXU_DT), b_ref[0:k, :].astype(MXU_DT), preferred_element_type=F32)
                   + jnp.dot(refs[1][...].astype(MXU_DT), b_ref[k:2 * k, :].astype(MXU_DT),
                             preferred_element_type=F32))
        else:
            res = lax.dot_general(a_ref[...].astype(MXU_DT), b_ref[...].astype(MXU_DT), dims,
                                  preferred_element_type=F32)
        if has_add:
            res = res + refs[na + 1][...].astype(F32)
        o_ref[...] = res.astype(o_ref.dtype)

    ins = [a] + ([a2] if has_a2 else []) + [b] + ([add] if has_add else []) + ([into] if into is not None else [])
    specs = ([a_spec] * na + [b_spec] + ([pl.BlockSpec((tm, tn), lambda i, j: (i, j))] if has_add else [])
             + ([pl.BlockSpec(memory_space=pl.ANY)] if into is not None else []))
    out_shape = (jax.ShapeDtypeStruct(into.shape, into.dtype) if into is not None
                 else jax.ShapeDtypeStruct((out_rows or m, n), out_dtype))
    (out,), carried = _grid_call(
        body, ins, carry, name=name, grid=(m // tm, n // tn), in_specs=specs, out_specs=[o_spec],
        out_shape=[out_shape], input_output_aliases={len(ins) - 1: 0} if into is not None else None,
        dims=("parallel", "parallel"))
    return out if carry is None else (out, carried)


def _rms_stats(xv):
    r = lax.rsqrt(jnp.mean(xv * xv, axis=-1, keepdims=True) + EPS)
    return r, xv * r


def _prenorm(x, nw, sc, sh, *, carry=None, name):
    rows, d = x.shape
    ts = min(rows, ROW_TS)

    def body(x_ref, nw_ref, sc_ref, sh_ref, h_ref):
        _, xn = _rms_stats(x_ref[...])
        h_ref[...] = ((xn * nw_ref[...]) * (1.0 + sc_ref[...]) + sh_ref[...]).astype(h_ref.dtype)

    (h,), carried = _grid_call(
        body, [x, nw, sc, sh], carry, name=name, grid=(rows // ts,),
        in_specs=[_row(ts, d), _vec(d), _vec(d), _vec(d)], out_specs=[_row(ts, d)],
        out_shape=[jax.ShapeDtypeStruct((rows, d), MXU_DT)], dims=("parallel",))
    return h, carried


def _out_proj_prenorm(attn_o, u3, w_out, x, g1, nw, sc, sh, *, name):
    rows, d = x.shape
    ka, ku = attn_o.shape[1], u3.shape[1]
    tm = min(rows, ROW_TS)

    def body(a_ref, u_ref, w_ref, x_ref, g_ref, nw_ref, sc_ref, sh_ref, mix_ref, x1_ref, h_ref):
        mix = (jnp.dot(a_ref[...].astype(MXU_DT), w_ref[0:ka, :].astype(MXU_DT), preferred_element_type=F32)
               + jnp.dot(u_ref[...].astype(MXU_DT), w_ref[ka:ka + ku, :].astype(MXU_DT),
                         preferred_element_type=F32))
        mix_ref[...] = mix
        x1 = x_ref[...] + g_ref[...] * mix
        x1_ref[...] = x1
        _, xn = _rms_stats(x1)
        h_ref[...] = ((xn * nw_ref[...]) * (1.0 + sc_ref[...]) + sh_ref[...]).astype(h_ref.dtype)

    return pl.pallas_call(
        body, name=name, grid=(rows // tm,),
        in_specs=[_row(tm, ka), _row(tm, ku), pl.BlockSpec((ka + ku, d), lambda i: (0, 0)), _row(tm, d),
                  _vec(d), _vec(d), _vec(d), _vec(d)],
        out_specs=[_row(tm, d), _row(tm, d), _row(tm, d)],
        out_shape=[jax.ShapeDtypeStruct((rows, d), F32), jax.ShapeDtypeStruct((rows, d), F32),
                   jax.ShapeDtypeStruct((rows, d), MXU_DT)],
        compiler_params=_cp("parallel"),
    )(attn_o, u3, w_out, x, g1, nw, sc, sh)


def _ffn_down_loss(act, w_down, x1, g2, target, *, name):
    rows, d = x1.shape
    k = act.shape[1]
    tm, tn = min(rows, 1024), 512

    def body(a_ref, b_ref, x1_ref, g_ref, t_ref, loss_ref, dy_ref, dffn_ref, dg_ref):
        j, i = pl.program_id(0), pl.program_id(1)
        f = jnp.dot(a_ref[...].astype(MXU_DT), b_ref[...].astype(MXU_DT), preferred_element_type=F32)
        e = x1_ref[...] + g_ref[...] * f - t_ref[...]
        part = (0.5 / d) * jnp.sum(jnp.sum(e * e, axis=-1, keepdims=True), axis=0, keepdims=True)
        dy = e * (1.0 / d)
        dy_ref[...] = dy
        dffn_ref[...] = (dy * g_ref[...]).astype(dffn_ref.dtype)

        @pl.when((i == 0) & (j == 0))
        def _():
            loss_ref[...] = jnp.zeros_like(loss_ref)

        @pl.when(i == 0)
        def _():
            dg_ref[...] = jnp.zeros_like(dg_ref)

        loss_ref[...] += jnp.broadcast_to(part, loss_ref.shape)
        dg_ref[...] += _colsum(dy * f)

    tile = pl.BlockSpec((tm, tn), lambda j, i: (i, j))
    vec = pl.BlockSpec((1, tn), lambda j, i: (0, j))
    return pl.pallas_call(
        body, name=name, grid=(d // tn, rows // tm),
        in_specs=[pl.BlockSpec((tm, k), lambda j, i: (i, 0)), pl.BlockSpec((k, tn), lambda j, i: (0, j)),
                  tile, vec, tile],
        out_specs=[pl.BlockSpec((8, LANES), lambda j, i: (0, 0)), tile, tile, vec],
        out_shape=[jax.ShapeDtypeStruct((8, LANES), F32), jax.ShapeDtypeStruct((rows, d), F32),
                   jax.ShapeDtypeStruct((rows, d), MXU_DT), jax.ShapeDtypeStruct((1, d), F32)],
        compiler_params=_cp("arbitrary", "arbitrary"),
    )(act, w_down, x1, g2, target)


def _norm_bwd(dh, xin, nw, sc, *, res=None, gate=None, want_dx=True, carry=None, name):
    rows, d = xin.shape
    ts = min(rows, ROW_TS)
    has_res, has_gate = res is not None, gate is not None

    def body(*refs):
        it = iter(refs)
        dh_ref, x_ref, nw_ref, sc_ref = next(it), next(it), next(it), next(it)
        res_ref = next(it) if has_res else None
        gated_ref, g_ref = (next(it), next(it)) if has_gate else (None, None)
        dx_ref = next(it) if want_dx else None
        dgx_ref = next(it) if has_gate else None
        dsh_ref, dsc_ref, dnw_ref = next(it), next(it), next(it)
        dg_ref = next(it) if has_gate else None
        i = pl.program_id(0)
        dhv = dh_ref[...].astype(F32)
        r, xn = _rms_stats(x_ref[...])
        dn = dhv * (1.0 + sc_ref[...])

        @pl.when(i == 0)
        def _():
            dsh_ref[...] = jnp.zeros_like(dsh_ref)
            dsc_ref[...] = jnp.zeros_like(dsc_ref)
            dnw_ref[...] = jnp.zeros_like(dnw_ref)
            if has_gate:
                dg_ref[...] = jnp.zeros_like(dg_ref)

        dsh_ref[...] += _colsum(dhv)
        dsc_ref[...] += _colsum(dhv * (xn * nw_ref[...]))
        dnw_ref[...] += _colsum(dn * xn)
        if want_dx:
            dxn = dn * nw_ref[...]
            dx = r * (dxn - xn * jnp.mean(dxn * xn, axis=-1, keepdims=True))
            if has_res:
                dx = dx + res_ref[...]
            dx_ref[...] = dx
            if has_gate:
                dgx_ref[...] = (dx * g_ref[...]).astype(dgx_ref.dtype)
                dg_ref[...] += _colsum(dx * gated_ref[...])

    ins = [dh, xin, nw, sc] + ([res] if has_res else []) + (list(gate) if has_gate else [])
    in_specs = ([_row(ts, d), _row(ts, d), _vec(d), _vec(d)] + ([_row(ts, d)] if has_res else [])
                + ([_row(ts, d), _vec(d)] if has_gate else []))
    out_specs, out_shape = [], []
    if want_dx:
        out_specs.append(_row(ts, d)); out_shape.append(jax.ShapeDtypeStruct((rows, d), F32))
    if has_gate:
        out_specs.append(_row(ts, d)); out_shape.append(jax.ShapeDtypeStruct((rows, d), MXU_DT))
    for _ in range(3 + int(has_gate)):
        out_specs.append(_vec(d)); out_shape.append(jax.ShapeDtypeStruct((1, d), F32))
    return _grid_call(body, ins, carry, name=name, grid=(rows // ts,), in_specs=in_specs, out_specs=out_specs,
                      out_shape=out_shape, dims=("arbitrary",))


def _group_sum(v, g):
    hi = v.astype(MXU_DT)
    lo = (v - hi.astype(F32)).astype(MXU_DT)
    return (jnp.dot(hi, g, preferred_element_type=F32) + jnp.dot(lo, g, preferred_element_type=F32))


def _rot(v):
    lane = lax.broadcasted_iota(jnp.int32, v.shape, 1)
    first = (lane & 31) < 16
    return jnp.where(first, -pltpu.roll(v, LANES - 16, 1), pltpu.roll(v, 16, 1))


def _head_group_matrix():
    r = jnp.arange(LANES) // HEAD_DIM
    return (r[:, None] == r[None, :]).astype(MXU_DT)


def _qk_prep(xin, width, cos, sin, w, *, name):
    rows = xin.shape[0]
    ts = min(rows, ROW_TS)
    nch = width // LANES

    def body(x_ref, cos_ref, sin_ref, w_ref, g_ref, o_ref):
        cs, sn, g = cos_ref[...], sin_ref[...], g_ref[...]
        for ch in range(nch):
            sl = slice(ch * LANES, (ch + 1) * LANES)
            xv = x_ref[:, sl]
            r = lax.rsqrt(_group_sum(xv * xv, g) * (1.0 / HEAD_DIM) + EPS)
            yw = (xv * r) * w_ref[:, sl]
            o_ref[:, sl] = (yw * cs + _rot(yw) * sn).astype(o_ref.dtype)

    return pl.pallas_call(
        body, name=name, grid=(rows // ts,),
        in_specs=[_row(ts, width), _row(ts, LANES), _row(ts, LANES), _vec(width),
                  pl.BlockSpec((LANES, LANES), lambda i: (0, 0))],
        out_specs=_row(ts, width),
        out_shape=jax.ShapeDtypeStruct((rows, width), MXU_DT), compiler_params=_cp("parallel"),
    )(xin, cos, sin, w, _head_group_matrix())


def _qk_prep_bwd(xin, douts, width, cos, sin, w, *, tail=None, name):
    rows = xin.shape[0]
    ts = min(rows, ROW_TS)
    nch = width // LANES
    has_tail = tail is not None
    nd = len(douts)
    assert sum(v.shape[1] for v in douts) == width
    src = [(k, c) for k, v in enumerate(douts) for c in range(v.shape[1] // LANES)]

    def body(*refs):
        x_ref, d_refs = refs[0], refs[1:1 + nd]
        t_ref = refs[1 + nd] if has_tail else None
        cos_ref, sin_ref, w_ref, g_ref, dx_ref, dw_ref = refs[1 + nd + int(has_tail):]
        i = pl.program_id(0)
        cs, sn, g = cos_ref[...], sin_ref[...], g_ref[...]

        @pl.when(i == 0)
        def _():
            dw_ref[...] = jnp.zeros_like(dw_ref)

        if has_tail:
            dx_ref[:, width:width + LANES] = t_ref[...].astype(dx_ref.dtype)
            dx_ref[:, width + LANES:] = jnp.zeros((ts, GLU_OFF - width - LANES), dx_ref.dtype)

        for ch in range(nch):
            sl = slice(ch * LANES, (ch + 1) * LANES)
            xv = x_ref[:, sl]
            dv = d_refs[src[ch][0]][:, src[ch][1] * LANES:(src[ch][1] + 1) * LANES].astype(F32)
            r = lax.rsqrt(_group_sum(xv * xv, g) * (1.0 / HEAD_DIM) + EPS)
            n = xv * r
            dyw = dv * cs - _rot(dv * sn)
            dw_ref[:, sl] += _colsum(dyw * n)
            dn = dyw * w_ref[:, sl]
            gm = _group_sum(dn * n, g) * (1.0 / HEAD_DIM)
            dx_ref[:, sl] = (r * (dn - n * gm)).astype(dx_ref.dtype)

    ins = [xin] + list(douts) + ([tail] if has_tail else []) + [cos, sin, w, _head_group_matrix()]
    in_specs = ([_row(ts, width)] + [_row(ts, v.shape[1]) for v in douts] + ([_row(ts, LANES)] if has_tail else [])
                + [_row(ts, LANES), _row(ts, LANES), _vec(width), pl.BlockSpec((LANES, LANES), lambda i: (0, 0))])
    out_w, arr_w = (GLU_OFF, IN_PAD) if has_tail else (width, width)
    return pl.pallas_call(
        body, name=name, grid=(rows // ts,), in_specs=in_specs,
        out_specs=[_row(ts, out_w), _vec(width)],
        out_shape=[jax.ShapeDtypeStruct((rows, arr_w), MXU_DT), jax.ShapeDtypeStruct((1, width), F32)],
        compiler_params=_cp("arbitrary"),
    )(*ins)


GB = GQA_GROUP * BLOCK
WIN = 3 * BLOCK
ATT_STEP = 8


def _win_start(i, seq):
    return pl.multiple_of(jnp.clip((i - 1) * BLOCK, 0, seq - WIN), BLOCK)


def _attn_mask(i, start):
    qpos = i * BLOCK + (lax.broadcasted_iota(jnp.int32, (GB, WIN), 0) & (BLOCK - 1))
    kpos = start + lax.broadcasted_iota(jnp.int32, (GB, WIN), 1)
    return jnp.abs(qpos - kpos) <= WINDOW


def _sink_col(sink_ref, kv):
    return jnp.concatenate(
        [jnp.full((BLOCK, 1), sink_ref[0, kv * GQA_GROUP + g], F32) for g in range(GQA_GROUP)], axis=0)


def _stack_slots(ref, rows):
    return jnp.concatenate([ref[rows, g * LANES:(g + 1) * LANES] for g in range(GQA_GROUP)], axis=0)


def _kv_lanes(kv):
    lane = lax.broadcasted_iota(jnp.int32, (1, LANES), 1)
    return (lane < HEAD_DIM) if kv == 0 else (lane >= HEAD_DIM)


def _keep(mask, v):
    return jnp.where(mask, v, jnp.zeros_like(v))


_NT = (((1,), (1,)), ((), ()))
_TN = (((0,), (0,)), ((), ()))


def _attn_specs(seq, n_ctx):
    qs = pl.BlockSpec((ATT_STEP * BLOCK, Q_COLS), lambda i: (i, 0))
    ks = pl.BlockSpec((seq, KV_COLS), lambda i: (0, Q_COLS // KV_COLS))
    vs = pl.BlockSpec((seq, KV_COLS), lambda i: (0, QK_COLS // KV_COLS))
    kcs = pl.BlockSpec((n_ctx, KV_COLS), lambda i: (0, 0))
    vcs = pl.BlockSpec((n_ctx, KV_COLS), lambda i: (0, 1))
    ls = pl.BlockSpec((ATT_STEP, N_KV_HEADS * GB, 1), lambda i: (i, 0, 0))
    return qs, ks, vs, kcs, vcs, ls


def _attn_fwd(sink, qk, proj, kc, kv_ctx, *, carry=None, name):
    seq, n_ctx = qk.shape[0], kc.shape[0]
    scale = 1.0 / math.sqrt(HEAD_DIM)

    def one_block(blk, sub, sink_ref, q_ref, k_ref, v_ref, kcw, vcw, o_ref, lse_ref):
        rows = slice(sub * BLOCK, (sub + 1) * BLOCK)
        start = _win_start(blk, seq)
        valid = _attn_mask(blk, start)
        qs = _stack_slots(q_ref, rows)
        kw, vw = k_ref[pl.ds(start, WIN), :], v_ref[pl.ds(start, WIN), :].astype(MXU_DT)
        o_all = jnp.zeros((GB, LANES), F32)
        for kv in range(N_KV_HEADS):
            mine = _kv_lanes(kv)
            s_loc = lax.dot_general(qs, _keep(mine, kw), _NT, preferred_element_type=F32) * scale
            s_loc = jnp.where(valid, s_loc, NEG_INF)
            s_ctx = lax.dot_general(qs, _keep(mine, kcw), _NT, preferred_element_type=F32) * scale
            sk = _sink_col(sink_ref, kv)
            m = jnp.maximum(jnp.maximum(jnp.max(s_loc, axis=-1, keepdims=True),
                                        jnp.max(s_ctx, axis=-1, keepdims=True)), sk)
            p_loc = jnp.exp(s_loc - m)
            p_ctx = jnp.exp(s_ctx - m)
            l = (jnp.sum(p_loc, axis=-1, keepdims=True) + jnp.sum(p_ctx, axis=-1, keepdims=True)
                 + jnp.exp(sk - m))
            o_all = o_all + (jnp.dot(p_loc.astype(MXU_DT), _keep(mine, vw), preferred_element_type=F32)
                             + jnp.dot(p_ctx.astype(MXU_DT), _keep(mine, vcw), preferred_element_type=F32)) / l
            lse_ref[sub, kv * GB:(kv + 1) * GB, :] = m + jnp.log(l)
        for g in range(GQA_GROUP):
            o_ref[rows, g * LANES:(g + 1) * LANES] = o_all[g * BLOCK:(g + 1) * BLOCK].astype(o_ref.dtype)

    def body(sink_ref, q_ref, k_ref, v_ref, kc_ref, vc_ref, o_ref, lse_ref):
        i = pl.program_id(0)
        kcw, vcw = kc_ref[...], vc_ref[...].astype(MXU_DT)
        for sub in range(ATT_STEP):
            one_block(i * ATT_STEP + sub, sub, sink_ref, q_ref, k_ref, v_ref, kcw, vcw, o_ref, lse_ref)

    qs, ks, vs, kcs, vcs, ls = _attn_specs(seq, n_ctx)
    return _grid_call(
        body, [sink, qk, qk, proj, kc, kv_ctx], carry, name=name, grid=(seq // (ATT_STEP * BLOCK),),
        in_specs=[pl.BlockSpec(memory_space=pltpu.SMEM), qs, ks, vs, kcs, vcs],
        out_specs=[qs, ls],
        out_shape=[jax.ShapeDtypeStruct((seq, Q_COLS), MXU_DT),
                   jax.ShapeDtypeStruct((seq // BLOCK, N_KV_HEADS * GB, 1), F32)],
        dims=("parallel",))


def _attn_bwd(sink, qk, proj, kc, kv_ctx, o, lse, do, *, carry=None, name):
    seq, n_ctx = qk.shape[0], kc.shape[0]
    scale = 1.0 / math.sqrt(HEAD_DIM)

    def body(sink_ref, q_ref, k_ref, v_ref, kc_ref, vc_ref, o_ref, lse_ref, do_ref,
             dq_ref, dk_ref, dv_ref, dkc_ref, dvc_ref, ds_ref):
        i = pl.program_id(0)

        @pl.when(i == 0)
        def _():
            dk_ref[...] = jnp.zeros_like(dk_ref)
            dv_ref[...] = jnp.zeros_like(dv_ref)
            dkc_ref[...] = jnp.zeros_like(dkc_ref)
            dvc_ref[...] = jnp.zeros_like(dvc_ref)
            ds_ref[...] = jnp.zeros_like(ds_ref)

        kcw, vcw = kc_ref[...], vc_ref[...].astype(MXU_DT)
        dkc, dvc = jnp.zeros((n_ctx, LANES), F32), jnp.zeros((n_ctx, LANES), F32)
        for sub in range(ATT_STEP):
            dkc_s, dvc_s = one_block(i * ATT_STEP + sub, sub, sink_ref, q_ref, k_ref, v_ref, kcw, vcw, o_ref,
                                     lse_ref, do_ref, dq_ref, dk_ref, dv_ref, ds_ref)
            dkc, dvc = dkc + dkc_s, dvc + dvc_s
        dkc_ref[...] += dkc
        dvc_ref[...] += dvc

    def one_block(blk, sub, sink_ref, q_ref, k_ref, v_ref, kcw, vcw, o_ref, lse_ref, do_ref,
                  dq_ref, dk_ref, dv_ref, ds_ref):
        qrows = slice(sub * BLOCK, (sub + 1) * BLOCK)
        start = _win_start(blk, seq)
        valid = _attn_mask(blk, start)
        win = pl.ds(start, WIN)
        qs, dos = _stack_slots(q_ref, qrows), _stack_slots(do_ref, qrows)
        do_o = dos.astype(F32) * _stack_slots(o_ref, qrows).astype(F32)
        kw, vw = k_ref[win, :], v_ref[win, :].astype(MXU_DT)
        dq = jnp.zeros((GB, LANES), F32)
        dk, dv = jnp.zeros((WIN, LANES), F32), jnp.zeros((WIN, LANES), F32)
        dkc, dvc = jnp.zeros((n_ctx, LANES), F32), jnp.zeros((n_ctx, LANES), F32)
        for kv in range(N_KV_HEADS):
            mine = _kv_lanes(kv)
            rows = slice(kv * GB, (kv + 1) * GB)
            lse_s = lse_ref[sub, rows, :]
            delta = jnp.sum(_keep(mine, do_o), axis=-1, keepdims=True)
            kz, vz, kcz, vcz = _keep(mine, kw), _keep(mine, vw), _keep(mine, kcw), _keep(mine, vcw)
            s_loc = lax.dot_general(qs, kz, _NT, preferred_element_type=F32) * scale
            s_loc = jnp.where(valid, s_loc, NEG_INF)
            s_ctx = lax.dot_general(qs, kcz, _NT, preferred_element_type=F32) * scale
            p_loc = jnp.exp(s_loc - lse_s)
            p_ctx = jnp.exp(s_ctx - lse_s)
            p_sink = jnp.exp(_sink_col(sink_ref, kv) - lse_s)
            dp_loc = lax.dot_general(dos, vz, _NT, preferred_element_type=F32)
            dp_ctx = lax.dot_general(dos, vcz, _NT, preferred_element_type=F32)
            ds_loc = (p_loc * (dp_loc - delta) * scale).astype(MXU_DT)
            ds_ctx = (p_ctx * (dp_ctx - delta) * scale).astype(MXU_DT)
            dq = dq + (jnp.dot(ds_loc, kz, preferred_element_type=F32)
                       + jnp.dot(ds_ctx, kcz, preferred_element_type=F32))
            dk = dk + _keep(mine, lax.dot_general(ds_loc, qs, _TN, preferred_element_type=F32))
            dv = dv + _keep(mine, lax.dot_general(p_loc.astype(MXU_DT), dos, _TN, preferred_element_type=F32))
            dkc = dkc + _keep(mine, lax.dot_general(ds_ctx, qs, _TN, preferred_element_type=F32))
            dvc = dvc + _keep(mine, lax.dot_general(p_ctx.astype(MXU_DT), dos, _TN, preferred_element_type=F32))
            ds_ref[rows, :] += -(p_sink * delta)
        for g in range(GQA_GROUP):
            dq_ref[qrows, g * LANES:(g + 1) * LANES] = dq[g * BLOCK:(g + 1) * BLOCK]
        dk_ref[win, :] += dk
        dv_ref[win, :] += dv
        return dkc, dvc

    qs, ks, vs, kcs, vcs, ls = _attn_specs(seq, n_ctx)
    whole = lambda r, c: pl.BlockSpec((r, c), lambda i: (0, 0))
    return _grid_call(
        body, [sink, qk, qk, proj, kc, kv_ctx, o, lse, do], carry, name=name, grid=(seq // (ATT_STEP * BLOCK),),
        in_specs=[pl.BlockSpec(memory_space=pltpu.SMEM), qs, ks, vs, kcs, vcs, qs, ls, qs],
        out_specs=[qs, whole(seq, KV_COLS), whole(seq, KV_COLS), whole(n_ctx, KV_COLS), whole(n_ctx, KV_COLS),
                   whole(N_KV_HEADS * GB, 1)],
        out_shape=[jax.ShapeDtypeStruct((seq, Q_COLS), F32), jax.ShapeDtypeStruct((seq, KV_COLS), F32),
                   jax.ShapeDtypeStruct((seq, KV_COLS), F32), jax.ShapeDtypeStruct((n_ctx, KV_COLS), F32),
                   jax.ShapeDtypeStruct((n_ctx, KV_COLS), F32), jax.ShapeDtypeStruct((N_KV_HEADS * GB, 1), F32)],
        dims=("arbitrary",))


def _halo_specs(ts, w, rows, col=0):
    per = ts // HALO
    last = rows // HALO - 1
    return [pl.BlockSpec((HALO, w), lambda i: (jnp.maximum(i * per - 1, 0), col)),
            pl.BlockSpec((ts, w), lambda i: (i, col)),
            pl.BlockSpec((HALO, w), lambda i: (jnp.minimum((i + 1) * per, last), col))]


def _glu(v):
    return v[:, :CONV_CH] * _sigmoid(v[:, CONV_CH:])


def _ln_stats(u):
    mu = jnp.mean(u, axis=-1, keepdims=True)
    xc = u - mu
    rstd = lax.rsqrt(jnp.mean(xc * xc, axis=-1, keepdims=True) + EPS)
    return xc * rstd, rstd


CONV_BWD_TS = 128


def _phases(ext_ref, ph_ref):
    n = ph_ref.shape[1]
    for b in range(1, 8):
        ph_ref[b - 1] = ext_ref[b:b + n, :]


def _window(ext_ref, ph_ref, o, n, cs):
    a, b = divmod(o, 8)
    src = ext_ref if b == 0 else ph_ref.at[b - 1]
    return src[8 * a:8 * a + n, cs]


def _conv_fwd(glu, col, cw, cb, lw, lb, *, carry=None, name):
    rows = glu.shape[0]
    ts = min(rows, 256)
    nt = rows // ts

    te = ts + 2 * HALO

    def body(gp_ref, g_ref, gn_ref, cw_ref, cb_ref, lw_ref, lb_ref, u3_ref, u1_ref, ext_ref, ph_ref):
        i = pl.program_id(0)
        ext_ref[0:HALO, :] = jnp.where(i > 0, _glu(gp_ref[...]), 0.0)
        ext_ref[HALO:HALO + ts, :] = _glu(g_ref[...])
        ext_ref[HALO + ts:, :] = jnp.where(i < nt - 1, _glu(gn_ref[...]), 0.0)
        _phases(ext_ref, ph_ref)
        for c in range(CONV_CH // LANES):
            cs = slice(c * LANES, (c + 1) * LANES)
            acc = jnp.broadcast_to(cb_ref[:, cs], (ts, LANES))
            for j in range(CONV_K):
                acc = acc + cw_ref[j:j + 1, cs] * _window(ext_ref, ph_ref, HALO - CONV_PAD + j, ts, cs)
            u1_ref[:, cs] = acc
        xh, _ = _ln_stats(u1_ref[...])
        u2 = xh * lw_ref[...] + lb_ref[...]
        u3_ref[...] = (u2 * _sigmoid(u2)).astype(u3_ref.dtype)

    full = lambda shape: pl.BlockSpec(shape, lambda i: (0,) * len(shape))
    return _grid_call(
        body, [glu, glu, glu, cw, cb, lw, lb], carry, name=name, grid=(nt,),
        in_specs=_halo_specs(ts, GLU_COLS, rows, col) + [full((CONV_K, CONV_CH))] + [_vec(CONV_CH)] * 3,
        out_specs=[_row(ts, CONV_CH), _row(ts, CONV_CH)],
        out_shape=[jax.ShapeDtypeStruct((rows, CONV_CH), MXU_DT), jax.ShapeDtypeStruct((rows, CONV_CH), F32)],
        scratch_shapes=[pltpu.VMEM((te, CONV_CH), F32), pltpu.VMEM((7, te - 8, CONV_CH), F32)],
        dims=("parallel",))


def _conv_bwd(glu, col, u1, du3, dproj, cw, lw, lb, *, carry=None, name):
    rows = glu.shape[0]
    ts = min(rows, CONV_BWD_TS)
    nt = rows // ts
    te = ts + 2 * HALO

    def du1_of(u1v, du3v, lw_v, lb_v):
        xh, rstd = _ln_stats(u1v)
        u2 = xh * lw_v + lb_v
        sg = _sigmoid(u2)
        du2 = du3v * (sg * (1.0 + u2 * (1.0 - sg)))
        dxh = du2 * lw_v
        du1 = rstd * (dxh - jnp.mean(dxh, axis=-1, keepdims=True)
                      - xh * jnp.mean(dxh * xh, axis=-1, keepdims=True))
        return du1, du2, xh

    half = ts // 2

    def body(gp_ref, g_ref, gn_ref, up_ref, u_ref, un_ref, dp_ref, d_ref, dn_ref, cw_ref, lw_ref, lb_ref,
             _, dglu_ref, dcw_ref, dvec_ref, u0_ref, du1_ref, pu_ref, pd_ref, du0_ref):
        i = pl.program_id(0)
        lw_v, lb_v = lw_ref[...], lb_ref[...]

        @pl.when(i == 0)
        def _():
            dcw_ref[...] = jnp.zeros_like(dcw_ref)
            dvec_ref[...] = jnp.zeros_like(dvec_ref)

        gv = g_ref[...]
        u0_ref[0:HALO, :] = jnp.where(i > 0, _glu(gp_ref[...]), 0.0)
        u0_ref[HALO:HALO + ts, :] = _glu(gv)
        u0_ref[HALO + ts:, :] = jnp.where(i < nt - 1, _glu(gn_ref[...]), 0.0)
        d_prev, _, _ = du1_of(up_ref[...], dp_ref[...], lw_v, lb_v)
        d_main, du2, xh = du1_of(u_ref[...], d_ref[...], lw_v, lb_v)
        d_next, _, _ = du1_of(un_ref[...], dn_ref[...], lw_v, lb_v)
        du1_ref[0:HALO, :] = jnp.where(i > 0, d_prev, 0.0)
        du1_ref[HALO:HALO + ts, :] = d_main
        du1_ref[HALO + ts:, :] = jnp.where(i < nt - 1, d_next, 0.0)

        rid = lax.broadcasted_iota(jnp.int32, (8, CONV_CH), 0)
        dvec_ref[...] += (jnp.where(rid == 0, _colsum(d_main), 0.0)
                          + jnp.where(rid == 1, _colsum(du2 * xh), 0.0)
                          + jnp.where(rid == 2, _colsum(du2), 0.0))
        _phases(u0_ref, pu_ref)
        _phases(du1_ref, pd_ref)
        for c in range(CONV_CH // LANES):
            cs = slice(c * LANES, (c + 1) * LANES)
            for r0 in (0, half):
                dm = du1_ref[HALO + r0:HALO + r0 + half, cs]
                acc = jnp.zeros((half, LANES), F32)
                for j in range(CONV_K):
                    acc = acc + cw_ref[j:j + 1, cs] * _window(du1_ref, pd_ref, r0 + HALO + CONV_PAD - j, half, cs)
                    prod = dm * _window(u0_ref, pu_ref, r0 + HALO - CONV_PAD + j, half, cs)
                    dcw_ref[j, :, cs] += jnp.sum(prod.reshape(half // 8, 8, LANES), axis=0)
                du0_ref[r0:r0 + half, cs] = acc
        du0 = du0_ref[...]
        ga, sg = gv[:, :CONV_CH], _sigmoid(gv[:, CONV_CH:])
        dglu_ref[:, :CONV_CH] = (du0 * sg).astype(dglu_ref.dtype)
        dglu_ref[:, CONV_CH:] = (du0 * ga * sg * (1.0 - sg)).astype(dglu_ref.dtype)

    full = lambda shape: pl.BlockSpec(shape, lambda i: (0,) * len(shape))
    return _grid_call(
        body, [glu, glu, glu, u1, u1, u1, du3, du3, du3, cw, lw, lb, dproj], carry, name=name, grid=(nt,),
        in_specs=(_halo_specs(ts, GLU_COLS, rows, col) + _halo_specs(ts, CONV_CH, rows)
                  + _halo_specs(ts, CONV_CH, rows) + [full((CONV_K, CONV_CH)), _vec(CONV_CH), _vec(CONV_CH)]
                  + [pl.BlockSpec(memory_space=pl.ANY)]),
        out_specs=[_row(ts, GLU_COLS, col), full((CONV_K, 8, CONV_CH)), full((8, CONV_CH))],
        out_shape=[jax.ShapeDtypeStruct(dproj.shape, dproj.dtype),
                   jax.ShapeDtypeStruct((CONV_K, 8, CONV_CH), F32), jax.ShapeDtypeStruct((8, CONV_CH), F32)],
        scratch_shapes=[pltpu.VMEM((te, CONV_CH), F32), pltpu.VMEM((te, CONV_CH), F32),
                        pltpu.VMEM((7, te - 8, CONV_CH), F32), pltpu.VMEM((7, te - 8, CONV_CH), F32),
                        pltpu.VMEM((ts, CONV_CH), F32)],
        input_output_aliases={12: 0}, dims=("arbitrary",))


FFN_CW = 1408
FFN_NJ = FFN_H // FFN_CW


def _ffn_halo_specs(ts, rows, col_of, inner_rows):
    per = ts // HALO
    last = rows // HALO - 1
    if inner_rows:
        return [pl.BlockSpec((HALO, FFN_CW), lambda j, i: (jnp.maximum(i * per - 1, 0), col_of(j))),
                pl.BlockSpec((ts, FFN_CW), lambda j, i: (i, col_of(j))),
                pl.BlockSpec((HALO, FFN_CW), lambda j, i: (jnp.minimum((i + 1) * per, last), col_of(j)))]
    return [pl.BlockSpec((HALO, FFN_CW), lambda i, j: (jnp.maximum(i * per - 1, 0), col_of(j))),
            pl.BlockSpec((ts, FFN_CW), lambda i, j: (i, col_of(j))),
            pl.BlockSpec((HALO, FFN_CW), lambda i, j: (jnp.minimum((i + 1) * per, last), col_of(j)))]


def _ffn_ext(p_ref, m_ref, n_ref, sl, i, nt):
    return jnp.concatenate([jnp.where(i > 0, p_ref[:, sl].astype(F32), 0.0), m_ref[:, sl].astype(F32),
                            jnp.where(i < nt - 1, n_ref[:, sl].astype(F32), 0.0)], axis=0)


def _prev_row(v):
    return pltpu.roll(v, 1, 0)


def _next_row(v):
    return pltpu.roll(v, v.shape[0] - 1, 0)


def _ffn_act(up0, w3, b3, *, name):
    rows = up0.shape[0]
    ts = min(rows, 256)
    nt = rows // ts
    main = slice(HALO, HALO + ts)

    def body(gp, g, gn, vp, v, vn, wg, wv, bg, bv, a_ref, go_ref, vo_ref):
        i = pl.program_id(0)
        for ch in range(FFN_CW // LANES):
            sl = slice(ch * LANES, (ch + 1) * LANES)
            xg, xv = _ffn_ext(gp, g, gn, sl, i, nt), _ffn_ext(vp, v, vn, sl, i, nt)
            wgv, wvv = wg[:, sl], wv[:, sl]
            gate = (wgv[0:1] * _prev_row(xg) + wgv[1:2] * xg + wgv[2:3] * _next_row(xg))[main] + bg[:, sl]
            val = (wvv[0:1] * _prev_row(xv) + wvv[1:2] * xv + wvv[2:3] * _next_row(xv))[main] + bv[:, sl]
            a_ref[:, sl] = (gate * _sigmoid(gate) * val).astype(a_ref.dtype)
            go_ref[:, sl] = gate.astype(go_ref.dtype)
            vo_ref[:, sl] = val.astype(vo_ref.dtype)

    gcol, vcol = (lambda j: j), (lambda j: j + FFN_NJ)
    wspec = lambda col_of: pl.BlockSpec((FFN_K, FFN_CW), lambda i, j: (0, col_of(j)))
    bspec = lambda col_of: pl.BlockSpec((1, FFN_CW), lambda i, j: (0, col_of(j)))
    ospec = pl.BlockSpec((ts, FFN_CW), lambda i, j: (i, j))
    return pl.pallas_call(
        body, name=name, grid=(nt, FFN_NJ),
        in_specs=(_ffn_halo_specs(ts, rows, gcol, False) + _ffn_halo_specs(ts, rows, vcol, False)
                  + [wspec(gcol), wspec(vcol), bspec(gcol), bspec(vcol)]),
        out_specs=[ospec] * 3, out_shape=[jax.ShapeDtypeStruct((rows, FFN_H), MXU_DT)] * 3,
        compiler_params=_cp("parallel", "parallel"),
    )(up0, up0, up0, up0, up0, up0, w3, w3, b3, b3)


def _ffn_act_bwd(up0, gate_s, val_s, da, w3, *, carry=None, name):
    rows = up0.shape[0]
    ts = min(rows, 256)
    nt = rows // ts
    main = slice(HALO, HALO + ts)

    def body(gp, g, gn, vp, v, vn, sgp, sg, sgn, svp, sv, svn, ap, a, an, wg, wv,
             dg_ref, dv_ref, sg_ref, sv_ref):
        i = pl.program_id(1)

        @pl.when(i == 0)
        def _():
            sg_ref[...] = jnp.zeros_like(sg_ref)
            sv_ref[...] = jnp.zeros_like(sv_ref)

        rid = lax.broadcasted_iota(jnp.int32, (8, LANES), 0)
        for ch in range(FFN_CW // LANES):
            sl = slice(ch * LANES, (ch + 1) * LANES)
            xg, xv, da_e = _ffn_ext(gp, g, gn, sl, i, nt), _ffn_ext(vp, v, vn, sl, i, nt), _ffn_ext(ap, a, an, sl, i, nt)
            gate, val = _ffn_ext(sgp, sg, sgn, sl, i, nt), _ffn_ext(svp, sv, svn, sl, i, nt)
            wgv, wvv = wg[:, sl], wv[:, sl]
            xg_p, xg_n, xv_p, xv_n = _prev_row(xg), _next_row(xg), _prev_row(xv), _next_row(xv)
            sgm = _sigmoid(gate)
            eg = da_e * val * (sgm * (1.0 + gate * (1.0 - sgm)))
            ev = da_e * (gate * sgm)
            for e, taps, w, d_ref, s_ref in ((eg, (xg_p, xg, xg_n), wgv, dg_ref, sg_ref),
                                             (ev, (xv_p, xv, xv_n), wvv, dv_ref, sv_ref)):
                d0 = w[0:1] * _next_row(e) + w[1:2] * e + w[2:3] * _prev_row(e)
                d_ref[:, sl] = d0[main].astype(d_ref.dtype)
                dm = e[main]
                s_ref[:, sl] += (jnp.where(rid == 0, _colsum(dm * taps[0][main]), 0.0)
                                 + jnp.where(rid == 1, _colsum(dm * taps[1][main]), 0.0)
                                 + jnp.where(rid == 2, _colsum(dm * taps[2][main]), 0.0)
                                 + jnp.where(rid == 3, _colsum(dm), 0.0))

    gcol, vcol = (lambda j: j), (lambda j: j + FFN_NJ)
    wspec = lambda col_of: pl.BlockSpec((FFN_K, FFN_CW), lambda j, i: (0, col_of(j)))
    ospec = pl.BlockSpec((ts, FFN_CW), lambda j, i: (i, j))
    sspec = pl.BlockSpec((8, FFN_CW), lambda j, i: (0, j))
    return _grid_call(
        body, [up0] * 6 + [gate_s] * 3 + [val_s] * 3 + [da] * 3 + [w3, w3], carry, name=name, grid=(FFN_NJ, nt),
        in_specs=(_ffn_halo_specs(ts, rows, gcol, True) + _ffn_halo_specs(ts, rows, vcol, True)
                  + _ffn_halo_specs(ts, rows, gcol, True) * 3 + [wspec(gcol), wspec(vcol)]),
        out_specs=[ospec, ospec, sspec, sspec],
        out_shape=[jax.ShapeDtypeStruct((rows, FFN_H), MXU_DT), jax.ShapeDtypeStruct((rows, FFN_H), MXU_DT),
                   jax.ShapeDtypeStruct((8, FFN_H), F32), jax.ShapeDtypeStruct((8, FFN_H), F32)],
        dims=("parallel", "arbitrary"))


def _adam_math(w, g, m, v):
    m = ADAM_B1 * m + (1.0 - ADAM_B1) * g
    v = ADAM_B2 * v + (1.0 - ADAM_B2) * (g * g)
    m_hat = m / (1.0 - ADAM_B1 ** ADAM_STEP)
    v_hat = v / (1.0 - ADAM_B2 ** ADAM_STEP)
    delta = -ADAM_LR * (m_hat / (jnp.sqrt(v_hat) + ADAM_EPS) + ADAM_WD * w)
    return delta, m, v


ROW_TILE_BYTES = 8 << 20


def _row_tile(rows, row_bytes):
    tiles = [rows] + [rows // k for k in range(2, rows // 16 + 1) if rows % k == 0 and (rows // k) % 16 == 0]
    return next(t for t in tiles if t * row_bytes <= ROW_TILE_BYTES)


def _adam(w, m, v, parts, *, name):
    rows, cols = w.shape
    nparts = parts.shape[0]
    tr = _row_tile(rows, cols * (7 * 4 + nparts * parts.dtype.itemsize))

    def body(w_ref, m_ref, v_ref, p_ref, g_ref, d_ref, nm_ref, nv_ref):
        g = p_ref[0].astype(F32)
        for p in range(1, nparts):
            g = g + p_ref[p].astype(F32)
        g_ref[...] = g
        d_ref[...], nm_ref[...], nv_ref[...] = _adam_math(w_ref[...], g, m_ref[...], v_ref[...])

    spec = _row(tr, cols)
    return pl.pallas_call(
        body, name=name, grid=(rows // tr,),
        in_specs=[spec, spec, spec, pl.BlockSpec((nparts, tr, cols), lambda i: (0, i, 0))],
        out_specs=[spec] * 4, out_shape=[jax.ShapeDtypeStruct((rows, cols), F32)] * 4,
        compiler_params=_cp("parallel"),
    )(w, m, v, parts)


def _adam_many(ws, ms, vs, gs, *, name):
    n = len(ws)

    def body(*refs):
        ins, outs = refs[:4 * n], refs[4 * n:]
        for k in range(n):
            delta, new_m, new_v = _adam_math(ins[k][...], ins[3 * n + k][...], ins[n + k][...], ins[2 * n + k][...])
            outs[k][...], outs[n + k][...], outs[2 * n + k][...] = delta, new_m, new_v

    vm = pl.BlockSpec(memory_space=pltpu.VMEM)
    res = pl.pallas_call(
        body, name=name, in_specs=[vm] * (4 * n), out_specs=[vm] * (3 * n),
        out_shape=[jax.ShapeDtypeStruct(w.shape, F32) for w in ws] * 3,
        compiler_params=pltpu.CompilerParams(vmem_limit_bytes=VMEM_LIMIT),
    )(*ws, *ms, *vs, *gs)
    return res[:n], res[n:2 * n], res[2 * n:]


def _sum_parts(parts, *, name):
    nparts, rows, cols = parts.shape
    tr = _row_tile(rows, cols * (4 + nparts * parts.dtype.itemsize))

    def body(p_ref, o_ref):
        g = p_ref[0].astype(F32)
        for p in range(1, nparts):
            g = g + p_ref[p].astype(F32)
        o_ref[...] = g

    return pl.pallas_call(
        body, name=name, grid=(rows // tr,),
        in_specs=[pl.BlockSpec((nparts, tr, cols), lambda i: (0, i, 0))], out_specs=_row(tr, cols),
        out_shape=jax.ShapeDtypeStruct((rows, cols), F32), compiler_params=_cp("parallel"),
    )(parts)


def _my_place():
    return lax.axis_index("x"), lax.axis_index("y"), lax.axis_index("c")


def _dev_index(p):
    return 4 * p[0] + 2 * p[1] + p[2]


def _all_gather(xs, *, name):
    return _run_comm(_gather_plan(xs), pltpu.VMEM, name)


class _Comm(NamedTuple):
    ins: list
    outs: list
    n_remote: int
    n_local: int
    start: Callable
    finish: Callable


def _join_plans(*plans):
    def split(in_refs, out_refs, send_sems, recv_sems, local_sems):
        i = o = r = l = 0
        for p in plans:
            ni, no = len(p.ins), len(p.outs)
            yield p, (in_refs[i:i + ni], out_refs[o:o + no], send_sems.at[pl.ds(r, p.n_remote)],
                      recv_sems.at[pl.ds(r, p.n_remote)], local_sems.at[pl.ds(l, p.n_local)])
            i, o, r, l = i + ni, o + no, r + p.n_remote, l + p.n_local

    def start(*refs):
        for p, part in split(*refs):
            p.start(*part)

    def finish(*refs):
        for p, part in split(*refs):
            p.finish(*part)

    return _Comm([v for p in plans for v in p.ins], [v for p in plans for v in p.outs],
                 sum(p.n_remote for p in plans), sum(p.n_local for p in plans), start, finish)


def _comm_scratch(plan):
    return [pltpu.SemaphoreType.DMA((plan.n_remote,)), pltpu.SemaphoreType.DMA((plan.n_remote,)),
            pltpu.SemaphoreType.DMA((plan.n_local,))]


def _run_comm(plan, space, name):
    n_in, n_out = len(plan.ins), len(plan.outs)

    def body(*refs):
        args = (refs[:n_in], refs[n_in:n_in + n_out], *refs[n_in + n_out:])
        plan.start(*args)
        plan.finish(*args)

    return pl.pallas_call(
        body, name=name, out_shape=plan.outs,
        in_specs=[pl.BlockSpec(memory_space=space)] * n_in, out_specs=[pl.BlockSpec(memory_space=space)] * n_out,
        scratch_shapes=_comm_scratch(plan),
        compiler_params=pltpu.CompilerParams(vmem_limit_bytes=VMEM_LIMIT),
    )(*plan.ins)


def _grid_call(body, ins, carry, *, name, grid, in_specs, out_specs, out_shape, dims, scratch_shapes=(),
               input_output_aliases=None):
    if carry is None:
        res = pl.pallas_call(
            body, name=name, grid=grid, in_specs=list(in_specs), out_specs=list(out_specs),
            out_shape=list(out_shape), scratch_shapes=list(scratch_shapes),
            input_output_aliases=input_output_aliases or {}, compiler_params=_cp(*dims))(*ins)
        return list(res), None

    def at(pos):
        conds = [pl.program_id(k) == p for k, p in enumerate(pos)]
        out = conds[0]
        for cnd in conds[1:]:
            out = jnp.logical_and(out, cnd)
        return out

    return _carried_call(body, carry, lambda: at([0] * len(grid)), lambda: at([g - 1 for g in grid]), ins,
                         name=name, grid=grid, in_specs=in_specs, out_specs=out_specs, out_shape=out_shape,
                         scratch_shapes=scratch_shapes, input_output_aliases=input_output_aliases)


def _carried_call(body, plan, first, last, ins, *, name, grid, in_specs, out_specs, out_shape, scratch_shapes=(),
                  input_output_aliases=None):
    in_specs, out_specs, out_shape = list(in_specs), list(out_specs), list(out_shape)
    n_in, n_out, n_scr = len(in_specs), len(out_specs), len(scratch_shapes)
    c_in, c_out = len(plan.ins), len(plan.outs)
    hbm = pl.BlockSpec(memory_space=pl.ANY)

    def full_body(*refs):
        ins, c_ins = refs[:n_in], refs[n_in:n_in + c_in]
        outs = refs[n_in + c_in:n_in + c_in + n_out]
        c_outs = refs[n_in + c_in + n_out:n_in + c_in + n_out + c_out]
        scr = refs[n_in + c_in + n_out + c_out:]
        sems = scr[n_scr:]

        @pl.when(first())
        def _():
            plan.start(c_ins, c_outs, *sems)

        body(*ins, *outs, *scr[:n_scr])

        @pl.when(last())
        def _():
            plan.finish(c_ins, c_outs, *sems)

    res = pl.pallas_call(
        full_body, name=name, grid=grid,
        in_specs=in_specs + [hbm] * c_in, out_specs=out_specs + [hbm] * c_out,
        out_shape=out_shape + list(plan.outs),
        scratch_shapes=list(scratch_shapes) + _comm_scratch(plan),
        input_output_aliases=input_output_aliases or {},
        compiler_params=_cp(*(["arbitrary"] * len(grid))),
    )(*ins, *plan.ins)
    return list(res[:n_out]), list(res[n_out:])


def _gather_plan(xs):
    n = len(xs)
    ms = [v.shape[0] for v in xs]

    def tools(x_refs, o_refs, send_sems, recv_sems, local_sems):
        x, y, c = _my_place()
        me, sib = (x, y, c), (x, y, 1 - c)
        chips = [(1 - x, y), (x, 1 - y), (1 - x, 1 - y)]

        def rows(a, p):
            return o_refs[a].at[pl.ds(pl.multiple_of(_dev_index(p) * ms[a], 8), ms[a])]

        def copy(a, k, block, to, src=None):
            return pltpu.make_async_remote_copy(
                src_ref=rows(a, block) if src is None else src, dst_ref=rows(a, block),
                send_sem=send_sems.at[a * 7 + k], recv_sem=recv_sems.at[a * 7 + k],
                device_id=to, device_id_type=MESH)

        mine = [pltpu.make_async_copy(x_refs[a], rows(a, me), local_sems.at[a]) for a in range(n)]
        first = []
        for a in range(n):
            first.append(copy(a, 0, me, sib, src=x_refs[a]))
            first += [copy(a, 1 + j, me, (*chip, c), src=x_refs[a]) for j, chip in enumerate(chips)]
        return me, sib, chips, c, copy, mine, first

    def start(*refs):
        _, _, _, _, _, mine, first = tools(*refs)
        for cp in mine + first:
            cp.start()

    def finish(*refs):
        me, sib, chips, c, copy, mine, first = tools(*refs)
        passed = []
        for j, chip in enumerate(chips):
            for a in range(n):
                copy(a, 1 + j, (*chip, c), me).wait_recv()
                cp = copy(a, 4 + j, (*chip, c), sib)
                cp.start()
                passed.append(cp)
        for a in range(n):
            copy(a, 0, sib, me).wait_recv()
            for j, chip in enumerate(chips):
                copy(a, 4 + j, (*chip, 1 - c), me).wait_recv()
        for cp in first + passed:
            cp.wait_send()
        for cp in mine:
            cp.wait()

    outs = [jax.ShapeDtypeStruct((N_DEV * v.shape[0], v.shape[1]), v.dtype) for v in xs]
    return _Comm(list(xs), outs, 7 * n, n, start, finish)


def _exchange_plan(gs, cols=None):
    n = len(gs)
    rs = [v.shape[0] // N_DEV for v in gs]
    flips = [(bx, by, bc) for bx in (0, 1) for by in (0, 1) for bc in (0, 1)][1:]

    def tools(g_refs, o_refs, send_sems, recv_sems, local_sems):
        x, y, c = _my_place()
        me = (x, y, c)

        def block(ref, a, p):
            return ref.at[pl.ds(_dev_index(p) * rs[a], rs[a])]

        def src(a, p):
            rows = pl.ds(_dev_index(p) * rs[a], rs[a])
            return g_refs[a].at[rows] if cols is None else g_refs[a].at[rows, pl.ds(cols[0], cols[1])]

        def peer(f):
            return (1 - x if f[0] else x, 1 - y if f[1] else y, 1 - c if f[2] else c)

        def copy(a, k, to):
            return pltpu.make_async_remote_copy(
                src_ref=src(a, to), dst_ref=block(o_refs[a], a, me),
                send_sem=send_sems.at[a * 7 + k], recv_sem=recv_sems.at[a * 7 + k],
                device_id=to, device_id_type=MESH)

        def arrival(a, k, frm):
            return pltpu.make_async_remote_copy(
                src_ref=src(a, frm), dst_ref=block(o_refs[a], a, frm),
                send_sem=send_sems.at[a * 7 + k], recv_sem=recv_sems.at[a * 7 + k],
                device_id=frm, device_id_type=MESH)

        mine = [pltpu.make_async_copy(src(a, me), block(o_refs[a], a, me), local_sems.at[a])
                for a in range(n)]
        sends = [copy(a, k, peer(f)) for a in range(n) for k, f in enumerate(flips)]
        arrivals = [arrival(a, k, peer(f)) for a in range(n) for k, f in enumerate(flips)]
        return mine, sends, arrivals

    def start(*refs):
        mine, sends, _ = tools(*refs)
        for cp in mine + sends:
            cp.start()

    def finish(*refs):
        mine, sends, arrivals = tools(*refs)
        for cp in arrivals:
            cp.wait_recv()
        for cp in sends:
            cp.wait_send()
        for cp in mine:
            cp.wait()

    outs = [jax.ShapeDtypeStruct((v.shape[0], v.shape[1] if cols is None else cols[1]), v.dtype) for v in gs]
    return _Comm(list(gs), outs, 7 * n, n, start, finish)


def _rope_tables(seq):
    t = jnp.arange(seq)
    row, col = t // GRID_W, t % GRID_W
    half = HEAD_DIM // 2
    inv = ROPE_BASE ** (-jnp.arange(0, half, 2, dtype=F32) / half)
    ang_r = row.astype(F32)[:, None] * inv
    ang_c = col.astype(F32)[:, None] * inv
    ang = jnp.concatenate([ang_r, ang_r, ang_c, ang_c], axis=-1)
    return jnp.tile(jnp.cos(ang), (1, 2)), jnp.tile(jnp.sin(ang), (1, 2))


def _to_slots(w):
    return w.reshape(N_KV_HEADS, GQA_GROUP, HEAD_DIM, w.shape[1]).transpose(1, 0, 2, 3).reshape(w.shape)


def _from_slots(w):
    return w.reshape(GQA_GROUP, N_KV_HEADS, HEAD_DIM, w.shape[1]).transpose(1, 0, 2, 3).reshape(w.shape)


def _pack(vs):
    flat = jnp.concatenate([v.reshape(-1).astype(F32) for v in vs])
    total = -(-flat.shape[0] // (8 * LANES)) * (8 * LANES)
    return jnp.pad(flat, (0, total - flat.shape[0])).reshape(-1, LANES)


def _unpack(packed, like):
    flat, out, off = packed.reshape(-1), [], 0
    for v in like:
        size = math.prod(v.shape)
        out.append(flat[off:off + size].reshape(v.shape))
        off += size
    return out


def _silu(v):
    return v * jax.nn.sigmoid(v)


def kernel(x, c, ctx, c_ctx, w_mod, b_mod, norm_mix_w, w_in, q_norm_w, k_norm_w, sink_logit, conv_w, conv_b, conv_norm_w, conv_norm_b, w_out, norm_ffn_w, w_up, ffn_conv_w, ffn_conv_b, w_down, loss_target, m_c_ctx, m_w_mod, m_b_mod, m_norm_mix_w, m_w_in, m_q_norm_w, m_k_norm_w, m_sink_logit, m_conv_w, m_conv_b, m_conv_norm_w, m_conv_norm_b, m_w_out, m_norm_ffn_w, m_w_up, m_ffn_conv_w, m_ffn_conv_b, m_w_down, v_c_ctx, v_w_mod, v_b_mod, v_norm_mix_w, v_w_in, v_q_norm_w, v_k_norm_w, v_sink_logit, v_conv_w, v_conv_b, v_conv_norm_w, v_conv_norm_b, v_w_out, v_norm_ffn_w, v_w_up, v_ffn_conv_w, v_ffn_conv_b, v_w_down):
    d = D_MODEL
    seq, n_ctx = x.shape[1], ctx.shape[1]
    me = _dev_index(_my_place())
    xs, ctxs, tgt = x[0], ctx[0], loss_target[0]

    small = _pack([c[0], conv_w[0], ffn_conv_w[0]])
    small_all = _all_gather([small], name="gather_small")[0].reshape(N_DEV, -1)
    n_cw, n_fw = conv_w[0].size, ffn_conv_w[0].size
    c_all = small_all[:, :d]
    cw_all = small_all[:, d:d + n_cw].reshape(N_DEV, CONV_K, -1)
    fw_all = small_all[:, d + n_cw:d + n_cw + n_fw].reshape(N_DEV, FFN_K, -1)
    conv_w_f = cw_all.transpose(1, 0, 2).reshape(CONV_K, CONV_CH)
    ffn_w_f = fw_all.transpose(1, 0, 2).reshape(FFN_K, 2 * FFN_H)

    mcols = w_mod.shape[2]
    act = jnp.zeros((16, d), F32).at[:N_DEV].set(_silu(c_all)).at[N_DEV].set(_silu(c_ctx))
    mod_part = _mm(act, w_mod[0], 16, mcols, d, name="mod_fwd")
    mod_all = _all_gather([mod_part], name="gather_mod")[0]
    mod_all = mod_all.reshape(N_DEV, 16, mcols).transpose(1, 0, 2).reshape(16, 6 * d) + b_mod
    mod = lax.dynamic_slice_in_dim(mod_all, me, 1, axis=0)
    sh1, sc1, g1, sh2, sc2, g2 = [mod[:, k * d:(k + 1) * d] for k in range(6)]
    sh1c, sc1c = mod_all[N_DEV:N_DEV + 1, :d], mod_all[N_DEV:N_DEV + 1, d:2 * d]

    cos, sin = _rope_tables(seq)
    ones_c, zeros_c = jnp.ones((n_ctx, LANES), F32), jnp.zeros((n_ctx, LANES), F32)
    qk_w = jnp.concatenate([jnp.tile(q_norm_w, (1, N_Q_HEADS)), jnp.tile(k_norm_w, (1, N_KV_HEADS))], axis=1)
    kc_w = jnp.tile(k_norm_w, (1, N_KV_HEADS))

    h, (w_in_t,) = _prenorm(xs, norm_mix_w, sc1, sh1, carry=_gather_plan([w_in[0].T.astype(MXU_DT)]),
                            name="prenorm_mix")
    hc, _ = _prenorm(ctxs, norm_mix_w, sc1c, sh1c, name="prenorm_ctx")
    w_in_p = jnp.concatenate([_to_slots(w_in_t[:Q_COLS]), w_in_t[Q_COLS:QKV_COLS],
                              jnp.zeros((GLU_OFF - QKV_COLS, d), MXU_DT), w_in_t[QKV_COLS:]])
    proj = _mm(h, w_in_p, seq, IN_PAD, d, tb=True, name="proj_in")
    kv_ctx = _mm(hc, w_in_p, n_ctx, 2 * KV_COLS, d, tb=True, n0=Q_COLS, name="proj_ctx")
    qk_r = _qk_prep(proj, QK_COLS, cos, sin, qk_w, name="qk_prep")
    kc_n = _qk_prep(kv_ctx, KV_COLS, ones_c, zeros_c, kc_w, name="k_ctx_prep")
    (attn_o, lse), (w_up_t, w_out_f) = _attn_fwd(
        sink_logit, qk_r, proj, kc_n, kv_ctx, name="attn_fwd",
        carry=_gather_plan([w_up[0].T.astype(MXU_DT), w_out[0].astype(MXU_DT)]))
    w_out_f = jnp.concatenate([_to_slots(w_out_f[:Q_COLS]), w_out_f[Q_COLS:]])
    (u3, u1), _ = _conv_fwd(proj, GLU_OFF // GLU_COLS, conv_w_f, conv_b, conv_norm_w, conv_norm_b, name="conv_fwd")
    mix, x1, h2 = _out_proj_prenorm(attn_o, u3, w_out_f, xs, g1, norm_ffn_w, sc2, sh2, name="out_proj")

    up0, (w_down_f,) = _mm(h2, w_up_t, seq, 2 * FFN_H, d, tb=True, out_dtype=MXU_DT, name="ffn_up",
                           carry=_gather_plan([w_down[0].astype(MXU_DT)]))
    act_a, gate_s, val_s = _ffn_act(up0, ffn_w_f, ffn_conv_b, name="ffn_act")
    loss_p, dy, dffn, dg2 = _ffn_down_loss(act_a, w_down_f, x1, g2, tgt, name="ffn_down_loss")

    da = _mm(dffn, w_down_f, seq, FFN_H, d, tb=True, out_dtype=MXU_DT, name="ffn_down_dx")
    gw_down = _mm(act_a, dffn, FFN_H, d, seq, ta=True, out_dtype=MXU_DT, name="ffn_down_dw")
    (dgate0, dval0, s_gate, s_val), (rx_down,) = _ffn_act_bwd(
        up0, gate_s, val_s, da, ffn_w_f, carry=_exchange_plan([gw_down]), name="ffn_act_bwd")
    gw_up_t = _mm(dgate0, h2, FFN_H, d, seq, ta=True, out_dtype=MXU_DT, out_rows=2 * FFN_H, name="ffn_up_dw_gate")
    gw_up_t = _mm(dval0, h2, FFN_H, d, seq, ta=True, into=gw_up_t, m0=FFN_H, name="ffn_up_dw_val")
    qd = d // 4
    dh2, (rx_up_a,) = _mm(dgate0, w_up_t, seq, d, FFN_H, a2=dval0, out_dtype=MXU_DT, name="ffn_up_dx",
                          carry=_exchange_plan([gw_up_t], cols=(0, qd)))
    (dx1, dmix, dsh2, dsc2, dnw2, dg1), _ = _norm_bwd(
        dh2, x1, norm_ffn_w, sc2, res=dy, gate=(mix, g1), name="prenorm_ffn_bwd")

    dattn = _mm(dmix, w_out_f, seq, Q_COLS, d, tb=True, out_dtype=MXU_DT, name="out_dx_attn")
    du3 = _mm(dmix, w_out_f, seq, CONV_CH, d, tb=True, n0=Q_COLS, name="out_dx_conv")
    gw_out = _mm(attn_o, dmix, Q_COLS, d, seq, ta=True, out_dtype=MXU_DT, out_rows=Q_COLS + CONV_CH,
                 name="out_dw_attn")
    gw_out = _mm(u3, dmix, CONV_CH, d, seq, ta=True, into=gw_out, m0=Q_COLS, name="out_dw_conv")
    gw_out = jnp.concatenate([_from_slots(gw_out[:Q_COLS]), gw_out[Q_COLS:]])
    (dq, dk, dv, dkc_r, dvc, dsink_rows), (rx_out, rx_up_b) = _attn_bwd(
        sink_logit, qk_r, proj, kc_n, kv_ctx, attn_o, lse, dattn, name="attn_bwd",
        carry=_join_plans(_exchange_plan([gw_out]), _exchange_plan([gw_up_t], cols=(qd, qd))))
    dproj, dqk_w = _qk_prep_bwd(proj, [dq, dk], QK_COLS, cos, sin, qk_w, tail=dv, name="qk_prep_bwd")
    (dproj, dcw8, dvec), (rx_up_c,) = _conv_bwd(
        proj, GLU_OFF // GLU_COLS, u1, du3, dproj, conv_w_f, conv_norm_w, conv_norm_b,
        carry=_exchange_plan([gw_up_t], cols=(2 * qd, 2 * qd)), name="conv_bwd")
    dkc, dkc_w = _qk_prep_bwd(kv_ctx, [dkc_r], KV_COLS, ones_c, zeros_c, kc_w, name="k_ctx_prep_bwd")
    dkv_ctx = jnp.concatenate([dkc, dvc.astype(MXU_DT)], axis=1)
    gw_p = _mm(dproj, h, IN_PAD, d, seq, ta=True, name="proj_dw")
    gw_ctx = _mm(dkv_ctx, hc, 2 * KV_COLS, d, n_ctx, ta=True, name="proj_dw_ctx")
    gw_in_t = jnp.concatenate([_from_slots(gw_p[:Q_COLS]), gw_p[Q_COLS:QKV_COLS] + gw_ctx, gw_p[GLU_OFF:]],
                              axis=0).astype(MXU_DT)
    dh, (rx_in,) = _mm(dproj, w_in_p, seq, d, IN_PAD, out_dtype=MXU_DT, carry=_exchange_plan([gw_in_t]),
                       name="proj_dx")
    dhc = _mm(dkv_ctx, w_in_p, n_ctx, d, 2 * KV_COLS, k0=Q_COLS, name="proj_dx_ctx")
    (grad_x, dsh1, dsc1, dnw1), _ = _norm_bwd(dh, xs, norm_mix_w, sc1, res=dx1, name="prenorm_mix_bwd")
    (dsh1c, dsc1c, dnw1c), _ = _norm_bwd(dhc, ctxs, norm_mix_w, sc1c, want_dx=False, name="prenorm_ctx_bwd")

    dmod = jnp.concatenate([dsh1, dsc1, dg1, dsh2, dsc2, dg2], axis=1)
    dmod_ctx = jnp.concatenate([dsh1c, dsc1c], axis=1)
    d_qn = dqk_w[0, :Q_COLS].reshape(N_Q_HEADS, HEAD_DIM).sum(0)
    d_kn = (dqk_w[0, Q_COLS:].reshape(N_KV_HEADS, HEAD_DIM).sum(0)
            + dkc_w[0].reshape(N_KV_HEADS, HEAD_DIM).sum(0))
    d_ffn_w = jnp.concatenate([s_gate[:FFN_K], s_val[:FFN_K]], axis=1)
    d_ffn_b = jnp.concatenate([s_gate[FFN_K], s_val[FFN_K]])
    d_sink = dsink_rows.reshape(N_Q_HEADS, BLOCK).sum(1)
    summed_like = [(dnw1 + dnw1c), d_qn[None], d_kn[None], d_sink[None], dvec[0:1], dvec[1:2],
                   dvec[2:3], dnw2, d_ffn_b[None], dcw8.sum(1), d_ffn_w, loss_p[0:1, 0:1]]
    pack = _pack([dmod, dmod_ctx] + summed_like)
    pack_all = _all_gather([pack], name="gather_small_grads")[0]
    pack_all = pack_all.reshape(N_DEV, pack.shape[0], LANES)
    tot = _sum_parts(pack_all, name="sum_small_grads")
    (dmod_sum, dmc_sum, g_nmix, g_qn, g_kn, g_sink, g_cb, g_lw, g_lb, g_nffn, g_fb, g_cw_f, g_fw_f,
     loss_sum) = _unpack(tot, [dmod, dmod_ctx] + summed_like)
    loss = loss_sum[0, 0]
    dmod_all = pack_all.reshape(N_DEV, -1)[:, :6 * d]
    g_b_mod = dmod_sum.at[:, :2 * d].add(dmc_sum)

    lo = me * mcols
    dm_rows = jnp.zeros((16, 6 * d), F32).at[:N_DEV].set(dmod_all).at[N_DEV, :2 * d].set(dmc_sum[0])
    dm_mine = lax.dynamic_slice_in_dim(dm_rows, lo, mcols, axis=1)
    parts_mod = _mm(act, dm_mine, d, mcols, 16, ta=True, name="mod_dw")[None]
    dact_part = _mm(dm_mine[N_DEV:N_DEV + 8], w_mod[0], 8, d, mcols, tb=True, name="mod_dx_ctx")
    dact_all = _all_gather([dact_part], name="gather_c_ctx_grad")[0].reshape(N_DEV, 8, d)
    dact = _sum_parts(dact_all, name="sum_c_ctx_grad")[0]
    sg = jax.nn.sigmoid(c_ctx)
    g_c_ctx = dact * (sg * (1.0 + c_ctx * (1.0 - sg)))

    def stacked(rx):
        return rx.reshape(N_DEV, rx.shape[0] // N_DEV, rx.shape[1])

    g_w_in = _sum_parts(stacked(rx_in), name="sum_w_in").T[None]
    g_w_up = jnp.concatenate([_sum_parts(stacked(rx_up_a), name="sum_w_up_a"),
                              _sum_parts(stacked(rx_up_b), name="sum_w_up_b"),
                              _sum_parts(stacked(rx_up_c), name="sum_w_up_c")], axis=1).T[None]
    big = {}
    big["w_in"] = _adam(w_in[0], m_w_in[0], v_w_in[0], g_w_in, name="adam_w_in")
    big["w_up"] = _adam(w_up[0], m_w_up[0], v_w_up[0], g_w_up, name="adam_w_up")
    big["w_out"] = _adam(w_out[0], m_w_out[0], v_w_out[0], stacked(rx_out), name="adam_w_out")
    big["w_down"] = _adam(w_down[0], m_w_down[0], v_w_down[0], stacked(rx_down), name="adam_w_down")
    big["w_mod"] = _adam(w_mod[0], m_w_mod[0], v_w_mod[0], parts_mod, name="adam_w_mod")

    ccols, fcols = conv_w.shape[2], ffn_conv_w.shape[2]
    g_conv_w = lax.dynamic_slice_in_dim(g_cw_f, me * ccols, ccols, axis=1)[None]
    g_ffn_w = lax.dynamic_slice_in_dim(g_fw_f, me * fcols, fcols, axis=1)[None]
    names = ["c_ctx", "b_mod", "norm_mix_w", "q_norm_w", "k_norm_w", "sink_logit", "conv_w", "conv_b",
             "conv_norm_w", "conv_norm_b", "norm_ffn_w", "ffn_conv_w", "ffn_conv_b"]
    ws = [c_ctx, b_mod, norm_mix_w, q_norm_w, k_norm_w, sink_logit, conv_w, conv_b, conv_norm_w, conv_norm_b,
          norm_ffn_w, ffn_conv_w, ffn_conv_b]
    msm = [m_c_ctx, m_b_mod, m_norm_mix_w, m_q_norm_w, m_k_norm_w, m_sink_logit, m_conv_w, m_conv_b,
           m_conv_norm_w, m_conv_norm_b, m_norm_ffn_w, m_ffn_conv_w, m_ffn_conv_b]
    vsm = [v_c_ctx, v_b_mod, v_norm_mix_w, v_q_norm_w, v_k_norm_w, v_sink_logit, v_conv_w, v_conv_b,
           v_conv_norm_w, v_conv_norm_b, v_norm_ffn_w, v_ffn_conv_w, v_ffn_conv_b]
    gsm = [g_c_ctx, g_b_mod, g_nmix, g_qn, g_kn, g_sink, g_conv_w, g_cb, g_lw, g_lb, g_nffn, g_ffn_w, g_fb]
    deltas, new_ms, new_vs = _adam_many(ws, msm, vsm, gsm, name="adam_small")
    sm = {nm: vals for nm, vals in zip(names, zip(gsm, deltas, new_ms, new_vs))}

    def out4(nm):
        if nm in sm:
            return sm[nm]
        return tuple(t[None] for t in big[nm])

    order = ["c_ctx", "w_mod", "b_mod", "norm_mix_w", "w_in", "q_norm_w", "k_norm_w", "sink_logit", "conv_w",
             "conv_b", "conv_norm_w", "conv_norm_b", "w_out", "norm_ffn_w", "w_up", "ffn_conv_w", "ffn_conv_b",
             "w_down"]
    quads = [out4(nm) for nm in order]
    return (loss, grad_x[None], *[q[0] for q in quads], *[q[1] for q in quads],
            *[q[2] for q in quads], *[q[3] for q in quads])
```
